```python
import jax, jax.numpy as jnp
from jax import lax
import numpy as np

D_MODEL = 1024
BATCH = 8
SEQ = 8192
DEPTH = 1

N_META = 16
D_MIX = D_MODEL
D_CONV = D_MIX // 2
CONV_HEADS = 8
CONV_WIDTH = 3
D_RET = D_MIX - D_CONV
RET_HEADS = 4
RET_HEAD_DIM = D_RET // RET_HEADS
CHUNK = 128
ROPE_BASE = 10000.0
EPS = 1e-6
N_PROJ = 8

kernel_name = "hymba_conv_retention_hybrid"


def rms_norm(x, g):
    xf = x.astype(jnp.float32)
    y = xf * lax.rsqrt(jnp.mean(xf * xf, axis=-1, keepdims=True) + EPS)
    return (y * g.astype(jnp.float32)).astype(x.dtype)


def rotary(t, pos):
    half = t.shape[-1] // 2
    freqs = 1.0 / (ROPE_BASE ** (jnp.arange(half, dtype=jnp.float32) / half))
    ang = pos.astype(jnp.float32)[:, None] * freqs[None, :]
    cos = jnp.cos(ang)[None, :, None, :]
    sin = jnp.sin(ang)[None, :, None, :]
    t1, t2 = t[..., :half], t[..., half:]
    return jnp.concatenate([t1 * cos - t2 * sin, t1 * sin + t2 * cos], axis=-1)


def short_conv_branch(h, b_gate, c_gate, conv_w):
    u = c_gate * h
    kern = conv_w.reshape(CONV_WIDTH, 1, D_CONV).astype(u.dtype)
    conv = lax.conv_general_dilated(
        u, kern, window_strides=(1,), padding=[(CONV_WIDTH - 1, 0)],
        dimension_numbers=('NWC', 'WIO', 'NWC'), feature_group_count=D_CONV)
    return b_gate * conv


def retention_chunkwise(q, k, v):
    bsz, L, H, d = q.shape
    pad = CHUNK - N_META
    P = L + pad
    n_chunks = P // CHUNK

    def to_chunks(t):
        t = jnp.pad(t, ((0, 0), (pad, 0), (0, 0), (0, 0)))
        return t.reshape(bsz, n_chunks, CHUNK, H, d).transpose(0, 3, 1, 2, 4)

    qc, kc, vc = to_chunks(q), to_chunks(k), to_chunks(v)
    log_g = jnp.log(1.0 - 2.0 ** (-5.0 - jnp.arange(H, dtype=jnp.float32)))
    idx = jnp.arange(CHUNK, dtype=jnp.float32)
    diff = idx[:, None] - idx[None, :]
    decay = jnp.where(diff[None] >= 0, jnp.exp(diff[None] * log_g[:, None, None]), 0.0)

    scores = jnp.einsum('bhnid,bhnjd->bhnij', qc, kc) * decay[None, :, None]
    inner = jnp.einsum('bhnij,bhnje->bhnie', scores, vc)

    zeta = jnp.exp((CHUNK - 1 - idx)[None, :] * log_g[:, None])
    upd = jnp.einsum('bhnjd,bhnje->nbhde', kc * zeta[None, :, None, :, None], vc)
    chunk_decay = jnp.exp(CHUNK * log_g)[None, :, None, None]

    def step(state, u):
        return chunk_decay * state + u, state

    init = jnp.zeros((bsz, H, d, d), jnp.float32)
    _, states = lax.scan(step, init, upd)
    xi = jnp.exp((idx + 1.0)[None, :] * log_g[:, None])
    cross = jnp.einsum('bhnid,nbhde->bhnie', qc * xi[None, :, None, :, None], states)

    out = (inner + cross).transpose(0, 2, 3, 1, 4).reshape(bsz, P, H, d)
    return out[:, pad:]


def head_group_norm(o, g):
    mu = jnp.mean(o, axis=-1, keepdims=True)
    var = jnp.mean(jnp.square(o - mu), axis=-1, keepdims=True)
    y = (o - mu) * lax.rsqrt(var + EPS)
    bsz, L = o.shape[:2]
    return y.reshape(bsz, L, D_RET) * g.astype(jnp.float32)


def mixer_layer(h, norm_g, w_in, conv_w, ret_norm_g, w_out, pos):
    bsz, L, _ = h.shape
    hn = rms_norm(h, norm_g)
    proj = jnp.einsum('bld,de->ble', hn, w_in)
    cx, cb, cc, cg, q, k, v, rg = jnp.split(proj, N_PROJ, axis=-1)

    conv_out = short_conv_branch(cx, cb, cc, conv_w) * jax.nn.silu(cg)

    shp = (bsz, L, RET_HEADS, RET_HEAD_DIM)
    qf = rotary(q.reshape(shp).astype(jnp.float32), pos) * (RET_HEAD_DIM ** -0.5)
    kf = rotary(k.reshape(shp).astype(jnp.float32), pos)
    vf = v.reshape(shp).astype(jnp.float32)
    ret = head_group_norm(retention_chunkwise(qf, kf, vf), ret_norm_g)
    ret_out = ret.astype(h.dtype) * jax.nn.silu(rg)

    mixed = jnp.concatenate([conv_out, ret_out], axis=-1)
    return h + jnp.einsum('ble,ed->bld', mixed, w_out)


def _fwd_setup_inputs(seed: int = 0) -> dict:
    key = jax.random.key(seed)
    ks = jax.random.split(key, 8)
    x = jax.random.normal(ks[0], (BATCH, SEQ, D_MODEL), jnp.float32)
    meta = jax.random.normal(ks[1], (N_META, D_MODEL), jnp.float32)
    norm1_g = 1.0 + 0.02 * jax.random.normal(ks[2], (D_MODEL,), jnp.float32)
    w_in = jax.random.normal(ks[3], (D_MODEL, N_PROJ * D_CONV), jnp.float32) * D_MODEL ** -0.5
    conv_w = jax.random.normal(ks[4], (CONV_WIDTH, D_CONV), jnp.float32) * CONV_WIDTH ** -0.5
    ret_norm_g = 1.0 + 0.02 * jax.random.normal(ks[5], (D_RET,), jnp.float32)
    w_out = jax.random.normal(ks[6], (D_MIX, D_MODEL), jnp.float32) * D_MIX ** -0.5
    final_g = 1.0 + 0.02 * jax.random.normal(ks[7], (D_MODEL,), jnp.float32)
    return {"x": x, "meta": meta, "norm1_g": norm1_g, "w_in": w_in, "conv_w": conv_w,
            "ret_norm_g": ret_norm_g, "w_out": w_out, "final_g": final_g}


def _fwd_reference(x, meta, norm1_g, w_in, conv_w, ret_norm_g, w_out, final_g):
    bsz = x.shape[0]
    meta_b = jnp.broadcast_to(meta.astype(x.dtype)[None], (bsz, N_META, D_MODEL))
    h = jnp.concatenate([meta_b, x], axis=1)
    pos = jnp.arange(h.shape[1], dtype=jnp.int32)
    for _ in range(DEPTH):
        h = mixer_layer(h, norm1_g, w_in, conv_w, ret_norm_g, w_out, pos)
    h = rms_norm(h, final_g)
    return h[:, N_META:]


import jax as _jax
import jax.numpy as _jnp

TWIN_FORMAT = 'train_step'
FWD_PARAMS = ['x', 'meta', 'norm1_g', 'w_in', 'conv_w', 'ret_norm_g', 'w_out', 'final_g']
TWIN_WEIGHTS = ['meta', 'norm1_g', 'w_in', 'conv_w', 'ret_norm_g', 'w_out', 'final_g']
TWIN_DIFF_INPUT = 'x'
TWIN_INPUTS = ['x', 'meta', 'norm1_g', 'w_in', 'conv_w', 'ret_norm_g', 'w_out', 'final_g', 'loss_target', 'm_meta', 'm_norm1_g', 'm_w_in', 'm_conv_w', 'm_ret_norm_g', 'm_w_out', 'm_final_g', 'v_meta', 'v_norm1_g', 'v_w_in', 'v_conv_w', 'v_ret_norm_g', 'v_w_out', 'v_final_g']
TWIN_OUTPUTS = ['loss', 'grad_x', 'grad_meta', 'grad_norm1_g', 'grad_w_in', 'grad_conv_w', 'grad_ret_norm_g', 'grad_w_out', 'grad_final_g', 'delta_meta', 'delta_norm1_g', 'delta_w_in', 'delta_conv_w', 'delta_ret_norm_g', 'delta_w_out', 'delta_final_g', 'new_m_meta', 'new_m_norm1_g', 'new_m_w_in', 'new_m_conv_w', 'new_m_ret_norm_g', 'new_m_w_out', 'new_m_final_g', 'new_v_meta', 'new_v_norm1_g', 'new_v_w_in', 'new_v_conv_w', 'new_v_ret_norm_g', 'new_v_w_out', 'new_v_final_g']
TWIN_LEAF_KINDS = {'loss': 'loss', 'grad_x': 'grad_x', 'grad_meta': 'grad_w', 'grad_norm1_g': 'grad_w', 'grad_w_in': 'grad_w', 'grad_conv_w': 'grad_w', 'grad_ret_norm_g': 'grad_w', 'grad_w_out': 'grad_w', 'grad_final_g': 'grad_w', 'delta_meta': 'delta_w', 'delta_norm1_g': 'delta_w', 'delta_w_in': 'delta_w', 'delta_conv_w': 'delta_w', 'delta_ret_norm_g': 'delta_w', 'delta_w_out': 'delta_w', 'delta_final_g': 'delta_w', 'new_m_meta': 'new_m', 'new_m_norm1_g': 'new_m', 'new_m_w_in': 'new_m', 'new_m_conv_w': 'new_m', 'new_m_ret_norm_g': 'new_m', 'new_m_w_out': 'new_m', 'new_m_final_g': 'new_m', 'new_v_meta': 'new_v', 'new_v_norm1_g': 'new_v', 'new_v_w_in': 'new_v', 'new_v_conv_w': 'new_v', 'new_v_ret_norm_g': 'new_v', 'new_v_w_out': 'new_v', 'new_v_final_g': 'new_v'}


def _forward(args):
    return _fwd_reference(*[args[k] for k in FWD_PARAMS])


def _output_shape():
    def fwd():
        inp = _fwd_setup_inputs(0)
        return _fwd_reference(*[inp[k] for k in FWD_PARAMS])
    out = _jax.eval_shape(fwd)
    return out.shape, out.dtype

N_MICROBATCH = 1
ADAM_LR = 0.001
ADAM_B1 = 0.9
ADAM_B2 = 0.999
ADAM_EPS = 1e-08
ADAM_WD = 0.01
ADAM_STEP = 10
PER_EXAMPLE_BATCH_AXIS = {'x': 0, 'loss_target': 0}
SHARED_INPUTS = []
_WEIGHT_DTYPES = {'meta': _jnp.float32, 'norm1_g': _jnp.float32, 'w_in': _jnp.float32, 'conv_w': _jnp.float32, 'ret_norm_g': _jnp.float32, 'w_out': _jnp.float32, 'final_g': _jnp.float32}
MOMENT_SCALE = {'meta': 9.664916e-03, 'norm1_g': 2.648856e-01, 'w_in': 1.285008e-01, 'conv_w': 1.306913e-01, 'ret_norm_g': 1.292127e-01, 'w_out': 1.272307e-01, 'final_g': 6.393932e+01}


def _to_microbatches(a, axis):
    t = _jnp.moveaxis(a, axis, 0)
    t = t.reshape((N_MICROBATCH, t.shape[0] // N_MICROBATCH) + t.shape[1:])
    return _jnp.moveaxis(t, 1, axis + 1)


def setup_inputs(seed: int = 0) -> dict:
    inp = _fwd_setup_inputs(seed)
    key = _jax.random.fold_in(_jax.random.key(seed), 7919)
    shape, _ = _output_shape()
    out = dict(inp)
    out["loss_target"] = _jax.random.normal(_jax.random.fold_in(key, 0), shape, _jnp.float32)
    for i, name in enumerate(TWIN_WEIGHTS):
        w = inp[name].astype(_jnp.float32)
        if MOMENT_SCALE is None:
            s = _jnp.sqrt(_jnp.mean(_jnp.square(w)) + 1e-30)
        else:
            s = MOMENT_SCALE[name]
        km, kv = _jax.random.split(_jax.random.fold_in(key, i + 1))
        out[name] = w
        out["m_" + name] = s * _jax.random.normal(km, w.shape, _jnp.float32)
        out["v_" + name] = (s * s) * _jax.random.uniform(kv, w.shape, _jnp.float32, 0.5, 1.5)
    if N_MICROBATCH > 1:
        for name, axis in PER_EXAMPLE_BATCH_AXIS.items():
            out[name] = _to_microbatches(out[name], axis)
    return {'x': out['x'], 'meta': out['meta'], 'norm1_g': out['norm1_g'], 'w_in': out['w_in'], 'conv_w': out['conv_w'], 'ret_norm_g': out['ret_norm_g'], 'w_out': out['w_out'], 'final_g': out['final_g'], 'loss_target': out['loss_target'], 'm_meta': out['m_meta'], 'm_norm1_g': out['m_norm1_g'], 'm_w_in': out['m_w_in'], 'm_conv_w': out['m_conv_w'], 'm_ret_norm_g': out['m_ret_norm_g'], 'm_w_out': out['m_w_out'], 'm_final_g': out['m_final_g'], 'v_meta': out['v_meta'], 'v_norm1_g': out['v_norm1_g'], 'v_w_in': out['v_w_in'], 'v_conv_w': out['v_conv_w'], 'v_ret_norm_g': out['v_ret_norm_g'], 'v_w_out': out['v_w_out'], 'v_final_g': out['v_final_g']}


def _loss(weights, diff, rest, loss_target):
    with _jax.named_scope("forward"):
        args = {**rest, TWIN_DIFF_INPUT: diff, **{k: w.astype(_WEIGHT_DTYPES[k]) for k, w in weights.items()}}
        y = _forward(args)
    with _jax.named_scope("loss_head"):
        err = _jnp.square(y.astype(_jnp.float32) - loss_target)
        return 0.5 * _jnp.sum(_jnp.mean(err, axis=-1)) if err.ndim else 0.5 * err


def _adamw(w, g, m, v):
    m = ADAM_B1 * m + (1.0 - ADAM_B1) * g
    v = ADAM_B2 * v + (1.0 - ADAM_B2) * _jnp.square(g)
    m_hat = m / (1.0 - ADAM_B1 ** ADAM_STEP)
    v_hat = v / (1.0 - ADAM_B2 ** ADAM_STEP)
    delta = -ADAM_LR * (m_hat / (_jnp.sqrt(v_hat) + ADAM_EPS) + ADAM_WD * w)
    return delta, m, v


def reference(x, meta, norm1_g, w_in, conv_w, ret_norm_g, w_out, final_g, loss_target, m_meta, m_norm1_g, m_w_in, m_conv_w, m_ret_norm_g, m_w_out, m_final_g, v_meta, v_norm1_g, v_w_in, v_conv_w, v_ret_norm_g, v_w_out, v_final_g):
    given = dict(x=x, meta=meta, norm1_g=norm1_g, w_in=w_in, conv_w=conv_w, ret_norm_g=ret_norm_g, w_out=w_out, final_g=final_g, loss_target=loss_target, m_meta=m_meta, m_norm1_g=m_norm1_g, m_w_in=m_w_in, m_conv_w=m_conv_w, m_ret_norm_g=m_ret_norm_g, m_w_out=m_w_out, m_final_g=m_final_g, v_meta=v_meta, v_norm1_g=v_norm1_g, v_w_in=v_w_in, v_conv_w=v_conv_w, v_ret_norm_g=v_ret_norm_g, v_w_out=v_w_out, v_final_g=v_final_g)
    weights = {n: given[n] for n in TWIN_WEIGHTS}
    shared = {n: given[n] for n in SHARED_INPUTS}
    per_example = {n: given[n] for n in ['x']}
    grad_fn = _jax.value_and_grad(_loss, argnums=(0, 1))

    def one_microbatch(ex, loss_target):
        ex = dict(ex)
        diff = ex.pop(TWIN_DIFF_INPUT)
        return grad_fn(weights, diff, {**shared, **ex}, loss_target)

    if N_MICROBATCH == 1:
        loss, (grad_w, grad_x) = one_microbatch(per_example, given["loss_target"])
    else:
        def body(carry, xs):
            loss_sum, grad_sum = carry
            l_k, (gw_k, gx_k) = one_microbatch(xs[0], xs[1])
            with _jax.named_scope("update"):
                return (loss_sum + l_k, _jax.tree.map(_jnp.add, grad_sum, gw_k)), gx_k

        init = (_jnp.zeros((), _jnp.float32), _jax.tree.map(_jnp.zeros_like, weights))
        (loss, grad_w), grad_x = _jax.lax.scan(body, init, (per_example, given["loss_target"]))
    with _jax.named_scope("update"):
        delta_w, new_m, new_v = {}, {}, {}
        for n in TWIN_WEIGHTS:
            delta_w[n], new_m[n], new_v[n] = _adamw(weights[n], grad_w[n], given["m_" + n], given["v_" + n])
    return (loss, grad_x, *[grad_w[n] for n in TWIN_WEIGHTS], *[delta_w[n] for n in TWIN_WEIGHTS],
            *[new_m[n] for n in TWIN_WEIGHTS], *[new_v[n] for n in TWIN_WEIGHTS])
```

```python
import functools
import math

import jax
import jax.numpy as jnp
import numpy as np
from jax import lax
from jax.experimental import pallas as pl
from jax.experimental.pallas import tpu as pltpu

F32 = jnp.float32
BF16 = jnp.bfloat16

N_DEV = 8
D_MODEL = 1024
N_META = 16
CHUNK = 128
D_CONV = 512
D_RET = 512
N_HEADS = 4
HEAD_DIM = 128
N_PROJ = 8
BLK = 512
ROPE_BASE = 10000.0
EPS = 1e-6
PAD_ROWS = CHUNK - N_META
Q_SCALE = HEAD_DIM ** -0.5
LOG_G = tuple(math.log(1.0 - 2.0 ** (-5.0 - h)) for h in range(N_HEADS))
CHUNK_DECAY = tuple(math.exp(CHUNK * lg) for lg in LOG_G)

ADAM_LR = 0.001
ADAM_B1 = 0.9
ADAM_B2 = 0.999
ADAM_EPS = 1e-08
ADAM_WD = 0.01
ADAM_STEP = 10

ROW_TILE = 640
VMEM_LIMIT = 56 * 1024 * 1024

SC_META0, SC_CONV0 = 0, 16
GA_N1, GA_RG, GA_FG, GA_LOSS = 0, 8, 12, 20
PACK_ROWS = 24

NT = (((1,), (1,)), ((), ()))
TN = (((0,), (0,)), ((), ()))
MESH = pl.DeviceIdType.MESH

_VMEM = pl.BlockSpec(memory_space=pltpu.VMEM)
_HBM = pl.BlockSpec(memory_space=pltpu.HBM)


def _params(n_axes=1):
    return pltpu.CompilerParams(dimension_semantics=("arbitrary",) * n_axes, vmem_limit_bytes=VMEM_LIMIT)


def _sigmoid(x):
    return 1.0 / (1.0 + jnp.exp(-x))


def _decay_tables():
    idx = np.arange(CHUNK, dtype=np.float64)
    diff = idx[:, None] - idx[None, :]
    dec = np.stack([np.where(diff >= 0, np.exp(diff * lg), 0.0) for lg in LOG_G])
    zeta = np.stack([np.exp((CHUNK - 1 - idx) * lg) for lg in LOG_G])
    xi = np.stack([np.exp((idx + 1.0) * lg) for lg in LOG_G])
    ones = np.ones((1, 1, CHUNK))
    return (
        jnp.asarray(dec, F32),
        jnp.asarray(dec.transpose(0, 2, 1), F32),
        jnp.asarray(zeta[:, :, None] * ones, F32),
        jnp.asarray(xi[:, :, None] * ones, F32),
    )


def _rotary_tables(n_rows):
    half = HEAD_DIM // 2
    freqs = 1.0 / (ROPE_BASE ** (jnp.arange(half, dtype=F32) / half))
    pos = (jnp.arange(n_rows, dtype=jnp.int32) - PAD_ROWS).astype(F32)
    ang = pos[:, None] * freqs[None, :]
    cos, sin = jnp.cos(ang), jnp.sin(ang)
    return jnp.concatenate([cos, cos], axis=1), jnp.concatenate([-sin, sin], axis=1)


def _rot(t, cos2, sin2):
    return t * cos2 + pltpu.roll(t, HEAD_DIM // 2, 1) * sin2


def _rot_bwd(d, cos2, sin2):
    return d * cos2 + pltpu.roll(d * sin2, HEAD_DIM // 2, 1)


def _exchange(name, srcs, by_dest):
    n = len(srcs)
    blocks = [s.shape[1:] if d else s.shape for s, d in zip(srcs, by_dest, strict=True)]

    def body(*refs):
        src_refs, out_refs = refs[:n], refs[n : 2 * n]
        send_sems, recv_sems, local_sems = refs[2 * n :]
        x, y, c = lax.axis_index("x"), lax.axis_index("y"), lax.axis_index("c")
        me = 4 * x + 2 * y + c
        local = []
        for a in range(n):
            own = src_refs[a].at[me] if by_dest[a] else src_refs[a]
            cp = pltpu.make_async_copy(own, out_refs[a].at[me], local_sems.at[a])
            cp.start()
            local.append(cp)
        remote = []
        for k in range(1, N_DEV):
            px = 1 - x if k & 4 else x
            py = 1 - y if k & 2 else y
            pc = 1 - c if k & 1 else c
            peer = 4 * px + 2 * py + pc
            for a in range(n):
                cp = pltpu.make_async_remote_copy(
                    src_ref=src_refs[a].at[peer] if by_dest[a] else src_refs[a],
                    dst_ref=out_refs[a].at[me],
                    send_sem=send_sems.at[a * (N_DEV - 1) + k - 1],
                    recv_sem=recv_sems.at[a * (N_DEV - 1) + k - 1],
                    device_id=(px, py, pc),
                    device_id_type=MESH,
                )
                cp.start()
                remote.append(cp)
        for cp in remote:
            cp.wait()
        for cp in local:
            cp.wait()

    return pl.pallas_call(
        body,
        name=name,
        out_shape=tuple(jax.ShapeDtypeStruct((N_DEV, *b), s.dtype) for b, s in zip(blocks, srcs, strict=True)),
        in_specs=[_HBM] * n,
        out_specs=tuple([_HBM] * n),
        scratch_shapes=[
            pltpu.SemaphoreType.DMA((n * (N_DEV - 1),)),
            pltpu.SemaphoreType.DMA((n * (N_DEV - 1),)),
            pltpu.SemaphoreType.DMA((n,)),
        ],
    )(*srcs)


def _f1(hpad, g1, w3, cw8, cos2, sin2):
    n_rows = hpad.shape[0]
    tm = ROW_TILE
    nt = n_rows // tm

    def body(h_ref, g_ref, w_ref, cw_ref, cos_ref, sin_ref, hnT_ref, pr_ref, co_ref, coT_ref, cv_ref, halo):
        i = pl.program_id(0)

        @pl.when(i == 0)
        def _():
            halo[...] = jnp.zeros_like(halo)

        h = h_ref[...]
        r = lax.rsqrt(jnp.mean(h * h, axis=-1, keepdims=True) + EPS)
        hn = (h * r * g_ref[...]).astype(BF16)
        hnT_ref[...] = hn.T

        def proj(j):
            return jnp.dot(hn, w_ref[j], preferred_element_type=F32)

        cx, cb, cc, cg = proj(0), proj(1), proj(2), proj(3)
        u = cc * cx
        rows = lax.broadcasted_iota(jnp.int32, u.shape, 0)
        hl = halo[...]
        u1 = jnp.where(rows == 0, hl[7:8], pltpu.roll(u, 1, 0))
        u2 = jnp.where(rows == 0, hl[6:7], jnp.where(rows == 1, hl[7:8], pltpu.roll(u, 2, 0)))
        halo[...] = u[tm - 8 : tm]
        cw = cw_ref[...]
        conv = cw[0:1] * u2 + cw[1:2] * u1 + cw[2:3] * u
        co = (cb * conv * (cg * _sigmoid(cg))).astype(BF16)
        co_ref[...] = co
        coT_ref[...] = co.T
        cv_ref[...] = conv.astype(BF16)
        pr_ref[:, 0 * BLK : 1 * BLK] = cx.astype(BF16)
        pr_ref[:, 1 * BLK : 2 * BLK] = cb.astype(BF16)
        pr_ref[:, 2 * BLK : 3 * BLK] = cc.astype(BF16)
        pr_ref[:, 3 * BLK : 4 * BLK] = cg.astype(BF16)
        cos_t, sin_t = cos_ref[...], sin_ref[...]
        q, k = proj(4), proj(5)
        for hd in range(N_HEADS):
            c0 = hd * HEAD_DIM
            pr_ref[:, 4 * BLK + c0 : 4 * BLK + c0 + HEAD_DIM] = (
                _rot(q[:, c0 : c0 + HEAD_DIM], cos_t, sin_t) * Q_SCALE
            ).astype(BF16)
            pr_ref[:, 5 * BLK + c0 : 5 * BLK + c0 + HEAD_DIM] = _rot(k[:, c0 : c0 + HEAD_DIM], cos_t, sin_t).astype(BF16)
        pr_ref[:, 6 * BLK : 7 * BLK] = proj(6).astype(BF16)
        pr_ref[:, 7 * BLK : 8 * BLK] = proj(7).astype(BF16)

    row = lambda w: pl.BlockSpec((tm, w), lambda i: (i, 0))
    col = lambda w: pl.BlockSpec((w, tm), lambda i: (0, i))
    return pl.pallas_call(
        body,
        name="f1_inproj_conv",
        grid=(nt,),
        in_specs=[row(D_MODEL), _VMEM, _VMEM, _VMEM, row(HEAD_DIM), row(HEAD_DIM)],
        out_specs=(col(D_MODEL), row(N_PROJ * BLK), row(D_CONV), col(D_CONV), row(D_CONV)),
        out_shape=(
            jax.ShapeDtypeStruct((D_MODEL, n_rows), BF16),
            jax.ShapeDtypeStruct((n_rows, N_PROJ * BLK), BF16),
            jax.ShapeDtypeStruct((n_rows, D_CONV), BF16),
            jax.ShapeDtypeStruct((D_CONV, n_rows), BF16),
            jax.ShapeDtypeStruct((n_rows, D_CONV), BF16),
        ),
        scratch_shapes=[pltpu.VMEM((8, D_CONV), F32)],
        compiler_params=_params(),
    )(hpad, g1, w3, cw8, cos2, sin2)


def _group_norm(o):
    ys, rs = [], []
    for hd in range(N_HEADS):
        oh = o[:, hd * HEAD_DIM : (hd + 1) * HEAD_DIM]
        xc = oh - jnp.mean(oh, axis=-1, keepdims=True)
        rstd = lax.rsqrt(jnp.mean(xc * xc, axis=-1, keepdims=True) + EPS)
        ys.append(xc * rstd)
        rs.append(jnp.broadcast_to(rstd, oh.shape))
    return jnp.concatenate(ys, axis=1), jnp.concatenate(rs, axis=1)


def _f2(proj, conv_out, hpad, tpad, wout, gr, gf, dec, zeta, xi):
    n_rows = hpad.shape[0]
    tm = ROW_TILE
    nt = n_rows // tm
    nct = tm // CHUNK

    def body(q_ref, k_ref, v_ref, rg_ref, co_ref, h_ref, t_ref, wo_ref, gr_ref, gf_ref, dec_ref, zeta_ref, xi_ref,
             dh2_ref, o_ref, ro_ref, roT_ref, st_ref, acc_ref, state, obuf):
        i = pl.program_id(0)

        @pl.when(i == 0)
        def _():
            state[...] = jnp.zeros_like(state)
            acc_ref[...] = jnp.zeros_like(acc_ref)

        for c in range(nct):
            rs = slice(c * CHUNK, (c + 1) * CHUNK)
            for hd in range(N_HEADS):
                cs = slice(hd * HEAD_DIM, (hd + 1) * HEAD_DIM)
                q, k, v = q_ref[rs, cs], k_ref[rs, cs], v_ref[rs, cs]
                st = state[hd]
                st_bf = st.astype(BF16)
                st_ref[c, hd] = st_bf
                s = lax.dot_general(q, k, NT, preferred_element_type=F32) * dec_ref[hd]
                inner = jnp.dot(s.astype(BF16), v, preferred_element_type=F32)
                qx = (q.astype(F32) * xi_ref[hd]).astype(BF16)
                obuf[rs, cs] = inner + jnp.dot(qx, st_bf, preferred_element_type=F32)
                kz = (k.astype(F32) * zeta_ref[hd]).astype(BF16)
                state[hd] = CHUNK_DECAY[hd] * st + lax.dot_general(kz, v, TN, preferred_element_type=F32)

        o = obuf[...]
        o_ref[...] = o.astype(BF16)
        yh, _ = _group_norm(o)
        rg = rg_ref[...].astype(F32)
        ro = (yh * gr_ref[...] * (rg * _sigmoid(rg))).astype(BF16)
        ro_ref[...] = ro
        roT_ref[...] = ro.T
        h2 = (
            h_ref[...]
            + jnp.dot(co_ref[...], wo_ref[0:D_CONV], preferred_element_type=F32)
            + jnp.dot(ro, wo_ref[D_CONV:], preferred_element_type=F32)
        )
        r2 = lax.rsqrt(jnp.mean(h2 * h2, axis=-1, keepdims=True) + EPS)
        yn = h2 * r2
        gfv = gf_ref[...]
        rows = i * tm + lax.broadcasted_iota(jnp.int32, (tm, 1), 0)
        valid = rows >= CHUNK
        err = jnp.where(valid, yn * gfv - t_ref[...], 0.0)
        tile_loss = jnp.sum(jnp.sum(err * err, axis=-1, keepdims=True), axis=0, keepdims=True) * (0.5 / D_MODEL)
        dy = err * (1.0 / D_MODEL)
        acc_ref[0:1, :] += jnp.sum(dy * yn, axis=0, keepdims=True)
        acc_ref[1:2, :] += tile_loss
        dyn = dy * gfv
        dh2_ref[...] = r2 * (dyn - yn * jnp.mean(dyn * yn, axis=-1, keepdims=True))

    row = lambda w, j=0: pl.BlockSpec((tm, w), lambda i: (i, j))
    return pl.pallas_call(
        body,
        name="f2_retention_out",
        grid=(nt,),
        in_specs=[row(BLK, 4), row(BLK, 5), row(BLK, 6), row(BLK, 7), row(D_CONV), row(D_MODEL), row(D_MODEL)]
        + [_VMEM] * 6,
        out_specs=(
            row(D_MODEL),
            row(D_RET),
            row(D_RET),
            pl.BlockSpec((D_RET, tm), lambda i: (0, i)),
            pl.BlockSpec((nct, N_HEADS, HEAD_DIM, HEAD_DIM), lambda i: (i, 0, 0, 0)),
            pl.BlockSpec((8, D_MODEL), lambda i: (0, 0)),
        ),
        out_shape=(
            jax.ShapeDtypeStruct((n_rows, D_MODEL), F32),
            jax.ShapeDtypeStruct((n_rows, D_RET), BF16),
            jax.ShapeDtypeStruct((n_rows, D_RET), BF16),
            jax.ShapeDtypeStruct((D_RET, n_rows), BF16),
            jax.ShapeDtypeStruct((n_rows // CHUNK, N_HEADS, HEAD_DIM, HEAD_DIM), BF16),
            jax.ShapeDtypeStruct((8, D_MODEL), F32),
        ),
        scratch_shapes=[pltpu.VMEM((N_HEADS, HEAD_DIM, HEAD_DIM), F32), pltpu.VMEM((tm, D_RET), F32)],
        compiler_params=_params(),
    )(proj, proj, proj, proj, conv_out, hpad, tpad, wout, gr, gf, dec, zeta, xi)


def _b1(dh2, proj, conv, o, states, wout, gr, cw8, dec, dect, zeta, xi, cos2, sin2):
    n_rows = dh2.shape[0]
    tm = ROW_TILE
    nt = n_rows // tm
    nct = tm // CHUNK

    def body(dh2_ref, pr_ref, cv_ref, o_ref, st_ref, wo_ref, gr_ref, cw_ref, dec_ref, dect_ref, zeta_ref, xi_ref,
             cos_ref, sin_ref, dp_ref, acc_ref, dstate, halo, dobuf):
        i = pl.program_id(0)

        @pl.when(i == 0)
        def _():
            dstate[...] = jnp.zeros_like(dstate)
            halo[...] = jnp.zeros_like(halo)
            acc_ref[...] = jnp.zeros_like(acc_ref)

        dmix = lax.dot_general(dh2_ref[...].astype(BF16), wo_ref[...], NT, preferred_element_type=F32)

        dco = dmix[:, :D_CONV]
        cx = pr_ref[:, 0 * BLK : 1 * BLK].astype(F32)
        cb = pr_ref[:, 1 * BLK : 2 * BLK].astype(F32)
        cc = pr_ref[:, 2 * BLK : 3 * BLK].astype(F32)
        cg = pr_ref[:, 3 * BLK : 4 * BLK].astype(F32)
        conv = cv_ref[...].astype(F32)
        sg = _sigmoid(cg)
        silu, dsilu = cg * sg, sg * (1.0 + cg * (1.0 - sg))
        t = dco * cb
        dp_ref[:, 1 * BLK : 2 * BLK] = (dco * conv * silu).astype(BF16)
        dp_ref[:, 3 * BLK : 4 * BLK] = (t * conv * dsilu).astype(BF16)
        dconv = t * silu
        rows = lax.broadcasted_iota(jnp.int32, dconv.shape, 0)
        hl = halo[...]
        dc1 = jnp.where(rows == tm - 1, hl[0:1], pltpu.roll(dconv, tm - 1, 0))
        dc2 = jnp.where(rows == tm - 2, hl[0:1], jnp.where(rows == tm - 1, hl[1:2], pltpu.roll(dconv, tm - 2, 0)))
        halo[...] = dconv[0:8]
        cw = cw_ref[...]
        du = cw[2:3] * dconv + cw[1:2] * dc1 + cw[0:1] * dc2
        u = cc * cx
        acc_ref[0:1, :] += jnp.sum(u * dc2, axis=0, keepdims=True)
        acc_ref[1:2, :] += jnp.sum(u * dc1, axis=0, keepdims=True)
        acc_ref[2:3, :] += jnp.sum(u * dconv, axis=0, keepdims=True)
        dp_ref[:, 0 * BLK : 1 * BLK] = (du * cc).astype(BF16)
        dp_ref[:, 2 * BLK : 3 * BLK] = (du * cx).astype(BF16)

        dro = dmix[:, D_CONV:]
        rg = pr_ref[:, 7 * BLK : 8 * BLK].astype(F32)
        sg = _sigmoid(rg)
        silu, dsilu = rg * sg, sg * (1.0 + rg * (1.0 - sg))
        yh, rstd = _group_norm(o_ref[...].astype(F32))
        grv = gr_ref[...]
        dp_ref[:, 7 * BLK : 8 * BLK] = (dro * (yh * grv) * dsilu).astype(BF16)
        dret = dro * silu
        acc_ref[3:4, :] += jnp.sum(dret * yh, axis=0, keepdims=True)
        dyh = dret * grv
        for hd in range(N_HEADS):
            cs = slice(hd * HEAD_DIM, (hd + 1) * HEAD_DIM)
            a, b = dyh[:, cs], yh[:, cs]
            dobuf[:, cs] = rstd[:, cs] * (
                a - jnp.mean(a, axis=-1, keepdims=True) - b * jnp.mean(a * b, axis=-1, keepdims=True)
            )

        for c in reversed(range(nct)):
            rs = slice(c * CHUNK, (c + 1) * CHUNK)
            cos_t, sin_t = cos_ref[rs, :], sin_ref[rs, :]
            for hd in range(N_HEADS):
                cs = slice(hd * HEAD_DIM, (hd + 1) * HEAD_DIM)
                q = pr_ref[rs, 4 * BLK + hd * HEAD_DIM : 4 * BLK + (hd + 1) * HEAD_DIM]
                k = pr_ref[rs, 5 * BLK + hd * HEAD_DIM : 5 * BLK + (hd + 1) * HEAD_DIM]
                v = pr_ref[rs, 6 * BLK + hd * HEAD_DIM : 6 * BLK + (hd + 1) * HEAD_DIM]
                do = dobuf[rs, cs].astype(BF16)
                st_bf = st_ref[c, hd]
                dst = dstate[hd]
                dst_bf = dst.astype(BF16)
                zt, xt = zeta_ref[hd], xi_ref[hd]
                sT = (lax.dot_general(k, q, NT, preferred_element_type=F32) * dect_ref[hd]).astype(BF16)
                dsT = (lax.dot_general(v, do, NT, preferred_element_type=F32) * dect_ref[hd]).astype(BF16)
                ds = (lax.dot_general(do, v, NT, preferred_element_type=F32) * dec_ref[hd]).astype(BF16)
                kz = (k.astype(F32) * zt).astype(BF16)
                dv = jnp.dot(sT, do, preferred_element_type=F32) + jnp.dot(kz, dst_bf, preferred_element_type=F32)
                dq = jnp.dot(ds, k, preferred_element_type=F32) + xt * lax.dot_general(
                    do, st_bf, NT, preferred_element_type=F32
                )
                dk = jnp.dot(dsT, q, preferred_element_type=F32) + zt * lax.dot_general(
                    v, dst_bf, NT, preferred_element_type=F32
                )
                qx = (q.astype(F32) * xt).astype(BF16)
                dstate[hd] = CHUNK_DECAY[hd] * dst + lax.dot_general(qx, do, TN, preferred_element_type=F32)
                dp_ref[rs, 4 * BLK + hd * HEAD_DIM : 4 * BLK + (hd + 1) * HEAD_DIM] = _rot_bwd(
                    dq * Q_SCALE, cos_t, sin_t
                ).astype(BF16)
                dp_ref[rs, 5 * BLK + hd * HEAD_DIM : 5 * BLK + (hd + 1) * HEAD_DIM] = _rot_bwd(dk, cos_t, sin_t).astype(BF16)
                dp_ref[rs, 6 * BLK + hd * HEAD_DIM : 6 * BLK + (hd + 1) * HEAD_DIM] = dv.astype(BF16)

    rev = lambda w: pl.BlockSpec((tm, w), lambda i: (nt - 1 - i, 0))
    return pl.pallas_call(
        body,
        name="b1_dproj",
        grid=(nt,),
        in_specs=[
            rev(D_MODEL),
            rev(N_PROJ * BLK),
            rev(D_CONV),
            rev(D_RET),
            pl.BlockSpec((nct, N_HEADS, HEAD_DIM, HEAD_DIM), lambda i: (nt - 1 - i, 0, 0, 0)),
        ]
        + [_VMEM] * 7
        + [rev(HEAD_DIM), rev(HEAD_DIM)],
        out_specs=(rev(N_PROJ * BLK), pl.BlockSpec((8, D_CONV), lambda i: (0, 0))),
        out_shape=(jax.ShapeDtypeStruct((n_rows, N_PROJ * BLK), BF16), jax.ShapeDtypeStruct((8, D_CONV), F32)),
        scratch_shapes=[
            pltpu.VMEM((N_HEADS, HEAD_DIM, HEAD_DIM), F32),
            pltpu.VMEM((8, D_CONV), F32),
            pltpu.VMEM((tm, D_RET), F32),
        ],
        compiler_params=_params(),
    )(dh2, proj, conv, o, states, wout, gr, cw8, dec, dect, zeta, xi, cos2, sin2)


def _b2(dproj, w3, hpad, dh2, g1):
    n_rows = hpad.shape[0]
    tm = ROW_TILE
    nt = n_rows // tm

    def body(dp_ref, w_ref, h_ref, dh2_ref, g_ref, dh_ref, acc_ref):
        @pl.when(pl.program_id(0) == 0)
        def _():
            acc_ref[...] = jnp.zeros_like(acc_ref)

        dhn = lax.dot_general(dp_ref[:, 0:BLK], w_ref[0], NT, preferred_element_type=F32)
        for j in range(1, N_PROJ):
            dhn += lax.dot_general(dp_ref[:, j * BLK : (j + 1) * BLK], w_ref[j], NT, preferred_element_type=F32)
        h = h_ref[...]
        r = lax.rsqrt(jnp.mean(h * h, axis=-1, keepdims=True) + EPS)
        hh = h * r
        acc_ref[0:1, :] += jnp.sum(dhn * hh, axis=0, keepdims=True)
        dg = dhn * g_ref[...]
        dh_ref[...] = dh2_ref[...] + r * (dg - hh * jnp.mean(dg * hh, axis=-1, keepdims=True))

    row = lambda w: pl.BlockSpec((tm, w), lambda i: (i, 0))
    return pl.pallas_call(
        body,
        name="b2_dh",
        grid=(nt,),
        in_specs=[row(N_PROJ * BLK), _VMEM, row(D_MODEL), row(D_MODEL), _VMEM],
        out_specs=(row(D_MODEL), pl.BlockSpec((8, D_MODEL), lambda i: (0, 0))),
        out_shape=(jax.ShapeDtypeStruct((n_rows, D_MODEL), F32), jax.ShapeDtypeStruct((8, D_MODEL), F32)),
        compiler_params=_params(),
    )(dproj, w3, hpad, dh2, g1)


def _gw_in(hnT, dproj):
    n_rows = dproj.shape[0]
    tk = ROW_TILE
    nk = n_rows // tk

    def body(a_ref, b_ref, out_ref, acc):
        k = pl.program_id(1)

        @pl.when(k == 0)
        def _():
            acc[...] = jnp.zeros_like(acc)

        acc[...] += jnp.dot(a_ref[...], b_ref[...], preferred_element_type=F32)

        @pl.when(k == nk - 1)
        def _():
            out_ref[0] = acc[...].astype(BF16)

    return pl.pallas_call(
        body,
        name="gw_in",
        grid=(N_PROJ, nk),
        in_specs=[pl.BlockSpec((D_MODEL, tk), lambda j, k: (0, k)), pl.BlockSpec((tk, BLK), lambda j, k: (k, j))],
        out_specs=pl.BlockSpec((1, D_MODEL, BLK), lambda j, k: (j, 0, 0)),
        out_shape=jax.ShapeDtypeStruct((N_PROJ, D_MODEL, BLK), BF16),
        scratch_shapes=[pltpu.VMEM((D_MODEL, BLK), F32)],
        compiler_params=_params(2),
    )(hnT, dproj)


def _gw_out(coT, roT, dh2):
    n_rows = dh2.shape[0]
    tk = ROW_TILE
    nk = n_rows // tk

    def body(c_ref, r_ref, d_ref, out_ref, acc):
        k = pl.program_id(0)

        @pl.when(k == 0)
        def _():
            acc[...] = jnp.zeros_like(acc)

        d = d_ref[...].astype(BF16)
        acc[0:D_CONV, :] += jnp.dot(c_ref[...], d, preferred_element_type=F32)
        acc[D_CONV:, :] += jnp.dot(r_ref[...], d, preferred_element_type=F32)

        @pl.when(k == nk - 1)
        def _():
            out_ref[...] = acc[...].astype(BF16)

    return pl.pallas_call(
        body,
        name="gw_out",
        grid=(nk,),
        in_specs=[
            pl.BlockSpec((D_CONV, tk), lambda k: (0, k)),
            pl.BlockSpec((D_RET, tk), lambda k: (0, k)),
            pl.BlockSpec((tk, D_MODEL), lambda k: (k, 0)),
        ],
        out_specs=pl.BlockSpec((D_MODEL, D_MODEL), lambda k: (0, 0)),
        out_shape=jax.ShapeDtypeStruct((D_MODEL, D_MODEL), BF16),
        scratch_shapes=[pltpu.VMEM((D_MODEL, D_MODEL), F32)],
        compiler_params=_params(),
    )(coT, roT, dh2)


def _adamw(w, g, m, v):
    m = ADAM_B1 * m + (1.0 - ADAM_B1) * g
    v = ADAM_B2 * v + (1.0 - ADAM_B2) * (g * g)
    m_hat = m / (1.0 - ADAM_B1**ADAM_STEP)
    v_hat = v / (1.0 - ADAM_B2**ADAM_STEP)
    delta = -ADAM_LR * (m_hat / (jnp.sqrt(v_hat) + ADAM_EPS) + ADAM_WD * w)
    return delta, m, v


def _sum_adamw(name, parts, w, m, v, rows_per_step):
    n_r, n_c = w.shape
    tr = rows_per_step

    def body(p_ref, w_ref, m_ref, v_ref, g_ref, d_ref, nm_ref, nv_ref):
        g = p_ref[0].astype(F32)
        for s in range(1, N_DEV):
            g = g + p_ref[s].astype(F32)
        g_ref[...] = g
        d_ref[...], nm_ref[...], nv_ref[...] = _adamw(w_ref[...], g, m_ref[...], v_ref[...])

    blk = pl.BlockSpec((tr, n_c), lambda i: (i, 0))
    return pl.pallas_call(
        body,
        name=name,
        grid=(n_r // tr,),
        in_specs=[pl.BlockSpec((N_DEV, tr, n_c), lambda i: (0, i, 0)), blk, blk, blk],
        out_specs=(blk,) * 4,
        out_shape=(jax.ShapeDtypeStruct((n_r, n_c), F32),) * 4,
        compiler_params=_params(),
    )(parts, w, m, v)


def _pack_sharded(meta_blk, conv_blk):
    p = jnp.zeros((PACK_ROWS, CHUNK), F32)
    p = p.at[SC_META0 : SC_META0 + N_META, :].set(meta_blk)
    return p.at[SC_CONV0 : SC_CONV0 + 3, : D_CONV // N_DEV].set(conv_blk)


def _unpack_sharded(p):
    return p[SC_META0 : SC_META0 + N_META, :], p[SC_CONV0 : SC_CONV0 + 3, : D_CONV // N_DEV]


def _pack_replicated(n1, rg, fg, loss=None):
    p = jnp.zeros((PACK_ROWS, CHUNK), F32)
    p = p.at[GA_N1 : GA_N1 + 8, :].set(n1.reshape(8, CHUNK))
    p = p.at[GA_RG : GA_RG + 4, :].set(rg.reshape(4, CHUNK))
    p = p.at[GA_FG : GA_FG + 8, :].set(fg.reshape(8, CHUNK))
    if loss is not None:
        p = p.at[GA_LOSS, :].set(loss)
    return p


def _unpack_replicated(p):
    return (
        p[GA_N1 : GA_N1 + 8, :].reshape(D_MODEL),
        p[GA_RG : GA_RG + 4, :].reshape(D_RET),
        p[GA_FG : GA_FG + 8, :].reshape(D_MODEL),
    )


def kernel(x, meta, norm1_g, w_in, conv_w, ret_norm_g, w_out, final_g, loss_target, m_meta, m_norm1_g, m_w_in, m_conv_w, m_ret_norm_g, m_w_out, m_final_g, v_meta, v_norm1_g, v_w_in, v_conv_w, v_ret_norm_g, v_w_out, v_final_g):
    seq = x.shape[1]
    n_rows = seq + CHUNK
    assert x.shape == (1, seq, D_MODEL) and n_rows % ROW_TILE == 0
    cols = D_CONV // N_DEV

    w3, wo3, small = _exchange(
        "gather_weights",
        [w_in.astype(BF16), w_out.astype(BF16), _pack_sharded(meta, conv_w)],
        [False, False, False],
    )
    wout = wo3.reshape(D_MODEL, D_MODEL)
    meta_full = small[:, SC_META0 : SC_META0 + N_META, :].transpose(1, 0, 2).reshape(N_META, D_MODEL)
    conv_full = small[:, SC_CONV0 : SC_CONV0 + 3, :cols].transpose(1, 0, 2).reshape(3, D_CONV)
    cw8 = jnp.zeros((8, D_CONV), F32).at[0:3, :].set(conv_full)

    hpad = jnp.concatenate([jnp.zeros((PAD_ROWS, D_MODEL), F32), meta_full, x[0]], axis=0)
    tpad = jnp.concatenate([jnp.zeros((CHUNK, D_MODEL), F32), loss_target[0]], axis=0)
    cos2, sin2 = _rotary_tables(n_rows)
    dec, dect, zeta, xi = _decay_tables()
    g1 = norm1_g.reshape(1, D_MODEL)
    gr = ret_norm_g.reshape(1, D_RET)
    gf = final_g.reshape(1, D_MODEL)

    hnT, proj, conv_out, conv_outT, conv = _f1(hpad, g1, w3, cw8, cos2, sin2)
    dh2, o, _, ret_outT, states, acc_f2 = _f2(proj, conv_out, hpad, tpad, wout, gr, gf, dec, zeta, xi)
    dproj, acc_b1 = _b1(dh2, proj, conv, o, states, wout, gr, cw8, dec, dect, zeta, xi, cos2, sin2)
    dh, acc_b2 = _b2(dproj, w3, hpad, dh2, g1)
    gw_in_parts = _gw_in(hnT, dproj)
    gw_out_parts = _gw_out(conv_outT, ret_outT, dh2).reshape(N_DEV, D_MODEL // N_DEV, D_MODEL)

    dmeta = dh[PAD_ROWS:CHUNK, :].reshape(N_META, N_DEV, CHUNK).transpose(1, 0, 2)
    dconv = acc_b1[0:3, :].reshape(3, N_DEV, cols).transpose(1, 0, 2)
    sc_parts = jnp.zeros((N_DEV, PACK_ROWS, CHUNK), F32)
    sc_parts = sc_parts.at[:, SC_META0 : SC_META0 + N_META, :].set(dmeta)
    sc_parts = sc_parts.at[:, SC_CONV0 : SC_CONV0 + 3, :cols].set(dconv)
    ga_part = _pack_replicated(acc_b2[0], acc_b1[3], acc_f2[0], acc_f2[1, 0:CHUNK])

    land_in, land_out, land_sc, land_ga = _exchange(
        "scatter_gradients", [gw_in_parts, gw_out_parts, sc_parts, ga_part], [True, True, True, False]
    )

    g_w_in, d_w_in, nm_w_in, nv_w_in = _sum_adamw("adamw_w_in", land_in, w_in, m_w_in, v_w_in, 256)
    g_w_out, d_w_out, nm_w_out, nv_w_out = _sum_adamw("adamw_w_out", land_out, w_out, m_w_out, v_w_out, 64)
    g_sc, d_sc, nm_sc, nv_sc = _sum_adamw(
        "adamw_small_sharded",
        land_sc,
        _pack_sharded(meta, conv_w),
        _pack_sharded(m_meta, m_conv_w),
        _pack_sharded(v_meta, v_conv_w),
        PACK_ROWS,
    )
    g_ga, d_ga, nm_ga, nv_ga = _sum_adamw(
        "adamw_small_replicated",
        land_ga,
        _pack_replicated(norm1_g, ret_norm_g, final_g),
        _pack_replicated(m_norm1_g, m_ret_norm_g, m_final_g),
        _pack_replicated(v_norm1_g, v_ret_norm_g, v_final_g),
        PACK_ROWS,
    )

    loss = g_ga[GA_LOSS, 0]
    grad_x = dh[CHUNK:, :].reshape(1, seq, D_MODEL)

    def leaves(w_in_leaf, w_out_leaf, sc, ga):
        meta_leaf, conv_leaf = _unpack_sharded(sc)
        n1_leaf, rg_leaf, fg_leaf = _unpack_replicated(ga)
        return (meta_leaf, n1_leaf, w_in_leaf, conv_leaf, rg_leaf, w_out_leaf, fg_leaf)

    return (
        loss,
        grad_x,
        *leaves(g_w_in, g_w_out, g_sc, g_ga),
        *leaves(d_w_in, d_w_out, d_sc, d_ga),
        *leaves(nm_w_in, nm_w_out, nm_sc, nm_ga),
        *leaves(nv_w_in, nv_w_out, nv_sc, nv_ga),
    )
```

```python
import functools
import math

import jax
import jax.numpy as jnp
import numpy as np
from jax import lax
from jax.experimental import pallas as pl
from jax.experimental.pallas import tpu as pltpu

F32 = jnp.float32
BF16 = jnp.bfloat16

N_DEV = 8
D_MODEL = 1024
N_META = 16
CHUNK = 128
D_CONV = 512
D_RET = 512
N_HEADS = 4
HEAD_DIM = 128
N_PROJ = 8
BLK = 512
ROPE_BASE = 10000.0
EPS = 1e-6
PAD_ROWS = CHUNK - N_META
Q_SCALE = HEAD_DIM ** -0.5
LOG_G = tuple(math.log(1.0 - 2.0 ** (-5.0 - h)) for h in range(N_HEADS))
CHUNK_DECAY = tuple(math.exp(CHUNK * lg) for lg in LOG_G)

ADAM_LR = 0.001
ADAM_B1 = 0.9
ADAM_B2 = 0.999
ADAM_EPS = 1e-08
ADAM_WD = 0.01
ADAM_STEP = 10

ROW_TILE = 640
VMEM_LIMIT = 56 * 1024 * 1024

SC_META0, SC_CONV0 = 0, 16
GA_N1, GA_RG, GA_FG, GA_LOSS = 0, 8, 12, 20
PACK_ROWS = 24

NT = (((1,), (1,)), ((), ()))
TN = (((0,), (0,)), ((), ()))
MESH = pl.DeviceIdType.MESH

_VMEM = pl.BlockSpec(memory_space=pltpu.VMEM)
_HBM = pl.BlockSpec(memory_space=pltpu.HBM)


def _params(n_axes=1):
    return pltpu.CompilerParams(dimension_semantics=("arbitrary",) * n_axes, vmem_limit_bytes=VMEM_LIMIT)


def _sigmoid(x):
    return 1.0 / (1.0 + jnp.exp(-x))


def _decay_tables():
    idx = np.arange(CHUNK, dtype=np.float64)
    diff = idx[:, None] - idx[None, :]
    dec = np.stack([np.where(diff >= 0, np.exp(diff * lg), 0.0) for lg in LOG_G])
    zeta = np.stack([np.exp((CHUNK - 1 - idx) * lg) for lg in LOG_G])
    xi = np.stack([np.exp((idx + 1.0) * lg) for lg in LOG_G])
    ones = np.ones((1, 1, CHUNK))
    return (
        jnp.asarray(dec, F32),
        jnp.asarray(dec.transpose(0, 2, 1), F32),
        jnp.asarray(zeta[:, :, None] * ones, F32),
        jnp.asarray(xi[:, :, None] * ones, F32),
    )


def _rotary_tables(n_rows):
    half = HEAD_DIM // 2
    freqs = 1.0 / (ROPE_BASE ** (jnp.arange(half, dtype=F32) / half))
    pos = (jnp.arange(n_rows, dtype=jnp.int32) - PAD_ROWS).astype(F32)
    ang = pos[:, None] * freqs[None, :]
    cos, sin = jnp.cos(ang), jnp.sin(ang)
    return jnp.concatenate([cos, cos], axis=1), jnp.concatenate([-sin, sin], axis=1)


def _rot(t, cos2, sin2):
    return t * cos2 + pltpu.roll(t, HEAD_DIM // 2, 1) * sin2


def _rot_bwd(d, cos2, sin2):
    return d * cos2 + pltpu.roll(d * sin2, HEAD_DIM // 2, 1)


def _gather(name, srcs):
    n = len(srcs)
    per = 7

    def body(*refs):
        src_refs, out_refs = refs[:n], refs[n : 2 * n]
        send_sems, recv_sems, local_sems = refs[2 * n :]
        x, y, c = lax.axis_index("x"), lax.axis_index("y"), lax.axis_index("c")
        sibling = (x, y, 1 - c)
        chips = [(1 - x, y), (x, 1 - y), (1 - x, 1 - y)]

        def slot(px, py, pc):
            return 4 * px + 2 * py + pc

        def copy(a, k, block, to, src=None):
            dst = out_refs[a].at[slot(*block)]
            return pltpu.make_async_remote_copy(
                src_ref=dst if src is None else src,
                dst_ref=dst,
                send_sem=send_sems.at[a * per + k],
                recv_sem=recv_sems.at[a * per + k],
                device_id=to,
                device_id_type=MESH,
            )

        me = (x, y, c)
        mine, first, passed = [], [], []
        for a in range(n):
            cp = pltpu.make_async_copy(src_refs[a], out_refs[a].at[slot(*me)], local_sems.at[a])
            cp.start()
            mine.append(cp)
        for j, chip in enumerate(chips):
            for a in range(n):
                cp = copy(a, 1 + j, me, (*chip, c), src=src_refs[a])
                cp.start()
                first.append(cp)
        for a in range(n):
            cp = copy(a, 0, me, sibling, src=src_refs[a])
            cp.start()
            first.append(cp)
        for j, chip in enumerate(chips):
            for a in range(n):
                copy(a, 1 + j, (*chip, c), me).wait_recv()
                cp = copy(a, 4 + j, (*chip, c), sibling)
                cp.start()
                passed.append(cp)
        for a in range(n):
            copy(a, 0, (x, y, 1 - c), me).wait_recv()
            for j, chip in enumerate(chips):
                copy(a, 4 + j, (*chip, 1 - c), me).wait_recv()
        for cp in first + passed:
            cp.wait_send()
        for cp in mine:
            cp.wait()

    return pl.pallas_call(
        body,
        name=name,
        out_shape=tuple(jax.ShapeDtypeStruct((N_DEV, *s.shape), s.dtype) for s in srcs),
        in_specs=[_HBM] * n,
        out_specs=tuple([_HBM] * n),
        scratch_shapes=[
            pltpu.SemaphoreType.DMA((n * per,)),
            pltpu.SemaphoreType.DMA((n * per,)),
            pltpu.SemaphoreType.DMA((n,)),
        ],
    )(*srcs)


def _f1(hpad, g1, w3, cw8, cos2, sin2):
    n_rows = hpad.shape[0]
    tm = ROW_TILE
    nt = n_rows // tm

    def body(h_ref, g_ref, w_ref, cw_ref, cos_ref, sin_ref, hnT_ref, pr_ref, co_ref, coT_ref, cv_ref, halo):
        i = pl.program_id(0)

        @pl.when(i == 0)
        def _():
            halo[...] = jnp.zeros_like(halo)

        h = h_ref[...]
        r = lax.rsqrt(jnp.mean(h * h, axis=-1, keepdims=True) + EPS)
        hn = (h * r * g_ref[...]).astype(BF16)
        hnT_ref[...] = hn.T

        def proj(j):
            return jnp.dot(hn, w_ref[j], preferred_element_type=F32)

        cx, cb, cc, cg = proj(0), proj(1), proj(2), proj(3)
        u = cc * cx
        rows = lax.broadcasted_iota(jnp.int32, u.shape, 0)
        hl = halo[...]
        u1 = jnp.where(rows == 0, hl[7:8], pltpu.roll(u, 1, 0))
        u2 = jnp.where(rows == 0, hl[6:7], jnp.where(rows == 1, hl[7:8], pltpu.roll(u, 2, 0)))
        halo[...] = u[tm - 8 : tm]
        cw = cw_ref[...]
        conv = cw[0:1] * u2 + cw[1:2] * u1 + cw[2:3] * u
        co = (cb * conv * (cg * _sigmoid(cg))).astype(BF16)
        co_ref[...] = co
        coT_ref[...] = co.T
        cv_ref[...] = conv.astype(BF16)
        pr_ref[:, 0 * BLK : 1 * BLK] = cx.astype(BF16)
        pr_ref[:, 1 * BLK : 2 * BLK] = cb.astype(BF16)
        pr_ref[:, 2 * BLK : 3 * BLK] = cc.astype(BF16)
        pr_ref[:, 3 * BLK : 4 * BLK] = cg.astype(BF16)
        cos_t, sin_t = cos_ref[...], sin_ref[...]
        q, k = proj(4), proj(5)
        for hd in range(N_HEADS):
            c0 = hd * HEAD_DIM
            pr_ref[:, 4 * BLK + c0 : 4 * BLK + c0 + HEAD_DIM] = (
                _rot(q[:, c0 : c0 + HEAD_DIM], cos_t, sin_t) * Q_SCALE
            ).astype(BF16)
            pr_ref[:, 5 * BLK + c0 : 5 * BLK + c0 + HEAD_DIM] = _rot(k[:, c0 : c0 + HEAD_DIM], cos_t, sin_t).astype(BF16)
        pr_ref[:, 6 * BLK : 7 * BLK] = proj(6).astype(BF16)
        pr_ref[:, 7 * BLK : 8 * BLK] = proj(7).astype(BF16)

    row = lambda w: pl.BlockSpec((tm, w), lambda i: (i, 0))
    col = lambda w: pl.BlockSpec((w, tm), lambda i: (0, i))
    return pl.pallas_call(
        body,
        name="f1_inproj_conv",
        grid=(nt,),
        in_specs=[row(D_MODEL), _VMEM, _VMEM, _VMEM, row(HEAD_DIM), row(HEAD_DIM)],
        out_specs=(col(D_MODEL), row(N_PROJ * BLK), row(D_CONV), col(D_CONV), row(D_CONV)),
        out_shape=(
            jax.ShapeDtypeStruct((D_MODEL, n_rows), BF16),
            jax.ShapeDtypeStruct((n_rows, N_PROJ * BLK), BF16),
            jax.ShapeDtypeStruct((n_rows, D_CONV), BF16),
            jax.ShapeDtypeStruct((D_CONV, n_rows), BF16),
            jax.ShapeDtypeStruct((n_rows, D_CONV), BF16),
        ),
        scratch_shapes=[pltpu.VMEM((8, D_CONV), F32)],
        compiler_params=_params(),
    )(hpad, g1, w3, cw8, cos2, sin2)


def _group_norm(o):
    ys, rs = [], []
    for hd in range(N_HEADS):
        oh = o[:, hd * HEAD_DIM : (hd + 1) * HEAD_DIM]
        xc = oh - jnp.mean(oh, axis=-1, keepdims=True)
        rstd = lax.rsqrt(jnp.mean(xc * xc, axis=-1, keepdims=True) + EPS)
        ys.append(xc * rstd)
        rs.append(jnp.broadcast_to(rstd, oh.shape))
    return jnp.concatenate(ys, axis=1), jnp.concatenate(rs, axis=1)


def _f2(proj, conv_out, hpad, tpad, wout, gr, gf, dec, zeta, xi):
    n_rows = hpad.shape[0]
    tm = ROW_TILE
    nt = n_rows // tm
    nct = tm // CHUNK

    def body(q_ref, k_ref, v_ref, rg_ref, co_ref, h_ref, t_ref, wo_ref, gr_ref, gf_ref, dec_ref, zeta_ref, xi_ref,
             dh2_ref, o_ref, ro_ref, roT_ref, st_ref, acc_ref, state, obuf):
        i = pl.program_id(0)

        @pl.when(i == 0)
        def _():
            state[...] = jnp.zeros_like(state)
            acc_ref[...] = jnp.zeros_like(acc_ref)

        for c in range(nct):
            rs = slice(c * CHUNK, (c + 1) * CHUNK)
            for hd in range(N_HEADS):
                cs = slice(hd * HEAD_DIM, (hd + 1) * HEAD_DIM)
                q, k, v = q_ref[rs, cs], k_ref[rs, cs], v_ref[rs, cs]
                st = state[hd]
                st_bf = st.astype(BF16)
                st_ref[c, hd] = st_bf
                s = lax.dot_general(q, k, NT, preferred_element_type=F32) * dec_ref[hd]
                inner = jnp.dot(s.astype(BF16), v, preferred_element_type=F32)
                qx = (q.astype(F32) * xi_ref[hd]).astype(BF16)
                obuf[rs, cs] = inner + jnp.dot(qx, st_bf, preferred_element_type=F32)
                kz = (k.astype(F32) * zeta_ref[hd]).astype(BF16)
                state[hd] = CHUNK_DECAY[hd] * st + lax.dot_general(kz, v, TN, preferred_element_type=F32)

        o = obuf[...]
        o_ref[...] = o.astype(BF16)
        yh, _ = _group_norm(o)
        rg = rg_ref[...].astype(F32)
        ro = (yh * gr_ref[...] * (rg * _sigmoid(rg))).astype(BF16)
        ro_ref[...] = ro
        roT_ref[...] = ro.T
        h2 = (
            h_ref[...]
            + jnp.dot(co_ref[...], wo_ref[0:D_CONV], preferred_element_type=F32)
            + jnp.dot(ro, wo_ref[D_CONV:], preferred_element_type=F32)
        )
        r2 = lax.rsqrt(jnp.mean(h2 * h2, axis=-1, keepdims=True) + EPS)
        yn = h2 * r2
        gfv = gf_ref[...]
        rows = i * tm + lax.broadcasted_iota(jnp.int32, (tm, 1), 0)
        valid = rows >= CHUNK
        err = jnp.where(valid, yn * gfv - t_ref[...], 0.0)
        tile_loss = jnp.sum(jnp.sum(err * err, axis=-1, keepdims=True), axis=0, keepdims=True) * (0.5 / D_MODEL)
        dy = err * (1.0 / D_MODEL)
        acc_ref[0:1, :] += jnp.sum(dy * yn, axis=0, keepdims=True)
        acc_ref[1:2, :] += tile_loss
        dyn = dy * gfv
        dh2_ref[...] = r2 * (dyn - yn * jnp.mean(dyn * yn, axis=-1, keepdims=True))

    row = lambda w, j=0: pl.BlockSpec((tm, w), lambda i: (i, j))
    return pl.pallas_call(
        body,
        name="f2_retention_out",
        grid=(nt,),
        in_specs=[row(BLK, 4), row(BLK, 5), row(BLK, 6), row(BLK, 7), row(D_CONV), row(D_MODEL), row(D_MODEL)]
        + [_VMEM] * 6,
        out_specs=(
            row(D_MODEL),
            row(D_RET),
            row(D_RET),
            pl.BlockSpec((D_RET, tm), lambda i: (0, i)),
            pl.BlockSpec((nct, N_HEADS, HEAD_DIM, HEAD_DIM), lambda i: (i, 0, 0, 0)),
            pl.BlockSpec((8, D_MODEL), lambda i: (0, 0)),
        ),
        out_shape=(
            jax.ShapeDtypeStruct((n_rows, D_MODEL), F32),
            jax.ShapeDtypeStruct((n_rows, D_RET), BF16),
            jax.ShapeDtypeStruct((n_rows, D_RET), BF16),
            jax.ShapeDtypeStruct((D_RET, n_rows), BF16),
            jax.ShapeDtypeStruct((n_rows // CHUNK, N_HEADS, HEAD_DIM, HEAD_DIM), BF16),
            jax.ShapeDtypeStruct((8, D_MODEL), F32),
        ),
        scratch_shapes=[pltpu.VMEM((N_HEADS, HEAD_DIM, HEAD_DIM), F32), pltpu.VMEM((tm, D_RET), F32)],
        compiler_params=_params(),
    )(proj, proj, proj, proj, conv_out, hpad, tpad, wout, gr, gf, dec, zeta, xi)


def _b1(dh2, proj, conv, o, states, wout, gr, cw8, dec, dect, zeta, xi, cos2, sin2):
    n_rows = dh2.shape[0]
    tm = ROW_TILE
    nt = n_rows // tm
    nct = tm // CHUNK

    def body(dh2_ref, pr_ref, cv_ref, o_ref, st_ref, wo_ref, gr_ref, cw_ref, dec_ref, dect_ref, zeta_ref, xi_ref,
             cos_ref, sin_ref, dp_ref, acc_ref, dstate, halo, dobuf):
        i = pl.program_id(0)

        @pl.when(i == 0)
        def _():
            dstate[...] = jnp.zeros_like(dstate)
            halo[...] = jnp.zeros_like(halo)
            acc_ref[...] = jnp.zeros_like(acc_ref)

        dmix = lax.dot_general(dh2_ref[...].astype(BF16), wo_ref[...], NT, preferred_element_type=F32)

        dco = dmix[:, :D_CONV]
        cx = pr_ref[:, 0 * BLK : 1 * BLK].astype(F32)
        cb = pr_ref[:, 1 * BLK : 2 * BLK].astype(F32)
        cc = pr_ref[:, 2 * BLK : 3 * BLK].astype(F32)
        cg = pr_ref[:, 3 * BLK : 4 * BLK].astype(F32)
        conv = cv_ref[...].astype(F32)
        sg = _sigmoid(cg)
        silu, dsilu = cg * sg, sg * (1.0 + cg * (1.0 - sg))
        t = dco * cb
        dp_ref[:, 1 * BLK : 2 * BLK] = (dco * conv * silu).astype(BF16)
        dp_ref[:, 3 * BLK : 4 * BLK] = (t * conv * dsilu).astype(BF16)
        dconv = t * silu
        rows = lax.broadcasted_iota(jnp.int32, dconv.shape, 0)
        hl = halo[...]
        dc1 = jnp.where(rows == tm - 1, hl[0:1], pltpu.roll(dconv, tm - 1, 0))
        dc2 = jnp.where(rows == tm - 2, hl[0:1], jnp.where(rows == tm - 1, hl[1:2], pltpu.roll(dconv, tm - 2, 0)))
        halo[...] = dconv[0:8]
        cw = cw_ref[...]
        du = cw[2:3] * dconv + cw[1:2] * dc1 + cw[0:1] * dc2
        u = cc * cx
        acc_ref[0:1, :] += jnp.sum(u * dc2, axis=0, keepdims=True)
        acc_ref[1:2, :] += jnp.sum(u * dc1, axis=0, keepdims=True)
        acc_ref[2:3, :] += jnp.sum(u * dconv, axis=0, keepdims=True)
        dp_ref[:, 0 * BLK : 1 * BLK] = (du * cc).astype(BF16)
        dp_ref[:, 2 * BLK : 3 * BLK] = (du * cx).astype(BF16)

        dro = dmix[:, D_CONV:]
        rg = pr_ref[:, 7 * BLK : 8 * BLK].astype(F32)
        sg = _sigmoid(rg)
        silu, dsilu = rg * sg, sg * (1.0 + rg * (1.0 - sg))
        yh, rstd = _group_norm(o_ref[...].astype(F32))
        grv = gr_ref[...]
        dp_ref[:, 7 * BLK : 8 * BLK] = (dro * (yh * grv) * dsilu).astype(BF16)
        dret = dro * silu
        acc_ref[3:4, :] += jnp.sum(dret * yh, axis=0, keepdims=True)
        dyh = dret * grv
        for hd in range(N_HEADS):
            cs = slice(hd * HEAD_DIM, (hd + 1) * HEAD_DIM)
            a, b = dyh[:, cs], yh[:, cs]
            dobuf[:, cs] = rstd[:, cs] * (
                a - jnp.mean(a, axis=-1, keepdims=True) - b * jnp.mean(a * b, axis=-1, keepdims=True)
            )

        for c in reversed(range(nct)):
            rs = slice(c * CHUNK, (c + 1) * CHUNK)
            cos_t, sin_t = cos_ref[rs, :], sin_ref[rs, :]
            for hd in range(N_HEADS):
                cs = slice(hd * HEAD_DIM, (hd + 1) * HEAD_DIM)
                q = pr_ref[rs, 4 * BLK + hd * HEAD_DIM : 4 * BLK + (hd + 1) * HEAD_DIM]
                k = pr_ref[rs, 5 * BLK + hd * HEAD_DIM : 5 * BLK + (hd + 1) * HEAD_DIM]
                v = pr_ref[rs, 6 * BLK + hd * HEAD_DIM : 6 * BLK + (hd + 1) * HEAD_DIM]
                do = dobuf[rs, cs].astype(BF16)
                st_bf = st_ref[c, hd]
                dst = dstate[hd]
                dst_bf = dst.astype(BF16)
                zt, xt = zeta_ref[hd], xi_ref[hd]
                sT = (lax.dot_general(k, q, NT, preferred_element_type=F32) * dect_ref[hd]).astype(BF16)
                dsT = (lax.dot_general(v, do, NT, preferred_element_type=F32) * dect_ref[hd]).astype(BF16)
                ds = (lax.dot_general(do, v, NT, preferred_element_type=F32) * dec_ref[hd]).astype(BF16)
                kz = (k.astype(F32) * zt).astype(BF16)
                dv = jnp.dot(sT, do, preferred_element_type=F32) + jnp.dot(kz, dst_bf, preferred_element_type=F32)
                dq = jnp.dot(ds, k, preferred_element_type=F32) + xt * lax.dot_general(
                    do, st_bf, NT, preferred_element_type=F32
                )
                dk = jnp.dot(dsT, q, preferred_element_type=F32) + zt * lax.dot_general(
                    v, dst_bf, NT, preferred_element_type=F32
                )
                qx = (q.astype(F32) * xt).astype(BF16)
                dstate[hd] = CHUNK_DECAY[hd] * dst + lax.dot_general(qx, do, TN, preferred_element_type=F32)
                dp_ref[rs, 4 * BLK + hd * HEAD_DIM : 4 * BLK + (hd + 1) * HEAD_DIM] = _rot_bwd(
                    dq * Q_SCALE, cos_t, sin_t
                ).astype(BF16)
                dp_ref[rs, 5 * BLK + hd * HEAD_DIM : 5 * BLK + (hd + 1) * HEAD_DIM] = _rot_bwd(dk, cos_t, sin_t).astype(BF16)
                dp_ref[rs, 6 * BLK + hd * HEAD_DIM : 6 * BLK + (hd + 1) * HEAD_DIM] = dv.astype(BF16)

    rev = lambda w: pl.BlockSpec((tm, w), lambda i: (nt - 1 - i, 0))
    return pl.pallas_call(
        body,
        name="b1_dproj",
        grid=(nt,),
        in_specs=[
            rev(D_MODEL),
            rev(N_PROJ * BLK),
            rev(D_CONV),
            rev(D_RET),
            pl.BlockSpec((nct, N_HEADS, HEAD_DIM, HEAD_DIM), lambda i: (nt - 1 - i, 0, 0, 0)),
        ]
        + [_VMEM] * 7
        + [rev(HEAD_DIM), rev(HEAD_DIM)],
        out_specs=(rev(N_PROJ * BLK), pl.BlockSpec((8, D_CONV), lambda i: (0, 0))),
        out_shape=(jax.ShapeDtypeStruct((n_rows, N_PROJ * BLK), BF16), jax.ShapeDtypeStruct((8, D_CONV), F32)),
        scratch_shapes=[
            pltpu.VMEM((N_HEADS, HEAD_DIM, HEAD_DIM), F32),
            pltpu.VMEM((8, D_CONV), F32),
            pltpu.VMEM((tm, D_RET), F32),
        ],
        compiler_params=_params(),
    )(dh2, proj, conv, o, states, wout, gr, cw8, dec, dect, zeta, xi, cos2, sin2)


def _b2(dproj, w3, hpad, dh2, g1):
    n_rows = hpad.shape[0]
    tm = ROW_TILE
    nt = n_rows // tm

    def body(dp_ref, w_ref, h_ref, dh2_ref, g_ref, dh_ref, acc_ref):
        @pl.when(pl.program_id(0) == 0)
        def _():
            acc_ref[...] = jnp.zeros_like(acc_ref)

        dhn = lax.dot_general(dp_ref[:, 0:BLK], w_ref[0], NT, preferred_element_type=F32)
        for j in range(1, N_PROJ):
            dhn += lax.dot_general(dp_ref[:, j * BLK : (j + 1) * BLK], w_ref[j], NT, preferred_element_type=F32)
        h = h_ref[...]
        r = lax.rsqrt(jnp.mean(h * h, axis=-1, keepdims=True) + EPS)
        hh = h * r
        acc_ref[0:1, :] += jnp.sum(dhn * hh, axis=0, keepdims=True)
        dg = dhn * g_ref[...]
        dh_ref[...] = dh2_ref[...] + r * (dg - hh * jnp.mean(dg * hh, axis=-1, keepdims=True))

    row = lambda w: pl.BlockSpec((tm, w), lambda i: (i, 0))
    return pl.pallas_call(
        body,
        name="b2_dh",
        grid=(nt,),
        in_specs=[row(N_PROJ * BLK), _VMEM, row(D_MODEL), row(D_MODEL), _VMEM],
        out_specs=(row(D_MODEL), pl.BlockSpec((8, D_MODEL), lambda i: (0, 0))),
        out_shape=(jax.ShapeDtypeStruct((n_rows, D_MODEL), F32), jax.ShapeDtypeStruct((8, D_MODEL), F32)),
        compiler_params=_params(),
    )(dproj, w3, hpad, dh2, g1)


def _gw_in_scatter(hnT, dproj, gw_out_parts, sc_parts, ga_part, me_arr):
    n_rows = dproj.shape[0]
    last = N_DEV - 1
    by_dest = (True, True, False)

    def body(me_ref, a_ref, b_ref, go_ref, sc_ref, ga_ref, land_in, land_go, land_sc, land_ga,
             sendbuf, send_sems, recv_sems, local_sems):
        del me_ref
        s = pl.program_id(0)
        t = last - s
        x, y, c = lax.axis_index("x"), lax.axis_index("y"), lax.axis_index("c")
        me = 4 * x + 2 * y + c
        srcs, lands = (go_ref, sc_ref, ga_ref), (land_go, land_sc, land_ga)

        def peer_at(k):
            return (1 - x if k & 4 else x, 1 - y if k & 2 else y, 1 - c if k & 1 else c)

        def small_copy(a, k):
            px, py, pc = peer_at(k)
            return pltpu.make_async_remote_copy(
                src_ref=srcs[a].at[4 * px + 2 * py + pc] if by_dest[a] else srcs[a],
                dst_ref=lands[a].at[me],
                send_sem=send_sems.at[(a + 1) * last + k - 1],
                recv_sem=recv_sems.at[(a + 1) * last + k - 1],
                device_id=(px, py, pc),
                device_id_type=MESH,
            )

        def small_local(a):
            return pltpu.make_async_copy(srcs[a].at[me] if by_dest[a] else srcs[a], lands[a].at[me], local_sems.at[a + 1])

        def block_copy(k, to):
            return pltpu.make_async_remote_copy(
                src_ref=sendbuf.at[k],
                dst_ref=land_in.at[me],
                send_sem=send_sems.at[k - 1],
                recv_sem=recv_sems.at[k - 1],
                device_id=to,
                device_id_type=MESH,
            )

        def own_copy():
            return pltpu.make_async_copy(sendbuf.at[0], land_in.at[me], local_sems.at[0])

        @pl.when(s == 0)
        def _():
            for a in range(3):
                small_local(a).start()
            for k in range(last, 0, -1):
                for a in range(3):
                    small_copy(a, k).start()

        sendbuf[t] = jnp.dot(a_ref[...], b_ref[...], preferred_element_type=F32).astype(BF16)

        @pl.when(t != 0)
        def _():
            to = (jnp.bitwise_xor(x, (t >> 2) & 1), jnp.bitwise_xor(y, (t >> 1) & 1), jnp.bitwise_xor(c, t & 1))
            block_copy(t, to).start()

        @pl.when(t == 0)
        def _():
            own_copy().start()
            for k in range(last, 0, -1):
                block_copy(k, peer_at(k)).wait()
                for a in range(3):
                    small_copy(a, k).wait()
            for a in range(3):
                small_local(a).wait()
            own_copy().wait()

    smalls = (gw_out_parts, sc_parts, ga_part)
    small_blocks = [p.shape[1:] if d else p.shape for p, d in zip(smalls, by_dest, strict=True)]
    grid_spec = pltpu.PrefetchScalarGridSpec(
        num_scalar_prefetch=1,
        grid=(N_DEV,),
        in_specs=[_VMEM, pl.BlockSpec((n_rows, BLK), lambda s, me: (0, jnp.bitwise_xor(me[0], last - s)))] + [_HBM] * 3,
        out_specs=tuple([_HBM] * 4),
        scratch_shapes=[
            pltpu.VMEM((N_DEV, D_MODEL, BLK), BF16),
            pltpu.SemaphoreType.DMA((4 * last,)),
            pltpu.SemaphoreType.DMA((4 * last,)),
            pltpu.SemaphoreType.DMA((4,)),
        ],
    )
    return pl.pallas_call(
        body,
        name="gw_in_scatter",
        grid_spec=grid_spec,
        out_shape=(jax.ShapeDtypeStruct((N_DEV, D_MODEL, BLK), BF16),)
        + tuple(jax.ShapeDtypeStruct((N_DEV, *b), p.dtype) for b, p in zip(small_blocks, smalls, strict=True)),
        compiler_params=_params(),
    )(me_arr, hnT, dproj, gw_out_parts, sc_parts, ga_part)


def _gw_out(coT, roT, dh2):
    n_rows = dh2.shape[0]
    tk = ROW_TILE
    nk = n_rows // tk

    def body(c_ref, r_ref, d_ref, out_ref, acc):
        k = pl.program_id(0)

        @pl.when(k == 0)
        def _():
            acc[...] = jnp.zeros_like(acc)

        d = d_ref[...].astype(BF16)
        acc[0:D_CONV, :] += jnp.dot(c_ref[...], d, preferred_element_type=F32)
        acc[D_CONV:, :] += jnp.dot(r_ref[...], d, preferred_element_type=F32)

        @pl.when(k == nk - 1)
        def _():
            out_ref[...] = acc[...].astype(BF16)

    return pl.pallas_call(
        body,
        name="gw_out",
        grid=(nk,),
        in_specs=[
            pl.BlockSpec((D_CONV, tk), lambda k: (0, k)),
            pl.BlockSpec((D_RET, tk), lambda k: (0, k)),
            pl.BlockSpec((tk, D_MODEL), lambda k: (k, 0)),
        ],
        out_specs=pl.BlockSpec((D_MODEL, D_MODEL), lambda k: (0, 0)),
        out_shape=jax.ShapeDtypeStruct((D_MODEL, D_MODEL), BF16),
        scratch_shapes=[pltpu.VMEM((D_MODEL, D_MODEL), F32)],
        compiler_params=_params(),
    )(coT, roT, dh2)


def _adamw(w, g, m, v):
    m = ADAM_B1 * m + (1.0 - ADAM_B1) * g
    v = ADAM_B2 * v + (1.0 - ADAM_B2) * (g * g)
    m_hat = m / (1.0 - ADAM_B1**ADAM_STEP)
    v_hat = v / (1.0 - ADAM_B2**ADAM_STEP)
    delta = -ADAM_LR * (m_hat / (jnp.sqrt(v_hat) + ADAM_EPS) + ADAM_WD * w)
    return delta, m, v


def _sum_adamw(name, parts, w, m, v, rows_per_step):
    n_r, n_c = w.shape
    tr = rows_per_step

    def body(p_ref, w_ref, m_ref, v_ref, g_ref, d_ref, nm_ref, nv_ref):
        g = p_ref[0].astype(F32)
        for s in range(1, N_DEV):
            g = g + p_ref[s].astype(F32)
        g_ref[...] = g
        d_ref[...], nm_ref[...], nv_ref[...] = _adamw(w_ref[...], g, m_ref[...], v_ref[...])

    blk = pl.BlockSpec((tr, n_c), lambda i: (i, 0))
    return pl.pallas_call(
        body,
        name=name,
        grid=(n_r // tr,),
        in_specs=[pl.BlockSpec((N_DEV, tr, n_c), lambda i: (0, i, 0)), blk, blk, blk],
        out_specs=(blk,) * 4,
        out_shape=(jax.ShapeDtypeStruct((n_r, n_c), F32),) * 4,
        compiler_params=_params(),
    )(parts, w, m, v)


def _pack_sharded(meta_blk, conv_blk):
    p = jnp.zeros((PACK_ROWS, CHUNK), F32)
    p = p.at[SC_META0 : SC_META0 + N_META, :].set(meta_blk)
    return p.at[SC_CONV0 : SC_CONV0 + 3, : D_CONV // N_DEV].set(conv_blk)


def _unpack_sharded(p):
    return p[SC_META0 : SC_META0 + N_META, :], p[SC_CONV0 : SC_CONV0 + 3, : D_CONV // N_DEV]


def _pack_replicated(n1, rg, fg, loss=None):
    p = jnp.zeros((PACK_ROWS, CHUNK), F32)
    p = p.at[GA_N1 : GA_N1 + 8, :].set(n1.reshape(8, CHUNK))
    p = p.at[GA_RG : GA_RG + 4, :].set(rg.reshape(4, CHUNK))
    p = p.at[GA_FG : GA_FG + 8, :].set(fg.reshape(8, CHUNK))
    if loss is not None:
        p = p.at[GA_LOSS, :].set(loss)
    return p


def _unpack_replicated(p):
    return (
        p[GA_N1 : GA_N1 + 8, :].reshape(D_MODEL),
        p[GA_RG : GA_RG + 4, :].reshape(D_RET),
        p[GA_FG : GA_FG + 8, :].reshape(D_MODEL),
    )


def kernel(x, meta, norm1_g, w_in, conv_w, ret_norm_g, w_out, final_g, loss_target, m_meta, m_norm1_g, m_w_in, m_conv_w, m_ret_norm_g, m_w_out, m_final_g, v_meta, v_norm1_g, v_w_in, v_conv_w, v_ret_norm_g, v_w_out, v_final_g):
    seq = x.shape[1]
    n_rows = seq + CHUNK
    assert x.shape == (1, seq, D_MODEL) and n_rows % ROW_TILE == 0
    cols = D_CONV // N_DEV

    w3, wo3, small = _gather("gather_weights", [w_in.astype(BF16), w_out.astype(BF16), _pack_sharded(meta, conv_w)])
    wout = wo3.reshape(D_MODEL, D_MODEL)
    meta_full = small[:, SC_META0 : SC_META0 + N_META, :].transpose(1, 0, 2).reshape(N_META, D_MODEL)
    conv_full = small[:, SC_CONV0 : SC_CONV0 + 3, :cols].transpose(1, 0, 2).reshape(3, D_CONV)
    cw8 = jnp.zeros((8, D_CONV), F32).at[0:3, :].set(conv_full)

    hpad = jnp.concatenate([jnp.zeros((PAD_ROWS, D_MODEL), F32), meta_full, x[0]], axis=0)
    tpad = jnp.concatenate([jnp.zeros((CHUNK, D_MODEL), F32), loss_target[0]], axis=0)
    cos2, sin2 = _rotary_tables(n_rows)
    dec, dect, zeta, xi = _decay_tables()
    g1 = norm1_g.reshape(1, D_MODEL)
    gr = ret_norm_g.reshape(1, D_RET)
    gf = final_g.reshape(1, D_MODEL)

    hnT, proj, conv_out, conv_outT, conv = _f1(hpad, g1, w3, cw8, cos2, sin2)
    dh2, o, _, ret_outT, states, acc_f2 = _f2(proj, conv_out, hpad, tpad, wout, gr, gf, dec, zeta, xi)
    dproj, acc_b1 = _b1(dh2, proj, conv, o, states, wout, gr, cw8, dec, dect, zeta, xi, cos2, sin2)
    dh, acc_b2 = _b2(dproj, w3, hpad, dh2, g1)
    gw_out_parts =_gw_out(conv_outT, ret_outT, dh2).reshape(N_DEV, D_MODEL // N_DEV, D_MODEL)

    dmeta = dh[PAD_ROWS:CHUNK, :].reshape(N_META, N_DEV, CHUNK).transpose(1, 0, 2)
    dconv = acc_b1[0:3, :].reshape(3, N_DEV, cols).transpose(1, 0, 2)
    sc_parts = jnp.zeros((N_DEV, PACK_ROWS, CHUNK), F32)
    sc_parts = sc_parts.at[:, SC_META0 : SC_META0 + N_META, :].set(dmeta)
    sc_parts = sc_parts.at[:, SC_CONV0 : SC_CONV0 + 3, :cols].set(dconv)
    ga_part = _pack_replicated(acc_b2[0], acc_b1[3], acc_f2[0], acc_f2[1, 0:CHUNK])

    me_arr = (4 * lax.axis_index("x") + 2 * lax.axis_index("y") + lax.axis_index("c")).astype(jnp.int32).reshape(1)
    land_in, land_out, land_sc, land_ga = _gw_in_scatter(hnT, dproj, gw_out_parts, sc_parts, ga_part, me_arr)

    g_w_in, d_w_in, nm_w_in, nv_w_in = _sum_adamw("adamw_w_in", land_in, w_in, m_w_in, v_w_in, 256)
    g_w_out, d_w_out, nm_w_out, nv_w_out = _sum_adamw("adamw_w_out", land_out, w_out, m_w_out, v_w_out, 64)
    g_sc, d_sc, nm_sc, nv_sc = _sum_adamw(
        "adamw_small_sharded",
        land_sc,
        _pack_sharded(meta, conv_w),
        _pack_sharded(m_meta, m_conv_w),
        _pack_sharded(v_meta, v_conv_w),
        PACK_ROWS,
    )
    g_ga, d_ga, nm_ga, nv_ga = _sum_adamw(
        "adamw_small_replicated",
        land_ga,
        _pack_replicated(norm1_g, ret_norm_g, final_g),
        _pack_replicated(m_norm1_g, m_ret_norm_g, m_final_g),
        _pack_replicated(v_norm1_g, v_ret_norm_g, v_final_g),
        PACK_ROWS,
    )

    loss = g_ga[GA_LOSS, 0]
    grad_x = dh[CHUNK:, :].reshape(1, seq, D_MODEL)

    def leaves(w_in_leaf, w_out_leaf, sc, ga):
        meta_leaf, conv_leaf = _unpack_sharded(sc)
        n1_leaf, rg_leaf, fg_leaf = _unpack_replicated(ga)
        return (meta_leaf, n1_leaf, w_in_leaf, conv_leaf, rg_leaf, w_out_leaf, fg_leaf)

    return (
        loss,
        grad_x,
        *leaves(g_w_in, g_w_out, g_sc, g_ga),
        *leaves(d_w_in, d_w_out, d_sc, d_ga),
        *leaves(nm_w_in, nm_w_out, nm_sc, nm_ga),
        *leaves(nv_w_in, nv_w_out, nv_sc, nv_ga),
    )
```

```python
import functools
import math

import jax
import jax.numpy as jnp
import numpy as np
from jax import lax
from jax.experimental import pallas as pl
from jax.experimental.pallas import tpu as pltpu

F32 = jnp.float32
BF16 = jnp.bfloat16

N_DEV = 8
N_CHIPS = 4
D_MODEL = 1024
N_META = 16
CHUNK = 128
D_CONV = 512
D_RET = 512
N_HEADS = 4
HEAD_DIM = 128
N_PROJ = 8
BLK = 512
ROPE_BASE = 10000.0
EPS = 1e-6
Q_SCALE = HEAD_DIM ** -0.5
LOG_G = tuple(math.log(1.0 - 2.0 ** (-5.0 - h)) for h in range(N_HEADS))
CHUNK_DECAY = tuple(math.exp(CHUNK * lg) for lg in LOG_G)

ADAM_LR = 0.001
ADAM_B1 = 0.9
ADAM_B2 = 0.999
ADAM_EPS = 1e-08
ADAM_WD = 0.01
ADAM_STEP = 10

ROW_TILE = 512
PAD_ROWS = ROW_TILE - N_META
VMEM_LIMIT = 56 * 1024 * 1024

SC_META0, SC_CONV0 = 0, 16
GA_N1, GA_RG, GA_FG, GA_LOSS = 0, 8, 12, 20
PACK_ROWS = 24

NT = (((1,), (1,)), ((), ()))
TN = (((0,), (0,)), ((), ()))
MESH = pl.DeviceIdType.MESH

_VMEM = pl.BlockSpec(memory_space=pltpu.VMEM)
_HBM = pl.BlockSpec(memory_space=pltpu.HBM)


def _params(n_axes=1):
    return pltpu.CompilerParams(dimension_semantics=("arbitrary",) * n_axes, vmem_limit_bytes=VMEM_LIMIT)


def _sigmoid(x):
    return 1.0 / (1.0 + jnp.exp(-x))


def _decay_tables():
    idx = np.arange(CHUNK, dtype=np.float64)
    diff = idx[:, None] - idx[None, :]
    dec = np.stack([np.where(diff >= 0, np.exp(diff * lg), 0.0) for lg in LOG_G])
    zeta = np.stack([np.exp((CHUNK - 1 - idx) * lg) for lg in LOG_G])
    xi = np.stack([np.exp((idx + 1.0) * lg) for lg in LOG_G])
    ones = np.ones((1, 1, CHUNK))
    return (
        jnp.asarray(dec, F32),
        jnp.asarray(dec.transpose(0, 2, 1), F32),
        jnp.asarray(zeta[:, :, None] * ones, F32),
        jnp.asarray(xi[:, :, None] * ones, F32),
    )


def _rotary_tables(n_tiles, tm):
    half = HEAD_DIM // 2
    freqs = (1.0 / (np.float32(ROPE_BASE) ** (np.arange(half, dtype=np.float32) / np.float32(half)))).astype(np.float64)
    sign = np.concatenate([-np.ones(half), np.ones(half)])
    two = lambda a: np.concatenate([a, a], axis=1)
    base = two((np.arange(n_tiles, dtype=np.float64) * tm - PAD_ROWS)[:, None] * freqs[None, :])
    off = two(np.arange(tm, dtype=np.float64)[:, None] * freqs[None, :])
    as32 = lambda a: jnp.asarray(a, F32)
    return as32(np.cos(base)), as32(np.sin(base) * sign), as32(np.cos(off)), as32(np.sin(off) * sign)


def _tile_rotary(ca_ref, sa_ref, cb_ref, sb_ref, tile):
    ca, sa = ca_ref[pl.ds(tile, 1), :], sa_ref[pl.ds(tile, 1), :]
    cb, sb = cb_ref[...], sb_ref[...]
    return ca * cb - sa * sb, sa * cb + ca * sb


def _rot(t, cos2, sin2):
    return t * cos2 + pltpu.roll(t, HEAD_DIM // 2, 1) * sin2


def _rot_bwd(d, cos2, sin2):
    return d * cos2 + pltpu.roll(d * sin2, HEAD_DIM // 2, 1)


def _token_tile(i):
    return jnp.maximum(i - 1, 0)


def _gather(name, srcs):
    n = len(srcs)
    per = 7

    def body(*refs):
        src_refs, out_refs = refs[:n], refs[n : 2 * n]
        send_sems, recv_sems, local_sems = refs[2 * n :]
        x, y, c = lax.axis_index("x"), lax.axis_index("y"), lax.axis_index("c")
        sibling = (x, y, 1 - c)
        chips = [(1 - x, y), (x, 1 - y), (1 - x, 1 - y)]

        def slot(px, py, pc):
            return 4 * px + 2 * py + pc

        def copy(a, k, block, to, src=None):
            dst = out_refs[a].at[slot(*block)]
            return pltpu.make_async_remote_copy(
                src_ref=dst if src is None else src,
                dst_ref=dst,
                send_sem=send_sems.at[a * per + k],
                recv_sem=recv_sems.at[a * per + k],
                device_id=to,
                device_id_type=MESH,
            )

        me = (x, y, c)
        mine, first, passed = [], [], []
        for a in range(n):
            cp = pltpu.make_async_copy(src_refs[a], out_refs[a].at[slot(*me)], local_sems.at[a])
            cp.start()
            mine.append(cp)
        for j, chip in enumerate(chips):
            for a in range(n):
                cp = copy(a, 1 + j, me, (*chip, c), src=src_refs[a])
                cp.start()
                first.append(cp)
        for a in range(n):
            cp = copy(a, 0, me, sibling, src=src_refs[a])
            cp.start()
            first.append(cp)
        for j, chip in enumerate(chips):
            for a in range(n):
                copy(a, 1 + j, (*chip, c), me).wait_recv()
                cp = copy(a, 4 + j, (*chip, c), sibling)
                cp.start()
                passed.append(cp)
        for a in range(n):
            copy(a, 0, (x, y, 1 - c), me).wait_recv()
            for j, chip in enumerate(chips):
                copy(a, 4 + j, (*chip, 1 - c), me).wait_recv()
        for cp in first + passed:
            cp.wait_send()
        for cp in mine:
            cp.wait()

    return pl.pallas_call(
        body,
        name=name,
        out_shape=tuple(jax.ShapeDtypeStruct((N_DEV, *s.shape), s.dtype) for s in srcs),
        in_specs=[_HBM] * n,
        out_specs=tuple([_HBM] * n),
        scratch_shapes=[
            pltpu.SemaphoreType.DMA((n * per,)),
            pltpu.SemaphoreType.DMA((n * per,)),
            pltpu.SemaphoreType.DMA((n,)),
        ],
    )(*srcs)


def _f1(x2, meta_tile, g1, w3, cw8, rot):
    tm = ROW_TILE
    nt = x2.shape[0] // tm + 1
    n_rows = nt * tm

    def body(x_ref, mt_ref, g_ref, w_ref, cw_ref, ca_ref, sa_ref, cb_ref, sb_ref,
             hnT_ref, pr_ref, co_ref, coT_ref, cv_ref, halo):
        i = pl.program_id(0)

        @pl.when(i == 0)
        def _():
            halo[...] = jnp.zeros_like(halo)

        h = jnp.where(i == 0, mt_ref[...], x_ref[...])
        r = lax.rsqrt(jnp.mean(h * h, axis=-1, keepdims=True) + EPS)
        hn = (h * r * g_ref[...]).astype(BF16)
        hnT_ref[...] = hn.T

        def proj(j):
            return jnp.dot(hn, w_ref[j], preferred_element_type=F32)

        cx, cb, cc, cg = proj(0), proj(1), proj(2), proj(3)
        u = cc * cx
        rows = lax.broadcasted_iota(jnp.int32, u.shape, 0)
        hl = halo[...]
        u1 = jnp.where(rows == 0, hl[7:8], pltpu.roll(u, 1, 0))
        u2 = jnp.where(rows == 0, hl[6:7], jnp.where(rows == 1, hl[7:8], pltpu.roll(u, 2, 0)))
        halo[...] = u[tm - 8 : tm]
        cw = cw_ref[...]
        conv = cw[0:1] * u2 + cw[1:2] * u1 + cw[2:3] * u
        co = (cb * conv * (cg * _sigmoid(cg))).astype(BF16)
        co_ref[...] = co
        coT_ref[...] = co.T
        cv_ref[...] = conv.astype(BF16)
        pr_ref[:, 0 * BLK : 1 * BLK] = cx.astype(BF16)
        pr_ref[:, 1 * BLK : 2 * BLK] = cb.astype(BF16)
        pr_ref[:, 2 * BLK : 3 * BLK] = cc.astype(BF16)
        pr_ref[:, 3 * BLK : 4 * BLK] = cg.astype(BF16)
        cos_t, sin_t = _tile_rotary(ca_ref, sa_ref, cb_ref, sb_ref, i)
        q, k = proj(4), proj(5)
        for hd in range(N_HEADS):
            c0 = hd * HEAD_DIM
            pr_ref[:, 4 * BLK + c0 : 4 * BLK + c0 + HEAD_DIM] = (
                _rot(q[:, c0 : c0 + HEAD_DIM], cos_t, sin_t) * Q_SCALE
            ).astype(BF16)
            pr_ref[:, 5 * BLK + c0 : 5 * BLK + c0 + HEAD_DIM] = _rot(k[:, c0 : c0 + HEAD_DIM], cos_t, sin_t).astype(BF16)
        pr_ref[:, 6 * BLK : 7 * BLK] = proj(6).astype(BF16)
        pr_ref[:, 7 * BLK : 8 * BLK] = proj(7).astype(BF16)

    row = lambda w: pl.BlockSpec((tm, w), lambda i: (i, 0))
    col = lambda w: pl.BlockSpec((w, tm), lambda i: (0, i))
    return pl.pallas_call(
        body,
        name="f1_inproj_conv",
        grid=(nt,),
        in_specs=[pl.BlockSpec((tm, D_MODEL), lambda i: (_token_tile(i), 0))] + [_VMEM] * 8,
        out_specs=(col(D_MODEL), row(N_PROJ * BLK), row(D_CONV), col(D_CONV), row(D_CONV)),
        out_shape=(
            jax.ShapeDtypeStruct((D_MODEL, n_rows), BF16),
            jax.ShapeDtypeStruct((n_rows, N_PROJ * BLK), BF16),
            jax.ShapeDtypeStruct((n_rows, D_CONV), BF16),
            jax.ShapeDtypeStruct((D_CONV, n_rows), BF16),
            jax.ShapeDtypeStruct((n_rows, D_CONV), BF16),
        ),
        scratch_shapes=[pltpu.VMEM((8, D_CONV), F32)],
        compiler_params=_params(),
    )(x2, meta_tile, g1, w3, cw8, *rot)


def _group_norm(o):
    ys, rs = [], []
    for hd in range(N_HEADS):
        oh = o[:, hd * HEAD_DIM : (hd + 1) * HEAD_DIM]
        xc = oh - jnp.mean(oh, axis=-1, keepdims=True)
        rstd = lax.rsqrt(jnp.mean(xc * xc, axis=-1, keepdims=True) + EPS)
        ys.append(xc * rstd)
        rs.append(jnp.broadcast_to(rstd, oh.shape))
    return jnp.concatenate(ys, axis=1), jnp.concatenate(rs, axis=1)


def _f2(proj, conv_out, x2, meta_tile, t2, wout, gr, gf, dec, zeta, xi):
    n_rows = proj.shape[0]
    tm = ROW_TILE
    nt = n_rows // tm
    nct = tm // CHUNK

    def body(q_ref, k_ref, v_ref, rg_ref, co_ref, x_ref, mt_ref, t_ref, wo_ref, gr_ref, gf_ref, dec_ref, zeta_ref,
             xi_ref, dh2_ref, o_ref, ro_ref, roT_ref, st_ref, acc_ref, state, obuf):
        i = pl.program_id(0)

        @pl.when(i == 0)
        def _():
            state[...] = jnp.zeros_like(state)
            acc_ref[...] = jnp.zeros_like(acc_ref)

        for c in range(nct):
            rs = slice(c * CHUNK, (c + 1) * CHUNK)
            for hd in range(N_HEADS):
                cs = slice(hd * HEAD_DIM, (hd + 1) * HEAD_DIM)
                q, k, v = q_ref[rs, cs], k_ref[rs, cs], v_ref[rs, cs]
                st = state[hd]
                st_bf = st.astype(BF16)
                st_ref[c, hd] = st_bf
                s = lax.dot_general(q, k, NT, preferred_element_type=F32) * dec_ref[hd]
                inner = jnp.dot(s.astype(BF16), v, preferred_element_type=F32)
                qx = (q.astype(F32) * xi_ref[hd]).astype(BF16)
                obuf[rs, cs] = inner + jnp.dot(qx, st_bf, preferred_element_type=F32)
                kz = (k.astype(F32) * zeta_ref[hd]).astype(BF16)
                state[hd] = CHUNK_DECAY[hd] * st + lax.dot_general(kz, v, TN, preferred_element_type=F32)

        o = obuf[...]
        o_ref[...] = o.astype(BF16)
        yh, _ = _group_norm(o)
        rg = rg_ref[...].astype(F32)
        ro = (yh * gr_ref[...] * (rg * _sigmoid(rg))).astype(BF16)
        ro_ref[...] = ro
        roT_ref[...] = ro.T
        h2 = (
            jnp.where(i == 0, mt_ref[...], x_ref[...])
            + jnp.dot(co_ref[...], wo_ref[0:D_CONV], preferred_element_type=F32)
            + jnp.dot(ro, wo_ref[D_CONV:], preferred_element_type=F32)
        )
        r2 = lax.rsqrt(jnp.mean(h2 * h2, axis=-1, keepdims=True) + EPS)
        yn = h2 * r2
        gfv = gf_ref[...]
        err = jnp.where(i > 0, yn * gfv - t_ref[...], 0.0)
        tile_loss = jnp.sum(jnp.sum(err * err, axis=-1, keepdims=True), axis=0, keepdims=True) * (0.5 / D_MODEL)
        dy = err * (1.0 / D_MODEL)
        acc_ref[0:1, :] += jnp.sum(dy * yn, axis=0, keepdims=True)
        acc_ref[1:2, :] += tile_loss
        dyn = dy * gfv
        dh2_ref[...] = r2 * (dyn - yn * jnp.mean(dyn * yn, axis=-1, keepdims=True))

    row = lambda w, j=0: pl.BlockSpec((tm, w), lambda i: (i, j))
    tok = pl.BlockSpec((tm, D_MODEL), lambda i: (_token_tile(i), 0))
    return pl.pallas_call(
        body,
        name="f2_retention_out",
        grid=(nt,),
        in_specs=[row(BLK, 4), row(BLK, 5), row(BLK, 6), row(BLK, 7), row(D_CONV), tok, _VMEM, tok] + [_VMEM] * 6,
        out_specs=(
            row(D_MODEL),
            row(D_RET),
            row(D_RET),
            pl.BlockSpec((D_RET, tm), lambda i: (0, i)),
            pl.BlockSpec((nct, N_HEADS, HEAD_DIM, HEAD_DIM), lambda i: (i, 0, 0, 0)),
            pl.BlockSpec((8, D_MODEL), lambda i: (0, 0)),
        ),
        out_shape=(
            jax.ShapeDtypeStruct((n_rows, D_MODEL), F32),
            jax.ShapeDtypeStruct((n_rows, D_RET), BF16),
            jax.ShapeDtypeStruct((n_rows, D_RET), BF16),
            jax.ShapeDtypeStruct((D_RET, n_rows), BF16),
            jax.ShapeDtypeStruct((n_rows // CHUNK, N_HEADS, HEAD_DIM, HEAD_DIM), BF16),
            jax.ShapeDtypeStruct((8, D_MODEL), F32),
        ),
        scratch_shapes=[pltpu.VMEM((N_HEADS, HEAD_DIM, HEAD_DIM), F32), pltpu.VMEM((tm, D_RET), F32)],
        compiler_params=_params(),
    )(proj, proj, proj, proj, conv_out, x2, meta_tile, t2, wout, gr, gf, dec, zeta, xi)


def _b1(dh2, proj, conv, o, states, wout, gr, cw8, dec, dect, zeta, xi, rot):
    n_rows = dh2.shape[0]
    tm = ROW_TILE
    nt = n_rows // tm
    nct = tm // CHUNK

    def body(dh2_ref, pr_ref, cv_ref, o_ref, st_ref, wo_ref, gr_ref, cw_ref, dec_ref, dect_ref, zeta_ref, xi_ref,
             ca_ref, sa_ref, cb_ref, sb_ref, dp_ref, acc_ref, dstate, halo, dobuf):
        i = pl.program_id(0)

        @pl.when(i == 0)
        def _():
            dstate[...] = jnp.zeros_like(dstate)
            halo[...] = jnp.zeros_like(halo)
            acc_ref[...] = jnp.zeros_like(acc_ref)

        dmix = lax.dot_general(dh2_ref[...].astype(BF16), wo_ref[...], NT, preferred_element_type=F32)

        dco = dmix[:, :D_CONV]
        cx = pr_ref[:, 0 * BLK : 1 * BLK].astype(F32)
        cb = pr_ref[:, 1 * BLK : 2 * BLK].astype(F32)
        cc = pr_ref[:, 2 * BLK : 3 * BLK].astype(F32)
        cg = pr_ref[:, 3 * BLK : 4 * BLK].astype(F32)
        conv = cv_ref[...].astype(F32)
        sg = _sigmoid(cg)
        silu, dsilu = cg * sg, sg * (1.0 + cg * (1.0 - sg))
        t = dco * cb
        dp_ref[:, 1 * BLK : 2 * BLK] = (dco * conv * silu).astype(BF16)
        dp_ref[:, 3 * BLK : 4 * BLK] = (t * conv * dsilu).astype(BF16)
        dconv = t * silu
        rows = lax.broadcasted_iota(jnp.int32, dconv.shape, 0)
        hl = halo[...]
        dc1 = jnp.where(rows == tm - 1, hl[0:1], pltpu.roll(dconv, tm - 1, 0))
        dc2 = jnp.where(rows == tm - 2, hl[0:1], jnp.where(rows == tm - 1, hl[1:2], pltpu.roll(dconv, tm - 2, 0)))
        halo[...] = dconv[0:8]
        cw = cw_ref[...]
        du = cw[2:3] * dconv + cw[1:2] * dc1 + cw[0:1] * dc2
        u = cc * cx
        acc_ref[0:1, :] += jnp.sum(u * dc2, axis=0, keepdims=True)
        acc_ref[1:2, :] += jnp.sum(u * dc1, axis=0, keepdims=True)
        acc_ref[2:3, :] += jnp.sum(u * dconv, axis=0, keepdims=True)
        dp_ref[:, 0 * BLK : 1 * BLK] = (du * cc).astype(BF16)
        dp_ref[:, 2 * BLK : 3 * BLK] = (du * cx).astype(BF16)

        dro = dmix[:, D_CONV:]
        rg = pr_ref[:, 7 * BLK : 8 * BLK].astype(F32)
        sg = _sigmoid(rg)
        silu, dsilu = rg * sg, sg * (1.0 + rg * (1.0 - sg))
        yh, rstd = _group_norm(o_ref[...].astype(F32))
        grv = gr_ref[...]
        dp_ref[:, 7 * BLK : 8 * BLK] = (dro * (yh * grv) * dsilu).astype(BF16)
        dret = dro * silu
        acc_ref[3:4, :] += jnp.sum(dret * yh, axis=0, keepdims=True)
        dyh = dret * grv
        for hd in range(N_HEADS):
            cs = slice(hd * HEAD_DIM, (hd + 1) * HEAD_DIM)
            a, b = dyh[:, cs], yh[:, cs]
            dobuf[:, cs] = rstd[:, cs] * (
                a - jnp.mean(a, axis=-1, keepdims=True) - b * jnp.mean(a * b, axis=-1, keepdims=True)
            )

        cos_all, sin_all = _tile_rotary(ca_ref, sa_ref, cb_ref, sb_ref, nt - 1 - i)
        for c in reversed(range(nct)):
            rs = slice(c * CHUNK, (c + 1) * CHUNK)
            cos_t, sin_t = cos_all[rs, :], sin_all[rs, :]
            for hd in range(N_HEADS):
                cs = slice(hd * HEAD_DIM, (hd + 1) * HEAD_DIM)
                q = pr_ref[rs, 4 * BLK + hd * HEAD_DIM : 4 * BLK + (hd + 1) * HEAD_DIM]
                k = pr_ref[rs, 5 * BLK + hd * HEAD_DIM : 5 * BLK + (hd + 1) * HEAD_DIM]
                v = pr_ref[rs, 6 * BLK + hd * HEAD_DIM : 6 * BLK + (hd + 1) * HEAD_DIM]
                do = dobuf[rs, cs].astype(BF16)
                st_bf = st_ref[c, hd]
                dst = dstate[hd]
                dst_bf = dst.astype(BF16)
                zt, xt = zeta_ref[hd], xi_ref[hd]
                sT = (lax.dot_general(k, q, NT, preferred_element_type=F32) * dect_ref[hd]).astype(BF16)
                dsT = (lax.dot_general(v, do, NT, preferred_element_type=F32) * dect_ref[hd]).astype(BF16)
                ds = (lax.dot_general(do, v, NT, preferred_element_type=F32) * dec_ref[hd]).astype(BF16)
                kz = (k.astype(F32) * zt).astype(BF16)
                dv = jnp.dot(sT, do, preferred_element_type=F32) + jnp.dot(kz, dst_bf, preferred_element_type=F32)
                dq = jnp.dot(ds, k, preferred_element_type=F32) + xt * lax.dot_general(
                    do, st_bf, NT, preferred_element_type=F32
                )
                dk = jnp.dot(dsT, q, preferred_element_type=F32) + zt * lax.dot_general(
                    v, dst_bf, NT, preferred_element_type=F32
                )
                qx = (q.astype(F32) * xt).astype(BF16)
                dstate[hd] = CHUNK_DECAY[hd] * dst + lax.dot_general(qx, do, TN, preferred_element_type=F32)
                dp_ref[rs, 4 * BLK + hd * HEAD_DIM : 4 * BLK + (hd + 1) * HEAD_DIM] = _rot_bwd(
                    dq * Q_SCALE, cos_t, sin_t
                ).astype(BF16)
                dp_ref[rs, 5 * BLK + hd * HEAD_DIM : 5 * BLK + (hd + 1) * HEAD_DIM] = _rot_bwd(dk, cos_t, sin_t).astype(BF16)
                dp_ref[rs, 6 * BLK + hd * HEAD_DIM : 6 * BLK + (hd + 1) * HEAD_DIM] = dv.astype(BF16)

    rev = lambda w: pl.BlockSpec((tm, w), lambda i: (nt - 1 - i, 0))
    return pl.pallas_call(
        body,
        name="b1_dproj",
        grid=(nt,),
        in_specs=[
            rev(D_MODEL),
            rev(N_PROJ * BLK),
            rev(D_CONV),
            rev(D_RET),
            pl.BlockSpec((nct, N_HEADS, HEAD_DIM, HEAD_DIM), lambda i: (nt - 1 - i, 0, 0, 0)),
        ]
        + [_VMEM] * 11,
        out_specs=(rev(N_PROJ * BLK), pl.BlockSpec((8, D_CONV), lambda i: (0, 0))),
        out_shape=(jax.ShapeDtypeStruct((n_rows, N_PROJ * BLK), BF16), jax.ShapeDtypeStruct((8, D_CONV), F32)),
        scratch_shapes=[
            pltpu.VMEM((N_HEADS, HEAD_DIM, HEAD_DIM), F32),
            pltpu.VMEM((8, D_CONV), F32),
            pltpu.VMEM((tm, D_RET), F32),
        ],
        compiler_params=_params(),
    )(dh2, proj, conv, o, states, wout, gr, cw8, dec, dect, zeta, xi, *rot)


def _b2(dproj, w3, x2, meta_tile, dh2, g1):
    n_rows = dproj.shape[0]
    tm = ROW_TILE
    nt = n_rows // tm

    def body(dp_ref, w_ref, x_ref, mt_ref, dh2_ref, g_ref, gx_ref, dm_ref, acc_ref):
        i = pl.program_id(0)

        @pl.when(i == 0)
        def _():
            acc_ref[...] = jnp.zeros_like(acc_ref)

        dhn = lax.dot_general(dp_ref[:, 0:BLK], w_ref[0], NT, preferred_element_type=F32)
        for j in range(1, N_PROJ):
            dhn += lax.dot_general(dp_ref[:, j * BLK : (j + 1) * BLK], w_ref[j], NT, preferred_element_type=F32)
        h = jnp.where(i == 0, mt_ref[...], x_ref[...])
        r = lax.rsqrt(jnp.mean(h * h, axis=-1, keepdims=True) + EPS)
        hh = h * r
        acc_ref[0:1, :] += jnp.sum(dhn * hh, axis=0, keepdims=True)
        dg = dhn * g_ref[...]
        dh = dh2_ref[...] + r * (dg - hh * jnp.mean(dg * hh, axis=-1, keepdims=True))

        @pl.when(i == 0)
        def _():
            dm_ref[...] = dh[tm - N_META : tm]

        @pl.when(i > 0)
        def _():
            gx_ref[...] = dh

    row = lambda w: pl.BlockSpec((tm, w), lambda i: (i, 0))
    tok = pl.BlockSpec((tm, D_MODEL), lambda i: (_token_tile(i), 0))
    return pl.pallas_call(
        body,
        name="b2_dh",
        grid=(nt,),
        in_specs=[row(N_PROJ * BLK), _VMEM, tok, _VMEM, row(D_MODEL), _VMEM],
        out_specs=(tok, pl.BlockSpec((N_META, D_MODEL), lambda i: (0, 0)), pl.BlockSpec((8, D_MODEL), lambda i: (0, 0))),
        out_shape=(
            jax.ShapeDtypeStruct(x2.shape, F32),
            jax.ShapeDtypeStruct((N_META, D_MODEL), F32),
            jax.ShapeDtypeStruct((8, D_MODEL), F32),
        ),
        compiler_params=_params(),
    )(dproj, w3, x2, meta_tile, dh2, g1)


def _gw_in_scatter(hnT, dproj, gw_out_parts, sc_parts, ga_part, me_arr):
    n_rows = dproj.shape[0]
    last = N_DEV - 1
    by_dest = (True, True, False)

    def body(me_ref, a_ref, b_ref, go_ref, sc_ref, ga_ref, land_in, land_go, land_sc, land_ga,
             d2d_buf, d2d_land, ici_buf, d2d_send, d2d_recv, ici_send, ici_recv, send_sems, recv_sems, local_sems):
        del me_ref
        s = pl.program_id(0)
        t = last - s
        q = t >> 1
        x, y, c = lax.axis_index("x"), lax.axis_index("y"), lax.axis_index("c")
        me = 4 * x + 2 * y + c
        chip = 2 * x + y
        srcs, lands = (go_ref, sc_ref, ga_ref), (land_go, land_sc, land_ga)

        def peer_at(k):
            return (1 - x if k & 4 else x, 1 - y if k & 2 else y, 1 - c if k & 1 else c)

        def small_copy(a, k):
            px, py, pc = peer_at(k)
            return pltpu.make_async_remote_copy(
                src_ref=srcs[a].at[4 * px + 2 * py + pc] if by_dest[a] else srcs[a],
                dst_ref=lands[a].at[me],
                send_sem=send_sems.at[a * last + k - 1],
                recv_sem=recv_sems.at[a * last + k - 1],
                device_id=(px, py, pc),
                device_id_type=MESH,
            )

        def small_local(a):
            return pltpu.make_async_copy(srcs[a].at[me] if by_dest[a] else srcs[a], lands[a].at[me], local_sems.at[a + 1])

        def d2d_copy(j):
            return pltpu.make_async_remote_copy(
                src_ref=d2d_buf.at[j],
                dst_ref=d2d_land.at[j],
                send_sem=d2d_send.at[j],
                recv_sem=d2d_recv.at[j],
                device_id=(x, y, 1 - c),
                device_id_type=MESH,
            )

        def ici_copy(j, to):
            return pltpu.make_async_remote_copy(
                src_ref=ici_buf.at[j],
                dst_ref=land_in.at[chip],
                send_sem=ici_send.at[j],
                recv_sem=ici_recv.at[j],
                device_id=to,
                device_id_type=MESH,
            )

        def own_copy():
            return pltpu.make_async_copy(ici_buf.at[0], land_in.at[chip], local_sems.at[0])

        @pl.when(s == 0)
        def _():
            for a in range(3):
                small_local(a).start()
            for k in range(last, 0, -1):
                for a in range(3):
                    small_copy(a, k).start()

        blk = jnp.dot(a_ref[...], b_ref[...], preferred_element_type=F32)

        @pl.when((t & 1) == 1)
        def _():
            d2d_buf[q] = blk.astype(BF16)
            d2d_copy(q).start()

        @pl.when((t & 1) == 0)
        def _():
            d2d_copy(q).wait_recv()
            ici_buf[q] = (blk + d2d_land[q].astype(F32)).astype(BF16)

            @pl.when(t != 0)
            def _():
                ici_copy(q, (jnp.bitwise_xor(x, (t >> 2) & 1), jnp.bitwise_xor(y, (t >> 1) & 1), c)).start()

            @pl.when(t == 0)
            def _():
                own_copy().start()
                for j in range(N_CHIPS):
                    d2d_copy(j).wait_send()
                for j in range(N_CHIPS - 1, 0, -1):
                    ici_copy(j, peer_at(2 * j)).wait()
                for k in range(last, 0, -1):
                    for a in range(3):
                        small_copy(a, k).wait()
                for a in range(3):
                    small_local(a).wait()
                own_copy().wait()

    smalls = (gw_out_parts, sc_parts, ga_part)
    small_blocks = [p.shape[1:] if d else p.shape for p, d in zip(smalls, by_dest, strict=True)]
    grid_spec = pltpu.PrefetchScalarGridSpec(
        num_scalar_prefetch=1,
        grid=(N_DEV,),
        in_specs=[_VMEM, pl.BlockSpec((n_rows, BLK), lambda s, me: (0, jnp.bitwise_xor(me[0], last - s)))] + [_HBM] * 3,
        out_specs=tuple([_HBM] * 4),
        scratch_shapes=[
            pltpu.VMEM((N_CHIPS, D_MODEL, BLK), BF16),
            pltpu.VMEM((N_CHIPS, D_MODEL, BLK), BF16),
            pltpu.VMEM((N_CHIPS, D_MODEL, BLK), BF16),
            pltpu.SemaphoreType.DMA((N_CHIPS,)),
            pltpu.SemaphoreType.DMA((N_CHIPS,)),
            pltpu.SemaphoreType.DMA((N_CHIPS,)),
            pltpu.SemaphoreType.DMA((N_CHIPS,)),
            pltpu.SemaphoreType.DMA((3 * last,)),
            pltpu.SemaphoreType.DMA((3 * last,)),
            pltpu.SemaphoreType.DMA((4,)),
        ],
    )
    return pl.pallas_call(
        body,
        name="gw_in_scatter",
        grid_spec=grid_spec,
        out_shape=(jax.ShapeDtypeStruct((N_CHIPS, D_MODEL, BLK), BF16),)
        + tuple(jax.ShapeDtypeStruct((N_DEV, *b), p.dtype) for b, p in zip(small_blocks, smalls, strict=True)),
        compiler_params=_params(),
    )(me_arr, hnT, dproj, gw_out_parts, sc_parts, ga_part)


def _gw_out(coT, roT, dh2):
    n_rows = dh2.shape[0]
    tk = ROW_TILE
    nk = n_rows // tk

    def body(c_ref, r_ref, d_ref, out_ref, acc):
        k = pl.program_id(0)

        @pl.when(k == 0)
        def _():
            acc[...] = jnp.zeros_like(acc)

        d = d_ref[...].astype(BF16)
        acc[0:D_CONV, :] += jnp.dot(c_ref[...], d, preferred_element_type=F32)
        acc[D_CONV:, :] += jnp.dot(r_ref[...], d, preferred_element_type=F32)

        @pl.when(k == nk - 1)
        def _():
            out_ref[...] = acc[...].astype(BF16)

    return pl.pallas_call(
        body,
        name="gw_out",
        grid=(nk,),
        in_specs=[
            pl.BlockSpec((D_CONV, tk), lambda k: (0, k)),
            pl.BlockSpec((D_RET, tk), lambda k: (0, k)),
            pl.BlockSpec((tk, D_MODEL), lambda k: (k, 0)),
        ],
        out_specs=pl.BlockSpec((D_MODEL, D_MODEL), lambda k: (0, 0)),
        out_shape=jax.ShapeDtypeStruct((D_MODEL, D_MODEL), BF16),
        scratch_shapes=[pltpu.VMEM((D_MODEL, D_MODEL), F32)],
        compiler_params=_params(),
    )(coT, roT, dh2)


def _adamw(w, g, m, v):
    m = ADAM_B1 * m + (1.0 - ADAM_B1) * g
    v = ADAM_B2 * v + (1.0 - ADAM_B2) * (g * g)
    m_hat = m / (1.0 - ADAM_B1**ADAM_STEP)
    v_hat = v / (1.0 - ADAM_B2**ADAM_STEP)
    delta = -ADAM_LR * (m_hat / (jnp.sqrt(v_hat) + ADAM_EPS) + ADAM_WD * w)
    return delta, m, v


def _sum_adamw(name, parts, w, m, v, rows_per_step):
    n_r, n_c = w.shape
    n_parts = parts.shape[0]
    tr = rows_per_step

    def body(p_ref, w_ref, m_ref, v_ref, g_ref, d_ref, nm_ref, nv_ref):
        g = p_ref[0].astype(F32)
        for s in range(1, n_parts):
            g = g + p_ref[s].astype(F32)
        g_ref[...] = g
        d_ref[...], nm_ref[...], nv_ref[...] = _adamw(w_ref[...], g, m_ref[...], v_ref[...])

    blk = pl.BlockSpec((tr, n_c), lambda i: (i, 0))
    return pl.pallas_call(
        body,
        name=name,
        grid=(n_r // tr,),
        in_specs=[pl.BlockSpec((n_parts, tr, n_c), lambda i: (0, i, 0)), blk, blk, blk],
        out_specs=(blk,) * 4,
        out_shape=(jax.ShapeDtypeStruct((n_r, n_c), F32),) * 4,
        compiler_params=_params(),
    )(parts, w, m, v)


def _pack_sharded(meta_blk, conv_blk):
    p = jnp.zeros((PACK_ROWS, CHUNK), F32)
    p = p.at[SC_META0 : SC_META0 + N_META, :].set(meta_blk)
    return p.at[SC_CONV0 : SC_CONV0 + 3, : D_CONV // N_DEV].set(conv_blk)


def _unpack_sharded(p):
    return p[SC_META0 : SC_META0 + N_META, :], p[SC_CONV0 : SC_CONV0 + 3, : D_CONV // N_DEV]


def _pack_replicated(n1, rg, fg, loss=None):
    p = jnp.zeros((PACK_ROWS, CHUNK), F32)
    p = p.at[GA_N1 : GA_N1 + 8, :].set(n1.reshape(8, CHUNK))
    p = p.at[GA_RG : GA_RG + 4, :].set(rg.reshape(4, CHUNK))
    p = p.at[GA_FG : GA_FG + 8, :].set(fg.reshape(8, CHUNK))
    if loss is not None:
        p = p.at[GA_LOSS, :].set(loss)
    return p


def _unpack_replicated(p):
    return (
        p[GA_N1 : GA_N1 + 8, :].reshape(D_MODEL),
        p[GA_RG : GA_RG + 4, :].reshape(D_RET),
        p[GA_FG : GA_FG + 8, :].reshape(D_MODEL),
    )


def kernel(x, meta, norm1_g, w_in, conv_w, ret_norm_g, w_out, final_g, loss_target, m_meta, m_norm1_g, m_w_in, m_conv_w, m_ret_norm_g, m_w_out, m_final_g, v_meta, v_norm1_g, v_w_in, v_conv_w, v_ret_norm_g, v_w_out, v_final_g):
    seq = x.shape[1]
    assert x.shape == (1, seq, D_MODEL) and seq % ROW_TILE == 0
    n_tiles = seq // ROW_TILE + 1
    cols = D_CONV // N_DEV
    x2, t2 = x[0], loss_target[0]

    w3, wo3, small = _gather("gather_weights", [w_in.astype(BF16), w_out.astype(BF16), _pack_sharded(meta, conv_w)])
    wout = wo3.reshape(D_MODEL, D_MODEL)
    meta_full = small[:, SC_META0 : SC_META0 + N_META, :].transpose(1, 0, 2).reshape(N_META, D_MODEL)
    conv_full = small[:, SC_CONV0 : SC_CONV0 + 3, :cols].transpose(1, 0, 2).reshape(3, D_CONV)
    cw8 = jnp.pad(conv_full, ((0, 5), (0, 0)))
    meta_tile = jnp.pad(meta_full, ((PAD_ROWS, 0), (0, 0)))

    rot = _rotary_tables(n_tiles, ROW_TILE)
    dec, dect, zeta, xi = _decay_tables()
    g1 = norm1_g.reshape(1, D_MODEL)
    gr = ret_norm_g.reshape(1, D_RET)
    gf = final_g.reshape(1, D_MODEL)

    hnT, proj, conv_out, conv_outT, conv = _f1(x2, meta_tile, g1, w3, cw8, rot)
    dh2, o, _, ret_outT, states, acc_f2 = _f2(proj, conv_out, x2, meta_tile, t2, wout, gr, gf, dec, zeta, xi)
    dproj, acc_b1 = _b1(dh2, proj, conv, o, states, wout, gr, cw8, dec, dect, zeta, xi, rot)
    grad_x2, dmeta_rows, acc_b2 = _b2(dproj, w3, x2, meta_tile, dh2, g1)
    gw_out_parts = _gw_out(conv_outT, ret_outT, dh2).reshape(N_DEV, D_MODEL // N_DEV, D_MODEL)

    dmeta = dmeta_rows.reshape(N_META, N_DEV, CHUNK).transpose(1, 0, 2)
    dconv = jnp.pad(acc_b1[0:3, :].reshape(3, N_DEV, cols).transpose(1, 0, 2), ((0, 0), (0, 5), (0, CHUNK - cols)))
    sc_parts = jnp.concatenate([dmeta, dconv], axis=1)
    ga_part = _pack_replicated(acc_b2[0], acc_b1[3], acc_f2[0], acc_f2[1, 0:CHUNK])

    me_arr = (4 * lax.axis_index("x") + 2 * lax.axis_index("y") + lax.axis_index("c")).astype(jnp.int32).reshape(1)
    land_in, land_out, land_sc, land_ga = _gw_in_scatter(hnT, dproj, gw_out_parts, sc_parts, ga_part, me_arr)

    g_w_in, d_w_in, nm_w_in, nv_w_in = _sum_adamw("adamw_w_in", land_in, w_in, m_w_in, v_w_in, 256)
    g_w_out, d_w_out, nm_w_out, nv_w_out = _sum_adamw("adamw_w_out", land_out, w_out, m_w_out, v_w_out, 64)
    g_sc, d_sc, nm_sc, nv_sc = _sum_adamw(
        "adamw_small_sharded",
        land_sc,
        _pack_sharded(meta, conv_w),
        _pack_sharded(m_meta, m_conv_w),
        _pack_sharded(v_meta, v_conv_w),
        PACK_ROWS,
    )
    g_ga, d_ga, nm_ga, nv_ga = _sum_adamw(
        "adamw_small_replicated",
        land_ga,
        _pack_replicated(norm1_g, ret_norm_g, final_g),
        _pack_replicated(m_norm1_g, m_ret_norm_g, m_final_g),
        _pack_replicated(v_norm1_g, v_ret_norm_g, v_final_g),
        PACK_ROWS,
    )

    loss = g_ga[GA_LOSS, 0]
    grad_x = grad_x2.reshape(1, seq, D_MODEL)

    def leaves(w_in_leaf, w_out_leaf, sc, ga):
        meta_leaf, conv_leaf = _unpack_sharded(sc)
        n1_leaf, rg_leaf, fg_leaf = _unpack_replicated(ga)
        return (meta_leaf, n1_leaf, w_in_leaf, conv_leaf, rg_leaf, w_out_leaf, fg_leaf)

    return (
        loss,
        grad_x,
        *leaves(g_w_in, g_w_out, g_sc, g_ga),
        *leaves(d_w_in, d_w_out, d_sc, d_ga),
        *leaves(nm_w_in, nm_w_out, nm_sc, nm_ga),
        *leaves(nv_w_in, nv_w_out, nv_sc, nv_ga),
    )
```

```python
import functools
import math

import jax
import jax.numpy as jnp
import numpy as np
from jax import lax
from jax.experimental import pallas as pl
from jax.experimental.pallas import tpu as pltpu

F32 = jnp.float32
BF16 = jnp.bfloat16

N_DEV = 8
N_CHIPS = 4
D_MODEL = 1024
N_META = 16
CHUNK = 128
D_CONV = 512
D_RET = 512
N_HEADS = 4
HEAD_DIM = 128
N_PROJ = 8
BLK = 512
ROPE_BASE = 10000.0
EPS = 1e-6
Q_SCALE = HEAD_DIM ** -0.5
LOG_G = tuple(math.log(1.0 - 2.0 ** (-5.0 - h)) for h in range(N_HEADS))
CHUNK_DECAY = tuple(math.exp(CHUNK * lg) for lg in LOG_G)

ADAM_LR = 0.001
ADAM_B1 = 0.9
ADAM_B2 = 0.999
ADAM_EPS = 1e-08
ADAM_WD = 0.01
ADAM_STEP = 10

ROW_TILE = 512
PAD_ROWS = ROW_TILE - N_META
VMEM_LIMIT = 56 * 1024 * 1024

SC_META0, SC_CONV0 = 0, 16
GA_N1, GA_RG, GA_FG, GA_LOSS = 0, 8, 16, 24

NT = (((1,), (1,)), ((), ()))
TN = (((0,), (0,)), ((), ()))
MESH = pl.DeviceIdType.MESH

_VMEM = pl.BlockSpec(memory_space=pltpu.VMEM)
_HBM = pl.BlockSpec(memory_space=pltpu.HBM)


def _params(n_axes=1):
    return pltpu.CompilerParams(dimension_semantics=("arbitrary",) * n_axes, vmem_limit_bytes=VMEM_LIMIT)


def _sigmoid(x):
    return 0.5 * jnp.tanh(0.5 * x) + 0.5


def _decay_tables():
    idx = np.arange(CHUNK, dtype=np.float64)
    diff = idx[:, None] - idx[None, :]
    dec = np.stack([np.where(diff >= 0, np.exp(diff * lg), 0.0) for lg in LOG_G])
    zeta = np.stack([np.exp((CHUNK - 1 - idx) * lg) for lg in LOG_G])
    xi = np.stack([np.exp((idx + 1.0) * lg) for lg in LOG_G])
    ones = np.ones((1, 1, CHUNK))
    return (
        jnp.asarray(dec, F32),
        jnp.asarray(dec.transpose(0, 2, 1), F32),
        jnp.asarray(zeta[:, :, None] * ones, F32),
        jnp.asarray(xi[:, :, None] * ones, F32),
    )


def _rotary_tables(n_tiles, tm):
    half = HEAD_DIM // 2
    freqs = (1.0 / (np.float32(ROPE_BASE) ** (np.arange(half, dtype=np.float32) / np.float32(half)))).astype(np.float64)
    sign = np.concatenate([-np.ones(half), np.ones(half)])
    two = lambda a: np.concatenate([a, a], axis=1)
    base = two((np.arange(n_tiles, dtype=np.float64) * tm - PAD_ROWS)[:, None] * freqs[None, :])
    off = two(np.arange(tm, dtype=np.float64)[:, None] * freqs[None, :])
    as32 = lambda a: jnp.asarray(a, F32)
    return as32(np.cos(base)), as32(np.sin(base) * sign), as32(np.cos(off)), as32(np.sin(off) * sign)


def _tile_rotary(ca_ref, sa_ref, cb_ref, sb_ref, tile):
    ca, sa = ca_ref[pl.ds(tile, 1), :], sa_ref[pl.ds(tile, 1), :]
    cb, sb = cb_ref[...], sb_ref[...]
    return ca * cb - sa * sb, sa * cb + ca * sb


def _rot(t, cos2, sin2):
    return t * cos2 + pltpu.roll(t, HEAD_DIM // 2, 1) * sin2


def _rot_bwd(d, cos2, sin2):
    return d * cos2 + pltpu.roll(d * sin2, HEAD_DIM // 2, 1)


def _token_tile(i):
    return jnp.maximum(i - 1, 0)


def _gather(name, srcs):
    n = len(srcs)
    per = 7

    def body(*refs):
        src_refs, out_refs = refs[:n], refs[n : 2 * n]
        send_sems, recv_sems, local_sems = refs[2 * n :]
        x, y, c = lax.axis_index("x"), lax.axis_index("y"), lax.axis_index("c")
        sibling = (x, y, 1 - c)
        chips = [(1 - x, y), (x, 1 - y), (1 - x, 1 - y)]

        def slot(px, py, pc):
            return 4 * px + 2 * py + pc

        def copy(a, k, block, to, src=None):
            dst = out_refs[a].at[slot(*block)]
            return pltpu.make_async_remote_copy(
                src_ref=dst if src is None else src,
                dst_ref=dst,
                send_sem=send_sems.at[a * per + k],
                recv_sem=recv_sems.at[a * per + k],
                device_id=to,
                device_id_type=MESH,
            )

        me = (x, y, c)
        mine, first, passed = [], [], []
        for a in range(n):
            cp = pltpu.make_async_copy(src_refs[a], out_refs[a].at[slot(*me)], local_sems.at[a])
            cp.start()
            mine.append(cp)
        for j, chip in enumerate(chips):
            for a in range(n):
                cp = copy(a, 1 + j, me, (*chip, c), src=src_refs[a])
                cp.start()
                first.append(cp)
        for a in range(n):
            cp = copy(a, 0, me, sibling, src=src_refs[a])
            cp.start()
            first.append(cp)
        for j, chip in enumerate(chips):
            for a in range(n):
                copy(a, 1 + j, (*chip, c), me).wait_recv()
                cp = copy(a, 4 + j, (*chip, c), sibling)
                cp.start()
                passed.append(cp)
        for a in range(n):
            copy(a, 0, (x, y, 1 - c), me).wait_recv()
            for j, chip in enumerate(chips):
                copy(a, 4 + j, (*chip, 1 - c), me).wait_recv()
        for cp in first + passed:
            cp.wait_send()
        for cp in mine:
            cp.wait()

    return pl.pallas_call(
        body,
        name=name,
        out_shape=tuple(jax.ShapeDtypeStruct((N_DEV, *s.shape), s.dtype) for s in srcs),
        in_specs=[_HBM] * n,
        out_specs=tuple([_HBM] * n),
        scratch_shapes=[
            pltpu.SemaphoreType.DMA((n * per,)),
            pltpu.SemaphoreType.DMA((n * per,)),
            pltpu.SemaphoreType.DMA((n,)),
        ],
    )(*srcs)


def _f1(x2, meta_tile, g1, w3, cw8, rot):
    tm = ROW_TILE
    nt = x2.shape[0] // tm + 1
    n_rows = nt * tm

    def body(x_ref, mt_ref, g_ref, w_ref, cw_ref, ca_ref, sa_ref, cb_ref, sb_ref,
             hnT_ref, pr_ref, co_ref, coT_ref, cv_ref, halo):
        i = pl.program_id(0)

        @pl.when(i == 0)
        def _():
            halo[...] = jnp.zeros_like(halo)

        h = jnp.where(i == 0, mt_ref[...], x_ref[...])
        r = lax.rsqrt(jnp.mean(h * h, axis=-1, keepdims=True) + EPS)
        hn = (h * r * g_ref[...]).astype(BF16)
        hnT_ref[...] = hn.T

        def proj(j):
            return jnp.dot(hn, w_ref[j], preferred_element_type=F32)

        cx, cb, cc, cg = proj(0), proj(1), proj(2), proj(3)
        u = cc * cx
        rows = lax.broadcasted_iota(jnp.int32, u.shape, 0)
        hl = halo[...]
        u1 = jnp.where(rows == 0, hl[7:8], pltpu.roll(u, 1, 0))
        u2 = jnp.where(rows == 0, hl[6:7], jnp.where(rows == 1, hl[7:8], pltpu.roll(u, 2, 0)))
        halo[...] = u[tm - 8 : tm]
        cw = cw_ref[...]
        conv = cw[0:1] * u2 + cw[1:2] * u1 + cw[2:3] * u
        co = (cb * conv * (cg * _sigmoid(cg))).astype(BF16)
        co_ref[...] = co
        coT_ref[...] = co.T
        cv_ref[...] = conv.astype(BF16)
        pr_ref[:, 0 * BLK : 1 * BLK] = cx.astype(BF16)
        pr_ref[:, 1 * BLK : 2 * BLK] = cb.astype(BF16)
        pr_ref[:, 2 * BLK : 3 * BLK] = cc.astype(BF16)
        pr_ref[:, 3 * BLK : 4 * BLK] = cg.astype(BF16)
        cos_t, sin_t = _tile_rotary(ca_ref, sa_ref, cb_ref, sb_ref, i)
        q, k = proj(4), proj(5)
        for hd in range(N_HEADS):
            c0 = hd * HEAD_DIM
            pr_ref[:, 4 * BLK + c0 : 4 * BLK + c0 + HEAD_DIM] = (
                _rot(q[:, c0 : c0 + HEAD_DIM], cos_t, sin_t) * Q_SCALE
            ).astype(BF16)
            pr_ref[:, 5 * BLK + c0 : 5 * BLK + c0 + HEAD_DIM] = _rot(k[:, c0 : c0 + HEAD_DIM], cos_t, sin_t).astype(BF16)
        pr_ref[:, 6 * BLK : 7 * BLK] = proj(6).astype(BF16)
        pr_ref[:, 7 * BLK : 8 * BLK] = proj(7).astype(BF16)

    row = lambda w: pl.BlockSpec((tm, w), lambda i: (i, 0))
    col = lambda w: pl.BlockSpec((w, tm), lambda i: (0, i))
    return pl.pallas_call(
        body,
        name="f1_inproj_conv",
        grid=(nt,),
        in_specs=[pl.BlockSpec((tm, D_MODEL), lambda i: (_token_tile(i), 0))] + [_VMEM] * 8,
        out_specs=(col(D_MODEL), row(N_PROJ * BLK), row(D_CONV), col(D_CONV), row(D_CONV)),
        out_shape=(
            jax.ShapeDtypeStruct((D_MODEL, n_rows), BF16),
            jax.ShapeDtypeStruct((n_rows, N_PROJ * BLK), BF16),
            jax.ShapeDtypeStruct((n_rows, D_CONV), BF16),
            jax.ShapeDtypeStruct((D_CONV, n_rows), BF16),
            jax.ShapeDtypeStruct((n_rows, D_CONV), BF16),
        ),
        scratch_shapes=[pltpu.VMEM((8, D_CONV), F32)],
        compiler_params=_params(),
    )(x2, meta_tile, g1, w3, cw8, *rot)


def _group_norm(o):
    ys, rs = [], []
    for hd in range(N_HEADS):
        oh = o[:, hd * HEAD_DIM : (hd + 1) * HEAD_DIM]
        xc = oh - jnp.mean(oh, axis=-1, keepdims=True)
        rstd = lax.rsqrt(jnp.mean(xc * xc, axis=-1, keepdims=True) + EPS)
        ys.append(xc * rstd)
        rs.append(jnp.broadcast_to(rstd, oh.shape))
    return jnp.concatenate(ys, axis=1), jnp.concatenate(rs, axis=1)


def _f2(proj, conv_out, x2, meta_tile, t2, wout, gr, gf, dec, zeta, xi):
    n_rows = proj.shape[0]
    tm = ROW_TILE
    nt = n_rows // tm
    nct = tm // CHUNK

    def body(q_ref, k_ref, v_ref, rg_ref, co_ref, x_ref, mt_ref, t_ref, wo_ref, gr_ref, gf_ref, dec_ref, zeta_ref,
             xi_ref, dh2_ref, o_ref, ro_ref, roT_ref, st_ref, acc_ref, state, obuf):
        i = pl.program_id(0)

        @pl.when(i == 0)
        def _():
            state[...] = jnp.zeros_like(state)
            acc_ref[...] = jnp.zeros_like(acc_ref)

        for c in range(nct):
            rs = slice(c * CHUNK, (c + 1) * CHUNK)
            for hd in range(N_HEADS):
                cs = slice(hd * HEAD_DIM, (hd + 1) * HEAD_DIM)
                q, k, v = q_ref[rs, cs], k_ref[rs, cs], v_ref[rs, cs]
                st = state[hd]
                st_bf = st.astype(BF16)
                st_ref[c, hd] = st_bf
                s = lax.dot_general(q, k, NT, preferred_element_type=F32) * dec_ref[hd]
                inner = jnp.dot(s.astype(BF16), v, preferred_element_type=F32)
                qx = (q.astype(F32) * xi_ref[hd]).astype(BF16)
                obuf[rs, cs] = inner + jnp.dot(qx, st_bf, preferred_element_type=F32)
                kz = (k.astype(F32) * zeta_ref[hd]).astype(BF16)
                state[hd] = CHUNK_DECAY[hd] * st + lax.dot_general(kz, v, TN, preferred_element_type=F32)

        o = obuf[...]
        o_ref[...] = o.astype(BF16)
        yh, _ = _group_norm(o)
        rg = rg_ref[...].astype(F32)
        ro = (yh * gr_ref[...] * (rg * _sigmoid(rg))).astype(BF16)
        ro_ref[...] = ro
        roT_ref[...] = ro.T
        h2 = (
            jnp.where(i == 0, mt_ref[...], x_ref[...])
            + jnp.dot(co_ref[...], wo_ref[0:D_CONV], preferred_element_type=F32)
            + jnp.dot(ro, wo_ref[D_CONV:], preferred_element_type=F32)
        )
        r2 = lax.rsqrt(jnp.mean(h2 * h2, axis=-1, keepdims=True) + EPS)
        yn = h2 * r2
        gfv = gf_ref[...]
        err = jnp.where(i > 0, yn * gfv - t_ref[...], 0.0)
        tile_loss = jnp.sum(jnp.sum(err * err, axis=-1, keepdims=True), axis=0, keepdims=True) * (0.5 / D_MODEL)
        dy = err * (1.0 / D_MODEL)
        acc_ref[0:1, :] += jnp.sum(dy * yn, axis=0, keepdims=True)
        acc_ref[1:2, :] += tile_loss
        dyn = dy * gfv
        dh2_ref[...] = r2 * (dyn - yn * jnp.mean(dyn * yn, axis=-1, keepdims=True))

    row = lambda w, j=0: pl.BlockSpec((tm, w), lambda i: (i, j))
    tok = pl.BlockSpec((tm, D_MODEL), lambda i: (_token_tile(i), 0))
    return pl.pallas_call(
        body,
        name="f2_retention_out",
        grid=(nt,),
        in_specs=[row(BLK, 4), row(BLK, 5), row(BLK, 6), row(BLK, 7), row(D_CONV), tok, _VMEM, tok] + [_VMEM] * 6,
        out_specs=(
            row(D_MODEL),
            row(D_RET),
            row(D_RET),
            pl.BlockSpec((D_RET, tm), lambda i: (0, i)),
            pl.BlockSpec((nct, N_HEADS, HEAD_DIM, HEAD_DIM), lambda i: (i, 0, 0, 0)),
            pl.BlockSpec((8, D_MODEL), lambda i: (0, 0)),
        ),
        out_shape=(
            jax.ShapeDtypeStruct((n_rows, D_MODEL), F32),
            jax.ShapeDtypeStruct((n_rows, D_RET), BF16),
            jax.ShapeDtypeStruct((n_rows, D_RET), BF16),
            jax.ShapeDtypeStruct((D_RET, n_rows), BF16),
            jax.ShapeDtypeStruct((n_rows // CHUNK, N_HEADS, HEAD_DIM, HEAD_DIM), BF16),
            jax.ShapeDtypeStruct((8, D_MODEL), F32),
        ),
        scratch_shapes=[pltpu.VMEM((N_HEADS, HEAD_DIM, HEAD_DIM), F32), pltpu.VMEM((tm, D_RET), F32)],
        compiler_params=_params(),
    )(proj, proj, proj, proj, conv_out, x2, meta_tile, t2, wout, gr, gf, dec, zeta, xi)


def _b1(dh2, proj, conv, o, states, wout, gr, cw8, dec, dect, zeta, xi, rot):
    n_rows = dh2.shape[0]
    tm = ROW_TILE
    nt = n_rows // tm
    nct = tm // CHUNK

    def body(dh2_ref, pr_ref, cv_ref, o_ref, st_ref, wo_ref, gr_ref, cw_ref, dec_ref, dect_ref, zeta_ref, xi_ref,
             ca_ref, sa_ref, cb_ref, sb_ref, dp_ref, acc_ref, dstate, halo, dobuf):
        i = pl.program_id(0)

        @pl.when(i == 0)
        def _():
            dstate[...] = jnp.zeros_like(dstate)
            halo[...] = jnp.zeros_like(halo)
            acc_ref[...] = jnp.zeros_like(acc_ref)

        dmix = lax.dot_general(dh2_ref[...].astype(BF16), wo_ref[...], NT, preferred_element_type=F32)

        dco = dmix[:, :D_CONV]
        cx = pr_ref[:, 0 * BLK : 1 * BLK].astype(F32)
        cb = pr_ref[:, 1 * BLK : 2 * BLK].astype(F32)
        cc = pr_ref[:, 2 * BLK : 3 * BLK].astype(F32)
        cg = pr_ref[:, 3 * BLK : 4 * BLK].astype(F32)
        conv = cv_ref[...].astype(F32)
        sg = _sigmoid(cg)
        silu, dsilu = cg * sg, sg * (1.0 + cg * (1.0 - sg))
        t = dco * cb
        dp_ref[:, 1 * BLK : 2 * BLK] = (dco * conv * silu).astype(BF16)
        dp_ref[:, 3 * BLK : 4 * BLK] = (t * conv * dsilu).astype(BF16)
        dconv = t * silu
        rows = lax.broadcasted_iota(jnp.int32, dconv.shape, 0)
        hl = halo[...]
        dc1 = jnp.where(rows == tm - 1, hl[0:1], pltpu.roll(dconv, tm - 1, 0))
        dc2 = jnp.where(rows == tm - 2, hl[0:1], jnp.where(rows == tm - 1, hl[1:2], pltpu.roll(dconv, tm - 2, 0)))
        halo[...] = dconv[0:8]
        cw = cw_ref[...]
        du = cw[2:3] * dconv + cw[1:2] * dc1 + cw[0:1] * dc2
        u = cc * cx
        acc_ref[0:1, :] += jnp.sum(u * dc2, axis=0, keepdims=True)
        acc_ref[1:2, :] += jnp.sum(u * dc1, axis=0, keepdims=True)
        acc_ref[2:3, :] += jnp.sum(u * dconv, axis=0, keepdims=True)
        dp_ref[:, 0 * BLK : 1 * BLK] = (du * cc).astype(BF16)
        dp_ref[:, 2 * BLK : 3 * BLK] = (du * cx).astype(BF16)

        dro = dmix[:, D_CONV:]
        rg = pr_ref[:, 7 * BLK : 8 * BLK].astype(F32)
        sg = _sigmoid(rg)
        silu, dsilu = rg * sg, sg * (1.0 + rg * (1.0 - sg))
        yh, rstd = _group_norm(o_ref[...].astype(F32))
        grv = gr_ref[...]
        dp_ref[:, 7 * BLK : 8 * BLK] = (dro * (yh * grv) * dsilu).astype(BF16)
        dret = dro * silu
        acc_ref[3:4, :] += jnp.sum(dret * yh, axis=0, keepdims=True)
        dyh = dret * grv
        for hd in range(N_HEADS):
            cs = slice(hd * HEAD_DIM, (hd + 1) * HEAD_DIM)
            a, b = dyh[:, cs], yh[:, cs]
            dobuf[:, cs] = rstd[:, cs] * (
                a - jnp.mean(a, axis=-1, keepdims=True) - b * jnp.mean(a * b, axis=-1, keepdims=True)
            )

        cos_all, sin_all = _tile_rotary(ca_ref, sa_ref, cb_ref, sb_ref, nt - 1 - i)
        for c in reversed(range(nct)):
            rs = slice(c * CHUNK, (c + 1) * CHUNK)
            cos_t, sin_t = cos_all[rs, :], sin_all[rs, :]
            for hd in range(N_HEADS):
                cs = slice(hd * HEAD_DIM, (hd + 1) * HEAD_DIM)
                q = pr_ref[rs, 4 * BLK + hd * HEAD_DIM : 4 * BLK + (hd + 1) * HEAD_DIM]
                k = pr_ref[rs, 5 * BLK + hd * HEAD_DIM : 5 * BLK + (hd + 1) * HEAD_DIM]
                v = pr_ref[rs, 6 * BLK + hd * HEAD_DIM : 6 * BLK + (hd + 1) * HEAD_DIM]
                do = dobuf[rs, cs].astype(BF16)
                st_bf = st_ref[c, hd]
                dst = dstate[hd]
                dst_bf = dst.astype(BF16)
                zt, xt = zeta_ref[hd], xi_ref[hd]
                sT = (lax.dot_general(k, q, NT, preferred_element_type=F32) * dect_ref[hd]).astype(BF16)
                dsT = (lax.dot_general(v, do, NT, preferred_element_type=F32) * dect_ref[hd]).astype(BF16)
                ds = (lax.dot_general(do, v, NT, preferred_element_type=F32) * dec_ref[hd]).astype(BF16)
                kz = (k.astype(F32) * zt).astype(BF16)
                dv = jnp.dot(sT, do, preferred_element_type=F32) + jnp.dot(kz, dst_bf, preferred_element_type=F32)
                dq = jnp.dot(ds, k, preferred_element_type=F32) + xt * lax.dot_general(
                    do, st_bf, NT, preferred_element_type=F32
                )
                dk = jnp.dot(dsT, q, preferred_element_type=F32) + zt * lax.dot_general(
                    v, dst_bf, NT, preferred_element_type=F32
                )
                qx = (q.astype(F32) * xt).astype(BF16)
                dstate[hd] = CHUNK_DECAY[hd] * dst + lax.dot_general(qx, do, TN, preferred_element_type=F32)
                dp_ref[rs, 4 * BLK + hd * HEAD_DIM : 4 * BLK + (hd + 1) * HEAD_DIM] = _rot_bwd(
                    dq * Q_SCALE, cos_t, sin_t
                ).astype(BF16)
                dp_ref[rs, 5 * BLK + hd * HEAD_DIM : 5 * BLK + (hd + 1) * HEAD_DIM] = _rot_bwd(dk, cos_t, sin_t).astype(BF16)
                dp_ref[rs, 6 * BLK + hd * HEAD_DIM : 6 * BLK + (hd + 1) * HEAD_DIM] = dv.astype(BF16)

    rev = lambda w: pl.BlockSpec((tm, w), lambda i: (nt - 1 - i, 0))
    return pl.pallas_call(
        body,
        name="b1_dproj",
        grid=(nt,),
        in_specs=[
            rev(D_MODEL),
            rev(N_PROJ * BLK),
            rev(D_CONV),
            rev(D_RET),
            pl.BlockSpec((nct, N_HEADS, HEAD_DIM, HEAD_DIM), lambda i: (nt - 1 - i, 0, 0, 0)),
        ]
        + [_VMEM] * 11,
        out_specs=(rev(N_PROJ * BLK), pl.BlockSpec((8, D_CONV), lambda i: (0, 0))),
        out_shape=(jax.ShapeDtypeStruct((n_rows, N_PROJ * BLK), BF16), jax.ShapeDtypeStruct((8, D_CONV), F32)),
        scratch_shapes=[
            pltpu.VMEM((N_HEADS, HEAD_DIM, HEAD_DIM), F32),
            pltpu.VMEM((8, D_CONV), F32),
            pltpu.VMEM((tm, D_RET), F32),
        ],
        compiler_params=_params(),
    )(dh2, proj, conv, o, states, wout, gr, cw8, dec, dect, zeta, xi, *rot)


def _b2(dproj, w3, x2, meta_tile, dh2, g1):
    n_rows = dproj.shape[0]
    tm = ROW_TILE
    nt = n_rows // tm

    def body(dp_ref, w_ref, x_ref, mt_ref, dh2_ref, g_ref, gx_ref, dm_ref, acc_ref):
        i = pl.program_id(0)

        @pl.when(i == 0)
        def _():
            acc_ref[...] = jnp.zeros_like(acc_ref)

        dhn = lax.dot_general(dp_ref[:, 0:BLK], w_ref[0], NT, preferred_element_type=F32)
        for j in range(1, N_PROJ):
            dhn += lax.dot_general(dp_ref[:, j * BLK : (j + 1) * BLK], w_ref[j], NT, preferred_element_type=F32)
        h = jnp.where(i == 0, mt_ref[...], x_ref[...])
        r = lax.rsqrt(jnp.mean(h * h, axis=-1, keepdims=True) + EPS)
        hh = h * r
        acc_ref[0:1, :] += jnp.sum(dhn * hh, axis=0, keepdims=True)
        dg = dhn * g_ref[...]
        dh = dh2_ref[...] + r * (dg - hh * jnp.mean(dg * hh, axis=-1, keepdims=True))

        @pl.when(i == 0)
        def _():
            dm_ref[...] = dh[tm - N_META : tm]

        @pl.when(i > 0)
        def _():
            gx_ref[...] = dh

    row = lambda w: pl.BlockSpec((tm, w), lambda i: (i, 0))
    tok = pl.BlockSpec((tm, D_MODEL), lambda i: (_token_tile(i), 0))
    return pl.pallas_call(
        body,
        name="b2_dh",
        grid=(nt,),
        in_specs=[row(N_PROJ * BLK), _VMEM, tok, _VMEM, row(D_MODEL), _VMEM],
        out_specs=(tok, pl.BlockSpec((N_META, D_MODEL), lambda i: (0, 0)), pl.BlockSpec((8, D_MODEL), lambda i: (0, 0))),
        out_shape=(
            jax.ShapeDtypeStruct(x2.shape, F32),
            jax.ShapeDtypeStruct((N_META, D_MODEL), F32),
            jax.ShapeDtypeStruct((8, D_MODEL), F32),
        ),
        compiler_params=_params(),
    )(dproj, w3, x2, meta_tile, dh2, g1)


def _gw_in_scatter(hnT, dproj, gw_out_parts, sc_parts, ga_part, me_arr):
    n_rows = dproj.shape[0]
    last = N_DEV - 1
    by_dest = (True, True, False)

    def body(me_ref, a_ref, b_ref, go_ref, sc_ref, ga_ref, land_in, land_go, land_sc, land_ga,
             d2d_buf, d2d_land, ici_buf, d2d_send, d2d_recv, ici_send, ici_recv, send_sems, recv_sems, local_sems):
        del me_ref
        s = pl.program_id(0)
        t = last - s
        q = t >> 1
        x, y, c = lax.axis_index("x"), lax.axis_index("y"), lax.axis_index("c")
        me = 4 * x + 2 * y + c
        chip = 2 * x + y
        srcs, lands = (go_ref, sc_ref, ga_ref), (land_go, land_sc, land_ga)

        def peer_at(k):
            return (1 - x if k & 4 else x, 1 - y if k & 2 else y, 1 - c if k & 1 else c)

        def small_copy(a, k):
            px, py, pc = peer_at(k)
            return pltpu.make_async_remote_copy(
                src_ref=srcs[a].at[4 * px + 2 * py + pc] if by_dest[a] else srcs[a],
                dst_ref=lands[a].at[me],
                send_sem=send_sems.at[a * last + k - 1],
                recv_sem=recv_sems.at[a * last + k - 1],
                device_id=(px, py, pc),
                device_id_type=MESH,
            )

        def small_local(a):
            return pltpu.make_async_copy(srcs[a].at[me] if by_dest[a] else srcs[a], lands[a].at[me], local_sems.at[a + 1])

        def d2d_copy(j):
            return pltpu.make_async_remote_copy(
                src_ref=d2d_buf.at[j],
                dst_ref=d2d_land.at[j],
                send_sem=d2d_send.at[j],
                recv_sem=d2d_recv.at[j],
                device_id=(x, y, 1 - c),
                device_id_type=MESH,
            )

        def ici_copy(j, to):
            return pltpu.make_async_remote_copy(
                src_ref=ici_buf.at[j],
                dst_ref=land_in.at[chip],
                send_sem=ici_send.at[j],
                recv_sem=ici_recv.at[j],
                device_id=to,
                device_id_type=MESH,
            )

        def own_copy():
            return pltpu.make_async_copy(ici_buf.at[0], land_in.at[chip], local_sems.at[0])

        @pl.when(s == 0)
        def _():
            for a in range(3):
                small_local(a).start()
            for k in range(last, 0, -1):
                for a in range(3):
                    small_copy(a, k).start()

        blk = jnp.dot(a_ref[...], b_ref[...], preferred_element_type=F32)

        @pl.when((t & 1) == 1)
        def _():
            d2d_buf[q] = blk.astype(BF16)
            d2d_copy(q).start()

        @pl.when((t & 1) == 0)
        def _():
            d2d_copy(q).wait_recv()
            ici_buf[q] = (blk + d2d_land[q].astype(F32)).astype(BF16)

            @pl.when(t != 0)
            def _():
                ici_copy(q, (jnp.bitwise_xor(x, (t >> 2) & 1), jnp.bitwise_xor(y, (t >> 1) & 1), c)).start()

            @pl.when(t == 0)
            def _():
                own_copy().start()
                for j in range(N_CHIPS):
                    d2d_copy(j).wait_send()
                for j in range(N_CHIPS - 1, 0, -1):
                    ici_copy(j, peer_at(2 * j)).wait()
                for k in range(last, 0, -1):
                    for a in range(3):
                        small_copy(a, k).wait()
                for a in range(3):
                    small_local(a).wait()
                own_copy().wait()

    smalls = (gw_out_parts, sc_parts, ga_part)
    small_blocks = [p.shape[1:] if d else p.shape for p, d in zip(smalls, by_dest, strict=True)]
    grid_spec = pltpu.PrefetchScalarGridSpec(
        num_scalar_prefetch=1,
        grid=(N_DEV,),
        in_specs=[_VMEM, pl.BlockSpec((n_rows, BLK), lambda s, me: (0, jnp.bitwise_xor(me[0], last - s)))] + [_HBM] * 3,
        out_specs=tuple([_HBM] * 4),
        scratch_shapes=[
            pltpu.VMEM((N_CHIPS, D_MODEL, BLK), BF16),
            pltpu.VMEM((N_CHIPS, D_MODEL, BLK), BF16),
            pltpu.VMEM((N_CHIPS, D_MODEL, BLK), BF16),
            pltpu.SemaphoreType.DMA((N_CHIPS,)),
            pltpu.SemaphoreType.DMA((N_CHIPS,)),
            pltpu.SemaphoreType.DMA((N_CHIPS,)),
            pltpu.SemaphoreType.DMA((N_CHIPS,)),
            pltpu.SemaphoreType.DMA((3 * last,)),
            pltpu.SemaphoreType.DMA((3 * last,)),
            pltpu.SemaphoreType.DMA((4,)),
        ],
    )
    return pl.pallas_call(
        body,
        name="gw_in_scatter",
        grid_spec=grid_spec,
        out_shape=(jax.ShapeDtypeStruct((N_CHIPS, D_MODEL, BLK), BF16),)
        + tuple(jax.ShapeDtypeStruct((N_DEV, *b), p.dtype) for b, p in zip(small_blocks, smalls, strict=True)),
        compiler_params=_params(),
    )(me_arr, hnT, dproj, gw_out_parts, sc_parts, ga_part)


def _gw_out(coT, roT, dh2):
    n_rows = dh2.shape[0]
    tk = ROW_TILE
    nk = n_rows // tk

    def body(c_ref, r_ref, d_ref, out_ref, acc):
        k = pl.program_id(0)

        @pl.when(k == 0)
        def _():
            acc[...] = jnp.zeros_like(acc)

        d = d_ref[...].astype(BF16)
        acc[0:D_CONV, :] += jnp.dot(c_ref[...], d, preferred_element_type=F32)
        acc[D_CONV:, :] += jnp.dot(r_ref[...], d, preferred_element_type=F32)

        @pl.when(k == nk - 1)
        def _():
            out_ref[...] = acc[...].astype(BF16)

    return pl.pallas_call(
        body,
        name="gw_out",
        grid=(nk,),
        in_specs=[
            pl.BlockSpec((D_CONV, tk), lambda k: (0, k)),
            pl.BlockSpec((D_RET, tk), lambda k: (0, k)),
            pl.BlockSpec((tk, D_MODEL), lambda k: (k, 0)),
        ],
        out_specs=pl.BlockSpec((D_MODEL, D_MODEL), lambda k: (0, 0)),
        out_shape=jax.ShapeDtypeStruct((D_MODEL, D_MODEL), BF16),
        scratch_shapes=[pltpu.VMEM((D_MODEL, D_MODEL), F32)],
        compiler_params=_params(),
    )(coT, roT, dh2)


def _adamw(w, g, m, v):
    m = ADAM_B1 * m + (1.0 - ADAM_B1) * g
    v = ADAM_B2 * v + (1.0 - ADAM_B2) * (g * g)
    m_hat = m / (1.0 - ADAM_B1**ADAM_STEP)
    v_hat = v / (1.0 - ADAM_B2**ADAM_STEP)
    delta = -ADAM_LR * (m_hat / (jnp.sqrt(v_hat) + ADAM_EPS) + ADAM_WD * w)
    return delta, m, v


def _sum_adamw(name, parts, w, m, v, rows_per_step):
    n_r, n_c = w.shape
    n_parts = parts.shape[0]
    tr = rows_per_step

    def body(p_ref, w_ref, m_ref, v_ref, g_ref, d_ref, nm_ref, nv_ref):
        g = p_ref[0].astype(F32)
        for s in range(1, n_parts):
            g = g + p_ref[s].astype(F32)
        g_ref[...] = g
        d_ref[...], nm_ref[...], nv_ref[...] = _adamw(w_ref[...], g, m_ref[...], v_ref[...])

    blk = pl.BlockSpec((tr, n_c), lambda i: (i, 0))
    return pl.pallas_call(
        body,
        name=name,
        grid=(n_r // tr,),
        in_specs=[pl.BlockSpec((n_parts, tr, n_c), lambda i: (0, i, 0)), blk, blk, blk],
        out_specs=(blk,) * 4,
        out_shape=(jax.ShapeDtypeStruct((n_r, n_c), F32),) * 4,
        compiler_params=_params(),
    )(parts, w, m, v)


def _small_leaves(meta, conv_w, n1, rg, fg):
    cols = D_CONV // N_DEV
    return (
        meta,
        jnp.pad(conv_w, ((0, 5), (0, CHUNK - cols))),
        n1.reshape(8, CHUNK),
        rg.reshape(4, CHUNK),
        fg.reshape(8, CHUNK),
    )


def _from_small_leaves(meta, conv8, n1, rg, fg):
    return meta, conv8[0:3, : D_CONV // N_DEV], n1.reshape(D_MODEL), rg.reshape(D_RET), fg.reshape(D_MODEL)


def _adamw_small(land_sc, land_ga, w, m, v):
    n_leaf = 5

    def body(sc_ref, ga_ref, *refs):
        ins, outs = refs[: 3 * n_leaf], refs[3 * n_leaf :]
        sc, ga = sc_ref[0], ga_ref[0]
        for s in range(1, N_DEV):
            sc = sc + sc_ref[s]
            ga = ga + ga_ref[s]
        grads = (
            sc[SC_META0 : SC_META0 + N_META],
            sc[SC_CONV0 : SC_CONV0 + 8],
            ga[GA_N1 : GA_N1 + 8],
            ga[GA_RG : GA_RG + 4],
            ga[GA_FG : GA_FG + 8],
        )
        for leaf, g in enumerate(grads):
            d, nm, nv = _adamw(ins[leaf][...], g, ins[n_leaf + leaf][...], ins[2 * n_leaf + leaf][...])
            outs[leaf][...] = g
            outs[n_leaf + leaf][...] = d
            outs[2 * n_leaf + leaf][...] = nm
            outs[3 * n_leaf + leaf][...] = nv
        outs[4 * n_leaf][...] = ga[GA_LOSS : GA_LOSS + 1]

    leaf_shapes = tuple(jax.ShapeDtypeStruct(a.shape, F32) for a in w)
    out = pl.pallas_call(
        body,
        name="adamw_small",
        out_shape=leaf_shapes * 4 + (jax.ShapeDtypeStruct((1, CHUNK), F32),),
    )(land_sc, land_ga, *w, *m, *v)
    return tuple(out[k * n_leaf : (k + 1) * n_leaf] for k in range(4)), out[4 * n_leaf]


def kernel(x, meta, norm1_g, w_in, conv_w, ret_norm_g, w_out, final_g, loss_target, m_meta, m_norm1_g, m_w_in, m_conv_w, m_ret_norm_g, m_w_out, m_final_g, v_meta, v_norm1_g, v_w_in, v_conv_w, v_ret_norm_g, v_w_out, v_final_g):
    seq = x.shape[1]
    assert x.shape == (1, seq, D_MODEL) and seq % ROW_TILE == 0
    n_tiles = seq // ROW_TILE + 1
    cols = D_CONV // N_DEV
    x2, t2 = x[0], loss_target[0]

    small_w = _small_leaves(meta, conv_w, norm1_g, ret_norm_g, final_g)
    w3, wo3, small = _gather(
        "gather_weights", [w_in.astype(BF16), w_out.astype(BF16), jnp.concatenate(small_w[0:2], axis=0)]
    )
    wout = wo3.reshape(D_MODEL, D_MODEL)
    meta_full = small[:, SC_META0 : SC_META0 + N_META, :].transpose(1, 0, 2).reshape(N_META, D_MODEL)
    conv_full = small[:, SC_CONV0 : SC_CONV0 + 3, :cols].transpose(1, 0, 2).reshape(3, D_CONV)
    cw8 = jnp.pad(conv_full, ((0, 5), (0, 0)))
    meta_tile = jnp.pad(meta_full, ((PAD_ROWS, 0), (0, 0)))

    rot = _rotary_tables(n_tiles, ROW_TILE)
    dec, dect, zeta, xi = _decay_tables()
    g1 = norm1_g.reshape(1, D_MODEL)
    gr = ret_norm_g.reshape(1, D_RET)
    gf = final_g.reshape(1, D_MODEL)

    hnT, proj, conv_out, conv_outT, conv = _f1(x2, meta_tile, g1, w3, cw8, rot)
    dh2, o, _, ret_outT, states, acc_f2 = _f2(proj, conv_out, x2, meta_tile, t2, wout, gr, gf, dec, zeta, xi)
    dproj, acc_b1 = _b1(dh2, proj, conv, o, states, wout, gr, cw8, dec, dect, zeta, xi, rot)
    grad_x2, dmeta_rows, acc_b2 = _b2(dproj, w3, x2, meta_tile, dh2, g1)
    gw_out_parts = _gw_out(conv_outT, ret_outT, dh2).reshape(N_DEV, D_MODEL // N_DEV, D_MODEL)

    dmeta = dmeta_rows.reshape(N_META, N_DEV, CHUNK).transpose(1, 0, 2)
    dconv = jnp.pad(acc_b1[0:3, :].reshape(3, N_DEV, cols).transpose(1, 0, 2), ((0, 0), (0, 5), (0, CHUNK - cols)))
    sc_parts = jnp.concatenate([dmeta, dconv], axis=1)
    ga_part = jnp.concatenate(
        [
            acc_b2[0].reshape(8, CHUNK),
            jnp.pad(acc_b1[3].reshape(4, CHUNK), ((0, 4), (0, 0))),
            acc_f2[0].reshape(8, CHUNK),
            jnp.pad(acc_f2[1:2, 0:CHUNK], ((0, 7), (0, 0))),
        ],
        axis=0,
    )

    me_arr = (4 * lax.axis_index("x") + 2 * lax.axis_index("y") + lax.axis_index("c")).astype(jnp.int32).reshape(1)
    land_in, land_out, land_sc, land_ga = _gw_in_scatter(hnT, dproj, gw_out_parts, sc_parts, ga_part, me_arr)

    g_w_in, d_w_in, nm_w_in, nv_w_in = _sum_adamw("adamw_w_in", land_in, w_in, m_w_in, v_w_in, 256)
    g_w_out, d_w_out, nm_w_out, nv_w_out = _sum_adamw("adamw_w_out", land_out, w_out, m_w_out, v_w_out, 64)
    small_out, loss_row = _adamw_small(
        land_sc,
        land_ga,
        small_w,
        _small_leaves(m_meta, m_conv_w, m_norm1_g, m_ret_norm_g, m_final_g),
        _small_leaves(v_meta, v_conv_w, v_norm1_g, v_ret_norm_g, v_final_g),
    )
    loss = loss_row[0, 0]
    grad_x = grad_x2.reshape(1, seq, D_MODEL)

    def leaves(w_in_leaf, w_out_leaf, small_leaves):
        meta_leaf, conv_leaf, n1_leaf, rg_leaf, fg_leaf = _from_small_leaves(*small_leaves)
        return (meta_leaf, n1_leaf, w_in_leaf, conv_leaf, rg_leaf, w_out_leaf, fg_leaf)

    return (
        loss,
        grad_x,
        *leaves(g_w_in, g_w_out, small_out[0]),
        *leaves(d_w_in, d_w_out, small_out[1]),
        *leaves(nm_w_in, nm_w_out, small_out[2]),
        *leaves(nv_w_in, nv_w_out, small_out[3]),
    )
```

```python
import functools
import math

import jax
import jax.numpy as jnp
import numpy as np
from jax import lax
from jax.experimental import pallas as pl
from jax.experimental.pallas import tpu as pltpu

F32 = jnp.float32
BF16 = jnp.bfloat16

N_DEV = 8
N_CHIPS = 4
D_MODEL = 1024
N_META = 16
CHUNK = 128
D_CONV = 512
D_RET = 512
N_HEADS = 4
HEAD_DIM = 128
N_PROJ = 8
BLK = 512
ROPE_BASE = 10000.0
EPS = 1e-6
Q_SCALE = HEAD_DIM ** -0.5
LOG_G = tuple(math.log(1.0 - 2.0 ** (-5.0 - h)) for h in range(N_HEADS))
CHUNK_DECAY = tuple(math.exp(CHUNK * lg) for lg in LOG_G)

ADAM_LR = 0.001
ADAM_B1 = 0.9
ADAM_B2 = 0.999
ADAM_EPS = 1e-08
ADAM_WD = 0.01
ADAM_STEP = 10

ROW_TILE = 512
PAD_ROWS = ROW_TILE - N_META
VMEM_LIMIT = 56 * 1024 * 1024

SC_META0, SC_CONV0 = 0, 16
GA_N1, GA_RG, GA_FG, GA_LOSS = 0, 8, 16, 24

NT = (((1,), (1,)), ((), ()))
TN = (((0,), (0,)), ((), ()))
MESH = pl.DeviceIdType.MESH

_VMEM = pl.BlockSpec(memory_space=pltpu.VMEM)
_HBM = pl.BlockSpec(memory_space=pltpu.HBM)


def _params(n_axes=1):
    return pltpu.CompilerParams(dimension_semantics=("arbitrary",) * n_axes, vmem_limit_bytes=VMEM_LIMIT)


def _sigmoid(x):
    return 0.5 * jnp.tanh(0.5 * x) + 0.5


def _decay_tables():
    idx = np.arange(CHUNK, dtype=np.float64)
    diff = idx[:, None] - idx[None, :]
    dec = np.stack([np.where(diff >= 0, np.exp(diff * lg), 0.0) for lg in LOG_G])
    zeta = np.stack([np.exp((CHUNK - 1 - idx) * lg) for lg in LOG_G])
    xi = np.stack([np.exp((idx + 1.0) * lg) for lg in LOG_G])
    ones = np.ones((1, 1, CHUNK))
    return (
        jnp.asarray(dec, F32),
        jnp.asarray(dec.transpose(0, 2, 1), F32),
        jnp.asarray(zeta[:, :, None] * ones, F32),
        jnp.asarray(xi[:, :, None] * ones, F32),
    )


def _rotary_tables(n_tiles, tm):
    half = HEAD_DIM // 2
    freqs = (1.0 / (np.float32(ROPE_BASE) ** (np.arange(half, dtype=np.float32) / np.float32(half)))).astype(np.float64)
    sign = np.concatenate([-np.ones(half), np.ones(half)])
    two = lambda a: np.concatenate([a, a], axis=1)
    base = two((np.arange(n_tiles, dtype=np.float64) * tm - PAD_ROWS)[:, None] * freqs[None, :])
    off = two(np.arange(tm, dtype=np.float64)[:, None] * freqs[None, :])
    as32 = lambda a: jnp.asarray(a, F32)
    return as32(np.cos(base)), as32(np.sin(base) * sign), as32(np.cos(off)), as32(np.sin(off) * sign)


def _tile_rotary(ca_ref, sa_ref, cb_ref, sb_ref, tile):
    ca, sa = ca_ref[pl.ds(tile, 1), :], sa_ref[pl.ds(tile, 1), :]
    cb, sb = cb_ref[...], sb_ref[...]
    return ca * cb - sa * sb, sa * cb + ca * sb


def _rot(t, cos2, sin2):
    return t * cos2 + pltpu.roll(t, HEAD_DIM // 2, 1) * sin2


def _rot_bwd(d, cos2, sin2):
    return d * cos2 + pltpu.roll(d * sin2, HEAD_DIM // 2, 1)


def _token_tile(i):
    return jnp.maximum(i - 1, 0)


def _gather(name, srcs):
    n = len(srcs)
    per = 7

    def body(*refs):
        src_refs, out_refs = refs[:n], refs[n : 2 * n]
        send_sems, recv_sems, local_sems = refs[2 * n :]
        x, y, c = lax.axis_index("x"), lax.axis_index("y"), lax.axis_index("c")
        sibling = (x, y, 1 - c)
        chips = [(1 - x, y), (x, 1 - y), (1 - x, 1 - y)]

        def slot(px, py, pc):
            return 4 * px + 2 * py + pc

        def copy(a, k, block, to, src=None):
            dst = out_refs[a].at[slot(*block)]
            return pltpu.make_async_remote_copy(
                src_ref=dst if src is None else src,
                dst_ref=dst,
                send_sem=send_sems.at[a * per + k],
                recv_sem=recv_sems.at[a * per + k],
                device_id=to,
                device_id_type=MESH,
            )

        me = (x, y, c)
        mine, first, passed = [], [], []
        for a in range(n):
            cp = pltpu.make_async_copy(src_refs[a], out_refs[a].at[slot(*me)], local_sems.at[a])
            cp.start()
            mine.append(cp)
        for j, chip in enumerate(chips):
            for a in range(n):
                cp = copy(a, 1 + j, me, (*chip, c), src=src_refs[a])
                cp.start()
                first.append(cp)
        for a in range(n):
            cp = copy(a, 0, me, sibling, src=src_refs[a])
            cp.start()
            first.append(cp)
        for j, chip in enumerate(chips):
            for a in range(n):
                copy(a, 1 + j, (*chip, c), me).wait_recv()
                cp = copy(a, 4 + j, (*chip, c), sibling)
                cp.start()
                passed.append(cp)
        for a in range(n):
            copy(a, 0, (x, y, 1 - c), me).wait_recv()
            for j, chip in enumerate(chips):
                copy(a, 4 + j, (*chip, 1 - c), me).wait_recv()
        for cp in first + passed:
            cp.wait_send()
        for cp in mine:
            cp.wait()

    return pl.pallas_call(
        body,
        name=name,
        out_shape=tuple(jax.ShapeDtypeStruct((N_DEV, *s.shape), s.dtype) for s in srcs),
        in_specs=[_HBM] * n,
        out_specs=tuple([_HBM] * n),
        scratch_shapes=[
            pltpu.SemaphoreType.DMA((n * per,)),
            pltpu.SemaphoreType.DMA((n * per,)),
            pltpu.SemaphoreType.DMA((n,)),
        ],
    )(*srcs)


def _f1(x2, meta_tile, g1, w3, cw8, rot):
    tm = ROW_TILE
    nt = x2.shape[0] // tm + 1
    n_rows = nt * tm

    def body(x_ref, mt_ref, g_ref, w_ref, cw_ref, ca_ref, sa_ref, cb_ref, sb_ref,
             hnT_ref, pr_ref, co_ref, coT_ref, cv_ref, halo):
        i = pl.program_id(0)

        @pl.when(i == 0)
        def _():
            halo[...] = jnp.zeros_like(halo)

        h = jnp.where(i == 0, mt_ref[...], x_ref[...])
        r = lax.rsqrt(jnp.mean(h * h, axis=-1, keepdims=True) + EPS)
        hn = (h * r * g_ref[...]).astype(BF16)
        hnT_ref[...] = hn.T

        def proj(j):
            return jnp.dot(hn, w_ref[j], preferred_element_type=F32)

        cx, cb, cc, cg = proj(0), proj(1), proj(2), proj(3)
        u = cc * cx
        rows = lax.broadcasted_iota(jnp.int32, u.shape, 0)
        hl = halo[...]
        u1 = jnp.where(rows == 0, hl[7:8], pltpu.roll(u, 1, 0))
        u2 = jnp.where(rows == 0, hl[6:7], jnp.where(rows == 1, hl[7:8], pltpu.roll(u, 2, 0)))
        halo[...] = u[tm - 8 : tm]
        cw = cw_ref[...]
        conv = cw[0:1] * u2 + cw[1:2] * u1 + cw[2:3] * u
        co = (cb * conv * (cg * _sigmoid(cg))).astype(BF16)
        co_ref[...] = co
        coT_ref[...] = co.T
        cv_ref[...] = conv.astype(BF16)
        pr_ref[:, 0 * BLK : 1 * BLK] = cx.astype(BF16)
        pr_ref[:, 1 * BLK : 2 * BLK] = cb.astype(BF16)
        pr_ref[:, 2 * BLK : 3 * BLK] = cc.astype(BF16)
        pr_ref[:, 3 * BLK : 4 * BLK] = cg.astype(BF16)
        cos_t, sin_t = _tile_rotary(ca_ref, sa_ref, cb_ref, sb_ref, i)
        q, k = proj(4), proj(5)
        for hd in range(N_HEADS):
            c0 = hd * HEAD_DIM
            pr_ref[:, 4 * BLK + c0 : 4 * BLK + c0 + HEAD_DIM] = (
                _rot(q[:, c0 : c0 + HEAD_DIM], cos_t, sin_t) * Q_SCALE
            ).astype(BF16)
            pr_ref[:, 5 * BLK + c0 : 5 * BLK + c0 + HEAD_DIM] = _rot(k[:, c0 : c0 + HEAD_DIM], cos_t, sin_t).astype(BF16)
        pr_ref[:, 6 * BLK : 7 * BLK] = proj(6).astype(BF16)
        pr_ref[:, 7 * BLK : 8 * BLK] = proj(7).astype(BF16)

    row = lambda w: pl.BlockSpec((tm, w), lambda i: (i, 0))
    col = lambda w: pl.BlockSpec((w, tm), lambda i: (0, i))
    return pl.pallas_call(
        body,
        name="f1_inproj_conv",
        grid=(nt,),
        in_specs=[pl.BlockSpec((tm, D_MODEL), lambda i: (_token_tile(i), 0))] + [_VMEM] * 8,
        out_specs=(col(D_MODEL), row(N_PROJ * BLK), row(D_CONV), col(D_CONV), row(D_CONV)),
        out_shape=(
            jax.ShapeDtypeStruct((D_MODEL, n_rows), BF16),
            jax.ShapeDtypeStruct((n_rows, N_PROJ * BLK), BF16),
            jax.ShapeDtypeStruct((n_rows, D_CONV), BF16),
            jax.ShapeDtypeStruct((D_CONV, n_rows), BF16),
            jax.ShapeDtypeStruct((n_rows, D_CONV), BF16),
        ),
        scratch_shapes=[pltpu.VMEM((8, D_CONV), F32)],
        compiler_params=_params(),
    )(x2, meta_tile, g1, w3, cw8, *rot)


def _group_norm(o):
    ys, rs = [], []
    for hd in range(N_HEADS):
        oh = o[:, hd * HEAD_DIM : (hd + 1) * HEAD_DIM]
        xc = oh - jnp.mean(oh, axis=-1, keepdims=True)
        rstd = lax.rsqrt(jnp.mean(xc * xc, axis=-1, keepdims=True) + EPS)
        ys.append(xc * rstd)
        rs.append(jnp.broadcast_to(rstd, oh.shape))
    return jnp.concatenate(ys, axis=1), jnp.concatenate(rs, axis=1)


def _f2(proj, conv_out, x2, meta_tile, t2, wout, gr, gf, dec, zeta, xi):
    n_rows = proj.shape[0]
    tm = ROW_TILE
    nt = n_rows // tm
    nct = tm // CHUNK

    def body(q_ref, k_ref, v_ref, rg_ref, co_ref, x_ref, mt_ref, t_ref, wo_ref, gr_ref, gf_ref, dec_ref, zeta_ref,
             xi_ref, dh2_ref, o_ref, ro_ref, roT_ref, st_ref, acc_ref, state, obuf):
        i = pl.program_id(0)

        @pl.when(i == 0)
        def _():
            state[...] = jnp.zeros_like(state)
            acc_ref[...] = jnp.zeros_like(acc_ref)

        for c in range(nct):
            rs = slice(c * CHUNK, (c + 1) * CHUNK)
            for hd in range(N_HEADS):
                cs = slice(hd * HEAD_DIM, (hd + 1) * HEAD_DIM)
                q, k, v = q_ref[rs, cs], k_ref[rs, cs], v_ref[rs, cs]
                st = state[hd]
                st_bf = st.astype(BF16)
                st_ref[c, hd] = st_bf
                s = lax.dot_general(q, k, NT, preferred_element_type=F32) * dec_ref[hd]
                inner = jnp.dot(s.astype(BF16), v, preferred_element_type=F32)
                qx = (q.astype(F32) * xi_ref[hd]).astype(BF16)
                obuf[rs, cs] = inner + jnp.dot(qx, st_bf, preferred_element_type=F32)
                kz = (k.astype(F32) * zeta_ref[hd]).astype(BF16)
                state[hd] = CHUNK_DECAY[hd] * st + lax.dot_general(kz, v, TN, preferred_element_type=F32)

        o = obuf[...]
        o_ref[...] = o.astype(BF16)
        yh, _ = _group_norm(o)
        rg = rg_ref[...].astype(F32)
        ro = (yh * gr_ref[...] * (rg * _sigmoid(rg))).astype(BF16)
        ro_ref[...] = ro
        roT_ref[...] = ro.T
        h2 = (
            jnp.where(i == 0, mt_ref[...], x_ref[...])
            + jnp.dot(co_ref[...], wo_ref[0:D_CONV], preferred_element_type=F32)
            + jnp.dot(ro, wo_ref[D_CONV:], preferred_element_type=F32)
        )
        r2 = lax.rsqrt(jnp.mean(h2 * h2, axis=-1, keepdims=True) + EPS)
        yn = h2 * r2
        gfv = gf_ref[...]
        err = jnp.where(i > 0, yn * gfv - t_ref[...], 0.0)
        tile_loss = jnp.sum(jnp.sum(err * err, axis=-1, keepdims=True), axis=0, keepdims=True) * (0.5 / D_MODEL)
        dy = err * (1.0 / D_MODEL)
        acc_ref[0:1, :] += jnp.sum(dy * yn, axis=0, keepdims=True)
        acc_ref[1:2, :] += tile_loss
        dyn = dy * gfv
        dh2_ref[...] = r2 * (dyn - yn * jnp.mean(dyn * yn, axis=-1, keepdims=True))

    row = lambda w, j=0: pl.BlockSpec((tm, w), lambda i: (i, j))
    tok = pl.BlockSpec((tm, D_MODEL), lambda i: (_token_tile(i), 0))
    return pl.pallas_call(
        body,
        name="f2_retention_out",
        grid=(nt,),
        in_specs=[row(BLK, 4), row(BLK, 5), row(BLK, 6), row(BLK, 7), row(D_CONV), tok, _VMEM, tok] + [_VMEM] * 6,
        out_specs=(
            row(D_MODEL),
            row(D_RET),
            row(D_RET),
            pl.BlockSpec((D_RET, tm), lambda i: (0, i)),
            pl.BlockSpec((nct, N_HEADS, HEAD_DIM, HEAD_DIM), lambda i: (i, 0, 0, 0)),
            pl.BlockSpec((8, D_MODEL), lambda i: (0, 0)),
        ),
        out_shape=(
            jax.ShapeDtypeStruct((n_rows, D_MODEL), F32),
            jax.ShapeDtypeStruct((n_rows, D_RET), BF16),
            jax.ShapeDtypeStruct((n_rows, D_RET), BF16),
            jax.ShapeDtypeStruct((D_RET, n_rows), BF16),
            jax.ShapeDtypeStruct((n_rows // CHUNK, N_HEADS, HEAD_DIM, HEAD_DIM), BF16),
            jax.ShapeDtypeStruct((8, D_MODEL), F32),
        ),
        scratch_shapes=[pltpu.VMEM((N_HEADS, HEAD_DIM, HEAD_DIM), F32), pltpu.VMEM((tm, D_RET), F32)],
        compiler_params=_params(),
    )(proj, proj, proj, proj, conv_out, x2, meta_tile, t2, wout, gr, gf, dec, zeta, xi)


def _b1(dh2, proj, o, states, wout, gr, dec, dect, zeta, xi, rot):
    n_rows = dh2.shape[0]
    tm = ROW_TILE
    nt = n_rows // tm
    nct = tm // CHUNK

    def body(dh2_ref, pr_ref, o_ref, st_ref, wo_ref, gr_ref, dec_ref, dect_ref, zeta_ref, xi_ref,
             ca_ref, sa_ref, cb_ref, sb_ref, dp_ref, acc_ref, dstate, dobuf):
        i = pl.program_id(0)

        @pl.when(i == 0)
        def _():
            dstate[...] = jnp.zeros_like(dstate)
            acc_ref[...] = jnp.zeros_like(acc_ref)

        dro = lax.dot_general(dh2_ref[...].astype(BF16), wo_ref[D_CONV:], NT, preferred_element_type=F32)
        rg = pr_ref[:, 3 * BLK : 4 * BLK].astype(F32)
        sg = _sigmoid(rg)
        silu, dsilu = rg * sg, sg * (1.0 + rg * (1.0 - sg))
        yh, rstd = _group_norm(o_ref[...].astype(F32))
        grv = gr_ref[...]
        dp_ref[:, 3 * BLK : 4 * BLK] = (dro * (yh * grv) * dsilu).astype(BF16)
        dret = dro * silu
        acc_ref[0:1, :] += jnp.sum(dret * yh, axis=0, keepdims=True)
        dyh = dret * grv
        for hd in range(N_HEADS):
            cs = slice(hd * HEAD_DIM, (hd + 1) * HEAD_DIM)
            a, b = dyh[:, cs], yh[:, cs]
            dobuf[:, cs] = rstd[:, cs] * (
                a - jnp.mean(a, axis=-1, keepdims=True) - b * jnp.mean(a * b, axis=-1, keepdims=True)
            )

        cos_all, sin_all = _tile_rotary(ca_ref, sa_ref, cb_ref, sb_ref, nt - 1 - i)
        for c in reversed(range(nct)):
            rs = slice(c * CHUNK, (c + 1) * CHUNK)
            cos_t, sin_t = cos_all[rs, :], sin_all[rs, :]
            for hd in range(N_HEADS):
                cs = slice(hd * HEAD_DIM, (hd + 1) * HEAD_DIM)
                q = pr_ref[rs, hd * HEAD_DIM : (hd + 1) * HEAD_DIM]
                k = pr_ref[rs, BLK + hd * HEAD_DIM : BLK + (hd + 1) * HEAD_DIM]
                v = pr_ref[rs, 2 * BLK + hd * HEAD_DIM : 2 * BLK + (hd + 1) * HEAD_DIM]
                do = dobuf[rs, cs].astype(BF16)
                st_bf = st_ref[c, hd]
                dst = dstate[hd]
                dst_bf = dst.astype(BF16)
                zt, xt = zeta_ref[hd], xi_ref[hd]
                sT = (lax.dot_general(k, q, NT, preferred_element_type=F32) * dect_ref[hd]).astype(BF16)
                dsT = (lax.dot_general(v, do, NT, preferred_element_type=F32) * dect_ref[hd]).astype(BF16)
                ds = (lax.dot_general(do, v, NT, preferred_element_type=F32) * dec_ref[hd]).astype(BF16)
                kz = (k.astype(F32) * zt).astype(BF16)
                dv = jnp.dot(sT, do, preferred_element_type=F32) + jnp.dot(kz, dst_bf, preferred_element_type=F32)
                dq = jnp.dot(ds, k, preferred_element_type=F32) + xt * lax.dot_general(
                    do, st_bf, NT, preferred_element_type=F32
                )
                dk = jnp.dot(dsT, q, preferred_element_type=F32) + zt * lax.dot_general(
                    v, dst_bf, NT, preferred_element_type=F32
                )
                qx = (q.astype(F32) * xt).astype(BF16)
                dstate[hd] = CHUNK_DECAY[hd] * dst + lax.dot_general(qx, do, TN, preferred_element_type=F32)
                dp_ref[rs, hd * HEAD_DIM : (hd + 1) * HEAD_DIM] = _rot_bwd(dq * Q_SCALE, cos_t, sin_t).astype(BF16)
                dp_ref[rs, BLK + hd * HEAD_DIM : BLK + (hd + 1) * HEAD_DIM] = _rot_bwd(dk, cos_t, sin_t).astype(BF16)
                dp_ref[rs, 2 * BLK + hd * HEAD_DIM : 2 * BLK + (hd + 1) * HEAD_DIM] = dv.astype(BF16)

    half = N_PROJ * BLK // 2
    rev = lambda w, j=0: pl.BlockSpec((tm, w), lambda i: (nt - 1 - i, j))
    return pl.pallas_call(
        body,
        name="b1_dret",
        grid=(nt,),
        in_specs=[
            rev(D_MODEL),
            rev(half, 1),
            rev(D_RET),
            pl.BlockSpec((nct, N_HEADS, HEAD_DIM, HEAD_DIM), lambda i: (nt - 1 - i, 0, 0, 0)),
        ]
        + [_VMEM] * 10,
        out_specs=(rev(half, 1), pl.BlockSpec((8, D_RET), lambda i: (0, 0))),
        out_shape=(jax.ShapeDtypeStruct((n_rows, N_PROJ * BLK), BF16), jax.ShapeDtypeStruct((8, D_RET), F32)),
        scratch_shapes=[pltpu.VMEM((N_HEADS, HEAD_DIM, HEAD_DIM), F32), pltpu.VMEM((tm, D_RET), F32)],
        compiler_params=_params(),
    )(dh2, proj, o, states, wout, gr, dec, dect, zeta, xi, *rot)


def _b2(dproj, proj, conv, w3, wout, cw8, x2, meta_tile, dh2, g1):
    n_rows = dproj.shape[0]
    tm = ROW_TILE
    nt = n_rows // tm
    half = N_PROJ * BLK // 2

    def body(dpr_ref, pr_ref, cv_ref, w_ref, wo_ref, cw_ref, x_ref, mt_ref, dh2_ref, g_ref,
             dpc_ref, gx_ref, dm_ref, acc_ref, accc_ref, halo):
        i = pl.program_id(0)
        tile = nt - 1 - i

        @pl.when(i == 0)
        def _():
            acc_ref[...] = jnp.zeros_like(acc_ref)
            accc_ref[...] = jnp.zeros_like(accc_ref)
            halo[...] = jnp.zeros_like(halo)

        dhn = lax.dot_general(dpr_ref[:, 0:BLK], w_ref[4], NT, preferred_element_type=F32)
        for j in range(1, 4):
            dhn += lax.dot_general(dpr_ref[:, j * BLK : (j + 1) * BLK], w_ref[4 + j], NT, preferred_element_type=F32)

        dh2 = dh2_ref[...]
        dco = lax.dot_general(dh2.astype(BF16), wo_ref[0:D_CONV], NT, preferred_element_type=F32)
        cx = pr_ref[:, 0 * BLK : 1 * BLK].astype(F32)
        cb = pr_ref[:, 1 * BLK : 2 * BLK].astype(F32)
        cc = pr_ref[:, 2 * BLK : 3 * BLK].astype(F32)
        cg = pr_ref[:, 3 * BLK : 4 * BLK].astype(F32)
        conv = cv_ref[...].astype(F32)
        sg = _sigmoid(cg)
        silu, dsilu = cg * sg, sg * (1.0 + cg * (1.0 - sg))
        t = dco * cb
        dcb = (dco * conv * silu).astype(BF16)
        dcg = (t * conv * dsilu).astype(BF16)
        dconv = t * silu
        rows = lax.broadcasted_iota(jnp.int32, dconv.shape, 0)
        hl = halo[...]
        dc1 = jnp.where(rows == tm - 1, hl[0:1], pltpu.roll(dconv, tm - 1, 0))
        dc2 = jnp.where(rows == tm - 2, hl[0:1], jnp.where(rows == tm - 1, hl[1:2], pltpu.roll(dconv, tm - 2, 0)))
        halo[...] = dconv[0:8]
        cw = cw_ref[...]
        du = cw[2:3] * dconv + cw[1:2] * dc1 + cw[0:1] * dc2
        u = cc * cx
        accc_ref[0:1, :] += jnp.sum(u * dc2, axis=0, keepdims=True)
        accc_ref[1:2, :] += jnp.sum(u * dc1, axis=0, keepdims=True)
        accc_ref[2:3, :] += jnp.sum(u * dconv, axis=0, keepdims=True)
        dcx = (du * cc).astype(BF16)
        dcc = (du * cx).astype(BF16)
        for j, d in enumerate((dcx, dcb, dcc, dcg)):
            dpc_ref[:, j * BLK : (j + 1) * BLK] = d
            dhn += lax.dot_general(d, w_ref[j], NT, preferred_element_type=F32)

        h = jnp.where(tile == 0, mt_ref[...], x_ref[...])
        r = lax.rsqrt(jnp.mean(h * h, axis=-1, keepdims=True) + EPS)
        hh = h * r
        acc_ref[0:1, :] += jnp.sum(dhn * hh, axis=0, keepdims=True)
        dg = dhn * g_ref[...]
        dh = dh2 + r * (dg - hh * jnp.mean(dg * hh, axis=-1, keepdims=True))

        @pl.when(tile == 0)
        def _():
            dm_ref[...] = dh[tm - N_META : tm]

        @pl.when(tile > 0)
        def _():
            gx_ref[...] = dh

    rev = lambda w, j=0: pl.BlockSpec((tm, w), lambda i: (nt - 1 - i, j))
    tok = pl.BlockSpec((tm, D_MODEL), lambda i: (_token_tile(nt - 1 - i), 0))
    const = lambda r, c: pl.BlockSpec((r, c), lambda i: (0, 0))
    return pl.pallas_call(
        body,
        name="b2_dconv_dh",
        grid=(nt,),
        in_specs=[rev(half, 1), rev(half, 0), rev(D_CONV), _VMEM, _VMEM, _VMEM, tok, _VMEM, rev(D_MODEL), _VMEM],
        out_specs=(rev(half, 0), tok, const(N_META, D_MODEL), const(8, D_MODEL), const(8, D_CONV)),
        out_shape=(
            jax.ShapeDtypeStruct(dproj.shape, BF16),
            jax.ShapeDtypeStruct(x2.shape, F32),
            jax.ShapeDtypeStruct((N_META, D_MODEL), F32),
            jax.ShapeDtypeStruct((8, D_MODEL), F32),
            jax.ShapeDtypeStruct((8, D_CONV), F32),
        ),
        scratch_shapes=[pltpu.VMEM((8, D_CONV), F32)],
        input_output_aliases={0: 0},
        compiler_params=_params(),
    )(dproj, proj, conv, w3, wout, cw8, x2, meta_tile, dh2, g1)


def _gw_in_scatter(hnT, dproj, gw_out_parts, sc_parts, ga_part, me_arr):
    n_rows = dproj.shape[0]
    last = N_DEV - 1
    by_dest = (True, True, False)

    def body(me_ref, a_ref, b_ref, go_ref, sc_ref, ga_ref, land_in, land_go, land_sc, land_ga,
             d2d_buf, d2d_land, ici_buf, d2d_send, d2d_recv, ici_send, ici_recv, send_sems, recv_sems, local_sems):
        del me_ref
        s = pl.program_id(0)
        t = last - s
        q = t >> 1
        x, y, c = lax.axis_index("x"), lax.axis_index("y"), lax.axis_index("c")
        me = 4 * x + 2 * y + c
        chip = 2 * x + y
        srcs, lands = (go_ref, sc_ref, ga_ref), (land_go, land_sc, land_ga)

        def peer_at(k):
            return (1 - x if k & 4 else x, 1 - y if k & 2 else y, 1 - c if k & 1 else c)

        def small_copy(a, k):
            px, py, pc = peer_at(k)
            return pltpu.make_async_remote_copy(
                src_ref=srcs[a].at[4 * px + 2 * py + pc] if by_dest[a] else srcs[a],
                dst_ref=lands[a].at[me],
                send_sem=send_sems.at[a * last + k - 1],
                recv_sem=recv_sems.at[a * last + k - 1],
                device_id=(px, py, pc),
                device_id_type=MESH,
            )

        def small_local(a):
            return pltpu.make_async_copy(srcs[a].at[me] if by_dest[a] else srcs[a], lands[a].at[me], local_sems.at[a + 1])

        def d2d_copy(j):
            return pltpu.make_async_remote_copy(
                src_ref=d2d_buf.at[j],
                dst_ref=d2d_land.at[j],
                send_sem=d2d_send.at[j],
                recv_sem=d2d_recv.at[j],
                device_id=(x, y, 1 - c),
                device_id_type=MESH,
            )

        def ici_copy(j, to):
            return pltpu.make_async_remote_copy(
                src_ref=ici_buf.at[j],
                dst_ref=land_in.at[chip],
                send_sem=ici_send.at[j],
                recv_sem=ici_recv.at[j],
                device_id=to,
                device_id_type=MESH,
            )

        def own_copy():
            return pltpu.make_async_copy(ici_buf.at[0], land_in.at[chip], local_sems.at[0])

        @pl.when(s == 0)
        def _():
            for a in range(3):
                small_local(a).start()
            for k in range(last, 0, -1):
                for a in range(3):
                    small_copy(a, k).start()

        blk = jnp.dot(a_ref[...], b_ref[...], preferred_element_type=F32)

        @pl.when((t & 1) == 1)
        def _():
            d2d_buf[q] = blk.astype(BF16)
            d2d_copy(q).start()

        @pl.when((t & 1) == 0)
        def _():
            d2d_copy(q).wait_recv()
            ici_buf[q] = (blk + d2d_land[q].astype(F32)).astype(BF16)

            @pl.when(t != 0)
            def _():
                ici_copy(q, (jnp.bitwise_xor(x, (t >> 2) & 1), jnp.bitwise_xor(y, (t >> 1) & 1), c)).start()

            @pl.when(t == 0)
            def _():
                own_copy().start()
                for j in range(N_CHIPS):
                    d2d_copy(j).wait_send()
                for j in range(N_CHIPS - 1, 0, -1):
                    ici_copy(j, peer_at(2 * j)).wait()
                for k in range(last, 0, -1):
                    for a in range(3):
                        small_copy(a, k).wait()
                for a in range(3):
                    small_local(a).wait()
                own_copy().wait()

    smalls = (gw_out_parts, sc_parts, ga_part)
    small_blocks = [p.shape[1:] if d else p.shape for p, d in zip(smalls, by_dest, strict=True)]
    grid_spec = pltpu.PrefetchScalarGridSpec(
        num_scalar_prefetch=1,
        grid=(N_DEV,),
        in_specs=[_VMEM, pl.BlockSpec((n_rows, BLK), lambda s, me: (0, jnp.bitwise_xor(me[0], last - s)))] + [_HBM] * 3,
        out_specs=tuple([_HBM] * 4),
        scratch_shapes=[
            pltpu.VMEM((N_CHIPS, D_MODEL, BLK), BF16),
            pltpu.VMEM((N_CHIPS, D_MODEL, BLK), BF16),
            pltpu.VMEM((N_CHIPS, D_MODEL, BLK), BF16),
            pltpu.SemaphoreType.DMA((N_CHIPS,)),
            pltpu.SemaphoreType.DMA((N_CHIPS,)),
            pltpu.SemaphoreType.DMA((N_CHIPS,)),
            pltpu.SemaphoreType.DMA((N_CHIPS,)),
            pltpu.SemaphoreType.DMA((3 * last,)),
            pltpu.SemaphoreType.DMA((3 * last,)),
            pltpu.SemaphoreType.DMA((4,)),
        ],
    )
    return pl.pallas_call(
        body,
        name="gw_in_scatter",
        grid_spec=grid_spec,
        out_shape=(jax.ShapeDtypeStruct((N_CHIPS, D_MODEL, BLK), BF16),)
        + tuple(jax.ShapeDtypeStruct((N_DEV, *b), p.dtype) for b, p in zip(small_blocks, smalls, strict=True)),
        compiler_params=_params(),
    )(me_arr, hnT, dproj, gw_out_parts, sc_parts, ga_part)


def _gw_out(coT, roT, dh2):
    n_rows = dh2.shape[0]
    tk = ROW_TILE
    nk = n_rows // tk

    def body(c_ref, r_ref, d_ref, out_ref, acc):
        k = pl.program_id(0)

        @pl.when(k == 0)
        def _():
            acc[...] = jnp.zeros_like(acc)

        d = d_ref[...].astype(BF16)
        acc[0:D_CONV, :] += jnp.dot(c_ref[...], d, preferred_element_type=F32)
        acc[D_CONV:, :] += jnp.dot(r_ref[...], d, preferred_element_type=F32)

        @pl.when(k == nk - 1)
        def _():
            out_ref[...] = acc[...].astype(BF16)

    return pl.pallas_call(
        body,
        name="gw_out",
        grid=(nk,),
        in_specs=[
            pl.BlockSpec((D_CONV, tk), lambda k: (0, k)),
            pl.BlockSpec((D_RET, tk), lambda k: (0, k)),
            pl.BlockSpec((tk, D_MODEL), lambda k: (k, 0)),
        ],
        out_specs=pl.BlockSpec((D_MODEL, D_MODEL), lambda k: (0, 0)),
        out_shape=jax.ShapeDtypeStruct((D_MODEL, D_MODEL), BF16),
        scratch_shapes=[pltpu.VMEM((D_MODEL, D_MODEL), F32)],
        compiler_params=_params(),
    )(coT, roT, dh2)


def _adamw(w, g, m, v):
    m = ADAM_B1 * m + (1.0 - ADAM_B1) * g
    v = ADAM_B2 * v + (1.0 - ADAM_B2) * (g * g)
    m_hat = m / (1.0 - ADAM_B1**ADAM_STEP)
    v_hat = v / (1.0 - ADAM_B2**ADAM_STEP)
    delta = -ADAM_LR * (m_hat / (jnp.sqrt(v_hat) + ADAM_EPS) + ADAM_WD * w)
    return delta, m, v


def _sum_adamw(name, parts, w, m, v, rows_per_step):
    n_r, n_c = w.shape
    n_parts = parts.shape[0]
    tr = rows_per_step

    def body(p_ref, w_ref, m_ref, v_ref, g_ref, d_ref, nm_ref, nv_ref):
        g = p_ref[0].astype(F32)
        for s in range(1, n_parts):
            g = g + p_ref[s].astype(F32)
        g_ref[...] = g
        d_ref[...], nm_ref[...], nv_ref[...] = _adamw(w_ref[...], g, m_ref[...], v_ref[...])

    blk = pl.BlockSpec((tr, n_c), lambda i: (i, 0))
    return pl.pallas_call(
        body,
        name=name,
        grid=(n_r // tr,),
        in_specs=[pl.BlockSpec((n_parts, tr, n_c), lambda i: (0, i, 0)), blk, blk, blk],
        out_specs=(blk,) * 4,
        out_shape=(jax.ShapeDtypeStruct((n_r, n_c), F32),) * 4,
        compiler_params=_params(),
    )(parts, w, m, v)


def _small_leaves(meta, conv_w, n1, rg, fg):
    cols = D_CONV // N_DEV
    return (
        meta,
        jnp.pad(conv_w, ((0, 5), (0, CHUNK - cols))),
        n1.reshape(8, CHUNK),
        rg.reshape(4, CHUNK),
        fg.reshape(8, CHUNK),
    )


def _from_small_leaves(meta, conv8, n1, rg, fg):
    return meta, conv8[0:3, : D_CONV // N_DEV], n1.reshape(D_MODEL), rg.reshape(D_RET), fg.reshape(D_MODEL)


def _adamw_small(land_sc, land_ga, w, m, v):
    n_leaf = 5

    def body(sc_ref, ga_ref, *refs):
        ins, outs = refs[: 3 * n_leaf], refs[3 * n_leaf :]
        sc, ga = sc_ref[0], ga_ref[0]
        for s in range(1, N_DEV):
            sc = sc + sc_ref[s]
            ga = ga + ga_ref[s]
        grads = (
            sc[SC_META0 : SC_META0 + N_META],
            sc[SC_CONV0 : SC_CONV0 + 8],
            ga[GA_N1 : GA_N1 + 8],
            ga[GA_RG : GA_RG + 4],
            ga[GA_FG : GA_FG + 8],
        )
        for leaf, g in enumerate(grads):
            d, nm, nv = _adamw(ins[leaf][...], g, ins[n_leaf + leaf][...], ins[2 * n_leaf + leaf][...])
            outs[leaf][...] = g
            outs[n_leaf + leaf][...] = d
            outs[2 * n_leaf + leaf][...] = nm
            outs[3 * n_leaf + leaf][...] = nv
        outs[4 * n_leaf][...] = ga[GA_LOSS : GA_LOSS + 1]

    leaf_shapes = tuple(jax.ShapeDtypeStruct(a.shape, F32) for a in w)
    out = pl.pallas_call(
        body,
        name="adamw_small",
        out_shape=leaf_shapes * 4 + (jax.ShapeDtypeStruct((1, CHUNK), F32),),
    )(land_sc, land_ga, *w, *m, *v)
    return tuple(out[k * n_leaf : (k + 1) * n_leaf] for k in range(4)), out[4 * n_leaf]


def kernel(x, meta, norm1_g, w_in, conv_w, ret_norm_g, w_out, final_g, loss_target, m_meta, m_norm1_g, m_w_in, m_conv_w, m_ret_norm_g, m_w_out, m_final_g, v_meta, v_norm1_g, v_w_in, v_conv_w, v_ret_norm_g, v_w_out, v_final_g):
    seq = x.shape[1]
    assert x.shape == (1, seq, D_MODEL) and seq % ROW_TILE == 0
    n_tiles = seq // ROW_TILE + 1
    cols = D_CONV // N_DEV
    x2, t2 = x[0], loss_target[0]

    small_w = _small_leaves(meta, conv_w, norm1_g, ret_norm_g, final_g)
    w3, wo3, small = _gather(
        "gather_weights", [w_in.astype(BF16), w_out.astype(BF16), jnp.concatenate(small_w[0:2], axis=0)]
    )
    wout = wo3.reshape(D_MODEL, D_MODEL)
    meta_full = small[:, SC_META0 : SC_META0 + N_META, :].transpose(1, 0, 2).reshape(N_META, D_MODEL)
    conv_full = small[:, SC_CONV0 : SC_CONV0 + 3, :cols].transpose(1, 0, 2).reshape(3, D_CONV)
    cw8 = jnp.pad(conv_full, ((0, 5), (0, 0)))
    meta_tile = jnp.pad(meta_full, ((PAD_ROWS, 0), (0, 0)))

    rot = _rotary_tables(n_tiles, ROW_TILE)
    dec, dect, zeta, xi = _decay_tables()
    g1 = norm1_g.reshape(1, D_MODEL)
    gr = ret_norm_g.reshape(1, D_RET)
    gf = final_g.reshape(1, D_MODEL)

    hnT, proj, conv_out, conv_outT, conv = _f1(x2, meta_tile, g1, w3, cw8, rot)
    dh2, o, _, ret_outT, states, acc_f2 = _f2(proj, conv_out, x2, meta_tile, t2, wout, gr, gf, dec, zeta, xi)
    dproj_ret, acc_b1 = _b1(dh2, proj, o, states, wout, gr, dec, dect, zeta, xi, rot)
    dproj, grad_x2, dmeta_rows, acc_b2, acc_conv = _b2(dproj_ret, proj, conv, w3, wout, cw8, x2, meta_tile, dh2, g1)
    gw_out_parts = _gw_out(conv_outT, ret_outT, dh2).reshape(N_DEV, D_MODEL // N_DEV, D_MODEL)

    dmeta = dmeta_rows.reshape(N_META, N_DEV, CHUNK).transpose(1, 0, 2)
    dconv = jnp.pad(acc_conv[0:3, :].reshape(3, N_DEV, cols).transpose(1, 0, 2), ((0, 0), (0, 5), (0, CHUNK - cols)))
    sc_parts = jnp.concatenate([dmeta, dconv], axis=1)
    ga_part = jnp.concatenate(
        [
            acc_b2[0].reshape(8, CHUNK),
            jnp.pad(acc_b1[0].reshape(4, CHUNK), ((0, 4), (0, 0))),
            acc_f2[0].reshape(8, CHUNK),
            jnp.pad(acc_f2[1:2, 0:CHUNK], ((0, 7), (0, 0))),
        ],
        axis=0,
    )

    me_arr = (4 * lax.axis_index("x") + 2 * lax.axis_index("y") + lax.axis_index("c")).astype(jnp.int32).reshape(1)
    land_in, land_out, land_sc, land_ga = _gw_in_scatter(hnT, dproj, gw_out_parts, sc_parts, ga_part, me_arr)

    g_w_in, d_w_in, nm_w_in, nv_w_in = _sum_adamw("adamw_w_in", land_in, w_in, m_w_in, v_w_in, 256)
    g_w_out, d_w_out, nm_w_out, nv_w_out = _sum_adamw("adamw_w_out", land_out, w_out, m_w_out, v_w_out, 64)
    small_out, loss_row = _adamw_small(
        land_sc,
        land_ga,
        small_w,
        _small_leaves(m_meta, m_conv_w, m_norm1_g, m_ret_norm_g, m_final_g),
        _small_leaves(v_meta, v_conv_w, v_norm1_g, v_ret_norm_g, v_final_g),
    )
    loss = loss_row[0, 0]
    grad_x = grad_x2.reshape(1, seq, D_MODEL)

    def leaves(w_in_leaf, w_out_leaf, small_leaves):
        meta_leaf, conv_leaf, n1_leaf, rg_leaf, fg_leaf = _from_small_leaves(*small_leaves)
        return (meta_leaf, n1_leaf, w_in_leaf, conv_leaf, rg_leaf, w_out_leaf, fg_leaf)

    return (
        loss,
        grad_x,
        *leaves(g_w_in, g_w_out, small_out[0]),
        *leaves(d_w_in, d_w_out, small_out[1]),
        *leaves(nm_w_in, nm_w_out, small_out[2]),
        *leaves(nv_w_in, nv_w_out, small_out[3]),
    )
```

```python
import functools
import math

import jax
import jax.numpy as jnp
import numpy as np
from jax import lax
from jax.experimental import pallas as pl
from jax.experimental.pallas import tpu as pltpu

F32 = jnp.float32
BF16 = jnp.bfloat16

N_DEV = 8
N_CHIPS = 4
D_MODEL = 1024
N_META = 16
CHUNK = 128
D_CONV = 512
D_RET = 512
N_HEADS = 4
HEAD_DIM = 128
N_PROJ = 8
BLK = 512
ROPE_BASE = 10000.0
EPS = 1e-6
Q_SCALE = HEAD_DIM ** -0.5
LOG_G = tuple(math.log(1.0 - 2.0 ** (-5.0 - h)) for h in range(N_HEADS))
CHUNK_DECAY = tuple(math.exp(CHUNK * lg) for lg in LOG_G)

ADAM_LR = 0.001
ADAM_B1 = 0.9
ADAM_B2 = 0.999
ADAM_EPS = 1e-08
ADAM_WD = 0.01
ADAM_STEP = 10

ROW_TILE = 512
PAD_ROWS = ROW_TILE - N_META
LIVE0 = ROW_TILE - CHUNK
VMEM_LIMIT = 56 * 1024 * 1024

SC_META0, SC_CONV0 = 0, 16
GA_N1, GA_RG, GA_FG, GA_LOSS = 0, 8, 16, 24

NT = (((1,), (1,)), ((), ()))
TN = (((0,), (0,)), ((), ()))
MESH = pl.DeviceIdType.MESH

_VMEM = pl.BlockSpec(memory_space=pltpu.VMEM)
_HBM = pl.BlockSpec(memory_space=pltpu.HBM)


def _params(n_axes=1):
    return pltpu.CompilerParams(dimension_semantics=("arbitrary",) * n_axes, vmem_limit_bytes=VMEM_LIMIT)


def _sigmoid(x):
    return 0.5 * jnp.tanh(0.5 * x) + 0.5


def _decay_tables():
    idx = np.arange(CHUNK, dtype=np.float64)
    diff = idx[:, None] - idx[None, :]
    dec = np.stack([np.where(diff >= 0, np.exp(diff * lg), 0.0) for lg in LOG_G])
    zeta = np.stack([np.exp((CHUNK - 1 - idx) * lg) for lg in LOG_G])
    xi = np.stack([np.exp((idx + 1.0) * lg) for lg in LOG_G])
    ones = np.ones((1, 1, CHUNK))
    return (
        jnp.asarray(dec, F32),
        jnp.asarray(dec.transpose(0, 2, 1), F32),
        jnp.asarray(zeta[:, :, None] * ones, F32),
        jnp.asarray(xi[:, :, None] * ones, F32),
    )


def _rotary_tables(n_tiles, tm):
    half = HEAD_DIM // 2
    freqs = (1.0 / (np.float32(ROPE_BASE) ** (np.arange(half, dtype=np.float32) / np.float32(half)))).astype(np.float64)
    sign = np.concatenate([-np.ones(half), np.ones(half)])
    two = lambda a: np.concatenate([a, a], axis=1)
    base = two((np.arange(n_tiles, dtype=np.float64) * tm - PAD_ROWS)[:, None] * freqs[None, :])
    off = two(np.arange(tm, dtype=np.float64)[:, None] * freqs[None, :])
    as32 = lambda a: jnp.asarray(a, F32)
    return as32(np.cos(base)), as32(np.sin(base) * sign), as32(np.cos(off)), as32(np.sin(off) * sign)


def _tile_rotary(ca_ref, sa_ref, cb_ref, sb_ref, tile):
    ca, sa = ca_ref[pl.ds(tile, 1), :], sa_ref[pl.ds(tile, 1), :]
    cb, sb = cb_ref[...], sb_ref[...]
    return ca * cb - sa * sb, sa * cb + ca * sb


def _rot(t, cos2, sin2):
    return t * cos2 + pltpu.roll(t, HEAD_DIM // 2, 1) * sin2


def _rot_bwd(d, cos2, sin2):
    return d * cos2 + pltpu.roll(d * sin2, HEAD_DIM // 2, 1)


def _token_tile(i):
    return jnp.maximum(i - 1, 0)


def _gather(name, srcs):
    n = len(srcs)
    per = 7

    def body(*refs):
        src_refs, out_refs = refs[:n], refs[n : 2 * n]
        send_sems, recv_sems, local_sems = refs[2 * n :]
        x, y, c = lax.axis_index("x"), lax.axis_index("y"), lax.axis_index("c")
        sibling = (x, y, 1 - c)
        chips = [(1 - x, y), (x, 1 - y), (1 - x, 1 - y)]

        def slot(px, py, pc):
            return 4 * px + 2 * py + pc

        def copy(a, k, block, to, src=None):
            dst = out_refs[a].at[slot(*block)]
            return pltpu.make_async_remote_copy(
                src_ref=dst if src is None else src,
                dst_ref=dst,
                send_sem=send_sems.at[a * per + k],
                recv_sem=recv_sems.at[a * per + k],
                device_id=to,
                device_id_type=MESH,
            )

        me = (x, y, c)
        mine, first, passed = [], [], []
        for a in range(n):
            cp = pltpu.make_async_copy(src_refs[a], out_refs[a].at[slot(*me)], local_sems.at[a])
            cp.start()
            mine.append(cp)
        for j, chip in enumerate(chips):
            for a in range(n):
                cp = copy(a, 1 + j, me, (*chip, c), src=src_refs[a])
                cp.start()
                first.append(cp)
        for a in range(n):
            cp = copy(a, 0, me, sibling, src=src_refs[a])
            cp.start()
            first.append(cp)
        for j, chip in enumerate(chips):
            for a in range(n):
                copy(a, 1 + j, (*chip, c), me).wait_recv()
                cp = copy(a, 4 + j, (*chip, c), sibling)
                cp.start()
                passed.append(cp)
        for a in range(n):
            copy(a, 0, (x, y, 1 - c), me).wait_recv()
            for j, chip in enumerate(chips):
                copy(a, 4 + j, (*chip, 1 - c), me).wait_recv()
        for cp in first + passed:
            cp.wait_send()
        for cp in mine:
            cp.wait()

    return pl.pallas_call(
        body,
        name=name,
        out_shape=tuple(jax.ShapeDtypeStruct((N_DEV, *s.shape), s.dtype) for s in srcs),
        in_specs=[_HBM] * n,
        out_specs=tuple([_HBM] * n),
        scratch_shapes=[
            pltpu.SemaphoreType.DMA((n * per,)),
            pltpu.SemaphoreType.DMA((n * per,)),
            pltpu.SemaphoreType.DMA((n,)),
        ],
    )(*srcs)


def _f1(x2, meta_chunk, g1, w3, cw8, rot):
    tm = ROW_TILE
    nt = x2.shape[0] // tm + 1
    n_rows = nt * tm

    def body(x_ref, mc_ref, g_ref, w_ref, cw_ref, ca_ref, sa_ref, cb_ref, sb_ref,
             hnT_ref, pr_ref, co_ref, coT_ref, cv_ref, halo):
        i = pl.program_id(0)

        def work(h, r0):
            rs = slice(r0, tm)
            n = tm - r0
            r = lax.rsqrt(jnp.mean(h * h, axis=-1, keepdims=True) + EPS)
            hn = (h * r * g_ref[...]).astype(BF16)
            hnT_ref[:, rs] = hn.T

            def proj(j):
                return jnp.dot(hn, w_ref[j], preferred_element_type=F32)

            cx, cb, cc, cg = proj(0), proj(1), proj(2), proj(3)
            u = cc * cx
            rows = lax.broadcasted_iota(jnp.int32, u.shape, 0)
            hl = halo[...]
            u1 = jnp.where(rows == 0, hl[7:8], pltpu.roll(u, 1, 0))
            u2 = jnp.where(rows == 0, hl[6:7], jnp.where(rows == 1, hl[7:8], pltpu.roll(u, 2, 0)))
            halo[...] = u[n - 8 : n]
            cw = cw_ref[...]
            conv = cw[0:1] * u2 + cw[1:2] * u1 + cw[2:3] * u
            co = (cb * conv * (cg * _sigmoid(cg))).astype(BF16)
            co_ref[rs, :] = co
            coT_ref[:, rs] = co.T
            cv_ref[rs, :] = conv.astype(BF16)
            pr_ref[rs, 0 * BLK : 1 * BLK] = cx.astype(BF16)
            pr_ref[rs, 1 * BLK : 2 * BLK] = cb.astype(BF16)
            pr_ref[rs, 2 * BLK : 3 * BLK] = cc.astype(BF16)
            pr_ref[rs, 3 * BLK : 4 * BLK] = cg.astype(BF16)
            cos_all, sin_all = _tile_rotary(ca_ref, sa_ref, cb_ref, sb_ref, i)
            cos_t, sin_t = cos_all[rs, :], sin_all[rs, :]
            q, k = proj(4), proj(5)
            for hd in range(N_HEADS):
                c0 = hd * HEAD_DIM
                pr_ref[rs, 4 * BLK + c0 : 4 * BLK + c0 + HEAD_DIM] = (
                    _rot(q[:, c0 : c0 + HEAD_DIM], cos_t, sin_t) * Q_SCALE
                ).astype(BF16)
                pr_ref[rs, 5 * BLK + c0 : 5 * BLK + c0 + HEAD_DIM] = _rot(
                    k[:, c0 : c0 + HEAD_DIM], cos_t, sin_t
                ).astype(BF16)
            pr_ref[rs, 6 * BLK : 7 * BLK] = proj(6).astype(BF16)
            pr_ref[rs, 7 * BLK : 8 * BLK] = proj(7).astype(BF16)

        @pl.when(i == 0)
        def _():
            halo[...] = jnp.zeros_like(halo)
            work(mc_ref[...], LIVE0)

        @pl.when(i > 0)
        def _():
            work(x_ref[...], 0)

    row = lambda w: pl.BlockSpec((tm, w), lambda i: (i, 0))
    col = lambda w: pl.BlockSpec((w, tm), lambda i: (0, i))
    return pl.pallas_call(
        body,
        name="f1_inproj_conv",
        grid=(nt,),
        in_specs=[pl.BlockSpec((tm, D_MODEL), lambda i: (_token_tile(i), 0))] + [_VMEM] * 8,
        out_specs=(col(D_MODEL), row(N_PROJ * BLK), row(D_CONV), col(D_CONV), row(D_CONV)),
        out_shape=(
            jax.ShapeDtypeStruct((D_MODEL, n_rows), BF16),
            jax.ShapeDtypeStruct((n_rows, N_PROJ * BLK), BF16),
            jax.ShapeDtypeStruct((n_rows, D_CONV), BF16),
            jax.ShapeDtypeStruct((D_CONV, n_rows), BF16),
            jax.ShapeDtypeStruct((n_rows, D_CONV), BF16),
        ),
        scratch_shapes=[pltpu.VMEM((8, D_CONV), F32)],
        compiler_params=_params(),
    )(x2, meta_chunk, g1, w3, cw8, *rot)


def _group_norm(o):
    ys, rs = [], []
    for hd in range(N_HEADS):
        oh = o[:, hd * HEAD_DIM : (hd + 1) * HEAD_DIM]
        xc = oh - jnp.mean(oh, axis=-1, keepdims=True)
        rstd = lax.rsqrt(jnp.mean(xc * xc, axis=-1, keepdims=True) + EPS)
        ys.append(xc * rstd)
        rs.append(jnp.broadcast_to(rstd, oh.shape))
    return jnp.concatenate(ys, axis=1), jnp.concatenate(rs, axis=1)


def _f2(proj, conv_out, x2, t2, wout, gr, gf, dec, zeta, xi):
    n_rows = proj.shape[0]
    tm = ROW_TILE
    nt = n_rows // tm
    nct = tm // CHUNK

    def body(q_ref, k_ref, v_ref, rg_ref, co_ref, x_ref, t_ref, wo_ref, gr_ref, gf_ref, dec_ref, zeta_ref,
             xi_ref, dh2_ref, o_ref, roT_ref, st_ref, acc_ref, state, obuf):
        i = pl.program_id(0)

        def mixer(r0):
            for c in range(r0 // CHUNK, nct):
                rs = slice(c * CHUNK, (c + 1) * CHUNK)
                for hd in range(N_HEADS):
                    cs = slice(hd * HEAD_DIM, (hd + 1) * HEAD_DIM)
                    q, k, v = q_ref[rs, cs], k_ref[rs, cs], v_ref[rs, cs]
                    st = state[hd]
                    st_bf = st.astype(BF16)
                    st_ref[c, hd] = st_bf
                    s = lax.dot_general(q, k, NT, preferred_element_type=F32) * dec_ref[hd]
                    inner = jnp.dot(s.astype(BF16), v, preferred_element_type=F32)
                    qx = (q.astype(F32) * xi_ref[hd]).astype(BF16)
                    obuf[rs, cs] = inner + jnp.dot(qx, st_bf, preferred_element_type=F32)
                    kz = (k.astype(F32) * zeta_ref[hd]).astype(BF16)
                    state[hd] = CHUNK_DECAY[hd] * st + lax.dot_general(kz, v, TN, preferred_element_type=F32)
            o = obuf[r0:tm, :]
            o_ref[r0:tm, :] = o.astype(BF16)
            yh, _ = _group_norm(o)
            rg = rg_ref[r0:tm, :].astype(F32)
            ro = (yh * gr_ref[...] * (rg * _sigmoid(rg))).astype(BF16)
            roT_ref[:, r0:tm] = ro.T
            return ro

        @pl.when(i == 0)
        def _():
            state[...] = jnp.zeros_like(state)
            acc_ref[...] = jnp.zeros_like(acc_ref)
            mixer(LIVE0)
            dh2_ref[...] = jnp.zeros_like(dh2_ref)

        @pl.when(i > 0)
        def _():
            ro = mixer(0)
            h2 = (
                x_ref[...]
                + jnp.dot(co_ref[...], wo_ref[0:D_CONV], preferred_element_type=F32)
                + jnp.dot(ro, wo_ref[D_CONV:], preferred_element_type=F32)
            )
            r2 = lax.rsqrt(jnp.mean(h2 * h2, axis=-1, keepdims=True) + EPS)
            yn = h2 * r2
            gfv = gf_ref[...]
            err = yn * gfv - t_ref[...]
            tile_loss = jnp.sum(jnp.sum(err * err, axis=-1, keepdims=True), axis=0, keepdims=True) * (0.5 / D_MODEL)
            acc_ref[0:1, :] += jnp.sum(err * yn, axis=0, keepdims=True) * (1.0 / D_MODEL)
            acc_ref[1:2, :] += tile_loss
            dyn = err * (gfv * (1.0 / D_MODEL))
            dh2_ref[...] = r2 * (dyn - yn * jnp.mean(dyn * yn, axis=-1, keepdims=True))

    row = lambda w, j=0: pl.BlockSpec((tm, w), lambda i: (i, j))
    tok = pl.BlockSpec((tm, D_MODEL), lambda i: (_token_tile(i), 0))
    return pl.pallas_call(
        body,
        name="f2_retention_out",
        grid=(nt,),
        in_specs=[row(BLK, 4), row(BLK, 5), row(BLK, 6), row(BLK, 7), row(D_CONV), tok, tok] + [_VMEM] * 6,
        out_specs=(
            row(D_MODEL),
            row(D_RET),
            pl.BlockSpec((D_RET, tm), lambda i: (0, i)),
            pl.BlockSpec((nct, N_HEADS, HEAD_DIM, HEAD_DIM), lambda i: (i, 0, 0, 0)),
            pl.BlockSpec((8, D_MODEL), lambda i: (0, 0)),
        ),
        out_shape=(
            jax.ShapeDtypeStruct((n_rows, D_MODEL), F32),
            jax.ShapeDtypeStruct((n_rows, D_RET), BF16),
            jax.ShapeDtypeStruct((D_RET, n_rows), BF16),
            jax.ShapeDtypeStruct((n_rows // CHUNK, N_HEADS, HEAD_DIM, HEAD_DIM), BF16),
            jax.ShapeDtypeStruct((8, D_MODEL), F32),
        ),
        scratch_shapes=[pltpu.VMEM((N_HEADS, HEAD_DIM, HEAD_DIM), F32), pltpu.VMEM((tm, D_RET), F32)],
        compiler_params=_params(),
    )(proj, proj, proj, proj, conv_out, x2, t2, wout, gr, gf, dec, zeta, xi)


def _b1(dh2, proj, o, states, wout, gr, dec, dect, zeta, xi, rot):
    n_rows = dh2.shape[0]
    tm = ROW_TILE
    nt = n_rows // tm
    nct = tm // CHUNK

    def body(dh2_ref, pr_ref, o_ref, st_ref, wo_ref, gr_ref, dec_ref, dect_ref, zeta_ref, xi_ref,
             ca_ref, sa_ref, cb_ref, sb_ref, dp_ref, dco_ref, acc_ref, dstate, dobuf):
        i = pl.program_id(0)
        tile = nt - 1 - i
        cos_all, sin_all = _tile_rotary(ca_ref, sa_ref, cb_ref, sb_ref, tile)

        @pl.when(i == 0)
        def _():
            dstate[...] = jnp.zeros_like(dstate)
            acc_ref[...] = jnp.zeros_like(acc_ref)

        @pl.when(tile > 0)
        def _():
            dmix = lax.dot_general(dh2_ref[...].astype(BF16), wo_ref[...], NT, preferred_element_type=F32)
            dco_ref[...] = dmix[:, :D_CONV].astype(BF16)
            dro = dmix[:, D_CONV:]
            rg = pr_ref[:, 3 * BLK : 4 * BLK].astype(F32)
            sg = _sigmoid(rg)
            silu, dsilu = rg * sg, sg * (1.0 + rg * (1.0 - sg))
            yh, rstd = _group_norm(o_ref[...].astype(F32))
            grv = gr_ref[...]
            dp_ref[:, 3 * BLK : 4 * BLK] = (dro * (yh * grv) * dsilu).astype(BF16)
            dret = dro * silu
            acc_ref[0:1, :] += jnp.sum(dret * yh, axis=0, keepdims=True)
            dyh = dret * grv
            for hd in range(N_HEADS):
                cs = slice(hd * HEAD_DIM, (hd + 1) * HEAD_DIM)
                a, b = dyh[:, cs], yh[:, cs]
                dobuf[:, cs] = rstd[:, cs] * (
                    a - jnp.mean(a, axis=-1, keepdims=True) - b * jnp.mean(a * b, axis=-1, keepdims=True)
                )

            for c in reversed(range(nct)):
                rs = slice(c * CHUNK, (c + 1) * CHUNK)
                cos_t, sin_t = cos_all[rs, :], sin_all[rs, :]
                for hd in range(N_HEADS):
                    cs = slice(hd * HEAD_DIM, (hd + 1) * HEAD_DIM)
                    q = pr_ref[rs, hd * HEAD_DIM : (hd + 1) * HEAD_DIM]
                    k = pr_ref[rs, BLK + hd * HEAD_DIM : BLK + (hd + 1) * HEAD_DIM]
                    v = pr_ref[rs, 2 * BLK + hd * HEAD_DIM : 2 * BLK + (hd + 1) * HEAD_DIM]
                    do = dobuf[rs, cs].astype(BF16)
                    st_bf = st_ref[c, hd]
                    dst = dstate[hd]
                    dst_bf = dst.astype(BF16)
                    zt, xt = zeta_ref[hd], xi_ref[hd]
                    sT = (lax.dot_general(k, q, NT, preferred_element_type=F32) * dect_ref[hd]).astype(BF16)
                    dsT = (lax.dot_general(v, do, NT, preferred_element_type=F32) * dect_ref[hd]).astype(BF16)
                    ds = (lax.dot_general(do, v, NT, preferred_element_type=F32) * dec_ref[hd]).astype(BF16)
                    kz = (k.astype(F32) * zt).astype(BF16)
                    dv = jnp.dot(sT, do, preferred_element_type=F32) + jnp.dot(kz, dst_bf, preferred_element_type=F32)
                    dq = jnp.dot(ds, k, preferred_element_type=F32) + xt * lax.dot_general(
                        do, st_bf, NT, preferred_element_type=F32
                    )
                    dk = jnp.dot(dsT, q, preferred_element_type=F32) + zt * lax.dot_general(
                        v, dst_bf, NT, preferred_element_type=F32
                    )
                    qx = (q.astype(F32) * xt).astype(BF16)
                    dstate[hd] = CHUNK_DECAY[hd] * dst + lax.dot_general(qx, do, TN, preferred_element_type=F32)
                    dp_ref[rs, hd * HEAD_DIM : (hd + 1) * HEAD_DIM] = _rot_bwd(dq * Q_SCALE, cos_t, sin_t).astype(BF16)
                    dp_ref[rs, BLK + hd * HEAD_DIM : BLK + (hd + 1) * HEAD_DIM] = _rot_bwd(dk, cos_t, sin_t).astype(BF16)
                    dp_ref[rs, 2 * BLK + hd * HEAD_DIM : 2 * BLK + (hd + 1) * HEAD_DIM] = dv.astype(BF16)

        @pl.when(tile == 0)
        def _():
            rs = slice(LIVE0, tm)
            cos_t, sin_t = cos_all[rs, :], sin_all[rs, :]
            zeros = jnp.zeros((tm - LIVE0, HEAD_DIM), BF16)
            for hd in range(N_HEADS):
                k = pr_ref[rs, BLK + hd * HEAD_DIM : BLK + (hd + 1) * HEAD_DIM]
                v = pr_ref[rs, 2 * BLK + hd * HEAD_DIM : 2 * BLK + (hd + 1) * HEAD_DIM]
                dst_bf = dstate[hd].astype(BF16)
                zt = zeta_ref[hd]
                kz = (k.astype(F32) * zt).astype(BF16)
                dv = jnp.dot(kz, dst_bf, preferred_element_type=F32)
                dk = zt * lax.dot_general(v, dst_bf, NT, preferred_element_type=F32)
                dp_ref[rs, hd * HEAD_DIM : (hd + 1) * HEAD_DIM] = zeros
                dp_ref[rs, BLK + hd * HEAD_DIM : BLK + (hd + 1) * HEAD_DIM] = _rot_bwd(dk, cos_t, sin_t).astype(BF16)
                dp_ref[rs, 2 * BLK + hd * HEAD_DIM : 2 * BLK + (hd + 1) * HEAD_DIM] = dv.astype(BF16)
                dp_ref[rs, 3 * BLK + hd * HEAD_DIM : 3 * BLK + (hd + 1) * HEAD_DIM] = zeros
            dco_ref[rs, :] = jnp.zeros((tm - LIVE0, D_CONV), BF16)

    half = N_PROJ * BLK // 2
    rev = lambda w, j=0: pl.BlockSpec((tm, w), lambda i: (nt - 1 - i, j))
    return pl.pallas_call(
        body,
        name="b1_dret",
        grid=(nt,),
        in_specs=[
            rev(D_MODEL),
            rev(half, 1),
            rev(D_RET),
            pl.BlockSpec((nct, N_HEADS, HEAD_DIM, HEAD_DIM), lambda i: (nt - 1 - i, 0, 0, 0)),
        ]
        + [_VMEM] * 10,
        out_specs=(rev(half, 1), rev(D_CONV), pl.BlockSpec((8, D_RET), lambda i: (0, 0))),
        out_shape=(
            jax.ShapeDtypeStruct((n_rows, N_PROJ * BLK), BF16),
            jax.ShapeDtypeStruct((n_rows, D_CONV), BF16),
            jax.ShapeDtypeStruct((8, D_RET), F32),
        ),
        scratch_shapes=[pltpu.VMEM((N_HEADS, HEAD_DIM, HEAD_DIM), F32), pltpu.VMEM((tm, D_RET), F32)],
        compiler_params=_params(),
    )(dh2, proj, o, states, wout, gr, dec, dect, zeta, xi, *rot)


def _b2(dproj, dco, proj, conv, w3, cw8, x2, meta_chunk, dh2, g1):
    n_rows = dproj.shape[0]
    tm = ROW_TILE
    nt = n_rows // tm
    half = N_PROJ * BLK // 2

    def body(dpr_ref, dco_ref, pr_ref, cv_ref, w_ref, cw_ref, x_ref, mc_ref, dh2_ref, g_ref,
             dpc_ref, gx_ref, dm_ref, acc_ref, accc_ref, halo):
        i = pl.program_id(0)
        tile = nt - 1 - i

        @pl.when(i == 0)
        def _():
            acc_ref[...] = jnp.zeros_like(acc_ref)
            accc_ref[...] = jnp.zeros_like(accc_ref)
            halo[...] = jnp.zeros_like(halo)

        def work(h, r0):
            rs = slice(r0, tm)
            n = tm - r0
            dco = dco_ref[rs, :].astype(F32)
            cx = pr_ref[rs, 0 * BLK : 1 * BLK].astype(F32)
            cb = pr_ref[rs, 1 * BLK : 2 * BLK].astype(F32)
            cc = pr_ref[rs, 2 * BLK : 3 * BLK].astype(F32)
            cg = pr_ref[rs, 3 * BLK : 4 * BLK].astype(F32)
            conv = cv_ref[rs, :].astype(F32)
            sg = _sigmoid(cg)
            silu, dsilu = cg * sg, sg * (1.0 + cg * (1.0 - sg))
            t = dco * cb
            dcb = (dco * conv * silu).astype(BF16)
            dcg = (t * conv * dsilu).astype(BF16)
            dconv = t * silu
            rows = lax.broadcasted_iota(jnp.int32, dconv.shape, 0)
            hl = halo[...]
            dc1 = jnp.where(rows == n - 1, hl[0:1], pltpu.roll(dconv, n - 1, 0))
            dc2 = jnp.where(rows == n - 2, hl[0:1], jnp.where(rows == n - 1, hl[1:2], pltpu.roll(dconv, n - 2, 0)))
            halo[...] = dconv[0:8]
            cw = cw_ref[...]
            du = cw[2:3] * dconv + cw[1:2] * dc1 + cw[0:1] * dc2
            u = cc * cx
            accc_ref[0:1, :] += jnp.sum(u * dc2, axis=0, keepdims=True)
            accc_ref[1:2, :] += jnp.sum(u * dc1, axis=0, keepdims=True)
            accc_ref[2:3, :] += jnp.sum(u * dconv, axis=0, keepdims=True)
            dcx = (du * cc).astype(BF16)
            dcc = (du * cx).astype(BF16)
            dhn = lax.dot_general(dpr_ref[rs, 0:BLK], w_ref[4], NT, preferred_element_type=F32)
            for j in range(1, 4):
                dhn += lax.dot_general(dpr_ref[rs, j * BLK : (j + 1) * BLK], w_ref[4 + j], NT, preferred_element_type=F32)
            for blk, d in ((1, dcb), (3, dcg), (0, dcx), (2, dcc)):
                dpc_ref[rs, blk * BLK : (blk + 1) * BLK] = d
                dhn += lax.dot_general(d, w_ref[blk], NT, preferred_element_type=F32)
            r = lax.rsqrt(jnp.mean(h * h, axis=-1, keepdims=True) + EPS)
            hh = h * r
            acc_ref[0:1, :] += jnp.sum(dhn * hh, axis=0, keepdims=True)
            dg = dhn * g_ref[...]
            return dh2_ref[rs, :] + r * (dg - hh * jnp.mean(dg * hh, axis=-1, keepdims=True))

        @pl.when(tile == 0)
        def _():
            dh = work(mc_ref[...], LIVE0)
            dm_ref[...] = dh[CHUNK - N_META : CHUNK]

        @pl.when(tile > 0)
        def _():
            gx_ref[...] = work(x_ref[...], 0)

    rev = lambda w, j=0: pl.BlockSpec((tm, w), lambda i: (nt - 1 - i, j))
    tok = pl.BlockSpec((tm, D_MODEL), lambda i: (_token_tile(nt - 1 - i), 0))
    const = lambda r, c: pl.BlockSpec((r, c), lambda i: (0, 0))
    return pl.pallas_call(
        body,
        name="b2_dconv_dh",
        grid=(nt,),
        in_specs=[rev(half, 1), rev(D_CONV), rev(half, 0), rev(D_CONV), _VMEM, _VMEM, tok, _VMEM, rev(D_MODEL), _VMEM],
        out_specs=(rev(half, 0), tok, const(N_META, D_MODEL), const(8, D_MODEL), const(8, D_CONV)),
        out_shape=(
            jax.ShapeDtypeStruct(dproj.shape, BF16),
            jax.ShapeDtypeStruct(x2.shape, F32),
            jax.ShapeDtypeStruct((N_META, D_MODEL), F32),
            jax.ShapeDtypeStruct((8, D_MODEL), F32),
            jax.ShapeDtypeStruct((8, D_CONV), F32),
        ),
        scratch_shapes=[pltpu.VMEM((8, D_CONV), F32)],
        input_output_aliases={0: 0},
        compiler_params=_params(),
    )(dproj, dco, proj, conv, w3, cw8, x2, meta_chunk, dh2, g1)


def _gw_in_scatter(hnT, dproj, gw_out_parts, sc_parts, ga_part, me_arr):
    n_rows = dproj.shape[0]
    last = N_DEV - 1
    by_dest = (True, True, False)

    def body(me_ref, a_ref, b_ref, go_ref, sc_ref, ga_ref, land_in, land_go, land_sc, land_ga,
             d2d_buf, d2d_land, ici_buf, d2d_send, d2d_recv, ici_send, ici_recv, send_sems, recv_sems, local_sems):
        del me_ref
        s = pl.program_id(0)
        t = last - s
        q = t >> 1
        x, y, c = lax.axis_index("x"), lax.axis_index("y"), lax.axis_index("c")
        me = 4 * x + 2 * y + c
        chip = 2 * x + y
        srcs, lands = (go_ref, sc_ref, ga_ref), (land_go, land_sc, land_ga)

        def peer_at(k):
            return (1 - x if k & 4 else x, 1 - y if k & 2 else y, 1 - c if k & 1 else c)

        def small_copy(a, k):
            px, py, pc = peer_at(k)
            return pltpu.make_async_remote_copy(
                src_ref=srcs[a].at[4 * px + 2 * py + pc] if by_dest[a] else srcs[a],
                dst_ref=lands[a].at[me],
                send_sem=send_sems.at[a * last + k - 1],
                recv_sem=recv_sems.at[a * last + k - 1],
                device_id=(px, py, pc),
                device_id_type=MESH,
            )

        def small_local(a):
            return pltpu.make_async_copy(srcs[a].at[me] if by_dest[a] else srcs[a], lands[a].at[me], local_sems.at[a + 1])

        def d2d_copy(j):
            return pltpu.make_async_remote_copy(
                src_ref=d2d_buf.at[j],
                dst_ref=d2d_land.at[j],
                send_sem=d2d_send.at[j],
                recv_sem=d2d_recv.at[j],
                device_id=(x, y, 1 - c),
                device_id_type=MESH,
            )

        def ici_copy(j, to):
            return pltpu.make_async_remote_copy(
                src_ref=ici_buf.at[j],
                dst_ref=land_in.at[chip],
                send_sem=ici_send.at[j],
                recv_sem=ici_recv.at[j],
                device_id=to,
                device_id_type=MESH,
            )

        def own_copy():
            return pltpu.make_async_copy(ici_buf.at[0], land_in.at[chip], local_sems.at[0])

        @pl.when(s == 0)
        def _():
            for a in range(3):
                small_local(a).start()
            for k in range(last, 0, -1):
                for a in range(3):
                    small_copy(a, k).start()

        blk = jnp.dot(a_ref[:, LIVE0:], b_ref[LIVE0:, :], preferred_element_type=F32)

        @pl.when((t & 1) == 1)
        def _():
            d2d_buf[q] = blk.astype(BF16)
            d2d_copy(q).start()

        @pl.when((t & 1) == 0)
        def _():
            d2d_copy(q).wait_recv()
            ici_buf[q] = (blk + d2d_land[q].astype(F32)).astype(BF16)

            @pl.when(t != 0)
            def _():
                ici_copy(q, (jnp.bitwise_xor(x, (t >> 2) & 1), jnp.bitwise_xor(y, (t >> 1) & 1), c)).start()

            @pl.when(t == 0)
            def _():
                own_copy().start()
                for j in range(N_CHIPS):
                    d2d_copy(j).wait_send()
                for j in range(N_CHIPS - 1, 0, -1):
                    ici_copy(j, peer_at(2 * j)).wait()
                for k in range(last, 0, -1):
                    for a in range(3):
                        small_copy(a, k).wait()
                for a in range(3):
                    small_local(a).wait()
                own_copy().wait()

    smalls = (gw_out_parts, sc_parts, ga_part)
    small_blocks = [p.shape[1:] if d else p.shape for p, d in zip(smalls, by_dest, strict=True)]
    grid_spec = pltpu.PrefetchScalarGridSpec(
        num_scalar_prefetch=1,
        grid=(N_DEV,),
        in_specs=[_VMEM, pl.BlockSpec((n_rows, BLK), lambda s, me: (0, jnp.bitwise_xor(me[0], last - s)))] + [_HBM] * 3,
        out_specs=tuple([_HBM] * 4),
        scratch_shapes=[
            pltpu.VMEM((N_CHIPS, D_MODEL, BLK), BF16),
            pltpu.VMEM((N_CHIPS, D_MODEL, BLK), BF16),
            pltpu.VMEM((N_CHIPS, D_MODEL, BLK), BF16),
            pltpu.SemaphoreType.DMA((N_CHIPS,)),
            pltpu.SemaphoreType.DMA((N_CHIPS,)),
            pltpu.SemaphoreType.DMA((N_CHIPS,)),
            pltpu.SemaphoreType.DMA((N_CHIPS,)),
            pltpu.SemaphoreType.DMA((3 * last,)),
            pltpu.SemaphoreType.DMA((3 * last,)),
            pltpu.SemaphoreType.DMA((4,)),
        ],
    )
    return pl.pallas_call(
        body,
        name="gw_in_scatter",
        grid_spec=grid_spec,
        out_shape=(jax.ShapeDtypeStruct((N_CHIPS, D_MODEL, BLK), BF16),)
        + tuple(jax.ShapeDtypeStruct((N_DEV, *b), p.dtype) for b, p in zip(small_blocks, smalls, strict=True)),
        compiler_params=_params(),
    )(me_arr, hnT, dproj, gw_out_parts, sc_parts, ga_part)


def _gw_out(coT, roT, dh2):
    n_rows = dh2.shape[0]
    tk = ROW_TILE
    nk = n_rows // tk - 1

    def body(c_ref, r_ref, d_ref, out_ref, acc):
        k = pl.program_id(0)

        @pl.when(k == 0)
        def _():
            acc[...] = jnp.zeros_like(acc)

        d = d_ref[...].astype(BF16)
        acc[0:D_CONV, :] += jnp.dot(c_ref[...], d, preferred_element_type=F32)
        acc[D_CONV:, :] += jnp.dot(r_ref[...], d, preferred_element_type=F32)

        @pl.when(k == nk - 1)
        def _():
            out_ref[...] = acc[...].astype(BF16)

    return pl.pallas_call(
        body,
        name="gw_out",
        grid=(nk,),
        in_specs=[
            pl.BlockSpec((D_CONV, tk), lambda k: (0, k + 1)),
            pl.BlockSpec((D_RET, tk), lambda k: (0, k + 1)),
            pl.BlockSpec((tk, D_MODEL), lambda k: (k + 1, 0)),
        ],
        out_specs=pl.BlockSpec((D_MODEL, D_MODEL), lambda k: (0, 0)),
        out_shape=jax.ShapeDtypeStruct((D_MODEL, D_MODEL), BF16),
        scratch_shapes=[pltpu.VMEM((D_MODEL, D_MODEL), F32)],
        compiler_params=_params(),
    )(coT, roT, dh2)


def _adamw(w, g, m, v):
    m = ADAM_B1 * m + (1.0 - ADAM_B1) * g
    v = ADAM_B2 * v + (1.0 - ADAM_B2) * (g * g)
    m_hat = m / (1.0 - ADAM_B1**ADAM_STEP)
    v_hat = v / (1.0 - ADAM_B2**ADAM_STEP)
    delta = -ADAM_LR * (m_hat / (jnp.sqrt(v_hat) + ADAM_EPS) + ADAM_WD * w)
    return delta, m, v


def _sum_adamw(name, parts, w, m, v, rows_per_step):
    n_r, n_c = w.shape
    n_parts = parts.shape[0]
    tr = rows_per_step

    def body(p_ref, w_ref, m_ref, v_ref, g_ref, d_ref, nm_ref, nv_ref):
        g = p_ref[0].astype(F32)
        for s in range(1, n_parts):
            g = g + p_ref[s].astype(F32)
        g_ref[...] = g
        d_ref[...], nm_ref[...], nv_ref[...] = _adamw(w_ref[...], g, m_ref[...], v_ref[...])

    blk = pl.BlockSpec((tr, n_c), lambda i: (i, 0))
    return pl.pallas_call(
        body,
        name=name,
        grid=(n_r // tr,),
        in_specs=[pl.BlockSpec((n_parts, tr, n_c), lambda i: (0, i, 0)), blk, blk, blk],
        out_specs=(blk,) * 4,
        out_shape=(jax.ShapeDtypeStruct((n_r, n_c), F32),) * 4,
        compiler_params=_params(),
    )(parts, w, m, v)


def _small_leaves(meta, conv_w, n1, rg, fg):
    cols = D_CONV // N_DEV
    return (
        meta,
        jnp.pad(conv_w, ((0, 5), (0, CHUNK - cols))),
        n1.reshape(8, CHUNK),
        rg.reshape(4, CHUNK),
        fg.reshape(8, CHUNK),
    )


def _from_small_leaves(meta, conv8, n1, rg, fg):
    return meta, conv8[0:3, : D_CONV // N_DEV], n1.reshape(D_MODEL), rg.reshape(D_RET), fg.reshape(D_MODEL)


def _adamw_small(land_sc, land_ga, w, m, v):
    n_leaf = 5

    def body(sc_ref, ga_ref, *refs):
        ins, outs = refs[: 3 * n_leaf], refs[3 * n_leaf :]
        sc, ga = sc_ref[0], ga_ref[0]
        for s in range(1, N_DEV):
            sc = sc + sc_ref[s]
            ga = ga + ga_ref[s]
        grads = (
            sc[SC_META0 : SC_META0 + N_META],
            sc[SC_CONV0 : SC_CONV0 + 8],
            ga[GA_N1 : GA_N1 + 8],
            ga[GA_RG : GA_RG + 4],
            ga[GA_FG : GA_FG + 8],
        )
        for leaf, g in enumerate(grads):
            d, nm, nv = _adamw(ins[leaf][...], g, ins[n_leaf + leaf][...], ins[2 * n_leaf + leaf][...])
            outs[leaf][...] = g
            outs[n_leaf + leaf][...] = d
            outs[2 * n_leaf + leaf][...] = nm
            outs[3 * n_leaf + leaf][...] = nv
        outs[4 * n_leaf][...] = ga[GA_LOSS : GA_LOSS + 1]

    leaf_shapes = tuple(jax.ShapeDtypeStruct(a.shape, F32) for a in w)
    out = pl.pallas_call(
        body,
        name="adamw_small",
        out_shape=leaf_shapes * 4 + (jax.ShapeDtypeStruct((1, CHUNK), F32),),
    )(land_sc, land_ga, *w, *m, *v)
    return tuple(out[k * n_leaf : (k + 1) * n_leaf] for k in range(4)), out[4 * n_leaf]


def kernel(x, meta, norm1_g, w_in, conv_w, ret_norm_g, w_out, final_g, loss_target, m_meta, m_norm1_g, m_w_in, m_conv_w, m_ret_norm_g, m_w_out, m_final_g, v_meta, v_norm1_g, v_w_in, v_conv_w, v_ret_norm_g, v_w_out, v_final_g):
    seq = x.shape[1]
    assert x.shape == (1, seq, D_MODEL) and seq % ROW_TILE == 0
    n_tiles = seq // ROW_TILE + 1
    cols = D_CONV // N_DEV
    x2, t2 = x[0], loss_target[0]

    small_w = _small_leaves(meta, conv_w, norm1_g, ret_norm_g, final_g)
    w3, wo3, small = _gather(
        "gather_weights", [w_in.astype(BF16), w_out.astype(BF16), jnp.concatenate(small_w[0:2], axis=0)]
    )
    wout = wo3.reshape(D_MODEL, D_MODEL)
    meta_full = small[:, SC_META0 : SC_META0 + N_META, :].transpose(1, 0, 2).reshape(N_META, D_MODEL)
    conv_full = small[:, SC_CONV0 : SC_CONV0 + 3, :cols].transpose(1, 0, 2).reshape(3, D_CONV)
    cw8 = jnp.pad(conv_full, ((0, 5), (0, 0)))
    meta_chunk = jnp.pad(meta_full, ((CHUNK - N_META, 0), (0, 0)))

    rot = _rotary_tables(n_tiles, ROW_TILE)
    dec, dect, zeta, xi = _decay_tables()
    g1 = norm1_g.reshape(1, D_MODEL)
    gr = ret_norm_g.reshape(1, D_RET)
    gf = final_g.reshape(1, D_MODEL)

    hnT, proj, conv_out, conv_outT, conv = _f1(x2, meta_chunk, g1, w3, cw8, rot)
    dh2, o, ret_outT, states, acc_f2 = _f2(proj, conv_out, x2, t2, wout, gr, gf, dec, zeta, xi)
    dproj_ret, dco, acc_b1 = _b1(dh2, proj, o, states, wout, gr, dec, dect, zeta, xi, rot)
    dproj, grad_x2, dmeta_rows, acc_b2, acc_conv = _b2(dproj_ret, dco, proj, conv, w3, cw8, x2, meta_chunk, dh2, g1)
    gw_out_parts = _gw_out(conv_outT, ret_outT, dh2).reshape(N_DEV, D_MODEL // N_DEV, D_MODEL)

    dmeta = dmeta_rows.reshape(N_META, N_DEV, CHUNK).transpose(1, 0, 2)
    dconv = jnp.pad(acc_conv[0:3, :].reshape(3, N_DEV, cols).transpose(1, 0, 2), ((0, 0), (0, 5), (0, CHUNK - cols)))
    sc_parts = jnp.concatenate([dmeta, dconv], axis=1)
    ga_part = jnp.concatenate(
        [
            acc_b2[0].reshape(8, CHUNK),
            jnp.pad(acc_b1[0].reshape(4, CHUNK), ((0, 4), (0, 0))),
            acc_f2[0].reshape(8, CHUNK),
            jnp.pad(acc_f2[1:2, 0:CHUNK], ((0, 7), (0, 0))),
        ],
        axis=0,
    )

    me_arr = (4 * lax.axis_index("x") + 2 * lax.axis_index("y") + lax.axis_index("c")).astype(jnp.int32).reshape(1)
    land_in, land_out, land_sc, land_ga = _gw_in_scatter(hnT, dproj, gw_out_parts, sc_parts, ga_part, me_arr)

    g_w_in, d_w_in, nm_w_in, nv_w_in = _sum_adamw("adamw_w_in", land_in, w_in, m_w_in, v_w_in, 256)
    g_w_out, d_w_out, nm_w_out, nv_w_out = _sum_adamw("adamw_w_out", land_out, w_out, m_w_out, v_w_out, 64)
    small_out, loss_row = _adamw_small(
        land_sc,
        land_ga,
        small_w,
        _small_leaves(m_meta, m_conv_w, m_norm1_g, m_ret_norm_g, m_final_g),
        _small_leaves(v_meta, v_conv_w, v_norm1_g, v_ret_norm_g, v_final_g),
    )
    loss = loss_row[0, 0]
    grad_x = grad_x2.reshape(1, seq, D_MODEL)

    def leaves(w_in_leaf, w_out_leaf, small_leaves):
        meta_leaf, conv_leaf, n1_leaf, rg_leaf, fg_leaf = _from_small_leaves(*small_leaves)
        return (meta_leaf, n1_leaf, w_in_leaf, conv_leaf, rg_leaf, w_out_leaf, fg_leaf)

    return (
        loss,
        grad_x,
        *leaves(g_w_in, g_w_out, small_out[0]),
        *leaves(d_w_in, d_w_out, small_out[1]),
        *leaves(nm_w_in, nm_w_out, small_out[2]),
        *leaves(nv_w_in, nv_w_out, small_out[3]),
    )
```

```python
import functools
import math

import jax
import jax.numpy as jnp
import numpy as np
from jax import lax
from jax.experimental import pallas as pl
from jax.experimental.pallas import tpu as pltpu

F32 = jnp.float32
BF16 = jnp.bfloat16

N_DEV = 8
N_CHIPS = 4
D_MODEL = 1024
N_META = 16
CHUNK = 128
D_CONV = 512
D_RET = 512
N_HEADS = 4
HEAD_DIM = 128
N_PROJ = 8
BLK = 512
ROPE_BASE = 10000.0
EPS = 1e-6
Q_SCALE = HEAD_DIM ** -0.5
LOG_G = tuple(math.log(1.0 - 2.0 ** (-5.0 - h)) for h in range(N_HEADS))
CHUNK_DECAY = tuple(math.exp(CHUNK * lg) for lg in LOG_G)

ADAM_LR = 0.001
ADAM_B1 = 0.9
ADAM_B2 = 0.999
ADAM_EPS = 1e-08
ADAM_WD = 0.01
ADAM_STEP = 10

ROW_TILE = 512
PAD_ROWS = ROW_TILE - N_META
LIVE0 = ROW_TILE - CHUNK
VMEM_LIMIT = 56 * 1024 * 1024

SC_META0, SC_CONV0 = 0, 16
GA_N1, GA_RG, GA_FG, GA_LOSS = 0, 8, 16, 24

NT = (((1,), (1,)), ((), ()))
TN = (((0,), (0,)), ((), ()))
MESH = pl.DeviceIdType.MESH

_VMEM = pl.BlockSpec(memory_space=pltpu.VMEM)
_HBM = pl.BlockSpec(memory_space=pltpu.HBM)


def _params(n_axes=1):
    return pltpu.CompilerParams(dimension_semantics=("arbitrary",) * n_axes, vmem_limit_bytes=VMEM_LIMIT)


def _sigmoid(x):
    return 0.5 * jnp.tanh(0.5 * x) + 0.5


def _decay_tables():
    idx = np.arange(CHUNK, dtype=np.float64)
    diff = idx[:, None] - idx[None, :]
    dec = np.stack([np.where(diff >= 0, np.exp(diff * lg), 0.0) for lg in LOG_G])
    zeta = np.stack([np.exp((CHUNK - 1 - idx) * lg) for lg in LOG_G])
    xi = np.stack([np.exp((idx + 1.0) * lg) for lg in LOG_G])
    ones = np.ones((1, 1, CHUNK))
    return (
        jnp.asarray(dec, F32),
        jnp.asarray(dec.transpose(0, 2, 1), F32),
        jnp.asarray(zeta[:, :, None] * ones, F32),
        jnp.asarray(xi[:, :, None] * ones, F32),
    )


def _rotary_tables(n_tiles, tm):
    half = HEAD_DIM // 2
    freqs = (1.0 / (np.float32(ROPE_BASE) ** (np.arange(half, dtype=np.float32) / np.float32(half)))).astype(np.float64)
    sign = np.concatenate([-np.ones(half), np.ones(half)])
    two = lambda a: np.concatenate([a, a], axis=1)
    base = two((np.arange(n_tiles, dtype=np.float64) * tm - PAD_ROWS)[:, None] * freqs[None, :])
    off = two(np.arange(tm, dtype=np.float64)[:, None] * freqs[None, :])
    as32 = lambda a: jnp.asarray(a, F32)
    return as32(np.cos(base)), as32(np.sin(base) * sign), as32(np.cos(off)), as32(np.sin(off) * sign)


def _tile_rotary(ca_ref, sa_ref, cb_ref, sb_ref, tile):
    ca, sa = ca_ref[pl.ds(tile, 1), :], sa_ref[pl.ds(tile, 1), :]
    cb, sb = cb_ref[...], sb_ref[...]
    return ca * cb - sa * sb, sa * cb + ca * sb


def _rot(t, cos2, sin2):
    return t * cos2 + pltpu.roll(t, HEAD_DIM // 2, 1) * sin2


def _rot_bwd(d, cos2, sin2):
    return d * cos2 + pltpu.roll(d * sin2, HEAD_DIM // 2, 1)


def _token_tile(i):
    return jnp.maximum(i - 1, 0)


GATHER_PAIRS = 7


def _two_level_gather(src_ref, out_ref, send_sems, recv_sems, local_sem, base=0):
    x, y, c = lax.axis_index("x"), lax.axis_index("y"), lax.axis_index("c")
    me, sibling = (x, y, c), (x, y, 1 - c)
    chips = [(1 - x, y), (x, 1 - y), (1 - x, 1 - y)]

    def copy(k, block, to, src=None):
        dst = out_ref.at[4 * block[0] + 2 * block[1] + block[2]]
        return pltpu.make_async_remote_copy(
            src_ref=dst if src is None else src,
            dst_ref=dst,
            send_sem=send_sems.at[base + k],
            recv_sem=recv_sems.at[base + k],
            device_id=to,
            device_id_type=MESH,
        )

    mine = pltpu.make_async_copy(src_ref, out_ref.at[4 * x + 2 * y + c], local_sem)
    first = [copy(1 + j, me, (*chip, c), src=src_ref) for j, chip in enumerate(chips)]
    first.append(copy(0, me, sibling, src=src_ref))
    passed = [copy(4 + j, (*chip, c), sibling) for j, chip in enumerate(chips)]

    def start():
        mine.start()
        for cp in first:
            cp.start()

    def forward():
        for j, chip in enumerate(chips):
            copy(1 + j, (*chip, c), me).wait_recv()
            passed[j].start()

    def finish():
        copy(0, sibling, me).wait_recv()
        for j, chip in enumerate(chips):
            copy(4 + j, (*chip, 1 - c), me).wait_recv()
        for cp in first + passed:
            cp.wait_send()
        mine.wait()

    return start, forward, finish


def _gather(name, srcs):
    n = len(srcs)
    per = GATHER_PAIRS

    def body(*refs):
        src_refs, out_refs = refs[:n], refs[n : 2 * n]
        send_sems, recv_sems, local_sems = refs[2 * n :]
        parts = [
            _two_level_gather(src_refs[a], out_refs[a], send_sems, recv_sems, local_sems.at[a], a * per)
            for a in range(n)
        ]
        for phase in range(3):
            for part in parts:
                part[phase]()

    return pl.pallas_call(
        body,
        name=name,
        out_shape=tuple(jax.ShapeDtypeStruct((N_DEV, *s.shape), s.dtype) for s in srcs),
        in_specs=[_HBM] * n,
        out_specs=tuple([_HBM] * n),
        scratch_shapes=[
            pltpu.SemaphoreType.DMA((n * per,)),
            pltpu.SemaphoreType.DMA((n * per,)),
            pltpu.SemaphoreType.DMA((n,)),
        ],
    )(*srcs)


def _f1(x2, meta_chunk, g1, w3, cw8, rot, wo_shard):
    tm = ROW_TILE
    nt = x2.shape[0] // tm + 1
    n_rows = nt * tm

    def body(x_ref, mc_ref, g_ref, w_ref, cw_ref, ca_ref, sa_ref, cb_ref, sb_ref, wo_ref,
             hnT_ref, pr_ref, co_ref, cv_ref, wo_all_ref, halo, send_sems, recv_sems, local_sem):
        i = pl.program_id(0)
        wo_start, wo_forward, wo_finish = _two_level_gather(wo_ref, wo_all_ref, send_sems, recv_sems, local_sem)
        pl.when(i == 0)(wo_start)
        pl.when(i == nt // 2)(wo_forward)
        pl.when(i == nt - 1)(wo_finish)

        def work(h, r0):
            rs = slice(r0, tm)
            n = tm - r0
            r = lax.rsqrt(jnp.mean(h * h, axis=-1, keepdims=True) + EPS)
            hn = (h * r * g_ref[...]).astype(BF16)
            hnT_ref[:, rs] = hn.T

            def proj(j):
                return jnp.dot(hn, w_ref[j], preferred_element_type=F32)

            cx, cb, cc, cg = proj(0), proj(1), proj(2), proj(3)
            u = cc * cx
            rows = lax.broadcasted_iota(jnp.int32, u.shape, 0)
            hl = halo[...]
            u1 = jnp.where(rows == 0, hl[7:8], pltpu.roll(u, 1, 0))
            u2 = jnp.where(rows == 0, hl[6:7], jnp.where(rows == 1, hl[7:8], pltpu.roll(u, 2, 0)))
            halo[...] = u[n - 8 : n]
            cw = cw_ref[...]
            conv = cw[0:1] * u2 + cw[1:2] * u1 + cw[2:3] * u
            co = (cb * conv * (cg * _sigmoid(cg))).astype(BF16)
            co_ref[rs, :] = co
            cv_ref[rs, :] = conv.astype(BF16)
            pr_ref[rs, 0 * BLK : 1 * BLK] = cx.astype(BF16)
            pr_ref[rs, 1 * BLK : 2 * BLK] = cb.astype(BF16)
            pr_ref[rs, 2 * BLK : 3 * BLK] = cc.astype(BF16)
            pr_ref[rs, 3 * BLK : 4 * BLK] = cg.astype(BF16)
            cos_all, sin_all = _tile_rotary(ca_ref, sa_ref, cb_ref, sb_ref, i)
            cos_t, sin_t = cos_all[rs, :], sin_all[rs, :]
            q, k = proj(4), proj(5)
            for hd in range(N_HEADS):
                c0 = hd * HEAD_DIM
                pr_ref[rs, 4 * BLK + c0 : 4 * BLK + c0 + HEAD_DIM] = (
                    _rot(q[:, c0 : c0 + HEAD_DIM], cos_t, sin_t) * Q_SCALE
                ).astype(BF16)
                pr_ref[rs, 5 * BLK + c0 : 5 * BLK + c0 + HEAD_DIM] = _rot(
                    k[:, c0 : c0 + HEAD_DIM], cos_t, sin_t
                ).astype(BF16)
            pr_ref[rs, 6 * BLK : 7 * BLK] = proj(6).astype(BF16)
            pr_ref[rs, 7 * BLK : 8 * BLK] = proj(7).astype(BF16)

        @pl.when(i == 0)
        def _():
            halo[...] = jnp.zeros_like(halo)
            work(mc_ref[...], LIVE0)

        @pl.when(i > 0)
        def _():
            work(x_ref[...], 0)

    row = lambda w: pl.BlockSpec((tm, w), lambda i: (i, 0))
    col = lambda w: pl.BlockSpec((w, tm), lambda i: (0, i))
    return pl.pallas_call(
        body,
        name="f1_inproj_conv",
        grid=(nt,),
        in_specs=[pl.BlockSpec((tm, D_MODEL), lambda i: (_token_tile(i), 0))] + [_VMEM] * 8 + [_HBM],
        out_specs=(col(D_MODEL), row(N_PROJ * BLK), row(D_CONV), row(D_CONV), _HBM),
        out_shape=(
            jax.ShapeDtypeStruct((D_MODEL, n_rows), BF16),
            jax.ShapeDtypeStruct((n_rows, N_PROJ * BLK), BF16),
            jax.ShapeDtypeStruct((n_rows, D_CONV), BF16),
            jax.ShapeDtypeStruct((n_rows, D_CONV), BF16),
            jax.ShapeDtypeStruct((N_DEV, *wo_shard.shape), wo_shard.dtype),
        ),
        scratch_shapes=[
            pltpu.VMEM((8, D_CONV), F32),
            pltpu.SemaphoreType.DMA((GATHER_PAIRS,)),
            pltpu.SemaphoreType.DMA((GATHER_PAIRS,)),
            pltpu.SemaphoreType.DMA(()),
        ],
        compiler_params=_params(),
    )(x2, meta_chunk, g1, w3, cw8, *rot, wo_shard)


def _group_norm(o):
    ys, rs = [], []
    for hd in range(N_HEADS):
        oh = o[:, hd * HEAD_DIM : (hd + 1) * HEAD_DIM]
        xc = oh - jnp.mean(oh, axis=-1, keepdims=True)
        rstd = lax.rsqrt(jnp.mean(xc * xc, axis=-1, keepdims=True) + EPS)
        ys.append(xc * rstd)
        rs.append(jnp.broadcast_to(rstd, oh.shape))
    return jnp.concatenate(ys, axis=1), jnp.concatenate(rs, axis=1)


def _f2(proj, conv_out, x2, t2, wout, gr, gf, dec, zeta, xi):
    n_rows = proj.shape[0]
    tm = ROW_TILE
    nt = n_rows // tm
    nct = tm // CHUNK

    def body(q_ref, k_ref, v_ref, rg_ref, co_ref, x_ref, t_ref, wo_ref, gr_ref, gf_ref, dec_ref, zeta_ref,
             xi_ref, dh2_ref, o_ref, st_ref, gw_ref, acc_ref, state, obuf, gacc):
        i = pl.program_id(0)

        def mixer(r0):
            for c in range(r0 // CHUNK, nct):
                rs = slice(c * CHUNK, (c + 1) * CHUNK)
                for hd in range(N_HEADS):
                    cs = slice(hd * HEAD_DIM, (hd + 1) * HEAD_DIM)
                    q, k, v = q_ref[rs, cs], k_ref[rs, cs], v_ref[rs, cs]
                    st = state[hd]
                    st_bf = st.astype(BF16)
                    st_ref[c, hd] = st_bf
                    s = lax.dot_general(q, k, NT, preferred_element_type=F32) * dec_ref[hd]
                    inner = jnp.dot(s.astype(BF16), v, preferred_element_type=F32)
                    qx = (q.astype(F32) * xi_ref[hd]).astype(BF16)
                    obuf[rs, cs] = inner + jnp.dot(qx, st_bf, preferred_element_type=F32)
                    kz = (k.astype(F32) * zeta_ref[hd]).astype(BF16)
                    state[hd] = CHUNK_DECAY[hd] * st + lax.dot_general(kz, v, TN, preferred_element_type=F32)
            o = obuf[r0:tm, :]
            o_ref[r0:tm, :] = o.astype(BF16)
            yh, _ = _group_norm(o)
            rg = rg_ref[r0:tm, :].astype(F32)
            return (yh * gr_ref[...] * (rg * _sigmoid(rg))).astype(BF16)

        @pl.when(i == 0)
        def _():
            state[...] = jnp.zeros_like(state)
            acc_ref[...] = jnp.zeros_like(acc_ref)
            gacc[...] = jnp.zeros_like(gacc)
            mixer(LIVE0)
            dh2_ref[...] = jnp.zeros_like(dh2_ref)

        @pl.when(i > 0)
        def _():
            ro = mixer(0)
            h2 = (
                x_ref[...]
                + jnp.dot(co_ref[...], wo_ref[0:D_CONV], preferred_element_type=F32)
                + jnp.dot(ro, wo_ref[D_CONV:], preferred_element_type=F32)
            )
            r2 = lax.rsqrt(jnp.mean(h2 * h2, axis=-1, keepdims=True) + EPS)
            yn = h2 * r2
            gfv = gf_ref[...]
            err = yn * gfv - t_ref[...]
            tile_loss = jnp.sum(jnp.sum(err * err, axis=-1, keepdims=True), axis=0, keepdims=True) * (0.5 / D_MODEL)
            acc_ref[0:1, :] += jnp.sum(err * yn, axis=0, keepdims=True) * (1.0 / D_MODEL)
            acc_ref[1:2, :] += tile_loss
            dyn = err * (gfv * (1.0 / D_MODEL))
            dh2 = r2 * (dyn - yn * jnp.mean(dyn * yn, axis=-1, keepdims=True))
            dh2_ref[...] = dh2
            dh2_bf = dh2.astype(BF16)
            gacc[0:D_CONV, :] += lax.dot_general(co_ref[...], dh2_bf, TN, preferred_element_type=F32)
            gacc[D_CONV:, :] += lax.dot_general(ro, dh2_bf, TN, preferred_element_type=F32)

        @pl.when(i == nt - 1)
        def _():
            gw_ref[...] = gacc[...].astype(BF16)

    row = lambda w, j=0: pl.BlockSpec((tm, w), lambda i: (i, j))
    tok = pl.BlockSpec((tm, D_MODEL), lambda i: (_token_tile(i), 0))
    return pl.pallas_call(
        body,
        name="f2_retention_out",
        grid=(nt,),
        in_specs=[row(BLK, 4), row(BLK, 5), row(BLK, 6), row(BLK, 7), row(D_CONV), tok, tok] + [_VMEM] * 6,
        out_specs=(
            row(D_MODEL),
            row(D_RET),
            pl.BlockSpec((nct, N_HEADS, HEAD_DIM, HEAD_DIM), lambda i: (i, 0, 0, 0)),
            pl.BlockSpec((D_MODEL, D_MODEL), lambda i: (0, 0)),
            pl.BlockSpec((8, D_MODEL), lambda i: (0, 0)),
        ),
        out_shape=(
            jax.ShapeDtypeStruct((n_rows, D_MODEL), F32),
            jax.ShapeDtypeStruct((n_rows, D_RET), BF16),
            jax.ShapeDtypeStruct((n_rows // CHUNK, N_HEADS, HEAD_DIM, HEAD_DIM), BF16),
            jax.ShapeDtypeStruct((D_MODEL, D_MODEL), BF16),
            jax.ShapeDtypeStruct((8, D_MODEL), F32),
        ),
        scratch_shapes=[
            pltpu.VMEM((N_HEADS, HEAD_DIM, HEAD_DIM), F32),
            pltpu.VMEM((tm, D_RET), F32),
            pltpu.VMEM((D_MODEL, D_MODEL), F32),
        ],
        compiler_params=_params(),
    )(proj, proj, proj, proj, conv_out, x2, t2, wout, gr, gf, dec, zeta, xi)


def _b1(dh2, proj, o, states, wout, gr, dec, dect, zeta, xi, rot):
    n_rows = dh2.shape[0]
    tm = ROW_TILE
    nt = n_rows // tm
    nct = tm // CHUNK

    def body(dh2_ref, pr_ref, o_ref, st_ref, wo_ref, gr_ref, dec_ref, dect_ref, zeta_ref, xi_ref,
             ca_ref, sa_ref, cb_ref, sb_ref, dp_ref, dco_ref, acc_ref, dstate, dobuf):
        i = pl.program_id(0)
        tile = nt - 1 - i
        cos_all, sin_all = _tile_rotary(ca_ref, sa_ref, cb_ref, sb_ref, tile)

        @pl.when(i == 0)
        def _():
            dstate[...] = jnp.zeros_like(dstate)
            acc_ref[...] = jnp.zeros_like(acc_ref)

        @pl.when(tile > 0)
        def _():
            dmix = lax.dot_general(dh2_ref[...].astype(BF16), wo_ref[...], NT, preferred_element_type=F32)
            dco_ref[...] = dmix[:, :D_CONV].astype(BF16)
            dro = dmix[:, D_CONV:]
            rg = pr_ref[:, 3 * BLK : 4 * BLK].astype(F32)
            sg = _sigmoid(rg)
            silu, dsilu = rg * sg, sg * (1.0 + rg * (1.0 - sg))
            yh, rstd = _group_norm(o_ref[...].astype(F32))
            grv = gr_ref[...]
            dp_ref[:, 3 * BLK : 4 * BLK] = (dro * (yh * grv) * dsilu).astype(BF16)
            dret = dro * silu
            acc_ref[0:1, :] += jnp.sum(dret * yh, axis=0, keepdims=True)
            dyh = dret * grv
            for hd in range(N_HEADS):
                cs = slice(hd * HEAD_DIM, (hd + 1) * HEAD_DIM)
                a, b = dyh[:, cs], yh[:, cs]
                dobuf[:, cs] = rstd[:, cs] * (
                    a - jnp.mean(a, axis=-1, keepdims=True) - b * jnp.mean(a * b, axis=-1, keepdims=True)
                )

            for c in reversed(range(nct)):
                rs = slice(c * CHUNK, (c + 1) * CHUNK)
                cos_t, sin_t = cos_all[rs, :], sin_all[rs, :]
                for hd in range(N_HEADS):
                    cs = slice(hd * HEAD_DIM, (hd + 1) * HEAD_DIM)
                    q = pr_ref[rs, hd * HEAD_DIM : (hd + 1) * HEAD_DIM]
                    k = pr_ref[rs, BLK + hd * HEAD_DIM : BLK + (hd + 1) * HEAD_DIM]
                    v = pr_ref[rs, 2 * BLK + hd * HEAD_DIM : 2 * BLK + (hd + 1) * HEAD_DIM]
                    do = dobuf[rs, cs].astype(BF16)
                    st_bf = st_ref[c, hd]
                    dst = dstate[hd]
                    dst_bf = dst.astype(BF16)
                    zt, xt = zeta_ref[hd], xi_ref[hd]
                    sT = (lax.dot_general(k, q, NT, preferred_element_type=F32) * dect_ref[hd]).astype(BF16)
                    dsT = (lax.dot_general(v, do, NT, preferred_element_type=F32) * dect_ref[hd]).astype(BF16)
                    ds = (lax.dot_general(do, v, NT, preferred_element_type=F32) * dec_ref[hd]).astype(BF16)
                    kz = (k.astype(F32) * zt).astype(BF16)
                    dv = jnp.dot(sT, do, preferred_element_type=F32) + jnp.dot(kz, dst_bf, preferred_element_type=F32)
                    dq = jnp.dot(ds, k, preferred_element_type=F32) + xt * lax.dot_general(
                        do, st_bf, NT, preferred_element_type=F32
                    )
                    dk = jnp.dot(dsT, q, preferred_element_type=F32) + zt * lax.dot_general(
                        v, dst_bf, NT, preferred_element_type=F32
                    )
                    qx = (q.astype(F32) * xt).astype(BF16)
                    dstate[hd] = CHUNK_DECAY[hd] * dst + lax.dot_general(qx, do, TN, preferred_element_type=F32)
                    dp_ref[rs, hd * HEAD_DIM : (hd + 1) * HEAD_DIM] = _rot_bwd(dq * Q_SCALE, cos_t, sin_t).astype(BF16)
                    dp_ref[rs, BLK + hd * HEAD_DIM : BLK + (hd + 1) * HEAD_DIM] = _rot_bwd(dk, cos_t, sin_t).astype(BF16)
                    dp_ref[rs, 2 * BLK + hd * HEAD_DIM : 2 * BLK + (hd + 1) * HEAD_DIM] = dv.astype(BF16)

        @pl.when(tile == 0)
        def _():
            rs = slice(LIVE0, tm)
            cos_t, sin_t = cos_all[rs, :], sin_all[rs, :]
            zeros = jnp.zeros((tm - LIVE0, HEAD_DIM), BF16)
            for hd in range(N_HEADS):
                k = pr_ref[rs, BLK + hd * HEAD_DIM : BLK + (hd + 1) * HEAD_DIM]
                v = pr_ref[rs, 2 * BLK + hd * HEAD_DIM : 2 * BLK + (hd + 1) * HEAD_DIM]
                dst_bf = dstate[hd].astype(BF16)
                zt = zeta_ref[hd]
                kz = (k.astype(F32) * zt).astype(BF16)
                dv = jnp.dot(kz, dst_bf, preferred_element_type=F32)
                dk = zt * lax.dot_general(v, dst_bf, NT, preferred_element_type=F32)
                dp_ref[rs, hd * HEAD_DIM : (hd + 1) * HEAD_DIM] = zeros
                dp_ref[rs, BLK + hd * HEAD_DIM : BLK + (hd + 1) * HEAD_DIM] = _rot_bwd(dk, cos_t, sin_t).astype(BF16)
                dp_ref[rs, 2 * BLK + hd * HEAD_DIM : 2 * BLK + (hd + 1) * HEAD_DIM] = dv.astype(BF16)
                dp_ref[rs, 3 * BLK + hd * HEAD_DIM : 3 * BLK + (hd + 1) * HEAD_DIM] = zeros
            dco_ref[rs, :] = jnp.zeros((tm - LIVE0, D_CONV), BF16)

    half = N_PROJ * BLK // 2
    rev = lambda w, j=0: pl.BlockSpec((tm, w), lambda i: (nt - 1 - i, j))
    return pl.pallas_call(
        body,
        name="b1_dret",
        grid=(nt,),
        in_specs=[
            rev(D_MODEL),
            rev(half, 1),
            rev(D_RET),
            pl.BlockSpec((nct, N_HEADS, HEAD_DIM, HEAD_DIM), lambda i: (nt - 1 - i, 0, 0, 0)),
        ]
        + [_VMEM] * 10,
        out_specs=(rev(half, 1), rev(D_CONV), pl.BlockSpec((8, D_RET), lambda i: (0, 0))),
        out_shape=(
            jax.ShapeDtypeStruct((n_rows, N_PROJ * BLK), BF16),
            jax.ShapeDtypeStruct((n_rows, D_CONV), BF16),
            jax.ShapeDtypeStruct((8, D_RET), F32),
        ),
        scratch_shapes=[pltpu.VMEM((N_HEADS, HEAD_DIM, HEAD_DIM), F32), pltpu.VMEM((tm, D_RET), F32)],
        compiler_params=_params(),
    )(dh2, proj, o, states, wout, gr, dec, dect, zeta, xi, *rot)


def _b2(dproj, dco, proj, conv, w3, cw8, x2, meta_chunk, dh2, g1):
    n_rows = dproj.shape[0]
    tm = ROW_TILE
    nt = n_rows // tm
    half = N_PROJ * BLK // 2

    def body(dpr_ref, dco_ref, pr_ref, cv_ref, w_ref, cw_ref, x_ref, mc_ref, dh2_ref, g_ref,
             dpc_ref, gx_ref, dm_ref, acc_ref, accc_ref, halo):
        i = pl.program_id(0)
        tile = nt - 1 - i

        @pl.when(i == 0)
        def _():
            acc_ref[...] = jnp.zeros_like(acc_ref)
            accc_ref[...] = jnp.zeros_like(accc_ref)
            halo[...] = jnp.zeros_like(halo)

        def work(h, r0):
            rs = slice(r0, tm)
            n = tm - r0
            dco = dco_ref[rs, :].astype(F32)
            cx = pr_ref[rs, 0 * BLK : 1 * BLK].astype(F32)
            cb = pr_ref[rs, 1 * BLK : 2 * BLK].astype(F32)
            cc = pr_ref[rs, 2 * BLK : 3 * BLK].astype(F32)
            cg = pr_ref[rs, 3 * BLK : 4 * BLK].astype(F32)
            conv = cv_ref[rs, :].astype(F32)
            sg = _sigmoid(cg)
            silu, dsilu = cg * sg, sg * (1.0 + cg * (1.0 - sg))
            t = dco * cb
            dcb = (dco * conv * silu).astype(BF16)
            dcg = (t * conv * dsilu).astype(BF16)
            dconv = t * silu
            rows = lax.broadcasted_iota(jnp.int32, dconv.shape, 0)
            hl = halo[...]
            dc1 = jnp.where(rows == n - 1, hl[0:1], pltpu.roll(dconv, n - 1, 0))
            dc2 = jnp.where(rows == n - 2, hl[0:1], jnp.where(rows == n - 1, hl[1:2], pltpu.roll(dconv, n - 2, 0)))
            halo[...] = dconv[0:8]
            cw = cw_ref[...]
            du = cw[2:3] * dconv + cw[1:2] * dc1 + cw[0:1] * dc2
            u = cc * cx
            accc_ref[0:1, :] += jnp.sum(u * dc2, axis=0, keepdims=True)
            accc_ref[1:2, :] += jnp.sum(u * dc1, axis=0, keepdims=True)
            accc_ref[2:3, :] += jnp.sum(u * dconv, axis=0, keepdims=True)
            dcx = (du * cc).astype(BF16)
            dcc = (du * cx).astype(BF16)
            dhn = lax.dot_general(dpr_ref[rs, 0:BLK], w_ref[4], NT, preferred_element_type=F32)
            for j in range(1, 4):
                dhn += lax.dot_general(dpr_ref[rs, j * BLK : (j + 1) * BLK], w_ref[4 + j], NT, preferred_element_type=F32)
            for blk, d in ((1, dcb), (3, dcg), (0, dcx), (2, dcc)):
                dpc_ref[rs, blk * BLK : (blk + 1) * BLK] = d
                dhn += lax.dot_general(d, w_ref[blk], NT, preferred_element_type=F32)
            r = lax.rsqrt(jnp.mean(h * h, axis=-1, keepdims=True) + EPS)
            hh = h * r
            acc_ref[0:1, :] += jnp.sum(dhn * hh, axis=0, keepdims=True)
            dg = dhn * g_ref[...]
            return dh2_ref[rs, :] + r * (dg - hh * jnp.mean(dg * hh, axis=-1, keepdims=True))

        @pl.when(tile == 0)
        def _():
            dh = work(mc_ref[...], LIVE0)
            dm_ref[...] = dh[CHUNK - N_META : CHUNK]

        @pl.when(tile > 0)
        def _():
            gx_ref[...] = work(x_ref[...], 0)

    rev = lambda w, j=0: pl.BlockSpec((tm, w), lambda i: (nt - 1 - i, j))
    tok = pl.BlockSpec((tm, D_MODEL), lambda i: (_token_tile(nt - 1 - i), 0))
    const = lambda r, c: pl.BlockSpec((r, c), lambda i: (0, 0))
    return pl.pallas_call(
        body,
        name="b2_dconv_dh",
        grid=(nt,),
        in_specs=[rev(half, 1), rev(D_CONV), rev(half, 0), rev(D_CONV), _VMEM, _VMEM, tok, _VMEM, rev(D_MODEL), _VMEM],
        out_specs=(rev(half, 0), tok, const(N_META, D_MODEL), const(8, D_MODEL), const(8, D_CONV)),
        out_shape=(
            jax.ShapeDtypeStruct(dproj.shape, BF16),
            jax.ShapeDtypeStruct(x2.shape, F32),
            jax.ShapeDtypeStruct((N_META, D_MODEL), F32),
            jax.ShapeDtypeStruct((8, D_MODEL), F32),
            jax.ShapeDtypeStruct((8, D_CONV), F32),
        ),
        scratch_shapes=[pltpu.VMEM((8, D_CONV), F32)],
        input_output_aliases={0: 0},
        compiler_params=_params(),
    )(dproj, dco, proj, conv, w3, cw8, x2, meta_chunk, dh2, g1)


def _gw_in_scatter(hnT, dproj, gw_out_parts, sc_parts, ga_part, me_arr):
    n_rows = dproj.shape[0]
    last = N_DEV - 1
    by_dest = (True, True, False)

    def body(me_ref, a_ref, b_ref, go_ref, sc_ref, ga_ref, land_in, land_go, land_sc, land_ga,
             d2d_buf, d2d_land, ici_buf, d2d_send, d2d_recv, ici_send, ici_recv, send_sems, recv_sems, local_sems):
        del me_ref
        s = pl.program_id(0)
        t = last - s
        q = t >> 1
        x, y, c = lax.axis_index("x"), lax.axis_index("y"), lax.axis_index("c")
        me = 4 * x + 2 * y + c
        chip = 2 * x + y
        srcs, lands = (go_ref, sc_ref, ga_ref), (land_go, land_sc, land_ga)

        def peer_at(k):
            return (1 - x if k & 4 else x, 1 - y if k & 2 else y, 1 - c if k & 1 else c)

        def small_copy(a, k):
            px, py, pc = peer_at(k)
            return pltpu.make_async_remote_copy(
                src_ref=srcs[a].at[4 * px + 2 * py + pc] if by_dest[a] else srcs[a],
                dst_ref=lands[a].at[me],
                send_sem=send_sems.at[a * last + k - 1],
                recv_sem=recv_sems.at[a * last + k - 1],
                device_id=(px, py, pc),
                device_id_type=MESH,
            )

        def small_local(a):
            return pltpu.make_async_copy(srcs[a].at[me] if by_dest[a] else srcs[a], lands[a].at[me], local_sems.at[a + 1])

        def d2d_copy(j):
            return pltpu.make_async_remote_copy(
                src_ref=d2d_buf.at[j],
                dst_ref=d2d_land.at[j],
                send_sem=d2d_send.at[j],
                recv_sem=d2d_recv.at[j],
                device_id=(x, y, 1 - c),
                device_id_type=MESH,
            )

        def ici_copy(j, to):
            return pltpu.make_async_remote_copy(
                src_ref=ici_buf.at[j],
                dst_ref=land_in.at[chip],
                send_sem=ici_send.at[j],
                recv_sem=ici_recv.at[j],
                device_id=to,
                device_id_type=MESH,
            )

        def own_copy():
            return pltpu.make_async_copy(ici_buf.at[0], land_in.at[chip], local_sems.at[0])

        @pl.when(s == 0)
        def _():
            for a in range(3):
                small_local(a).start()
            for k in range(last, 0, -1):
                for a in range(3):
                    small_copy(a, k).start()

        blk = jnp.dot(a_ref[:, LIVE0:], b_ref[LIVE0:, :], preferred_element_type=F32)

        @pl.when((t & 1) == 1)
        def _():
            d2d_buf[q] = blk.astype(BF16)
            d2d_copy(q).start()

        @pl.when((t & 1) == 0)
        def _():
            d2d_copy(q).wait_recv()
            ici_buf[q] = (blk + d2d_land[q].astype(F32)).astype(BF16)

            @pl.when(t != 0)
            def _():
                ici_copy(q, (jnp.bitwise_xor(x, (t >> 2) & 1), jnp.bitwise_xor(y, (t >> 1) & 1), c)).start()

            @pl.when(t == 0)
            def _():
                own_copy().start()
                for j in range(N_CHIPS):
                    d2d_copy(j).wait_send()
                for j in range(N_CHIPS - 1, 0, -1):
                    ici_copy(j, peer_at(2 * j)).wait()
                for k in range(last, 0, -1):
                    for a in range(3):
                        small_copy(a, k).wait()
                for a in range(3):
                    small_local(a).wait()
                own_copy().wait()

    smalls = (gw_out_parts, sc_parts, ga_part)
    small_blocks = [p.shape[1:] if d else p.shape for p, d in zip(smalls, by_dest, strict=True)]
    grid_spec = pltpu.PrefetchScalarGridSpec(
        num_scalar_prefetch=1,
        grid=(N_DEV,),
        in_specs=[_VMEM, pl.BlockSpec((n_rows, BLK), lambda s, me: (0, jnp.bitwise_xor(me[0], last - s)))] + [_HBM] * 3,
        out_specs=tuple([_HBM] * 4),
        scratch_shapes=[
            pltpu.VMEM((N_CHIPS, D_MODEL, BLK), BF16),
            pltpu.VMEM((N_CHIPS, D_MODEL, BLK), BF16),
            pltpu.VMEM((N_CHIPS, D_MODEL, BLK), BF16),
            pltpu.SemaphoreType.DMA((N_CHIPS,)),
            pltpu.SemaphoreType.DMA((N_CHIPS,)),
            pltpu.SemaphoreType.DMA((N_CHIPS,)),
            pltpu.SemaphoreType.DMA((N_CHIPS,)),
            pltpu.SemaphoreType.DMA((3 * last,)),
            pltpu.SemaphoreType.DMA((3 * last,)),
            pltpu.SemaphoreType.DMA((4,)),
        ],
    )
    return pl.pallas_call(
        body,
        name="gw_in_scatter",
        grid_spec=grid_spec,
        out_shape=(jax.ShapeDtypeStruct((N_CHIPS, D_MODEL, BLK), BF16),)
        + tuple(jax.ShapeDtypeStruct((N_DEV, *b), p.dtype) for b, p in zip(small_blocks, smalls, strict=True)),
        compiler_params=_params(),
    )(me_arr, hnT, dproj, gw_out_parts, sc_parts, ga_part)


def _adamw(w, g, m, v):
    m = ADAM_B1 * m + (1.0 - ADAM_B1) * g
    v = ADAM_B2 * v + (1.0 - ADAM_B2) * (g * g)
    m_hat = m / (1.0 - ADAM_B1**ADAM_STEP)
    v_hat = v / (1.0 - ADAM_B2**ADAM_STEP)
    delta = -ADAM_LR * (m_hat / (jnp.sqrt(v_hat) + ADAM_EPS) + ADAM_WD * w)
    return delta, m, v


def _sum_adamw(name, parts, w, m, v, rows_per_step):
    n_r, n_c = w.shape
    n_parts = parts.shape[0]
    tr = rows_per_step

    def body(p_ref, w_ref, m_ref, v_ref, g_ref, d_ref, nm_ref, nv_ref):
        g = p_ref[0].astype(F32)
        for s in range(1, n_parts):
            g = g + p_ref[s].astype(F32)
        g_ref[...] = g
        d_ref[...], nm_ref[...], nv_ref[...] = _adamw(w_ref[...], g, m_ref[...], v_ref[...])

    blk = pl.BlockSpec((tr, n_c), lambda i: (i, 0))
    return pl.pallas_call(
        body,
        name=name,
        grid=(n_r // tr,),
        in_specs=[pl.BlockSpec((n_parts, tr, n_c), lambda i: (0, i, 0)), blk, blk, blk],
        out_specs=(blk,) * 4,
        out_shape=(jax.ShapeDtypeStruct((n_r, n_c), F32),) * 4,
        compiler_params=_params(),
    )(parts, w, m, v)


def _small_leaves(meta, conv_w, n1, rg, fg):
    cols = D_CONV // N_DEV
    return (
        meta,
        jnp.pad(conv_w, ((0, 5), (0, CHUNK - cols))),
        n1.reshape(8, CHUNK),
        rg.reshape(4, CHUNK),
        fg.reshape(8, CHUNK),
    )


def _from_small_leaves(meta, conv8, n1, rg, fg):
    return meta, conv8[0:3, : D_CONV // N_DEV], n1.reshape(D_MODEL), rg.reshape(D_RET), fg.reshape(D_MODEL)


def _adamw_small(land_sc, land_ga, w, m, v):
    n_leaf = 5

    def body(sc_ref, ga_ref, *refs):
        ins, outs = refs[: 3 * n_leaf], refs[3 * n_leaf :]
        sc, ga = sc_ref[0], ga_ref[0]
        for s in range(1, N_DEV):
            sc = sc + sc_ref[s]
            ga = ga + ga_ref[s]
        grads = (
            sc[SC_META0 : SC_META0 + N_META],
            sc[SC_CONV0 : SC_CONV0 + 8],
            ga[GA_N1 : GA_N1 + 8],
            ga[GA_RG : GA_RG + 4],
            ga[GA_FG : GA_FG + 8],
        )
        for leaf, g in enumerate(grads):
            d, nm, nv = _adamw(ins[leaf][...], g, ins[n_leaf + leaf][...], ins[2 * n_leaf + leaf][...])
            outs[leaf][...] = g
            outs[n_leaf + leaf][...] = d
            outs[2 * n_leaf + leaf][...] = nm
            outs[3 * n_leaf + leaf][...] = nv
        outs[4 * n_leaf][...] = ga[GA_LOSS : GA_LOSS + 1]

    leaf_shapes = tuple(jax.ShapeDtypeStruct(a.shape, F32) for a in w)
    out = pl.pallas_call(
        body,
        name="adamw_small",
        out_shape=leaf_shapes * 4 + (jax.ShapeDtypeStruct((1, CHUNK), F32),),
    )(land_sc, land_ga, *w, *m, *v)
    return tuple(out[k * n_leaf : (k + 1) * n_leaf] for k in range(4)), out[4 * n_leaf]


def kernel(x, meta, norm1_g, w_in, conv_w, ret_norm_g, w_out, final_g, loss_target, m_meta, m_norm1_g, m_w_in, m_conv_w, m_ret_norm_g, m_w_out, m_final_g, v_meta, v_norm1_g, v_w_in, v_conv_w, v_ret_norm_g, v_w_out, v_final_g):
    seq = x.shape[1]
    assert x.shape == (1, seq, D_MODEL) and seq % ROW_TILE == 0
    n_tiles = seq // ROW_TILE + 1
    cols = D_CONV // N_DEV
    x2, t2 = x[0], loss_target[0]

    small_w = _small_leaves(meta, conv_w, norm1_g, ret_norm_g, final_g)
    w3, small = _gather("gather_weights", [w_in.astype(BF16), jnp.concatenate(small_w[0:2], axis=0)])
    meta_full = small[:, SC_META0 : SC_META0 + N_META, :].transpose(1, 0, 2).reshape(N_META, D_MODEL)
    conv_full = small[:, SC_CONV0 : SC_CONV0 + 3, :cols].transpose(1, 0, 2).reshape(3, D_CONV)
    cw8 = jnp.pad(conv_full, ((0, 5), (0, 0)))
    meta_chunk = jnp.pad(meta_full, ((CHUNK - N_META, 0), (0, 0)))

    rot = _rotary_tables(n_tiles, ROW_TILE)
    dec, dect, zeta, xi = _decay_tables()
    g1 = norm1_g.reshape(1, D_MODEL)
    gr = ret_norm_g.reshape(1, D_RET)
    gf = final_g.reshape(1, D_MODEL)

    hnT, proj, conv_out, conv, wo3 = _f1(x2, meta_chunk, g1, w3, cw8, rot, w_out.astype(BF16))
    wout = wo3.reshape(D_MODEL, D_MODEL)
    dh2, o, states, gw_out, acc_f2 = _f2(proj, conv_out, x2, t2, wout, gr, gf, dec, zeta, xi)
    dproj_ret, dco, acc_b1 = _b1(dh2, proj, o, states, wout, gr, dec, dect, zeta, xi, rot)
    dproj, grad_x2, dmeta_rows, acc_b2, acc_conv = _b2(dproj_ret, dco, proj, conv, w3, cw8, x2, meta_chunk, dh2, g1)
    gw_out_parts = gw_out.reshape(N_DEV, D_MODEL // N_DEV, D_MODEL)

    dmeta = dmeta_rows.reshape(N_META, N_DEV, CHUNK).transpose(1, 0, 2)
    dconv = jnp.pad(acc_conv[0:3, :].reshape(3, N_DEV, cols).transpose(1, 0, 2), ((0, 0), (0, 5), (0, CHUNK - cols)))
    sc_parts = jnp.concatenate([dmeta, dconv], axis=1)
    ga_part = jnp.concatenate(
        [
            acc_b2[0].reshape(8, CHUNK),
            jnp.pad(acc_b1[0].reshape(4, CHUNK), ((0, 4), (0, 0))),
            acc_f2[0].reshape(8, CHUNK),
            jnp.pad(acc_f2[1:2, 0:CHUNK], ((0, 7), (0, 0))),
        ],
        axis=0,
    )

    me_arr = (4 * lax.axis_index("x") + 2 * lax.axis_index("y") + lax.axis_index("c")).astype(jnp.int32).reshape(1)
    land_in, land_out, land_sc, land_ga = _gw_in_scatter(hnT, dproj, gw_out_parts, sc_parts, ga_part, me_arr)

    g_w_in, d_w_in, nm_w_in, nv_w_in = _sum_adamw("adamw_w_in", land_in, w_in, m_w_in, v_w_in, 256)
    g_w_out, d_w_out, nm_w_out, nv_w_out = _sum_adamw("adamw_w_out", land_out, w_out, m_w_out, v_w_out, 64)
    small_out, loss_row = _adamw_small(
        land_sc,
        land_ga,
        small_w,
        _small_leaves(m_meta, m_conv_w, m_norm1_g, m_ret_norm_g, m_final_g),
        _small_leaves(v_meta, v_conv_w, v_norm1_g, v_ret_norm_g, v_final_g),
    )
    loss = loss_row[0, 0]
    grad_x = grad_x2.reshape(1, seq, D_MODEL)

    def leaves(w_in_leaf, w_out_leaf, small_leaves):
        meta_leaf, conv_leaf, n1_leaf, rg_leaf, fg_leaf = _from_small_leaves(*small_leaves)
        return (meta_leaf, n1_leaf, w_in_leaf, conv_leaf, rg_leaf, w_out_leaf, fg_leaf)

    return (
        loss,
        grad_x,
        *leaves(g_w_in, g_w_out, small_out[0]),
        *leaves(d_w_in, d_w_out, small_out[1]),
        *leaves(nm_w_in, nm_w_out, small_out[2]),
        *leaves(nv_w_in, nv_w_out, small_out[3]),
    )
```

```python
import functools
import math

import jax
import jax.numpy as jnp
import numpy as np
from jax import lax
from jax.experimental import pallas as pl
from jax.experimental.pallas import tpu as pltpu

F32 = jnp.float32
BF16 = jnp.bfloat16

N_DEV = 8
N_CHIPS = 4
D_MODEL = 1024
N_META = 16
CHUNK = 128
D_CONV = 512
D_RET = 512
N_HEADS = 4
HEAD_DIM = 128
N_PROJ = 8
BLK = 512
ROPE_BASE = 10000.0
EPS = 1e-6
Q_SCALE = HEAD_DIM ** -0.5
LOG_G = tuple(math.log(1.0 - 2.0 ** (-5.0 - h)) for h in range(N_HEADS))
CHUNK_DECAY = tuple(math.exp(CHUNK * lg) for lg in LOG_G)

ADAM_LR = 0.001
ADAM_B1 = 0.9
ADAM_B2 = 0.999
ADAM_EPS = 1e-08
ADAM_WD = 0.01
ADAM_STEP = 10

ROW_TILE = 512
PAD_ROWS = ROW_TILE - N_META
LIVE0 = ROW_TILE - CHUNK
VMEM_LIMIT = 56 * 1024 * 1024

SC_META0, SC_CONV0 = 0, 16
GA_N1, GA_RG, GA_FG, GA_LOSS = 0, 8, 16, 24

NT = (((1,), (1,)), ((), ()))
TN = (((0,), (0,)), ((), ()))
MESH = pl.DeviceIdType.MESH

_VMEM = pl.BlockSpec(memory_space=pltpu.VMEM)
_HBM = pl.BlockSpec(memory_space=pltpu.HBM)


def _params(n_axes=1):
    return pltpu.CompilerParams(dimension_semantics=("arbitrary",) * n_axes, vmem_limit_bytes=VMEM_LIMIT)


def _sigmoid(x):
    return 0.5 * jnp.tanh(0.5 * x) + 0.5


def _decay_tables():
    idx = np.arange(CHUNK, dtype=np.float64)
    diff = idx[:, None] - idx[None, :]
    dec = np.stack([np.where(diff >= 0, np.exp(diff * lg), 0.0) for lg in LOG_G])
    zeta = np.stack([np.exp((CHUNK - 1 - idx) * lg) for lg in LOG_G])
    xi = np.stack([np.exp((idx + 1.0) * lg) for lg in LOG_G])
    ones = np.ones((1, 1, CHUNK))
    return (
        jnp.asarray(dec, F32),
        jnp.asarray(dec.transpose(0, 2, 1), F32),
        jnp.asarray(zeta[:, :, None] * ones, F32),
        jnp.asarray(xi[:, :, None] * ones, F32),
    )


def _rotary_tables(n_tiles, tm):
    half = HEAD_DIM // 2
    freqs = (1.0 / (np.float32(ROPE_BASE) ** (np.arange(half, dtype=np.float32) / np.float32(half)))).astype(np.float64)
    sign = np.concatenate([-np.ones(half), np.ones(half)])
    two = lambda a: np.concatenate([a, a], axis=1)
    base = two((np.arange(n_tiles, dtype=np.float64) * tm - PAD_ROWS)[:, None] * freqs[None, :])
    off = two(np.arange(tm, dtype=np.float64)[:, None] * freqs[None, :])
    as32 = lambda a: jnp.asarray(a, F32)
    return as32(np.cos(base)), as32(np.sin(base) * sign), as32(np.cos(off)), as32(np.sin(off) * sign)


def _tile_rotary(ca_ref, sa_ref, cb_ref, sb_ref, tile):
    ca, sa = ca_ref[pl.ds(tile, 1), :], sa_ref[pl.ds(tile, 1), :]
    cb, sb = cb_ref[...], sb_ref[...]
    return ca * cb - sa * sb, sa * cb + ca * sb


def _rot(t, cos2, sin2):
    return t * cos2 + pltpu.roll(t, HEAD_DIM // 2, 1) * sin2


def _rot_bwd(d, cos2, sin2):
    return d * cos2 + pltpu.roll(d * sin2, HEAD_DIM // 2, 1)


def _token_tile(i):
    return jnp.maximum(i - 1, 0)


GATHER_PAIRS = 7


def _two_level_gather(src_ref, out_ref, send_sems, recv_sems, local_sem, base=0):
    x, y, c = lax.axis_index("x"), lax.axis_index("y"), lax.axis_index("c")
    me, sibling = (x, y, c), (x, y, 1 - c)
    chips = [(1 - x, y), (x, 1 - y), (1 - x, 1 - y)]

    def copy(k, block, to, src=None):
        dst = out_ref.at[4 * block[0] + 2 * block[1] + block[2]]
        return pltpu.make_async_remote_copy(
            src_ref=dst if src is None else src,
            dst_ref=dst,
            send_sem=send_sems.at[base + k],
            recv_sem=recv_sems.at[base + k],
            device_id=to,
            device_id_type=MESH,
        )

    mine = pltpu.make_async_copy(src_ref, out_ref.at[4 * x + 2 * y + c], local_sem)
    first = [copy(1 + j, me, (*chip, c), src=src_ref) for j, chip in enumerate(chips)]
    first.append(copy(0, me, sibling, src=src_ref))
    passed = [copy(4 + j, (*chip, c), sibling) for j, chip in enumerate(chips)]

    def start():
        mine.start()
        for cp in first:
            cp.start()

    def forward():
        for j, chip in enumerate(chips):
            copy(1 + j, (*chip, c), me).wait_recv()
            passed[j].start()

    def finish():
        copy(0, sibling, me).wait_recv()
        for j, chip in enumerate(chips):
            copy(4 + j, (*chip, 1 - c), me).wait_recv()
        for cp in first + passed:
            cp.wait_send()
        mine.wait()

    return start, forward, finish


def _gather(name, srcs):
    n = len(srcs)
    per = GATHER_PAIRS

    def body(*refs):
        src_refs, out_refs = refs[:n], refs[n : 2 * n]
        send_sems, recv_sems, local_sems = refs[2 * n :]
        parts = [
            _two_level_gather(src_refs[a], out_refs[a], send_sems, recv_sems, local_sems.at[a], a * per)
            for a in range(n)
        ]
        for phase in range(3):
            for part in parts:
                part[phase]()

    return pl.pallas_call(
        body,
        name=name,
        out_shape=tuple(jax.ShapeDtypeStruct((N_DEV, *s.shape), s.dtype) for s in srcs),
        in_specs=[_HBM] * n,
        out_specs=tuple([_HBM] * n),
        scratch_shapes=[
            pltpu.SemaphoreType.DMA((n * per,)),
            pltpu.SemaphoreType.DMA((n * per,)),
            pltpu.SemaphoreType.DMA((n,)),
        ],
    )(*srcs)


def _f1(x2, meta_chunk, g1, w3, cw8, rot, wo_shard):
    tm = ROW_TILE
    nt = x2.shape[0] // tm + 1
    n_rows = nt * tm

    def body(x_ref, mc_ref, g_ref, w_ref, cw_ref, ca_ref, sa_ref, cb_ref, sb_ref, wo_ref,
             hnT_ref, pr_ref, co_ref, cv_ref, wo_all_ref, halo, send_sems, recv_sems, local_sem):
        i = pl.program_id(0)
        wo_start, wo_forward, wo_finish = _two_level_gather(wo_ref, wo_all_ref, send_sems, recv_sems, local_sem)
        pl.when(i == 0)(wo_start)
        pl.when(i == nt // 2)(wo_forward)
        pl.when(i == nt - 1)(wo_finish)

        def work(h, r0):
            rs = slice(r0, tm)
            n = tm - r0
            r = lax.rsqrt(jnp.mean(h * h, axis=-1, keepdims=True) + EPS)
            hn = (h * r * g_ref[...]).astype(BF16)
            hnT_ref[:, rs] = hn.T

            def proj(j):
                return jnp.dot(hn, w_ref[j], preferred_element_type=F32)

            cx, cb, cc, cg = proj(0), proj(1), proj(2), proj(3)
            u = cc * cx
            rows = lax.broadcasted_iota(jnp.int32, u.shape, 0)
            hl = halo[...]
            u1 = jnp.where(rows == 0, hl[7:8], pltpu.roll(u, 1, 0))
            u2 = jnp.where(rows == 0, hl[6:7], jnp.where(rows == 1, hl[7:8], pltpu.roll(u, 2, 0)))
            halo[...] = u[n - 8 : n]
            cw = cw_ref[...]
            conv = cw[0:1] * u2 + cw[1:2] * u1 + cw[2:3] * u
            co = (cb * conv * (cg * _sigmoid(cg))).astype(BF16)
            co_ref[rs, :] = co
            cv_ref[rs, :] = conv.astype(BF16)
            pr_ref[rs, 0 * BLK : 1 * BLK] = cx.astype(BF16)
            pr_ref[rs, 1 * BLK : 2 * BLK] = cb.astype(BF16)
            pr_ref[rs, 2 * BLK : 3 * BLK] = cc.astype(BF16)
            pr_ref[rs, 3 * BLK : 4 * BLK] = cg.astype(BF16)
            cos_all, sin_all = _tile_rotary(ca_ref, sa_ref, cb_ref, sb_ref, i)
            cos_t, sin_t = cos_all[rs, :], sin_all[rs, :]
            q, k = proj(4), proj(5)
            for hd in range(N_HEADS):
                c0 = hd * HEAD_DIM
                pr_ref[rs, 4 * BLK + c0 : 4 * BLK + c0 + HEAD_DIM] = (
                    _rot(q[:, c0 : c0 + HEAD_DIM], cos_t, sin_t) * Q_SCALE
                ).astype(BF16)
                pr_ref[rs, 5 * BLK + c0 : 5 * BLK + c0 + HEAD_DIM] = _rot(
                    k[:, c0 : c0 + HEAD_DIM], cos_t, sin_t
                ).astype(BF16)
            pr_ref[rs, 6 * BLK : 7 * BLK] = proj(6).astype(BF16)
            pr_ref[rs, 7 * BLK : 8 * BLK] = proj(7).astype(BF16)

        @pl.when(i == 0)
        def _():
            halo[...] = jnp.zeros_like(halo)
            work(mc_ref[...], LIVE0)

        @pl.when(i > 0)
        def _():
            work(x_ref[...], 0)

    row = lambda w: pl.BlockSpec((tm, w), lambda i: (i, 0))
    col = lambda w: pl.BlockSpec((w, tm), lambda i: (0, i))
    return pl.pallas_call(
        body,
        name="f1_inproj_conv",
        grid=(nt,),
        in_specs=[pl.BlockSpec((tm, D_MODEL), lambda i: (_token_tile(i), 0))] + [_VMEM] * 8 + [_HBM],
        out_specs=(col(D_MODEL), row(N_PROJ * BLK), row(D_CONV), row(D_CONV), _HBM),
        out_shape=(
            jax.ShapeDtypeStruct((D_MODEL, n_rows), BF16),
            jax.ShapeDtypeStruct((n_rows, N_PROJ * BLK), BF16),
            jax.ShapeDtypeStruct((n_rows, D_CONV), BF16),
            jax.ShapeDtypeStruct((n_rows, D_CONV), BF16),
            jax.ShapeDtypeStruct((N_DEV, *wo_shard.shape), wo_shard.dtype),
        ),
        scratch_shapes=[
            pltpu.VMEM((8, D_CONV), F32),
            pltpu.SemaphoreType.DMA((GATHER_PAIRS,)),
            pltpu.SemaphoreType.DMA((GATHER_PAIRS,)),
            pltpu.SemaphoreType.DMA(()),
        ],
        compiler_params=_params(),
    )(x2, meta_chunk, g1, w3, cw8, *rot, wo_shard)


def _group_norm(o):
    ys, rs = [], []
    for hd in range(N_HEADS):
        oh = o[:, hd * HEAD_DIM : (hd + 1) * HEAD_DIM]
        xc = oh - jnp.mean(oh, axis=-1, keepdims=True)
        rstd = lax.rsqrt(jnp.mean(xc * xc, axis=-1, keepdims=True) + EPS)
        ys.append(xc * rstd)
        rs.append(jnp.broadcast_to(rstd, oh.shape))
    return jnp.concatenate(ys, axis=1), jnp.concatenate(rs, axis=1)


def _f2(proj, conv_out, x2, t2, wout, gr, gf, dec, zeta, xi):
    n_rows = proj.shape[0]
    tm = ROW_TILE
    nt = n_rows // tm
    nct = tm // CHUNK

    def body(q_ref, k_ref, v_ref, rg_ref, co_ref, x_ref, t_ref, wo_ref, gr_ref, gf_ref, dec_ref, zeta_ref,
             xi_ref, dh2_ref, o_ref, st_ref, gw_ref, acc_ref, state, obuf, gacc):
        i = pl.program_id(0)

        def mixer(r0):
            chunks = range(r0 // CHUNK, nct)
            for hd in range(N_HEADS):
                cs = slice(hd * HEAD_DIM, (hd + 1) * HEAD_DIM)
                upd = {}
                for c in chunks:
                    rs = slice(c * CHUNK, (c + 1) * CHUNK)
                    kz = (k_ref[rs, cs].astype(F32) * zeta_ref[hd]).astype(BF16)
                    upd[c] = lax.dot_general(kz, v_ref[rs, cs], TN, preferred_element_type=F32)
                st = state[hd]
                for c in chunks:
                    rs = slice(c * CHUNK, (c + 1) * CHUNK)
                    q, k, v = q_ref[rs, cs], k_ref[rs, cs], v_ref[rs, cs]
                    st_bf = st.astype(BF16)
                    st_ref[c, hd] = st_bf
                    s = lax.dot_general(q, k, NT, preferred_element_type=F32) * dec_ref[hd]
                    inner = jnp.dot(s.astype(BF16), v, preferred_element_type=F32)
                    qx = (q.astype(F32) * xi_ref[hd]).astype(BF16)
                    obuf[rs, cs] = inner + jnp.dot(qx, st_bf, preferred_element_type=F32)
                    st = CHUNK_DECAY[hd] * st + upd[c]
                state[hd] = st
            o = obuf[r0:tm, :]
            o_ref[r0:tm, :] = o.astype(BF16)
            yh, _ = _group_norm(o)
            rg = rg_ref[r0:tm, :].astype(F32)
            return (yh * gr_ref[...] * (rg * _sigmoid(rg))).astype(BF16)

        @pl.when(i == 0)
        def _():
            state[...] = jnp.zeros_like(state)
            acc_ref[...] = jnp.zeros_like(acc_ref)
            gacc[...] = jnp.zeros_like(gacc)
            mixer(LIVE0)
            dh2_ref[...] = jnp.zeros_like(dh2_ref)

        @pl.when(i > 0)
        def _():
            ro = mixer(0)
            h2 = (
                x_ref[...]
                + jnp.dot(co_ref[...], wo_ref[0:D_CONV], preferred_element_type=F32)
                + jnp.dot(ro, wo_ref[D_CONV:], preferred_element_type=F32)
            )
            r2 = lax.rsqrt(jnp.mean(h2 * h2, axis=-1, keepdims=True) + EPS)
            yn = h2 * r2
            gfv = gf_ref[...]
            err = yn * gfv - t_ref[...]
            tile_loss = jnp.sum(jnp.sum(err * err, axis=-1, keepdims=True), axis=0, keepdims=True) * (0.5 / D_MODEL)
            acc_ref[0:1, :] += jnp.sum(err * yn, axis=0, keepdims=True) * (1.0 / D_MODEL)
            acc_ref[1:2, :] += tile_loss
            dyn = err * (gfv * (1.0 / D_MODEL))
            dh2 = r2 * (dyn - yn * jnp.mean(dyn * yn, axis=-1, keepdims=True))
            dh2_ref[...] = dh2
            dh2_bf = dh2.astype(BF16)
            gacc[0:D_CONV, :] += lax.dot_general(co_ref[...], dh2_bf, TN, preferred_element_type=F32)
            gacc[D_CONV:, :] += lax.dot_general(ro, dh2_bf, TN, preferred_element_type=F32)

        @pl.when(i == nt - 1)
        def _():
            gw_ref[...] = gacc[...].astype(BF16)

    row = lambda w, j=0: pl.BlockSpec((tm, w), lambda i: (i, j))
    tok = pl.BlockSpec((tm, D_MODEL), lambda i: (_token_tile(i), 0))
    return pl.pallas_call(
        body,
        name="f2_retention_out",
        grid=(nt,),
        in_specs=[row(BLK, 4), row(BLK, 5), row(BLK, 6), row(BLK, 7), row(D_CONV), tok, tok] + [_VMEM] * 6,
        out_specs=(
            row(D_MODEL),
            row(D_RET),
            pl.BlockSpec((nct, N_HEADS, HEAD_DIM, HEAD_DIM), lambda i: (i, 0, 0, 0)),
            pl.BlockSpec((D_MODEL, D_MODEL), lambda i: (0, 0)),
            pl.BlockSpec((8, D_MODEL), lambda i: (0, 0)),
        ),
        out_shape=(
            jax.ShapeDtypeStruct((n_rows, D_MODEL), F32),
            jax.ShapeDtypeStruct((n_rows, D_RET), BF16),
            jax.ShapeDtypeStruct((n_rows // CHUNK, N_HEADS, HEAD_DIM, HEAD_DIM), BF16),
            jax.ShapeDtypeStruct((D_MODEL, D_MODEL), BF16),
            jax.ShapeDtypeStruct((8, D_MODEL), F32),
        ),
        scratch_shapes=[
            pltpu.VMEM((N_HEADS, HEAD_DIM, HEAD_DIM), F32),
            pltpu.VMEM((tm, D_RET), F32),
            pltpu.VMEM((D_MODEL, D_MODEL), F32),
        ],
        compiler_params=_params(),
    )(proj, proj, proj, proj, conv_out, x2, t2, wout, gr, gf, dec, zeta, xi)


def _b1(dh2, proj, o, states, wout, gr, dec, dect, zeta, xi, rot):
    n_rows = dh2.shape[0]
    tm = ROW_TILE
    nt = n_rows // tm
    nct = tm // CHUNK

    def body(dh2_ref, pr_ref, o_ref, st_ref, wo_ref, gr_ref, dec_ref, dect_ref, zeta_ref, xi_ref,
             ca_ref, sa_ref, cb_ref, sb_ref, dp_ref, dco_ref, acc_ref, dstate, dobuf):
        i = pl.program_id(0)
        tile = nt - 1 - i
        cos_all, sin_all = _tile_rotary(ca_ref, sa_ref, cb_ref, sb_ref, tile)

        @pl.when(i == 0)
        def _():
            dstate[...] = jnp.zeros_like(dstate)
            acc_ref[...] = jnp.zeros_like(acc_ref)

        @pl.when(tile > 0)
        def _():
            dmix = lax.dot_general(dh2_ref[...].astype(BF16), wo_ref[...], NT, preferred_element_type=F32)
            dco_ref[...] = dmix[:, :D_CONV].astype(BF16)
            dro = dmix[:, D_CONV:]
            rg = pr_ref[:, 3 * BLK : 4 * BLK].astype(F32)
            sg = _sigmoid(rg)
            silu, dsilu = rg * sg, sg * (1.0 + rg * (1.0 - sg))
            yh, rstd = _group_norm(o_ref[...].astype(F32))
            grv = gr_ref[...]
            dp_ref[:, 3 * BLK : 4 * BLK] = (dro * (yh * grv) * dsilu).astype(BF16)
            dret = dro * silu
            acc_ref[0:1, :] += jnp.sum(dret * yh, axis=0, keepdims=True)
            dyh = dret * grv
            for hd in range(N_HEADS):
                cs = slice(hd * HEAD_DIM, (hd + 1) * HEAD_DIM)
                a, b = dyh[:, cs], yh[:, cs]
                dobuf[:, cs] = rstd[:, cs] * (
                    a - jnp.mean(a, axis=-1, keepdims=True) - b * jnp.mean(a * b, axis=-1, keepdims=True)
                )

            for hd in range(N_HEADS):
                cs = slice(hd * HEAD_DIM, (hd + 1) * HEAD_DIM)
                dupd = {}
                for c in range(nct):
                    rs = slice(c * CHUNK, (c + 1) * CHUNK)
                    qx = (pr_ref[rs, cs].astype(F32) * xi_ref[hd]).astype(BF16)
                    dupd[c] = lax.dot_general(qx, dobuf[rs, cs].astype(BF16), TN, preferred_element_type=F32)
                dst = dstate[hd]
                for c in reversed(range(nct)):
                    rs = slice(c * CHUNK, (c + 1) * CHUNK)
                    cos_t, sin_t = cos_all[rs, :], sin_all[rs, :]
                    q = pr_ref[rs, hd * HEAD_DIM : (hd + 1) * HEAD_DIM]
                    k = pr_ref[rs, BLK + hd * HEAD_DIM : BLK + (hd + 1) * HEAD_DIM]
                    v = pr_ref[rs, 2 * BLK + hd * HEAD_DIM : 2 * BLK + (hd + 1) * HEAD_DIM]
                    do = dobuf[rs, cs].astype(BF16)
                    st_bf = st_ref[c, hd]
                    dst_bf = dst.astype(BF16)
                    zt, xt = zeta_ref[hd], xi_ref[hd]
                    sT = (lax.dot_general(k, q, NT, preferred_element_type=F32) * dect_ref[hd]).astype(BF16)
                    dsT = (lax.dot_general(v, do, NT, preferred_element_type=F32) * dect_ref[hd]).astype(BF16)
                    ds = (lax.dot_general(do, v, NT, preferred_element_type=F32) * dec_ref[hd]).astype(BF16)
                    kz = (k.astype(F32) * zt).astype(BF16)
                    dv = jnp.dot(sT, do, preferred_element_type=F32) + jnp.dot(kz, dst_bf, preferred_element_type=F32)
                    dq = jnp.dot(ds, k, preferred_element_type=F32) + xt * lax.dot_general(
                        do, st_bf, NT, preferred_element_type=F32
                    )
                    dk = jnp.dot(dsT, q, preferred_element_type=F32) + zt * lax.dot_general(
                        v, dst_bf, NT, preferred_element_type=F32
                    )
                    dst = CHUNK_DECAY[hd] * dst + dupd[c]
                    dp_ref[rs, hd * HEAD_DIM : (hd + 1) * HEAD_DIM] = _rot_bwd(dq * Q_SCALE, cos_t, sin_t).astype(BF16)
                    dp_ref[rs, BLK + hd * HEAD_DIM : BLK + (hd + 1) * HEAD_DIM] = _rot_bwd(dk, cos_t, sin_t).astype(BF16)
                    dp_ref[rs, 2 * BLK + hd * HEAD_DIM : 2 * BLK + (hd + 1) * HEAD_DIM] = dv.astype(BF16)
                dstate[hd] = dst

        @pl.when(tile == 0)
        def _():
            rs = slice(LIVE0, tm)
            cos_t, sin_t = cos_all[rs, :], sin_all[rs, :]
            zeros = jnp.zeros((tm - LIVE0, HEAD_DIM), BF16)
            for hd in range(N_HEADS):
                k = pr_ref[rs, BLK + hd * HEAD_DIM : BLK + (hd + 1) * HEAD_DIM]
                v = pr_ref[rs, 2 * BLK + hd * HEAD_DIM : 2 * BLK + (hd + 1) * HEAD_DIM]
                dst_bf = dstate[hd].astype(BF16)
                zt = zeta_ref[hd]
                kz = (k.astype(F32) * zt).astype(BF16)
                dv = jnp.dot(kz, dst_bf, preferred_element_type=F32)
                dk = zt * lax.dot_general(v, dst_bf, NT, preferred_element_type=F32)
                dp_ref[rs, hd * HEAD_DIM : (hd + 1) * HEAD_DIM] = zeros
                dp_ref[rs, BLK + hd * HEAD_DIM : BLK + (hd + 1) * HEAD_DIM] = _rot_bwd(dk, cos_t, sin_t).astype(BF16)
                dp_ref[rs, 2 * BLK + hd * HEAD_DIM : 2 * BLK + (hd + 1) * HEAD_DIM] = dv.astype(BF16)
                dp_ref[rs, 3 * BLK + hd * HEAD_DIM : 3 * BLK + (hd + 1) * HEAD_DIM] = zeros
            dco_ref[rs, :] = jnp.zeros((tm - LIVE0, D_CONV), BF16)

    half = N_PROJ * BLK // 2
    rev = lambda w, j=0: pl.BlockSpec((tm, w), lambda i: (nt - 1 - i, j))
    return pl.pallas_call(
        body,
        name="b1_dret",
        grid=(nt,),
        in_specs=[
            rev(D_MODEL),
            rev(half, 1),
            rev(D_RET),
            pl.BlockSpec((nct, N_HEADS, HEAD_DIM, HEAD_DIM), lambda i: (nt - 1 - i, 0, 0, 0)),
        ]
        + [_VMEM] * 10,
        out_specs=(rev(half, 1), rev(D_CONV), pl.BlockSpec((8, D_RET), lambda i: (0, 0))),
        out_shape=(
            jax.ShapeDtypeStruct((n_rows, N_PROJ * BLK), BF16),
            jax.ShapeDtypeStruct((n_rows, D_CONV), BF16),
            jax.ShapeDtypeStruct((8, D_RET), F32),
        ),
        scratch_shapes=[pltpu.VMEM((N_HEADS, HEAD_DIM, HEAD_DIM), F32), pltpu.VMEM((tm, D_RET), F32)],
        compiler_params=_params(),
    )(dh2, proj, o, states, wout, gr, dec, dect, zeta, xi, *rot)


def _b2(dproj, dco, proj, conv, w3, cw8, x2, meta_chunk, dh2, g1):
    n_rows = dproj.shape[0]
    tm = ROW_TILE
    nt = n_rows // tm
    half = N_PROJ * BLK // 2

    def body(dpr_ref, dco_ref, pr_ref, cv_ref, w_ref, cw_ref, x_ref, mc_ref, dh2_ref, g_ref,
             dpc_ref, gx_ref, dm_ref, acc_ref, accc_ref, halo):
        i = pl.program_id(0)
        tile = nt - 1 - i

        @pl.when(i == 0)
        def _():
            acc_ref[...] = jnp.zeros_like(acc_ref)
            accc_ref[...] = jnp.zeros_like(accc_ref)
            halo[...] = jnp.zeros_like(halo)

        def work(h, r0):
            rs = slice(r0, tm)
            n = tm - r0
            dco = dco_ref[rs, :].astype(F32)
            cx = pr_ref[rs, 0 * BLK : 1 * BLK].astype(F32)
            cb = pr_ref[rs, 1 * BLK : 2 * BLK].astype(F32)
            cc = pr_ref[rs, 2 * BLK : 3 * BLK].astype(F32)
            cg = pr_ref[rs, 3 * BLK : 4 * BLK].astype(F32)
            conv = cv_ref[rs, :].astype(F32)
            sg = _sigmoid(cg)
            silu, dsilu = cg * sg, sg * (1.0 + cg * (1.0 - sg))
            t = dco * cb
            dcb = (dco * conv * silu).astype(BF16)
            dcg = (t * conv * dsilu).astype(BF16)
            dconv = t * silu
            rows = lax.broadcasted_iota(jnp.int32, dconv.shape, 0)
            hl = halo[...]
            dc1 = jnp.where(rows == n - 1, hl[0:1], pltpu.roll(dconv, n - 1, 0))
            dc2 = jnp.where(rows == n - 2, hl[0:1], jnp.where(rows == n - 1, hl[1:2], pltpu.roll(dconv, n - 2, 0)))
            halo[...] = dconv[0:8]
            cw = cw_ref[...]
            du = cw[2:3] * dconv + cw[1:2] * dc1 + cw[0:1] * dc2
            u = cc * cx
            accc_ref[0:1, :] += jnp.sum(u * dc2, axis=0, keepdims=True)
            accc_ref[1:2, :] += jnp.sum(u * dc1, axis=0, keepdims=True)
            accc_ref[2:3, :] += jnp.sum(u * dconv, axis=0, keepdims=True)
            dcx = (du * cc).astype(BF16)
            dcc = (du * cx).astype(BF16)
            dhn = lax.dot_general(dpr_ref[rs, 0:BLK], w_ref[4], NT, preferred_element_type=F32)
            for j in range(1, 4):
                dhn += lax.dot_general(dpr_ref[rs, j * BLK : (j + 1) * BLK], w_ref[4 + j], NT, preferred_element_type=F32)
            for blk, d in ((1, dcb), (3, dcg), (0, dcx), (2, dcc)):
                dpc_ref[rs, blk * BLK : (blk + 1) * BLK] = d
                dhn += lax.dot_general(d, w_ref[blk], NT, preferred_element_type=F32)
            r = lax.rsqrt(jnp.mean(h * h, axis=-1, keepdims=True) + EPS)
            hh = h * r
            acc_ref[0:1, :] += jnp.sum(dhn * hh, axis=0, keepdims=True)
            dg = dhn * g_ref[...]
            return dh2_ref[rs, :] + r * (dg - hh * jnp.mean(dg * hh, axis=-1, keepdims=True))

        @pl.when(tile == 0)
        def _():
            dh = work(mc_ref[...], LIVE0)
            dm_ref[...] = dh[CHUNK - N_META : CHUNK]

        @pl.when(tile > 0)
        def _():
            gx_ref[...] = work(x_ref[...], 0)

    rev = lambda w, j=0: pl.BlockSpec((tm, w), lambda i: (nt - 1 - i, j))
    tok = pl.BlockSpec((tm, D_MODEL), lambda i: (_token_tile(nt - 1 - i), 0))
    const = lambda r, c: pl.BlockSpec((r, c), lambda i: (0, 0))
    return pl.pallas_call(
        body,
        name="b2_dconv_dh",
        grid=(nt,),
        in_specs=[rev(half, 1), rev(D_CONV), rev(half, 0), rev(D_CONV), _VMEM, _VMEM, tok, _VMEM, rev(D_MODEL), _VMEM],
        out_specs=(rev(half, 0), tok, const(N_META, D_MODEL), const(8, D_MODEL), const(8, D_CONV)),
        out_shape=(
            jax.ShapeDtypeStruct(dproj.shape, BF16),
            jax.ShapeDtypeStruct(x2.shape, F32),
            jax.ShapeDtypeStruct((N_META, D_MODEL), F32),
            jax.ShapeDtypeStruct((8, D_MODEL), F32),
            jax.ShapeDtypeStruct((8, D_CONV), F32),
        ),
        scratch_shapes=[pltpu.VMEM((8, D_CONV), F32)],
        input_output_aliases={0: 0},
        compiler_params=_params(),
    )(dproj, dco, proj, conv, w3, cw8, x2, meta_chunk, dh2, g1)


def _gw_in_scatter(hnT, dproj, gw_out_parts, sc_parts, ga_part, me_arr):
    n_rows = dproj.shape[0]
    last = N_DEV - 1
    by_dest = (True, True, False)

    def body(me_ref, a_ref, b_ref, go_ref, sc_ref, ga_ref, land_in, land_go, land_sc, land_ga,
             d2d_buf, d2d_land, ici_buf, d2d_send, d2d_recv, ici_send, ici_recv, send_sems, recv_sems, local_sems):
        del me_ref
        s = pl.program_id(0)
        t = last - s
        q = t >> 1
        x, y, c = lax.axis_index("x"), lax.axis_index("y"), lax.axis_index("c")
        me = 4 * x + 2 * y + c
        chip = 2 * x + y
        srcs, lands = (go_ref, sc_ref, ga_ref), (land_go, land_sc, land_ga)

        def peer_at(k):
            return (1 - x if k & 4 else x, 1 - y if k & 2 else y, 1 - c if k & 1 else c)

        def small_copy(a, k):
            px, py, pc = peer_at(k)
            return pltpu.make_async_remote_copy(
                src_ref=srcs[a].at[4 * px + 2 * py + pc] if by_dest[a] else srcs[a],
                dst_ref=lands[a].at[me],
                send_sem=send_sems.at[a * last + k - 1],
                recv_sem=recv_sems.at[a * last + k - 1],
                device_id=(px, py, pc),
                device_id_type=MESH,
            )

        def small_local(a):
            return pltpu.make_async_copy(srcs[a].at[me] if by_dest[a] else srcs[a], lands[a].at[me], local_sems.at[a + 1])

        def d2d_copy(j):
            return pltpu.make_async_remote_copy(
                src_ref=d2d_buf.at[j],
                dst_ref=d2d_land.at[j],
                send_sem=d2d_send.at[j],
                recv_sem=d2d_recv.at[j],
                device_id=(x, y, 1 - c),
                device_id_type=MESH,
            )

        def ici_copy(j, to):
            return pltpu.make_async_remote_copy(
                src_ref=ici_buf.at[j],
                dst_ref=land_in.at[chip],
                send_sem=ici_send.at[j],
                recv_sem=ici_recv.at[j],
                device_id=to,
                device_id_type=MESH,
            )

        def own_copy():
            return pltpu.make_async_copy(ici_buf.at[0], land_in.at[chip], local_sems.at[0])

        @pl.when(s == 0)
        def _():
            for a in range(3):
                small_local(a).start()
            for k in range(last, 0, -1):
                for a in range(3):
                    small_copy(a, k).start()

        blk = jnp.dot(a_ref[:, LIVE0:], b_ref[LIVE0:, :], preferred_element_type=F32)

        @pl.when((t & 1) == 1)
        def _():
            d2d_buf[q] = blk.astype(BF16)
            d2d_copy(q).start()

        @pl.when((t & 1) == 0)
        def _():
            d2d_copy(q).wait_recv()
            ici_buf[q] = (blk + d2d_land[q].astype(F32)).astype(BF16)

            @pl.when(t != 0)
            def _():
                ici_copy(q, (jnp.bitwise_xor(x, (t >> 2) & 1), jnp.bitwise_xor(y, (t >> 1) & 1), c)).start()

            @pl.when(t == 0)
            def _():
                own_copy().start()
                for j in range(N_CHIPS):
                    d2d_copy(j).wait_send()
                for j in range(N_CHIPS - 1, 0, -1):
                    ici_copy(j, peer_at(2 * j)).wait()
                for k in range(last, 0, -1):
                    for a in range(3):
                        small_copy(a, k).wait()
                for a in range(3):
                    small_local(a).wait()
                own_copy().wait()

    smalls = (gw_out_parts, sc_parts, ga_part)
    small_blocks = [p.shape[1:] if d else p.shape for p, d in zip(smalls, by_dest, strict=True)]
    grid_spec = pltpu.PrefetchScalarGridSpec(
        num_scalar_prefetch=1,
        grid=(N_DEV,),
        in_specs=[_VMEM, pl.BlockSpec((n_rows, BLK), lambda s, me: (0, jnp.bitwise_xor(me[0], last - s)))] + [_HBM] * 3,
        out_specs=tuple([_HBM] * 4),
        scratch_shapes=[
            pltpu.VMEM((N_CHIPS, D_MODEL, BLK), BF16),
            pltpu.VMEM((N_CHIPS, D_MODEL, BLK), BF16),
            pltpu.VMEM((N_CHIPS, D_MODEL, BLK), BF16),
            pltpu.SemaphoreType.DMA((N_CHIPS,)),
            pltpu.SemaphoreType.DMA((N_CHIPS,)),
            pltpu.SemaphoreType.DMA((N_CHIPS,)),
            pltpu.SemaphoreType.DMA((N_CHIPS,)),
            pltpu.SemaphoreType.DMA((3 * last,)),
            pltpu.SemaphoreType.DMA((3 * last,)),
            pltpu.SemaphoreType.DMA((4,)),
        ],
    )
    return pl.pallas_call(
        body,
        name="gw_in_scatter",
        grid_spec=grid_spec,
        out_shape=(jax.ShapeDtypeStruct((N_CHIPS, D_MODEL, BLK), BF16),)
        + tuple(jax.ShapeDtypeStruct((N_DEV, *b), p.dtype) for b, p in zip(small_blocks, smalls, strict=True)),
        compiler_params=_params(),
    )(me_arr, hnT, dproj, gw_out_parts, sc_parts, ga_part)


def _adamw(w, g, m, v):
    m = ADAM_B1 * m + (1.0 - ADAM_B1) * g
    v = ADAM_B2 * v + (1.0 - ADAM_B2) * (g * g)
    m_hat = m / (1.0 - ADAM_B1**ADAM_STEP)
    v_hat = v / (1.0 - ADAM_B2**ADAM_STEP)
    delta = -ADAM_LR * (m_hat / (jnp.sqrt(v_hat) + ADAM_EPS) + ADAM_WD * w)
    return delta, m, v


def _sum_adamw(name, parts, w, m, v, rows_per_step):
    n_r, n_c = w.shape
    n_parts = parts.shape[0]
    tr = rows_per_step

    def body(p_ref, w_ref, m_ref, v_ref, g_ref, d_ref, nm_ref, nv_ref):
        g = p_ref[0].astype(F32)
        for s in range(1, n_parts):
            g = g + p_ref[s].astype(F32)
        g_ref[...] = g
        d_ref[...], nm_ref[...], nv_ref[...] = _adamw(w_ref[...], g, m_ref[...], v_ref[...])

    blk = pl.BlockSpec((tr, n_c), lambda i: (i, 0))
    return pl.pallas_call(
        body,
        name=name,
        grid=(n_r // tr,),
        in_specs=[pl.BlockSpec((n_parts, tr, n_c), lambda i: (0, i, 0)), blk, blk, blk],
        out_specs=(blk,) * 4,
        out_shape=(jax.ShapeDtypeStruct((n_r, n_c), F32),) * 4,
        compiler_params=_params(),
    )(parts, w, m, v)


def _small_leaves(meta, conv_w, n1, rg, fg):
    cols = D_CONV // N_DEV
    return (
        meta,
        jnp.pad(conv_w, ((0, 5), (0, CHUNK - cols))),
        n1.reshape(8, CHUNK),
        rg.reshape(4, CHUNK),
        fg.reshape(8, CHUNK),
    )


def _from_small_leaves(meta, conv8, n1, rg, fg):
    return meta, conv8[0:3, : D_CONV // N_DEV], n1.reshape(D_MODEL), rg.reshape(D_RET), fg.reshape(D_MODEL)


def _adamw_small(land_sc, land_ga, w, m, v):
    n_leaf = 5

    def body(sc_ref, ga_ref, *refs):
        ins, outs = refs[: 3 * n_leaf], refs[3 * n_leaf :]
        sc, ga = sc_ref[0], ga_ref[0]
        for s in range(1, N_DEV):
            sc = sc + sc_ref[s]
            ga = ga + ga_ref[s]
        grads = (
            sc[SC_META0 : SC_META0 + N_META],
            sc[SC_CONV0 : SC_CONV0 + 8],
            ga[GA_N1 : GA_N1 + 8],
            ga[GA_RG : GA_RG + 4],
            ga[GA_FG : GA_FG + 8],
        )
        for leaf, g in enumerate(grads):
            d, nm, nv = _adamw(ins[leaf][...], g, ins[n_leaf + leaf][...], ins[2 * n_leaf + leaf][...])
            outs[leaf][...] = g
            outs[n_leaf + leaf][...] = d
            outs[2 * n_leaf + leaf][...] = nm
            outs[3 * n_leaf + leaf][...] = nv
        outs[4 * n_leaf][...] = ga[GA_LOSS : GA_LOSS + 1]

    leaf_shapes = tuple(jax.ShapeDtypeStruct(a.shape, F32) for a in w)
    out = pl.pallas_call(
        body,
        name="adamw_small",
        out_shape=leaf_shapes * 4 + (jax.ShapeDtypeStruct((1, CHUNK), F32),),
    )(land_sc, land_ga, *w, *m, *v)
    return tuple(out[k * n_leaf : (k + 1) * n_leaf] for k in range(4)), out[4 * n_leaf]


def kernel(x, meta, norm1_g, w_in, conv_w, ret_norm_g, w_out, final_g, loss_target, m_meta, m_norm1_g, m_w_in, m_conv_w, m_ret_norm_g, m_w_out, m_final_g, v_meta, v_norm1_g, v_w_in, v_conv_w, v_ret_norm_g, v_w_out, v_final_g):
    seq = x.shape[1]
    assert x.shape == (1, seq, D_MODEL) and seq % ROW_TILE == 0
    n_tiles = seq // ROW_TILE + 1
    cols = D_CONV // N_DEV
    x2, t2 = x[0], loss_target[0]

    small_w = _small_leaves(meta, conv_w, norm1_g, ret_norm_g, final_g)
    w3, small = _gather("gather_weights", [w_in.astype(BF16), jnp.concatenate(small_w[0:2], axis=0)])
    meta_full = small[:, SC_META0 : SC_META0 + N_META, :].transpose(1, 0, 2).reshape(N_META, D_MODEL)
    conv_full = small[:, SC_CONV0 : SC_CONV0 + 3, :cols].transpose(1, 0, 2).reshape(3, D_CONV)
    cw8 = jnp.pad(conv_full, ((0, 5), (0, 0)))
    meta_chunk = jnp.pad(meta_full, ((CHUNK - N_META, 0), (0, 0)))

    rot = _rotary_tables(n_tiles, ROW_TILE)
    dec, dect, zeta, xi = _decay_tables()
    g1 = norm1_g.reshape(1, D_MODEL)
    gr = ret_norm_g.reshape(1, D_RET)
    gf = final_g.reshape(1, D_MODEL)

    hnT, proj, conv_out, conv, wo3 = _f1(x2, meta_chunk, g1, w3, cw8, rot, w_out.astype(BF16))
    wout = wo3.reshape(D_MODEL, D_MODEL)
    dh2, o, states, gw_out, acc_f2 = _f2(proj, conv_out, x2, t2, wout, gr, gf, dec, zeta, xi)
    dproj_ret, dco, acc_b1 = _b1(dh2, proj, o, states, wout, gr, dec, dect, zeta, xi, rot)
    dproj, grad_x2, dmeta_rows, acc_b2, acc_conv = _b2(dproj_ret, dco, proj, conv, w3, cw8, x2, meta_chunk, dh2, g1)
    gw_out_parts = gw_out.reshape(N_DEV, D_MODEL // N_DEV, D_MODEL)

    dmeta = dmeta_rows.reshape(N_META, N_DEV, CHUNK).transpose(1, 0, 2)
    dconv = jnp.pad(acc_conv[0:3, :].reshape(3, N_DEV, cols).transpose(1, 0, 2), ((0, 0), (0, 5), (0, CHUNK - cols)))
    sc_parts = jnp.concatenate([dmeta, dconv], axis=1)
    ga_part = jnp.concatenate(
        [
            acc_b2[0].reshape(8, CHUNK),
            jnp.pad(acc_b1[0].reshape(4, CHUNK), ((0, 4), (0, 0))),
            acc_f2[0].reshape(8, CHUNK),
            jnp.pad(acc_f2[1:2, 0:CHUNK], ((0, 7), (0, 0))),
        ],
        axis=0,
    )

    me_arr = (4 * lax.axis_index("x") + 2 * lax.axis_index("y") + lax.axis_index("c")).astype(jnp.int32).reshape(1)
    land_in, land_out, land_sc, land_ga = _gw_in_scatter(hnT, dproj, gw_out_parts, sc_parts, ga_part, me_arr)

    g_w_in, d_w_in, nm_w_in, nv_w_in = _sum_adamw("adamw_w_in", land_in, w_in, m_w_in, v_w_in, 256)
    g_w_out, d_w_out, nm_w_out, nv_w_out = _sum_adamw("adamw_w_out", land_out, w_out, m_w_out, v_w_out, 64)
    small_out, loss_row = _adamw_small(
        land_sc,
        land_ga,
        small_w,
        _small_leaves(m_meta, m_conv_w, m_norm1_g, m_ret_norm_g, m_final_g),
        _small_leaves(v_meta, v_conv_w, v_norm1_g, v_ret_norm_g, v_final_g),
    )
    loss = loss_row[0, 0]
    grad_x = grad_x2.reshape(1, seq, D_MODEL)

    def leaves(w_in_leaf, w_out_leaf, small_leaves):
        meta_leaf, conv_leaf, n1_leaf, rg_leaf, fg_leaf = _from_small_leaves(*small_leaves)
        return (meta_leaf, n1_leaf, w_in_leaf, conv_leaf, rg_leaf, w_out_leaf, fg_leaf)

    return (
        loss,
        grad_x,
        *leaves(g_w_in, g_w_out, small_out[0]),
        *leaves(d_w_in, d_w_out, small_out[1]),
        *leaves(nm_w_in, nm_w_out, small_out[2]),
        *leaves(nv_w_in, nv_w_out, small_out[3]),
    )
```

```python
import functools
import math

import jax
import jax.numpy as jnp
import numpy as np
from jax import lax
from jax.experimental import pallas as pl
from jax.experimental.pallas import tpu as pltpu

F32 = jnp.float32
BF16 = jnp.bfloat16

N_DEV = 8
N_CHIPS = 4
D_MODEL = 1024
N_META = 16
CHUNK = 128
D_CONV = 512
D_RET = 512
N_HEADS = 4
HEAD_DIM = 128
N_PROJ = 8
BLK = 512
ROPE_BASE = 10000.0
EPS = 1e-6
Q_SCALE = HEAD_DIM ** -0.5
LOG_G = tuple(math.log(1.0 - 2.0 ** (-5.0 - h)) for h in range(N_HEADS))
CHUNK_DECAY = tuple(math.exp(CHUNK * lg) for lg in LOG_G)

ADAM_LR = 0.001
ADAM_B1 = 0.9
ADAM_B2 = 0.999
ADAM_EPS = 1e-08
ADAM_WD = 0.01
ADAM_STEP = 10

ROW_TILE = 512
PAD_ROWS = ROW_TILE - N_META
LIVE0 = ROW_TILE - CHUNK
VMEM_LIMIT = 56 * 1024 * 1024

SC_META0, SC_CONV0 = 0, 16
GA_N1, GA_RG, GA_FG, GA_LOSS = 0, 8, 16, 24

NT = (((1,), (1,)), ((), ()))
TN = (((0,), (0,)), ((), ()))
MESH = pl.DeviceIdType.MESH

_VMEM = pl.BlockSpec(memory_space=pltpu.VMEM)
_HBM = pl.BlockSpec(memory_space=pltpu.HBM)


def _params(n_axes=1):
    return pltpu.CompilerParams(dimension_semantics=("arbitrary",) * n_axes, vmem_limit_bytes=VMEM_LIMIT)


def _sigmoid(x):
    return 0.5 * jnp.tanh(0.5 * x) + 0.5


def _decay_tables():
    idx = np.arange(CHUNK, dtype=np.float64)
    diff = idx[:, None] - idx[None, :]
    dec = np.stack([np.where(diff >= 0, np.exp(diff * lg), 0.0) for lg in LOG_G])
    zeta = np.stack([np.exp((CHUNK - 1 - idx) * lg) for lg in LOG_G])
    xi = np.stack([np.exp((idx + 1.0) * lg) for lg in LOG_G])
    ones = np.ones((1, 1, CHUNK))
    return (
        jnp.asarray(dec, F32),
        jnp.asarray(dec.transpose(0, 2, 1), F32),
        jnp.asarray(zeta[:, :, None] * ones, F32),
        jnp.asarray(xi[:, :, None] * ones, F32),
    )


def _rotary_tables(n_tiles, tm):
    half = HEAD_DIM // 2
    freqs = (1.0 / (np.float32(ROPE_BASE) ** (np.arange(half, dtype=np.float32) / np.float32(half)))).astype(np.float64)
    sign = np.concatenate([-np.ones(half), np.ones(half)])
    two = lambda a: np.concatenate([a, a], axis=1)
    base = two((np.arange(n_tiles, dtype=np.float64) * tm - PAD_ROWS)[:, None] * freqs[None, :])
    off = two(np.arange(tm, dtype=np.float64)[:, None] * freqs[None, :])
    as32 = lambda a: jnp.asarray(a, F32)
    return as32(np.cos(base)), as32(np.sin(base) * sign), as32(np.cos(off)), as32(np.sin(off) * sign)


def _tile_rotary(ca_ref, sa_ref, cb_ref, sb_ref, tile):
    ca, sa = ca_ref[pl.ds(tile, 1), :], sa_ref[pl.ds(tile, 1), :]
    cb, sb = cb_ref[...], sb_ref[...]
    return ca * cb - sa * sb, sa * cb + ca * sb


def _rot(t, cos2, sin2):
    return t * cos2 + pltpu.roll(t, HEAD_DIM // 2, 1) * sin2


def _rot_bwd(d, cos2, sin2):
    return d * cos2 + pltpu.roll(d * sin2, HEAD_DIM // 2, 1)


def _token_tile(i):
    return jnp.maximum(i - 1, 0)


GATHER_PAIRS = 7


def _two_level_gather(src_ref, out_ref, send_sems, recv_sems, local_sem, base=0):
    x, y, c = lax.axis_index("x"), lax.axis_index("y"), lax.axis_index("c")
    me, sibling = (x, y, c), (x, y, 1 - c)
    chips = [(1 - x, y), (x, 1 - y), (1 - x, 1 - y)]

    def copy(k, block, to, src=None):
        dst = out_ref.at[4 * block[0] + 2 * block[1] + block[2]]
        return pltpu.make_async_remote_copy(
            src_ref=dst if src is None else src,
            dst_ref=dst,
            send_sem=send_sems.at[base + k],
            recv_sem=recv_sems.at[base + k],
            device_id=to,
            device_id_type=MESH,
        )

    mine = pltpu.make_async_copy(src_ref, out_ref.at[4 * x + 2 * y + c], local_sem)
    first = [copy(1 + j, me, (*chip, c), src=src_ref) for j, chip in enumerate(chips)]
    first.append(copy(0, me, sibling, src=src_ref))
    passed = [copy(4 + j, (*chip, c), sibling) for j, chip in enumerate(chips)]

    def start():
        mine.start()
        for cp in first:
            cp.start()

    def forward():
        for j, chip in enumerate(chips):
            copy(1 + j, (*chip, c), me).wait_recv()
            passed[j].start()

    def finish():
        copy(0, sibling, me).wait_recv()
        for j, chip in enumerate(chips):
            copy(4 + j, (*chip, 1 - c), me).wait_recv()
        for cp in first + passed:
            cp.wait_send()
        mine.wait()

    return start, forward, finish


def _gather(name, srcs):
    n = len(srcs)
    per = GATHER_PAIRS

    def body(*refs):
        src_refs, out_refs = refs[:n], refs[n : 2 * n]
        send_sems, recv_sems, local_sems = refs[2 * n :]
        parts = [
            _two_level_gather(src_refs[a], out_refs[a], send_sems, recv_sems, local_sems.at[a], a * per)
            for a in range(n)
        ]
        for phase in range(3):
            for part in parts:
                part[phase]()

    return pl.pallas_call(
        body,
        name=name,
        out_shape=tuple(jax.ShapeDtypeStruct((N_DEV, *s.shape), s.dtype) for s in srcs),
        in_specs=[_HBM] * n,
        out_specs=tuple([_HBM] * n),
        scratch_shapes=[
            pltpu.SemaphoreType.DMA((n * per,)),
            pltpu.SemaphoreType.DMA((n * per,)),
            pltpu.SemaphoreType.DMA((n,)),
        ],
    )(*srcs)


def _f1(x2, meta_chunk, g1, w3, cw8, rot, wo_shard):
    tm = ROW_TILE
    nt = x2.shape[0] // tm + 1
    n_rows = nt * tm

    def body(x_ref, mc_ref, g_ref, w_ref, cw_ref, ca_ref, sa_ref, cb_ref, sb_ref, wo_ref,
             hnT_ref, pr_ref, co_ref, cv_ref, wT_ref, wo_all_ref, halo, send_sems, recv_sems, local_sem):
        i = pl.program_id(0)
        wo_start, wo_forward, wo_finish = _two_level_gather(wo_ref, wo_all_ref, send_sems, recv_sems, local_sem)
        pl.when(i == 0)(wo_start)
        pl.when(i == nt // 2)(wo_forward)
        pl.when(i == nt - 1)(wo_finish)

        def work(h, r0):
            rs = slice(r0, tm)
            n = tm - r0
            r = lax.rsqrt(jnp.mean(h * h, axis=-1, keepdims=True) + EPS)
            hn = (h * r * g_ref[...]).astype(BF16)
            hnT_ref[:, rs] = hn.T

            def proj(j):
                return jnp.dot(hn, w_ref[j], preferred_element_type=F32)

            cx, cb, cc, cg = proj(0), proj(1), proj(2), proj(3)
            u = cc * cx
            rows = lax.broadcasted_iota(jnp.int32, u.shape, 0)
            hl = halo[...]
            u1 = jnp.where(rows == 0, hl[7:8], pltpu.roll(u, 1, 0))
            u2 = jnp.where(rows == 0, hl[6:7], jnp.where(rows == 1, hl[7:8], pltpu.roll(u, 2, 0)))
            halo[...] = u[n - 8 : n]
            cw = cw_ref[...]
            conv = cw[0:1] * u2 + cw[1:2] * u1 + cw[2:3] * u
            co = (cb * conv * (cg * _sigmoid(cg))).astype(BF16)
            co_ref[rs, :] = co
            cv_ref[rs, :] = conv.astype(BF16)
            pr_ref[rs, 0 * BLK : 1 * BLK] = cx.astype(BF16)
            pr_ref[rs, 1 * BLK : 2 * BLK] = cb.astype(BF16)
            pr_ref[rs, 2 * BLK : 3 * BLK] = cc.astype(BF16)
            pr_ref[rs, 3 * BLK : 4 * BLK] = cg.astype(BF16)
            cos_all, sin_all = _tile_rotary(ca_ref, sa_ref, cb_ref, sb_ref, i)
            cos_t, sin_t = cos_all[rs, :], sin_all[rs, :]
            q, k = proj(4), proj(5)
            for hd in range(N_HEADS):
                c0 = hd * HEAD_DIM
                pr_ref[rs, 4 * BLK + c0 : 4 * BLK + c0 + HEAD_DIM] = (
                    _rot(q[:, c0 : c0 + HEAD_DIM], cos_t, sin_t) * Q_SCALE
                ).astype(BF16)
                pr_ref[rs, 5 * BLK + c0 : 5 * BLK + c0 + HEAD_DIM] = _rot(
                    k[:, c0 : c0 + HEAD_DIM], cos_t, sin_t
                ).astype(BF16)
            pr_ref[rs, 6 * BLK : 7 * BLK] = proj(6).astype(BF16)
            pr_ref[rs, 7 * BLK : 8 * BLK] = proj(7).astype(BF16)

        @pl.when(i == 0)
        def _():
            halo[...] = jnp.zeros_like(halo)
            work(mc_ref[...], LIVE0)
            for j in range(N_PROJ):
                wT_ref[j] = w_ref[j].T

        @pl.when(i > 0)
        def _():
            work(x_ref[...], 0)

    row = lambda w: pl.BlockSpec((tm, w), lambda i: (i, 0))
    col = lambda w: pl.BlockSpec((w, tm), lambda i: (0, i))
    return pl.pallas_call(
        body,
        name="f1_inproj_conv",
        grid=(nt,),
        in_specs=[pl.BlockSpec((tm, D_MODEL), lambda i: (_token_tile(i), 0))] + [_VMEM] * 8 + [_HBM],
        out_specs=(
            col(D_MODEL),
            row(N_PROJ * BLK),
            row(D_CONV),
            row(D_CONV),
            pl.BlockSpec((N_PROJ, BLK, D_MODEL), lambda i: (0, 0, 0)),
            _HBM,
        ),
        out_shape=(
            jax.ShapeDtypeStruct((D_MODEL, n_rows), BF16),
            jax.ShapeDtypeStruct((n_rows, N_PROJ * BLK), BF16),
            jax.ShapeDtypeStruct((n_rows, D_CONV), BF16),
            jax.ShapeDtypeStruct((n_rows, D_CONV), BF16),
            jax.ShapeDtypeStruct((N_PROJ, BLK, D_MODEL), BF16),
            jax.ShapeDtypeStruct((N_DEV, *wo_shard.shape), wo_shard.dtype),
        ),
        scratch_shapes=[
            pltpu.VMEM((8, D_CONV), F32),
            pltpu.SemaphoreType.DMA((GATHER_PAIRS,)),
            pltpu.SemaphoreType.DMA((GATHER_PAIRS,)),
            pltpu.SemaphoreType.DMA(()),
        ],
        compiler_params=_params(),
    )(x2, meta_chunk, g1, w3, cw8, *rot, wo_shard)


def _group_norm(o):
    ys, rs = [], []
    for hd in range(N_HEADS):
        oh = o[:, hd * HEAD_DIM : (hd + 1) * HEAD_DIM]
        xc = oh - jnp.mean(oh, axis=-1, keepdims=True)
        rstd = lax.rsqrt(jnp.mean(xc * xc, axis=-1, keepdims=True) + EPS)
        ys.append(xc * rstd)
        rs.append(jnp.broadcast_to(rstd, oh.shape))
    return jnp.concatenate(ys, axis=1), jnp.concatenate(rs, axis=1)


def _f2(proj, conv_out, x2, t2, wout, gr, gf, dec, zeta, xi):
    n_rows = proj.shape[0]
    tm = ROW_TILE
    nt = n_rows // tm
    nct = tm // CHUNK

    def body(q_ref, k_ref, v_ref, rg_ref, co_ref, x_ref, t_ref, wo_ref, gr_ref, gf_ref, dec_ref, zeta_ref,
             xi_ref, dh2_ref, o_ref, st_ref, gw_ref, acc_ref, state, obuf, gacc):
        i = pl.program_id(0)

        def mixer(r0):
            chunks = range(r0 // CHUNK, nct)
            for hd in range(N_HEADS):
                cs = slice(hd * HEAD_DIM, (hd + 1) * HEAD_DIM)
                upd = {}
                for c in chunks:
                    rs = slice(c * CHUNK, (c + 1) * CHUNK)
                    kz = (k_ref[rs, cs].astype(F32) * zeta_ref[hd]).astype(BF16)
                    upd[c] = lax.dot_general(kz, v_ref[rs, cs], TN, preferred_element_type=F32)
                st = state[hd]
                for c in chunks:
                    rs = slice(c * CHUNK, (c + 1) * CHUNK)
                    q, k, v = q_ref[rs, cs], k_ref[rs, cs], v_ref[rs, cs]
                    st_bf = st.astype(BF16)
                    st_ref[c, hd] = st_bf
                    s = lax.dot_general(q, k, NT, preferred_element_type=F32) * dec_ref[hd]
                    inner = jnp.dot(s.astype(BF16), v, preferred_element_type=F32)
                    qx = (q.astype(F32) * xi_ref[hd]).astype(BF16)
                    obuf[rs, cs] = inner + jnp.dot(qx, st_bf, preferred_element_type=F32)
                    st = CHUNK_DECAY[hd] * st + upd[c]
                state[hd] = st
            o = obuf[r0:tm, :]
            o_ref[r0:tm, :] = o.astype(BF16)
            yh, _ = _group_norm(o)
            rg = rg_ref[r0:tm, :].astype(F32)
            return (yh * gr_ref[...] * (rg * _sigmoid(rg))).astype(BF16)

        @pl.when(i == 0)
        def _():
            state[...] = jnp.zeros_like(state)
            acc_ref[...] = jnp.zeros_like(acc_ref)
            gacc[...] = jnp.zeros_like(gacc)
            mixer(LIVE0)
            dh2_ref[...] = jnp.zeros_like(dh2_ref)

        @pl.when(i > 0)
        def _():
            ro = mixer(0)
            h2 = (
                x_ref[...]
                + jnp.dot(co_ref[...], wo_ref[0:D_CONV], preferred_element_type=F32)
                + jnp.dot(ro, wo_ref[D_CONV:], preferred_element_type=F32)
            )
            r2 = lax.rsqrt(jnp.mean(h2 * h2, axis=-1, keepdims=True) + EPS)
            yn = h2 * r2
            gfv = gf_ref[...]
            err = yn * gfv - t_ref[...]
            tile_loss = jnp.sum(jnp.sum(err * err, axis=-1, keepdims=True), axis=0, keepdims=True) * (0.5 / D_MODEL)
            acc_ref[0:1, :] += jnp.sum(err * yn, axis=0, keepdims=True) * (1.0 / D_MODEL)
            acc_ref[1:2, :] += tile_loss
            dyn = err * (gfv * (1.0 / D_MODEL))
            dh2 = r2 * (dyn - yn * jnp.mean(dyn * yn, axis=-1, keepdims=True))
            dh2_ref[...] = dh2
            dh2_bf = dh2.astype(BF16)
            gacc[0:D_CONV, :] += lax.dot_general(co_ref[...], dh2_bf, TN, preferred_element_type=F32)
            gacc[D_CONV:, :] += lax.dot_general(ro, dh2_bf, TN, preferred_element_type=F32)

        @pl.when(i == nt - 1)
        def _():
            gw_ref[...] = gacc[...].astype(BF16)

    row = lambda w, j=0: pl.BlockSpec((tm, w), lambda i: (i, j))
    tok = pl.BlockSpec((tm, D_MODEL), lambda i: (_token_tile(i), 0))
    return pl.pallas_call(
        body,
        name="f2_retention_out",
        grid=(nt,),
        in_specs=[row(BLK, 4), row(BLK, 5), row(BLK, 6), row(BLK, 7), row(D_CONV), tok, tok] + [_VMEM] * 6,
        out_specs=(
            row(D_MODEL),
            row(D_RET),
            pl.BlockSpec((nct, N_HEADS, HEAD_DIM, HEAD_DIM), lambda i: (i, 0, 0, 0)),
            pl.BlockSpec((D_MODEL, D_MODEL), lambda i: (0, 0)),
            pl.BlockSpec((8, D_MODEL), lambda i: (0, 0)),
        ),
        out_shape=(
            jax.ShapeDtypeStruct((n_rows, D_MODEL), F32),
            jax.ShapeDtypeStruct((n_rows, D_RET), BF16),
            jax.ShapeDtypeStruct((n_rows // CHUNK, N_HEADS, HEAD_DIM, HEAD_DIM), BF16),
            jax.ShapeDtypeStruct((D_MODEL, D_MODEL), BF16),
            jax.ShapeDtypeStruct((8, D_MODEL), F32),
        ),
        scratch_shapes=[
            pltpu.VMEM((N_HEADS, HEAD_DIM, HEAD_DIM), F32),
            pltpu.VMEM((tm, D_RET), F32),
            pltpu.VMEM((D_MODEL, D_MODEL), F32),
        ],
        compiler_params=_params(),
    )(proj, proj, proj, proj, conv_out, x2, t2, wout, gr, gf, dec, zeta, xi)


def _b1(dh2, proj, o, states, wout, gr, dec, dect, zeta, xi, rot):
    n_rows = dh2.shape[0]
    tm = ROW_TILE
    nt = n_rows // tm
    nct = tm // CHUNK

    def body(dh2_ref, pr_ref, o_ref, st_ref, wo_ref, gr_ref, dec_ref, dect_ref, zeta_ref, xi_ref,
             ca_ref, sa_ref, cb_ref, sb_ref, dp_ref, dco_ref, acc_ref, dstate, dobuf):
        i = pl.program_id(0)
        tile = nt - 1 - i
        cos_all, sin_all = _tile_rotary(ca_ref, sa_ref, cb_ref, sb_ref, tile)

        @pl.when(i == 0)
        def _():
            dstate[...] = jnp.zeros_like(dstate)
            acc_ref[...] = jnp.zeros_like(acc_ref)

        @pl.when(tile > 0)
        def _():
            dmix = lax.dot_general(dh2_ref[...].astype(BF16), wo_ref[...], NT, preferred_element_type=F32)
            dco_ref[...] = dmix[:, :D_CONV].astype(BF16)
            dro = dmix[:, D_CONV:]
            rg = pr_ref[:, 3 * BLK : 4 * BLK].astype(F32)
            sg = _sigmoid(rg)
            silu, dsilu = rg * sg, sg * (1.0 + rg * (1.0 - sg))
            yh, rstd = _group_norm(o_ref[...].astype(F32))
            grv = gr_ref[...]
            dp_ref[:, 3 * BLK : 4 * BLK] = (dro * (yh * grv) * dsilu).astype(BF16)
            dret = dro * silu
            acc_ref[0:1, :] += jnp.sum(dret * yh, axis=0, keepdims=True)
            dyh = dret * grv
            for hd in range(N_HEADS):
                cs = slice(hd * HEAD_DIM, (hd + 1) * HEAD_DIM)
                a, b = dyh[:, cs], yh[:, cs]
                dobuf[:, cs] = rstd[:, cs] * (
                    a - jnp.mean(a, axis=-1, keepdims=True) - b * jnp.mean(a * b, axis=-1, keepdims=True)
                )

            for hd in range(N_HEADS):
                cs = slice(hd * HEAD_DIM, (hd + 1) * HEAD_DIM)
                dupd = {}
                for c in range(nct):
                    rs = slice(c * CHUNK, (c + 1) * CHUNK)
                    qx = (pr_ref[rs, cs].astype(F32) * xi_ref[hd]).astype(BF16)
                    dupd[c] = lax.dot_general(qx, dobuf[rs, cs].astype(BF16), TN, preferred_element_type=F32)
                dst = dstate[hd]
                for c in reversed(range(nct)):
                    rs = slice(c * CHUNK, (c + 1) * CHUNK)
                    cos_t, sin_t = cos_all[rs, :], sin_all[rs, :]
                    q = pr_ref[rs, hd * HEAD_DIM : (hd + 1) * HEAD_DIM]
                    k = pr_ref[rs, BLK + hd * HEAD_DIM : BLK + (hd + 1) * HEAD_DIM]
                    v = pr_ref[rs, 2 * BLK + hd * HEAD_DIM : 2 * BLK + (hd + 1) * HEAD_DIM]
                    do = dobuf[rs, cs].astype(BF16)
                    st_bf = st_ref[c, hd]
                    dst_bf = dst.astype(BF16)
                    zt, xt = zeta_ref[hd], xi_ref[hd]
                    sT = (lax.dot_general(k, q, NT, preferred_element_type=F32) * dect_ref[hd]).astype(BF16)
                    dsT = (lax.dot_general(v, do, NT, preferred_element_type=F32) * dect_ref[hd]).astype(BF16)
                    ds = (lax.dot_general(do, v, NT, preferred_element_type=F32) * dec_ref[hd]).astype(BF16)
                    kz = (k.astype(F32) * zt).astype(BF16)
                    dv = jnp.dot(sT, do, preferred_element_type=F32) + jnp.dot(kz, dst_bf, preferred_element_type=F32)
                    dq = jnp.dot(ds, k, preferred_element_type=F32) + xt * lax.dot_general(
                        do, st_bf, NT, preferred_element_type=F32
                    )
                    dk = jnp.dot(dsT, q, preferred_element_type=F32) + zt * lax.dot_general(
                        v, dst_bf, NT, preferred_element_type=F32
                    )
                    dst = CHUNK_DECAY[hd] * dst + dupd[c]
                    dp_ref[rs, hd * HEAD_DIM : (hd + 1) * HEAD_DIM] = _rot_bwd(dq * Q_SCALE, cos_t, sin_t).astype(BF16)
                    dp_ref[rs, BLK + hd * HEAD_DIM : BLK + (hd + 1) * HEAD_DIM] = _rot_bwd(dk, cos_t, sin_t).astype(BF16)
                    dp_ref[rs, 2 * BLK + hd * HEAD_DIM : 2 * BLK + (hd + 1) * HEAD_DIM] = dv.astype(BF16)
                dstate[hd] = dst

        @pl.when(tile == 0)
        def _():
            rs = slice(LIVE0, tm)
            cos_t, sin_t = cos_all[rs, :], sin_all[rs, :]
            zeros = jnp.zeros((tm - LIVE0, HEAD_DIM), BF16)
            for hd in range(N_HEADS):
                k = pr_ref[rs, BLK + hd * HEAD_DIM : BLK + (hd + 1) * HEAD_DIM]
                v = pr_ref[rs, 2 * BLK + hd * HEAD_DIM : 2 * BLK + (hd + 1) * HEAD_DIM]
                dst_bf = dstate[hd].astype(BF16)
                zt = zeta_ref[hd]
                kz = (k.astype(F32) * zt).astype(BF16)
                dv = jnp.dot(kz, dst_bf, preferred_element_type=F32)
                dk = zt * lax.dot_general(v, dst_bf, NT, preferred_element_type=F32)
                dp_ref[rs, hd * HEAD_DIM : (hd + 1) * HEAD_DIM] = zeros
                dp_ref[rs, BLK + hd * HEAD_DIM : BLK + (hd + 1) * HEAD_DIM] = _rot_bwd(dk, cos_t, sin_t).astype(BF16)
                dp_ref[rs, 2 * BLK + hd * HEAD_DIM : 2 * BLK + (hd + 1) * HEAD_DIM] = dv.astype(BF16)
                dp_ref[rs, 3 * BLK + hd * HEAD_DIM : 3 * BLK + (hd + 1) * HEAD_DIM] = zeros
            dco_ref[rs, :] = jnp.zeros((tm - LIVE0, D_CONV), BF16)

    half = N_PROJ * BLK // 2
    rev = lambda w, j=0: pl.BlockSpec((tm, w), lambda i: (nt - 1 - i, j))
    return pl.pallas_call(
        body,
        name="b1_dret",
        grid=(nt,),
        in_specs=[
            rev(D_MODEL),
            rev(half, 1),
            rev(D_RET),
            pl.BlockSpec((nct, N_HEADS, HEAD_DIM, HEAD_DIM), lambda i: (nt - 1 - i, 0, 0, 0)),
        ]
        + [_VMEM] * 10,
        out_specs=(rev(half, 1), rev(D_CONV), pl.BlockSpec((8, D_RET), lambda i: (0, 0))),
        out_shape=(
            jax.ShapeDtypeStruct((n_rows, N_PROJ * BLK), BF16),
            jax.ShapeDtypeStruct((n_rows, D_CONV), BF16),
            jax.ShapeDtypeStruct((8, D_RET), F32),
        ),
        scratch_shapes=[pltpu.VMEM((N_HEADS, HEAD_DIM, HEAD_DIM), F32), pltpu.VMEM((tm, D_RET), F32)],
        compiler_params=_params(),
    )(dh2, proj, o, states, wout, gr, dec, dect, zeta, xi, *rot)


def _b2(dproj, dco, proj, conv, w3, cw8, x2, meta_chunk, dh2, g1):
    n_rows = dproj.shape[0]
    tm = ROW_TILE
    nt = n_rows // tm
    half = N_PROJ * BLK // 2

    def body(dpr_ref, dco_ref, pr_ref, cv_ref, w_ref, cw_ref, x_ref, mc_ref, dh2_ref, g_ref,
             dpc_ref, gx_ref, dm_ref, acc_ref, accc_ref, halo):
        i = pl.program_id(0)
        tile = nt - 1 - i

        @pl.when(i == 0)
        def _():
            acc_ref[...] = jnp.zeros_like(acc_ref)
            accc_ref[...] = jnp.zeros_like(accc_ref)
            halo[...] = jnp.zeros_like(halo)

        def work(h, r0):
            rs = slice(r0, tm)
            n = tm - r0
            dco = dco_ref[rs, :].astype(F32)
            cx = pr_ref[rs, 0 * BLK : 1 * BLK].astype(F32)
            cb = pr_ref[rs, 1 * BLK : 2 * BLK].astype(F32)
            cc = pr_ref[rs, 2 * BLK : 3 * BLK].astype(F32)
            cg = pr_ref[rs, 3 * BLK : 4 * BLK].astype(F32)
            conv = cv_ref[rs, :].astype(F32)
            sg = _sigmoid(cg)
            silu, dsilu = cg * sg, sg * (1.0 + cg * (1.0 - sg))
            t = dco * cb
            dcb = (dco * conv * silu).astype(BF16)
            dcg = (t * conv * dsilu).astype(BF16)
            dconv = t * silu
            rows = lax.broadcasted_iota(jnp.int32, dconv.shape, 0)
            hl = halo[...]
            dc1 = jnp.where(rows == n - 1, hl[0:1], pltpu.roll(dconv, n - 1, 0))
            dc2 = jnp.where(rows == n - 2, hl[0:1], jnp.where(rows == n - 1, hl[1:2], pltpu.roll(dconv, n - 2, 0)))
            halo[...] = dconv[0:8]
            cw = cw_ref[...]
            du = cw[2:3] * dconv + cw[1:2] * dc1 + cw[0:1] * dc2
            u = cc * cx
            accc_ref[0:1, :] += jnp.sum(u * dc2, axis=0, keepdims=True)
            accc_ref[1:2, :] += jnp.sum(u * dc1, axis=0, keepdims=True)
            accc_ref[2:3, :] += jnp.sum(u * dconv, axis=0, keepdims=True)
            dcx = (du * cc).astype(BF16)
            dcc = (du * cx).astype(BF16)
            dhn = jnp.dot(dpr_ref[rs, 0:BLK], w_ref[4], preferred_element_type=F32)
            for j in range(1, 4):
                dhn += jnp.dot(dpr_ref[rs, j * BLK : (j + 1) * BLK], w_ref[4 + j], preferred_element_type=F32)
            for blk, d in ((1, dcb), (3, dcg), (0, dcx), (2, dcc)):
                dpc_ref[rs, blk * BLK : (blk + 1) * BLK] = d
                dhn += jnp.dot(d, w_ref[blk], preferred_element_type=F32)
            r = lax.rsqrt(jnp.mean(h * h, axis=-1, keepdims=True) + EPS)
            hh = h * r
            acc_ref[0:1, :] += jnp.sum(dhn * hh, axis=0, keepdims=True)
            dg = dhn * g_ref[...]
            return dh2_ref[rs, :] + r * (dg - hh * jnp.mean(dg * hh, axis=-1, keepdims=True))

        @pl.when(tile == 0)
        def _():
            dh = work(mc_ref[...], LIVE0)
            dm_ref[...] = dh[CHUNK - N_META : CHUNK]

        @pl.when(tile > 0)
        def _():
            gx_ref[...] = work(x_ref[...], 0)

    rev = lambda w, j=0: pl.BlockSpec((tm, w), lambda i: (nt - 1 - i, j))
    tok = pl.BlockSpec((tm, D_MODEL), lambda i: (_token_tile(nt - 1 - i), 0))
    const = lambda r, c: pl.BlockSpec((r, c), lambda i: (0, 0))
    return pl.pallas_call(
        body,
        name="b2_dconv_dh",
        grid=(nt,),
        in_specs=[rev(half, 1), rev(D_CONV), rev(half, 0), rev(D_CONV), _VMEM, _VMEM, tok, _VMEM, rev(D_MODEL), _VMEM],
        out_specs=(rev(half, 0), tok, const(N_META, D_MODEL), const(8, D_MODEL), const(8, D_CONV)),
        out_shape=(
            jax.ShapeDtypeStruct(dproj.shape, BF16),
            jax.ShapeDtypeStruct(x2.shape, F32),
            jax.ShapeDtypeStruct((N_META, D_MODEL), F32),
            jax.ShapeDtypeStruct((8, D_MODEL), F32),
            jax.ShapeDtypeStruct((8, D_CONV), F32),
        ),
        scratch_shapes=[pltpu.VMEM((8, D_CONV), F32)],
        input_output_aliases={0: 0},
        compiler_params=_params(),
    )(dproj, dco, proj, conv, w3, cw8, x2, meta_chunk, dh2, g1)


def _gw_in_scatter(hnT, dproj, gw_out_parts, sc_parts, ga_part, me_arr):
    n_rows = dproj.shape[0]
    last = N_DEV - 1
    by_dest = (True, True, False)

    def body(me_ref, a_ref, b_ref, go_ref, sc_ref, ga_ref, land_in, land_go, land_sc, land_ga,
             d2d_buf, d2d_land, ici_buf, d2d_send, d2d_recv, ici_send, ici_recv, send_sems, recv_sems, local_sems):
        del me_ref
        s = pl.program_id(0)
        t = last - s
        q = t >> 1
        x, y, c = lax.axis_index("x"), lax.axis_index("y"), lax.axis_index("c")
        me = 4 * x + 2 * y + c
        chip = 2 * x + y
        srcs, lands = (go_ref, sc_ref, ga_ref), (land_go, land_sc, land_ga)

        def peer_at(k):
            return (1 - x if k & 4 else x, 1 - y if k & 2 else y, 1 - c if k & 1 else c)

        def small_copy(a, k):
            px, py, pc = peer_at(k)
            return pltpu.make_async_remote_copy(
                src_ref=srcs[a].at[4 * px + 2 * py + pc] if by_dest[a] else srcs[a],
                dst_ref=lands[a].at[me],
                send_sem=send_sems.at[a * last + k - 1],
                recv_sem=recv_sems.at[a * last + k - 1],
                device_id=(px, py, pc),
                device_id_type=MESH,
            )

        def small_local(a):
            return pltpu.make_async_copy(srcs[a].at[me] if by_dest[a] else srcs[a], lands[a].at[me], local_sems.at[a + 1])

        def d2d_copy(j):
            return pltpu.make_async_remote_copy(
                src_ref=d2d_buf.at[j],
                dst_ref=d2d_land.at[j],
                send_sem=d2d_send.at[j],
                recv_sem=d2d_recv.at[j],
                device_id=(x, y, 1 - c),
                device_id_type=MESH,
            )

        def ici_copy(j, to):
            return pltpu.make_async_remote_copy(
                src_ref=ici_buf.at[j],
                dst_ref=land_in.at[chip],
                send_sem=ici_send.at[j],
                recv_sem=ici_recv.at[j],
                device_id=to,
                device_id_type=MESH,
            )

        def own_copy():
            return pltpu.make_async_copy(ici_buf.at[0], land_in.at[chip], local_sems.at[0])

        @pl.when(s == 0)
        def _():
            for a in range(3):
                small_local(a).start()
            for k in range(last, 0, -1):
                for a in range(3):
                    small_copy(a, k).start()

        blk = jnp.dot(a_ref[:, LIVE0:], b_ref[LIVE0:, :], preferred_element_type=F32)

        @pl.when((t & 1) == 1)
        def _():
            d2d_buf[q] = blk.astype(BF16)
            d2d_copy(q).start()

        @pl.when((t & 1) == 0)
        def _():
            d2d_copy(q).wait_recv()
            ici_buf[q] = (blk + d2d_land[q].astype(F32)).astype(BF16)

            @pl.when(t != 0)
            def _():
                ici_copy(q, (jnp.bitwise_xor(x, (t >> 2) & 1), jnp.bitwise_xor(y, (t >> 1) & 1), c)).start()

            @pl.when(t == 0)
            def _():
                own_copy().start()
                for j in range(N_CHIPS):
                    d2d_copy(j).wait_send()
                for j in range(N_CHIPS - 1, 0, -1):
                    ici_copy(j, peer_at(2 * j)).wait()
                for k in range(last, 0, -1):
                    for a in range(3):
                        small_copy(a, k).wait()
                for a in range(3):
                    small_local(a).wait()
                own_copy().wait()

    smalls = (gw_out_parts, sc_parts, ga_part)
    small_blocks = [p.shape[1:] if d else p.shape for p, d in zip(smalls, by_dest, strict=True)]
    grid_spec = pltpu.PrefetchScalarGridSpec(
        num_scalar_prefetch=1,
        grid=(N_DEV,),
        in_specs=[_VMEM, pl.BlockSpec((n_rows, BLK), lambda s, me: (0, jnp.bitwise_xor(me[0], last - s)))] + [_HBM] * 3,
        out_specs=tuple([_HBM] * 4),
        scratch_shapes=[
            pltpu.VMEM((N_CHIPS, D_MODEL, BLK), BF16),
            pltpu.VMEM((N_CHIPS, D_MODEL, BLK), BF16),
            pltpu.VMEM((N_CHIPS, D_MODEL, BLK), BF16),
            pltpu.SemaphoreType.DMA((N_CHIPS,)),
            pltpu.SemaphoreType.DMA((N_CHIPS,)),
            pltpu.SemaphoreType.DMA((N_CHIPS,)),
            pltpu.SemaphoreType.DMA((N_CHIPS,)),
            pltpu.SemaphoreType.DMA((3 * last,)),
            pltpu.SemaphoreType.DMA((3 * last,)),
            pltpu.SemaphoreType.DMA((4,)),
        ],
    )
    return pl.pallas_call(
        body,
        name="gw_in_scatter",
        grid_spec=grid_spec,
        out_shape=(jax.ShapeDtypeStruct((N_CHIPS, D_MODEL, BLK), BF16),)
        + tuple(jax.ShapeDtypeStruct((N_DEV, *b), p.dtype) for b, p in zip(small_blocks, smalls, strict=True)),
        compiler_params=_params(),
    )(me_arr, hnT, dproj, gw_out_parts, sc_parts, ga_part)


def _adamw(w, g, m, v):
    m = ADAM_B1 * m + (1.0 - ADAM_B1) * g
    v = ADAM_B2 * v + (1.0 - ADAM_B2) * (g * g)
    m_hat = m / (1.0 - ADAM_B1**ADAM_STEP)
    v_hat = v / (1.0 - ADAM_B2**ADAM_STEP)
    delta = -ADAM_LR * (m_hat / (jnp.sqrt(v_hat) + ADAM_EPS) + ADAM_WD * w)
    return delta, m, v


def _sum_adamw(name, parts, w, m, v, rows_per_step):
    n_r, n_c = w.shape
    n_parts = parts.shape[0]
    tr = rows_per_step

    def body(p_ref, w_ref, m_ref, v_ref, g_ref, d_ref, nm_ref, nv_ref):
        g = p_ref[0].astype(F32)
        for s in range(1, n_parts):
            g = g + p_ref[s].astype(F32)
        g_ref[...] = g
        d_ref[...], nm_ref[...], nv_ref[...] = _adamw(w_ref[...], g, m_ref[...], v_ref[...])

    blk = pl.BlockSpec((tr, n_c), lambda i: (i, 0))
    return pl.pallas_call(
        body,
        name=name,
        grid=(n_r // tr,),
        in_specs=[pl.BlockSpec((n_parts, tr, n_c), lambda i: (0, i, 0)), blk, blk, blk],
        out_specs=(blk,) * 4,
        out_shape=(jax.ShapeDtypeStruct((n_r, n_c), F32),) * 4,
        compiler_params=_params(),
    )(parts, w, m, v)


def _small_leaves(meta, conv_w, n1, rg, fg):
    cols = D_CONV // N_DEV
    return (
        meta,
        jnp.pad(conv_w, ((0, 5), (0, CHUNK - cols))),
        n1.reshape(8, CHUNK),
        rg.reshape(4, CHUNK),
        fg.reshape(8, CHUNK),
    )


def _from_small_leaves(meta, conv8, n1, rg, fg):
    return meta, conv8[0:3, : D_CONV // N_DEV], n1.reshape(D_MODEL), rg.reshape(D_RET), fg.reshape(D_MODEL)


def _adamw_small(land_sc, land_ga, w, m, v):
    n_leaf = 5

    def body(sc_ref, ga_ref, *refs):
        ins, outs = refs[: 3 * n_leaf], refs[3 * n_leaf :]
        sc, ga = sc_ref[0], ga_ref[0]
        for s in range(1, N_DEV):
            sc = sc + sc_ref[s]
            ga = ga + ga_ref[s]
        grads = (
            sc[SC_META0 : SC_META0 + N_META],
            sc[SC_CONV0 : SC_CONV0 + 8],
            ga[GA_N1 : GA_N1 + 8],
            ga[GA_RG : GA_RG + 4],
            ga[GA_FG : GA_FG + 8],
        )
        for leaf, g in enumerate(grads):
            d, nm, nv = _adamw(ins[leaf][...], g, ins[n_leaf + leaf][...], ins[2 * n_leaf + leaf][...])
            outs[leaf][...] = g
            outs[n_leaf + leaf][...] = d
            outs[2 * n_leaf + leaf][...] = nm
            outs[3 * n_leaf + leaf][...] = nv
        outs[4 * n_leaf][...] = ga[GA_LOSS : GA_LOSS + 1]

    leaf_shapes = tuple(jax.ShapeDtypeStruct(a.shape, F32) for a in w)
    out = pl.pallas_call(
        body,
        name="adamw_small",
        out_shape=leaf_shapes * 4 + (jax.ShapeDtypeStruct((1, CHUNK), F32),),
    )(land_sc, land_ga, *w, *m, *v)
    return tuple(out[k * n_leaf : (k + 1) * n_leaf] for k in range(4)), out[4 * n_leaf]


def kernel(x, meta, norm1_g, w_in, conv_w, ret_norm_g, w_out, final_g, loss_target, m_meta, m_norm1_g, m_w_in, m_conv_w, m_ret_norm_g, m_w_out, m_final_g, v_meta, v_norm1_g, v_w_in, v_conv_w, v_ret_norm_g, v_w_out, v_final_g):
    seq = x.shape[1]
    assert x.shape == (1, seq, D_MODEL) and seq % ROW_TILE == 0
    n_tiles = seq // ROW_TILE + 1
    cols = D_CONV // N_DEV
    x2, t2 = x[0], loss_target[0]

    small_w = _small_leaves(meta, conv_w, norm1_g, ret_norm_g, final_g)
    w3, small = _gather("gather_weights", [w_in.astype(BF16), jnp.concatenate(small_w[0:2], axis=0)])
    meta_full = small[:, SC_META0 : SC_META0 + N_META, :].transpose(1, 0, 2).reshape(N_META, D_MODEL)
    conv_full = small[:, SC_CONV0 : SC_CONV0 + 3, :cols].transpose(1, 0, 2).reshape(3, D_CONV)
    cw8 = jnp.pad(conv_full, ((0, 5), (0, 0)))
    meta_chunk = jnp.pad(meta_full, ((CHUNK - N_META, 0), (0, 0)))

    rot = _rotary_tables(n_tiles, ROW_TILE)
    dec, dect, zeta, xi = _decay_tables()
    g1 = norm1_g.reshape(1, D_MODEL)
    gr = ret_norm_g.reshape(1, D_RET)
    gf = final_g.reshape(1, D_MODEL)

    hnT, proj, conv_out, conv, w3t, wo3 = _f1(x2, meta_chunk, g1, w3, cw8, rot, w_out.astype(BF16))
    wout = wo3.reshape(D_MODEL, D_MODEL)
    dh2, o, states, gw_out, acc_f2 = _f2(proj, conv_out, x2, t2, wout, gr, gf, dec, zeta, xi)
    dproj_ret, dco, acc_b1 = _b1(dh2, proj, o, states, wout, gr, dec, dect, zeta, xi, rot)
    dproj, grad_x2, dmeta_rows, acc_b2, acc_conv = _b2(dproj_ret, dco, proj, conv, w3t, cw8, x2, meta_chunk, dh2, g1)
    gw_out_parts = gw_out.reshape(N_DEV, D_MODEL // N_DEV, D_MODEL)

    dmeta = dmeta_rows.reshape(N_META, N_DEV, CHUNK).transpose(1, 0, 2)
    dconv = jnp.pad(acc_conv[0:3, :].reshape(3, N_DEV, cols).transpose(1, 0, 2), ((0, 0), (0, 5), (0, CHUNK - cols)))
    sc_parts = jnp.concatenate([dmeta, dconv], axis=1)
    ga_part = jnp.concatenate(
        [
            acc_b2[0].reshape(8, CHUNK),
            jnp.pad(acc_b1[0].reshape(4, CHUNK), ((0, 4), (0, 0))),
            acc_f2[0].reshape(8, CHUNK),
            jnp.pad(acc_f2[1:2, 0:CHUNK], ((0, 7), (0, 0))),
        ],
        axis=0,
    )

    me_arr = (4 * lax.axis_index("x") + 2 * lax.axis_index("y") + lax.axis_index("c")).astype(jnp.int32).reshape(1)
    land_in, land_out, land_sc, land_ga = _gw_in_scatter(hnT, dproj, gw_out_parts, sc_parts, ga_part, me_arr)

    g_w_in, d_w_in, nm_w_in, nv_w_in = _sum_adamw("adamw_w_in", land_in, w_in, m_w_in, v_w_in, 256)
    g_w_out, d_w_out, nm_w_out, nv_w_out = _sum_adamw("adamw_w_out", land_out, w_out, m_w_out, v_w_out, 64)
    small_out, loss_row = _adamw_small(
        land_sc,
        land_ga,
        small_w,
        _small_leaves(m_meta, m_conv_w, m_norm1_g, m_ret_norm_g, m_final_g),
        _small_leaves(v_meta, v_conv_w, v_norm1_g, v_ret_norm_g, v_final_g),
    )
    loss = loss_row[0, 0]
    grad_x = grad_x2.reshape(1, seq, D_MODEL)

    def leaves(w_in_leaf, w_out_leaf, small_leaves):
        meta_leaf, conv_leaf, n1_leaf, rg_leaf, fg_leaf = _from_small_leaves(*small_leaves)
        return (meta_leaf, n1_leaf, w_in_leaf, conv_leaf, rg_leaf, w_out_leaf, fg_leaf)

    return (
        loss,
        grad_x,
        *leaves(g_w_in, g_w_out, small_out[0]),
        *leaves(d_w_in, d_w_out, small_out[1]),
        *leaves(nm_w_in, nm_w_out, small_out[2]),
        *leaves(nv_w_in, nv_w_out, small_out[3]),
    )
```

```python
import functools
import math

import jax
import jax.numpy as jnp
import numpy as np
from jax import lax
from jax.experimental import pallas as pl
from jax.experimental.pallas import tpu as pltpu

F32 = jnp.float32
BF16 = jnp.bfloat16

N_DEV = 8
N_CHIPS = 4
D_MODEL = 1024
N_META = 16
CHUNK = 128
D_CONV = 512
D_RET = 512
N_HEADS = 4
HEAD_DIM = 128
N_PROJ = 8
BLK = 512
ROPE_BASE = 10000.0
EPS = 1e-6
Q_SCALE = HEAD_DIM ** -0.5
LOG_G = tuple(math.log(1.0 - 2.0 ** (-5.0 - h)) for h in range(N_HEADS))
CHUNK_DECAY = tuple(math.exp(CHUNK * lg) for lg in LOG_G)

ADAM_LR = 0.001
ADAM_B1 = 0.9
ADAM_B2 = 0.999
ADAM_EPS = 1e-08
ADAM_WD = 0.01
ADAM_STEP = 10

ROW_TILE = 512
PAD_ROWS = ROW_TILE - N_META
LIVE0 = ROW_TILE - CHUNK
VMEM_LIMIT = 56 * 1024 * 1024

SC_META0, SC_CONV0 = 0, 16
GA_N1, GA_RG, GA_FG, GA_LOSS = 0, 8, 16, 24

NT = (((1,), (1,)), ((), ()))
TN = (((0,), (0,)), ((), ()))
MESH = pl.DeviceIdType.MESH

_VMEM = pl.BlockSpec(memory_space=pltpu.VMEM)
_HBM = pl.BlockSpec(memory_space=pltpu.HBM)


def _params(n_axes=1):
    return pltpu.CompilerParams(dimension_semantics=("arbitrary",) * n_axes, vmem_limit_bytes=VMEM_LIMIT)


def _sigmoid(x):
    return 0.5 * jnp.tanh(0.5 * x) + 0.5


def _decay_tables():
    idx = np.arange(CHUNK, dtype=np.float64)
    diff = idx[:, None] - idx[None, :]
    dec = np.stack([np.where(diff >= 0, np.exp(diff * lg), 0.0) for lg in LOG_G])
    zeta = np.stack([np.exp((CHUNK - 1 - idx) * lg) for lg in LOG_G])
    xi = np.stack([np.exp((idx + 1.0) * lg) for lg in LOG_G])
    ones = np.ones((1, 1, CHUNK))
    return (
        jnp.asarray(dec, F32),
        jnp.asarray(dec.transpose(0, 2, 1), F32),
        jnp.asarray(zeta[:, :, None] * ones, F32),
        jnp.asarray(xi[:, :, None] * ones, F32),
    )


def _rotary_tables(n_tiles, tm):
    half = HEAD_DIM // 2
    freqs = (1.0 / (np.float32(ROPE_BASE) ** (np.arange(half, dtype=np.float32) / np.float32(half)))).astype(np.float64)
    sign = np.concatenate([-np.ones(half), np.ones(half)])
    two = lambda a: np.concatenate([a, a], axis=1)
    base = two((np.arange(n_tiles, dtype=np.float64) * tm - PAD_ROWS)[:, None] * freqs[None, :])
    off = two(np.arange(tm, dtype=np.float64)[:, None] * freqs[None, :])
    as32 = lambda a: jnp.asarray(a, F32)
    return as32(np.cos(base)), as32(np.sin(base) * sign), as32(np.cos(off)), as32(np.sin(off) * sign)


def _tile_rotary(ca_ref, sa_ref, cb_ref, sb_ref, tile):
    ca, sa = ca_ref[pl.ds(tile, 1), :], sa_ref[pl.ds(tile, 1), :]
    cb, sb = cb_ref[...], sb_ref[...]
    return ca * cb - sa * sb, sa * cb + ca * sb


def _rot(t, cos2, sin2):
    return t * cos2 + pltpu.roll(t, HEAD_DIM // 2, 1) * sin2


def _rot_bwd(d, cos2, sin2):
    return d * cos2 + pltpu.roll(d * sin2, HEAD_DIM // 2, 1)


def _token_tile(i):
    return jnp.maximum(i - 1, 0)


GATHER_PAIRS = 7


def _two_level_gather(src_ref, out_ref, send_sems, recv_sems, local_sem, base=0):
    x, y, c = lax.axis_index("x"), lax.axis_index("y"), lax.axis_index("c")
    me, sibling = (x, y, c), (x, y, 1 - c)
    xnb, ynb, diag = (1 - x, y), (x, 1 - y), (1 - x, 1 - y)
    relayed = (jnp.bitwise_xor(x, 1 - c), jnp.bitwise_xor(y, c))
    other = (jnp.bitwise_xor(x, c), jnp.bitwise_xor(y, 1 - c))

    def copy(k, block, to, src=None):
        dst = out_ref.at[4 * block[0] + 2 * block[1] + block[2]]
        return pltpu.make_async_remote_copy(
            src_ref=dst if src is None else src,
            dst_ref=dst,
            send_sem=send_sems.at[base + k],
            recv_sem=recv_sems.at[base + k],
            device_id=to,
            device_id_type=MESH,
        )

    mine = pltpu.make_async_copy(src_ref, out_ref.at[4 * x + 2 * y + c], local_sem)
    first = [copy(1, me, (*xnb, c), src=src_ref), copy(2, me, (*ynb, c), src=src_ref), copy(0, me, sibling, src=src_ref)]
    relay = copy(3, (*relayed, c), (*other, c))
    passed = [copy(4 + j, (*chip, c), sibling) for j, chip in enumerate((xnb, ynb, diag))]

    def start():
        mine.start()
        for cp in first:
            cp.start()

    def forward():
        copy(1 + c, (*relayed, c), me).wait_recv()
        relay.start()
        copy(2 - c, (*other, c), me).wait_recv()
        passed[0].start()
        passed[1].start()

    def forward_relayed():
        copy(3, (*diag, c), me).wait_recv()
        passed[2].start()

    def finish():
        copy(0, sibling, me).wait_recv()
        for j, chip in enumerate((xnb, ynb, diag)):
            copy(4 + j, (*chip, 1 - c), me).wait_recv()
        for cp in first + [relay] + passed:
            cp.wait_send()
        mine.wait()

    return start, forward, forward_relayed, finish


def _gather(name, srcs):
    n = len(srcs)
    per = GATHER_PAIRS

    def body(*refs):
        src_refs, out_refs = refs[:n], refs[n : 2 * n]
        send_sems, recv_sems, local_sems = refs[2 * n :]
        parts = [
            _two_level_gather(src_refs[a], out_refs[a], send_sems, recv_sems, local_sems.at[a], a * per)
            for a in range(n)
        ]
        for phase in range(4):
            for part in parts:
                part[phase]()

    return pl.pallas_call(
        body,
        name=name,
        out_shape=tuple(jax.ShapeDtypeStruct((N_DEV, *s.shape), s.dtype) for s in srcs),
        in_specs=[_HBM] * n,
        out_specs=tuple([_HBM] * n),
        scratch_shapes=[
            pltpu.SemaphoreType.DMA((n * per,)),
            pltpu.SemaphoreType.DMA((n * per,)),
            pltpu.SemaphoreType.DMA((n,)),
        ],
    )(*srcs)


def _f1(x2, meta_chunk, g1, w3, cw8, rot, wo_shard):
    tm = ROW_TILE
    nt = x2.shape[0] // tm + 1
    n_rows = nt * tm

    def body(x_ref, mc_ref, g_ref, w_ref, cw_ref, ca_ref, sa_ref, cb_ref, sb_ref, wo_ref,
             hnT_ref, pr_ref, co_ref, cv_ref, wo_all_ref, halo, send_sems, recv_sems, local_sem):
        i = pl.program_id(0)
        wo_phases = _two_level_gather(wo_ref, wo_all_ref, send_sems, recv_sems, local_sem)
        for step, phase in zip((0, nt // 3, 2 * nt // 3, nt - 1), wo_phases, strict=True):
            pl.when(i == step)(phase)

        def work(h, r0):
            rs = slice(r0, tm)
            n = tm - r0
            r = lax.rsqrt(jnp.mean(h * h, axis=-1, keepdims=True) + EPS)
            hn = (h * r * g_ref[...]).astype(BF16)
            hnT_ref[:, rs] = hn.T

            def proj(j):
                return jnp.dot(hn, w_ref[j], preferred_element_type=F32)

            cx, cb, cc, cg = proj(0), proj(1), proj(2), proj(3)
            u = cc * cx
            rows = lax.broadcasted_iota(jnp.int32, u.shape, 0)
            hl = halo[...]
            u1 = jnp.where(rows == 0, hl[7:8], pltpu.roll(u, 1, 0))
            u2 = jnp.where(rows == 0, hl[6:7], jnp.where(rows == 1, hl[7:8], pltpu.roll(u, 2, 0)))
            halo[...] = u[n - 8 : n]
            cw = cw_ref[...]
            conv = cw[0:1] * u2 + cw[1:2] * u1 + cw[2:3] * u
            co = (cb * conv * (cg * _sigmoid(cg))).astype(BF16)
            co_ref[rs, :] = co
            cv_ref[rs, :] = conv.astype(BF16)
            pr_ref[rs, 0 * BLK : 1 * BLK] = cx.astype(BF16)
            pr_ref[rs, 1 * BLK : 2 * BLK] = cb.astype(BF16)
            pr_ref[rs, 2 * BLK : 3 * BLK] = cc.astype(BF16)
            pr_ref[rs, 3 * BLK : 4 * BLK] = cg.astype(BF16)
            cos_all, sin_all = _tile_rotary(ca_ref, sa_ref, cb_ref, sb_ref, i)
            cos_t, sin_t = cos_all[rs, :], sin_all[rs, :]
            q, k = proj(4), proj(5)
            for hd in range(N_HEADS):
                c0 = hd * HEAD_DIM
                pr_ref[rs, 4 * BLK + c0 : 4 * BLK + c0 + HEAD_DIM] = (
                    _rot(q[:, c0 : c0 + HEAD_DIM], cos_t, sin_t) * Q_SCALE
                ).astype(BF16)
                pr_ref[rs, 5 * BLK + c0 : 5 * BLK + c0 + HEAD_DIM] = _rot(
                    k[:, c0 : c0 + HEAD_DIM], cos_t, sin_t
                ).astype(BF16)
            pr_ref[rs, 6 * BLK : 7 * BLK] = proj(6).astype(BF16)
            pr_ref[rs, 7 * BLK : 8 * BLK] = proj(7).astype(BF16)

        @pl.when(i == 0)
        def _():
            halo[...] = jnp.zeros_like(halo)
            work(mc_ref[...], LIVE0)

        @pl.when(i > 0)
        def _():
            work(x_ref[...], 0)

    row = lambda w: pl.BlockSpec((tm, w), lambda i: (i, 0))
    col = lambda w: pl.BlockSpec((w, tm), lambda i: (0, i))
    return pl.pallas_call(
        body,
        name="f1_inproj_conv",
        grid=(nt,),
        in_specs=[pl.BlockSpec((tm, D_MODEL), lambda i: (_token_tile(i), 0))] + [_VMEM] * 8 + [_HBM],
        out_specs=(col(D_MODEL), row(N_PROJ * BLK), row(D_CONV), row(D_CONV), _HBM),
        out_shape=(
            jax.ShapeDtypeStruct((D_MODEL, n_rows), BF16),
            jax.ShapeDtypeStruct((n_rows, N_PROJ * BLK), BF16),
            jax.ShapeDtypeStruct((n_rows, D_CONV), BF16),
            jax.ShapeDtypeStruct((n_rows, D_CONV), BF16),
            jax.ShapeDtypeStruct((N_DEV, *wo_shard.shape), wo_shard.dtype),
        ),
        scratch_shapes=[
            pltpu.VMEM((8, D_CONV), F32),
            pltpu.SemaphoreType.DMA((GATHER_PAIRS,)),
            pltpu.SemaphoreType.DMA((GATHER_PAIRS,)),
            pltpu.SemaphoreType.DMA(()),
        ],
        compiler_params=_params(),
    )(x2, meta_chunk, g1, w3, cw8, *rot, wo_shard)


def _group_norm(o):
    ys, rs = [], []
    for hd in range(N_HEADS):
        oh = o[:, hd * HEAD_DIM : (hd + 1) * HEAD_DIM]
        xc = oh - jnp.mean(oh, axis=-1, keepdims=True)
        rstd = lax.rsqrt(jnp.mean(xc * xc, axis=-1, keepdims=True) + EPS)
        ys.append(xc * rstd)
        rs.append(jnp.broadcast_to(rstd, oh.shape))
    return jnp.concatenate(ys, axis=1), jnp.concatenate(rs, axis=1)


def _f2(proj, conv_out, x2, t2, wout, gr, gf, dec, zeta, xi):
    n_rows = proj.shape[0]
    tm = ROW_TILE
    nt = n_rows // tm
    nct = tm // CHUNK

    def body(q_ref, k_ref, v_ref, rg_ref, co_ref, x_ref, t_ref, wo_ref, gr_ref, gf_ref, dec_ref, zeta_ref,
             xi_ref, dh2_ref, o_ref, st_ref, gw_ref, acc_ref, state, gacc):
        i = pl.program_id(0)

        def mixer(r0):
            chunks = range(r0 // CHUNK, nct)
            heads = []
            for hd in range(N_HEADS):
                cs = slice(hd * HEAD_DIM, (hd + 1) * HEAD_DIM)
                upd = {}
                for c in chunks:
                    rs = slice(c * CHUNK, (c + 1) * CHUNK)
                    kz = (k_ref[rs, cs].astype(F32) * zeta_ref[hd]).astype(BF16)
                    upd[c] = lax.dot_general(kz, v_ref[rs, cs], TN, preferred_element_type=F32)
                st = state[hd]
                outs = []
                for c in chunks:
                    rs = slice(c * CHUNK, (c + 1) * CHUNK)
                    q, k, v = q_ref[rs, cs], k_ref[rs, cs], v_ref[rs, cs]
                    st_bf = st.astype(BF16)
                    st_ref[c, hd] = st_bf
                    s = lax.dot_general(q, k, NT, preferred_element_type=F32) * dec_ref[hd]
                    inner = jnp.dot(s.astype(BF16), v, preferred_element_type=F32)
                    qx = (q.astype(F32) * xi_ref[hd]).astype(BF16)
                    outs.append(inner + jnp.dot(qx, st_bf, preferred_element_type=F32))
                    st = CHUNK_DECAY[hd] * st + upd[c]
                state[hd] = st
                heads.append(jnp.concatenate(outs, axis=0) if len(outs) > 1 else outs[0])
            o = jnp.concatenate(heads, axis=1)
            o_ref[r0:tm, :] = o.astype(BF16)
            yh, _ = _group_norm(o)
            rg = rg_ref[r0:tm, :].astype(F32)
            return (yh * gr_ref[...] * (rg * _sigmoid(rg))).astype(BF16)

        @pl.when(i == 0)
        def _():
            state[...] = jnp.zeros_like(state)
            acc_ref[...] = jnp.zeros_like(acc_ref)
            gacc[...] = jnp.zeros_like(gacc)
            mixer(LIVE0)
            dh2_ref[...] = jnp.zeros_like(dh2_ref)

        @pl.when(i > 0)
        def _():
            ro = mixer(0)
            h2 = (
                x_ref[...]
                + jnp.dot(co_ref[...], wo_ref[0:D_CONV], preferred_element_type=F32)
                + jnp.dot(ro, wo_ref[D_CONV:], preferred_element_type=F32)
            )
            r2 = lax.rsqrt(jnp.mean(h2 * h2, axis=-1, keepdims=True) + EPS)
            yn = h2 * r2
            gfv = gf_ref[...]
            err = yn * gfv - t_ref[...]
            tile_loss = jnp.sum(jnp.sum(err * err, axis=-1, keepdims=True), axis=0, keepdims=True) * (0.5 / D_MODEL)
            acc_ref[0:1, :] += jnp.sum(err * yn, axis=0, keepdims=True) * (1.0 / D_MODEL)
            acc_ref[1:2, :] += tile_loss
            dyn = err * (gfv * (1.0 / D_MODEL))
            dh2 = r2 * (dyn - yn * jnp.mean(dyn * yn, axis=-1, keepdims=True))
            dh2_ref[...] = dh2
            dh2_bf = dh2.astype(BF16)
            gacc[0:D_CONV, :] += lax.dot_general(co_ref[...], dh2_bf, TN, preferred_element_type=F32)
            gacc[D_CONV:, :] += lax.dot_general(ro, dh2_bf, TN, preferred_element_type=F32)

        @pl.when(i == nt - 1)
        def _():
            gw_ref[...] = gacc[...].astype(BF16)

    row = lambda w, j=0: pl.BlockSpec((tm, w), lambda i: (i, j))
    tok = pl.BlockSpec((tm, D_MODEL), lambda i: (_token_tile(i), 0))
    return pl.pallas_call(
        body,
        name="f2_retention_out",
        grid=(nt,),
        in_specs=[row(BLK, 4), row(BLK, 5), row(BLK, 6), row(BLK, 7), row(D_CONV), tok, tok] + [_VMEM] * 6,
        out_specs=(
            row(D_MODEL),
            row(D_RET),
            pl.BlockSpec((nct, N_HEADS, HEAD_DIM, HEAD_DIM), lambda i: (i, 0, 0, 0)),
            pl.BlockSpec((D_MODEL, D_MODEL), lambda i: (0, 0)),
            pl.BlockSpec((8, D_MODEL), lambda i: (0, 0)),
        ),
        out_shape=(
            jax.ShapeDtypeStruct((n_rows, D_MODEL), F32),
            jax.ShapeDtypeStruct((n_rows, D_RET), BF16),
            jax.ShapeDtypeStruct((n_rows // CHUNK, N_HEADS, HEAD_DIM, HEAD_DIM), BF16),
            jax.ShapeDtypeStruct((D_MODEL, D_MODEL), BF16),
            jax.ShapeDtypeStruct((8, D_MODEL), F32),
        ),
        scratch_shapes=[
            pltpu.VMEM((N_HEADS, HEAD_DIM, HEAD_DIM), F32),
            pltpu.VMEM((D_MODEL, D_MODEL), F32),
        ],
        compiler_params=_params(),
    )(proj, proj, proj, proj, conv_out, x2, t2, wout, gr, gf, dec, zeta, xi)


def _b1(dh2, proj, o, states, wout, gr, dec, dect, zeta, xi, rot):
    n_rows = dh2.shape[0]
    tm = ROW_TILE
    nt = n_rows // tm
    nct = tm // CHUNK

    def body(dh2_ref, pr_ref, o_ref, st_ref, wo_ref, gr_ref, dec_ref, dect_ref, zeta_ref, xi_ref,
             ca_ref, sa_ref, cb_ref, sb_ref, dp_ref, dco_ref, acc_ref, dstate):
        i = pl.program_id(0)
        tile = nt - 1 - i
        cos_all, sin_all = _tile_rotary(ca_ref, sa_ref, cb_ref, sb_ref, tile)

        @pl.when(i == 0)
        def _():
            dstate[...] = jnp.zeros_like(dstate)
            acc_ref[...] = jnp.zeros_like(acc_ref)

        @pl.when(tile > 0)
        def _():
            dmix = lax.dot_general(dh2_ref[...].astype(BF16), wo_ref[...], NT, preferred_element_type=F32)
            dco_ref[...] = dmix[:, :D_CONV].astype(BF16)
            dro = dmix[:, D_CONV:]
            rg = pr_ref[:, 3 * BLK : 4 * BLK].astype(F32)
            sg = _sigmoid(rg)
            silu, dsilu = rg * sg, sg * (1.0 + rg * (1.0 - sg))
            yh, rstd = _group_norm(o_ref[...].astype(F32))
            grv = gr_ref[...]
            dp_ref[:, 3 * BLK : 4 * BLK] = (dro * (yh * grv) * dsilu).astype(BF16)
            dret = dro * silu
            acc_ref[0:1, :] += jnp.sum(dret * yh, axis=0, keepdims=True)
            dyh = dret * grv
            for hd in range(N_HEADS):
                cs = slice(hd * HEAD_DIM, (hd + 1) * HEAD_DIM)
                a, b = dyh[:, cs], yh[:, cs]
                do_head = (
                    rstd[:, cs]
                    * (a - jnp.mean(a, axis=-1, keepdims=True) - b * jnp.mean(a * b, axis=-1, keepdims=True))
                ).astype(BF16)
                dupd = {}
                for c in range(nct):
                    rs = slice(c * CHUNK, (c + 1) * CHUNK)
                    qx = (pr_ref[rs, cs].astype(F32) * xi_ref[hd]).astype(BF16)
                    dupd[c] = lax.dot_general(qx, do_head[rs, :], TN, preferred_element_type=F32)
                dst = dstate[hd]
                for c in reversed(range(nct)):
                    rs = slice(c * CHUNK, (c + 1) * CHUNK)
                    cos_t, sin_t = cos_all[rs, :], sin_all[rs, :]
                    q = pr_ref[rs, hd * HEAD_DIM : (hd + 1) * HEAD_DIM]
                    k = pr_ref[rs, BLK + hd * HEAD_DIM : BLK + (hd + 1) * HEAD_DIM]
                    v = pr_ref[rs, 2 * BLK + hd * HEAD_DIM : 2 * BLK + (hd + 1) * HEAD_DIM]
                    do = do_head[rs, :]
                    st_bf = st_ref[c, hd]
                    dst_bf = dst.astype(BF16)
                    zt, xt = zeta_ref[hd], xi_ref[hd]
                    sT = (lax.dot_general(k, q, NT, preferred_element_type=F32) * dect_ref[hd]).astype(BF16)
                    dsT = (lax.dot_general(v, do, NT, preferred_element_type=F32) * dect_ref[hd]).astype(BF16)
                    ds = (lax.dot_general(do, v, NT, preferred_element_type=F32) * dec_ref[hd]).astype(BF16)
                    kz = (k.astype(F32) * zt).astype(BF16)
                    dv = jnp.dot(sT, do, preferred_element_type=F32) + jnp.dot(kz, dst_bf, preferred_element_type=F32)
                    dq = jnp.dot(ds, k, preferred_element_type=F32) + xt * lax.dot_general(
                        do, st_bf, NT, preferred_element_type=F32
                    )
                    dk = jnp.dot(dsT, q, preferred_element_type=F32) + zt * lax.dot_general(
                        v, dst_bf, NT, preferred_element_type=F32
                    )
                    dst = CHUNK_DECAY[hd] * dst + dupd[c]
                    dp_ref[rs, hd * HEAD_DIM : (hd + 1) * HEAD_DIM] = _rot_bwd(dq * Q_SCALE, cos_t, sin_t).astype(BF16)
                    dp_ref[rs, BLK + hd * HEAD_DIM : BLK + (hd + 1) * HEAD_DIM] = _rot_bwd(dk, cos_t, sin_t).astype(BF16)
                    dp_ref[rs, 2 * BLK + hd * HEAD_DIM : 2 * BLK + (hd + 1) * HEAD_DIM] = dv.astype(BF16)
                dstate[hd] = dst

        @pl.when(tile == 0)
        def _():
            rs = slice(LIVE0, tm)
            cos_t, sin_t = cos_all[rs, :], sin_all[rs, :]
            zeros = jnp.zeros((tm - LIVE0, HEAD_DIM), BF16)
            for hd in range(N_HEADS):
                k = pr_ref[rs, BLK + hd * HEAD_DIM : BLK + (hd + 1) * HEAD_DIM]
                v = pr_ref[rs, 2 * BLK + hd * HEAD_DIM : 2 * BLK + (hd + 1) * HEAD_DIM]
                dst_bf = dstate[hd].astype(BF16)
                zt = zeta_ref[hd]
                kz = (k.astype(F32) * zt).astype(BF16)
                dv = jnp.dot(kz, dst_bf, preferred_element_type=F32)
                dk = zt * lax.dot_general(v, dst_bf, NT, preferred_element_type=F32)
                dp_ref[rs, hd * HEAD_DIM : (hd + 1) * HEAD_DIM] = zeros
                dp_ref[rs, BLK + hd * HEAD_DIM : BLK + (hd + 1) * HEAD_DIM] = _rot_bwd(dk, cos_t, sin_t).astype(BF16)
                dp_ref[rs, 2 * BLK + hd * HEAD_DIM : 2 * BLK + (hd + 1) * HEAD_DIM] = dv.astype(BF16)
                dp_ref[rs, 3 * BLK + hd * HEAD_DIM : 3 * BLK + (hd + 1) * HEAD_DIM] = zeros
            dco_ref[rs, :] = jnp.zeros((tm - LIVE0, D_CONV), BF16)

    half = N_PROJ * BLK // 2
    rev = lambda w, j=0: pl.BlockSpec((tm, w), lambda i: (nt - 1 - i, j))
    return pl.pallas_call(
        body,
        name="b1_dret",
        grid=(nt,),
        in_specs=[
            rev(D_MODEL),
            rev(half, 1),
            rev(D_RET),
            pl.BlockSpec((nct, N_HEADS, HEAD_DIM, HEAD_DIM), lambda i: (nt - 1 - i, 0, 0, 0)),
        ]
        + [_VMEM] * 10,
        out_specs=(rev(half, 1), rev(D_CONV), pl.BlockSpec((8, D_RET), lambda i: (0, 0))),
        out_shape=(
            jax.ShapeDtypeStruct((n_rows, N_PROJ * BLK), BF16),
            jax.ShapeDtypeStruct((n_rows, D_CONV), BF16),
            jax.ShapeDtypeStruct((8, D_RET), F32),
        ),
        scratch_shapes=[pltpu.VMEM((N_HEADS, HEAD_DIM, HEAD_DIM), F32)],
        compiler_params=_params(),
    )(dh2, proj, o, states, wout, gr, dec, dect, zeta, xi, *rot)


def _b2(dproj, dco, proj, conv, w3, cw8, x2, meta_chunk, dh2, g1):
    n_rows = dproj.shape[0]
    tm = ROW_TILE
    nt = n_rows // tm
    half = N_PROJ * BLK // 2

    def body(dpr_ref, dco_ref, pr_ref, cv_ref, w_ref, cw_ref, x_ref, mc_ref, dh2_ref, g_ref,
             dpc_ref, gx_ref, dm_ref, acc_ref, accc_ref, halo):
        i = pl.program_id(0)
        tile = nt - 1 - i

        @pl.when(i == 0)
        def _():
            acc_ref[...] = jnp.zeros_like(acc_ref)
            accc_ref[...] = jnp.zeros_like(accc_ref)
            halo[...] = jnp.zeros_like(halo)

        def work(h, r0):
            rs = slice(r0, tm)
            n = tm - r0
            dco = dco_ref[rs, :].astype(F32)
            cx = pr_ref[rs, 0 * BLK : 1 * BLK].astype(F32)
            cb = pr_ref[rs, 1 * BLK : 2 * BLK].astype(F32)
            cc = pr_ref[rs, 2 * BLK : 3 * BLK].astype(F32)
            cg = pr_ref[rs, 3 * BLK : 4 * BLK].astype(F32)
            conv = cv_ref[rs, :].astype(F32)
            sg = _sigmoid(cg)
            silu, dsilu = cg * sg, sg * (1.0 + cg * (1.0 - sg))
            t = dco * cb
            dcb = (dco * conv * silu).astype(BF16)
            dcg = (t * conv * dsilu).astype(BF16)
            dconv = t * silu
            rows = lax.broadcasted_iota(jnp.int32, dconv.shape, 0)
            hl = halo[...]
            dc1 = jnp.where(rows == n - 1, hl[0:1], pltpu.roll(dconv, n - 1, 0))
            dc2 = jnp.where(rows == n - 2, hl[0:1], jnp.where(rows == n - 1, hl[1:2], pltpu.roll(dconv, n - 2, 0)))
            halo[...] = dconv[0:8]
            cw = cw_ref[...]
            du = cw[2:3] * dconv + cw[1:2] * dc1 + cw[0:1] * dc2
            u = cc * cx
            accc_ref[0:1, :] += jnp.sum(u * dc2, axis=0, keepdims=True)
            accc_ref[1:2, :] += jnp.sum(u * dc1, axis=0, keepdims=True)
            accc_ref[2:3, :] += jnp.sum(u * dconv, axis=0, keepdims=True)
            dcx = (du * cc).astype(BF16)
            dcc = (du * cx).astype(BF16)
            dhn = lax.dot_general(dpr_ref[rs, 0:BLK], w_ref[4], NT, preferred_element_type=F32)
            for j in range(1, 4):
                dhn += lax.dot_general(dpr_ref[rs, j * BLK : (j + 1) * BLK], w_ref[4 + j], NT, preferred_element_type=F32)
            for blk, d in ((1, dcb), (3, dcg), (0, dcx), (2, dcc)):
                dpc_ref[rs, blk * BLK : (blk + 1) * BLK] = d
                dhn += lax.dot_general(d, w_ref[blk], NT, preferred_element_type=F32)
            r = lax.rsqrt(jnp.mean(h * h, axis=-1, keepdims=True) + EPS)
            hh = h * r
            acc_ref[0:1, :] += jnp.sum(dhn * hh, axis=0, keepdims=True)
            dg = dhn * g_ref[...]
            return dh2_ref[rs, :] + r * (dg - hh * jnp.mean(dg * hh, axis=-1, keepdims=True))

        @pl.when(tile == 0)
        def _():
            dh = work(mc_ref[...], LIVE0)
            dm_ref[...] = dh[CHUNK - N_META : CHUNK]

        @pl.when(tile > 0)
        def _():
            gx_ref[...] = work(x_ref[...], 0)

    rev = lambda w, j=0: pl.BlockSpec((tm, w), lambda i: (nt - 1 - i, j))
    tok = pl.BlockSpec((tm, D_MODEL), lambda i: (_token_tile(nt - 1 - i), 0))
    const = lambda r, c: pl.BlockSpec((r, c), lambda i: (0, 0))
    return pl.pallas_call(
        body,
        name="b2_dconv_dh",
        grid=(nt,),
        in_specs=[rev(half, 1), rev(D_CONV), rev(half, 0), rev(D_CONV), _VMEM, _VMEM, tok, _VMEM, rev(D_MODEL), _VMEM],
        out_specs=(rev(half, 0), tok, const(N_META, D_MODEL), const(8, D_MODEL), const(8, D_CONV)),
        out_shape=(
            jax.ShapeDtypeStruct(dproj.shape, BF16),
            jax.ShapeDtypeStruct(x2.shape, F32),
            jax.ShapeDtypeStruct((N_META, D_MODEL), F32),
            jax.ShapeDtypeStruct((8, D_MODEL), F32),
            jax.ShapeDtypeStruct((8, D_CONV), F32),
        ),
        scratch_shapes=[pltpu.VMEM((8, D_CONV), F32)],
        input_output_aliases={0: 0},
        compiler_params=_params(),
    )(dproj, dco, proj, conv, w3, cw8, x2, meta_chunk, dh2, g1)


def _gw_in_scatter(hnT, dproj, gw_out_parts, sc_parts, ga_part, me_arr):
    n_rows = dproj.shape[0]
    last = N_DEV - 1
    by_dest = (True, True, False)

    def body(me_ref, a_ref, b_ref, go_ref, sc_ref, ga_ref, land_in, land_go, land_sc, land_ga,
             d2d_buf, d2d_land, ici_buf, d2d_send, d2d_recv, ici_send, ici_recv, send_sems, recv_sems, local_sems):
        del me_ref
        s = pl.program_id(0)
        t = last - s
        q = t >> 1
        x, y, c = lax.axis_index("x"), lax.axis_index("y"), lax.axis_index("c")
        me = 4 * x + 2 * y + c
        chip = 2 * x + y
        srcs, lands = (go_ref, sc_ref, ga_ref), (land_go, land_sc, land_ga)

        def peer_at(k):
            return (1 - x if k & 4 else x, 1 - y if k & 2 else y, 1 - c if k & 1 else c)

        def small_copy(a, k):
            px, py, pc = peer_at(k)
            return pltpu.make_async_remote_copy(
                src_ref=srcs[a].at[4 * px + 2 * py + pc] if by_dest[a] else srcs[a],
                dst_ref=lands[a].at[me],
                send_sem=send_sems.at[a * last + k - 1],
                recv_sem=recv_sems.at[a * last + k - 1],
                device_id=(px, py, pc),
                device_id_type=MESH,
            )

        def small_local(a):
            return pltpu.make_async_copy(srcs[a].at[me] if by_dest[a] else srcs[a], lands[a].at[me], local_sems.at[a + 1])

        def d2d_copy(j):
            return pltpu.make_async_remote_copy(
                src_ref=d2d_buf.at[j],
                dst_ref=d2d_land.at[j],
                send_sem=d2d_send.at[j],
                recv_sem=d2d_recv.at[j],
                device_id=(x, y, 1 - c),
                device_id_type=MESH,
            )

        def ici_copy(j, to):
            return pltpu.make_async_remote_copy(
                src_ref=ici_buf.at[j],
                dst_ref=land_in.at[chip],
                send_sem=ici_send.at[j],
                recv_sem=ici_recv.at[j],
                device_id=to,
                device_id_type=MESH,
            )

        def own_copy():
            return pltpu.make_async_copy(ici_buf.at[0], land_in.at[chip], local_sems.at[0])

        @pl.when(s == 0)
        def _():
            for a in range(3):
                small_local(a).start()
            for k in range(last, 0, -1):
                for a in range(3):
                    small_copy(a, k).start()

        blk = jnp.dot(a_ref[:, LIVE0:], b_ref[LIVE0:, :], preferred_element_type=F32)

        @pl.when((t & 1) == 1)
        def _():
            d2d_buf[q] = blk.astype(BF16)
            d2d_copy(q).start()

        @pl.when((t & 1) == 0)
        def _():
            d2d_copy(q).wait_recv()
            ici_buf[q] = (blk + d2d_land[q].astype(F32)).astype(BF16)

            @pl.when(t != 0)
            def _():
                ici_copy(q, (jnp.bitwise_xor(x, (t >> 2) & 1), jnp.bitwise_xor(y, (t >> 1) & 1), c)).start()

            @pl.when(t == 0)
            def _():
                own_copy().start()
                for j in range(N_CHIPS):
                    d2d_copy(j).wait_send()
                for j in range(N_CHIPS - 1, 0, -1):
                    ici_copy(j, peer_at(2 * j)).wait()
                for k in range(last, 0, -1):
                    for a in range(3):
                        small_copy(a, k).wait()
                for a in range(3):
                    small_local(a).wait()
                own_copy().wait()

    smalls = (gw_out_parts, sc_parts, ga_part)
    small_blocks = [p.shape[1:] if d else p.shape for p, d in zip(smalls, by_dest, strict=True)]
    grid_spec = pltpu.PrefetchScalarGridSpec(
        num_scalar_prefetch=1,
        grid=(N_DEV,),
        in_specs=[_VMEM, pl.BlockSpec((n_rows, BLK), lambda s, me: (0, jnp.bitwise_xor(me[0], last - s)))] + [_HBM] * 3,
        out_specs=tuple([_HBM] * 4),
        scratch_shapes=[
            pltpu.VMEM((N_CHIPS, D_MODEL, BLK), BF16),
            pltpu.VMEM((N_CHIPS, D_MODEL, BLK), BF16),
            pltpu.VMEM((N_CHIPS, D_MODEL, BLK), BF16),
            pltpu.SemaphoreType.DMA((N_CHIPS,)),
            pltpu.SemaphoreType.DMA((N_CHIPS,)),
            pltpu.SemaphoreType.DMA((N_CHIPS,)),
            pltpu.SemaphoreType.DMA((N_CHIPS,)),
            pltpu.SemaphoreType.DMA((3 * last,)),
            pltpu.SemaphoreType.DMA((3 * last,)),
            pltpu.SemaphoreType.DMA((4,)),
        ],
    )
    return pl.pallas_call(
        body,
        name="gw_in_scatter",
        grid_spec=grid_spec,
        out_shape=(jax.ShapeDtypeStruct((N_CHIPS, D_MODEL, BLK), BF16),)
        + tuple(jax.ShapeDtypeStruct((N_DEV, *b), p.dtype) for b, p in zip(small_blocks, smalls, strict=True)),
        compiler_params=_params(),
    )(me_arr, hnT, dproj, gw_out_parts, sc_parts, ga_part)


def _adamw(w, g, m, v):
    m = ADAM_B1 * m + (1.0 - ADAM_B1) * g
    v = ADAM_B2 * v + (1.0 - ADAM_B2) * (g * g)
    m_hat = m / (1.0 - ADAM_B1**ADAM_STEP)
    v_hat = v / (1.0 - ADAM_B2**ADAM_STEP)
    delta = -ADAM_LR * (m_hat / (jnp.sqrt(v_hat) + ADAM_EPS) + ADAM_WD * w)
    return delta, m, v


def _sum_adamw(name, parts, w, m, v, rows_per_step):
    n_r, n_c = w.shape
    n_parts = parts.shape[0]
    tr = rows_per_step

    def body(p_ref, w_ref, m_ref, v_ref, g_ref, d_ref, nm_ref, nv_ref):
        g = p_ref[0].astype(F32)
        for s in range(1, n_parts):
            g = g + p_ref[s].astype(F32)
        g_ref[...] = g
        d_ref[...], nm_ref[...], nv_ref[...] = _adamw(w_ref[...], g, m_ref[...], v_ref[...])

    blk = pl.BlockSpec((tr, n_c), lambda i: (i, 0))
    return pl.pallas_call(
        body,
        name=name,
        grid=(n_r // tr,),
        in_specs=[pl.BlockSpec((n_parts, tr, n_c), lambda i: (0, i, 0)), blk, blk, blk],
        out_specs=(blk,) * 4,
        out_shape=(jax.ShapeDtypeStruct((n_r, n_c), F32),) * 4,
        compiler_params=_params(),
    )(parts, w, m, v)


def _small_leaves(meta, conv_w, n1, rg, fg):
    cols = D_CONV // N_DEV
    return (
        meta,
        jnp.pad(conv_w, ((0, 5), (0, CHUNK - cols))),
        n1.reshape(8, CHUNK),
        rg.reshape(4, CHUNK),
        fg.reshape(8, CHUNK),
    )


def _from_small_leaves(meta, conv8, n1, rg, fg):
    return meta, conv8[0:3, : D_CONV // N_DEV], n1.reshape(D_MODEL), rg.reshape(D_RET), fg.reshape(D_MODEL)


def _adamw_small(land_sc, land_ga, w, m, v):
    n_leaf = 5

    def body(sc_ref, ga_ref, *refs):
        ins, outs = refs[: 3 * n_leaf], refs[3 * n_leaf :]
        sc, ga = sc_ref[0], ga_ref[0]
        for s in range(1, N_DEV):
            sc = sc + sc_ref[s]
            ga = ga + ga_ref[s]
        grads = (
            sc[SC_META0 : SC_META0 + N_META],
            sc[SC_CONV0 : SC_CONV0 + 8],
            ga[GA_N1 : GA_N1 + 8],
            ga[GA_RG : GA_RG + 4],
            ga[GA_FG : GA_FG + 8],
        )
        for leaf, g in enumerate(grads):
            d, nm, nv = _adamw(ins[leaf][...], g, ins[n_leaf + leaf][...], ins[2 * n_leaf + leaf][...])
            outs[leaf][...] = g
            outs[n_leaf + leaf][...] = d
            outs[2 * n_leaf + leaf][...] = nm
            outs[3 * n_leaf + leaf][...] = nv
        outs[4 * n_leaf][...] = ga[GA_LOSS : GA_LOSS + 1]

    leaf_shapes = tuple(jax.ShapeDtypeStruct(a.shape, F32) for a in w)
    out = pl.pallas_call(
        body,
        name="adamw_small",
        out_shape=leaf_shapes * 4 + (jax.ShapeDtypeStruct((1, CHUNK), F32),),
    )(land_sc, land_ga, *w, *m, *v)
    return tuple(out[k * n_leaf : (k + 1) * n_leaf] for k in range(4)), out[4 * n_leaf]


def kernel(x, meta, norm1_g, w_in, conv_w, ret_norm_g, w_out, final_g, loss_target, m_meta, m_norm1_g, m_w_in, m_conv_w, m_ret_norm_g, m_w_out, m_final_g, v_meta, v_norm1_g, v_w_in, v_conv_w, v_ret_norm_g, v_w_out, v_final_g):
    seq = x.shape[1]
    assert x.shape == (1, seq, D_MODEL) and seq % ROW_TILE == 0
    n_tiles = seq // ROW_TILE + 1
    cols = D_CONV // N_DEV
    x2, t2 = x[0], loss_target[0]

    small_w = _small_leaves(meta, conv_w, norm1_g, ret_norm_g, final_g)
    w3, small = _gather("gather_weights", [w_in.astype(BF16), jnp.concatenate(small_w[0:2], axis=0)])
    meta_full = small[:, SC_META0 : SC_META0 + N_META, :].transpose(1, 0, 2).reshape(N_META, D_MODEL)
    conv_full = small[:, SC_CONV0 : SC_CONV0 + 3, :cols].transpose(1, 0, 2).reshape(3, D_CONV)
    cw8 = jnp.pad(conv_full, ((0, 5), (0, 0)))
    meta_chunk = jnp.pad(meta_full, ((CHUNK - N_META, 0), (0, 0)))

    rot = _rotary_tables(n_tiles, ROW_TILE)
    dec, dect, zeta, xi = _decay_tables()
    g1 = norm1_g.reshape(1, D_MODEL)
    gr = ret_norm_g.reshape(1, D_RET)
    gf = final_g.reshape(1, D_MODEL)

    hnT, proj, conv_out, conv, wo3 = _f1(x2, meta_chunk, g1, w3, cw8, rot, w_out.astype(BF16))
    wout = wo3.reshape(D_MODEL, D_MODEL)
    dh2, o, states, gw_out, acc_f2 = _f2(proj, conv_out, x2, t2, wout, gr, gf, dec, zeta, xi)
    dproj_ret, dco, acc_b1 = _b1(dh2, proj, o, states, wout, gr, dec, dect, zeta, xi, rot)
    dproj, grad_x2, dmeta_rows, acc_b2, acc_conv = _b2(dproj_ret, dco, proj, conv, w3, cw8, x2, meta_chunk, dh2, g1)
    gw_out_parts = gw_out.reshape(N_DEV, D_MODEL // N_DEV, D_MODEL)

    dmeta = dmeta_rows.reshape(N_META, N_DEV, CHUNK).transpose(1, 0, 2)
    dconv = jnp.pad(acc_conv[0:3, :].reshape(3, N_DEV, cols).transpose(1, 0, 2), ((0, 0), (0, 5), (0, CHUNK - cols)))
    sc_parts = jnp.concatenate([dmeta, dconv], axis=1)
    ga_part = jnp.concatenate(
        [
            acc_b2[0].reshape(8, CHUNK),
            jnp.pad(acc_b1[0].reshape(4, CHUNK), ((0, 4), (0, 0))),
            acc_f2[0].reshape(8, CHUNK),
            jnp.pad(acc_f2[1:2, 0:CHUNK], ((0, 7), (0, 0))),
        ],
        axis=0,
    )

    me_arr = (4 * lax.axis_index("x") + 2 * lax.axis_index("y") + lax.axis_index("c")).astype(jnp.int32).reshape(1)
    land_in, land_out, land_sc, land_ga = _gw_in_scatter(hnT, dproj, gw_out_parts, sc_parts, ga_part, me_arr)

    g_w_in, d_w_in, nm_w_in, nv_w_in = _sum_adamw("adamw_w_in", land_in, w_in, m_w_in, v_w_in, 256)
    g_w_out, d_w_out, nm_w_out, nv_w_out = _sum_adamw("adamw_w_out", land_out, w_out, m_w_out, v_w_out, 64)
    small_out, loss_row = _adamw_small(
        land_sc,
        land_ga,
        small_w,
        _small_leaves(m_meta, m_conv_w, m_norm1_g, m_ret_norm_g, m_final_g),
        _small_leaves(v_meta, v_conv_w, v_norm1_g, v_ret_norm_g, v_final_g),
    )
    loss = loss_row[0, 0]
    grad_x = grad_x2.reshape(1, seq, D_MODEL)

    def leaves(w_in_leaf, w_out_leaf, small_leaves):
        meta_leaf, conv_leaf, n1_leaf, rg_leaf, fg_leaf = _from_small_leaves(*small_leaves)
        return (meta_leaf, n1_leaf, w_in_leaf, conv_leaf, rg_leaf, w_out_leaf, fg_leaf)

    return (
        loss,
        grad_x,
        *leaves(g_w_in, g_w_out, small_out[0]),
        *leaves(d_w_in, d_w_out, small_out[1]),
        *leaves(nm_w_in, nm_w_out, small_out[2]),
        *leaves(nv_w_in, nv_w_out, small_out[3]),
    )
```

```python
import functools
import math

import jax
import jax.numpy as jnp
import numpy as np
from jax import lax
from jax.experimental import pallas as pl
from jax.experimental.pallas import tpu as pltpu

F32 = jnp.float32
BF16 = jnp.bfloat16

N_DEV = 8
N_CHIPS = 4
D_MODEL = 1024
N_META = 16
CHUNK = 128
D_CONV = 512
D_RET = 512
N_HEADS = 4
HEAD_DIM = 128
N_PROJ = 8
BLK = 512
ROPE_BASE = 10000.0
EPS = 1e-6
Q_SCALE = HEAD_DIM ** -0.5
LOG_G = tuple(math.log(1.0 - 2.0 ** (-5.0 - h)) for h in range(N_HEADS))
CHUNK_DECAY = tuple(math.exp(CHUNK * lg) for lg in LOG_G)

ADAM_LR = 0.001
ADAM_B1 = 0.9
ADAM_B2 = 0.999
ADAM_EPS = 1e-08
ADAM_WD = 0.01
ADAM_STEP = 10

ROW_TILE = 512
PAD_ROWS = ROW_TILE - N_META
LIVE0 = ROW_TILE - CHUNK
VMEM_LIMIT = 56 * 1024 * 1024

SC_META0, SC_CONV0 = 0, 16
GA_N1, GA_RG, GA_FG, GA_LOSS = 0, 8, 16, 24

NT = (((1,), (1,)), ((), ()))
TN = (((0,), (0,)), ((), ()))
MESH = pl.DeviceIdType.MESH

_VMEM = pl.BlockSpec(memory_space=pltpu.VMEM)
_HBM = pl.BlockSpec(memory_space=pltpu.HBM)


def _params(n_axes=1):
    return pltpu.CompilerParams(dimension_semantics=("arbitrary",) * n_axes, vmem_limit_bytes=VMEM_LIMIT)


def _sigmoid(x):
    return 0.5 * jnp.tanh(0.5 * x) + 0.5


def _decay_tables():
    idx = np.arange(CHUNK, dtype=np.float64)
    diff = idx[:, None] - idx[None, :]
    dec = np.stack([np.where(diff >= 0, np.exp(diff * lg), 0.0) for lg in LOG_G])
    zeta = np.stack([np.exp((CHUNK - 1 - idx) * lg) for lg in LOG_G])
    xi = np.stack([np.exp((idx + 1.0) * lg) for lg in LOG_G])
    ones = np.ones((1, 1, CHUNK))
    return (
        jnp.asarray(dec, F32),
        jnp.asarray(dec.transpose(0, 2, 1), F32),
        jnp.asarray(zeta[:, :, None] * ones, F32),
        jnp.asarray(xi[:, :, None] * ones, F32),
    )


def _rotary_tables(n_tiles, tm):
    half = HEAD_DIM // 2
    freqs = (1.0 / (np.float32(ROPE_BASE) ** (np.arange(half, dtype=np.float32) / np.float32(half)))).astype(np.float64)
    sign = np.concatenate([-np.ones(half), np.ones(half)])
    two = lambda a: np.concatenate([a, a], axis=1)
    base = two((np.arange(n_tiles, dtype=np.float64) * tm - PAD_ROWS)[:, None] * freqs[None, :])
    off = two(np.arange(tm, dtype=np.float64)[:, None] * freqs[None, :])
    as32 = lambda a: jnp.asarray(a, F32)
    return as32(np.cos(base)), as32(np.sin(base) * sign), as32(np.cos(off)), as32(np.sin(off) * sign)


def _tile_rotary(ca_ref, sa_ref, cb_ref, sb_ref, tile):
    ca, sa = ca_ref[pl.ds(tile, 1), :], sa_ref[pl.ds(tile, 1), :]
    cb, sb = cb_ref[...], sb_ref[...]
    return ca * cb - sa * sb, sa * cb + ca * sb


def _rot(t, cos2, sin2):
    return t * cos2 + pltpu.roll(t, HEAD_DIM // 2, 1) * sin2


def _rot_bwd(d, cos2, sin2):
    return d * cos2 + pltpu.roll(d * sin2, HEAD_DIM // 2, 1)


def _token_tile(i):
    return jnp.maximum(i - 1, 0)


GATHER_PAIRS = 7


def _two_level_gather(src_ref, out_ref, send_sems, recv_sems, local_sem, base=0):
    x, y, c = lax.axis_index("x"), lax.axis_index("y"), lax.axis_index("c")
    me, sibling = (x, y, c), (x, y, 1 - c)
    xnb, ynb, diag = (1 - x, y), (x, 1 - y), (1 - x, 1 - y)
    relayed = (jnp.bitwise_xor(x, 1 - c), jnp.bitwise_xor(y, c))
    other = (jnp.bitwise_xor(x, c), jnp.bitwise_xor(y, 1 - c))

    def copy(k, block, to, src=None):
        dst = out_ref.at[4 * block[0] + 2 * block[1] + block[2]]
        return pltpu.make_async_remote_copy(
            src_ref=dst if src is None else src,
            dst_ref=dst,
            send_sem=send_sems.at[base + k],
            recv_sem=recv_sems.at[base + k],
            device_id=to,
            device_id_type=MESH,
        )

    mine = pltpu.make_async_copy(src_ref, out_ref.at[4 * x + 2 * y + c], local_sem)
    first = [copy(1, me, (*xnb, c), src=src_ref), copy(2, me, (*ynb, c), src=src_ref), copy(0, me, sibling, src=src_ref)]
    relay = copy(3, (*relayed, c), (*other, c))
    passed = [copy(4 + j, (*chip, c), sibling) for j, chip in enumerate((xnb, ynb, diag))]

    def start():
        mine.start()
        for cp in first:
            cp.start()

    def forward():
        copy(1 + c, (*relayed, c), me).wait_recv()
        relay.start()
        copy(2 - c, (*other, c), me).wait_recv()
        passed[0].start()
        passed[1].start()

    def forward_relayed():
        copy(3, (*diag, c), me).wait_recv()
        passed[2].start()

    def finish():
        copy(0, sibling, me).wait_recv()
        for j, chip in enumerate((xnb, ynb, diag)):
            copy(4 + j, (*chip, 1 - c), me).wait_recv()
        for cp in first + [relay] + passed:
            cp.wait_send()
        mine.wait()

    return start, forward, forward_relayed, finish


W_IN_CHUNKS = 4


def _gather_weights(w_shard, small):
    per = GATHER_PAIRS
    n_ch = W_IN_CHUNKS
    rows = w_shard.shape[0] // n_ch

    def body(w_ref, small_ref, w_all_ref, small_all_ref, w_bf, send_sems, recv_sems, local_sems):
        w_bf[...] = w_ref[...].astype(BF16)
        parts = [
            _two_level_gather(
                w_bf.at[pl.ds(ch * rows, rows)],
                w_all_ref.at[:, pl.ds(ch * rows, rows)],
                send_sems,
                recv_sems,
                local_sems.at[ch],
                ch * per,
            )
            for ch in range(n_ch)
        ]
        parts.append(_two_level_gather(small_ref, small_all_ref, send_sems, recv_sems, local_sems.at[n_ch], n_ch * per))
        for phase in range(4):
            for part in parts:
                part[phase]()

    return pl.pallas_call(
        body,
        name="gather_weights",
        out_shape=(
            jax.ShapeDtypeStruct((N_DEV, *w_shard.shape), BF16),
            jax.ShapeDtypeStruct((N_DEV, *small.shape), small.dtype),
        ),
        in_specs=[_VMEM, _HBM],
        out_specs=(_HBM, _HBM),
        scratch_shapes=[
            pltpu.VMEM(w_shard.shape, BF16),
            pltpu.SemaphoreType.DMA(((n_ch + 1) * per,)),
            pltpu.SemaphoreType.DMA(((n_ch + 1) * per,)),
            pltpu.SemaphoreType.DMA((n_ch + 1,)),
        ],
    )(w_shard, small)


def _f1(x2, meta_chunk, g1, w3, cw8, rot, wo_shard):
    tm = ROW_TILE
    nt = x2.shape[0] // tm + 1
    n_rows = nt * tm

    def body(x_ref, mc_ref, g_ref, w_ref, cw_ref, ca_ref, sa_ref, cb_ref, sb_ref, wo_ref,
             hnT_ref, pr_ref, co_ref, cv_ref, wo_all_ref, halo, wo_bf, send_sems, recv_sems, local_sem):
        i = pl.program_id(0)

        @pl.when(i == 0)
        def _():
            wo_bf[...] = wo_ref[...].astype(BF16)

        wo_phases = _two_level_gather(wo_bf, wo_all_ref, send_sems, recv_sems, local_sem)
        for step, phase in zip((0, nt // 3, 2 * nt // 3, nt - 1), wo_phases, strict=True):
            pl.when(i == step)(phase)

        def work(h, r0):
            rs = slice(r0, tm)
            n = tm - r0
            r = lax.rsqrt(jnp.mean(h * h, axis=-1, keepdims=True) + EPS)
            hn = (h * r * g_ref[...]).astype(BF16)
            hnT_ref[:, rs] = hn.T

            def proj(j):
                return jnp.dot(hn, w_ref[j], preferred_element_type=F32)

            cx, cb, cc, cg = proj(0), proj(1), proj(2), proj(3)
            u = cc * cx
            rows = lax.broadcasted_iota(jnp.int32, u.shape, 0)
            hl = halo[...]
            u1 = jnp.where(rows == 0, hl[7:8], pltpu.roll(u, 1, 0))
            u2 = jnp.where(rows == 0, hl[6:7], jnp.where(rows == 1, hl[7:8], pltpu.roll(u, 2, 0)))
            halo[...] = u[n - 8 : n]
            cw = cw_ref[...]
            conv = cw[0:1] * u2 + cw[1:2] * u1 + cw[2:3] * u
            co = (cb * conv * (cg * _sigmoid(cg))).astype(BF16)
            co_ref[rs, :] = co
            cv_ref[rs, :] = conv.astype(BF16)
            pr_ref[rs, 0 * BLK : 1 * BLK] = cx.astype(BF16)
            pr_ref[rs, 1 * BLK : 2 * BLK] = cb.astype(BF16)
            pr_ref[rs, 2 * BLK : 3 * BLK] = cc.astype(BF16)
            pr_ref[rs, 3 * BLK : 4 * BLK] = cg.astype(BF16)
            cos_all, sin_all = _tile_rotary(ca_ref, sa_ref, cb_ref, sb_ref, i)
            cos_t, sin_t = cos_all[rs, :], sin_all[rs, :]
            q, k = proj(4), proj(5)
            for hd in range(N_HEADS):
                c0 = hd * HEAD_DIM
                pr_ref[rs, 4 * BLK + c0 : 4 * BLK + c0 + HEAD_DIM] = (
                    _rot(q[:, c0 : c0 + HEAD_DIM], cos_t, sin_t) * Q_SCALE
                ).astype(BF16)
                pr_ref[rs, 5 * BLK + c0 : 5 * BLK + c0 + HEAD_DIM] = _rot(
                    k[:, c0 : c0 + HEAD_DIM], cos_t, sin_t
                ).astype(BF16)
            pr_ref[rs, 6 * BLK : 7 * BLK] = proj(6).astype(BF16)
            pr_ref[rs, 7 * BLK : 8 * BLK] = proj(7).astype(BF16)

        @pl.when(i == 0)
        def _():
            halo[...] = jnp.zeros_like(halo)
            work(mc_ref[...], LIVE0)

        @pl.when(i > 0)
        def _():
            work(x_ref[...], 0)

    row = lambda w: pl.BlockSpec((tm, w), lambda i: (i, 0))
    col = lambda w: pl.BlockSpec((w, tm), lambda i: (0, i))
    return pl.pallas_call(
        body,
        name="f1_inproj_conv",
        grid=(nt,),
        in_specs=[pl.BlockSpec((tm, D_MODEL), lambda i: (_token_tile(i), 0))] + [_VMEM] * 9,
        out_specs=(col(D_MODEL), row(N_PROJ * BLK), row(D_CONV), row(D_CONV), _HBM),
        out_shape=(
            jax.ShapeDtypeStruct((D_MODEL, n_rows), BF16),
            jax.ShapeDtypeStruct((n_rows, N_PROJ * BLK), BF16),
            jax.ShapeDtypeStruct((n_rows, D_CONV), BF16),
            jax.ShapeDtypeStruct((n_rows, D_CONV), BF16),
            jax.ShapeDtypeStruct((N_DEV, *wo_shard.shape), BF16),
        ),
        scratch_shapes=[
            pltpu.VMEM((8, D_CONV), F32),
            pltpu.VMEM(wo_shard.shape, BF16),
            pltpu.SemaphoreType.DMA((GATHER_PAIRS,)),
            pltpu.SemaphoreType.DMA((GATHER_PAIRS,)),
            pltpu.SemaphoreType.DMA(()),
        ],
        compiler_params=_params(),
    )(x2, meta_chunk, g1, w3, cw8, *rot, wo_shard)


def _group_norm(o):
    ys, rs = [], []
    for hd in range(N_HEADS):
        oh = o[:, hd * HEAD_DIM : (hd + 1) * HEAD_DIM]
        xc = oh - jnp.mean(oh, axis=-1, keepdims=True)
        rstd = lax.rsqrt(jnp.mean(xc * xc, axis=-1, keepdims=True) + EPS)
        ys.append(xc * rstd)
        rs.append(jnp.broadcast_to(rstd, oh.shape))
    return jnp.concatenate(ys, axis=1), jnp.concatenate(rs, axis=1)


def _f2(proj, conv_out, x2, t2, wout, gr, gf, dec, zeta, xi):
    n_rows = proj.shape[0]
    tm = ROW_TILE
    nt = n_rows // tm
    nct = tm // CHUNK

    def body(q_ref, k_ref, v_ref, rg_ref, co_ref, x_ref, t_ref, wo_ref, gr_ref, gf_ref, dec_ref, zeta_ref,
             xi_ref, dh2_ref, o_ref, st_ref, gw_ref, acc_ref, state, gacc):
        i = pl.program_id(0)

        def mixer(r0):
            chunks = range(r0 // CHUNK, nct)
            heads = []
            for hd in range(N_HEADS):
                cs = slice(hd * HEAD_DIM, (hd + 1) * HEAD_DIM)
                upd = {}
                for c in chunks:
                    rs = slice(c * CHUNK, (c + 1) * CHUNK)
                    kz = (k_ref[rs, cs].astype(F32) * zeta_ref[hd]).astype(BF16)
                    upd[c] = lax.dot_general(kz, v_ref[rs, cs], TN, preferred_element_type=F32)
                st = state[hd]
                outs = []
                for c in chunks:
                    rs = slice(c * CHUNK, (c + 1) * CHUNK)
                    q, k, v = q_ref[rs, cs], k_ref[rs, cs], v_ref[rs, cs]
                    st_bf = st.astype(BF16)
                    st_ref[c, hd] = st_bf
                    s = lax.dot_general(q, k, NT, preferred_element_type=F32) * dec_ref[hd]
                    inner = jnp.dot(s.astype(BF16), v, preferred_element_type=F32)
                    qx = (q.astype(F32) * xi_ref[hd]).astype(BF16)
                    outs.append(inner + jnp.dot(qx, st_bf, preferred_element_type=F32))
                    st = CHUNK_DECAY[hd] * st + upd[c]
                state[hd] = st
                heads.append(jnp.concatenate(outs, axis=0) if len(outs) > 1 else outs[0])
            o = jnp.concatenate(heads, axis=1)
            o_ref[r0:tm, :] = o.astype(BF16)
            yh, _ = _group_norm(o)
            rg = rg_ref[r0:tm, :].astype(F32)
            return (yh * gr_ref[...] * (rg * _sigmoid(rg))).astype(BF16)

        @pl.when(i == 0)
        def _():
            state[...] = jnp.zeros_like(state)
            acc_ref[...] = jnp.zeros_like(acc_ref)
            gacc[...] = jnp.zeros_like(gacc)
            mixer(LIVE0)
            dh2_ref[...] = jnp.zeros_like(dh2_ref)

        @pl.when(i > 0)
        def _():
            ro = mixer(0)
            h2 = (
                x_ref[...]
                + jnp.dot(co_ref[...], wo_ref[0:D_CONV], preferred_element_type=F32)
                + jnp.dot(ro, wo_ref[D_CONV:], preferred_element_type=F32)
            )
            r2 = lax.rsqrt(jnp.mean(h2 * h2, axis=-1, keepdims=True) + EPS)
            yn = h2 * r2
            gfv = gf_ref[...]
            err = yn * gfv - t_ref[...]
            tile_loss = jnp.sum(jnp.sum(err * err, axis=-1, keepdims=True), axis=0, keepdims=True) * (0.5 / D_MODEL)
            acc_ref[0:1, :] += jnp.sum(err * yn, axis=0, keepdims=True) * (1.0 / D_MODEL)
            acc_ref[1:2, :] += tile_loss
            dyn = err * (gfv * (1.0 / D_MODEL))
            dh2 = r2 * (dyn - yn * jnp.mean(dyn * yn, axis=-1, keepdims=True))
            dh2_ref[...] = dh2
            dh2_bf = dh2.astype(BF16)
            gacc[0:D_CONV, :] += lax.dot_general(co_ref[...], dh2_bf, TN, preferred_element_type=F32)
            gacc[D_CONV:, :] += lax.dot_general(ro, dh2_bf, TN, preferred_element_type=F32)

        @pl.when(i == nt - 1)
        def _():
            gw_ref[...] = gacc[...].astype(BF16)

    row = lambda w, j=0: pl.BlockSpec((tm, w), lambda i: (i, j))
    tok = pl.BlockSpec((tm, D_MODEL), lambda i: (_token_tile(i), 0))
    return pl.pallas_call(
        body,
        name="f2_retention_out",
        grid=(nt,),
        in_specs=[row(BLK, 4), row(BLK, 5), row(BLK, 6), row(BLK, 7), row(D_CONV), tok, tok] + [_VMEM] * 6,
        out_specs=(
            row(D_MODEL),
            row(D_RET),
            pl.BlockSpec((nct, N_HEADS, HEAD_DIM, HEAD_DIM), lambda i: (i, 0, 0, 0)),
            pl.BlockSpec((D_MODEL, D_MODEL), lambda i: (0, 0)),
            pl.BlockSpec((8, D_MODEL), lambda i: (0, 0)),
        ),
        out_shape=(
            jax.ShapeDtypeStruct((n_rows, D_MODEL), F32),
            jax.ShapeDtypeStruct((n_rows, D_RET), BF16),
            jax.ShapeDtypeStruct((n_rows // CHUNK, N_HEADS, HEAD_DIM, HEAD_DIM), BF16),
            jax.ShapeDtypeStruct((D_MODEL, D_MODEL), BF16),
            jax.ShapeDtypeStruct((8, D_MODEL), F32),
        ),
        scratch_shapes=[
            pltpu.VMEM((N_HEADS, HEAD_DIM, HEAD_DIM), F32),
            pltpu.VMEM((D_MODEL, D_MODEL), F32),
        ],
        compiler_params=_params(),
    )(proj, proj, proj, proj, conv_out, x2, t2, wout, gr, gf, dec, zeta, xi)


def _b1(dh2, proj, o, states, wout, gr, dec, dect, zeta, xi, rot):
    n_rows = dh2.shape[0]
    tm = ROW_TILE
    nt = n_rows // tm
    nct = tm // CHUNK

    def body(dh2_ref, pr_ref, o_ref, st_ref, wo_ref, gr_ref, dec_ref, dect_ref, zeta_ref, xi_ref,
             ca_ref, sa_ref, cb_ref, sb_ref, dp_ref, dco_ref, acc_ref, dstate):
        i = pl.program_id(0)
        tile = nt - 1 - i
        cos_all, sin_all = _tile_rotary(ca_ref, sa_ref, cb_ref, sb_ref, tile)

        @pl.when(i == 0)
        def _():
            dstate[...] = jnp.zeros_like(dstate)
            acc_ref[...] = jnp.zeros_like(acc_ref)

        @pl.when(tile > 0)
        def _():
            dmix = lax.dot_general(dh2_ref[...].astype(BF16), wo_ref[...], NT, preferred_element_type=F32)
            dco_ref[...] = dmix[:, :D_CONV].astype(BF16)
            dro = dmix[:, D_CONV:]
            rg = pr_ref[:, 3 * BLK : 4 * BLK].astype(F32)
            sg = _sigmoid(rg)
            silu, dsilu = rg * sg, sg * (1.0 + rg * (1.0 - sg))
            yh, rstd = _group_norm(o_ref[...].astype(F32))
            grv = gr_ref[...]
            dp_ref[:, 3 * BLK : 4 * BLK] = (dro * (yh * grv) * dsilu).astype(BF16)
            dret = dro * silu
            acc_ref[0:1, :] += jnp.sum(dret * yh, axis=0, keepdims=True)
            dyh = dret * grv
            for hd in range(N_HEADS):
                cs = slice(hd * HEAD_DIM, (hd + 1) * HEAD_DIM)
                a, b = dyh[:, cs], yh[:, cs]
                do_head = (
                    rstd[:, cs]
                    * (a - jnp.mean(a, axis=-1, keepdims=True) - b * jnp.mean(a * b, axis=-1, keepdims=True))
                ).astype(BF16)
                dupd = {}
                for c in range(nct):
                    rs = slice(c * CHUNK, (c + 1) * CHUNK)
                    qx = (pr_ref[rs, cs].astype(F32) * xi_ref[hd]).astype(BF16)
                    dupd[c] = lax.dot_general(qx, do_head[rs, :], TN, preferred_element_type=F32)
                dst = dstate[hd]
                for c in reversed(range(nct)):
                    rs = slice(c * CHUNK, (c + 1) * CHUNK)
                    cos_t, sin_t = cos_all[rs, :], sin_all[rs, :]
                    q = pr_ref[rs, hd * HEAD_DIM : (hd + 1) * HEAD_DIM]
                    k = pr_ref[rs, BLK + hd * HEAD_DIM : BLK + (hd + 1) * HEAD_DIM]
                    v = pr_ref[rs, 2 * BLK + hd * HEAD_DIM : 2 * BLK + (hd + 1) * HEAD_DIM]
                    do = do_head[rs, :]
                    st_bf = st_ref[c, hd]
                    dst_bf = dst.astype(BF16)
                    zt, xt = zeta_ref[hd], xi_ref[hd]
                    sT = (lax.dot_general(k, q, NT, preferred_element_type=F32) * dect_ref[hd]).astype(BF16)
                    dsT = (lax.dot_general(v, do, NT, preferred_element_type=F32) * dect_ref[hd]).astype(BF16)
                    ds = (lax.dot_general(do, v, NT, preferred_element_type=F32) * dec_ref[hd]).astype(BF16)
                    kz = (k.astype(F32) * zt).astype(BF16)
                    dv = jnp.dot(sT, do, preferred_element_type=F32) + jnp.dot(kz, dst_bf, preferred_element_type=F32)
                    dq = jnp.dot(ds, k, preferred_element_type=F32) + xt * lax.dot_general(
                        do, st_bf, NT, preferred_element_type=F32
                    )
                    dk = jnp.dot(dsT, q, preferred_element_type=F32) + zt * lax.dot_general(
                        v, dst_bf, NT, preferred_element_type=F32
                    )
                    dst = CHUNK_DECAY[hd] * dst + dupd[c]
                    dp_ref[rs, hd * HEAD_DIM : (hd + 1) * HEAD_DIM] = _rot_bwd(dq * Q_SCALE, cos_t, sin_t).astype(BF16)
                    dp_ref[rs, BLK + hd * HEAD_DIM : BLK + (hd + 1) * HEAD_DIM] = _rot_bwd(dk, cos_t, sin_t).astype(BF16)
                    dp_ref[rs, 2 * BLK + hd * HEAD_DIM : 2 * BLK + (hd + 1) * HEAD_DIM] = dv.astype(BF16)
                dstate[hd] = dst

        @pl.when(tile == 0)
        def _():
            rs = slice(LIVE0, tm)
            cos_t, sin_t = cos_all[rs, :], sin_all[rs, :]
            zeros = jnp.zeros((tm - LIVE0, HEAD_DIM), BF16)
            for hd in range(N_HEADS):
                k = pr_ref[rs, BLK + hd * HEAD_DIM : BLK + (hd + 1) * HEAD_DIM]
                v = pr_ref[rs, 2 * BLK + hd * HEAD_DIM : 2 * BLK + (hd + 1) * HEAD_DIM]
                dst_bf = dstate[hd].astype(BF16)
                zt = zeta_ref[hd]
                kz = (k.astype(F32) * zt).astype(BF16)
                dv = jnp.dot(kz, dst_bf, preferred_element_type=F32)
                dk = zt * lax.dot_general(v, dst_bf, NT, preferred_element_type=F32)
                dp_ref[rs, hd * HEAD_DIM : (hd + 1) * HEAD_DIM] = zeros
                dp_ref[rs, BLK + hd * HEAD_DIM : BLK + (hd + 1) * HEAD_DIM] = _rot_bwd(dk, cos_t, sin_t).astype(BF16)
                dp_ref[rs, 2 * BLK + hd * HEAD_DIM : 2 * BLK + (hd + 1) * HEAD_DIM] = dv.astype(BF16)
                dp_ref[rs, 3 * BLK + hd * HEAD_DIM : 3 * BLK + (hd + 1) * HEAD_DIM] = zeros
            dco_ref[rs, :] = jnp.zeros((tm - LIVE0, D_CONV), BF16)

    half = N_PROJ * BLK // 2
    rev = lambda w, j=0: pl.BlockSpec((tm, w), lambda i: (nt - 1 - i, j))
    return pl.pallas_call(
        body,
        name="b1_dret",
        grid=(nt,),
        in_specs=[
            rev(D_MODEL),
            rev(half, 1),
            rev(D_RET),
            pl.BlockSpec((nct, N_HEADS, HEAD_DIM, HEAD_DIM), lambda i: (nt - 1 - i, 0, 0, 0)),
        ]
        + [_VMEM] * 10,
        out_specs=(rev(half, 1), rev(D_CONV), pl.BlockSpec((8, D_RET), lambda i: (0, 0))),
        out_shape=(
            jax.ShapeDtypeStruct((n_rows, N_PROJ * BLK), BF16),
            jax.ShapeDtypeStruct((n_rows, D_CONV), BF16),
            jax.ShapeDtypeStruct((8, D_RET), F32),
        ),
        scratch_shapes=[pltpu.VMEM((N_HEADS, HEAD_DIM, HEAD_DIM), F32)],
        compiler_params=_params(),
    )(dh2, proj, o, states, wout, gr, dec, dect, zeta, xi, *rot)


def _b2(dproj, dco, proj, conv, w3, cw8, x2, meta_chunk, dh2, g1):
    n_rows = dproj.shape[0]
    tm = ROW_TILE
    nt = n_rows // tm
    half = N_PROJ * BLK // 2

    def body(dpr_ref, dco_ref, pr_ref, cv_ref, w_ref, cw_ref, x_ref, mc_ref, dh2_ref, g_ref,
             dpc_ref, gx_ref, dm_ref, acc_ref, accc_ref, halo):
        i = pl.program_id(0)
        tile = nt - 1 - i

        @pl.when(i == 0)
        def _():
            acc_ref[...] = jnp.zeros_like(acc_ref)
            accc_ref[...] = jnp.zeros_like(accc_ref)
            halo[...] = jnp.zeros_like(halo)

        def work(h, r0):
            rs = slice(r0, tm)
            n = tm - r0
            dco = dco_ref[rs, :].astype(F32)
            cx = pr_ref[rs, 0 * BLK : 1 * BLK].astype(F32)
            cb = pr_ref[rs, 1 * BLK : 2 * BLK].astype(F32)
            cc = pr_ref[rs, 2 * BLK : 3 * BLK].astype(F32)
            cg = pr_ref[rs, 3 * BLK : 4 * BLK].astype(F32)
            conv = cv_ref[rs, :].astype(F32)
            sg = _sigmoid(cg)
            silu, dsilu = cg * sg, sg * (1.0 + cg * (1.0 - sg))
            t = dco * cb
            dcb = (dco * conv * silu).astype(BF16)
            dcg = (t * conv * dsilu).astype(BF16)
            dconv = t * silu
            rows = lax.broadcasted_iota(jnp.int32, dconv.shape, 0)
            hl = halo[...]
            dc1 = jnp.where(rows == n - 1, hl[0:1], pltpu.roll(dconv, n - 1, 0))
            dc2 = jnp.where(rows == n - 2, hl[0:1], jnp.where(rows == n - 1, hl[1:2], pltpu.roll(dconv, n - 2, 0)))
            halo[...] = dconv[0:8]
            cw = cw_ref[...]
            du = cw[2:3] * dconv + cw[1:2] * dc1 + cw[0:1] * dc2
            u = cc * cx
            accc_ref[0:1, :] += jnp.sum(u * dc2, axis=0, keepdims=True)
            accc_ref[1:2, :] += jnp.sum(u * dc1, axis=0, keepdims=True)
            accc_ref[2:3, :] += jnp.sum(u * dconv, axis=0, keepdims=True)
            dcx = (du * cc).astype(BF16)
            dcc = (du * cx).astype(BF16)
            dhn = lax.dot_general(dpr_ref[rs, 0:BLK], w_ref[4], NT, preferred_element_type=F32)
            for j in range(1, 4):
                dhn += lax.dot_general(dpr_ref[rs, j * BLK : (j + 1) * BLK], w_ref[4 + j], NT, preferred_element_type=F32)
            for blk, d in ((1, dcb), (3, dcg), (0, dcx), (2, dcc)):
                dpc_ref[rs, blk * BLK : (blk + 1) * BLK] = d
                dhn += lax.dot_general(d, w_ref[blk], NT, preferred_element_type=F32)
            r = lax.rsqrt(jnp.mean(h * h, axis=-1, keepdims=True) + EPS)
            hh = h * r
            acc_ref[0:1, :] += jnp.sum(dhn * hh, axis=0, keepdims=True)
            dg = dhn * g_ref[...]
            return dh2_ref[rs, :] + r * (dg - hh * jnp.mean(dg * hh, axis=-1, keepdims=True))

        @pl.when(tile == 0)
        def _():
            dh = work(mc_ref[...], LIVE0)
            dm_ref[...] = dh[CHUNK - N_META : CHUNK]

        @pl.when(tile > 0)
        def _():
            gx_ref[...] = work(x_ref[...], 0)

    rev = lambda w, j=0: pl.BlockSpec((tm, w), lambda i: (nt - 1 - i, j))
    tok = pl.BlockSpec((tm, D_MODEL), lambda i: (_token_tile(nt - 1 - i), 0))
    const = lambda r, c: pl.BlockSpec((r, c), lambda i: (0, 0))
    return pl.pallas_call(
        body,
        name="b2_dconv_dh",
        grid=(nt,),
        in_specs=[rev(half, 1), rev(D_CONV), rev(half, 0), rev(D_CONV), _VMEM, _VMEM, tok, _VMEM, rev(D_MODEL), _VMEM],
        out_specs=(rev(half, 0), tok, const(N_META, D_MODEL), const(8, D_MODEL), const(8, D_CONV)),
        out_shape=(
            jax.ShapeDtypeStruct(dproj.shape, BF16),
            jax.ShapeDtypeStruct(x2.shape, F32),
            jax.ShapeDtypeStruct((N_META, D_MODEL), F32),
            jax.ShapeDtypeStruct((8, D_MODEL), F32),
            jax.ShapeDtypeStruct((8, D_CONV), F32),
        ),
        scratch_shapes=[pltpu.VMEM((8, D_CONV), F32)],
        input_output_aliases={0: 0},
        compiler_params=_params(),
    )(dproj, dco, proj, conv, w3, cw8, x2, meta_chunk, dh2, g1)


def _gw_in_scatter(hnT, dproj, gw_out_parts, sc_parts, ga_part, me_arr):
    n_rows = dproj.shape[0]
    last = N_DEV - 1
    by_dest = (True, True, False)

    def body(me_ref, a_ref, b_ref, go_ref, sc_ref, ga_ref, land_in, land_go, land_sc, land_ga,
             d2d_buf, d2d_land, ici_buf, d2d_send, d2d_recv, ici_send, ici_recv, send_sems, recv_sems, local_sems):
        del me_ref
        s = pl.program_id(0)
        t = last - s
        q = t >> 1
        x, y, c = lax.axis_index("x"), lax.axis_index("y"), lax.axis_index("c")
        me = 4 * x + 2 * y + c
        chip = 2 * x + y
        srcs, lands = (go_ref, sc_ref, ga_ref), (land_go, land_sc, land_ga)

        def peer_at(k):
            return (1 - x if k & 4 else x, 1 - y if k & 2 else y, 1 - c if k & 1 else c)

        def small_copy(a, k):
            px, py, pc = peer_at(k)
            return pltpu.make_async_remote_copy(
                src_ref=srcs[a].at[4 * px + 2 * py + pc] if by_dest[a] else srcs[a],
                dst_ref=lands[a].at[me],
                send_sem=send_sems.at[a * last + k - 1],
                recv_sem=recv_sems.at[a * last + k - 1],
                device_id=(px, py, pc),
                device_id_type=MESH,
            )

        def small_local(a):
            return pltpu.make_async_copy(srcs[a].at[me] if by_dest[a] else srcs[a], lands[a].at[me], local_sems.at[a + 1])

        def d2d_copy(j):
            return pltpu.make_async_remote_copy(
                src_ref=d2d_buf.at[j],
                dst_ref=d2d_land.at[j],
                send_sem=d2d_send.at[j],
                recv_sem=d2d_recv.at[j],
                device_id=(x, y, 1 - c),
                device_id_type=MESH,
            )

        def ici_copy(j, to):
            return pltpu.make_async_remote_copy(
                src_ref=ici_buf.at[j],
                dst_ref=land_in.at[chip],
                send_sem=ici_send.at[j],
                recv_sem=ici_recv.at[j],
                device_id=to,
                device_id_type=MESH,
            )

        def own_copy():
            return pltpu.make_async_copy(ici_buf.at[0], land_in.at[chip], local_sems.at[0])

        @pl.when(s == 0)
        def _():
            for a in range(3):
                small_local(a).start()
            for k in range(last, 0, -1):
                for a in range(3):
                    small_copy(a, k).start()

        blk = jnp.dot(a_ref[:, LIVE0:], b_ref[LIVE0:, :], preferred_element_type=F32)

        @pl.when((t & 1) == 1)
        def _():
            d2d_buf[q] = blk.astype(BF16)
            d2d_copy(q).start()

        @pl.when((t & 1) == 0)
        def _():
            d2d_copy(q).wait_recv()
            ici_buf[q] = (blk + d2d_land[q].astype(F32)).astype(BF16)

            @pl.when(t != 0)
            def _():
                ici_copy(q, (jnp.bitwise_xor(x, (t >> 2) & 1), jnp.bitwise_xor(y, (t >> 1) & 1), c)).start()

            @pl.when(t == 0)
            def _():
                own_copy().start()
                for j in range(N_CHIPS):
                    d2d_copy(j).wait_send()
                for j in range(N_CHIPS - 1, 0, -1):
                    ici_copy(j, peer_at(2 * j)).wait()
                for k in range(last, 0, -1):
                    for a in range(3):
                        small_copy(a, k).wait()
                for a in range(3):
                    small_local(a).wait()
                own_copy().wait()

    smalls = (gw_out_parts, sc_parts, ga_part)
    small_blocks = [p.shape[1:] if d else p.shape for p, d in zip(smalls, by_dest, strict=True)]
    grid_spec = pltpu.PrefetchScalarGridSpec(
        num_scalar_prefetch=1,
        grid=(N_DEV,),
        in_specs=[_VMEM, pl.BlockSpec((n_rows, BLK), lambda s, me: (0, jnp.bitwise_xor(me[0], last - s)))] + [_HBM] * 3,
        out_specs=tuple([_HBM] * 4),
        scratch_shapes=[
            pltpu.VMEM((N_CHIPS, D_MODEL, BLK), BF16),
            pltpu.VMEM((N_CHIPS, D_MODEL, BLK), BF16),
            pltpu.VMEM((N_CHIPS, D_MODEL, BLK), BF16),
            pltpu.SemaphoreType.DMA((N_CHIPS,)),
            pltpu.SemaphoreType.DMA((N_CHIPS,)),
            pltpu.SemaphoreType.DMA((N_CHIPS,)),
            pltpu.SemaphoreType.DMA((N_CHIPS,)),
            pltpu.SemaphoreType.DMA((3 * last,)),
            pltpu.SemaphoreType.DMA((3 * last,)),
            pltpu.SemaphoreType.DMA((4,)),
        ],
    )
    return pl.pallas_call(
        body,
        name="gw_in_scatter",
        grid_spec=grid_spec,
        out_shape=(jax.ShapeDtypeStruct((N_CHIPS, D_MODEL, BLK), BF16),)
        + tuple(jax.ShapeDtypeStruct((N_DEV, *b), p.dtype) for b, p in zip(small_blocks, smalls, strict=True)),
        compiler_params=_params(),
    )(me_arr, hnT, dproj, gw_out_parts, sc_parts, ga_part)


def _adamw(w, g, m, v):
    m = ADAM_B1 * m + (1.0 - ADAM_B1) * g
    v = ADAM_B2 * v + (1.0 - ADAM_B2) * (g * g)
    m_hat = m / (1.0 - ADAM_B1**ADAM_STEP)
    v_hat = v / (1.0 - ADAM_B2**ADAM_STEP)
    delta = -ADAM_LR * (m_hat / (jnp.sqrt(v_hat) + ADAM_EPS) + ADAM_WD * w)
    return delta, m, v


def _sum_adamw(name, parts, w, m, v, rows_per_step):
    n_r, n_c = w.shape
    n_parts = parts.shape[0]
    tr = rows_per_step

    def body(p_ref, w_ref, m_ref, v_ref, g_ref, d_ref, nm_ref, nv_ref):
        g = p_ref[0].astype(F32)
        for s in range(1, n_parts):
            g = g + p_ref[s].astype(F32)
        g_ref[...] = g
        d_ref[...], nm_ref[...], nv_ref[...] = _adamw(w_ref[...], g, m_ref[...], v_ref[...])

    blk = pl.BlockSpec((tr, n_c), lambda i: (i, 0))
    return pl.pallas_call(
        body,
        name=name,
        grid=(n_r // tr,),
        in_specs=[pl.BlockSpec((n_parts, tr, n_c), lambda i: (0, i, 0)), blk, blk, blk],
        out_specs=(blk,) * 4,
        out_shape=(jax.ShapeDtypeStruct((n_r, n_c), F32),) * 4,
        compiler_params=_params(),
    )(parts, w, m, v)


def _small_leaves(meta, conv_w, n1, rg, fg):
    return meta, conv_w, n1.reshape(8, CHUNK), rg.reshape(4, CHUNK), fg.reshape(8, CHUNK)


def _from_small_leaves(meta, conv_w, n1, rg, fg):
    return meta, conv_w, n1.reshape(D_MODEL), rg.reshape(D_RET), fg.reshape(D_MODEL)


def _adamw_small(land_sc, land_ga, w, m, v):
    n_leaf = 5

    def body(sc_ref, ga_ref, *refs):
        ins, outs = refs[: 3 * n_leaf], refs[3 * n_leaf :]
        sc, ga = sc_ref[0], ga_ref[0]
        for s in range(1, N_DEV):
            sc = sc + sc_ref[s]
            ga = ga + ga_ref[s]
        grads = (
            sc[SC_META0 : SC_META0 + N_META],
            sc[SC_CONV0 : SC_CONV0 + 3, 0 : D_CONV // N_DEV],
            ga[GA_N1 : GA_N1 + 8],
            ga[GA_RG : GA_RG + 4],
            ga[GA_FG : GA_FG + 8],
        )
        for leaf, g in enumerate(grads):
            d, nm, nv = _adamw(ins[leaf][...], g, ins[n_leaf + leaf][...], ins[2 * n_leaf + leaf][...])
            outs[leaf][...] = g
            outs[n_leaf + leaf][...] = d
            outs[2 * n_leaf + leaf][...] = nm
            outs[3 * n_leaf + leaf][...] = nv
        outs[4 * n_leaf][...] = ga[GA_LOSS : GA_LOSS + 1]

    leaf_shapes = tuple(jax.ShapeDtypeStruct(a.shape, F32) for a in w)
    out = pl.pallas_call(
        body,
        name="adamw_small",
        out_shape=leaf_shapes * 4 + (jax.ShapeDtypeStruct((1, CHUNK), F32),),
    )(land_sc, land_ga, *w, *m, *v)
    return tuple(out[k * n_leaf : (k + 1) * n_leaf] for k in range(4)), out[4 * n_leaf]


def kernel(x, meta, norm1_g, w_in, conv_w, ret_norm_g, w_out, final_g, loss_target, m_meta, m_norm1_g, m_w_in, m_conv_w, m_ret_norm_g, m_w_out, m_final_g, v_meta, v_norm1_g, v_w_in, v_conv_w, v_ret_norm_g, v_w_out, v_final_g):
    seq = x.shape[1]
    assert x.shape == (1, seq, D_MODEL) and seq % ROW_TILE == 0
    n_tiles = seq // ROW_TILE + 1
    cols = D_CONV // N_DEV
    x2, t2 = x[0], loss_target[0]

    small_w = _small_leaves(meta, conv_w, norm1_g, ret_norm_g, final_g)
    conv8 = jnp.pad(conv_w, ((0, 5), (0, CHUNK - cols)))
    w3, small = _gather_weights(w_in, jnp.concatenate([meta, conv8], axis=0))
    meta_full = small[:, SC_META0 : SC_META0 + N_META, :].transpose(1, 0, 2).reshape(N_META, D_MODEL)
    conv_full = small[:, SC_CONV0 : SC_CONV0 + 3, :cols].transpose(1, 0, 2).reshape(3, D_CONV)
    cw8 = jnp.pad(conv_full, ((0, 5), (0, 0)))
    meta_chunk = jnp.pad(meta_full, ((CHUNK - N_META, 0), (0, 0)))

    rot = _rotary_tables(n_tiles, ROW_TILE)
    dec, dect, zeta, xi = _decay_tables()
    g1 = norm1_g.reshape(1, D_MODEL)
    gr = ret_norm_g.reshape(1, D_RET)
    gf = final_g.reshape(1, D_MODEL)

    hnT, proj, conv_out, conv, wo3 = _f1(x2, meta_chunk, g1, w3, cw8, rot, w_out)
    wout = wo3.reshape(D_MODEL, D_MODEL)
    dh2, o, states, gw_out, acc_f2 = _f2(proj, conv_out, x2, t2, wout, gr, gf, dec, zeta, xi)
    dproj_ret, dco, acc_b1 = _b1(dh2, proj, o, states, wout, gr, dec, dect, zeta, xi, rot)
    dproj, grad_x2, dmeta_rows, acc_b2, acc_conv = _b2(dproj_ret, dco, proj, conv, w3, cw8, x2, meta_chunk, dh2, g1)
    gw_out_parts = gw_out.reshape(N_DEV, D_MODEL // N_DEV, D_MODEL)

    dmeta = dmeta_rows.reshape(N_META, N_DEV, CHUNK).transpose(1, 0, 2)
    dconv = jnp.pad(acc_conv[0:3, :].reshape(3, N_DEV, cols).transpose(1, 0, 2), ((0, 0), (0, 5), (0, CHUNK - cols)))
    sc_parts = jnp.concatenate([dmeta, dconv], axis=1)
    ga_part = jnp.concatenate(
        [
            acc_b2[0].reshape(8, CHUNK),
            jnp.pad(acc_b1[0].reshape(4, CHUNK), ((0, 4), (0, 0))),
            acc_f2[0].reshape(8, CHUNK),
            jnp.pad(acc_f2[1:2, 0:CHUNK], ((0, 7), (0, 0))),
        ],
        axis=0,
    )

    me_arr = (4 * lax.axis_index("x") + 2 * lax.axis_index("y") + lax.axis_index("c")).astype(jnp.int32).reshape(1)
    land_in, land_out, land_sc, land_ga = _gw_in_scatter(hnT, dproj, gw_out_parts, sc_parts, ga_part, me_arr)

    g_w_in, d_w_in, nm_w_in, nv_w_in = _sum_adamw("adamw_w_in", land_in, w_in, m_w_in, v_w_in, 256)
    g_w_out, d_w_out, nm_w_out, nv_w_out = _sum_adamw("adamw_w_out", land_out, w_out, m_w_out, v_w_out, 64)
    small_out, loss_row = _adamw_small(
        land_sc,
        land_ga,
        small_w,
        _small_leaves(m_meta, m_conv_w, m_norm1_g, m_ret_norm_g, m_final_g),
        _small_leaves(v_meta, v_conv_w, v_norm1_g, v_ret_norm_g, v_final_g),
    )
    loss = loss_row[0, 0]
    grad_x = grad_x2.reshape(1, seq, D_MODEL)

    def leaves(w_in_leaf, w_out_leaf, small_leaves):
        meta_leaf, conv_leaf, n1_leaf, rg_leaf, fg_leaf = _from_small_leaves(*small_leaves)
        return (meta_leaf, n1_leaf, w_in_leaf, conv_leaf, rg_leaf, w_out_leaf, fg_leaf)

    return (
        loss,
        grad_x,
        *leaves(g_w_in, g_w_out, small_out[0]),
        *leaves(d_w_in, d_w_out, small_out[1]),
        *leaves(nm_w_in, nm_w_out, small_out[2]),
        *leaves(nv_w_in, nv_w_out, small_out[3]),
    )
```

```python
import functools
import math

import jax
import jax.numpy as jnp
import numpy as np
from jax import lax
from jax.experimental import pallas as pl
from jax.experimental.pallas import tpu as pltpu

F32 = jnp.float32
BF16 = jnp.bfloat16

N_DEV = 8
N_CHIPS = 4
D_MODEL = 1024
N_META = 16
CHUNK = 128
D_CONV = 512
D_RET = 512
N_HEADS = 4
HEAD_DIM = 128
N_PROJ = 8
BLK = 512
ROPE_BASE = 10000.0
EPS = 1e-6
Q_SCALE = HEAD_DIM ** -0.5
LOG_G = tuple(math.log(1.0 - 2.0 ** (-5.0 - h)) for h in range(N_HEADS))
CHUNK_DECAY = tuple(math.exp(CHUNK * lg) for lg in LOG_G)

ADAM_LR = 0.001
ADAM_B1 = 0.9
ADAM_B2 = 0.999
ADAM_EPS = 1e-08
ADAM_WD = 0.01
ADAM_STEP = 10

ROW_TILE = 512
PAD_ROWS = ROW_TILE - N_META
LIVE0 = ROW_TILE - CHUNK
VMEM_LIMIT = 56 * 1024 * 1024

SC_META0, SC_CONV0 = 0, 16
GA_N1, GA_RG, GA_FG, GA_LOSS = 0, 8, 16, 24

NT = (((1,), (1,)), ((), ()))
TN = (((0,), (0,)), ((), ()))
MESH = pl.DeviceIdType.MESH

_VMEM = pl.BlockSpec(memory_space=pltpu.VMEM)
_HBM = pl.BlockSpec(memory_space=pltpu.HBM)


def _params(n_axes=1):
    return pltpu.CompilerParams(dimension_semantics=("arbitrary",) * n_axes, vmem_limit_bytes=VMEM_LIMIT)


def _sigmoid(x):
    return 0.5 * jnp.tanh(0.5 * x) + 0.5


def _decay_tables():
    idx = np.arange(CHUNK, dtype=np.float64)
    diff = idx[:, None] - idx[None, :]
    dec = np.stack([np.where(diff >= 0, np.exp(diff * lg), 0.0) for lg in LOG_G])
    zeta = np.stack([np.exp((CHUNK - 1 - idx) * lg) for lg in LOG_G])
    xi = np.stack([np.exp((idx + 1.0) * lg) for lg in LOG_G])
    ones = np.ones((1, 1, CHUNK))
    return (
        jnp.asarray(dec, F32),
        jnp.asarray(dec.transpose(0, 2, 1), F32),
        jnp.asarray(zeta[:, :, None] * ones, F32),
        jnp.asarray(xi[:, :, None] * ones, F32),
    )


def _rotary_tables(n_tiles, tm):
    half = HEAD_DIM // 2
    freqs = (1.0 / (np.float32(ROPE_BASE) ** (np.arange(half, dtype=np.float32) / np.float32(half)))).astype(np.float64)
    sign = np.concatenate([-np.ones(half), np.ones(half)])
    two = lambda a: np.concatenate([a, a], axis=1)
    base = two((np.arange(n_tiles, dtype=np.float64) * tm - PAD_ROWS)[:, None] * freqs[None, :])
    off = two(np.arange(tm, dtype=np.float64)[:, None] * freqs[None, :])
    as32 = lambda a: jnp.asarray(a, F32)
    return as32(np.cos(base)), as32(np.sin(base) * sign), as32(np.cos(off)), as32(np.sin(off) * sign)


def _tile_rotary(ca_ref, sa_ref, cb_ref, sb_ref, tile):
    ca, sa = ca_ref[pl.ds(tile, 1), :], sa_ref[pl.ds(tile, 1), :]
    cb, sb = cb_ref[...], sb_ref[...]
    return ca * cb - sa * sb, sa * cb + ca * sb


def _rot(t, cos2, sin2):
    return t * cos2 + pltpu.roll(t, HEAD_DIM // 2, 1) * sin2


def _rot_bwd(d, cos2, sin2):
    return d * cos2 + pltpu.roll(d * sin2, HEAD_DIM // 2, 1)


def _token_tile(i):
    return jnp.maximum(i - 1, 0)


GATHER_PAIRS = 7


def _two_level_gather(src_ref, out_ref, send_sems, recv_sems, local_sem, base=0):
    x, y, c = lax.axis_index("x"), lax.axis_index("y"), lax.axis_index("c")
    me, sibling = (x, y, c), (x, y, 1 - c)
    xnb, ynb, diag = (1 - x, y), (x, 1 - y), (1 - x, 1 - y)
    relayed = (jnp.bitwise_xor(x, 1 - c), jnp.bitwise_xor(y, c))
    other = (jnp.bitwise_xor(x, c), jnp.bitwise_xor(y, 1 - c))

    def copy(k, block, to, src=None):
        dst = out_ref.at[4 * block[0] + 2 * block[1] + block[2]]
        return pltpu.make_async_remote_copy(
            src_ref=dst if src is None else src,
            dst_ref=dst,
            send_sem=send_sems.at[base + k],
            recv_sem=recv_sems.at[base + k],
            device_id=to,
            device_id_type=MESH,
        )

    mine = pltpu.make_async_copy(src_ref, out_ref.at[4 * x + 2 * y + c], local_sem)
    first = [copy(1, me, (*xnb, c), src=src_ref), copy(2, me, (*ynb, c), src=src_ref), copy(0, me, sibling, src=src_ref)]
    relay = copy(3, (*relayed, c), (*other, c))
    passed = [copy(4 + j, (*chip, c), sibling) for j, chip in enumerate((xnb, ynb, diag))]

    def start():
        mine.start()
        for cp in first:
            cp.start()

    def forward():
        copy(1 + c, (*relayed, c), me).wait_recv()
        relay.start()
        copy(2 - c, (*other, c), me).wait_recv()
        passed[0].start()
        passed[1].start()

    def forward_relayed():
        copy(3, (*diag, c), me).wait_recv()
        passed[2].start()

    def finish():
        copy(0, sibling, me).wait_recv()
        for j, chip in enumerate((xnb, ynb, diag)):
            copy(4 + j, (*chip, 1 - c), me).wait_recv()
        for cp in first + [relay] + passed:
            cp.wait_send()
        mine.wait()

    return start, forward, forward_relayed, finish


W_IN_CHUNKS = 4


def _gather_weights_and_norm(x2, g1, w_shard, small):
    per = GATHER_PAIRS
    n_ch = W_IN_CHUNKS
    rows = w_shard.shape[0] // n_ch
    tm = ROW_TILE
    nt = x2.shape[0] // tm
    starts = [0] * (n_ch + 1)
    forwards = [min(5 + 2 * p, nt - 1) for p in range(n_ch)] + [min(12, nt - 1)]
    relayeds = [min(12 + p, nt - 1) for p in range(n_ch)] + [nt - 1]
    finishes = [nt - 1] * (n_ch + 1)

    def body(x_ref, g_ref, w_ref, small_ref, hn_ref, hnT_ref, w_all_ref, small_all_ref,
             w_bf, send_sems, recv_sems, local_sems):
        i = pl.program_id(0)

        @pl.when(i == 0)
        def _():
            w_bf[...] = w_ref[...].astype(BF16)

        parts = [
            _two_level_gather(
                w_bf.at[pl.ds(ch * rows, rows)],
                w_all_ref.at[:, pl.ds(ch * rows, rows)],
                send_sems,
                recv_sems,
                local_sems.at[ch],
                ch * per,
            )
            for ch in range(n_ch)
        ]
        parts.append(_two_level_gather(small_ref, small_all_ref, send_sems, recv_sems, local_sems.at[n_ch], n_ch * per))
        for phase, steps in enumerate((starts, forwards, relayeds, finishes)):
            for part, step in zip(parts, steps, strict=True):
                pl.when(i == step)(part[phase])

        h = x_ref[...]
        r = lax.rsqrt(jnp.mean(h * h, axis=-1, keepdims=True) + EPS)
        hn = (h * r * g_ref[...]).astype(BF16)
        hn_ref[...] = hn
        hnT_ref[...] = hn.T

    return pl.pallas_call(
        body,
        name="gather_weights_norm",
        grid=(nt,),
        out_shape=(
            jax.ShapeDtypeStruct(x2.shape, BF16),
            jax.ShapeDtypeStruct(x2.shape[::-1], BF16),
            jax.ShapeDtypeStruct((N_DEV, *w_shard.shape), BF16),
            jax.ShapeDtypeStruct((N_DEV, *small.shape), small.dtype),
        ),
        in_specs=[pl.BlockSpec((tm, D_MODEL), lambda i: (i, 0)), _VMEM, _VMEM, _HBM],
        out_specs=(
            pl.BlockSpec((tm, D_MODEL), lambda i: (i, 0)),
            pl.BlockSpec((D_MODEL, tm), lambda i: (0, i)),
            _HBM,
            _HBM,
        ),
        scratch_shapes=[
            pltpu.VMEM(w_shard.shape, BF16),
            pltpu.SemaphoreType.DMA(((n_ch + 1) * per,)),
            pltpu.SemaphoreType.DMA(((n_ch + 1) * per,)),
            pltpu.SemaphoreType.DMA((n_ch + 1,)),
        ],
        compiler_params=_params(),
    )(x2, g1, w_shard, small)


def _f1(hn_tok, meta_chunk, g1, w3, cw8, rot, wo_shard):
    tm = ROW_TILE
    nt = hn_tok.shape[0] // tm + 1
    n_rows = nt * tm

    def body(hn_ref, mc_ref, g_ref, w_ref, cw_ref, ca_ref, sa_ref, cb_ref, sb_ref, wo_ref,
             hnT_ref, pr_ref, co_ref, cv_ref, wo_all_ref, halo, wo_bf, send_sems, recv_sems, local_sem):
        i = pl.program_id(0)

        @pl.when(i == 0)
        def _():
            wo_bf[...] = wo_ref[...].astype(BF16)

        wo_phases = _two_level_gather(wo_bf, wo_all_ref, send_sems, recv_sems, local_sem)
        for step, phase in zip((0, nt // 3, 2 * nt // 3, nt - 1), wo_phases, strict=True):
            pl.when(i == step)(phase)

        def work(hn, r0):
            rs = slice(r0, tm)
            n = tm - r0

            def proj(j):
                return jnp.dot(hn, w_ref[j], preferred_element_type=F32)

            cx, cb, cc, cg = proj(0), proj(1), proj(2), proj(3)
            u = cc * cx
            rows = lax.broadcasted_iota(jnp.int32, u.shape, 0)
            hl = halo[...]
            u1 = jnp.where(rows == 0, hl[7:8], pltpu.roll(u, 1, 0))
            u2 = jnp.where(rows == 0, hl[6:7], jnp.where(rows == 1, hl[7:8], pltpu.roll(u, 2, 0)))
            halo[...] = u[n - 8 : n]
            cw = cw_ref[...]
            conv = cw[0:1] * u2 + cw[1:2] * u1 + cw[2:3] * u
            co = (cb * conv * (cg * _sigmoid(cg))).astype(BF16)
            co_ref[rs, :] = co
            cv_ref[rs, :] = conv.astype(BF16)
            pr_ref[rs, 0 * BLK : 1 * BLK] = cx.astype(BF16)
            pr_ref[rs, 1 * BLK : 2 * BLK] = cb.astype(BF16)
            pr_ref[rs, 2 * BLK : 3 * BLK] = cc.astype(BF16)
            pr_ref[rs, 3 * BLK : 4 * BLK] = cg.astype(BF16)
            cos_all, sin_all = _tile_rotary(ca_ref, sa_ref, cb_ref, sb_ref, i)
            cos_t, sin_t = cos_all[rs, :], sin_all[rs, :]
            q, k = proj(4), proj(5)
            for hd in range(N_HEADS):
                c0 = hd * HEAD_DIM
                pr_ref[rs, 4 * BLK + c0 : 4 * BLK + c0 + HEAD_DIM] = (
                    _rot(q[:, c0 : c0 + HEAD_DIM], cos_t, sin_t) * Q_SCALE
                ).astype(BF16)
                pr_ref[rs, 5 * BLK + c0 : 5 * BLK + c0 + HEAD_DIM] = _rot(
                    k[:, c0 : c0 + HEAD_DIM], cos_t, sin_t
                ).astype(BF16)
            pr_ref[rs, 6 * BLK : 7 * BLK] = proj(6).astype(BF16)
            pr_ref[rs, 7 * BLK : 8 * BLK] = proj(7).astype(BF16)

        @pl.when(i == 0)
        def _():
            halo[...] = jnp.zeros_like(halo)
            h = mc_ref[...]
            r = lax.rsqrt(jnp.mean(h * h, axis=-1, keepdims=True) + EPS)
            hn = (h * r * g_ref[...]).astype(BF16)
            hnT_ref[...] = hn.T
            work(hn, LIVE0)

        @pl.when(i > 0)
        def _():
            work(hn_ref[...], 0)

    row = lambda w: pl.BlockSpec((tm, w), lambda i: (i, 0))
    return pl.pallas_call(
        body,
        name="f1_inproj_conv",
        grid=(nt,),
        in_specs=[pl.BlockSpec((tm, D_MODEL), lambda i: (_token_tile(i), 0))] + [_VMEM] * 9,
        out_specs=(
            pl.BlockSpec((D_MODEL, CHUNK), lambda i: (0, 0)),
            row(N_PROJ * BLK),
            row(D_CONV),
            row(D_CONV),
            _HBM,
        ),
        out_shape=(
            jax.ShapeDtypeStruct((D_MODEL, CHUNK), BF16),
            jax.ShapeDtypeStruct((n_rows, N_PROJ * BLK), BF16),
            jax.ShapeDtypeStruct((n_rows, D_CONV), BF16),
            jax.ShapeDtypeStruct((n_rows, D_CONV), BF16),
            jax.ShapeDtypeStruct((N_DEV, *wo_shard.shape), BF16),
        ),
        scratch_shapes=[
            pltpu.VMEM((8, D_CONV), F32),
            pltpu.VMEM(wo_shard.shape, BF16),
            pltpu.SemaphoreType.DMA((GATHER_PAIRS,)),
            pltpu.SemaphoreType.DMA((GATHER_PAIRS,)),
            pltpu.SemaphoreType.DMA(()),
        ],
        compiler_params=_params(),
    )(hn_tok, meta_chunk, g1, w3, cw8, *rot, wo_shard)


def _group_norm(o):
    ys, rs = [], []
    for hd in range(N_HEADS):
        oh = o[:, hd * HEAD_DIM : (hd + 1) * HEAD_DIM]
        xc = oh - jnp.mean(oh, axis=-1, keepdims=True)
        rstd = lax.rsqrt(jnp.mean(xc * xc, axis=-1, keepdims=True) + EPS)
        ys.append(xc * rstd)
        rs.append(jnp.broadcast_to(rstd, oh.shape))
    return jnp.concatenate(ys, axis=1), jnp.concatenate(rs, axis=1)


def _f2(proj, conv_out, x2, t2, wout, gr, gf, dec, zeta, xi):
    n_rows = proj.shape[0]
    tm = ROW_TILE
    nt = n_rows // tm
    nct = tm // CHUNK

    def body(q_ref, k_ref, v_ref, rg_ref, co_ref, x_ref, t_ref, wo_ref, gr_ref, gf_ref, dec_ref, zeta_ref,
             xi_ref, dh2_ref, o_ref, st_ref, gw_ref, acc_ref, state, gacc):
        i = pl.program_id(0)

        def mixer(r0):
            chunks = range(r0 // CHUNK, nct)
            heads = []
            for hd in range(N_HEADS):
                cs = slice(hd * HEAD_DIM, (hd + 1) * HEAD_DIM)
                upd = {}
                for c in chunks:
                    rs = slice(c * CHUNK, (c + 1) * CHUNK)
                    kz = (k_ref[rs, cs].astype(F32) * zeta_ref[hd]).astype(BF16)
                    upd[c] = lax.dot_general(kz, v_ref[rs, cs], TN, preferred_element_type=F32)
                st = state[hd]
                outs = []
                for c in chunks:
                    rs = slice(c * CHUNK, (c + 1) * CHUNK)
                    q, k, v = q_ref[rs, cs], k_ref[rs, cs], v_ref[rs, cs]
                    st_bf = st.astype(BF16)
                    st_ref[c, hd] = st_bf
                    s = lax.dot_general(q, k, NT, preferred_element_type=F32) * dec_ref[hd]
                    inner = jnp.dot(s.astype(BF16), v, preferred_element_type=F32)
                    qx = (q.astype(F32) * xi_ref[hd]).astype(BF16)
                    outs.append(inner + jnp.dot(qx, st_bf, preferred_element_type=F32))
                    st = CHUNK_DECAY[hd] * st + upd[c]
                state[hd] = st
                heads.append(jnp.concatenate(outs, axis=0) if len(outs) > 1 else outs[0])
            o = jnp.concatenate(heads, axis=1)
            o_ref[r0:tm, :] = o.astype(BF16)
            yh, _ = _group_norm(o)
            rg = rg_ref[r0:tm, :].astype(F32)
            return (yh * gr_ref[...] * (rg * _sigmoid(rg))).astype(BF16)

        @pl.when(i == 0)
        def _():
            state[...] = jnp.zeros_like(state)
            acc_ref[...] = jnp.zeros_like(acc_ref)
            gacc[...] = jnp.zeros_like(gacc)
            mixer(LIVE0)
            dh2_ref[...] = jnp.zeros_like(dh2_ref)

        @pl.when(i > 0)
        def _():
            ro = mixer(0)
            h2 = (
                x_ref[...]
                + jnp.dot(co_ref[...], wo_ref[0:D_CONV], preferred_element_type=F32)
                + jnp.dot(ro, wo_ref[D_CONV:], preferred_element_type=F32)
            )
            r2 = lax.rsqrt(jnp.mean(h2 * h2, axis=-1, keepdims=True) + EPS)
            yn = h2 * r2
            gfv = gf_ref[...]
            err = yn * gfv - t_ref[...]
            tile_loss = jnp.sum(jnp.sum(err * err, axis=-1, keepdims=True), axis=0, keepdims=True) * (0.5 / D_MODEL)
            acc_ref[0:1, :] += jnp.sum(err * yn, axis=0, keepdims=True) * (1.0 / D_MODEL)
            acc_ref[1:2, :] += tile_loss
            dyn = err * (gfv * (1.0 / D_MODEL))
            dh2 = r2 * (dyn - yn * jnp.mean(dyn * yn, axis=-1, keepdims=True))
            dh2_ref[...] = dh2
            dh2_bf = dh2.astype(BF16)
            gacc[0:D_CONV, :] += lax.dot_general(co_ref[...], dh2_bf, TN, preferred_element_type=F32)
            gacc[D_CONV:, :] += lax.dot_general(ro, dh2_bf, TN, preferred_element_type=F32)

        @pl.when(i == nt - 1)
        def _():
            gw_ref[...] = gacc[...].astype(BF16)

    row = lambda w, j=0: pl.BlockSpec((tm, w), lambda i: (i, j))
    tok = pl.BlockSpec((tm, D_MODEL), lambda i: (_token_tile(i), 0))
    return pl.pallas_call(
        body,
        name="f2_retention_out",
        grid=(nt,),
        in_specs=[row(BLK, 4), row(BLK, 5), row(BLK, 6), row(BLK, 7), row(D_CONV), tok, tok] + [_VMEM] * 6,
        out_specs=(
            row(D_MODEL),
            row(D_RET),
            pl.BlockSpec((nct, N_HEADS, HEAD_DIM, HEAD_DIM), lambda i: (i, 0, 0, 0)),
            pl.BlockSpec((D_MODEL, D_MODEL), lambda i: (0, 0)),
            pl.BlockSpec((8, D_MODEL), lambda i: (0, 0)),
        ),
        out_shape=(
            jax.ShapeDtypeStruct((n_rows, D_MODEL), F32),
            jax.ShapeDtypeStruct((n_rows, D_RET), BF16),
            jax.ShapeDtypeStruct((n_rows // CHUNK, N_HEADS, HEAD_DIM, HEAD_DIM), BF16),
            jax.ShapeDtypeStruct((D_MODEL, D_MODEL), BF16),
            jax.ShapeDtypeStruct((8, D_MODEL), F32),
        ),
        scratch_shapes=[
            pltpu.VMEM((N_HEADS, HEAD_DIM, HEAD_DIM), F32),
            pltpu.VMEM((D_MODEL, D_MODEL), F32),
        ],
        compiler_params=_params(),
    )(proj, proj, proj, proj, conv_out, x2, t2, wout, gr, gf, dec, zeta, xi)


def _b1(dh2, proj, o, states, wout, gr, dec, dect, zeta, xi, rot):
    n_rows = dh2.shape[0]
    tm = ROW_TILE
    nt = n_rows // tm
    nct = tm // CHUNK

    def body(dh2_ref, pr_ref, o_ref, st_ref, wo_ref, gr_ref, dec_ref, dect_ref, zeta_ref, xi_ref,
             ca_ref, sa_ref, cb_ref, sb_ref, dp_ref, dco_ref, acc_ref, dstate):
        i = pl.program_id(0)
        tile = nt - 1 - i
        cos_all, sin_all = _tile_rotary(ca_ref, sa_ref, cb_ref, sb_ref, tile)

        @pl.when(i == 0)
        def _():
            dstate[...] = jnp.zeros_like(dstate)
            acc_ref[...] = jnp.zeros_like(acc_ref)

        @pl.when(tile > 0)
        def _():
            dmix = lax.dot_general(dh2_ref[...].astype(BF16), wo_ref[...], NT, preferred_element_type=F32)
            dco_ref[...] = dmix[:, :D_CONV].astype(BF16)
            dro = dmix[:, D_CONV:]
            rg = pr_ref[:, 3 * BLK : 4 * BLK].astype(F32)
            sg = _sigmoid(rg)
            silu, dsilu = rg * sg, sg * (1.0 + rg * (1.0 - sg))
            yh, rstd = _group_norm(o_ref[...].astype(F32))
            grv = gr_ref[...]
            dp_ref[:, 3 * BLK : 4 * BLK] = (dro * (yh * grv) * dsilu).astype(BF16)
            dret = dro * silu
            acc_ref[0:1, :] += jnp.sum(dret * yh, axis=0, keepdims=True)
            dyh = dret * grv
            for hd in range(N_HEADS):
                cs = slice(hd * HEAD_DIM, (hd + 1) * HEAD_DIM)
                a, b = dyh[:, cs], yh[:, cs]
                do_head = (
                    rstd[:, cs]
                    * (a - jnp.mean(a, axis=-1, keepdims=True) - b * jnp.mean(a * b, axis=-1, keepdims=True))
                ).astype(BF16)
                dupd = {}
                for c in range(nct):
                    rs = slice(c * CHUNK, (c + 1) * CHUNK)
                    qx = (pr_ref[rs, cs].astype(F32) * xi_ref[hd]).astype(BF16)
                    dupd[c] = lax.dot_general(qx, do_head[rs, :], TN, preferred_element_type=F32)
                dst = dstate[hd]
                for c in reversed(range(nct)):
                    rs = slice(c * CHUNK, (c + 1) * CHUNK)
                    cos_t, sin_t = cos_all[rs, :], sin_all[rs, :]
                    q = pr_ref[rs, hd * HEAD_DIM : (hd + 1) * HEAD_DIM]
                    k = pr_ref[rs, BLK + hd * HEAD_DIM : BLK + (hd + 1) * HEAD_DIM]
                    v = pr_ref[rs, 2 * BLK + hd * HEAD_DIM : 2 * BLK + (hd + 1) * HEAD_DIM]
                    do = do_head[rs, :]
                    st_bf = st_ref[c, hd]
                    dst_bf = dst.astype(BF16)
                    zt, xt = zeta_ref[hd], xi_ref[hd]
                    sT = (lax.dot_general(k, q, NT, preferred_element_type=F32) * dect_ref[hd]).astype(BF16)
                    dsT = (lax.dot_general(v, do, NT, preferred_element_type=F32) * dect_ref[hd]).astype(BF16)
                    ds = (lax.dot_general(do, v, NT, preferred_element_type=F32) * dec_ref[hd]).astype(BF16)
                    kz = (k.astype(F32) * zt).astype(BF16)
                    dv = jnp.dot(sT, do, preferred_element_type=F32) + jnp.dot(kz, dst_bf, preferred_element_type=F32)
                    dq = jnp.dot(ds, k, preferred_element_type=F32) + xt * lax.dot_general(
                        do, st_bf, NT, preferred_element_type=F32
                    )
                    dk = jnp.dot(dsT, q, preferred_element_type=F32) + zt * lax.dot_general(
                        v, dst_bf, NT, preferred_element_type=F32
                    )
                    dst = CHUNK_DECAY[hd] * dst + dupd[c]
                    dp_ref[rs, hd * HEAD_DIM : (hd + 1) * HEAD_DIM] = _rot_bwd(dq * Q_SCALE, cos_t, sin_t).astype(BF16)
                    dp_ref[rs, BLK + hd * HEAD_DIM : BLK + (hd + 1) * HEAD_DIM] = _rot_bwd(dk, cos_t, sin_t).astype(BF16)
                    dp_ref[rs, 2 * BLK + hd * HEAD_DIM : 2 * BLK + (hd + 1) * HEAD_DIM] = dv.astype(BF16)
                dstate[hd] = dst

        @pl.when(tile == 0)
        def _():
            rs = slice(LIVE0, tm)
            cos_t, sin_t = cos_all[rs, :], sin_all[rs, :]
            zeros = jnp.zeros((tm - LIVE0, HEAD_DIM), BF16)
            for hd in range(N_HEADS):
                k = pr_ref[rs, BLK + hd * HEAD_DIM : BLK + (hd + 1) * HEAD_DIM]
                v = pr_ref[rs, 2 * BLK + hd * HEAD_DIM : 2 * BLK + (hd + 1) * HEAD_DIM]
                dst_bf = dstate[hd].astype(BF16)
                zt = zeta_ref[hd]
                kz = (k.astype(F32) * zt).astype(BF16)
                dv = jnp.dot(kz, dst_bf, preferred_element_type=F32)
                dk = zt * lax.dot_general(v, dst_bf, NT, preferred_element_type=F32)
                dp_ref[rs, hd * HEAD_DIM : (hd + 1) * HEAD_DIM] = zeros
                dp_ref[rs, BLK + hd * HEAD_DIM : BLK + (hd + 1) * HEAD_DIM] = _rot_bwd(dk, cos_t, sin_t).astype(BF16)
                dp_ref[rs, 2 * BLK + hd * HEAD_DIM : 2 * BLK + (hd + 1) * HEAD_DIM] = dv.astype(BF16)
                dp_ref[rs, 3 * BLK + hd * HEAD_DIM : 3 * BLK + (hd + 1) * HEAD_DIM] = zeros
            dco_ref[rs, :] = jnp.zeros((tm - LIVE0, D_CONV), BF16)

    half = N_PROJ * BLK // 2
    rev = lambda w, j=0: pl.BlockSpec((tm, w), lambda i: (nt - 1 - i, j))
    return pl.pallas_call(
        body,
        name="b1_dret",
        grid=(nt,),
        in_specs=[
            rev(D_MODEL),
            rev(half, 1),
            rev(D_RET),
            pl.BlockSpec((nct, N_HEADS, HEAD_DIM, HEAD_DIM), lambda i: (nt - 1 - i, 0, 0, 0)),
        ]
        + [_VMEM] * 10,
        out_specs=(rev(half, 1), rev(D_CONV), pl.BlockSpec((8, D_RET), lambda i: (0, 0))),
        out_shape=(
            jax.ShapeDtypeStruct((n_rows, N_PROJ * BLK), BF16),
            jax.ShapeDtypeStruct((n_rows, D_CONV), BF16),
            jax.ShapeDtypeStruct((8, D_RET), F32),
        ),
        scratch_shapes=[pltpu.VMEM((N_HEADS, HEAD_DIM, HEAD_DIM), F32)],
        compiler_params=_params(),
    )(dh2, proj, o, states, wout, gr, dec, dect, zeta, xi, *rot)


def _b2(dproj, dco, proj, conv, w3, cw8, x2, meta_chunk, dh2, g1):
    n_rows = dproj.shape[0]
    tm = ROW_TILE
    nt = n_rows // tm
    half = N_PROJ * BLK // 2

    def body(dpr_ref, dco_ref, pr_ref, cv_ref, w_ref, cw_ref, x_ref, mc_ref, dh2_ref, g_ref,
             dpc_ref, gx_ref, dm_ref, acc_ref, accc_ref, halo):
        i = pl.program_id(0)
        tile = nt - 1 - i

        @pl.when(i == 0)
        def _():
            acc_ref[...] = jnp.zeros_like(acc_ref)
            accc_ref[...] = jnp.zeros_like(accc_ref)
            halo[...] = jnp.zeros_like(halo)

        def work(h, r0):
            rs = slice(r0, tm)
            n = tm - r0
            dco = dco_ref[rs, :].astype(F32)
            cx = pr_ref[rs, 0 * BLK : 1 * BLK].astype(F32)
            cb = pr_ref[rs, 1 * BLK : 2 * BLK].astype(F32)
            cc = pr_ref[rs, 2 * BLK : 3 * BLK].astype(F32)
            cg = pr_ref[rs, 3 * BLK : 4 * BLK].astype(F32)
            conv = cv_ref[rs, :].astype(F32)
            sg = _sigmoid(cg)
            silu, dsilu = cg * sg, sg * (1.0 + cg * (1.0 - sg))
            t = dco * cb
            dcb = (dco * conv * silu).astype(BF16)
            dcg = (t * conv * dsilu).astype(BF16)
            dconv = t * silu
            rows = lax.broadcasted_iota(jnp.int32, dconv.shape, 0)
            hl = halo[...]
            dc1 = jnp.where(rows == n - 1, hl[0:1], pltpu.roll(dconv, n - 1, 0))
            dc2 = jnp.where(rows == n - 2, hl[0:1], jnp.where(rows == n - 1, hl[1:2], pltpu.roll(dconv, n - 2, 0)))
            halo[...] = dconv[0:8]
            cw = cw_ref[...]
            du = cw[2:3] * dconv + cw[1:2] * dc1 + cw[0:1] * dc2
            u = cc * cx
            accc_ref[0:1, :] += jnp.sum(u * dc2, axis=0, keepdims=True)
            accc_ref[1:2, :] += jnp.sum(u * dc1, axis=0, keepdims=True)
            accc_ref[2:3, :] += jnp.sum(u * dconv, axis=0, keepdims=True)
            dcx = (du * cc).astype(BF16)
            dcc = (du * cx).astype(BF16)
            dhn = lax.dot_general(dpr_ref[rs, 0:BLK], w_ref[4], NT, preferred_element_type=F32)
            for j in range(1, 4):
                dhn += lax.dot_general(dpr_ref[rs, j * BLK : (j + 1) * BLK], w_ref[4 + j], NT, preferred_element_type=F32)
            for blk, d in ((1, dcb), (3, dcg), (0, dcx), (2, dcc)):
                dpc_ref[rs, blk * BLK : (blk + 1) * BLK] = d
                dhn += lax.dot_general(d, w_ref[blk], NT, preferred_element_type=F32)
            r = lax.rsqrt(jnp.mean(h * h, axis=-1, keepdims=True) + EPS)
            hh = h * r
            acc_ref[0:1, :] += jnp.sum(dhn * hh, axis=0, keepdims=True)
            dg = dhn * g_ref[...]
            return dh2_ref[rs, :] + r * (dg - hh * jnp.mean(dg * hh, axis=-1, keepdims=True))

        @pl.when(tile == 0)
        def _():
            dh = work(mc_ref[...], LIVE0)
            dm_ref[...] = dh[CHUNK - N_META : CHUNK]

        @pl.when(tile > 0)
        def _():
            gx_ref[...] = work(x_ref[...], 0)

    rev = lambda w, j=0: pl.BlockSpec((tm, w), lambda i: (nt - 1 - i, j))
    tok = pl.BlockSpec((tm, D_MODEL), lambda i: (_token_tile(nt - 1 - i), 0))
    const = lambda r, c: pl.BlockSpec((r, c), lambda i: (0, 0))
    return pl.pallas_call(
        body,
        name="b2_dconv_dh",
        grid=(nt,),
        in_specs=[rev(half, 1), rev(D_CONV), rev(half, 0), rev(D_CONV), _VMEM, _VMEM, tok, _VMEM, rev(D_MODEL), _VMEM],
        out_specs=(rev(half, 0), tok, const(N_META, D_MODEL), const(8, D_MODEL), const(8, D_CONV)),
        out_shape=(
            jax.ShapeDtypeStruct(dproj.shape, BF16),
            jax.ShapeDtypeStruct(x2.shape, F32),
            jax.ShapeDtypeStruct((N_META, D_MODEL), F32),
            jax.ShapeDtypeStruct((8, D_MODEL), F32),
            jax.ShapeDtypeStruct((8, D_CONV), F32),
        ),
        scratch_shapes=[pltpu.VMEM((8, D_CONV), F32)],
        input_output_aliases={0: 0},
        compiler_params=_params(),
    )(dproj, dco, proj, conv, w3, cw8, x2, meta_chunk, dh2, g1)


def _gw_in_scatter(hnT_tok, hnT_meta, dproj, gw_out_parts, sc_parts, ga_part, me_arr):
    n_rows = dproj.shape[0]
    last = N_DEV - 1
    by_dest = (True, True, False)

    def body(me_ref, a_ref, am_ref, b_ref, go_ref, sc_ref, ga_ref, land_in, land_go, land_sc, land_ga,
             d2d_buf, d2d_land, ici_buf, d2d_send, d2d_recv, ici_send, ici_recv, send_sems, recv_sems, local_sems):
        del me_ref
        s = pl.program_id(0)
        t = last - s
        q = t >> 1
        x, y, c = lax.axis_index("x"), lax.axis_index("y"), lax.axis_index("c")
        me = 4 * x + 2 * y + c
        chip = 2 * x + y
        srcs, lands = (go_ref, sc_ref, ga_ref), (land_go, land_sc, land_ga)

        def peer_at(k):
            return (1 - x if k & 4 else x, 1 - y if k & 2 else y, 1 - c if k & 1 else c)

        def small_copy(a, k):
            px, py, pc = peer_at(k)
            return pltpu.make_async_remote_copy(
                src_ref=srcs[a].at[4 * px + 2 * py + pc] if by_dest[a] else srcs[a],
                dst_ref=lands[a].at[me],
                send_sem=send_sems.at[a * last + k - 1],
                recv_sem=recv_sems.at[a * last + k - 1],
                device_id=(px, py, pc),
                device_id_type=MESH,
            )

        def small_local(a):
            return pltpu.make_async_copy(srcs[a].at[me] if by_dest[a] else srcs[a], lands[a].at[me], local_sems.at[a + 1])

        def d2d_copy(j):
            return pltpu.make_async_remote_copy(
                src_ref=d2d_buf.at[j],
                dst_ref=d2d_land.at[j],
                send_sem=d2d_send.at[j],
                recv_sem=d2d_recv.at[j],
                device_id=(x, y, 1 - c),
                device_id_type=MESH,
            )

        def ici_copy(j, to):
            return pltpu.make_async_remote_copy(
                src_ref=ici_buf.at[j],
                dst_ref=land_in.at[chip],
                send_sem=ici_send.at[j],
                recv_sem=ici_recv.at[j],
                device_id=to,
                device_id_type=MESH,
            )

        def own_copy():
            return pltpu.make_async_copy(ici_buf.at[0], land_in.at[chip], local_sems.at[0])

        @pl.when(s == 0)
        def _():
            for a in range(3):
                small_local(a).start()
            for k in range(last, 0, -1):
                for a in range(3):
                    small_copy(a, k).start()

        blk = jnp.dot(a_ref[...], b_ref[ROW_TILE:, :], preferred_element_type=F32)
        blk += jnp.dot(am_ref[...], b_ref[LIVE0:ROW_TILE, :], preferred_element_type=F32)

        @pl.when((t & 1) == 1)
        def _():
            d2d_buf[q] = blk.astype(BF16)
            d2d_copy(q).start()

        @pl.when((t & 1) == 0)
        def _():
            d2d_copy(q).wait_recv()
            ici_buf[q] = (blk + d2d_land[q].astype(F32)).astype(BF16)

            @pl.when(t != 0)
            def _():
                ici_copy(q, (jnp.bitwise_xor(x, (t >> 2) & 1), jnp.bitwise_xor(y, (t >> 1) & 1), c)).start()

            @pl.when(t == 0)
            def _():
                own_copy().start()
                for j in range(N_CHIPS):
                    d2d_copy(j).wait_send()
                for j in range(N_CHIPS - 1, 0, -1):
                    ici_copy(j, peer_at(2 * j)).wait()
                for k in range(last, 0, -1):
                    for a in range(3):
                        small_copy(a, k).wait()
                for a in range(3):
                    small_local(a).wait()
                own_copy().wait()

    smalls = (gw_out_parts, sc_parts, ga_part)
    small_blocks = [p.shape[1:] if d else p.shape for p, d in zip(smalls, by_dest, strict=True)]
    grid_spec = pltpu.PrefetchScalarGridSpec(
        num_scalar_prefetch=1,
        grid=(N_DEV,),
        in_specs=[_VMEM, _VMEM, pl.BlockSpec((n_rows, BLK), lambda s, me: (0, jnp.bitwise_xor(me[0], last - s)))]
        + [_HBM] * 3,
        out_specs=tuple([_HBM] * 4),
        scratch_shapes=[
            pltpu.VMEM((N_CHIPS, D_MODEL, BLK), BF16),
            pltpu.VMEM((N_CHIPS, D_MODEL, BLK), BF16),
            pltpu.VMEM((N_CHIPS, D_MODEL, BLK), BF16),
            pltpu.SemaphoreType.DMA((N_CHIPS,)),
            pltpu.SemaphoreType.DMA((N_CHIPS,)),
            pltpu.SemaphoreType.DMA((N_CHIPS,)),
            pltpu.SemaphoreType.DMA((N_CHIPS,)),
            pltpu.SemaphoreType.DMA((3 * last,)),
            pltpu.SemaphoreType.DMA((3 * last,)),
            pltpu.SemaphoreType.DMA((4,)),
        ],
    )
    return pl.pallas_call(
        body,
        name="gw_in_scatter",
        grid_spec=grid_spec,
        out_shape=(jax.ShapeDtypeStruct((N_CHIPS, D_MODEL, BLK), BF16),)
        + tuple(jax.ShapeDtypeStruct((N_DEV, *b), p.dtype) for b, p in zip(small_blocks, smalls, strict=True)),
        compiler_params=_params(),
    )(me_arr, hnT_tok, hnT_meta, dproj, gw_out_parts, sc_parts, ga_part)


def _adamw(w, g, m, v):
    m = ADAM_B1 * m + (1.0 - ADAM_B1) * g
    v = ADAM_B2 * v + (1.0 - ADAM_B2) * (g * g)
    m_hat = m / (1.0 - ADAM_B1**ADAM_STEP)
    v_hat = v / (1.0 - ADAM_B2**ADAM_STEP)
    delta = -ADAM_LR * (m_hat / (jnp.sqrt(v_hat) + ADAM_EPS) + ADAM_WD * w)
    return delta, m, v


def _sum_adamw(name, parts, w, m, v, rows_per_step):
    n_r, n_c = w.shape
    n_parts = parts.shape[0]
    tr = rows_per_step

    def body(p_ref, w_ref, m_ref, v_ref, g_ref, d_ref, nm_ref, nv_ref):
        g = p_ref[0].astype(F32)
        for s in range(1, n_parts):
            g = g + p_ref[s].astype(F32)
        g_ref[...] = g
        d_ref[...], nm_ref[...], nv_ref[...] = _adamw(w_ref[...], g, m_ref[...], v_ref[...])

    blk = pl.BlockSpec((tr, n_c), lambda i: (i, 0))
    return pl.pallas_call(
        body,
        name=name,
        grid=(n_r // tr,),
        in_specs=[pl.BlockSpec((n_parts, tr, n_c), lambda i: (0, i, 0)), blk, blk, blk],
        out_specs=(blk,) * 4,
        out_shape=(jax.ShapeDtypeStruct((n_r, n_c), F32),) * 4,
        compiler_params=_params(),
    )(parts, w, m, v)


def _small_leaves(meta, conv_w, n1, rg, fg):
    return meta, conv_w, n1.reshape(8, CHUNK), rg.reshape(4, CHUNK), fg.reshape(8, CHUNK)


def _from_small_leaves(meta, conv_w, n1, rg, fg):
    return meta, conv_w, n1.reshape(D_MODEL), rg.reshape(D_RET), fg.reshape(D_MODEL)


def _adamw_small(land_sc, land_ga, w, m, v):
    n_leaf = 5

    def body(sc_ref, ga_ref, *refs):
        ins, outs = refs[: 3 * n_leaf], refs[3 * n_leaf :]
        sc, ga = sc_ref[0], ga_ref[0]
        for s in range(1, N_DEV):
            sc = sc + sc_ref[s]
            ga = ga + ga_ref[s]
        grads = (
            sc[SC_META0 : SC_META0 + N_META],
            sc[SC_CONV0 : SC_CONV0 + 3, 0 : D_CONV // N_DEV],
            ga[GA_N1 : GA_N1 + 8],
            ga[GA_RG : GA_RG + 4],
            ga[GA_FG : GA_FG + 8],
        )
        for leaf, g in enumerate(grads):
            d, nm, nv = _adamw(ins[leaf][...], g, ins[n_leaf + leaf][...], ins[2 * n_leaf + leaf][...])
            outs[leaf][...] = g
            outs[n_leaf + leaf][...] = d
            outs[2 * n_leaf + leaf][...] = nm
            outs[3 * n_leaf + leaf][...] = nv
        outs[4 * n_leaf][...] = ga[GA_LOSS : GA_LOSS + 1]

    leaf_shapes = tuple(jax.ShapeDtypeStruct(a.shape, F32) for a in w)
    out = pl.pallas_call(
        body,
        name="adamw_small",
        out_shape=leaf_shapes * 4 + (jax.ShapeDtypeStruct((1, CHUNK), F32),),
    )(land_sc, land_ga, *w, *m, *v)
    return tuple(out[k * n_leaf : (k + 1) * n_leaf] for k in range(4)), out[4 * n_leaf]


def kernel(x, meta, norm1_g, w_in, conv_w, ret_norm_g, w_out, final_g, loss_target, m_meta, m_norm1_g, m_w_in, m_conv_w, m_ret_norm_g, m_w_out, m_final_g, v_meta, v_norm1_g, v_w_in, v_conv_w, v_ret_norm_g, v_w_out, v_final_g):
    seq = x.shape[1]
    assert x.shape == (1, seq, D_MODEL) and seq % ROW_TILE == 0
    n_tiles = seq // ROW_TILE + 1
    cols = D_CONV // N_DEV
    x2, t2 = x[0], loss_target[0]

    small_w = _small_leaves(meta, conv_w, norm1_g, ret_norm_g, final_g)
    conv8 = jnp.pad(conv_w, ((0, 5), (0, CHUNK - cols)))
    g1 = norm1_g.reshape(1, D_MODEL)
    hn_tok, hnT_tok, w3, small = _gather_weights_and_norm(x2, g1, w_in, jnp.concatenate([meta, conv8], axis=0))
    meta_full = small[:, SC_META0 : SC_META0 + N_META, :].transpose(1, 0, 2).reshape(N_META, D_MODEL)
    conv_full = small[:, SC_CONV0 : SC_CONV0 + 3, :cols].transpose(1, 0, 2).reshape(3, D_CONV)
    cw8 = jnp.pad(conv_full, ((0, 5), (0, 0)))
    meta_chunk = jnp.pad(meta_full, ((CHUNK - N_META, 0), (0, 0)))

    rot = _rotary_tables(n_tiles, ROW_TILE)
    dec, dect, zeta, xi = _decay_tables()
    gr = ret_norm_g.reshape(1, D_RET)
    gf = final_g.reshape(1, D_MODEL)

    hnT_meta, proj, conv_out, conv, wo3 = _f1(hn_tok, meta_chunk, g1, w3, cw8, rot, w_out)
    wout = wo3.reshape(D_MODEL, D_MODEL)
    dh2, o, states, gw_out, acc_f2 = _f2(proj, conv_out, x2, t2, wout, gr, gf, dec, zeta, xi)
    dproj_ret, dco, acc_b1 = _b1(dh2, proj, o, states, wout, gr, dec, dect, zeta, xi, rot)
    dproj, grad_x2, dmeta_rows, acc_b2, acc_conv = _b2(dproj_ret, dco, proj, conv, w3, cw8, x2, meta_chunk, dh2, g1)
    gw_out_parts = gw_out.reshape(N_DEV, D_MODEL // N_DEV, D_MODEL)

    dmeta = dmeta_rows.reshape(N_META, N_DEV, CHUNK).transpose(1, 0, 2)
    dconv = jnp.pad(acc_conv[0:3, :].reshape(3, N_DEV, cols).transpose(1, 0, 2), ((0, 0), (0, 5), (0, CHUNK - cols)))
    sc_parts = jnp.concatenate([dmeta, dconv], axis=1)
    ga_part = jnp.concatenate(
        [
            acc_b2[0].reshape(8, CHUNK),
            jnp.pad(acc_b1[0].reshape(4, CHUNK), ((0, 4), (0, 0))),
            acc_f2[0].reshape(8, CHUNK),
            jnp.pad(acc_f2[1:2, 0:CHUNK], ((0, 7), (0, 0))),
        ],
        axis=0,
    )

    me_arr = (4 * lax.axis_index("x") + 2 * lax.axis_index("y") + lax.axis_index("c")).astype(jnp.int32).reshape(1)
    land_in, land_out, land_sc, land_ga = _gw_in_scatter(
        hnT_tok, hnT_meta, dproj, gw_out_parts, sc_parts, ga_part, me_arr
    )

    g_w_in, d_w_in, nm_w_in, nv_w_in = _sum_adamw("adamw_w_in", land_in, w_in, m_w_in, v_w_in, 256)
    g_w_out, d_w_out, nm_w_out, nv_w_out = _sum_adamw("adamw_w_out", land_out, w_out, m_w_out, v_w_out, 64)
    small_out, loss_row = _adamw_small(
        land_sc,
        land_ga,
        small_w,
        _small_leaves(m_meta, m_conv_w, m_norm1_g, m_ret_norm_g, m_final_g),
        _small_leaves(v_meta, v_conv_w, v_norm1_g, v_ret_norm_g, v_final_g),
    )
    loss = loss_row[0, 0]
    grad_x = grad_x2.reshape(1, seq, D_MODEL)

    def leaves(w_in_leaf, w_out_leaf, small_leaves):
        meta_leaf, conv_leaf, n1_leaf, rg_leaf, fg_leaf = _from_small_leaves(*small_leaves)
        return (meta_leaf, n1_leaf, w_in_leaf, conv_leaf, rg_leaf, w_out_leaf, fg_leaf)

    return (
        loss,
        grad_x,
        *leaves(g_w_in, g_w_out, small_out[0]),
        *leaves(d_w_in, d_w_out, small_out[1]),
        *leaves(nm_w_in, nm_w_out, small_out[2]),
        *leaves(nv_w_in, nv_w_out, small_out[3]),
    )
```

```python
import math

import jax
import jax.numpy as jnp
import numpy as np
from jax import lax
from jax.experimental import pallas as pl
from jax.experimental.pallas import tpu as pltpu

F32 = jnp.float32
BF16 = jnp.bfloat16

N_DEV = 8
N_CHIPS = 4
D_MODEL = 1024
N_META = 16
CHUNK = 128
D_CONV = 512
D_RET = 512
N_HEADS = 4
HEAD_DIM = 128
N_PROJ = 8
BLK = 512
ROPE_BASE = 10000.0
EPS = 1e-6
Q_SCALE = HEAD_DIM ** -0.5
LOG_G = tuple(math.log(1.0 - 2.0 ** (-5.0 - h)) for h in range(N_HEADS))
CHUNK_DECAY = tuple(math.exp(CHUNK * lg) for lg in LOG_G)

ADAM_LR = 0.001
ADAM_B1 = 0.9
ADAM_B2 = 0.999
ADAM_EPS = 1e-08
ADAM_WD = 0.01
ADAM_STEP = 10

ROW_TILE = 512
PAD_ROWS = ROW_TILE - N_META
LIVE0 = ROW_TILE - CHUNK
B1_CHUNK = 256
VMEM_LIMIT = 56 * 1024 * 1024

SC_META0, SC_CONV0 = 0, 16
GA_N1, GA_RG, GA_FG, GA_LOSS = 0, 8, 16, 24

NT = (((1,), (1,)), ((), ()))
TN = (((0,), (0,)), ((), ()))
MESH = pl.DeviceIdType.MESH

_VMEM = pl.BlockSpec(memory_space=pltpu.VMEM)
_HBM = pl.BlockSpec(memory_space=pltpu.HBM)


def _params(n_axes=1):
    return pltpu.CompilerParams(dimension_semantics=("arbitrary",) * n_axes, vmem_limit_bytes=VMEM_LIMIT)


def _sigmoid(x):
    return 0.5 * jnp.tanh(0.5 * x) + 0.5


def _decay_tables(chunk=CHUNK):
    idx = np.arange(chunk, dtype=np.float64)
    diff = idx[:, None] - idx[None, :]
    dec = np.stack([np.where(diff >= 0, np.exp(diff * lg), 0.0) for lg in LOG_G])
    zeta = np.stack([np.exp((chunk - 1 - idx) * lg) for lg in LOG_G])
    xi = np.stack([np.exp((idx + 1.0) * lg) for lg in LOG_G])
    ones = np.ones((1, 1, HEAD_DIM))
    return (
        jnp.asarray(dec, F32),
        jnp.asarray(dec.transpose(0, 2, 1), F32),
        jnp.asarray(zeta[:, :, None] * ones, F32),
        jnp.asarray(xi[:, :, None] * ones, F32),
    )


def _rotary_tables(n_tiles, tm):
    half = HEAD_DIM // 2
    freqs = (1.0 / (np.float32(ROPE_BASE) ** (np.arange(half, dtype=np.float32) / np.float32(half)))).astype(np.float64)
    sign = np.concatenate([-np.ones(half), np.ones(half)])
    two = lambda a: np.concatenate([a, a], axis=1)
    base = two((np.arange(n_tiles, dtype=np.float64) * tm - PAD_ROWS)[:, None] * freqs[None, :])
    off = two(np.arange(tm, dtype=np.float64)[:, None] * freqs[None, :])
    as32 = lambda a: jnp.asarray(a, F32)
    return as32(np.cos(base)), as32(np.sin(base) * sign), as32(np.cos(off)), as32(np.sin(off) * sign)


def _tile_rotary(ca_ref, sa_ref, cb_ref, sb_ref, tile):
    ca, sa = ca_ref[pl.ds(tile, 1), :], sa_ref[pl.ds(tile, 1), :]
    cb, sb = cb_ref[...], sb_ref[...]
    return ca * cb - sa * sb, sa * cb + ca * sb


def _rot(t, cos2, sin2):
    return t * cos2 + pltpu.roll(t, HEAD_DIM // 2, 1) * sin2


def _rot_bwd(d, cos2, sin2):
    return d * cos2 + pltpu.roll(d * sin2, HEAD_DIM // 2, 1)


def _token_tile(i):
    return jnp.maximum(i - 1, 0)


GATHER_PAIRS = 7


def _two_level_gather(src_ref, out_ref, send_sems, recv_sems, local_sem, base=0):
    x, y, c = lax.axis_index("x"), lax.axis_index("y"), lax.axis_index("c")
    me, sibling = (x, y, c), (x, y, 1 - c)
    xnb, ynb, diag = (1 - x, y), (x, 1 - y), (1 - x, 1 - y)
    relayed = (jnp.bitwise_xor(x, 1 - c), jnp.bitwise_xor(y, c))
    other = (jnp.bitwise_xor(x, c), jnp.bitwise_xor(y, 1 - c))

    def copy(k, block, to, src=None):
        dst = out_ref.at[4 * block[0] + 2 * block[1] + block[2]]
        return pltpu.make_async_remote_copy(
            src_ref=dst if src is None else src,
            dst_ref=dst,
            send_sem=send_sems.at[base + k],
            recv_sem=recv_sems.at[base + k],
            device_id=to,
            device_id_type=MESH,
        )

    mine = pltpu.make_async_copy(src_ref, out_ref.at[4 * x + 2 * y + c], local_sem)
    first = [copy(1, me, (*xnb, c), src=src_ref), copy(2, me, (*ynb, c), src=src_ref), copy(0, me, sibling, src=src_ref)]
    relay = copy(3, (*relayed, c), (*other, c))
    passed = [copy(4 + j, (*chip, c), sibling) for j, chip in enumerate((xnb, ynb, diag))]

    def start():
        mine.start()
        for cp in first:
            cp.start()

    def forward():
        copy(1 + c, (*relayed, c), me).wait_recv()
        relay.start()
        copy(2 - c, (*other, c), me).wait_recv()
        passed[0].start()
        passed[1].start()

    def forward_relayed():
        copy(3, (*diag, c), me).wait_recv()
        passed[2].start()

    def finish():
        copy(0, sibling, me).wait_recv()
        for j, chip in enumerate((xnb, ynb, diag)):
            copy(4 + j, (*chip, 1 - c), me).wait_recv()
        for cp in first + [relay] + passed:
            cp.wait_send()
        mine.wait()

    return start, forward, forward_relayed, finish


W_IN_CHUNKS = 4


def _gather_weights_and_norm(x2, g1, w_shard, small):
    per = GATHER_PAIRS
    n_ch = W_IN_CHUNKS
    rows = w_shard.shape[0] // n_ch
    tm = ROW_TILE
    nt = x2.shape[0] // tm
    starts = [0] * (n_ch + 1)
    forwards = [min(5 + 2 * p, nt - 1) for p in range(n_ch)] + [min(12, nt - 1)]
    relayeds = [min(12 + p, nt - 1) for p in range(n_ch)] + [nt - 1]
    finishes = [nt - 1] * (n_ch + 1)

    def body(x_ref, g_ref, w_ref, small_ref, hn_ref, hnT_ref, w_all_ref, small_all_ref,
             w_bf, send_sems, recv_sems, local_sems):
        i = pl.program_id(0)

        @pl.when(i == 0)
        def _():
            w_bf[...] = w_ref[...].astype(BF16)

        parts = [
            _two_level_gather(
                w_bf.at[pl.ds(ch * rows, rows)],
                w_all_ref.at[:, pl.ds(ch * rows, rows)],
                send_sems,
                recv_sems,
                local_sems.at[ch],
                ch * per,
            )
            for ch in range(n_ch)
        ]
        parts.append(_two_level_gather(small_ref, small_all_ref, send_sems, recv_sems, local_sems.at[n_ch], n_ch * per))
        for phase, steps in enumerate((starts, forwards, relayeds, finishes)):
            for part, step in zip(parts, steps, strict=True):
                pl.when(i == step)(part[phase])

        h = x_ref[...]
        r = lax.rsqrt(jnp.mean(h * h, axis=-1, keepdims=True) + EPS)
        hn = (h * r * g_ref[...]).astype(BF16)
        hn_ref[...] = hn
        hnT_ref[...] = hn.T

    return pl.pallas_call(
        body,
        name="gather_weights_norm",
        grid=(nt,),
        out_shape=(
            jax.ShapeDtypeStruct(x2.shape, BF16),
            jax.ShapeDtypeStruct(x2.shape[::-1], BF16),
            jax.ShapeDtypeStruct((N_DEV, *w_shard.shape), BF16),
            jax.ShapeDtypeStruct((N_DEV, *small.shape), small.dtype),
        ),
        in_specs=[pl.BlockSpec((tm, D_MODEL), lambda i: (i, 0)), _VMEM, _VMEM, _HBM],
        out_specs=(
            pl.BlockSpec((tm, D_MODEL), lambda i: (i, 0)),
            pl.BlockSpec((D_MODEL, tm), lambda i: (0, i)),
            _HBM,
            _HBM,
        ),
        scratch_shapes=[
            pltpu.VMEM(w_shard.shape, BF16),
            pltpu.SemaphoreType.DMA(((n_ch + 1) * per,)),
            pltpu.SemaphoreType.DMA(((n_ch + 1) * per,)),
            pltpu.SemaphoreType.DMA((n_ch + 1,)),
        ],
        compiler_params=_params(),
    )(x2, g1, w_shard, small)


def _f1(hn_tok, meta_chunk, g1, w3, cw8, rot, wo_shard):
    tm = ROW_TILE
    nt = hn_tok.shape[0] // tm + 1
    n_rows = nt * tm

    def body(hn_ref, mc_ref, g_ref, w_ref, cw_ref, ca_ref, sa_ref, cb_ref, sb_ref, wo_ref,
             hnT_ref, pr_ref, co_ref, cv_ref, wo_all_ref, halo, wo_bf, send_sems, recv_sems, local_sem):
        i = pl.program_id(0)

        @pl.when(i == 0)
        def _():
            wo_bf[...] = wo_ref[...].astype(BF16)

        wo_phases = _two_level_gather(wo_bf, wo_all_ref, send_sems, recv_sems, local_sem)
        for step, phase in zip((0, nt // 3, 2 * nt // 3, nt - 1), wo_phases, strict=True):
            pl.when(i == step)(phase)

        def work(hn, r0):
            rs = slice(r0, tm)
            n = tm - r0

            def proj(j):
                return jnp.dot(hn, w_ref[j], preferred_element_type=F32)

            cx, cb, cc, cg = proj(0), proj(1), proj(2), proj(3)
            u = cc * cx
            rows = lax.broadcasted_iota(jnp.int32, u.shape, 0)
            hl = halo[...]
            u1 = jnp.where(rows == 0, hl[7:8], pltpu.roll(u, 1, 0))
            u2 = jnp.where(rows == 0, hl[6:7], jnp.where(rows == 1, hl[7:8], pltpu.roll(u, 2, 0)))
            halo[...] = u[n - 8 : n]
            cw = cw_ref[...]
            conv = cw[0:1] * u2 + cw[1:2] * u1 + cw[2:3] * u
            co = (cb * conv * (cg * _sigmoid(cg))).astype(BF16)
            co_ref[rs, :] = co
            cv_ref[rs, :] = conv.astype(BF16)
            pr_ref[rs, 0 * BLK : 1 * BLK] = cx.astype(BF16)
            pr_ref[rs, 1 * BLK : 2 * BLK] = cb.astype(BF16)
            pr_ref[rs, 2 * BLK : 3 * BLK] = cc.astype(BF16)
            pr_ref[rs, 3 * BLK : 4 * BLK] = cg.astype(BF16)
            cos_all, sin_all = _tile_rotary(ca_ref, sa_ref, cb_ref, sb_ref, i)
            cos_t, sin_t = cos_all[rs, :], sin_all[rs, :]
            q, k = proj(4), proj(5)
            for hd in range(N_HEADS):
                c0 = hd * HEAD_DIM
                pr_ref[rs, 4 * BLK + c0 : 4 * BLK + c0 + HEAD_DIM] = (
                    _rot(q[:, c0 : c0 + HEAD_DIM], cos_t, sin_t) * Q_SCALE
                ).astype(BF16)
                pr_ref[rs, 5 * BLK + c0 : 5 * BLK + c0 + HEAD_DIM] = _rot(
                    k[:, c0 : c0 + HEAD_DIM], cos_t, sin_t
                ).astype(BF16)
            pr_ref[rs, 6 * BLK : 7 * BLK] = proj(6).astype(BF16)
            pr_ref[rs, 7 * BLK : 8 * BLK] = proj(7).astype(BF16)

        @pl.when(i == 0)
        def _():
            halo[...] = jnp.zeros_like(halo)
            h = mc_ref[...]
            r = lax.rsqrt(jnp.mean(h * h, axis=-1, keepdims=True) + EPS)
            hn = (h * r * g_ref[...]).astype(BF16)
            hnT_ref[...] = hn.T
            work(hn, LIVE0)

        @pl.when(i > 0)
        def _():
            work(hn_ref[...], 0)

    row = lambda w: pl.BlockSpec((tm, w), lambda i: (i, 0))
    return pl.pallas_call(
        body,
        name="f1_inproj_conv",
        grid=(nt,),
        in_specs=[pl.BlockSpec((tm, D_MODEL), lambda i: (_token_tile(i), 0))] + [_VMEM] * 9,
        out_specs=(
            pl.BlockSpec((D_MODEL, CHUNK), lambda i: (0, 0)),
            row(N_PROJ * BLK),
            row(D_CONV),
            row(D_CONV),
            _HBM,
        ),
        out_shape=(
            jax.ShapeDtypeStruct((D_MODEL, CHUNK), BF16),
            jax.ShapeDtypeStruct((n_rows, N_PROJ * BLK), BF16),
            jax.ShapeDtypeStruct((n_rows, D_CONV), BF16),
            jax.ShapeDtypeStruct((n_rows, D_CONV), BF16),
            jax.ShapeDtypeStruct((N_DEV, *wo_shard.shape), BF16),
        ),
        scratch_shapes=[
            pltpu.VMEM((8, D_CONV), F32),
            pltpu.VMEM(wo_shard.shape, BF16),
            pltpu.SemaphoreType.DMA((GATHER_PAIRS,)),
            pltpu.SemaphoreType.DMA((GATHER_PAIRS,)),
            pltpu.SemaphoreType.DMA(()),
        ],
        compiler_params=_params(),
    )(hn_tok, meta_chunk, g1, w3, cw8, *rot, wo_shard)


def _group_norm(o):
    ys, rs = [], []
    for hd in range(N_HEADS):
        oh = o[:, hd * HEAD_DIM : (hd + 1) * HEAD_DIM]
        xc = oh - jnp.mean(oh, axis=-1, keepdims=True)
        rstd = lax.rsqrt(jnp.mean(xc * xc, axis=-1, keepdims=True) + EPS)
        ys.append(xc * rstd)
        rs.append(jnp.broadcast_to(rstd, oh.shape))
    return jnp.concatenate(ys, axis=1), jnp.concatenate(rs, axis=1)


def _f2(proj, conv_out, x2, t2, wout, gr, gf, dec, zeta, xi):
    n_rows = proj.shape[0]
    tm = ROW_TILE
    nt = n_rows // tm
    nct = tm // CHUNK

    def body(q_ref, k_ref, v_ref, rg_ref, co_ref, x_ref, t_ref, wo_ref, gr_ref, gf_ref, dec_ref, zeta_ref,
             xi_ref, dh2_ref, o_ref, st_ref, gw_ref, acc_ref, state, gacc):
        i = pl.program_id(0)

        def mixer(r0):
            chunks = range(r0 // CHUNK, nct)
            heads = []
            for hd in range(N_HEADS):
                cs = slice(hd * HEAD_DIM, (hd + 1) * HEAD_DIM)
                upd = {}
                for c in chunks:
                    rs = slice(c * CHUNK, (c + 1) * CHUNK)
                    kz = (k_ref[rs, cs].astype(F32) * zeta_ref[hd]).astype(BF16)
                    upd[c] = lax.dot_general(kz, v_ref[rs, cs], TN, preferred_element_type=F32)
                st = state[hd]
                outs = []
                for c in chunks:
                    rs = slice(c * CHUNK, (c + 1) * CHUNK)
                    q, k, v = q_ref[rs, cs], k_ref[rs, cs], v_ref[rs, cs]
                    st_bf = st.astype(BF16)
                    st_ref[c, hd] = st_bf
                    s = lax.dot_general(q, k, NT, preferred_element_type=F32) * dec_ref[hd]
                    inner = jnp.dot(s.astype(BF16), v, preferred_element_type=F32)
                    qx = (q.astype(F32) * xi_ref[hd]).astype(BF16)
                    outs.append(inner + jnp.dot(qx, st_bf, preferred_element_type=F32))
                    st = CHUNK_DECAY[hd] * st + upd[c]
                state[hd] = st
                heads.append(jnp.concatenate(outs, axis=0) if len(outs) > 1 else outs[0])
            o = jnp.concatenate(heads, axis=1)
            o_ref[r0:tm, :] = o.astype(BF16)
            yh, _ = _group_norm(o)
            rg = rg_ref[r0:tm, :].astype(F32)
            return (yh * gr_ref[...] * (rg * _sigmoid(rg))).astype(BF16)

        @pl.when(i == 0)
        def _():
            state[...] = jnp.zeros_like(state)
            acc_ref[...] = jnp.zeros_like(acc_ref)
            gacc[...] = jnp.zeros_like(gacc)
            mixer(LIVE0)
            dh2_ref[...] = jnp.zeros_like(dh2_ref)

        @pl.when(i > 0)
        def _():
            ro = mixer(0)
            h2 = (
                x_ref[...]
                + jnp.dot(co_ref[...], wo_ref[0:D_CONV], preferred_element_type=F32)
                + jnp.dot(ro, wo_ref[D_CONV:], preferred_element_type=F32)
            )
            r2 = lax.rsqrt(jnp.mean(h2 * h2, axis=-1, keepdims=True) + EPS)
            yn = h2 * r2
            gfv = gf_ref[...]
            err = yn * gfv - t_ref[...]
            tile_loss = jnp.sum(jnp.sum(err * err, axis=-1, keepdims=True), axis=0, keepdims=True) * (0.5 / D_MODEL)
            acc_ref[0:1, :] += jnp.sum(err * yn, axis=0, keepdims=True) * (1.0 / D_MODEL)
            acc_ref[1:2, :] += tile_loss
            dyn = err * (gfv * (1.0 / D_MODEL))
            dh2 = r2 * (dyn - yn * jnp.mean(dyn * yn, axis=-1, keepdims=True))
            dh2_ref[...] = dh2
            dh2_bf = dh2.astype(BF16)
            gacc[0:D_CONV, :] += lax.dot_general(co_ref[...], dh2_bf, TN, preferred_element_type=F32)
            gacc[D_CONV:, :] += lax.dot_general(ro, dh2_bf, TN, preferred_element_type=F32)

        @pl.when(i == nt - 1)
        def _():
            gw_ref[...] = gacc[...].astype(BF16)

    row = lambda w, j=0: pl.BlockSpec((tm, w), lambda i: (i, j))
    tok = pl.BlockSpec((tm, D_MODEL), lambda i: (_token_tile(i), 0))
    return pl.pallas_call(
        body,
        name="f2_retention_out",
        grid=(nt,),
        in_specs=[row(BLK, 4), row(BLK, 5), row(BLK, 6), row(BLK, 7), row(D_CONV), tok, tok] + [_VMEM] * 6,
        out_specs=(
            row(D_MODEL),
            row(D_RET),
            pl.BlockSpec((nct, N_HEADS, HEAD_DIM, HEAD_DIM), lambda i: (i, 0, 0, 0)),
            pl.BlockSpec((D_MODEL, D_MODEL), lambda i: (0, 0)),
            pl.BlockSpec((8, D_MODEL), lambda i: (0, 0)),
        ),
        out_shape=(
            jax.ShapeDtypeStruct((n_rows, D_MODEL), F32),
            jax.ShapeDtypeStruct((n_rows, D_RET), BF16),
            jax.ShapeDtypeStruct((n_rows // CHUNK, N_HEADS, HEAD_DIM, HEAD_DIM), BF16),
            jax.ShapeDtypeStruct((D_MODEL, D_MODEL), BF16),
            jax.ShapeDtypeStruct((8, D_MODEL), F32),
        ),
        scratch_shapes=[
            pltpu.VMEM((N_HEADS, HEAD_DIM, HEAD_DIM), F32),
            pltpu.VMEM((D_MODEL, D_MODEL), F32),
        ],
        compiler_params=_params(),
    )(proj, proj, proj, proj, conv_out, x2, t2, wout, gr, gf, dec, zeta, xi)


def _b1(dh2, proj, o, states, wout, gr, tables, zeta0, rot):
    n_rows = dh2.shape[0]
    tm = ROW_TILE
    nt = n_rows // tm
    chunk = B1_CHUNK
    nct = tm // chunk
    decay = tuple(math.exp(chunk * lg) for lg in LOG_G)

    def body(dh2_ref, pr_ref, o_ref, st_ref, wo_ref, gr_ref, dec_ref, dect_ref, zeta_ref, xi_ref, zeta0_ref,
             ca_ref, sa_ref, cb_ref, sb_ref, dp_ref, dco_ref, acc_ref, dstate):
        i = pl.program_id(0)
        tile = nt - 1 - i
        cos_all, sin_all = _tile_rotary(ca_ref, sa_ref, cb_ref, sb_ref, tile)

        @pl.when(i == 0)
        def _():
            dstate[...] = jnp.zeros_like(dstate)
            acc_ref[...] = jnp.zeros_like(acc_ref)

        @pl.when(tile > 0)
        def _():
            dmix = lax.dot_general(dh2_ref[...].astype(BF16), wo_ref[...], NT, preferred_element_type=F32)
            dco_ref[...] = dmix[:, :D_CONV].astype(BF16)
            dro = dmix[:, D_CONV:]
            rg = pr_ref[:, 3 * BLK : 4 * BLK].astype(F32)
            sg = _sigmoid(rg)
            silu, dsilu = rg * sg, sg * (1.0 + rg * (1.0 - sg))
            yh, rstd = _group_norm(o_ref[...].astype(F32))
            grv = gr_ref[...]
            dp_ref[:, 3 * BLK : 4 * BLK] = (dro * (yh * grv) * dsilu).astype(BF16)
            dret = dro * silu
            acc_ref[0:1, :] += jnp.sum(dret * yh, axis=0, keepdims=True)
            dyh = dret * grv
            for hd in range(N_HEADS):
                cs = slice(hd * HEAD_DIM, (hd + 1) * HEAD_DIM)
                a, b = dyh[:, cs], yh[:, cs]
                do_head = (
                    rstd[:, cs]
                    * (a - jnp.mean(a, axis=-1, keepdims=True) - b * jnp.mean(a * b, axis=-1, keepdims=True))
                ).astype(BF16)
                dupd = {}
                for c in range(nct):
                    rs = slice(c * chunk, (c + 1) * chunk)
                    qx = (pr_ref[rs, cs].astype(F32) * xi_ref[hd]).astype(BF16)
                    dupd[c] = lax.dot_general(qx, do_head[rs, :], TN, preferred_element_type=F32)
                dst = dstate[hd]
                for c in reversed(range(nct)):
                    rs = slice(c * chunk, (c + 1) * chunk)
                    cos_t, sin_t = cos_all[rs, :], sin_all[rs, :]
                    q = pr_ref[rs, hd * HEAD_DIM : (hd + 1) * HEAD_DIM]
                    k = pr_ref[rs, BLK + hd * HEAD_DIM : BLK + (hd + 1) * HEAD_DIM]
                    v = pr_ref[rs, 2 * BLK + hd * HEAD_DIM : 2 * BLK + (hd + 1) * HEAD_DIM]
                    do = do_head[rs, :]
                    st_bf = st_ref[c * (chunk // CHUNK), hd]
                    dst_bf = dst.astype(BF16)
                    zt, xt = zeta_ref[hd], xi_ref[hd]
                    sT = (lax.dot_general(k, q, NT, preferred_element_type=F32) * dect_ref[hd]).astype(BF16)
                    dsT = (lax.dot_general(v, do, NT, preferred_element_type=F32) * dect_ref[hd]).astype(BF16)
                    ds = (lax.dot_general(do, v, NT, preferred_element_type=F32) * dec_ref[hd]).astype(BF16)
                    kz = (k.astype(F32) * zt).astype(BF16)
                    dv = jnp.dot(sT, do, preferred_element_type=F32) + jnp.dot(kz, dst_bf, preferred_element_type=F32)
                    dq = jnp.dot(ds, k, preferred_element_type=F32) + xt * lax.dot_general(
                        do, st_bf, NT, preferred_element_type=F32
                    )
                    dk = jnp.dot(dsT, q, preferred_element_type=F32) + zt * lax.dot_general(
                        v, dst_bf, NT, preferred_element_type=F32
                    )
                    dst = decay[hd] * dst + dupd[c]
                    dp_ref[rs, hd * HEAD_DIM : (hd + 1) * HEAD_DIM] = _rot_bwd(dq * Q_SCALE, cos_t, sin_t).astype(BF16)
                    dp_ref[rs, BLK + hd * HEAD_DIM : BLK + (hd + 1) * HEAD_DIM] = _rot_bwd(dk, cos_t, sin_t).astype(BF16)
                    dp_ref[rs, 2 * BLK + hd * HEAD_DIM : 2 * BLK + (hd + 1) * HEAD_DIM] = dv.astype(BF16)
                dstate[hd] = dst

        @pl.when(tile == 0)
        def _():
            rs = slice(LIVE0, tm)
            cos_t, sin_t = cos_all[rs, :], sin_all[rs, :]
            zeros = jnp.zeros((tm - LIVE0, HEAD_DIM), BF16)
            for hd in range(N_HEADS):
                k = pr_ref[rs, BLK + hd * HEAD_DIM : BLK + (hd + 1) * HEAD_DIM]
                v = pr_ref[rs, 2 * BLK + hd * HEAD_DIM : 2 * BLK + (hd + 1) * HEAD_DIM]
                dst_bf = dstate[hd].astype(BF16)
                zt = zeta0_ref[hd]
                kz = (k.astype(F32) * zt).astype(BF16)
                dv = jnp.dot(kz, dst_bf, preferred_element_type=F32)
                dk = zt * lax.dot_general(v, dst_bf, NT, preferred_element_type=F32)
                dp_ref[rs, hd * HEAD_DIM : (hd + 1) * HEAD_DIM] = zeros
                dp_ref[rs, BLK + hd * HEAD_DIM : BLK + (hd + 1) * HEAD_DIM] = _rot_bwd(dk, cos_t, sin_t).astype(BF16)
                dp_ref[rs, 2 * BLK + hd * HEAD_DIM : 2 * BLK + (hd + 1) * HEAD_DIM] = dv.astype(BF16)
                dp_ref[rs, 3 * BLK + hd * HEAD_DIM : 3 * BLK + (hd + 1) * HEAD_DIM] = zeros
            dco_ref[rs, :] = jnp.zeros((tm - LIVE0, D_CONV), BF16)

    half = N_PROJ * BLK // 2
    rev = lambda w, j=0: pl.BlockSpec((tm, w), lambda i: (nt - 1 - i, j))
    return pl.pallas_call(
        body,
        name="b1_dret",
        grid=(nt,),
        in_specs=[
            rev(D_MODEL),
            rev(half, 1),
            rev(D_RET),
            pl.BlockSpec((tm // CHUNK, N_HEADS, HEAD_DIM, HEAD_DIM), lambda i: (nt - 1 - i, 0, 0, 0)),
        ]
        + [_VMEM] * 11,
        out_specs=(rev(half, 1), rev(D_CONV), pl.BlockSpec((8, D_RET), lambda i: (0, 0))),
        out_shape=(
            jax.ShapeDtypeStruct((n_rows, N_PROJ * BLK), BF16),
            jax.ShapeDtypeStruct((n_rows, D_CONV), BF16),
            jax.ShapeDtypeStruct((8, D_RET), F32),
        ),
        scratch_shapes=[pltpu.VMEM((N_HEADS, HEAD_DIM, HEAD_DIM), F32)],
        compiler_params=_params(),
    )(dh2, proj, o, states, wout, gr, *tables, zeta0, *rot)


def _b2(dproj, dco, proj, conv, w3, cw8, x2, meta_chunk, dh2, g1):
    n_rows = dproj.shape[0]
    tm = ROW_TILE
    nt = n_rows // tm
    half = N_PROJ * BLK // 2

    def body(dpr_ref, dco_ref, pr_ref, cv_ref, w_ref, cw_ref, x_ref, mc_ref, dh2_ref, g_ref,
             dpc_ref, gx_ref, dm_ref, acc_ref, accc_ref, halo):
        i = pl.program_id(0)
        tile = nt - 1 - i

        @pl.when(i == 0)
        def _():
            acc_ref[...] = jnp.zeros_like(acc_ref)
            accc_ref[...] = jnp.zeros_like(accc_ref)
            halo[...] = jnp.zeros_like(halo)

        def work(h, r0):
            rs = slice(r0, tm)
            n = tm - r0
            dco = dco_ref[rs, :].astype(F32)
            cx = pr_ref[rs, 0 * BLK : 1 * BLK].astype(F32)
            cb = pr_ref[rs, 1 * BLK : 2 * BLK].astype(F32)
            cc = pr_ref[rs, 2 * BLK : 3 * BLK].astype(F32)
            cg = pr_ref[rs, 3 * BLK : 4 * BLK].astype(F32)
            conv = cv_ref[rs, :].astype(F32)
            sg = _sigmoid(cg)
            silu, dsilu = cg * sg, sg * (1.0 + cg * (1.0 - sg))
            t = dco * cb
            dcb = (dco * conv * silu).astype(BF16)
            dcg = (t * conv * dsilu).astype(BF16)
            dconv = t * silu
            rows = lax.broadcasted_iota(jnp.int32, dconv.shape, 0)
            hl = halo[...]
            dc1 = jnp.where(rows == n - 1, hl[0:1], pltpu.roll(dconv, n - 1, 0))
            dc2 = jnp.where(rows == n - 2, hl[0:1], jnp.where(rows == n - 1, hl[1:2], pltpu.roll(dconv, n - 2, 0)))
            halo[...] = dconv[0:8]
            cw = cw_ref[...]
            du = cw[2:3] * dconv + cw[1:2] * dc1 + cw[0:1] * dc2
            u = cc * cx
            accc_ref[0:1, :] += jnp.sum(u * dc2, axis=0, keepdims=True)
            accc_ref[1:2, :] += jnp.sum(u * dc1, axis=0, keepdims=True)
            accc_ref[2:3, :] += jnp.sum(u * dconv, axis=0, keepdims=True)
            dcx = (du * cc).astype(BF16)
            dcc = (du * cx).astype(BF16)
            dhn = lax.dot_general(dpr_ref[rs, 0:BLK], w_ref[4], NT, preferred_element_type=F32)
            for j in range(1, 4):
                dhn += lax.dot_general(dpr_ref[rs, j * BLK : (j + 1) * BLK], w_ref[4 + j], NT, preferred_element_type=F32)
            for blk, d in ((1, dcb), (3, dcg), (0, dcx), (2, dcc)):
                dpc_ref[rs, blk * BLK : (blk + 1) * BLK] = d
                dhn += lax.dot_general(d, w_ref[blk], NT, preferred_element_type=F32)
            r = lax.rsqrt(jnp.mean(h * h, axis=-1, keepdims=True) + EPS)
            hh = h * r
            acc_ref[0:1, :] += jnp.sum(dhn * hh, axis=0, keepdims=True)
            dg = dhn * g_ref[...]
            return dh2_ref[rs, :] + r * (dg - hh * jnp.mean(dg * hh, axis=-1, keepdims=True))

        @pl.when(tile == 0)
        def _():
            dh = work(mc_ref[...], LIVE0)
            dm_ref[...] = dh[CHUNK - N_META : CHUNK]

        @pl.when(tile > 0)
        def _():
            gx_ref[...] = work(x_ref[...], 0)

    rev = lambda w, j=0: pl.BlockSpec((tm, w), lambda i: (nt - 1 - i, j))
    tok = pl.BlockSpec((tm, D_MODEL), lambda i: (_token_tile(nt - 1 - i), 0))
    const = lambda r, c: pl.BlockSpec((r, c), lambda i: (0, 0))
    return pl.pallas_call(
        body,
        name="b2_dconv_dh",
        grid=(nt,),
        in_specs=[rev(half, 1), rev(D_CONV), rev(half, 0), rev(D_CONV), _VMEM, _VMEM, tok, _VMEM, rev(D_MODEL), _VMEM],
        out_specs=(rev(half, 0), tok, const(N_META, D_MODEL), const(8, D_MODEL), const(8, D_CONV)),
        out_shape=(
            jax.ShapeDtypeStruct(dproj.shape, BF16),
            jax.ShapeDtypeStruct(x2.shape, F32),
            jax.ShapeDtypeStruct((N_META, D_MODEL), F32),
            jax.ShapeDtypeStruct((8, D_MODEL), F32),
            jax.ShapeDtypeStruct((8, D_CONV), F32),
        ),
        scratch_shapes=[pltpu.VMEM((8, D_CONV), F32)],
        input_output_aliases={0: 0},
        compiler_params=_params(),
    )(dproj, dco, proj, conv, w3, cw8, x2, meta_chunk, dh2, g1)


def _gw_in_scatter(hnT_tok, hnT_meta, dproj, gw_out_parts, sc_parts, ga_part, me_arr):
    n_rows = dproj.shape[0]
    last = N_DEV - 1
    by_dest = (True, True, False)

    def body(me_ref, a_ref, am_ref, b_ref, go_ref, sc_ref, ga_ref, land_in, land_go, land_sc, land_ga,
             d2d_buf, d2d_land, ici_buf, d2d_send, d2d_recv, ici_send, ici_recv, send_sems, recv_sems, local_sems):
        del me_ref
        s = pl.program_id(0)
        t = last - s
        q = t >> 1
        x, y, c = lax.axis_index("x"), lax.axis_index("y"), lax.axis_index("c")
        me = 4 * x + 2 * y + c
        chip = 2 * x + y
        srcs, lands = (go_ref, sc_ref, ga_ref), (land_go, land_sc, land_ga)

        def peer_at(k):
            return (1 - x if k & 4 else x, 1 - y if k & 2 else y, 1 - c if k & 1 else c)

        def small_copy(a, k):
            px, py, pc = peer_at(k)
            return pltpu.make_async_remote_copy(
                src_ref=srcs[a].at[4 * px + 2 * py + pc] if by_dest[a] else srcs[a],
                dst_ref=lands[a].at[me],
                send_sem=send_sems.at[a * last + k - 1],
                recv_sem=recv_sems.at[a * last + k - 1],
                device_id=(px, py, pc),
                device_id_type=MESH,
            )

        def small_local(a):
            return pltpu.make_async_copy(srcs[a].at[me] if by_dest[a] else srcs[a], lands[a].at[me], local_sems.at[a + 1])

        def d2d_copy(j):
            return pltpu.make_async_remote_copy(
                src_ref=d2d_buf.at[j],
                dst_ref=d2d_land.at[j],
                send_sem=d2d_send.at[j],
                recv_sem=d2d_recv.at[j],
                device_id=(x, y, 1 - c),
                device_id_type=MESH,
            )

        def ici_copy(j, to):
            return pltpu.make_async_remote_copy(
                src_ref=ici_buf.at[j],
                dst_ref=land_in.at[chip],
                send_sem=ici_send.at[j],
                recv_sem=ici_recv.at[j],
                device_id=to,
                device_id_type=MESH,
            )

        def own_copy():
            return pltpu.make_async_copy(ici_buf.at[0], land_in.at[chip], local_sems.at[0])

        @pl.when(s == 0)
        def _():
            for a in range(3):
                small_local(a).start()
            for k in range(last, 0, -1):
                for a in range(3):
                    small_copy(a, k).start()

        blk = jnp.dot(a_ref[...], b_ref[ROW_TILE:, :], preferred_element_type=F32)
        blk += jnp.dot(am_ref[...], b_ref[LIVE0:ROW_TILE, :], preferred_element_type=F32)

        @pl.when((t & 1) == 1)
        def _():
            d2d_buf[q] = blk.astype(BF16)
            d2d_copy(q).start()

        @pl.when((t & 1) == 0)
        def _():
            d2d_copy(q).wait_recv()
            ici_buf[q] = (blk + d2d_land[q].astype(F32)).astype(BF16)

            @pl.when(t != 0)
            def _():
                ici_copy(q, (jnp.bitwise_xor(x, (t >> 2) & 1), jnp.bitwise_xor(y, (t >> 1) & 1), c)).start()

            @pl.when(t == 0)
            def _():
                own_copy().start()
                for j in range(N_CHIPS):
                    d2d_copy(j).wait_send()
                for j in range(N_CHIPS - 1, 0, -1):
                    ici_copy(j, peer_at(2 * j)).wait()
                for k in range(last, 0, -1):
                    for a in range(3):
                        small_copy(a, k).wait()
                for a in range(3):
                    small_local(a).wait()
                own_copy().wait()

    smalls = (gw_out_parts, sc_parts, ga_part)
    small_blocks = [p.shape[1:] if d else p.shape for p, d in zip(smalls, by_dest, strict=True)]
    grid_spec = pltpu.PrefetchScalarGridSpec(
        num_scalar_prefetch=1,
        grid=(N_DEV,),
        in_specs=[_VMEM, _VMEM, pl.BlockSpec((n_rows, BLK), lambda s, me: (0, jnp.bitwise_xor(me[0], last - s)))]
        + [_HBM] * 3,
        out_specs=tuple([_HBM] * 4),
        scratch_shapes=[
            pltpu.VMEM((N_CHIPS, D_MODEL, BLK), BF16),
            pltpu.VMEM((N_CHIPS, D_MODEL, BLK), BF16),
            pltpu.VMEM((N_CHIPS, D_MODEL, BLK), BF16),
            pltpu.SemaphoreType.DMA((N_CHIPS,)),
            pltpu.SemaphoreType.DMA((N_CHIPS,)),
            pltpu.SemaphoreType.DMA((N_CHIPS,)),
            pltpu.SemaphoreType.DMA((N_CHIPS,)),
            pltpu.SemaphoreType.DMA((3 * last,)),
            pltpu.SemaphoreType.DMA((3 * last,)),
            pltpu.SemaphoreType.DMA((4,)),
        ],
    )
    return pl.pallas_call(
        body,
        name="gw_in_scatter",
        grid_spec=grid_spec,
        out_shape=(jax.ShapeDtypeStruct((N_CHIPS, D_MODEL, BLK), BF16),)
        + tuple(jax.ShapeDtypeStruct((N_DEV, *b), p.dtype) for b, p in zip(small_blocks, smalls, strict=True)),
        compiler_params=_params(),
    )(me_arr, hnT_tok, hnT_meta, dproj, gw_out_parts, sc_parts, ga_part)


def _adamw(w, g, m, v):
    m = ADAM_B1 * m + (1.0 - ADAM_B1) * g
    v = ADAM_B2 * v + (1.0 - ADAM_B2) * (g * g)
    m_hat = m / (1.0 - ADAM_B1**ADAM_STEP)
    v_hat = v / (1.0 - ADAM_B2**ADAM_STEP)
    delta = -ADAM_LR * (m_hat / (jnp.sqrt(v_hat) + ADAM_EPS) + ADAM_WD * w)
    return delta, m, v


def _sum_adamw(name, parts, w, m, v, rows_per_step):
    n_r, n_c = w.shape
    n_parts = parts.shape[0]
    tr = rows_per_step

    def body(p_ref, w_ref, m_ref, v_ref, g_ref, d_ref, nm_ref, nv_ref):
        g = p_ref[0].astype(F32)
        for s in range(1, n_parts):
            g = g + p_ref[s].astype(F32)
        g_ref[...] = g
        d_ref[...], nm_ref[...], nv_ref[...] = _adamw(w_ref[...], g, m_ref[...], v_ref[...])

    blk = pl.BlockSpec((tr, n_c), lambda i: (i, 0))
    return pl.pallas_call(
        body,
        name=name,
        grid=(n_r // tr,),
        in_specs=[pl.BlockSpec((n_parts, tr, n_c), lambda i: (0, i, 0)), blk, blk, blk],
        out_specs=(blk,) * 4,
        out_shape=(jax.ShapeDtypeStruct((n_r, n_c), F32),) * 4,
        compiler_params=_params(),
    )(parts, w, m, v)


def _small_leaves(meta, conv_w, n1, rg, fg):
    return meta, conv_w, n1.reshape(8, CHUNK), rg.reshape(4, CHUNK), fg.reshape(8, CHUNK)


def _from_small_leaves(meta, conv_w, n1, rg, fg):
    return meta, conv_w, n1.reshape(D_MODEL), rg.reshape(D_RET), fg.reshape(D_MODEL)


def _adamw_small(land_sc, land_ga, w, m, v):
    n_leaf = 5

    def body(sc_ref, ga_ref, *refs):
        ins, outs = refs[: 3 * n_leaf], refs[3 * n_leaf :]
        sc, ga = sc_ref[0], ga_ref[0]
        for s in range(1, N_DEV):
            sc = sc + sc_ref[s]
            ga = ga + ga_ref[s]
        grads = (
            sc[SC_META0 : SC_META0 + N_META],
            sc[SC_CONV0 : SC_CONV0 + 3, 0 : D_CONV // N_DEV],
            ga[GA_N1 : GA_N1 + 8],
            ga[GA_RG : GA_RG + 4],
            ga[GA_FG : GA_FG + 8],
        )
        for leaf, g in enumerate(grads):
            d, nm, nv = _adamw(ins[leaf][...], g, ins[n_leaf + leaf][...], ins[2 * n_leaf + leaf][...])
            outs[leaf][...] = g
            outs[n_leaf + leaf][...] = d
            outs[2 * n_leaf + leaf][...] = nm
            outs[3 * n_leaf + leaf][...] = nv
        outs[4 * n_leaf][...] = ga[GA_LOSS : GA_LOSS + 1]

    leaf_shapes = tuple(jax.ShapeDtypeStruct(a.shape, F32) for a in w)
    out = pl.pallas_call(
        body,
        name="adamw_small",
        out_shape=leaf_shapes * 4 + (jax.ShapeDtypeStruct((1, CHUNK), F32),),
    )(land_sc, land_ga, *w, *m, *v)
    return tuple(out[k * n_leaf : (k + 1) * n_leaf] for k in range(4)), out[4 * n_leaf]


def kernel(x, meta, norm1_g, w_in, conv_w, ret_norm_g, w_out, final_g, loss_target, m_meta, m_norm1_g, m_w_in, m_conv_w, m_ret_norm_g, m_w_out, m_final_g, v_meta, v_norm1_g, v_w_in, v_conv_w, v_ret_norm_g, v_w_out, v_final_g):
    seq = x.shape[1]
    assert x.shape == (1, seq, D_MODEL) and seq % ROW_TILE == 0
    n_tiles = seq // ROW_TILE + 1
    cols = D_CONV // N_DEV
    x2, t2 = x[0], loss_target[0]

    small_w = _small_leaves(meta, conv_w, norm1_g, ret_norm_g, final_g)
    conv8 = jnp.pad(conv_w, ((0, 5), (0, CHUNK - cols)))
    g1 = norm1_g.reshape(1, D_MODEL)
    hn_tok, hnT_tok, w3, small = _gather_weights_and_norm(x2, g1, w_in, jnp.concatenate([meta, conv8], axis=0))
    meta_full = small[:, SC_META0 : SC_META0 + N_META, :].transpose(1, 0, 2).reshape(N_META, D_MODEL)
    conv_full = small[:, SC_CONV0 : SC_CONV0 + 3, :cols].transpose(1, 0, 2).reshape(3, D_CONV)
    cw8 = jnp.pad(conv_full, ((0, 5), (0, 0)))
    meta_chunk = jnp.pad(meta_full, ((CHUNK - N_META, 0), (0, 0)))

    rot = _rotary_tables(n_tiles, ROW_TILE)
    dec, dect, zeta, xi = _decay_tables()
    gr = ret_norm_g.reshape(1, D_RET)
    gf = final_g.reshape(1, D_MODEL)

    hnT_meta, proj, conv_out, conv, wo3 = _f1(hn_tok, meta_chunk, g1, w3, cw8, rot, w_out)
    wout = wo3.reshape(D_MODEL, D_MODEL)
    dh2, o, states, gw_out, acc_f2 = _f2(proj, conv_out, x2, t2, wout, gr, gf, dec, zeta, xi)
    dproj_ret, dco, acc_b1 = _b1(dh2, proj, o, states, wout, gr, _decay_tables(B1_CHUNK), zeta, rot)
    dproj, grad_x2, dmeta_rows, acc_b2, acc_conv = _b2(dproj_ret, dco, proj, conv, w3, cw8, x2, meta_chunk, dh2, g1)
    gw_out_parts = gw_out.reshape(N_DEV, D_MODEL // N_DEV, D_MODEL)

    dmeta = dmeta_rows.reshape(N_META, N_DEV, CHUNK).transpose(1, 0, 2)
    dconv = jnp.pad(acc_conv[0:3, :].reshape(3, N_DEV, cols).transpose(1, 0, 2), ((0, 0), (0, 5), (0, CHUNK - cols)))
    sc_parts = jnp.concatenate([dmeta, dconv], axis=1)
    ga_part = jnp.concatenate(
        [
            acc_b2[0].reshape(8, CHUNK),
            jnp.pad(acc_b1[0].reshape(4, CHUNK), ((0, 4), (0, 0))),
            acc_f2[0].reshape(8, CHUNK),
            jnp.pad(acc_f2[1:2, 0:CHUNK], ((0, 7), (0, 0))),
        ],
        axis=0,
    )

    me_arr = (4 * lax.axis_index("x") + 2 * lax.axis_index("y") + lax.axis_index("c")).astype(jnp.int32).reshape(1)
    land_in, land_out, land_sc, land_ga = _gw_in_scatter(
        hnT_tok, hnT_meta, dproj, gw_out_parts, sc_parts, ga_part, me_arr
    )

    g_w_in, d_w_in, nm_w_in, nv_w_in = _sum_adamw("adamw_w_in", land_in, w_in, m_w_in, v_w_in, 256)
    g_w_out, d_w_out, nm_w_out, nv_w_out = _sum_adamw("adamw_w_out", land_out, w_out, m_w_out, v_w_out, 64)
    small_out, loss_row = _adamw_small(
        land_sc,
        land_ga,
        small_w,
        _small_leaves(m_meta, m_conv_w, m_norm1_g, m_ret_norm_g, m_final_g),
        _small_leaves(v_meta, v_conv_w, v_norm1_g, v_ret_norm_g, v_final_g),
    )
    loss = loss_row[0, 0]
    grad_x = grad_x2.reshape(1, seq, D_MODEL)

    def leaves(w_in_leaf, w_out_leaf, small_leaves):
        meta_leaf, conv_leaf, n1_leaf, rg_leaf, fg_leaf = _from_small_leaves(*small_leaves)
        return (meta_leaf, n1_leaf, w_in_leaf, conv_leaf, rg_leaf, w_out_leaf, fg_leaf)

    return (
        loss,
        grad_x,
        *leaves(g_w_in, g_w_out, small_out[0]),
        *leaves(d_w_in, d_w_out, small_out[1]),
        *leaves(nm_w_in, nm_w_out, small_out[2]),
        *leaves(nv_w_in, nv_w_out, small_out[3]),
    )
```

```python
import math

import jax
import jax.numpy as jnp
import numpy as np
from jax import lax
from jax.experimental import pallas as pl
from jax.experimental.pallas import tpu as pltpu

F32 = jnp.float32
BF16 = jnp.bfloat16

N_DEV = 8
N_CHIPS = 4
D_MODEL = 1024
N_META = 16
CHUNK = 128
D_CONV = 512
D_RET = 512
N_HEADS = 4
HEAD_DIM = 128
N_PROJ = 8
BLK = 512
ROPE_BASE = 10000.0
EPS = 1e-6
Q_SCALE = HEAD_DIM ** -0.5
LOG_G = tuple(math.log(1.0 - 2.0 ** (-5.0 - h)) for h in range(N_HEADS))
CHUNK_DECAY = tuple(math.exp(CHUNK * lg) for lg in LOG_G)

ADAM_LR = 0.001
ADAM_B1 = 0.9
ADAM_B2 = 0.999
ADAM_EPS = 1e-08
ADAM_WD = 0.01
ADAM_STEP = 10

ROW_TILE = 512
PAD_ROWS = ROW_TILE - N_META
LIVE0 = ROW_TILE - CHUNK
B1_CHUNK = 256
VMEM_LIMIT = 56 * 1024 * 1024

SC_META0, SC_CONV0 = 0, 16
GA_N1, GA_RG, GA_FG, GA_LOSS = 0, 8, 16, 24

NT = (((1,), (1,)), ((), ()))
TN = (((0,), (0,)), ((), ()))
MESH = pl.DeviceIdType.MESH

_VMEM = pl.BlockSpec(memory_space=pltpu.VMEM)
_HBM = pl.BlockSpec(memory_space=pltpu.HBM)


def _params(n_axes=1):
    return pltpu.CompilerParams(dimension_semantics=("arbitrary",) * n_axes, vmem_limit_bytes=VMEM_LIMIT)


def _sigmoid(x):
    return 0.5 * jnp.tanh(0.5 * x) + 0.5


def _decay_tables(chunk=CHUNK):
    idx = np.arange(chunk, dtype=np.float64)
    diff = idx[:, None] - idx[None, :]
    dec = np.stack([np.where(diff >= 0, np.exp(diff * lg), 0.0) for lg in LOG_G])
    zeta = np.stack([np.exp((chunk - 1 - idx) * lg) for lg in LOG_G])
    xi = np.stack([np.exp((idx + 1.0) * lg) for lg in LOG_G])
    ones = np.ones((1, 1, HEAD_DIM))
    return (
        jnp.asarray(dec, F32),
        jnp.asarray(dec.transpose(0, 2, 1), F32),
        jnp.asarray(zeta[:, :, None] * ones, F32),
        jnp.asarray(xi[:, :, None] * ones, F32),
    )


def _rotary_tables(n_tiles, tm):
    half = HEAD_DIM // 2
    freqs = (1.0 / (np.float32(ROPE_BASE) ** (np.arange(half, dtype=np.float32) / np.float32(half)))).astype(np.float64)
    sign = np.concatenate([-np.ones(half), np.ones(half)])
    two = lambda a: np.concatenate([a, a], axis=1)
    base = two((np.arange(n_tiles, dtype=np.float64) * tm - PAD_ROWS)[:, None] * freqs[None, :])
    off = two(np.arange(tm, dtype=np.float64)[:, None] * freqs[None, :])
    as32 = lambda a: jnp.asarray(a, F32)
    return as32(np.cos(base)), as32(np.sin(base) * sign), as32(np.cos(off)), as32(np.sin(off) * sign)


def _tile_rotary(ca_ref, sa_ref, cb_ref, sb_ref, tile):
    ca, sa = ca_ref[pl.ds(tile, 1), :], sa_ref[pl.ds(tile, 1), :]
    cb, sb = cb_ref[...], sb_ref[...]
    return ca * cb - sa * sb, sa * cb + ca * sb


def _rot(t, cos2, sin2):
    return t * cos2 + pltpu.roll(t, HEAD_DIM // 2, 1) * sin2


def _rot_bwd(d, cos2, sin2):
    return d * cos2 + pltpu.roll(d * sin2, HEAD_DIM // 2, 1)


def _token_tile(i):
    return jnp.maximum(i - 1, 0)


GATHER_PAIRS = 7


def _two_level_gather(src_ref, out_ref, send_sems, recv_sems, local_sem, base=0):
    x, y, c = lax.axis_index("x"), lax.axis_index("y"), lax.axis_index("c")
    me, sibling = (x, y, c), (x, y, 1 - c)
    xnb, ynb, diag = (1 - x, y), (x, 1 - y), (1 - x, 1 - y)
    relayed = (jnp.bitwise_xor(x, 1 - c), jnp.bitwise_xor(y, c))
    other = (jnp.bitwise_xor(x, c), jnp.bitwise_xor(y, 1 - c))

    def copy(k, block, to, src=None):
        dst = out_ref.at[4 * block[0] + 2 * block[1] + block[2]]
        return pltpu.make_async_remote_copy(
            src_ref=dst if src is None else src,
            dst_ref=dst,
            send_sem=send_sems.at[base + k],
            recv_sem=recv_sems.at[base + k],
            device_id=to,
            device_id_type=MESH,
        )

    mine = pltpu.make_async_copy(src_ref, out_ref.at[4 * x + 2 * y + c], local_sem)
    first = [copy(1, me, (*xnb, c), src=src_ref), copy(2, me, (*ynb, c), src=src_ref), copy(0, me, sibling, src=src_ref)]
    relay = copy(3, (*relayed, c), (*other, c))
    passed = [copy(4 + j, (*chip, c), sibling) for j, chip in enumerate((xnb, ynb, diag))]

    def start():
        mine.start()
        for cp in first:
            cp.start()

    def forward():
        copy(1 + c, (*relayed, c), me).wait_recv()
        relay.start()
        copy(2 - c, (*other, c), me).wait_recv()
        passed[0].start()
        passed[1].start()

    def forward_relayed():
        copy(3, (*diag, c), me).wait_recv()
        passed[2].start()

    def finish():
        copy(0, sibling, me).wait_recv()
        for j, chip in enumerate((xnb, ynb, diag)):
            copy(4 + j, (*chip, 1 - c), me).wait_recv()
        for cp in first + [relay] + passed:
            cp.wait_send()
        mine.wait()

    return start, forward, forward_relayed, finish


W_IN_CHUNKS = 4


def _gather_weights_and_norm(x2, g1, w_shard, meta, conv_w):
    per = GATHER_PAIRS
    n_ch = W_IN_CHUNKS
    rows = w_shard.shape[0] // n_ch
    tm = ROW_TILE
    nt = x2.shape[0] // tm
    cols = D_CONV // N_DEV
    pack_shape = (SC_CONV0 + 8, CHUNK)
    starts = [0] * (n_ch + 1)
    forwards = [min(5 + 2 * p, nt - 1) for p in range(n_ch)] + [min(12, nt - 1)]
    relayeds = [min(12 + p, nt - 1) for p in range(n_ch)] + [nt - 1]
    finishes = [nt - 1] * (n_ch + 1)

    def body(x_ref, g_ref, w_ref, meta_ref, conv_ref, hn_ref, hnT_ref, w_all_ref, mc_ref, cw_ref,
             w_bf, small_ref, small_all_ref, send_sems, recv_sems, local_sems):
        i = pl.program_id(0)

        @pl.when(i == 0)
        def _():
            w_bf[...] = w_ref[...].astype(BF16)
            small_ref[...] = jnp.zeros_like(small_ref)
            small_ref[SC_META0 : SC_META0 + N_META, :] = meta_ref[...]
            small_ref[SC_CONV0 : SC_CONV0 + 3, 0:cols] = conv_ref[...]

        parts = [
            _two_level_gather(
                w_bf.at[pl.ds(ch * rows, rows)],
                w_all_ref.at[:, pl.ds(ch * rows, rows)],
                send_sems,
                recv_sems,
                local_sems.at[ch],
                ch * per,
            )
            for ch in range(n_ch)
        ]
        parts.append(_two_level_gather(small_ref, small_all_ref, send_sems, recv_sems, local_sems.at[n_ch], n_ch * per))
        for phase, steps in enumerate((starts, forwards, relayeds, finishes)):
            for part, step in zip(parts, steps, strict=True):
                pl.when(i == step)(part[phase])

        h = x_ref[...]
        r = lax.rsqrt(jnp.mean(h * h, axis=-1, keepdims=True) + EPS)
        hn = (h * r * g_ref[...]).astype(BF16)
        hn_ref[...] = hn
        hnT_ref[...] = hn.T

        @pl.when(i == nt - 1)
        def _():
            mc_ref[0 : CHUNK - N_META, :] = jnp.zeros((CHUNK - N_META, D_MODEL), F32)
            cw_ref[...] = jnp.zeros_like(cw_ref)
            for d in range(N_DEV):
                mc_ref[CHUNK - N_META : CHUNK, d * CHUNK : (d + 1) * CHUNK] = small_all_ref[d, SC_META0 : SC_META0 + N_META, :]
                lane0 = (d * cols) % CHUNK
                rows8 = small_all_ref[d, SC_CONV0 : SC_CONV0 + 8, :]
                rows8 = pltpu.roll(rows8, lane0, 1) if lane0 else rows8
                cw_ref[0:3, d * cols : (d + 1) * cols] = rows8[0:3, lane0 : lane0 + cols]

    const = lambda r, c: pl.BlockSpec((r, c), lambda i: (0, 0))
    return pl.pallas_call(
        body,
        name="gather_weights_norm",
        grid=(nt,),
        out_shape=(
            jax.ShapeDtypeStruct(x2.shape, BF16),
            jax.ShapeDtypeStruct(x2.shape[::-1], BF16),
            jax.ShapeDtypeStruct((N_DEV, *w_shard.shape), BF16),
            jax.ShapeDtypeStruct((CHUNK, D_MODEL), F32),
            jax.ShapeDtypeStruct((8, D_CONV), F32),
        ),
        in_specs=[pl.BlockSpec((tm, D_MODEL), lambda i: (i, 0)), _VMEM, _VMEM, _VMEM, _VMEM],
        out_specs=(
            pl.BlockSpec((tm, D_MODEL), lambda i: (i, 0)),
            pl.BlockSpec((D_MODEL, tm), lambda i: (0, i)),
            _HBM,
            const(CHUNK, D_MODEL),
            const(8, D_CONV),
        ),
        scratch_shapes=[
            pltpu.VMEM(w_shard.shape, BF16),
            pltpu.VMEM(pack_shape, F32),
            pltpu.VMEM((N_DEV, *pack_shape), F32),
            pltpu.SemaphoreType.DMA(((n_ch + 1) * per,)),
            pltpu.SemaphoreType.DMA(((n_ch + 1) * per,)),
            pltpu.SemaphoreType.DMA((n_ch + 1,)),
        ],
        compiler_params=_params(),
    )(x2, g1, w_shard, meta, conv_w)


def _f1(hn_tok, meta_chunk, g1, w3, cw8, rot, wo_shard):
    tm = ROW_TILE
    nt = hn_tok.shape[0] // tm + 1
    n_rows = nt * tm

    def body(hn_ref, mc_ref, g_ref, w_ref, cw_ref, ca_ref, sa_ref, cb_ref, sb_ref, wo_ref,
             hnT_ref, pr_ref, co_ref, cv_ref, wo_all_ref, halo, wo_bf, send_sems, recv_sems, local_sem):
        i = pl.program_id(0)

        @pl.when(i == 0)
        def _():
            wo_bf[...] = wo_ref[...].astype(BF16)

        wo_phases = _two_level_gather(wo_bf, wo_all_ref, send_sems, recv_sems, local_sem)
        for step, phase in zip((0, nt // 3, 2 * nt // 3, nt - 1), wo_phases, strict=True):
            pl.when(i == step)(phase)

        def work(hn, r0):
            rs = slice(r0, tm)
            n = tm - r0

            def proj(j):
                return jnp.dot(hn, w_ref[j], preferred_element_type=F32)

            cx, cb, cc, cg = proj(0), proj(1), proj(2), proj(3)
            u = cc * cx
            rows = lax.broadcasted_iota(jnp.int32, u.shape, 0)
            hl = halo[...]
            u1 = jnp.where(rows == 0, hl[7:8], pltpu.roll(u, 1, 0))
            u2 = jnp.where(rows == 0, hl[6:7], jnp.where(rows == 1, hl[7:8], pltpu.roll(u, 2, 0)))
            halo[...] = u[n - 8 : n]
            cw = cw_ref[...]
            conv = cw[0:1] * u2 + cw[1:2] * u1 + cw[2:3] * u
            co = (cb * conv * (cg * _sigmoid(cg))).astype(BF16)
            co_ref[rs, :] = co
            cv_ref[rs, :] = conv.astype(BF16)
            pr_ref[rs, 0 * BLK : 1 * BLK] = cx.astype(BF16)
            pr_ref[rs, 1 * BLK : 2 * BLK] = cb.astype(BF16)
            pr_ref[rs, 2 * BLK : 3 * BLK] = cc.astype(BF16)
            pr_ref[rs, 3 * BLK : 4 * BLK] = cg.astype(BF16)
            cos_all, sin_all = _tile_rotary(ca_ref, sa_ref, cb_ref, sb_ref, i)
            cos_t, sin_t = cos_all[rs, :], sin_all[rs, :]
            q, k = proj(4), proj(5)
            for hd in range(N_HEADS):
                c0 = hd * HEAD_DIM
                pr_ref[rs, 4 * BLK + c0 : 4 * BLK + c0 + HEAD_DIM] = (
                    _rot(q[:, c0 : c0 + HEAD_DIM], cos_t, sin_t) * Q_SCALE
                ).astype(BF16)
                pr_ref[rs, 5 * BLK + c0 : 5 * BLK + c0 + HEAD_DIM] = _rot(
                    k[:, c0 : c0 + HEAD_DIM], cos_t, sin_t
                ).astype(BF16)
            pr_ref[rs, 6 * BLK : 7 * BLK] = proj(6).astype(BF16)
            pr_ref[rs, 7 * BLK : 8 * BLK] = proj(7).astype(BF16)

        @pl.when(i == 0)
        def _():
            halo[...] = jnp.zeros_like(halo)
            h = mc_ref[...]
            r = lax.rsqrt(jnp.mean(h * h, axis=-1, keepdims=True) + EPS)
            hn = (h * r * g_ref[...]).astype(BF16)
            hnT_ref[...] = hn.T
            work(hn, LIVE0)

        @pl.when(i > 0)
        def _():
            work(hn_ref[...], 0)

    row = lambda w: pl.BlockSpec((tm, w), lambda i: (i, 0))
    return pl.pallas_call(
        body,
        name="f1_inproj_conv",
        grid=(nt,),
        in_specs=[pl.BlockSpec((tm, D_MODEL), lambda i: (_token_tile(i), 0))] + [_VMEM] * 9,
        out_specs=(
            pl.BlockSpec((D_MODEL, CHUNK), lambda i: (0, 0)),
            row(N_PROJ * BLK),
            row(D_CONV),
            row(D_CONV),
            _HBM,
        ),
        out_shape=(
            jax.ShapeDtypeStruct((D_MODEL, CHUNK), BF16),
            jax.ShapeDtypeStruct((n_rows, N_PROJ * BLK), BF16),
            jax.ShapeDtypeStruct((n_rows, D_CONV), BF16),
            jax.ShapeDtypeStruct((n_rows, D_CONV), BF16),
            jax.ShapeDtypeStruct((N_DEV, *wo_shard.shape), BF16),
        ),
        scratch_shapes=[
            pltpu.VMEM((8, D_CONV), F32),
            pltpu.VMEM(wo_shard.shape, BF16),
            pltpu.SemaphoreType.DMA((GATHER_PAIRS,)),
            pltpu.SemaphoreType.DMA((GATHER_PAIRS,)),
            pltpu.SemaphoreType.DMA(()),
        ],
        compiler_params=_params(),
    )(hn_tok, meta_chunk, g1, w3, cw8, *rot, wo_shard)


def _group_norm(o):
    ys, rs = [], []
    for hd in range(N_HEADS):
        oh = o[:, hd * HEAD_DIM : (hd + 1) * HEAD_DIM]
        xc = oh - jnp.mean(oh, axis=-1, keepdims=True)
        rstd = lax.rsqrt(jnp.mean(xc * xc, axis=-1, keepdims=True) + EPS)
        ys.append(xc * rstd)
        rs.append(jnp.broadcast_to(rstd, oh.shape))
    return jnp.concatenate(ys, axis=1), jnp.concatenate(rs, axis=1)


def _f2(proj, conv_out, x2, t2, wout, gr, gf, dec, zeta, xi):
    n_rows = proj.shape[0]
    tm = ROW_TILE
    nt = n_rows // tm
    nct = tm // CHUNK

    def body(q_ref, k_ref, v_ref, rg_ref, co_ref, x_ref, t_ref, wo_ref, gr_ref, gf_ref, dec_ref, zeta_ref,
             xi_ref, dh2_ref, o_ref, st_ref, gw_ref, acc_ref, state, gacc):
        i = pl.program_id(0)

        def mixer(r0):
            chunks = range(r0 // CHUNK, nct)
            heads = []
            for hd in range(N_HEADS):
                cs = slice(hd * HEAD_DIM, (hd + 1) * HEAD_DIM)
                upd = {}
                for c in chunks:
                    rs = slice(c * CHUNK, (c + 1) * CHUNK)
                    kz = (k_ref[rs, cs].astype(F32) * zeta_ref[hd]).astype(BF16)
                    upd[c] = lax.dot_general(kz, v_ref[rs, cs], TN, preferred_element_type=F32)
                st = state[hd]
                outs = []
                for c in chunks:
                    rs = slice(c * CHUNK, (c + 1) * CHUNK)
                    q, k, v = q_ref[rs, cs], k_ref[rs, cs], v_ref[rs, cs]
                    st_bf = st.astype(BF16)
                    st_ref[c, hd] = st_bf
                    s = lax.dot_general(q, k, NT, preferred_element_type=F32) * dec_ref[hd]
                    inner = jnp.dot(s.astype(BF16), v, preferred_element_type=F32)
                    qx = (q.astype(F32) * xi_ref[hd]).astype(BF16)
                    outs.append(inner + jnp.dot(qx, st_bf, preferred_element_type=F32))
                    st = CHUNK_DECAY[hd] * st + upd[c]
                state[hd] = st
                heads.append(jnp.concatenate(outs, axis=0) if len(outs) > 1 else outs[0])
            o = jnp.concatenate(heads, axis=1)
            o_ref[r0:tm, :] = o.astype(BF16)
            yh, _ = _group_norm(o)
            rg = rg_ref[r0:tm, :].astype(F32)
            return (yh * gr_ref[...] * (rg * _sigmoid(rg))).astype(BF16)

        @pl.when(i == 0)
        def _():
            state[...] = jnp.zeros_like(state)
            acc_ref[...] = jnp.zeros_like(acc_ref)
            gacc[...] = jnp.zeros_like(gacc)
            mixer(LIVE0)
            dh2_ref[...] = jnp.zeros_like(dh2_ref)

        @pl.when(i > 0)
        def _():
            ro = mixer(0)
            h2 = (
                x_ref[...]
                + jnp.dot(co_ref[...], wo_ref[0:D_CONV], preferred_element_type=F32)
                + jnp.dot(ro, wo_ref[D_CONV:], preferred_element_type=F32)
            )
            r2 = lax.rsqrt(jnp.mean(h2 * h2, axis=-1, keepdims=True) + EPS)
            yn = h2 * r2
            gfv = gf_ref[...]
            err = yn * gfv - t_ref[...]
            tile_loss = jnp.sum(jnp.sum(err * err, axis=-1, keepdims=True), axis=0, keepdims=True) * (0.5 / D_MODEL)
            acc_ref[0:1, :] += jnp.sum(err * yn, axis=0, keepdims=True) * (1.0 / D_MODEL)
            acc_ref[1:2, :] += tile_loss
            dyn = err * (gfv * (1.0 / D_MODEL))
            dh2 = r2 * (dyn - yn * jnp.mean(dyn * yn, axis=-1, keepdims=True))
            dh2_ref[...] = dh2
            dh2_bf = dh2.astype(BF16)
            gacc[0:D_CONV, :] += lax.dot_general(co_ref[...], dh2_bf, TN, preferred_element_type=F32)
            gacc[D_CONV:, :] += lax.dot_general(ro, dh2_bf, TN, preferred_element_type=F32)

        @pl.when(i == nt - 1)
        def _():
            gw_ref[...] = gacc[...].astype(BF16)

    row = lambda w, j=0: pl.BlockSpec((tm, w), lambda i: (i, j))
    tok = pl.BlockSpec((tm, D_MODEL), lambda i: (_token_tile(i), 0))
    return pl.pallas_call(
        body,
        name="f2_retention_out",
        grid=(nt,),
        in_specs=[row(BLK, 4), row(BLK, 5), row(BLK, 6), row(BLK, 7), row(D_CONV), tok, tok] + [_VMEM] * 6,
        out_specs=(
            row(D_MODEL),
            row(D_RET),
            pl.BlockSpec((nct, N_HEADS, HEAD_DIM, HEAD_DIM), lambda i: (i, 0, 0, 0)),
            pl.BlockSpec((D_MODEL, D_MODEL), lambda i: (0, 0)),
            pl.BlockSpec((8, D_MODEL), lambda i: (0, 0)),
        ),
        out_shape=(
            jax.ShapeDtypeStruct((n_rows, D_MODEL), F32),
            jax.ShapeDtypeStruct((n_rows, D_RET), BF16),
            jax.ShapeDtypeStruct((n_rows // CHUNK, N_HEADS, HEAD_DIM, HEAD_DIM), BF16),
            jax.ShapeDtypeStruct((D_MODEL, D_MODEL), BF16),
            jax.ShapeDtypeStruct((8, D_MODEL), F32),
        ),
        scratch_shapes=[
            pltpu.VMEM((N_HEADS, HEAD_DIM, HEAD_DIM), F32),
            pltpu.VMEM((D_MODEL, D_MODEL), F32),
        ],
        compiler_params=_params(),
    )(proj, proj, proj, proj, conv_out, x2, t2, wout, gr, gf, dec, zeta, xi)


def _b1(dh2, proj, o, states, wout, gr, tables, zeta0, rot):
    n_rows = dh2.shape[0]
    tm = ROW_TILE
    nt = n_rows // tm
    chunk = B1_CHUNK
    nct = tm // chunk
    decay = tuple(math.exp(chunk * lg) for lg in LOG_G)

    def body(dh2_ref, pr_ref, o_ref, st_ref, wo_ref, gr_ref, dec_ref, dect_ref, zeta_ref, xi_ref, zeta0_ref,
             ca_ref, sa_ref, cb_ref, sb_ref, dp_ref, dco_ref, acc_ref, dstate):
        i = pl.program_id(0)
        tile = nt - 1 - i
        cos_all, sin_all = _tile_rotary(ca_ref, sa_ref, cb_ref, sb_ref, tile)

        @pl.when(i == 0)
        def _():
            dstate[...] = jnp.zeros_like(dstate)
            acc_ref[...] = jnp.zeros_like(acc_ref)

        @pl.when(tile > 0)
        def _():
            dmix = lax.dot_general(dh2_ref[...].astype(BF16), wo_ref[...], NT, preferred_element_type=F32)
            dco_ref[...] = dmix[:, :D_CONV].astype(BF16)
            dro = dmix[:, D_CONV:]
            rg = pr_ref[:, 3 * BLK : 4 * BLK].astype(F32)
            sg = _sigmoid(rg)
            silu, dsilu = rg * sg, sg * (1.0 + rg * (1.0 - sg))
            yh, rstd = _group_norm(o_ref[...].astype(F32))
            grv = gr_ref[...]
            dp_ref[:, 3 * BLK : 4 * BLK] = (dro * (yh * grv) * dsilu).astype(BF16)
            dret = dro * silu
            acc_ref[0:1, :] += jnp.sum(dret * yh, axis=0, keepdims=True)
            dyh = dret * grv
            for hd in range(N_HEADS):
                cs = slice(hd * HEAD_DIM, (hd + 1) * HEAD_DIM)
                a, b = dyh[:, cs], yh[:, cs]
                do_head = (
                    rstd[:, cs]
                    * (a - jnp.mean(a, axis=-1, keepdims=True) - b * jnp.mean(a * b, axis=-1, keepdims=True))
                ).astype(BF16)
                dupd = {}
                for c in range(nct):
                    rs = slice(c * chunk, (c + 1) * chunk)
                    qx = (pr_ref[rs, cs].astype(F32) * xi_ref[hd]).astype(BF16)
                    dupd[c] = lax.dot_general(qx, do_head[rs, :], TN, preferred_element_type=F32)
                dst = dstate[hd]
                for c in reversed(range(nct)):
                    rs = slice(c * chunk, (c + 1) * chunk)
                    cos_t, sin_t = cos_all[rs, :], sin_all[rs, :]
                    q = pr_ref[rs, hd * HEAD_DIM : (hd + 1) * HEAD_DIM]
                    k = pr_ref[rs, BLK + hd * HEAD_DIM : BLK + (hd + 1) * HEAD_DIM]
                    v = pr_ref[rs, 2 * BLK + hd * HEAD_DIM : 2 * BLK + (hd + 1) * HEAD_DIM]
                    do = do_head[rs, :]
                    st_bf = st_ref[c * (chunk // CHUNK), hd]
                    dst_bf = dst.astype(BF16)
                    zt, xt = zeta_ref[hd], xi_ref[hd]
                    sT = (lax.dot_general(k, q, NT, preferred_element_type=F32) * dect_ref[hd]).astype(BF16)
                    dsT = (lax.dot_general(v, do, NT, preferred_element_type=F32) * dect_ref[hd]).astype(BF16)
                    ds = (lax.dot_general(do, v, NT, preferred_element_type=F32) * dec_ref[hd]).astype(BF16)
                    kz = (k.astype(F32) * zt).astype(BF16)
                    dv = jnp.dot(sT, do, preferred_element_type=F32) + jnp.dot(kz, dst_bf, preferred_element_type=F32)
                    dq = jnp.dot(ds, k, preferred_element_type=F32) + xt * lax.dot_general(
                        do, st_bf, NT, preferred_element_type=F32
                    )
                    dk = jnp.dot(dsT, q, preferred_element_type=F32) + zt * lax.dot_general(
                        v, dst_bf, NT, preferred_element_type=F32
                    )
                    dst = decay[hd] * dst + dupd[c]
                    dp_ref[rs, hd * HEAD_DIM : (hd + 1) * HEAD_DIM] = _rot_bwd(dq * Q_SCALE, cos_t, sin_t).astype(BF16)
                    dp_ref[rs, BLK + hd * HEAD_DIM : BLK + (hd + 1) * HEAD_DIM] = _rot_bwd(dk, cos_t, sin_t).astype(BF16)
                    dp_ref[rs, 2 * BLK + hd * HEAD_DIM : 2 * BLK + (hd + 1) * HEAD_DIM] = dv.astype(BF16)
                dstate[hd] = dst

        @pl.when(tile == 0)
        def _():
            rs = slice(LIVE0, tm)
            cos_t, sin_t = cos_all[rs, :], sin_all[rs, :]
            zeros = jnp.zeros((tm - LIVE0, HEAD_DIM), BF16)
            for hd in range(N_HEADS):
                k = pr_ref[rs, BLK + hd * HEAD_DIM : BLK + (hd + 1) * HEAD_DIM]
                v = pr_ref[rs, 2 * BLK + hd * HEAD_DIM : 2 * BLK + (hd + 1) * HEAD_DIM]
                dst_bf = dstate[hd].astype(BF16)
                zt = zeta0_ref[hd]
                kz = (k.astype(F32) * zt).astype(BF16)
                dv = jnp.dot(kz, dst_bf, preferred_element_type=F32)
                dk = zt * lax.dot_general(v, dst_bf, NT, preferred_element_type=F32)
                dp_ref[rs, hd * HEAD_DIM : (hd + 1) * HEAD_DIM] = zeros
                dp_ref[rs, BLK + hd * HEAD_DIM : BLK + (hd + 1) * HEAD_DIM] = _rot_bwd(dk, cos_t, sin_t).astype(BF16)
                dp_ref[rs, 2 * BLK + hd * HEAD_DIM : 2 * BLK + (hd + 1) * HEAD_DIM] = dv.astype(BF16)
                dp_ref[rs, 3 * BLK + hd * HEAD_DIM : 3 * BLK + (hd + 1) * HEAD_DIM] = zeros
            dco_ref[rs, :] = jnp.zeros((tm - LIVE0, D_CONV), BF16)

    half = N_PROJ * BLK // 2
    rev = lambda w, j=0: pl.BlockSpec((tm, w), lambda i: (nt - 1 - i, j))
    return pl.pallas_call(
        body,
        name="b1_dret",
        grid=(nt,),
        in_specs=[
            rev(D_MODEL),
            rev(half, 1),
            rev(D_RET),
            pl.BlockSpec((tm // CHUNK, N_HEADS, HEAD_DIM, HEAD_DIM), lambda i: (nt - 1 - i, 0, 0, 0)),
        ]
        + [_VMEM] * 11,
        out_specs=(rev(half, 1), rev(D_CONV), pl.BlockSpec((8, D_RET), lambda i: (0, 0))),
        out_shape=(
            jax.ShapeDtypeStruct((n_rows, N_PROJ * BLK), BF16),
            jax.ShapeDtypeStruct((n_rows, D_CONV), BF16),
            jax.ShapeDtypeStruct((8, D_RET), F32),
        ),
        scratch_shapes=[pltpu.VMEM((N_HEADS, HEAD_DIM, HEAD_DIM), F32)],
        compiler_params=_params(),
    )(dh2, proj, o, states, wout, gr, *tables, zeta0, *rot)


def _b2(dproj, dco, proj, conv, w3, cw8, x2, meta_chunk, dh2, g1):
    n_rows = dproj.shape[0]
    tm = ROW_TILE
    nt = n_rows // tm
    half = N_PROJ * BLK // 2

    def body(dpr_ref, dco_ref, pr_ref, cv_ref, w_ref, cw_ref, x_ref, mc_ref, dh2_ref, g_ref,
             dpc_ref, gx_ref, dm_ref, acc_ref, accc_ref, halo):
        i = pl.program_id(0)
        tile = nt - 1 - i

        @pl.when(i == 0)
        def _():
            acc_ref[...] = jnp.zeros_like(acc_ref)
            accc_ref[...] = jnp.zeros_like(accc_ref)
            halo[...] = jnp.zeros_like(halo)

        def work(h, r0):
            rs = slice(r0, tm)
            n = tm - r0
            dco = dco_ref[rs, :].astype(F32)
            cx = pr_ref[rs, 0 * BLK : 1 * BLK].astype(F32)
            cb = pr_ref[rs, 1 * BLK : 2 * BLK].astype(F32)
            cc = pr_ref[rs, 2 * BLK : 3 * BLK].astype(F32)
            cg = pr_ref[rs, 3 * BLK : 4 * BLK].astype(F32)
            conv = cv_ref[rs, :].astype(F32)
            sg = _sigmoid(cg)
            silu, dsilu = cg * sg, sg * (1.0 + cg * (1.0 - sg))
            t = dco * cb
            dcb = (dco * conv * silu).astype(BF16)
            dcg = (t * conv * dsilu).astype(BF16)
            dconv = t * silu
            rows = lax.broadcasted_iota(jnp.int32, dconv.shape, 0)
            hl = halo[...]
            dc1 = jnp.where(rows == n - 1, hl[0:1], pltpu.roll(dconv, n - 1, 0))
            dc2 = jnp.where(rows == n - 2, hl[0:1], jnp.where(rows == n - 1, hl[1:2], pltpu.roll(dconv, n - 2, 0)))
            halo[...] = dconv[0:8]
            cw = cw_ref[...]
            du = cw[2:3] * dconv + cw[1:2] * dc1 + cw[0:1] * dc2
            u = cc * cx
            accc_ref[0:1, :] += jnp.sum(u * dc2, axis=0, keepdims=True)
            accc_ref[1:2, :] += jnp.sum(u * dc1, axis=0, keepdims=True)
            accc_ref[2:3, :] += jnp.sum(u * dconv, axis=0, keepdims=True)
            dcx = (du * cc).astype(BF16)
            dcc = (du * cx).astype(BF16)
            dhn = lax.dot_general(dpr_ref[rs, 0:BLK], w_ref[4], NT, preferred_element_type=F32)
            for j in range(1, 4):
                dhn += lax.dot_general(dpr_ref[rs, j * BLK : (j + 1) * BLK], w_ref[4 + j], NT, preferred_element_type=F32)
            for blk, d in ((1, dcb), (3, dcg), (0, dcx), (2, dcc)):
                dpc_ref[rs, blk * BLK : (blk + 1) * BLK] = d
                dhn += lax.dot_general(d, w_ref[blk], NT, preferred_element_type=F32)
            r = lax.rsqrt(jnp.mean(h * h, axis=-1, keepdims=True) + EPS)
            hh = h * r
            acc_ref[0:1, :] += jnp.sum(dhn * hh, axis=0, keepdims=True)
            dg = dhn * g_ref[...]
            return dh2_ref[rs, :] + r * (dg - hh * jnp.mean(dg * hh, axis=-1, keepdims=True))

        @pl.when(tile == 0)
        def _():
            dh = work(mc_ref[...], LIVE0)
            dm_ref[...] = dh[CHUNK - N_META : CHUNK]

        @pl.when(tile > 0)
        def _():
            gx_ref[...] = work(x_ref[...], 0)

    rev = lambda w, j=0: pl.BlockSpec((tm, w), lambda i: (nt - 1 - i, j))
    tok = pl.BlockSpec((tm, D_MODEL), lambda i: (_token_tile(nt - 1 - i), 0))
    const = lambda r, c: pl.BlockSpec((r, c), lambda i: (0, 0))
    return pl.pallas_call(
        body,
        name="b2_dconv_dh",
        grid=(nt,),
        in_specs=[rev(half, 1), rev(D_CONV), rev(half, 0), rev(D_CONV), _VMEM, _VMEM, tok, _VMEM, rev(D_MODEL), _VMEM],
        out_specs=(rev(half, 0), tok, const(N_META, D_MODEL), const(8, D_MODEL), const(8, D_CONV)),
        out_shape=(
            jax.ShapeDtypeStruct(dproj.shape, BF16),
            jax.ShapeDtypeStruct(x2.shape, F32),
            jax.ShapeDtypeStruct((N_META, D_MODEL), F32),
            jax.ShapeDtypeStruct((8, D_MODEL), F32),
            jax.ShapeDtypeStruct((8, D_CONV), F32),
        ),
        scratch_shapes=[pltpu.VMEM((8, D_CONV), F32)],
        input_output_aliases={0: 0},
        compiler_params=_params(),
    )(dproj, dco, proj, conv, w3, cw8, x2, meta_chunk, dh2, g1)


def _gw_in_scatter(hnT_tok, hnT_meta, dproj, gw_out_parts, dmeta_rows, acc_conv, acc_b2, acc_b1, acc_f2, me_arr):
    n_rows = dproj.shape[0]
    last = N_DEV - 1
    cols = D_CONV // N_DEV
    by_dest = (True, True, False)
    small_blocks = (gw_out_parts.shape[1:], (SC_CONV0 + 8, CHUNK), (GA_LOSS + 8, CHUNK))

    def body(me_ref, a_ref, am_ref, b_ref, go_ref, dm_ref, ac_ref, ab2_ref, ab1_ref, af2_ref,
             land_in, land_go, land_sc, land_ga, d2d_buf, d2d_land, ici_buf, sc_ref, ga_ref,
             d2d_send, d2d_recv, ici_send, ici_recv, send_sems, recv_sems, local_sems):
        del me_ref
        s = pl.program_id(0)
        t = last - s
        q = t >> 1
        x, y, c = lax.axis_index("x"), lax.axis_index("y"), lax.axis_index("c")
        me = 4 * x + 2 * y + c
        chip = 2 * x + y
        srcs, lands = (go_ref, sc_ref, ga_ref), (land_go, land_sc, land_ga)

        def peer_at(k):
            return (1 - x if k & 4 else x, 1 - y if k & 2 else y, 1 - c if k & 1 else c)

        def small_copy(a, k):
            px, py, pc = peer_at(k)
            return pltpu.make_async_remote_copy(
                src_ref=srcs[a].at[4 * px + 2 * py + pc] if by_dest[a] else srcs[a],
                dst_ref=lands[a].at[me],
                send_sem=send_sems.at[a * last + k - 1],
                recv_sem=recv_sems.at[a * last + k - 1],
                device_id=(px, py, pc),
                device_id_type=MESH,
            )

        def small_local(a):
            return pltpu.make_async_copy(srcs[a].at[me] if by_dest[a] else srcs[a], lands[a].at[me], local_sems.at[a + 1])

        def d2d_copy(j):
            return pltpu.make_async_remote_copy(
                src_ref=d2d_buf.at[j],
                dst_ref=d2d_land.at[j],
                send_sem=d2d_send.at[j],
                recv_sem=d2d_recv.at[j],
                device_id=(x, y, 1 - c),
                device_id_type=MESH,
            )

        def ici_copy(j, to):
            return pltpu.make_async_remote_copy(
                src_ref=ici_buf.at[j],
                dst_ref=land_in.at[chip],
                send_sem=ici_send.at[j],
                recv_sem=ici_recv.at[j],
                device_id=to,
                device_id_type=MESH,
            )

        def own_copy():
            return pltpu.make_async_copy(ici_buf.at[0], land_in.at[chip], local_sems.at[0])

        @pl.when(s == 0)
        def _():
            sc_ref[...] = jnp.zeros_like(sc_ref)
            for d in range(N_DEV):
                sc_ref[d, SC_META0 : SC_META0 + N_META, :] = dm_ref[:, d * CHUNK : (d + 1) * CHUNK]
                lane0 = (d * cols) % CHUNK
                rows8 = ac_ref[:, d * cols - lane0 : d * cols - lane0 + CHUNK]
                rows8 = pltpu.roll(rows8, CHUNK - lane0, 1) if lane0 else rows8
                sc_ref[d, SC_CONV0 : SC_CONV0 + 3, 0:cols] = rows8[0:3, 0:cols]
            ga_ref[...] = jnp.zeros_like(ga_ref)
            for row0, acc, n in ((GA_N1, ab2_ref, 8), (GA_RG, ab1_ref, 4), (GA_FG, af2_ref, 8)):
                for j in range(n):
                    ga_ref[row0 + j : row0 + j + 1, :] = acc[0:1, j * CHUNK : (j + 1) * CHUNK]
            ga_ref[GA_LOSS : GA_LOSS + 1, :] = af2_ref[1:2, 0:CHUNK]
            for a in range(3):
                small_local(a).start()
            for k in range(last, 0, -1):
                for a in range(3):
                    small_copy(a, k).start()

        blk = jnp.dot(a_ref[...], b_ref[ROW_TILE:, :], preferred_element_type=F32)
        blk += jnp.dot(am_ref[...], b_ref[LIVE0:ROW_TILE, :], preferred_element_type=F32)

        @pl.when((t & 1) == 1)
        def _():
            d2d_buf[q] = blk.astype(BF16)
            d2d_copy(q).start()

        @pl.when((t & 1) == 0)
        def _():
            d2d_copy(q).wait_recv()
            ici_buf[q] = (blk + d2d_land[q].astype(F32)).astype(BF16)

            @pl.when(t != 0)
            def _():
                ici_copy(q, (jnp.bitwise_xor(x, (t >> 2) & 1), jnp.bitwise_xor(y, (t >> 1) & 1), c)).start()

            @pl.when(t == 0)
            def _():
                own_copy().start()
                for j in range(N_CHIPS):
                    d2d_copy(j).wait_send()
                for j in range(N_CHIPS - 1, 0, -1):
                    ici_copy(j, peer_at(2 * j)).wait()
                for k in range(last, 0, -1):
                    for a in range(3):
                        small_copy(a, k).wait()
                for a in range(3):
                    small_local(a).wait()
                own_copy().wait()

    grid_spec = pltpu.PrefetchScalarGridSpec(
        num_scalar_prefetch=1,
        grid=(N_DEV,),
        in_specs=[_VMEM, _VMEM, pl.BlockSpec((n_rows, BLK), lambda s, me: (0, jnp.bitwise_xor(me[0], last - s))), _HBM]
        + [_VMEM] * 5,
        out_specs=tuple([_HBM] * 4),
        scratch_shapes=[
            pltpu.VMEM((N_CHIPS, D_MODEL, BLK), BF16),
            pltpu.VMEM((N_CHIPS, D_MODEL, BLK), BF16),
            pltpu.VMEM((N_CHIPS, D_MODEL, BLK), BF16),
            pltpu.VMEM((N_DEV, *small_blocks[1]), F32),
            pltpu.VMEM(small_blocks[2], F32),
            pltpu.SemaphoreType.DMA((N_CHIPS,)),
            pltpu.SemaphoreType.DMA((N_CHIPS,)),
            pltpu.SemaphoreType.DMA((N_CHIPS,)),
            pltpu.SemaphoreType.DMA((N_CHIPS,)),
            pltpu.SemaphoreType.DMA((3 * last,)),
            pltpu.SemaphoreType.DMA((3 * last,)),
            pltpu.SemaphoreType.DMA((4,)),
        ],
    )
    return pl.pallas_call(
        body,
        name="gw_in_scatter",
        grid_spec=grid_spec,
        out_shape=(jax.ShapeDtypeStruct((N_CHIPS, D_MODEL, BLK), BF16),)
        + tuple(jax.ShapeDtypeStruct((N_DEV, *b), dt) for b, dt in zip(small_blocks, (gw_out_parts.dtype, F32, F32), strict=True)),
        compiler_params=_params(),
    )(me_arr, hnT_tok, hnT_meta, dproj, gw_out_parts, dmeta_rows, acc_conv, acc_b2, acc_b1, acc_f2)


def _adamw(w, g, m, v):
    m = ADAM_B1 * m + (1.0 - ADAM_B1) * g
    v = ADAM_B2 * v + (1.0 - ADAM_B2) * (g * g)
    m_hat = m / (1.0 - ADAM_B1**ADAM_STEP)
    v_hat = v / (1.0 - ADAM_B2**ADAM_STEP)
    delta = -ADAM_LR * (m_hat / (jnp.sqrt(v_hat) + ADAM_EPS) + ADAM_WD * w)
    return delta, m, v


def _sum_adamw(name, parts, w, m, v, rows_per_step):
    n_r, n_c = w.shape
    n_parts = parts.shape[0]
    tr = rows_per_step

    def body(p_ref, w_ref, m_ref, v_ref, g_ref, d_ref, nm_ref, nv_ref):
        g = p_ref[0].astype(F32)
        for s in range(1, n_parts):
            g = g + p_ref[s].astype(F32)
        g_ref[...] = g
        d_ref[...], nm_ref[...], nv_ref[...] = _adamw(w_ref[...], g, m_ref[...], v_ref[...])

    blk = pl.BlockSpec((tr, n_c), lambda i: (i, 0))
    return pl.pallas_call(
        body,
        name=name,
        grid=(n_r // tr,),
        in_specs=[pl.BlockSpec((n_parts, tr, n_c), lambda i: (0, i, 0)), blk, blk, blk],
        out_specs=(blk,) * 4,
        out_shape=(jax.ShapeDtypeStruct((n_r, n_c), F32),) * 4,
        compiler_params=_params(),
    )(parts, w, m, v)


def _small_leaves(meta, conv_w, n1, rg, fg):
    return meta, conv_w, n1.reshape(8, CHUNK), rg.reshape(4, CHUNK), fg.reshape(8, CHUNK)


def _from_small_leaves(meta, conv_w, n1, rg, fg):
    return meta, conv_w, n1.reshape(D_MODEL), rg.reshape(D_RET), fg.reshape(D_MODEL)


def _adamw_small(land_sc, land_ga, w, m, v):
    n_leaf = 5

    def body(sc_ref, ga_ref, *refs):
        ins, outs = refs[: 3 * n_leaf], refs[3 * n_leaf :]
        sc, ga = sc_ref[0], ga_ref[0]
        for s in range(1, N_DEV):
            sc = sc + sc_ref[s]
            ga = ga + ga_ref[s]
        grads = (
            sc[SC_META0 : SC_META0 + N_META],
            sc[SC_CONV0 : SC_CONV0 + 3, 0 : D_CONV // N_DEV],
            ga[GA_N1 : GA_N1 + 8],
            ga[GA_RG : GA_RG + 4],
            ga[GA_FG : GA_FG + 8],
        )
        for leaf, g in enumerate(grads):
            d, nm, nv = _adamw(ins[leaf][...], g, ins[n_leaf + leaf][...], ins[2 * n_leaf + leaf][...])
            outs[leaf][...] = g
            outs[n_leaf + leaf][...] = d
            outs[2 * n_leaf + leaf][...] = nm
            outs[3 * n_leaf + leaf][...] = nv
        outs[4 * n_leaf][...] = ga[GA_LOSS : GA_LOSS + 1]

    leaf_shapes = tuple(jax.ShapeDtypeStruct(a.shape, F32) for a in w)
    out = pl.pallas_call(
        body,
        name="adamw_small",
        out_shape=leaf_shapes * 4 + (jax.ShapeDtypeStruct((1, CHUNK), F32),),
    )(land_sc, land_ga, *w, *m, *v)
    return tuple(out[k * n_leaf : (k + 1) * n_leaf] for k in range(4)), out[4 * n_leaf]


def kernel(x, meta, norm1_g, w_in, conv_w, ret_norm_g, w_out, final_g, loss_target, m_meta, m_norm1_g, m_w_in, m_conv_w, m_ret_norm_g, m_w_out, m_final_g, v_meta, v_norm1_g, v_w_in, v_conv_w, v_ret_norm_g, v_w_out, v_final_g):
    seq = x.shape[1]
    assert x.shape == (1, seq, D_MODEL) and seq % ROW_TILE == 0
    n_tiles = seq // ROW_TILE + 1
    x2, t2 = x[0], loss_target[0]

    small_w = _small_leaves(meta, conv_w, norm1_g, ret_norm_g, final_g)
    g1 = norm1_g.reshape(1, D_MODEL)
    hn_tok, hnT_tok, w3, meta_chunk, cw8 = _gather_weights_and_norm(x2, g1, w_in, meta, conv_w)

    rot = _rotary_tables(n_tiles, ROW_TILE)
    dec, dect, zeta, xi = _decay_tables()
    gr = ret_norm_g.reshape(1, D_RET)
    gf = final_g.reshape(1, D_MODEL)

    hnT_meta, proj, conv_out, conv, wo3 = _f1(hn_tok, meta_chunk, g1, w3, cw8, rot, w_out)
    wout = wo3.reshape(D_MODEL, D_MODEL)
    dh2, o, states, gw_out, acc_f2 = _f2(proj, conv_out, x2, t2, wout, gr, gf, dec, zeta, xi)
    dproj_ret, dco, acc_b1 = _b1(dh2, proj, o, states, wout, gr, _decay_tables(B1_CHUNK), zeta, rot)
    dproj, grad_x2, dmeta_rows, acc_b2, acc_conv = _b2(dproj_ret, dco, proj, conv, w3, cw8, x2, meta_chunk, dh2, g1)
    gw_out_parts = gw_out.reshape(N_DEV, D_MODEL // N_DEV, D_MODEL)

    me_arr = (4 * lax.axis_index("x") + 2 * lax.axis_index("y") + lax.axis_index("c")).astype(jnp.int32).reshape(1)
    land_in, land_out, land_sc, land_ga = _gw_in_scatter(
        hnT_tok, hnT_meta, dproj, gw_out_parts, dmeta_rows, acc_conv, acc_b2, acc_b1, acc_f2, me_arr
    )

    g_w_in, d_w_in, nm_w_in, nv_w_in = _sum_adamw("adamw_w_in", land_in, w_in, m_w_in, v_w_in, 256)
    g_w_out, d_w_out, nm_w_out, nv_w_out = _sum_adamw("adamw_w_out", land_out, w_out, m_w_out, v_w_out, 64)
    small_out, loss_row = _adamw_small(
        land_sc,
        land_ga,
        small_w,
        _small_leaves(m_meta, m_conv_w, m_norm1_g, m_ret_norm_g, m_final_g),
        _small_leaves(v_meta, v_conv_w, v_norm1_g, v_ret_norm_g, v_final_g),
    )
    loss = loss_row[0, 0]
    grad_x = grad_x2.reshape(1, seq, D_MODEL)

    def leaves(w_in_leaf, w_out_leaf, small_leaves):
        meta_leaf, conv_leaf, n1_leaf, rg_leaf, fg_leaf = _from_small_leaves(*small_leaves)
        return (meta_leaf, n1_leaf, w_in_leaf, conv_leaf, rg_leaf, w_out_leaf, fg_leaf)

    return (
        loss,
        grad_x,
        *leaves(g_w_in, g_w_out, small_out[0]),
        *leaves(d_w_in, d_w_out, small_out[1]),
        *leaves(nm_w_in, nm_w_out, small_out[2]),
        *leaves(nv_w_in, nv_w_out, small_out[3]),
    )
```

```python
import math

import jax
import jax.numpy as jnp
import numpy as np
from jax import lax
from jax.experimental import pallas as pl
from jax.experimental.pallas import tpu as pltpu

F32 = jnp.float32
BF16 = jnp.bfloat16

N_DEV = 8
N_CHIPS = 4
D_MODEL = 1024
N_META = 16
CHUNK = 128
D_CONV = 512
D_RET = 512
N_HEADS = 4
HEAD_DIM = 128
N_PROJ = 8
BLK = 512
ROPE_BASE = 10000.0
EPS = 1e-6
Q_SCALE = HEAD_DIM ** -0.5
LOG_G = tuple(math.log(1.0 - 2.0 ** (-5.0 - h)) for h in range(N_HEADS))
CHUNK_DECAY = tuple(math.exp(CHUNK * lg) for lg in LOG_G)

ADAM_LR = 0.001
ADAM_B1 = 0.9
ADAM_B2 = 0.999
ADAM_EPS = 1e-08
ADAM_WD = 0.01
ADAM_STEP = 10

ROW_TILE = 512
PAD_ROWS = ROW_TILE - N_META
LIVE0 = ROW_TILE - CHUNK
B1_CHUNK = 256
VMEM_LIMIT = 56 * 1024 * 1024

SC_META0, SC_CONV0 = 0, 16
GA_N1, GA_RG, GA_FG, GA_LOSS = 0, 8, 16, 24

NT = (((1,), (1,)), ((), ()))
TN = (((0,), (0,)), ((), ()))
MESH = pl.DeviceIdType.MESH

_VMEM = pl.BlockSpec(memory_space=pltpu.VMEM)
_HBM = pl.BlockSpec(memory_space=pltpu.HBM)


def _params(n_axes=1):
    return pltpu.CompilerParams(dimension_semantics=("arbitrary",) * n_axes, vmem_limit_bytes=VMEM_LIMIT)


def _sigmoid(x):
    return 0.5 * jnp.tanh(0.5 * x) + 0.5


def _decay_tables(chunk=CHUNK):
    idx = np.arange(chunk, dtype=np.float64)
    diff = idx[:, None] - idx[None, :]
    dec = np.stack([np.where(diff >= 0, np.exp(diff * lg), 0.0) for lg in LOG_G])
    zeta = np.stack([np.exp((chunk - 1 - idx) * lg) for lg in LOG_G])
    xi = np.stack([np.exp((idx + 1.0) * lg) for lg in LOG_G])
    ones = np.ones((1, 1, HEAD_DIM))
    return (
        jnp.asarray(dec, F32),
        jnp.asarray(dec.transpose(0, 2, 1), F32),
        jnp.asarray(zeta[:, :, None] * ones, F32),
        jnp.asarray(xi[:, :, None] * ones, F32),
    )


def _rotary_tables(n_tiles, tm):
    half = HEAD_DIM // 2
    freqs = (1.0 / (np.float32(ROPE_BASE) ** (np.arange(half, dtype=np.float32) / np.float32(half)))).astype(np.float64)
    sign = np.concatenate([-np.ones(half), np.ones(half)])
    two = lambda a: np.concatenate([a, a], axis=1)
    base = two((np.arange(n_tiles, dtype=np.float64) * tm - PAD_ROWS)[:, None] * freqs[None, :])
    off = two(np.arange(tm, dtype=np.float64)[:, None] * freqs[None, :])
    as32 = lambda a: jnp.asarray(a, F32)
    return as32(np.cos(base)), as32(np.sin(base) * sign), as32(np.cos(off)), as32(np.sin(off) * sign)


def _tile_rotary(ca_ref, sa_ref, cb_ref, sb_ref, tile):
    ca, sa = ca_ref[pl.ds(tile, 1), :], sa_ref[pl.ds(tile, 1), :]
    cb, sb = cb_ref[...], sb_ref[...]
    return ca * cb - sa * sb, sa * cb + ca * sb


def _rot(t, cos2, sin2):
    return t * cos2 + pltpu.roll(t, HEAD_DIM // 2, 1) * sin2


def _rot_bwd(d, cos2, sin2):
    return d * cos2 + pltpu.roll(d * sin2, HEAD_DIM // 2, 1)


def _token_tile(i):
    return jnp.maximum(i - 1, 0)


GATHER_PAIRS = 7


def _two_level_gather(src_ref, out_ref, send_sems, recv_sems, local_sem, base=0):
    x, y, c = lax.axis_index("x"), lax.axis_index("y"), lax.axis_index("c")
    me, sibling = (x, y, c), (x, y, 1 - c)
    xnb, ynb, diag = (1 - x, y), (x, 1 - y), (1 - x, 1 - y)
    relayed = (jnp.bitwise_xor(x, 1 - c), jnp.bitwise_xor(y, c))
    other = (jnp.bitwise_xor(x, c), jnp.bitwise_xor(y, 1 - c))

    def copy(k, block, to, src=None):
        dst = out_ref.at[4 * block[0] + 2 * block[1] + block[2]]
        return pltpu.make_async_remote_copy(
            src_ref=dst if src is None else src,
            dst_ref=dst,
            send_sem=send_sems.at[base + k],
            recv_sem=recv_sems.at[base + k],
            device_id=to,
            device_id_type=MESH,
        )

    mine = pltpu.make_async_copy(src_ref, out_ref.at[4 * x + 2 * y + c], local_sem)
    first = [copy(1, me, (*xnb, c), src=src_ref), copy(2, me, (*ynb, c), src=src_ref), copy(0, me, sibling, src=src_ref)]
    relay = copy(3, (*relayed, c), (*other, c))
    passed = [copy(4 + j, (*chip, c), sibling) for j, chip in enumerate((xnb, ynb, diag))]

    def start():
        mine.start()
        for cp in first:
            cp.start()

    def forward():
        copy(1 + c, (*relayed, c), me).wait_recv()
        relay.start()
        copy(2 - c, (*other, c), me).wait_recv()
        passed[0].start()
        passed[1].start()

    def forward_relayed():
        copy(3, (*diag, c), me).wait_recv()
        passed[2].start()

    def finish():
        copy(0, sibling, me).wait_recv()
        for j, chip in enumerate((xnb, ynb, diag)):
            copy(4 + j, (*chip, 1 - c), me).wait_recv()
        for cp in first + [relay] + passed:
            cp.wait_send()
        mine.wait()

    return start, forward, forward_relayed, finish


W_IN_CHUNKS = 4


def _gather_weights_and_norm(x2, g1, w_shard, meta, conv_w):
    per = GATHER_PAIRS
    n_ch = W_IN_CHUNKS
    rows = w_shard.shape[0] // n_ch
    tm = ROW_TILE
    nt = x2.shape[0] // tm
    cols = D_CONV // N_DEV
    pack_shape = (SC_CONV0 + 8, CHUNK)
    starts = [0] * (n_ch + 1)
    forwards = [min(5 + 2 * p, nt - 1) for p in range(n_ch)] + [min(12, nt - 1)]
    relayeds = [min(12 + p, nt - 1) for p in range(n_ch)] + [nt - 1]
    finishes = [nt - 1] * (n_ch + 1)

    def body(x_ref, g_ref, w_ref, meta_ref, conv_ref, hn_ref, hnT_ref, w_all_ref, mc_ref, cw_ref,
             w_bf, small_ref, small_all_ref, send_sems, recv_sems, local_sems):
        i = pl.program_id(0)

        @pl.when(i == 0)
        def _():
            w_bf[...] = w_ref[...].astype(BF16)
            small_ref[...] = jnp.zeros_like(small_ref)
            small_ref[SC_META0 : SC_META0 + N_META, :] = meta_ref[...]
            small_ref[SC_CONV0 : SC_CONV0 + 3, 0:cols] = conv_ref[...]

        parts = [
            _two_level_gather(
                w_bf.at[pl.ds(ch * rows, rows)],
                w_all_ref.at[:, pl.ds(ch * rows, rows)],
                send_sems,
                recv_sems,
                local_sems.at[ch],
                ch * per,
            )
            for ch in range(n_ch)
        ]
        parts.append(_two_level_gather(small_ref, small_all_ref, send_sems, recv_sems, local_sems.at[n_ch], n_ch * per))
        for phase, steps in enumerate((starts, forwards, relayeds, finishes)):
            for part, step in zip(parts, steps, strict=True):
                pl.when(i == step)(part[phase])

        h = x_ref[...]
        r = lax.rsqrt(jnp.mean(h * h, axis=-1, keepdims=True) + EPS)
        hn = (h * r * g_ref[...]).astype(BF16)
        hn_ref[...] = hn
        hnT_ref[...] = hn.T

        @pl.when(i == nt - 1)
        def _():
            mc_ref[0 : CHUNK - N_META, :] = jnp.zeros((CHUNK - N_META, D_MODEL), F32)
            cw_ref[...] = jnp.zeros_like(cw_ref)
            for d in range(N_DEV):
                mc_ref[CHUNK - N_META : CHUNK, d * CHUNK : (d + 1) * CHUNK] = small_all_ref[d, SC_META0 : SC_META0 + N_META, :]
                lane0 = (d * cols) % CHUNK
                rows8 = small_all_ref[d, SC_CONV0 : SC_CONV0 + 8, :]
                rows8 = pltpu.roll(rows8, lane0, 1) if lane0 else rows8
                cw_ref[0:3, d * cols : (d + 1) * cols] = rows8[0:3, lane0 : lane0 + cols]

    const = lambda r, c: pl.BlockSpec((r, c), lambda i: (0, 0))
    return pl.pallas_call(
        body,
        name="gather_weights_norm",
        grid=(nt,),
        out_shape=(
            jax.ShapeDtypeStruct(x2.shape, BF16),
            jax.ShapeDtypeStruct(x2.shape[::-1], BF16),
            jax.ShapeDtypeStruct((N_DEV, *w_shard.shape), BF16),
            jax.ShapeDtypeStruct((CHUNK, D_MODEL), F32),
            jax.ShapeDtypeStruct((8, D_CONV), F32),
        ),
        in_specs=[pl.BlockSpec((tm, D_MODEL), lambda i: (i, 0)), _VMEM, _VMEM, _VMEM, _VMEM],
        out_specs=(
            pl.BlockSpec((tm, D_MODEL), lambda i: (i, 0)),
            pl.BlockSpec((D_MODEL, tm), lambda i: (0, i)),
            _HBM,
            const(CHUNK, D_MODEL),
            const(8, D_CONV),
        ),
        scratch_shapes=[
            pltpu.VMEM(w_shard.shape, BF16),
            pltpu.VMEM(pack_shape, F32),
            pltpu.VMEM((N_DEV, *pack_shape), F32),
            pltpu.SemaphoreType.DMA(((n_ch + 1) * per,)),
            pltpu.SemaphoreType.DMA(((n_ch + 1) * per,)),
            pltpu.SemaphoreType.DMA((n_ch + 1,)),
        ],
        compiler_params=_params(),
    )(x2, g1, w_shard, meta, conv_w)


def _f1(hn_tok, meta_chunk, g1, w3, cw8, rot, wo_shard):
    tm = ROW_TILE
    nt = hn_tok.shape[0] // tm + 1
    n_rows = nt * tm

    def body(hn_ref, mc_ref, g_ref, w_hbm, cw_ref, ca_ref, sa_ref, cb_ref, sb_ref, wo_ref,
             hnT_ref, pr_ref, co_ref, cv_ref, wo_all_ref, halo, wo_bf, w_ref, send_sems, recv_sems, local_sem, w_sems):
        i = pl.program_id(0)

        def w_copy(j):
            return pltpu.make_async_copy(w_hbm.at[j], w_ref.at[j], w_sems.at[j])

        @pl.when(i == 0)
        def _():
            for j in range(N_PROJ):
                w_copy(j).start()
            wo_bf[...] = wo_ref[...].astype(BF16)

        wo_phases = _two_level_gather(wo_bf, wo_all_ref, send_sems, recv_sems, local_sem)
        for step, phase in zip((0, nt // 3, 2 * nt // 3, nt - 1), wo_phases, strict=True):
            pl.when(i == step)(phase)

        def work(hn, r0, before_block=lambda j: None):
            rs = slice(r0, tm)
            n = tm - r0

            def proj(j):
                before_block(j)
                return jnp.dot(hn, w_ref[j], preferred_element_type=F32)

            cx, cb, cc, cg = proj(0), proj(1), proj(2), proj(3)
            u = cc * cx
            rows = lax.broadcasted_iota(jnp.int32, u.shape, 0)
            hl = halo[...]
            u1 = jnp.where(rows == 0, hl[7:8], pltpu.roll(u, 1, 0))
            u2 = jnp.where(rows == 0, hl[6:7], jnp.where(rows == 1, hl[7:8], pltpu.roll(u, 2, 0)))
            halo[...] = u[n - 8 : n]
            cw = cw_ref[...]
            conv = cw[0:1] * u2 + cw[1:2] * u1 + cw[2:3] * u
            co = (cb * conv * (cg * _sigmoid(cg))).astype(BF16)
            co_ref[rs, :] = co
            cv_ref[rs, :] = conv.astype(BF16)
            pr_ref[rs, 0 * BLK : 1 * BLK] = cx.astype(BF16)
            pr_ref[rs, 1 * BLK : 2 * BLK] = cb.astype(BF16)
            pr_ref[rs, 2 * BLK : 3 * BLK] = cc.astype(BF16)
            pr_ref[rs, 3 * BLK : 4 * BLK] = cg.astype(BF16)
            cos_all, sin_all = _tile_rotary(ca_ref, sa_ref, cb_ref, sb_ref, i)
            cos_t, sin_t = cos_all[rs, :], sin_all[rs, :]
            q, k = proj(4), proj(5)
            for hd in range(N_HEADS):
                c0 = hd * HEAD_DIM
                pr_ref[rs, 4 * BLK + c0 : 4 * BLK + c0 + HEAD_DIM] = (
                    _rot(q[:, c0 : c0 + HEAD_DIM], cos_t, sin_t) * Q_SCALE
                ).astype(BF16)
                pr_ref[rs, 5 * BLK + c0 : 5 * BLK + c0 + HEAD_DIM] = _rot(
                    k[:, c0 : c0 + HEAD_DIM], cos_t, sin_t
                ).astype(BF16)
            pr_ref[rs, 6 * BLK : 7 * BLK] = proj(6).astype(BF16)
            pr_ref[rs, 7 * BLK : 8 * BLK] = proj(7).astype(BF16)

        @pl.when(i == 0)
        def _():
            halo[...] = jnp.zeros_like(halo)
            h = mc_ref[...]
            r = lax.rsqrt(jnp.mean(h * h, axis=-1, keepdims=True) + EPS)
            hn = (h * r * g_ref[...]).astype(BF16)
            hnT_ref[...] = hn.T
            work(hn, LIVE0, lambda j: w_copy(j).wait())

        @pl.when(i > 0)
        def _():
            work(hn_ref[...], 0)

    row = lambda w: pl.BlockSpec((tm, w), lambda i: (i, 0))
    return pl.pallas_call(
        body,
        name="f1_inproj_conv",
        grid=(nt,),
        in_specs=[pl.BlockSpec((tm, D_MODEL), lambda i: (_token_tile(i), 0)), _VMEM, _VMEM, _HBM] + [_VMEM] * 6,
        out_specs=(
            pl.BlockSpec((D_MODEL, CHUNK), lambda i: (0, 0)),
            row(N_PROJ * BLK),
            row(D_CONV),
            row(D_CONV),
            _HBM,
        ),
        out_shape=(
            jax.ShapeDtypeStruct((D_MODEL, CHUNK), BF16),
            jax.ShapeDtypeStruct((n_rows, N_PROJ * BLK), BF16),
            jax.ShapeDtypeStruct((n_rows, D_CONV), BF16),
            jax.ShapeDtypeStruct((n_rows, D_CONV), BF16),
            jax.ShapeDtypeStruct((N_DEV, *wo_shard.shape), BF16),
        ),
        scratch_shapes=[
            pltpu.VMEM((8, D_CONV), F32),
            pltpu.VMEM(wo_shard.shape, BF16),
            pltpu.VMEM(w3.shape, w3.dtype),
            pltpu.SemaphoreType.DMA((GATHER_PAIRS,)),
            pltpu.SemaphoreType.DMA((GATHER_PAIRS,)),
            pltpu.SemaphoreType.DMA(()),
            pltpu.SemaphoreType.DMA((N_PROJ,)),
        ],
        compiler_params=_params(),
    )(hn_tok, meta_chunk, g1, w3, cw8, *rot, wo_shard)


def _group_norm(o):
    ys, rs = [], []
    for hd in range(N_HEADS):
        oh = o[:, hd * HEAD_DIM : (hd + 1) * HEAD_DIM]
        xc = oh - jnp.mean(oh, axis=-1, keepdims=True)
        rstd = lax.rsqrt(jnp.mean(xc * xc, axis=-1, keepdims=True) + EPS)
        ys.append(xc * rstd)
        rs.append(jnp.broadcast_to(rstd, oh.shape))
    return jnp.concatenate(ys, axis=1), jnp.concatenate(rs, axis=1)


def _f2(proj, conv_out, x2, t2, wout, gr, gf, dec, zeta, xi):
    n_rows = proj.shape[0]
    tm = ROW_TILE
    nt = n_rows // tm
    nct = tm // CHUNK

    def body(q_ref, k_ref, v_ref, rg_ref, co_ref, x_ref, t_ref, wo_ref, gr_ref, gf_ref, dec_ref, zeta_ref,
             xi_ref, dh2_ref, o_ref, st_ref, gw_ref, acc_ref, state, gacc):
        i = pl.program_id(0)

        def mixer(r0):
            chunks = range(r0 // CHUNK, nct)
            heads = []
            for hd in range(N_HEADS):
                cs = slice(hd * HEAD_DIM, (hd + 1) * HEAD_DIM)
                upd = {}
                for c in chunks:
                    rs = slice(c * CHUNK, (c + 1) * CHUNK)
                    kz = (k_ref[rs, cs].astype(F32) * zeta_ref[hd]).astype(BF16)
                    upd[c] = lax.dot_general(kz, v_ref[rs, cs], TN, preferred_element_type=F32)
                st = state[hd]
                outs = []
                for c in chunks:
                    rs = slice(c * CHUNK, (c + 1) * CHUNK)
                    q, k, v = q_ref[rs, cs], k_ref[rs, cs], v_ref[rs, cs]
                    st_bf = st.astype(BF16)
                    st_ref[c, hd] = st_bf
                    s = lax.dot_general(q, k, NT, preferred_element_type=F32) * dec_ref[hd]
                    inner = jnp.dot(s.astype(BF16), v, preferred_element_type=F32)
                    qx = (q.astype(F32) * xi_ref[hd]).astype(BF16)
                    outs.append(inner + jnp.dot(qx, st_bf, preferred_element_type=F32))
                    st = CHUNK_DECAY[hd] * st + upd[c]
                state[hd] = st
                heads.append(jnp.concatenate(outs, axis=0) if len(outs) > 1 else outs[0])
            o = jnp.concatenate(heads, axis=1)
            o_ref[r0:tm, :] = o.astype(BF16)
            yh, _ = _group_norm(o)
            rg = rg_ref[r0:tm, :].astype(F32)
            return (yh * gr_ref[...] * (rg * _sigmoid(rg))).astype(BF16)

        @pl.when(i == 0)
        def _():
            state[...] = jnp.zeros_like(state)
            acc_ref[...] = jnp.zeros_like(acc_ref)
            gacc[...] = jnp.zeros_like(gacc)
            mixer(LIVE0)
            dh2_ref[...] = jnp.zeros_like(dh2_ref)

        @pl.when(i > 0)
        def _():
            ro = mixer(0)
            h2 = (
                x_ref[...]
                + jnp.dot(co_ref[...], wo_ref[0:D_CONV], preferred_element_type=F32)
                + jnp.dot(ro, wo_ref[D_CONV:], preferred_element_type=F32)
            )
            r2 = lax.rsqrt(jnp.mean(h2 * h2, axis=-1, keepdims=True) + EPS)
            yn = h2 * r2
            gfv = gf_ref[...]
            err = yn * gfv - t_ref[...]
            tile_loss = jnp.sum(jnp.sum(err * err, axis=-1, keepdims=True), axis=0, keepdims=True) * (0.5 / D_MODEL)
            acc_ref[0:1, :] += jnp.sum(err * yn, axis=0, keepdims=True) * (1.0 / D_MODEL)
            acc_ref[1:2, :] += tile_loss
            dyn = err * (gfv * (1.0 / D_MODEL))
            dh2 = r2 * (dyn - yn * jnp.mean(dyn * yn, axis=-1, keepdims=True))
            dh2_ref[...] = dh2
            dh2_bf = dh2.astype(BF16)
            gacc[0:D_CONV, :] += lax.dot_general(co_ref[...], dh2_bf, TN, preferred_element_type=F32)
            gacc[D_CONV:, :] += lax.dot_general(ro, dh2_bf, TN, preferred_element_type=F32)

        @pl.when(i == nt - 1)
        def _():
            gw_ref[...] = gacc[...].astype(BF16)

    row = lambda w, j=0: pl.BlockSpec((tm, w), lambda i: (i, j))
    tok = pl.BlockSpec((tm, D_MODEL), lambda i: (_token_tile(i), 0))
    return pl.pallas_call(
        body,
        name="f2_retention_out",
        grid=(nt,),
        in_specs=[row(BLK, 4), row(BLK, 5), row(BLK, 6), row(BLK, 7), row(D_CONV), tok, tok] + [_VMEM] * 6,
        out_specs=(
            row(D_MODEL),
            row(D_RET),
            pl.BlockSpec((nct, N_HEADS, HEAD_DIM, HEAD_DIM), lambda i: (i, 0, 0, 0)),
            pl.BlockSpec((D_MODEL, D_MODEL), lambda i: (0, 0)),
            pl.BlockSpec((8, D_MODEL), lambda i: (0, 0)),
        ),
        out_shape=(
            jax.ShapeDtypeStruct((n_rows, D_MODEL), F32),
            jax.ShapeDtypeStruct((n_rows, D_RET), BF16),
            jax.ShapeDtypeStruct((n_rows // CHUNK, N_HEADS, HEAD_DIM, HEAD_DIM), BF16),
            jax.ShapeDtypeStruct((D_MODEL, D_MODEL), BF16),
            jax.ShapeDtypeStruct((8, D_MODEL), F32),
        ),
        scratch_shapes=[
            pltpu.VMEM((N_HEADS, HEAD_DIM, HEAD_DIM), F32),
            pltpu.VMEM((D_MODEL, D_MODEL), F32),
        ],
        compiler_params=_params(),
    )(proj, proj, proj, proj, conv_out, x2, t2, wout, gr, gf, dec, zeta, xi)


def _b1(dh2, proj, o, states, wout, gr, tables, zeta0, rot):
    n_rows = dh2.shape[0]
    tm = ROW_TILE
    nt = n_rows // tm
    chunk = B1_CHUNK
    nct = tm // chunk
    decay = tuple(math.exp(chunk * lg) for lg in LOG_G)

    def body(dh2_ref, pr_ref, o_ref, st_ref, wo_ref, gr_ref, dec_ref, dect_ref, zeta_ref, xi_ref, zeta0_ref,
             ca_ref, sa_ref, cb_ref, sb_ref, dp_ref, dco_ref, acc_ref, dstate):
        i = pl.program_id(0)
        tile = nt - 1 - i
        cos_all, sin_all = _tile_rotary(ca_ref, sa_ref, cb_ref, sb_ref, tile)

        @pl.when(i == 0)
        def _():
            dstate[...] = jnp.zeros_like(dstate)
            acc_ref[...] = jnp.zeros_like(acc_ref)

        @pl.when(tile > 0)
        def _():
            dmix = lax.dot_general(dh2_ref[...].astype(BF16), wo_ref[...], NT, preferred_element_type=F32)
            dco_ref[...] = dmix[:, :D_CONV].astype(BF16)
            dro = dmix[:, D_CONV:]
            rg = pr_ref[:, 3 * BLK : 4 * BLK].astype(F32)
            sg = _sigmoid(rg)
            silu, dsilu = rg * sg, sg * (1.0 + rg * (1.0 - sg))
            yh, rstd = _group_norm(o_ref[...].astype(F32))
            grv = gr_ref[...]
            dp_ref[:, 3 * BLK : 4 * BLK] = (dro * (yh * grv) * dsilu).astype(BF16)
            dret = dro * silu
            acc_ref[0:1, :] += jnp.sum(dret * yh, axis=0, keepdims=True)
            dyh = dret * grv
            for hd in range(N_HEADS):
                cs = slice(hd * HEAD_DIM, (hd + 1) * HEAD_DIM)
                a, b = dyh[:, cs], yh[:, cs]
                do_head = (
                    rstd[:, cs]
                    * (a - jnp.mean(a, axis=-1, keepdims=True) - b * jnp.mean(a * b, axis=-1, keepdims=True))
                ).astype(BF16)
                dupd = {}
                for c in range(nct):
                    rs = slice(c * chunk, (c + 1) * chunk)
                    qx = (pr_ref[rs, cs].astype(F32) * xi_ref[hd]).astype(BF16)
                    dupd[c] = lax.dot_general(qx, do_head[rs, :], TN, preferred_element_type=F32)
                dst = dstate[hd]
                for c in reversed(range(nct)):
                    rs = slice(c * chunk, (c + 1) * chunk)
                    cos_t, sin_t = cos_all[rs, :], sin_all[rs, :]
                    q = pr_ref[rs, hd * HEAD_DIM : (hd + 1) * HEAD_DIM]
                    k = pr_ref[rs, BLK + hd * HEAD_DIM : BLK + (hd + 1) * HEAD_DIM]
                    v = pr_ref[rs, 2 * BLK + hd * HEAD_DIM : 2 * BLK + (hd + 1) * HEAD_DIM]
                    do = do_head[rs, :]
                    st_bf = st_ref[c * (chunk // CHUNK), hd]
                    dst_bf = dst.astype(BF16)
                    zt, xt = zeta_ref[hd], xi_ref[hd]
                    sT = (lax.dot_general(k, q, NT, preferred_element_type=F32) * dect_ref[hd]).astype(BF16)
                    dsT = (lax.dot_general(v, do, NT, preferred_element_type=F32) * dect_ref[hd]).astype(BF16)
                    ds = (lax.dot_general(do, v, NT, preferred_element_type=F32) * dec_ref[hd]).astype(BF16)
                    kz = (k.astype(F32) * zt).astype(BF16)
                    dv = jnp.dot(sT, do, preferred_element_type=F32) + jnp.dot(kz, dst_bf, preferred_element_type=F32)
                    dq = jnp.dot(ds, k, preferred_element_type=F32) + xt * lax.dot_general(
                        do, st_bf, NT, preferred_element_type=F32
                    )
                    dk = jnp.dot(dsT, q, preferred_element_type=F32) + zt * lax.dot_general(
                        v, dst_bf, NT, preferred_element_type=F32
                    )
                    dst = decay[hd] * dst + dupd[c]
                    dp_ref[rs, hd * HEAD_DIM : (hd + 1) * HEAD_DIM] = _rot_bwd(dq * Q_SCALE, cos_t, sin_t).astype(BF16)
                    dp_ref[rs, BLK + hd * HEAD_DIM : BLK + (hd + 1) * HEAD_DIM] = _rot_bwd(dk, cos_t, sin_t).astype(BF16)
                    dp_ref[rs, 2 * BLK + hd * HEAD_DIM : 2 * BLK + (hd + 1) * HEAD_DIM] = dv.astype(BF16)
                dstate[hd] = dst

        @pl.when(tile == 0)
        def _():
            rs = slice(LIVE0, tm)
            cos_t, sin_t = cos_all[rs, :], sin_all[rs, :]
            zeros = jnp.zeros((tm - LIVE0, HEAD_DIM), BF16)
            for hd in range(N_HEADS):
                k = pr_ref[rs, BLK + hd * HEAD_DIM : BLK + (hd + 1) * HEAD_DIM]
                v = pr_ref[rs, 2 * BLK + hd * HEAD_DIM : 2 * BLK + (hd + 1) * HEAD_DIM]
                dst_bf = dstate[hd].astype(BF16)
                zt = zeta0_ref[hd]
                kz = (k.astype(F32) * zt).astype(BF16)
                dv = jnp.dot(kz, dst_bf, preferred_element_type=F32)
                dk = zt * lax.dot_general(v, dst_bf, NT, preferred_element_type=F32)
                dp_ref[rs, hd * HEAD_DIM : (hd + 1) * HEAD_DIM] = zeros
                dp_ref[rs, BLK + hd * HEAD_DIM : BLK + (hd + 1) * HEAD_DIM] = _rot_bwd(dk, cos_t, sin_t).astype(BF16)
                dp_ref[rs, 2 * BLK + hd * HEAD_DIM : 2 * BLK + (hd + 1) * HEAD_DIM] = dv.astype(BF16)
                dp_ref[rs, 3 * BLK + hd * HEAD_DIM : 3 * BLK + (hd + 1) * HEAD_DIM] = zeros
            dco_ref[rs, :] = jnp.zeros((tm - LIVE0, D_CONV), BF16)

    half = N_PROJ * BLK // 2
    rev = lambda w, j=0: pl.BlockSpec((tm, w), lambda i: (nt - 1 - i, j))
    return pl.pallas_call(
        body,
        name="b1_dret",
        grid=(nt,),
        in_specs=[
            rev(D_MODEL),
            rev(half, 1),
            rev(D_RET),
            pl.BlockSpec((tm // CHUNK, N_HEADS, HEAD_DIM, HEAD_DIM), lambda i: (nt - 1 - i, 0, 0, 0)),
        ]
        + [_VMEM] * 11,
        out_specs=(rev(half, 1), rev(D_CONV), pl.BlockSpec((8, D_RET), lambda i: (0, 0))),
        out_shape=(
            jax.ShapeDtypeStruct((n_rows, N_PROJ * BLK), BF16),
            jax.ShapeDtypeStruct((n_rows, D_CONV), BF16),
            jax.ShapeDtypeStruct((8, D_RET), F32),
        ),
        scratch_shapes=[pltpu.VMEM((N_HEADS, HEAD_DIM, HEAD_DIM), F32)],
        compiler_params=_params(),
    )(dh2, proj, o, states, wout, gr, *tables, zeta0, *rot)


def _b2(dproj, dco, proj, conv, w3, cw8, x2, meta_chunk, dh2, g1):
    n_rows = dproj.shape[0]
    tm = ROW_TILE
    nt = n_rows // tm
    half = N_PROJ * BLK // 2

    def body(dpr_ref, dco_ref, pr_ref, cv_ref, w_ref, cw_ref, x_ref, mc_ref, dh2_ref, g_ref,
             dpc_ref, gx_ref, dm_ref, acc_ref, accc_ref, halo):
        i = pl.program_id(0)
        tile = nt - 1 - i

        @pl.when(i == 0)
        def _():
            acc_ref[...] = jnp.zeros_like(acc_ref)
            accc_ref[...] = jnp.zeros_like(accc_ref)
            halo[...] = jnp.zeros_like(halo)

        def work(h, r0):
            rs = slice(r0, tm)
            n = tm - r0
            dco = dco_ref[rs, :].astype(F32)
            cx = pr_ref[rs, 0 * BLK : 1 * BLK].astype(F32)
            cb = pr_ref[rs, 1 * BLK : 2 * BLK].astype(F32)
            cc = pr_ref[rs, 2 * BLK : 3 * BLK].astype(F32)
            cg = pr_ref[rs, 3 * BLK : 4 * BLK].astype(F32)
            conv = cv_ref[rs, :].astype(F32)
            sg = _sigmoid(cg)
            silu, dsilu = cg * sg, sg * (1.0 + cg * (1.0 - sg))
            t = dco * cb
            dcb = (dco * conv * silu).astype(BF16)
            dcg = (t * conv * dsilu).astype(BF16)
            dconv = t * silu
            rows = lax.broadcasted_iota(jnp.int32, dconv.shape, 0)
            hl = halo[...]
            dc1 = jnp.where(rows == n - 1, hl[0:1], pltpu.roll(dconv, n - 1, 0))
            dc2 = jnp.where(rows == n - 2, hl[0:1], jnp.where(rows == n - 1, hl[1:2], pltpu.roll(dconv, n - 2, 0)))
            halo[...] = dconv[0:8]
            cw = cw_ref[...]
            du = cw[2:3] * dconv + cw[1:2] * dc1 + cw[0:1] * dc2
            u = cc * cx
            accc_ref[0:1, :] += jnp.sum(u * dc2, axis=0, keepdims=True)
            accc_ref[1:2, :] += jnp.sum(u * dc1, axis=0, keepdims=True)
            accc_ref[2:3, :] += jnp.sum(u * dconv, axis=0, keepdims=True)
            dcx = (du * cc).astype(BF16)
            dcc = (du * cx).astype(BF16)
            dhn = lax.dot_general(dpr_ref[rs, 0:BLK], w_ref[4], NT, preferred_element_type=F32)
            for j in range(1, 4):
                dhn += lax.dot_general(dpr_ref[rs, j * BLK : (j + 1) * BLK], w_ref[4 + j], NT, preferred_element_type=F32)
            for blk, d in ((1, dcb), (3, dcg), (0, dcx), (2, dcc)):
                dpc_ref[rs, blk * BLK : (blk + 1) * BLK] = d
                dhn += lax.dot_general(d, w_ref[blk], NT, preferred_element_type=F32)
            r = lax.rsqrt(jnp.mean(h * h, axis=-1, keepdims=True) + EPS)
            hh = h * r
            acc_ref[0:1, :] += jnp.sum(dhn * hh, axis=0, keepdims=True)
            dg = dhn * g_ref[...]
            return dh2_ref[rs, :] + r * (dg - hh * jnp.mean(dg * hh, axis=-1, keepdims=True))

        @pl.when(tile == 0)
        def _():
            dh = work(mc_ref[...], LIVE0)
            dm_ref[...] = dh[CHUNK - N_META : CHUNK]

        @pl.when(tile > 0)
        def _():
            gx_ref[...] = work(x_ref[...], 0)

    rev = lambda w, j=0: pl.BlockSpec((tm, w), lambda i: (nt - 1 - i, j))
    tok = pl.BlockSpec((tm, D_MODEL), lambda i: (_token_tile(nt - 1 - i), 0))
    const = lambda r, c: pl.BlockSpec((r, c), lambda i: (0, 0))
    return pl.pallas_call(
        body,
        name="b2_dconv_dh",
        grid=(nt,),
        in_specs=[rev(half, 1), rev(D_CONV), rev(half, 0), rev(D_CONV), _VMEM, _VMEM, tok, _VMEM, rev(D_MODEL), _VMEM],
        out_specs=(rev(half, 0), tok, const(N_META, D_MODEL), const(8, D_MODEL), const(8, D_CONV)),
        out_shape=(
            jax.ShapeDtypeStruct(dproj.shape, BF16),
            jax.ShapeDtypeStruct(x2.shape, F32),
            jax.ShapeDtypeStruct((N_META, D_MODEL), F32),
            jax.ShapeDtypeStruct((8, D_MODEL), F32),
            jax.ShapeDtypeStruct((8, D_CONV), F32),
        ),
        scratch_shapes=[pltpu.VMEM((8, D_CONV), F32)],
        input_output_aliases={0: 0},
        compiler_params=_params(),
    )(dproj, dco, proj, conv, w3, cw8, x2, meta_chunk, dh2, g1)


A_CHUNKS = 4


def _gw_in_scatter(hnT_tok, hnT_meta, dproj, gw_out_parts, dmeta_rows, acc_conv, acc_b2, acc_b1, acc_f2, me_arr):
    n_rows = dproj.shape[0]
    last = N_DEV - 1
    cols = D_CONV // N_DEV
    a_cols = hnT_tok.shape[1] // A_CHUNKS
    by_dest = (True, True, False)
    small_blocks = (gw_out_parts.shape[1:], (SC_CONV0 + 8, CHUNK), (GA_LOSS + 8, CHUNK))

    def body(me_ref, a_hbm, am_ref, b_ref, go_ref, dm_ref, ac_ref, ab2_ref, ab1_ref, af2_ref,
             land_in, land_go, land_sc, land_ga, d2d_buf, d2d_land, ici_buf, sc_ref, ga_ref, a_ref,
             d2d_send, d2d_recv, ici_send, ici_recv, send_sems, recv_sems, local_sems, a_sems):
        del me_ref
        s = pl.program_id(0)

        def a_copy(ch):
            cs = pl.ds(ch * a_cols, a_cols)
            return pltpu.make_async_copy(a_hbm.at[:, cs], a_ref.at[:, cs], a_sems.at[ch])

        @pl.when(s == 0)
        def _():
            for ch in range(A_CHUNKS):
                a_copy(ch).start()

        t = last - s
        q = t >> 1
        x, y, c = lax.axis_index("x"), lax.axis_index("y"), lax.axis_index("c")
        me = 4 * x + 2 * y + c
        chip = 2 * x + y
        srcs, lands = (go_ref, sc_ref, ga_ref), (land_go, land_sc, land_ga)

        def peer_at(k):
            return (1 - x if k & 4 else x, 1 - y if k & 2 else y, 1 - c if k & 1 else c)

        def small_copy(a, k):
            px, py, pc = peer_at(k)
            return pltpu.make_async_remote_copy(
                src_ref=srcs[a].at[4 * px + 2 * py + pc] if by_dest[a] else srcs[a],
                dst_ref=lands[a].at[me],
                send_sem=send_sems.at[a * last + k - 1],
                recv_sem=recv_sems.at[a * last + k - 1],
                device_id=(px, py, pc),
                device_id_type=MESH,
            )

        def small_local(a):
            return pltpu.make_async_copy(srcs[a].at[me] if by_dest[a] else srcs[a], lands[a].at[me], local_sems.at[a + 1])

        def d2d_copy(j):
            return pltpu.make_async_remote_copy(
                src_ref=d2d_buf.at[j],
                dst_ref=d2d_land.at[j],
                send_sem=d2d_send.at[j],
                recv_sem=d2d_recv.at[j],
                device_id=(x, y, 1 - c),
                device_id_type=MESH,
            )

        def ici_copy(j, to):
            return pltpu.make_async_remote_copy(
                src_ref=ici_buf.at[j],
                dst_ref=land_in.at[chip],
                send_sem=ici_send.at[j],
                recv_sem=ici_recv.at[j],
                device_id=to,
                device_id_type=MESH,
            )

        def own_copy():
            return pltpu.make_async_copy(ici_buf.at[0], land_in.at[chip], local_sems.at[0])

        @pl.when(s == 0)
        def _():
            sc_ref[...] = jnp.zeros_like(sc_ref)
            for d in range(N_DEV):
                sc_ref[d, SC_META0 : SC_META0 + N_META, :] = dm_ref[:, d * CHUNK : (d + 1) * CHUNK]
                lane0 = (d * cols) % CHUNK
                rows8 = ac_ref[:, d * cols - lane0 : d * cols - lane0 + CHUNK]
                rows8 = pltpu.roll(rows8, CHUNK - lane0, 1) if lane0 else rows8
                sc_ref[d, SC_CONV0 : SC_CONV0 + 3, 0:cols] = rows8[0:3, 0:cols]
            ga_ref[...] = jnp.zeros_like(ga_ref)
            for row0, acc, n in ((GA_N1, ab2_ref, 8), (GA_RG, ab1_ref, 4), (GA_FG, af2_ref, 8)):
                for j in range(n):
                    ga_ref[row0 + j : row0 + j + 1, :] = acc[0:1, j * CHUNK : (j + 1) * CHUNK]
            ga_ref[GA_LOSS : GA_LOSS + 1, :] = af2_ref[1:2, 0:CHUNK]
            for a in range(3):
                small_local(a).start()
            for k in range(last, 0, -1):
                for a in range(3):
                    small_copy(a, k).start()

        blk = jnp.dot(am_ref[...], b_ref[LIVE0:ROW_TILE, :], preferred_element_type=F32)
        for ch in range(A_CHUNKS):
            pl.when(s == 0)(a_copy(ch).wait)
            blk += jnp.dot(
                a_ref[:, ch * a_cols : (ch + 1) * a_cols],
                b_ref[ROW_TILE + ch * a_cols : ROW_TILE + (ch + 1) * a_cols, :],
                preferred_element_type=F32,
            )

        @pl.when((t & 1) == 1)
        def _():
            d2d_buf[q] = blk.astype(BF16)
            d2d_copy(q).start()

        @pl.when((t & 1) == 0)
        def _():
            d2d_copy(q).wait_recv()
            ici_buf[q] = (blk + d2d_land[q].astype(F32)).astype(BF16)

            @pl.when(t != 0)
            def _():
                ici_copy(q, (jnp.bitwise_xor(x, (t >> 2) & 1), jnp.bitwise_xor(y, (t >> 1) & 1), c)).start()

            @pl.when(t == 0)
            def _():
                own_copy().start()
                for j in range(N_CHIPS):
                    d2d_copy(j).wait_send()
                for j in range(N_CHIPS - 1, 0, -1):
                    ici_copy(j, peer_at(2 * j)).wait()
                for k in range(last, 0, -1):
                    for a in range(3):
                        small_copy(a, k).wait()
                for a in range(3):
                    small_local(a).wait()
                own_copy().wait()

    grid_spec = pltpu.PrefetchScalarGridSpec(
        num_scalar_prefetch=1,
        grid=(N_DEV,),
        in_specs=[_HBM, _VMEM, pl.BlockSpec((n_rows, BLK), lambda s, me: (0, jnp.bitwise_xor(me[0], last - s))), _HBM]
        + [_VMEM] * 5,
        out_specs=tuple([_HBM] * 4),
        scratch_shapes=[
            pltpu.VMEM((N_CHIPS, D_MODEL, BLK), BF16),
            pltpu.VMEM((N_CHIPS, D_MODEL, BLK), BF16),
            pltpu.VMEM((N_CHIPS, D_MODEL, BLK), BF16),
            pltpu.VMEM((N_DEV, *small_blocks[1]), F32),
            pltpu.VMEM(small_blocks[2], F32),
            pltpu.VMEM(hnT_tok.shape, hnT_tok.dtype),
            pltpu.SemaphoreType.DMA((N_CHIPS,)),
            pltpu.SemaphoreType.DMA((N_CHIPS,)),
            pltpu.SemaphoreType.DMA((N_CHIPS,)),
            pltpu.SemaphoreType.DMA((N_CHIPS,)),
            pltpu.SemaphoreType.DMA((3 * last,)),
            pltpu.SemaphoreType.DMA((3 * last,)),
            pltpu.SemaphoreType.DMA((4,)),
            pltpu.SemaphoreType.DMA((A_CHUNKS,)),
        ],
    )
    return pl.pallas_call(
        body,
        name="gw_in_scatter",
        grid_spec=grid_spec,
        out_shape=(jax.ShapeDtypeStruct((N_CHIPS, D_MODEL, BLK), BF16),)
        + tuple(jax.ShapeDtypeStruct((N_DEV, *b), dt) for b, dt in zip(small_blocks, (gw_out_parts.dtype, F32, F32), strict=True)),
        compiler_params=_params(),
    )(me_arr, hnT_tok, hnT_meta, dproj, gw_out_parts, dmeta_rows, acc_conv, acc_b2, acc_b1, acc_f2)


def _adamw(w, g, m, v):
    m = ADAM_B1 * m + (1.0 - ADAM_B1) * g
    v = ADAM_B2 * v + (1.0 - ADAM_B2) * (g * g)
    m_hat = m / (1.0 - ADAM_B1**ADAM_STEP)
    v_hat = v / (1.0 - ADAM_B2**ADAM_STEP)
    delta = -ADAM_LR * (m_hat / (jnp.sqrt(v_hat) + ADAM_EPS) + ADAM_WD * w)
    return delta, m, v


def _sum_adamw(name, parts, w, m, v, rows_per_step):
    n_r, n_c = w.shape
    n_parts = parts.shape[0]
    tr = rows_per_step

    def body(p_ref, w_ref, m_ref, v_ref, g_ref, d_ref, nm_ref, nv_ref):
        g = p_ref[0].astype(F32)
        for s in range(1, n_parts):
            g = g + p_ref[s].astype(F32)
        g_ref[...] = g
        d_ref[...], nm_ref[...], nv_ref[...] = _adamw(w_ref[...], g, m_ref[...], v_ref[...])

    blk = pl.BlockSpec((tr, n_c), lambda i: (i, 0))
    return pl.pallas_call(
        body,
        name=name,
        grid=(n_r // tr,),
        in_specs=[pl.BlockSpec((n_parts, tr, n_c), lambda i: (0, i, 0)), blk, blk, blk],
        out_specs=(blk,) * 4,
        out_shape=(jax.ShapeDtypeStruct((n_r, n_c), F32),) * 4,
        compiler_params=_params(),
    )(parts, w, m, v)


def _small_leaves(meta, conv_w, n1, rg, fg):
    return meta, conv_w, n1.reshape(8, CHUNK), rg.reshape(4, CHUNK), fg.reshape(8, CHUNK)


def _from_small_leaves(meta, conv_w, n1, rg, fg):
    return meta, conv_w, n1.reshape(D_MODEL), rg.reshape(D_RET), fg.reshape(D_MODEL)


def _adamw_small(land_sc, land_ga, w, m, v):
    n_leaf = 5

    def body(sc_ref, ga_ref, *refs):
        ins, outs = refs[: 3 * n_leaf], refs[3 * n_leaf :]
        sc, ga = sc_ref[0], ga_ref[0]
        for s in range(1, N_DEV):
            sc = sc + sc_ref[s]
            ga = ga + ga_ref[s]
        grads = (
            sc[SC_META0 : SC_META0 + N_META],
            sc[SC_CONV0 : SC_CONV0 + 3, 0 : D_CONV // N_DEV],
            ga[GA_N1 : GA_N1 + 8],
            ga[GA_RG : GA_RG + 4],
            ga[GA_FG : GA_FG + 8],
        )
        for leaf, g in enumerate(grads):
            d, nm, nv = _adamw(ins[leaf][...], g, ins[n_leaf + leaf][...], ins[2 * n_leaf + leaf][...])
            outs[leaf][...] = g
            outs[n_leaf + leaf][...] = d
            outs[2 * n_leaf + leaf][...] = nm
            outs[3 * n_leaf + leaf][...] = nv
        outs[4 * n_leaf][...] = ga[GA_LOSS : GA_LOSS + 1]

    leaf_shapes = tuple(jax.ShapeDtypeStruct(a.shape, F32) for a in w)
    out = pl.pallas_call(
        body,
        name="adamw_small",
        out_shape=leaf_shapes * 4 + (jax.ShapeDtypeStruct((1, CHUNK), F32),),
    )(land_sc, land_ga, *w, *m, *v)
    return tuple(out[k * n_leaf : (k + 1) * n_leaf] for k in range(4)), out[4 * n_leaf]


def kernel(x, meta, norm1_g, w_in, conv_w, ret_norm_g, w_out, final_g, loss_target, m_meta, m_norm1_g, m_w_in, m_conv_w, m_ret_norm_g, m_w_out, m_final_g, v_meta, v_norm1_g, v_w_in, v_conv_w, v_ret_norm_g, v_w_out, v_final_g):
    seq = x.shape[1]
    assert x.shape == (1, seq, D_MODEL) and seq % ROW_TILE == 0
    n_tiles = seq // ROW_TILE + 1
    x2, t2 = x[0], loss_target[0]

    small_w = _small_leaves(meta, conv_w, norm1_g, ret_norm_g, final_g)
    g1 = norm1_g.reshape(1, D_MODEL)
    hn_tok, hnT_tok, w3, meta_chunk, cw8 = _gather_weights_and_norm(x2, g1, w_in, meta, conv_w)

    rot = _rotary_tables(n_tiles, ROW_TILE)
    dec, dect, zeta, xi = _decay_tables()
    gr = ret_norm_g.reshape(1, D_RET)
    gf = final_g.reshape(1, D_MODEL)

    hnT_meta, proj, conv_out, conv, wo3 = _f1(hn_tok, meta_chunk, g1, w3, cw8, rot, w_out)
    wout = wo3.reshape(D_MODEL, D_MODEL)
    dh2, o, states, gw_out, acc_f2 = _f2(proj, conv_out, x2, t2, wout, gr, gf, dec, zeta, xi)
    dproj_ret, dco, acc_b1 = _b1(dh2, proj, o, states, wout, gr, _decay_tables(B1_CHUNK), zeta, rot)
    dproj, grad_x2, dmeta_rows, acc_b2, acc_conv = _b2(dproj_ret, dco, proj, conv, w3, cw8, x2, meta_chunk, dh2, g1)
    gw_out_parts = gw_out.reshape(N_DEV, D_MODEL // N_DEV, D_MODEL)

    me_arr = (4 * lax.axis_index("x") + 2 * lax.axis_index("y") + lax.axis_index("c")).astype(jnp.int32).reshape(1)
    land_in, land_out, land_sc, land_ga = _gw_in_scatter(
        hnT_tok, hnT_meta, dproj, gw_out_parts, dmeta_rows, acc_conv, acc_b2, acc_b1, acc_f2, me_arr
    )

    g_w_in, d_w_in, nm_w_in, nv_w_in = _sum_adamw("adamw_w_in", land_in, w_in, m_w_in, v_w_in, 256)
    g_w_out, d_w_out, nm_w_out, nv_w_out = _sum_adamw("adamw_w_out", land_out, w_out, m_w_out, v_w_out, 64)
    small_out, loss_row = _adamw_small(
        land_sc,
        land_ga,
        small_w,
        _small_leaves(m_meta, m_conv_w, m_norm1_g, m_ret_norm_g, m_final_g),
        _small_leaves(v_meta, v_conv_w, v_norm1_g, v_ret_norm_g, v_final_g),
    )
    loss = loss_row[0, 0]
    grad_x = grad_x2.reshape(1, seq, D_MODEL)

    def leaves(w_in_leaf, w_out_leaf, small_leaves):
        meta_leaf, conv_leaf, n1_leaf, rg_leaf, fg_leaf = _from_small_leaves(*small_leaves)
        return (meta_leaf, n1_leaf, w_in_leaf, conv_leaf, rg_leaf, w_out_leaf, fg_leaf)

    return (
        loss,
        grad_x,
        *leaves(g_w_in, g_w_out, small_out[0]),
        *leaves(d_w_in, d_w_out, small_out[1]),
        *leaves(nm_w_in, nm_w_out, small_out[2]),
        *leaves(nv_w_in, nv_w_out, small_out[3]),
    )
```

```python
import math

import jax
import jax.numpy as jnp
import numpy as np
from jax import lax
from jax.experimental import pallas as pl
from jax.experimental.pallas import tpu as pltpu

F32 = jnp.float32
BF16 = jnp.bfloat16

N_DEV = 8
N_CHIPS = 4
D_MODEL = 1024
N_META = 16
CHUNK = 128
D_CONV = 512
D_RET = 512
N_HEADS = 4
HEAD_DIM = 128
N_PROJ = 8
BLK = 512
ROPE_BASE = 10000.0
EPS = 1e-6
Q_SCALE = HEAD_DIM ** -0.5
LOG_G = tuple(math.log(1.0 - 2.0 ** (-5.0 - h)) for h in range(N_HEADS))
CHUNK_DECAY = tuple(math.exp(CHUNK * lg) for lg in LOG_G)

ADAM_LR = 0.001
ADAM_B1 = 0.9
ADAM_B2 = 0.999
ADAM_EPS = 1e-08
ADAM_WD = 0.01
ADAM_STEP = 10

ROW_TILE = 512
PAD_ROWS = ROW_TILE - N_META
LIVE0 = ROW_TILE - CHUNK
B1_CHUNK = 256
VMEM_LIMIT = 56 * 1024 * 1024

SC_META0, SC_CONV0 = 0, 16
GA_N1, GA_RG, GA_FG, GA_LOSS = 0, 8, 16, 24

NT = (((1,), (1,)), ((), ()))
TN = (((0,), (0,)), ((), ()))
MESH = pl.DeviceIdType.MESH

_VMEM = pl.BlockSpec(memory_space=pltpu.VMEM)
_HBM = pl.BlockSpec(memory_space=pltpu.HBM)


def _params(n_axes=1):
    return pltpu.CompilerParams(dimension_semantics=("arbitrary",) * n_axes, vmem_limit_bytes=VMEM_LIMIT)


def _sigmoid(x):
    return 0.5 * jnp.tanh(0.5 * x) + 0.5


def _decay_tables(chunk=CHUNK):
    idx = np.arange(chunk, dtype=np.float64)
    diff = idx[:, None] - idx[None, :]
    dec = np.stack([np.where(diff >= 0, np.exp(diff * lg), 0.0) for lg in LOG_G])
    zeta = np.stack([np.exp((chunk - 1 - idx) * lg) for lg in LOG_G])
    xi = np.stack([np.exp((idx + 1.0) * lg) for lg in LOG_G])
    ones = np.ones((1, 1, HEAD_DIM))
    return (
        jnp.asarray(dec, F32),
        jnp.asarray(dec.transpose(0, 2, 1), F32),
        jnp.asarray(zeta[:, :, None] * ones, F32),
        jnp.asarray(xi[:, :, None] * ones, F32),
    )


def _rotary_tables(n_tiles, tm):
    half = HEAD_DIM // 2
    freqs = (1.0 / (np.float32(ROPE_BASE) ** (np.arange(half, dtype=np.float32) / np.float32(half)))).astype(np.float64)
    sign = np.concatenate([-np.ones(half), np.ones(half)])
    two = lambda a: np.concatenate([a, a], axis=1)
    base = two((np.arange(n_tiles, dtype=np.float64) * tm - PAD_ROWS)[:, None] * freqs[None, :])
    off = two(np.arange(tm, dtype=np.float64)[:, None] * freqs[None, :])
    as32 = lambda a: jnp.asarray(a, F32)
    return as32(np.cos(base)), as32(np.sin(base) * sign), as32(np.cos(off)), as32(np.sin(off) * sign)


def _tile_rotary(ca_ref, sa_ref, cb_ref, sb_ref, tile):
    ca, sa = ca_ref[pl.ds(tile, 1), :], sa_ref[pl.ds(tile, 1), :]
    cb, sb = cb_ref[...], sb_ref[...]
    return ca * cb - sa * sb, sa * cb + ca * sb


def _rot(t, cos2, sin2):
    return t * cos2 + pltpu.roll(t, HEAD_DIM // 2, 1) * sin2


def _rot_bwd(d, cos2, sin2):
    return d * cos2 + pltpu.roll(d * sin2, HEAD_DIM // 2, 1)


def _token_tile(i):
    return jnp.maximum(i - 1, 0)


GATHER_PAIRS = 7


def _two_level_gather(src_ref, out_ref, send_sems, recv_sems, local_sem, base=0):
    x, y, c = lax.axis_index("x"), lax.axis_index("y"), lax.axis_index("c")
    me, sibling = (x, y, c), (x, y, 1 - c)
    xnb, ynb, diag = (1 - x, y), (x, 1 - y), (1 - x, 1 - y)
    relayed = (jnp.bitwise_xor(x, 1 - c), jnp.bitwise_xor(y, c))
    other = (jnp.bitwise_xor(x, c), jnp.bitwise_xor(y, 1 - c))

    def copy(k, block, to, src=None):
        dst = out_ref.at[4 * block[0] + 2 * block[1] + block[2]]
        return pltpu.make_async_remote_copy(
            src_ref=dst if src is None else src,
            dst_ref=dst,
            send_sem=send_sems.at[base + k],
            recv_sem=recv_sems.at[base + k],
            device_id=to,
            device_id_type=MESH,
        )

    mine = pltpu.make_async_copy(src_ref, out_ref.at[4 * x + 2 * y + c], local_sem)
    first = [copy(1, me, (*xnb, c), src=src_ref), copy(2, me, (*ynb, c), src=src_ref), copy(0, me, sibling, src=src_ref)]
    relay = copy(3, (*relayed, c), (*other, c))
    passed = [copy(4 + j, (*chip, c), sibling) for j, chip in enumerate((xnb, ynb, diag))]

    def start():
        mine.start()
        for cp in first:
            cp.start()

    def forward():
        copy(1 + c, (*relayed, c), me).wait_recv()
        relay.start()
        copy(2 - c, (*other, c), me).wait_recv()
        passed[0].start()
        passed[1].start()

    def forward_relayed():
        copy(3, (*diag, c), me).wait_recv()
        passed[2].start()

    def finish():
        copy(0, sibling, me).wait_recv()
        for j, chip in enumerate((xnb, ynb, diag)):
            copy(4 + j, (*chip, 1 - c), me).wait_recv()
        for cp in first + [relay] + passed:
            cp.wait_send()
        mine.wait()

    return start, forward, forward_relayed, finish


W_IN_CHUNKS = 4


def _gather_weights_and_norm(x2, g1, w_shard, meta, conv_w):
    per = GATHER_PAIRS
    n_ch = W_IN_CHUNKS
    rows = w_shard.shape[0] // n_ch
    tm = ROW_TILE
    nt = x2.shape[0] // tm
    cols = D_CONV // N_DEV
    pack_shape = (SC_CONV0 + 8, CHUNK)
    starts = [0] * (n_ch + 1)
    forwards = [min(5 + 2 * p, nt - 1) for p in range(n_ch)] + [min(12, nt - 1)]
    relayeds = [min(12 + p, nt - 1) for p in range(n_ch)] + [nt - 1]
    finishes = [nt - 1] * (n_ch + 1)

    def body(x_ref, g_ref, w_ref, meta_ref, conv_ref, hn_ref, hnT_ref, w_out_ref, mc_ref, cw_ref,
             w_bf, small_ref, small_all_ref, w_all_ref, send_sems, recv_sems, local_sems, out_sem):
        i = pl.program_id(0)

        @pl.when(i == 0)
        def _():
            w_bf[...] = w_ref[...].astype(BF16)
            small_ref[...] = jnp.zeros_like(small_ref)
            small_ref[SC_META0 : SC_META0 + N_META, :] = meta_ref[...]
            small_ref[SC_CONV0 : SC_CONV0 + 3, 0:cols] = conv_ref[...]

        parts = [
            _two_level_gather(
                w_bf.at[pl.ds(ch * rows, rows)],
                w_all_ref.at[:, pl.ds(ch * rows, rows)],
                send_sems,
                recv_sems,
                local_sems.at[ch],
                ch * per,
            )
            for ch in range(n_ch)
        ]
        parts.append(_two_level_gather(small_ref, small_all_ref, send_sems, recv_sems, local_sems.at[n_ch], n_ch * per))
        for phase, steps in enumerate((starts, forwards, relayeds, finishes)):
            for part, step in zip(parts, steps, strict=True):
                pl.when(i == step)(part[phase])

        h = x_ref[...]
        r = lax.rsqrt(jnp.mean(h * h, axis=-1, keepdims=True) + EPS)
        hn = (h * r * g_ref[...]).astype(BF16)
        hn_ref[...] = hn
        hnT_ref[...] = hn.T

        @pl.when(i == nt - 1)
        def _():
            out = pltpu.make_async_copy(w_all_ref, w_out_ref, out_sem)
            out.start()
            out.wait()

        @pl.when(i == nt - 1)
        def _():
            mc_ref[0 : CHUNK - N_META, :] = jnp.zeros((CHUNK - N_META, D_MODEL), F32)
            cw_ref[...] = jnp.zeros_like(cw_ref)
            for d in range(N_DEV):
                mc_ref[CHUNK - N_META : CHUNK, d * CHUNK : (d + 1) * CHUNK] = small_all_ref[d, SC_META0 : SC_META0 + N_META, :]
                lane0 = (d * cols) % CHUNK
                rows8 = small_all_ref[d, SC_CONV0 : SC_CONV0 + 8, :]
                rows8 = pltpu.roll(rows8, lane0, 1) if lane0 else rows8
                cw_ref[0:3, d * cols : (d + 1) * cols] = rows8[0:3, lane0 : lane0 + cols]

    const = lambda r, c: pl.BlockSpec((r, c), lambda i: (0, 0))
    return pl.pallas_call(
        body,
        name="gather_weights_norm",
        grid=(nt,),
        out_shape=(
            jax.ShapeDtypeStruct(x2.shape, BF16),
            jax.ShapeDtypeStruct(x2.shape[::-1], BF16),
            jax.ShapeDtypeStruct((N_DEV, *w_shard.shape), BF16),
            jax.ShapeDtypeStruct((CHUNK, D_MODEL), F32),
            jax.ShapeDtypeStruct((8, D_CONV), F32),
        ),
        in_specs=[pl.BlockSpec((tm, D_MODEL), lambda i: (i, 0)), _VMEM, _VMEM, _VMEM, _VMEM],
        out_specs=(
            pl.BlockSpec((tm, D_MODEL), lambda i: (i, 0)),
            pl.BlockSpec((D_MODEL, tm), lambda i: (0, i)),
            _HBM,
            const(CHUNK, D_MODEL),
            const(8, D_CONV),
        ),
        scratch_shapes=[
            pltpu.VMEM(w_shard.shape, BF16),
            pltpu.VMEM(pack_shape, F32),
            pltpu.VMEM((N_DEV, *pack_shape), F32),
            pltpu.VMEM((N_DEV, *w_shard.shape), BF16),
            pltpu.SemaphoreType.DMA(((n_ch + 1) * per,)),
            pltpu.SemaphoreType.DMA(((n_ch + 1) * per,)),
            pltpu.SemaphoreType.DMA((n_ch + 1,)),
            pltpu.SemaphoreType.DMA(()),
        ],
        compiler_params=_params(),
    )(x2, g1, w_shard, meta, conv_w)


def _f1(hn_tok, meta_chunk, g1, w3, cw8, rot, wo_shard):
    tm = ROW_TILE
    nt = hn_tok.shape[0] // tm + 1
    n_rows = nt * tm

    def body(hn_ref, mc_ref, g_ref, w_ref, cw_ref, ca_ref, sa_ref, cb_ref, sb_ref, wo_ref,
             hnT_ref, pr_ref, co_ref, cv_ref, wo_all_ref, halo, wo_bf, send_sems, recv_sems, local_sem):
        i = pl.program_id(0)

        @pl.when(i == 0)
        def _():
            wo_bf[...] = wo_ref[...].astype(BF16)

        wo_phases = _two_level_gather(wo_bf, wo_all_ref, send_sems, recv_sems, local_sem)
        for step, phase in zip((0, nt // 3, 2 * nt // 3, nt - 1), wo_phases, strict=True):
            pl.when(i == step)(phase)

        def work(hn, r0):
            rs = slice(r0, tm)
            n = tm - r0

            def proj(j):
                return jnp.dot(hn, w_ref[j], preferred_element_type=F32)

            cx, cb, cc, cg = proj(0), proj(1), proj(2), proj(3)
            u = cc * cx
            rows = lax.broadcasted_iota(jnp.int32, u.shape, 0)
            hl = halo[...]
            u1 = jnp.where(rows == 0, hl[7:8], pltpu.roll(u, 1, 0))
            u2 = jnp.where(rows == 0, hl[6:7], jnp.where(rows == 1, hl[7:8], pltpu.roll(u, 2, 0)))
            halo[...] = u[n - 8 : n]
            cw = cw_ref[...]
            conv = cw[0:1] * u2 + cw[1:2] * u1 + cw[2:3] * u
            co = (cb * conv * (cg * _sigmoid(cg))).astype(BF16)
            co_ref[rs, :] = co
            cv_ref[rs, :] = conv.astype(BF16)
            pr_ref[rs, 0 * BLK : 1 * BLK] = cx.astype(BF16)
            pr_ref[rs, 1 * BLK : 2 * BLK] = cb.astype(BF16)
            pr_ref[rs, 2 * BLK : 3 * BLK] = cc.astype(BF16)
            pr_ref[rs, 3 * BLK : 4 * BLK] = cg.astype(BF16)
            cos_all, sin_all = _tile_rotary(ca_ref, sa_ref, cb_ref, sb_ref, i)
            cos_t, sin_t = cos_all[rs, :], sin_all[rs, :]
            q, k = proj(4), proj(5)
            for hd in range(N_HEADS):
                c0 = hd * HEAD_DIM
                pr_ref[rs, 4 * BLK + c0 : 4 * BLK + c0 + HEAD_DIM] = (
                    _rot(q[:, c0 : c0 + HEAD_DIM], cos_t, sin_t) * Q_SCALE
                ).astype(BF16)
                pr_ref[rs, 5 * BLK + c0 : 5 * BLK + c0 + HEAD_DIM] = _rot(
                    k[:, c0 : c0 + HEAD_DIM], cos_t, sin_t
                ).astype(BF16)
            pr_ref[rs, 6 * BLK : 7 * BLK] = proj(6).astype(BF16)
            pr_ref[rs, 7 * BLK : 8 * BLK] = proj(7).astype(BF16)

        @pl.when(i == 0)
        def _():
            halo[...] = jnp.zeros_like(halo)
            h = mc_ref[...]
            r = lax.rsqrt(jnp.mean(h * h, axis=-1, keepdims=True) + EPS)
            hn = (h * r * g_ref[...]).astype(BF16)
            hnT_ref[...] = hn.T
            work(hn, LIVE0)

        @pl.when(i > 0)
        def _():
            work(hn_ref[...], 0)

    row = lambda w: pl.BlockSpec((tm, w), lambda i: (i, 0))
    return pl.pallas_call(
        body,
        name="f1_inproj_conv",
        grid=(nt,),
        in_specs=[pl.BlockSpec((tm, D_MODEL), lambda i: (_token_tile(i), 0))] + [_VMEM] * 9,
        out_specs=(
            pl.BlockSpec((D_MODEL, CHUNK), lambda i: (0, 0)),
            row(N_PROJ * BLK),
            row(D_CONV),
            row(D_CONV),
            _HBM,
        ),
        out_shape=(
            jax.ShapeDtypeStruct((D_MODEL, CHUNK), BF16),
            jax.ShapeDtypeStruct((n_rows, N_PROJ * BLK), BF16),
            jax.ShapeDtypeStruct((n_rows, D_CONV), BF16),
            jax.ShapeDtypeStruct((n_rows, D_CONV), BF16),
            jax.ShapeDtypeStruct((N_DEV, *wo_shard.shape), BF16),
        ),
        scratch_shapes=[
            pltpu.VMEM((8, D_CONV), F32),
            pltpu.VMEM(wo_shard.shape, BF16),
            pltpu.SemaphoreType.DMA((GATHER_PAIRS,)),
            pltpu.SemaphoreType.DMA((GATHER_PAIRS,)),
            pltpu.SemaphoreType.DMA(()),
        ],
        compiler_params=_params(),
    )(hn_tok, meta_chunk, g1, w3, cw8, *rot, wo_shard)


def _group_norm(o):
    ys, rs = [], []
    for hd in range(N_HEADS):
        oh = o[:, hd * HEAD_DIM : (hd + 1) * HEAD_DIM]
        xc = oh - jnp.mean(oh, axis=-1, keepdims=True)
        rstd = lax.rsqrt(jnp.mean(xc * xc, axis=-1, keepdims=True) + EPS)
        ys.append(xc * rstd)
        rs.append(jnp.broadcast_to(rstd, oh.shape))
    return jnp.concatenate(ys, axis=1), jnp.concatenate(rs, axis=1)


def _f2(proj, conv_out, x2, t2, wout, gr, gf, dec, zeta, xi):
    n_rows = proj.shape[0]
    tm = ROW_TILE
    nt = n_rows // tm
    nct = tm // CHUNK

    def body(q_ref, k_ref, v_ref, rg_ref, co_ref, x_ref, t_ref, wo_ref, gr_ref, gf_ref, dec_ref, zeta_ref,
             xi_ref, dh2_ref, o_ref, st_ref, gw_ref, acc_ref, state, gacc):
        i = pl.program_id(0)

        def mixer(r0):
            chunks = range(r0 // CHUNK, nct)
            heads = []
            for hd in range(N_HEADS):
                cs = slice(hd * HEAD_DIM, (hd + 1) * HEAD_DIM)
                upd = {}
                for c in chunks:
                    rs = slice(c * CHUNK, (c + 1) * CHUNK)
                    kz = (k_ref[rs, cs].astype(F32) * zeta_ref[hd]).astype(BF16)
                    upd[c] = lax.dot_general(kz, v_ref[rs, cs], TN, preferred_element_type=F32)
                st = state[hd]
                outs = []
                for c in chunks:
                    rs = slice(c * CHUNK, (c + 1) * CHUNK)
                    q, k, v = q_ref[rs, cs], k_ref[rs, cs], v_ref[rs, cs]
                    st_bf = st.astype(BF16)
                    st_ref[c, hd] = st_bf
                    s = lax.dot_general(q, k, NT, preferred_element_type=F32) * dec_ref[hd]
                    inner = jnp.dot(s.astype(BF16), v, preferred_element_type=F32)
                    qx = (q.astype(F32) * xi_ref[hd]).astype(BF16)
                    outs.append(inner + jnp.dot(qx, st_bf, preferred_element_type=F32))
                    st = CHUNK_DECAY[hd] * st + upd[c]
                state[hd] = st
                heads.append(jnp.concatenate(outs, axis=0) if len(outs) > 1 else outs[0])
            o = jnp.concatenate(heads, axis=1)
            o_ref[r0:tm, :] = o.astype(BF16)
            yh, _ = _group_norm(o)
            rg = rg_ref[r0:tm, :].astype(F32)
            return (yh * gr_ref[...] * (rg * _sigmoid(rg))).astype(BF16)

        @pl.when(i == 0)
        def _():
            state[...] = jnp.zeros_like(state)
            acc_ref[...] = jnp.zeros_like(acc_ref)
            gacc[...] = jnp.zeros_like(gacc)
            mixer(LIVE0)
            dh2_ref[...] = jnp.zeros_like(dh2_ref)

        @pl.when(i > 0)
        def _():
            ro = mixer(0)
            h2 = (
                x_ref[...]
                + jnp.dot(co_ref[...], wo_ref[0:D_CONV], preferred_element_type=F32)
                + jnp.dot(ro, wo_ref[D_CONV:], preferred_element_type=F32)
            )
            r2 = lax.rsqrt(jnp.mean(h2 * h2, axis=-1, keepdims=True) + EPS)
            yn = h2 * r2
            gfv = gf_ref[...]
            err = yn * gfv - t_ref[...]
            tile_loss = jnp.sum(jnp.sum(err * err, axis=-1, keepdims=True), axis=0, keepdims=True) * (0.5 / D_MODEL)
            acc_ref[0:1, :] += jnp.sum(err * yn, axis=0, keepdims=True) * (1.0 / D_MODEL)
            acc_ref[1:2, :] += tile_loss
            dyn = err * (gfv * (1.0 / D_MODEL))
            dh2 = r2 * (dyn - yn * jnp.mean(dyn * yn, axis=-1, keepdims=True))
            dh2_ref[...] = dh2
            dh2_bf = dh2.astype(BF16)
            gacc[0:D_CONV, :] += lax.dot_general(co_ref[...], dh2_bf, TN, preferred_element_type=F32)
            gacc[D_CONV:, :] += lax.dot_general(ro, dh2_bf, TN, preferred_element_type=F32)

        @pl.when(i == nt - 1)
        def _():
            gw_ref[...] = gacc[...].astype(BF16)

    row = lambda w, j=0: pl.BlockSpec((tm, w), lambda i: (i, j))
    tok = pl.BlockSpec((tm, D_MODEL), lambda i: (_token_tile(i), 0))
    return pl.pallas_call(
        body,
        name="f2_retention_out",
        grid=(nt,),
        in_specs=[row(BLK, 4), row(BLK, 5), row(BLK, 6), row(BLK, 7), row(D_CONV), tok, tok] + [_VMEM] * 6,
        out_specs=(
            row(D_MODEL),
            row(D_RET),
            pl.BlockSpec((nct, N_HEADS, HEAD_DIM, HEAD_DIM), lambda i: (i, 0, 0, 0)),
            pl.BlockSpec((D_MODEL, D_MODEL), lambda i: (0, 0)),
            pl.BlockSpec((8, D_MODEL), lambda i: (0, 0)),
        ),
        out_shape=(
            jax.ShapeDtypeStruct((n_rows, D_MODEL), F32),
            jax.ShapeDtypeStruct((n_rows, D_RET), BF16),
            jax.ShapeDtypeStruct((n_rows // CHUNK, N_HEADS, HEAD_DIM, HEAD_DIM), BF16),
            jax.ShapeDtypeStruct((D_MODEL, D_MODEL), BF16),
            jax.ShapeDtypeStruct((8, D_MODEL), F32),
        ),
        scratch_shapes=[
            pltpu.VMEM((N_HEADS, HEAD_DIM, HEAD_DIM), F32),
            pltpu.VMEM((D_MODEL, D_MODEL), F32),
        ],
        compiler_params=_params(),
    )(proj, proj, proj, proj, conv_out, x2, t2, wout, gr, gf, dec, zeta, xi)


def _b1(dh2, proj, o, states, wout, gr, tables, zeta0, rot):
    n_rows = dh2.shape[0]
    tm = ROW_TILE
    nt = n_rows // tm
    chunk = B1_CHUNK
    nct = tm // chunk
    decay = tuple(math.exp(chunk * lg) for lg in LOG_G)

    def body(dh2_ref, pr_ref, o_ref, st_ref, wo_ref, gr_ref, dec_ref, dect_ref, zeta_ref, xi_ref, zeta0_ref,
             ca_ref, sa_ref, cb_ref, sb_ref, dp_ref, dco_ref, acc_ref, dstate):
        i = pl.program_id(0)
        tile = nt - 1 - i
        cos_all, sin_all = _tile_rotary(ca_ref, sa_ref, cb_ref, sb_ref, tile)

        @pl.when(i == 0)
        def _():
            dstate[...] = jnp.zeros_like(dstate)
            acc_ref[...] = jnp.zeros_like(acc_ref)

        @pl.when(tile > 0)
        def _():
            dmix = lax.dot_general(dh2_ref[...].astype(BF16), wo_ref[...], NT, preferred_element_type=F32)
            dco_ref[...] = dmix[:, :D_CONV].astype(BF16)
            dro = dmix[:, D_CONV:]
            rg = pr_ref[:, 3 * BLK : 4 * BLK].astype(F32)
            sg = _sigmoid(rg)
            silu, dsilu = rg * sg, sg * (1.0 + rg * (1.0 - sg))
            yh, rstd = _group_norm(o_ref[...].astype(F32))
            grv = gr_ref[...]
            dp_ref[:, 3 * BLK : 4 * BLK] = (dro * (yh * grv) * dsilu).astype(BF16)
            dret = dro * silu
            acc_ref[0:1, :] += jnp.sum(dret * yh, axis=0, keepdims=True)
            dyh = dret * grv
            for hd in range(N_HEADS):
                cs = slice(hd * HEAD_DIM, (hd + 1) * HEAD_DIM)
                a, b = dyh[:, cs], yh[:, cs]
                do_head = (
                    rstd[:, cs]
                    * (a - jnp.mean(a, axis=-1, keepdims=True) - b * jnp.mean(a * b, axis=-1, keepdims=True))
                ).astype(BF16)
                dupd = {}
                for c in range(nct):
                    rs = slice(c * chunk, (c + 1) * chunk)
                    qx = (pr_ref[rs, cs].astype(F32) * xi_ref[hd]).astype(BF16)
                    dupd[c] = lax.dot_general(qx, do_head[rs, :], TN, preferred_element_type=F32)
                dst = dstate[hd]
                for c in reversed(range(nct)):
                    rs = slice(c * chunk, (c + 1) * chunk)
                    cos_t, sin_t = cos_all[rs, :], sin_all[rs, :]
                    q = pr_ref[rs, hd * HEAD_DIM : (hd + 1) * HEAD_DIM]
                    k = pr_ref[rs, BLK + hd * HEAD_DIM : BLK + (hd + 1) * HEAD_DIM]
                    v = pr_ref[rs, 2 * BLK + hd * HEAD_DIM : 2 * BLK + (hd + 1) * HEAD_DIM]
                    do = do_head[rs, :]
                    st_bf = st_ref[c * (chunk // CHUNK), hd]
                    dst_bf = dst.astype(BF16)
                    zt, xt = zeta_ref[hd], xi_ref[hd]
                    sT = (lax.dot_general(k, q, NT, preferred_element_type=F32) * dect_ref[hd]).astype(BF16)
                    dsT = (lax.dot_general(v, do, NT, preferred_element_type=F32) * dect_ref[hd]).astype(BF16)
                    ds = (lax.dot_general(do, v, NT, preferred_element_type=F32) * dec_ref[hd]).astype(BF16)
                    kz = (k.astype(F32) * zt).astype(BF16)
                    dv = jnp.dot(sT, do, preferred_element_type=F32) + jnp.dot(kz, dst_bf, preferred_element_type=F32)
                    dq = jnp.dot(ds, k, preferred_element_type=F32) + xt * lax.dot_general(
                        do, st_bf, NT, preferred_element_type=F32
                    )
                    dk = jnp.dot(dsT, q, preferred_element_type=F32) + zt * lax.dot_general(
                        v, dst_bf, NT, preferred_element_type=F32
                    )
                    dst = decay[hd] * dst + dupd[c]
                    dp_ref[rs, hd * HEAD_DIM : (hd + 1) * HEAD_DIM] = _rot_bwd(dq * Q_SCALE, cos_t, sin_t).astype(BF16)
                    dp_ref[rs, BLK + hd * HEAD_DIM : BLK + (hd + 1) * HEAD_DIM] = _rot_bwd(dk, cos_t, sin_t).astype(BF16)
                    dp_ref[rs, 2 * BLK + hd * HEAD_DIM : 2 * BLK + (hd + 1) * HEAD_DIM] = dv.astype(BF16)
                dstate[hd] = dst

        @pl.when(tile == 0)
        def _():
            rs = slice(LIVE0, tm)
            cos_t, sin_t = cos_all[rs, :], sin_all[rs, :]
            zeros = jnp.zeros((tm - LIVE0, HEAD_DIM), BF16)
            for hd in range(N_HEADS):
                k = pr_ref[rs, BLK + hd * HEAD_DIM : BLK + (hd + 1) * HEAD_DIM]
                v = pr_ref[rs, 2 * BLK + hd * HEAD_DIM : 2 * BLK + (hd + 1) * HEAD_DIM]
                dst_bf = dstate[hd].astype(BF16)
                zt = zeta0_ref[hd]
                kz = (k.astype(F32) * zt).astype(BF16)
                dv = jnp.dot(kz, dst_bf, preferred_element_type=F32)
                dk = zt * lax.dot_general(v, dst_bf, NT, preferred_element_type=F32)
                dp_ref[rs, hd * HEAD_DIM : (hd + 1) * HEAD_DIM] = zeros
                dp_ref[rs, BLK + hd * HEAD_DIM : BLK + (hd + 1) * HEAD_DIM] = _rot_bwd(dk, cos_t, sin_t).astype(BF16)
                dp_ref[rs, 2 * BLK + hd * HEAD_DIM : 2 * BLK + (hd + 1) * HEAD_DIM] = dv.astype(BF16)
                dp_ref[rs, 3 * BLK + hd * HEAD_DIM : 3 * BLK + (hd + 1) * HEAD_DIM] = zeros
            dco_ref[rs, :] = jnp.zeros((tm - LIVE0, D_CONV), BF16)

    half = N_PROJ * BLK // 2
    rev = lambda w, j=0: pl.BlockSpec((tm, w), lambda i: (nt - 1 - i, j))
    return pl.pallas_call(
        body,
        name="b1_dret",
        grid=(nt,),
        in_specs=[
            rev(D_MODEL),
            rev(half, 1),
            rev(D_RET),
            pl.BlockSpec((tm // CHUNK, N_HEADS, HEAD_DIM, HEAD_DIM), lambda i: (nt - 1 - i, 0, 0, 0)),
        ]
        + [_VMEM] * 11,
        out_specs=(rev(half, 1), rev(D_CONV), pl.BlockSpec((8, D_RET), lambda i: (0, 0))),
        out_shape=(
            jax.ShapeDtypeStruct((n_rows, N_PROJ * BLK), BF16),
            jax.ShapeDtypeStruct((n_rows, D_CONV), BF16),
            jax.ShapeDtypeStruct((8, D_RET), F32),
        ),
        scratch_shapes=[pltpu.VMEM((N_HEADS, HEAD_DIM, HEAD_DIM), F32)],
        compiler_params=_params(),
    )(dh2, proj, o, states, wout, gr, *tables, zeta0, *rot)


def _b2(dproj, dco, proj, conv, w3, cw8, x2, meta_chunk, dh2, g1):
    n_rows = dproj.shape[0]
    tm = ROW_TILE
    nt = n_rows // tm
    half = N_PROJ * BLK // 2

    def body(dpr_ref, dco_ref, pr_ref, cv_ref, w_ref, cw_ref, x_ref, mc_ref, dh2_ref, g_ref,
             dpc_ref, gx_ref, dm_ref, acc_ref, accc_ref, halo):
        i = pl.program_id(0)
        tile = nt - 1 - i

        @pl.when(i == 0)
        def _():
            acc_ref[...] = jnp.zeros_like(acc_ref)
            accc_ref[...] = jnp.zeros_like(accc_ref)
            halo[...] = jnp.zeros_like(halo)

        def work(h, r0):
            rs = slice(r0, tm)
            n = tm - r0
            dco = dco_ref[rs, :].astype(F32)
            cx = pr_ref[rs, 0 * BLK : 1 * BLK].astype(F32)
            cb = pr_ref[rs, 1 * BLK : 2 * BLK].astype(F32)
            cc = pr_ref[rs, 2 * BLK : 3 * BLK].astype(F32)
            cg = pr_ref[rs, 3 * BLK : 4 * BLK].astype(F32)
            conv = cv_ref[rs, :].astype(F32)
            sg = _sigmoid(cg)
            silu, dsilu = cg * sg, sg * (1.0 + cg * (1.0 - sg))
            t = dco * cb
            dcb = (dco * conv * silu).astype(BF16)
            dcg = (t * conv * dsilu).astype(BF16)
            dconv = t * silu
            rows = lax.broadcasted_iota(jnp.int32, dconv.shape, 0)
            hl = halo[...]
            dc1 = jnp.where(rows == n - 1, hl[0:1], pltpu.roll(dconv, n - 1, 0))
            dc2 = jnp.where(rows == n - 2, hl[0:1], jnp.where(rows == n - 1, hl[1:2], pltpu.roll(dconv, n - 2, 0)))
            halo[...] = dconv[0:8]
            cw = cw_ref[...]
            du = cw[2:3] * dconv + cw[1:2] * dc1 + cw[0:1] * dc2
            u = cc * cx
            accc_ref[0:1, :] += jnp.sum(u * dc2, axis=0, keepdims=True)
            accc_ref[1:2, :] += jnp.sum(u * dc1, axis=0, keepdims=True)
            accc_ref[2:3, :] += jnp.sum(u * dconv, axis=0, keepdims=True)
            dcx = (du * cc).astype(BF16)
            dcc = (du * cx).astype(BF16)
            dhn = lax.dot_general(dpr_ref[rs, 0:BLK], w_ref[4], NT, preferred_element_type=F32)
            for j in range(1, 4):
                dhn += lax.dot_general(dpr_ref[rs, j * BLK : (j + 1) * BLK], w_ref[4 + j], NT, preferred_element_type=F32)
            for blk, d in ((1, dcb), (3, dcg), (0, dcx), (2, dcc)):
                dpc_ref[rs, blk * BLK : (blk + 1) * BLK] = d
                dhn += lax.dot_general(d, w_ref[blk], NT, preferred_element_type=F32)
            r = lax.rsqrt(jnp.mean(h * h, axis=-1, keepdims=True) + EPS)
            hh = h * r
            acc_ref[0:1, :] += jnp.sum(dhn * hh, axis=0, keepdims=True)
            dg = dhn * g_ref[...]
            return dh2_ref[rs, :] + r * (dg - hh * jnp.mean(dg * hh, axis=-1, keepdims=True))

        @pl.when(tile == 0)
        def _():
            dh = work(mc_ref[...], LIVE0)
            dm_ref[...] = dh[CHUNK - N_META : CHUNK]

        @pl.when(tile > 0)
        def _():
            gx_ref[...] = work(x_ref[...], 0)

    rev = lambda w, j=0: pl.BlockSpec((tm, w), lambda i: (nt - 1 - i, j))
    tok = pl.BlockSpec((tm, D_MODEL), lambda i: (_token_tile(nt - 1 - i), 0))
    const = lambda r, c: pl.BlockSpec((r, c), lambda i: (0, 0))
    return pl.pallas_call(
        body,
        name="b2_dconv_dh",
        grid=(nt,),
        in_specs=[rev(half, 1), rev(D_CONV), rev(half, 0), rev(D_CONV), _VMEM, _VMEM, tok, _VMEM, rev(D_MODEL), _VMEM],
        out_specs=(rev(half, 0), tok, const(N_META, D_MODEL), const(8, D_MODEL), const(8, D_CONV)),
        out_shape=(
            jax.ShapeDtypeStruct(dproj.shape, BF16),
            jax.ShapeDtypeStruct(x2.shape, F32),
            jax.ShapeDtypeStruct((N_META, D_MODEL), F32),
            jax.ShapeDtypeStruct((8, D_MODEL), F32),
            jax.ShapeDtypeStruct((8, D_CONV), F32),
        ),
        scratch_shapes=[pltpu.VMEM((8, D_CONV), F32)],
        input_output_aliases={0: 0},
        compiler_params=_params(),
    )(dproj, dco, proj, conv, w3, cw8, x2, meta_chunk, dh2, g1)


def _gw_in_scatter(hnT_tok, hnT_meta, dproj, gw_out_parts, dmeta_rows, acc_conv, acc_b2, acc_b1, acc_f2, me_arr):
    n_rows = dproj.shape[0]
    last = N_DEV - 1
    cols = D_CONV // N_DEV
    by_dest = (True, True, False)
    small_blocks = (gw_out_parts.shape[1:], (SC_CONV0 + 8, CHUNK), (GA_LOSS + 8, CHUNK))

    def body(me_ref, a_ref, am_ref, b_ref, go_ref, dm_ref, ac_ref, ab2_ref, ab1_ref, af2_ref,
             land_in, land_go, land_sc, land_ga, d2d_buf, d2d_land, ici_buf, sc_ref, ga_ref,
             d2d_send, d2d_recv, ici_send, ici_recv, send_sems, recv_sems, local_sems):
        del me_ref
        s = pl.program_id(0)
        t = last - s
        q = t >> 1
        x, y, c = lax.axis_index("x"), lax.axis_index("y"), lax.axis_index("c")
        me = 4 * x + 2 * y + c
        chip = 2 * x + y
        srcs, lands = (go_ref, sc_ref, ga_ref), (land_go, land_sc, land_ga)

        def peer_at(k):
            return (1 - x if k & 4 else x, 1 - y if k & 2 else y, 1 - c if k & 1 else c)

        def small_copy(a, k):
            px, py, pc = peer_at(k)
            return pltpu.make_async_remote_copy(
                src_ref=srcs[a].at[4 * px + 2 * py + pc] if by_dest[a] else srcs[a],
                dst_ref=lands[a].at[me],
                send_sem=send_sems.at[a * last + k - 1],
                recv_sem=recv_sems.at[a * last + k - 1],
                device_id=(px, py, pc),
                device_id_type=MESH,
            )

        def small_local(a):
            return pltpu.make_async_copy(srcs[a].at[me] if by_dest[a] else srcs[a], lands[a].at[me], local_sems.at[a + 1])

        def d2d_copy(j):
            return pltpu.make_async_remote_copy(
                src_ref=d2d_buf.at[j],
                dst_ref=d2d_land.at[j],
                send_sem=d2d_send.at[j],
                recv_sem=d2d_recv.at[j],
                device_id=(x, y, 1 - c),
                device_id_type=MESH,
            )

        def ici_copy(j, to):
            return pltpu.make_async_remote_copy(
                src_ref=ici_buf.at[j],
                dst_ref=land_in.at[chip],
                send_sem=ici_send.at[j],
                recv_sem=ici_recv.at[j],
                device_id=to,
                device_id_type=MESH,
            )

        def own_copy():
            return pltpu.make_async_copy(ici_buf.at[0], land_in.at[chip], local_sems.at[0])

        @pl.when(s == 0)
        def _():
            sc_ref[...] = jnp.zeros_like(sc_ref)
            for d in range(N_DEV):
                sc_ref[d, SC_META0 : SC_META0 + N_META, :] = dm_ref[:, d * CHUNK : (d + 1) * CHUNK]
                lane0 = (d * cols) % CHUNK
                rows8 = ac_ref[:, d * cols - lane0 : d * cols - lane0 + CHUNK]
                rows8 = pltpu.roll(rows8, CHUNK - lane0, 1) if lane0 else rows8
                sc_ref[d, SC_CONV0 : SC_CONV0 + 3, 0:cols] = rows8[0:3, 0:cols]
            ga_ref[...] = jnp.zeros_like(ga_ref)
            for row0, acc, n in ((GA_N1, ab2_ref, 8), (GA_RG, ab1_ref, 4), (GA_FG, af2_ref, 8)):
                for j in range(n):
                    ga_ref[row0 + j : row0 + j + 1, :] = acc[0:1, j * CHUNK : (j + 1) * CHUNK]
            ga_ref[GA_LOSS : GA_LOSS + 1, :] = af2_ref[1:2, 0:CHUNK]
            for a in range(3):
                small_local(a).start()
            for k in range(last, 0, -1):
                for a in range(3):
                    small_copy(a, k).start()

        blk = jnp.dot(a_ref[...], b_ref[ROW_TILE:, :], preferred_element_type=F32)
        blk += jnp.dot(am_ref[...], b_ref[LIVE0:ROW_TILE, :], preferred_element_type=F32)

        @pl.when((t & 1) == 1)
        def _():
            d2d_buf[q] = blk.astype(BF16)
            d2d_copy(q).start()

        @pl.when((t & 1) == 0)
        def _():
            d2d_copy(q).wait_recv()
            ici_buf[q] = (blk + d2d_land[q].astype(F32)).astype(BF16)

            @pl.when(t != 0)
            def _():
                ici_copy(q, (jnp.bitwise_xor(x, (t >> 2) & 1), jnp.bitwise_xor(y, (t >> 1) & 1), c)).start()

            @pl.when(t == 0)
            def _():
                own_copy().start()
                for j in range(N_CHIPS):
                    d2d_copy(j).wait_send()
                for j in range(N_CHIPS - 1, 0, -1):
                    ici_copy(j, peer_at(2 * j)).wait()
                for k in range(last, 0, -1):
                    for a in range(3):
                        small_copy(a, k).wait()
                for a in range(3):
                    small_local(a).wait()
                own_copy().wait()

    grid_spec = pltpu.PrefetchScalarGridSpec(
        num_scalar_prefetch=1,
        grid=(N_DEV,),
        in_specs=[_VMEM, _VMEM, pl.BlockSpec((n_rows, BLK), lambda s, me: (0, jnp.bitwise_xor(me[0], last - s))), _HBM]
        + [_VMEM] * 5,
        out_specs=tuple([_HBM] * 4),
        scratch_shapes=[
            pltpu.VMEM((N_CHIPS, D_MODEL, BLK), BF16),
            pltpu.VMEM((N_CHIPS, D_MODEL, BLK), BF16),
            pltpu.VMEM((N_CHIPS, D_MODEL, BLK), BF16),
            pltpu.VMEM((N_DEV, *small_blocks[1]), F32),
            pltpu.VMEM(small_blocks[2], F32),
            pltpu.SemaphoreType.DMA((N_CHIPS,)),
            pltpu.SemaphoreType.DMA((N_CHIPS,)),
            pltpu.SemaphoreType.DMA((N_CHIPS,)),
            pltpu.SemaphoreType.DMA((N_CHIPS,)),
            pltpu.SemaphoreType.DMA((3 * last,)),
            pltpu.SemaphoreType.DMA((3 * last,)),
            pltpu.SemaphoreType.DMA((4,)),
        ],
    )
    return pl.pallas_call(
        body,
        name="gw_in_scatter",
        grid_spec=grid_spec,
        out_shape=(jax.ShapeDtypeStruct((N_CHIPS, D_MODEL, BLK), BF16),)
        + tuple(jax.ShapeDtypeStruct((N_DEV, *b), dt) for b, dt in zip(small_blocks, (gw_out_parts.dtype, F32, F32), strict=True)),
        compiler_params=_params(),
    )(me_arr, hnT_tok, hnT_meta, dproj, gw_out_parts, dmeta_rows, acc_conv, acc_b2, acc_b1, acc_f2)


def _adamw(w, g, m, v):
    m = ADAM_B1 * m + (1.0 - ADAM_B1) * g
    v = ADAM_B2 * v + (1.0 - ADAM_B2) * (g * g)
    m_hat = m / (1.0 - ADAM_B1**ADAM_STEP)
    v_hat = v / (1.0 - ADAM_B2**ADAM_STEP)
    delta = -ADAM_LR * (m_hat / (jnp.sqrt(v_hat) + ADAM_EPS) + ADAM_WD * w)
    return delta, m, v


def _sum_adamw(name, parts, w, m, v, rows_per_step):
    n_r, n_c = w.shape
    n_parts = parts.shape[0]
    tr = rows_per_step

    def body(p_ref, w_ref, m_ref, v_ref, g_ref, d_ref, nm_ref, nv_ref):
        g = p_ref[0].astype(F32)
        for s in range(1, n_parts):
            g = g + p_ref[s].astype(F32)
        g_ref[...] = g
        d_ref[...], nm_ref[...], nv_ref[...] = _adamw(w_ref[...], g, m_ref[...], v_ref[...])

    blk = pl.BlockSpec((tr, n_c), lambda i: (i, 0))
    return pl.pallas_call(
        body,
        name=name,
        grid=(n_r // tr,),
        in_specs=[pl.BlockSpec((n_parts, tr, n_c), lambda i: (0, i, 0)), blk, blk, blk],
        out_specs=(blk,) * 4,
        out_shape=(jax.ShapeDtypeStruct((n_r, n_c), F32),) * 4,
        compiler_params=_params(),
    )(parts, w, m, v)


def _small_leaves(meta, conv_w, n1, rg, fg):
    return meta, conv_w, n1.reshape(8, CHUNK), rg.reshape(4, CHUNK), fg.reshape(8, CHUNK)


def _from_small_leaves(meta, conv_w, n1, rg, fg):
    return meta, conv_w, n1.reshape(D_MODEL), rg.reshape(D_RET), fg.reshape(D_MODEL)


def _adamw_small(land_sc, land_ga, w, m, v):
    n_leaf = 5

    def body(sc_ref, ga_ref, *refs):
        ins, outs = refs[: 3 * n_leaf], refs[3 * n_leaf :]
        sc, ga = sc_ref[0], ga_ref[0]
        for s in range(1, N_DEV):
            sc = sc + sc_ref[s]
            ga = ga + ga_ref[s]
        grads = (
            sc[SC_META0 : SC_META0 + N_META],
            sc[SC_CONV0 : SC_CONV0 + 3, 0 : D_CONV // N_DEV],
            ga[GA_N1 : GA_N1 + 8],
            ga[GA_RG : GA_RG + 4],
            ga[GA_FG : GA_FG + 8],
        )
        for leaf, g in enumerate(grads):
            d, nm, nv = _adamw(ins[leaf][...], g, ins[n_leaf + leaf][...], ins[2 * n_leaf + leaf][...])
            outs[leaf][...] = g
            outs[n_leaf + leaf][...] = d
            outs[2 * n_leaf + leaf][...] = nm
            outs[3 * n_leaf + leaf][...] = nv
        outs[4 * n_leaf][...] = ga[GA_LOSS : GA_LOSS + 1]

    leaf_shapes = tuple(jax.ShapeDtypeStruct(a.shape, F32) for a in w)
    out = pl.pallas_call(
        body,
        name="adamw_small",
        out_shape=leaf_shapes * 4 + (jax.ShapeDtypeStruct((1, CHUNK), F32),),
    )(land_sc, land_ga, *w, *m, *v)
    return tuple(out[k * n_leaf : (k + 1) * n_leaf] for k in range(4)), out[4 * n_leaf]


def kernel(x, meta, norm1_g, w_in, conv_w, ret_norm_g, w_out, final_g, loss_target, m_meta, m_norm1_g, m_w_in, m_conv_w, m_ret_norm_g, m_w_out, m_final_g, v_meta, v_norm1_g, v_w_in, v_conv_w, v_ret_norm_g, v_w_out, v_final_g):
    seq = x.shape[1]
    assert x.shape == (1, seq, D_MODEL) and seq % ROW_TILE == 0
    n_tiles = seq // ROW_TILE + 1
    x2, t2 = x[0], loss_target[0]

    small_w = _small_leaves(meta, conv_w, norm1_g, ret_norm_g, final_g)
    g1 = norm1_g.reshape(1, D_MODEL)
    hn_tok, hnT_tok, w3, meta_chunk, cw8 = _gather_weights_and_norm(x2, g1, w_in, meta, conv_w)

    rot = _rotary_tables(n_tiles, ROW_TILE)
    dec, dect, zeta, xi = _decay_tables()
    gr = ret_norm_g.reshape(1, D_RET)
    gf = final_g.reshape(1, D_MODEL)

    hnT_meta, proj, conv_out, conv, wo3 = _f1(hn_tok, meta_chunk, g1, w3, cw8, rot, w_out)
    wout = wo3.reshape(D_MODEL, D_MODEL)
    dh2, o, states, gw_out, acc_f2 = _f2(proj, conv_out, x2, t2, wout, gr, gf, dec, zeta, xi)
    dproj_ret, dco, acc_b1 = _b1(dh2, proj, o, states, wout, gr, _decay_tables(B1_CHUNK), zeta, rot)
    dproj, grad_x2, dmeta_rows, acc_b2, acc_conv = _b2(dproj_ret, dco, proj, conv, w3, cw8, x2, meta_chunk, dh2, g1)
    gw_out_parts = gw_out.reshape(N_DEV, D_MODEL // N_DEV, D_MODEL)

    me_arr = (4 * lax.axis_index("x") + 2 * lax.axis_index("y") + lax.axis_index("c")).astype(jnp.int32).reshape(1)
    land_in, land_out, land_sc, land_ga = _gw_in_scatter(
        hnT_tok, hnT_meta, dproj, gw_out_parts, dmeta_rows, acc_conv, acc_b2, acc_b1, acc_f2, me_arr
    )

    g_w_in, d_w_in, nm_w_in, nv_w_in = _sum_adamw("adamw_w_in", land_in, w_in, m_w_in, v_w_in, 256)
    g_w_out, d_w_out, nm_w_out, nv_w_out = _sum_adamw("adamw_w_out", land_out, w_out, m_w_out, v_w_out, 64)
    small_out, loss_row = _adamw_small(
        land_sc,
        land_ga,
        small_w,
        _small_leaves(m_meta, m_conv_w, m_norm1_g, m_ret_norm_g, m_final_g),
        _small_leaves(v_meta, v_conv_w, v_norm1_g, v_ret_norm_g, v_final_g),
    )
    loss = loss_row[0, 0]
    grad_x = grad_x2.reshape(1, seq, D_MODEL)

    def leaves(w_in_leaf, w_out_leaf, small_leaves):
        meta_leaf, conv_leaf, n1_leaf, rg_leaf, fg_leaf = _from_small_leaves(*small_leaves)
        return (meta_leaf, n1_leaf, w_in_leaf, conv_leaf, rg_leaf, w_out_leaf, fg_leaf)

    return (
        loss,
        grad_x,
        *leaves(g_w_in, g_w_out, small_out[0]),
        *leaves(d_w_in, d_w_out, small_out[1]),
        *leaves(nm_w_in, nm_w_out, small_out[2]),
        *leaves(nv_w_in, nv_w_out, small_out[3]),
    )
```

```python
import math

import jax
import jax.numpy as jnp
import numpy as np
from jax import lax
from jax.experimental import pallas as pl
from jax.experimental.pallas import tpu as pltpu

F32 = jnp.float32
BF16 = jnp.bfloat16

N_DEV = 8
N_CHIPS = 4
D_MODEL = 1024
N_META = 16
CHUNK = 128
D_CONV = 512
D_RET = 512
N_HEADS = 4
HEAD_DIM = 128
N_PROJ = 8
BLK = 512
ROPE_BASE = 10000.0
EPS = 1e-6
Q_SCALE = HEAD_DIM ** -0.5
LOG_G = tuple(math.log(1.0 - 2.0 ** (-5.0 - h)) for h in range(N_HEADS))
CHUNK_DECAY = tuple(math.exp(CHUNK * lg) for lg in LOG_G)

ADAM_LR = 0.001
ADAM_B1 = 0.9
ADAM_B2 = 0.999
ADAM_EPS = 1e-08
ADAM_WD = 0.01
ADAM_STEP = 10

ROW_TILE = 512
PAD_ROWS = ROW_TILE - N_META
LIVE0 = ROW_TILE - CHUNK
B1_CHUNK = 256
VMEM_LIMIT = 56 * 1024 * 1024

SC_META0, SC_CONV0 = 0, 16
GA_N1, GA_RG, GA_FG, GA_LOSS = 0, 8, 16, 24

NT = (((1,), (1,)), ((), ()))
TN = (((0,), (0,)), ((), ()))
MESH = pl.DeviceIdType.MESH

_VMEM = pl.BlockSpec(memory_space=pltpu.VMEM)
_HBM = pl.BlockSpec(memory_space=pltpu.HBM)


def _params(n_axes=1):
    return pltpu.CompilerParams(dimension_semantics=("arbitrary",) * n_axes, vmem_limit_bytes=VMEM_LIMIT)


def _sigmoid(x):
    return 0.5 * jnp.tanh(0.5 * x) + 0.5


def _decay_tables(chunk=CHUNK):
    idx = np.arange(chunk, dtype=np.float64)
    diff = idx[:, None] - idx[None, :]
    dec = np.stack([np.where(diff >= 0, np.exp(diff * lg), 0.0) for lg in LOG_G])
    zeta = np.stack([np.exp((chunk - 1 - idx) * lg) for lg in LOG_G])
    xi = np.stack([np.exp((idx + 1.0) * lg) for lg in LOG_G])
    ones = np.ones((1, 1, HEAD_DIM))
    return (
        jnp.asarray(dec, F32),
        jnp.asarray(dec.transpose(0, 2, 1), F32),
        jnp.asarray(zeta[:, :, None] * ones, F32),
        jnp.asarray(xi[:, :, None] * ones, F32),
    )


def _rotary_tables(n_tiles, tm):
    half = HEAD_DIM // 2
    freqs = (1.0 / (np.float32(ROPE_BASE) ** (np.arange(half, dtype=np.float32) / np.float32(half)))).astype(np.float64)
    sign = np.concatenate([-np.ones(half), np.ones(half)])
    two = lambda a: np.concatenate([a, a], axis=1)
    base = two((np.arange(n_tiles, dtype=np.float64) * tm - PAD_ROWS)[:, None] * freqs[None, :])
    off = two(np.arange(tm, dtype=np.float64)[:, None] * freqs[None, :])
    as32 = lambda a: jnp.asarray(a, F32)
    return as32(np.cos(base)), as32(np.sin(base) * sign), as32(np.cos(off)), as32(np.sin(off) * sign)


def _tile_rotary(ca_ref, sa_ref, cb_ref, sb_ref, tile):
    ca, sa = ca_ref[pl.ds(tile, 1), :], sa_ref[pl.ds(tile, 1), :]
    cb, sb = cb_ref[...], sb_ref[...]
    return ca * cb - sa * sb, sa * cb + ca * sb


def _rot(t, cos2, sin2):
    return t * cos2 + pltpu.roll(t, HEAD_DIM // 2, 1) * sin2


def _rot_bwd(d, cos2, sin2):
    return d * cos2 + pltpu.roll(d * sin2, HEAD_DIM // 2, 1)


def _token_tile(i):
    return jnp.maximum(i - 1, 0)


GATHER_PAIRS = 7


def _two_level_gather(src_ref, out_ref, send_sems, recv_sems, local_sem, base=0):
    x, y, c = lax.axis_index("x"), lax.axis_index("y"), lax.axis_index("c")
    me, sibling = (x, y, c), (x, y, 1 - c)
    xnb, ynb, diag = (1 - x, y), (x, 1 - y), (1 - x, 1 - y)
    relayed = (jnp.bitwise_xor(x, 1 - c), jnp.bitwise_xor(y, c))
    other = (jnp.bitwise_xor(x, c), jnp.bitwise_xor(y, 1 - c))

    def copy(k, block, to, src=None):
        dst = out_ref.at[4 * block[0] + 2 * block[1] + block[2]]
        return pltpu.make_async_remote_copy(
            src_ref=dst if src is None else src,
            dst_ref=dst,
            send_sem=send_sems.at[base + k],
            recv_sem=recv_sems.at[base + k],
            device_id=to,
            device_id_type=MESH,
        )

    mine = pltpu.make_async_copy(src_ref, out_ref.at[4 * x + 2 * y + c], local_sem)
    first = [copy(1, me, (*xnb, c), src=src_ref), copy(2, me, (*ynb, c), src=src_ref), copy(0, me, sibling, src=src_ref)]
    relay = copy(3, (*relayed, c), (*other, c))
    passed = [copy(4 + j, (*chip, c), sibling) for j, chip in enumerate((xnb, ynb, diag))]

    def start():
        mine.start()
        for cp in first:
            cp.start()

    def forward():
        copy(1 + c, (*relayed, c), me).wait_recv()
        relay.start()
        copy(2 - c, (*other, c), me).wait_recv()
        passed[0].start()
        passed[1].start()

    def forward_relayed():
        copy(3, (*diag, c), me).wait_recv()
        passed[2].start()

    def finish():
        copy(0, sibling, me).wait_recv()
        for j, chip in enumerate((xnb, ynb, diag)):
            copy(4 + j, (*chip, 1 - c), me).wait_recv()
        for cp in first + [relay] + passed:
            cp.wait_send()
        mine.wait()

    return start, forward, forward_relayed, finish


W_IN_CHUNKS = 4


def _gather_weights_and_norm(x2, g1, w_shard, meta, conv_w):
    per = GATHER_PAIRS
    n_ch = W_IN_CHUNKS
    rows = w_shard.shape[0] // n_ch
    tm = ROW_TILE
    nt = x2.shape[0] // tm
    cols = D_CONV // N_DEV
    pack_shape = (SC_CONV0 + 8, CHUNK)
    starts = [0] * (n_ch + 1)
    forwards = [min(5 + 2 * p, nt - 1) for p in range(n_ch)] + [min(12, nt - 1)]
    relayeds = [min(12 + p, nt - 1) for p in range(n_ch)] + [nt - 1]
    finishes = [nt - 1] * (n_ch + 1)

    def body(x_ref, g_ref, w_ref, meta_ref, conv_ref, hn_ref, hnT_ref, w_all_ref, mc_ref, cw_ref,
             w_bf, small_ref, small_all_ref, send_sems, recv_sems, local_sems):
        i = pl.program_id(0)

        @pl.when(i == 0)
        def _():
            w_bf[...] = w_ref[...].astype(BF16)
            small_ref[...] = jnp.zeros_like(small_ref)
            small_ref[SC_META0 : SC_META0 + N_META, :] = meta_ref[...]
            small_ref[SC_CONV0 : SC_CONV0 + 3, 0:cols] = conv_ref[...]

        parts = [
            _two_level_gather(
                w_bf.at[pl.ds(ch * rows, rows)],
                w_all_ref.at[:, pl.ds(ch * rows, rows)],
                send_sems,
                recv_sems,
                local_sems.at[ch],
                ch * per,
            )
            for ch in range(n_ch)
        ]
        parts.append(_two_level_gather(small_ref, small_all_ref, send_sems, recv_sems, local_sems.at[n_ch], n_ch * per))
        for phase, steps in enumerate((starts, forwards, relayeds, finishes)):
            for part, step in zip(parts, steps, strict=True):
                pl.when(i == step)(part[phase])

        h = x_ref[...]
        r = lax.rsqrt(jnp.mean(h * h, axis=-1, keepdims=True) + EPS)
        hn = (h * r * g_ref[...]).astype(BF16)
        hn_ref[...] = hn
        hnT_ref[...] = hn.T

        @pl.when(i == nt - 1)
        def _():
            mc_ref[0 : CHUNK - N_META, :] = jnp.zeros((CHUNK - N_META, D_MODEL), F32)
            cw_ref[...] = jnp.zeros_like(cw_ref)
            for d in range(N_DEV):
                mc_ref[CHUNK - N_META : CHUNK, d * CHUNK : (d + 1) * CHUNK] = small_all_ref[d, SC_META0 : SC_META0 + N_META, :]
                lane0 = (d * cols) % CHUNK
                rows8 = small_all_ref[d, SC_CONV0 : SC_CONV0 + 8, :]
                rows8 = pltpu.roll(rows8, lane0, 1) if lane0 else rows8
                cw_ref[0:3, d * cols : (d + 1) * cols] = rows8[0:3, lane0 : lane0 + cols]

    const = lambda r, c: pl.BlockSpec((r, c), lambda i: (0, 0))
    return pl.pallas_call(
        body,
        name="gather_weights_norm",
        grid=(nt,),
        out_shape=(
            jax.ShapeDtypeStruct(x2.shape, BF16),
            jax.ShapeDtypeStruct(x2.shape[::-1], BF16),
            jax.ShapeDtypeStruct((N_DEV, *w_shard.shape), BF16),
            jax.ShapeDtypeStruct((CHUNK, D_MODEL), F32),
            jax.ShapeDtypeStruct((8, D_CONV), F32),
        ),
        in_specs=[pl.BlockSpec((tm, D_MODEL), lambda i: (i, 0)), _VMEM, _VMEM, _VMEM, _VMEM],
        out_specs=(
            pl.BlockSpec((tm, D_MODEL), lambda i: (i, 0)),
            pl.BlockSpec((D_MODEL, tm), lambda i: (0, i)),
            _HBM,
            const(CHUNK, D_MODEL),
            const(8, D_CONV),
        ),
        scratch_shapes=[
            pltpu.VMEM(w_shard.shape, BF16),
            pltpu.VMEM(pack_shape, F32),
            pltpu.VMEM((N_DEV, *pack_shape), F32),
            pltpu.SemaphoreType.DMA(((n_ch + 1) * per,)),
            pltpu.SemaphoreType.DMA(((n_ch + 1) * per,)),
            pltpu.SemaphoreType.DMA((n_ch + 1,)),
        ],
        compiler_params=_params(),
    )(x2, g1, w_shard, meta, conv_w)


def _f1(hn_tok, meta_chunk, g1, w3, cw8, rot, dec, zeta, xi, wo_shard):
    tm = ROW_TILE
    nt = hn_tok.shape[0] // tm + 1
    n_rows = nt * tm
    nct = tm // CHUNK

    def body(hn_ref, mc_ref, g_ref, w_ref, cw_ref, ca_ref, sa_ref, cb_ref, sb_ref, dec_ref, zeta_ref, xi_ref, wo_ref,
             hnT_ref, pr_ref, co_ref, cv_ref, o_ref, st_ref, wo_all_ref,
             halo, state, wo_bf, send_sems, recv_sems, local_sem):
        i = pl.program_id(0)

        @pl.when(i == 0)
        def _():
            wo_bf[...] = wo_ref[...].astype(BF16)

        wo_phases = _two_level_gather(wo_bf, wo_all_ref, send_sems, recv_sems, local_sem)
        for step, phase in zip((0, nt // 3, 2 * nt // 3, nt - 1), wo_phases, strict=True):
            pl.when(i == step)(phase)

        def work(hn, r0):
            rs = slice(r0, tm)
            n = tm - r0

            def proj(j):
                return jnp.dot(hn, w_ref[j], preferred_element_type=F32)

            cos_all, sin_all = _tile_rotary(ca_ref, sa_ref, cb_ref, sb_ref, i)
            cos_t, sin_t = cos_all[rs, :], sin_all[rs, :]
            q, k, v = proj(4), proj(5), proj(6).astype(BF16)
            pr_ref[rs, 6 * BLK : 7 * BLK] = v
            chunks = range(r0 // CHUNK, tm // CHUNK)
            heads = []
            for hd in range(N_HEADS):
                cs = slice(hd * HEAD_DIM, (hd + 1) * HEAD_DIM)
                qh = (_rot(q[:, cs], cos_t, sin_t) * Q_SCALE).astype(BF16)
                kh = _rot(k[:, cs], cos_t, sin_t).astype(BF16)
                vh = v[:, cs]
                pr_ref[rs, 4 * BLK + hd * HEAD_DIM : 4 * BLK + (hd + 1) * HEAD_DIM] = qh
                pr_ref[rs, 5 * BLK + hd * HEAD_DIM : 5 * BLK + (hd + 1) * HEAD_DIM] = kh
                upd = {}
                for c in chunks:
                    cr = slice(c * CHUNK - r0, (c + 1) * CHUNK - r0)
                    kz = (kh[cr].astype(F32) * zeta_ref[hd]).astype(BF16)
                    upd[c] = lax.dot_general(kz, vh[cr], TN, preferred_element_type=F32)
                st = state[hd]
                outs = []
                for c in chunks:
                    cr = slice(c * CHUNK - r0, (c + 1) * CHUNK - r0)
                    qc, kc, vc = qh[cr], kh[cr], vh[cr]
                    st_bf = st.astype(BF16)
                    st_ref[c, hd] = st_bf
                    s = lax.dot_general(qc, kc, NT, preferred_element_type=F32) * dec_ref[hd]
                    inner = jnp.dot(s.astype(BF16), vc, preferred_element_type=F32)
                    qx = (qc.astype(F32) * xi_ref[hd]).astype(BF16)
                    outs.append(inner + jnp.dot(qx, st_bf, preferred_element_type=F32))
                    st = CHUNK_DECAY[hd] * st + upd[c]
                state[hd] = st
                heads.append(jnp.concatenate(outs, axis=0) if len(outs) > 1 else outs[0])
            o_ref[rs, :] = jnp.concatenate(heads, axis=1)

            cx, cb, cc, cg = proj(0), proj(1), proj(2), proj(3)
            u = cc * cx
            rows = lax.broadcasted_iota(jnp.int32, u.shape, 0)
            hl = halo[...]
            u1 = jnp.where(rows == 0, hl[7:8], pltpu.roll(u, 1, 0))
            u2 = jnp.where(rows == 0, hl[6:7], jnp.where(rows == 1, hl[7:8], pltpu.roll(u, 2, 0)))
            halo[...] = u[n - 8 : n]
            cw = cw_ref[...]
            conv = cw[0:1] * u2 + cw[1:2] * u1 + cw[2:3] * u
            co = (cb * conv * (cg * _sigmoid(cg))).astype(BF16)
            co_ref[rs, :] = co
            cv_ref[rs, :] = conv.astype(BF16)
            pr_ref[rs, 0 * BLK : 1 * BLK] = cx.astype(BF16)
            pr_ref[rs, 1 * BLK : 2 * BLK] = cb.astype(BF16)
            pr_ref[rs, 2 * BLK : 3 * BLK] = cc.astype(BF16)
            pr_ref[rs, 3 * BLK : 4 * BLK] = cg.astype(BF16)
            pr_ref[rs, 7 * BLK : 8 * BLK] = proj(7).astype(BF16)

        @pl.when(i == 0)
        def _():
            halo[...] = jnp.zeros_like(halo)
            state[...] = jnp.zeros_like(state)
            h = mc_ref[...]
            r = lax.rsqrt(jnp.mean(h * h, axis=-1, keepdims=True) + EPS)
            hn = (h * r * g_ref[...]).astype(BF16)
            hnT_ref[...] = hn.T
            work(hn, LIVE0)

        @pl.when(i > 0)
        def _():
            work(hn_ref[...], 0)

    row = lambda w: pl.BlockSpec((tm, w), lambda i: (i, 0))
    return pl.pallas_call(
        body,
        name="f1_inproj_conv",
        grid=(nt,),
        in_specs=[pl.BlockSpec((tm, D_MODEL), lambda i: (_token_tile(i), 0))] + [_VMEM] * 12,
        out_specs=(
            pl.BlockSpec((D_MODEL, CHUNK), lambda i: (0, 0)),
            row(N_PROJ * BLK),
            row(D_CONV),
            row(D_CONV),
            row(D_RET),
            pl.BlockSpec((nct, N_HEADS, HEAD_DIM, HEAD_DIM), lambda i: (i, 0, 0, 0)),
            _HBM,
        ),
        out_shape=(
            jax.ShapeDtypeStruct((D_MODEL, CHUNK), BF16),
            jax.ShapeDtypeStruct((n_rows, N_PROJ * BLK), BF16),
            jax.ShapeDtypeStruct((n_rows, D_CONV), BF16),
            jax.ShapeDtypeStruct((n_rows, D_CONV), BF16),
            jax.ShapeDtypeStruct((n_rows, D_RET), F32),
            jax.ShapeDtypeStruct((n_rows // CHUNK, N_HEADS, HEAD_DIM, HEAD_DIM), BF16),
            jax.ShapeDtypeStruct((N_DEV, *wo_shard.shape), BF16),
        ),
        scratch_shapes=[
            pltpu.VMEM((8, D_CONV), F32),
            pltpu.VMEM((N_HEADS, HEAD_DIM, HEAD_DIM), F32),
            pltpu.VMEM(wo_shard.shape, BF16),
            pltpu.SemaphoreType.DMA((GATHER_PAIRS,)),
            pltpu.SemaphoreType.DMA((GATHER_PAIRS,)),
            pltpu.SemaphoreType.DMA(()),
        ],
        compiler_params=_params(),
    )(hn_tok, meta_chunk, g1, w3, cw8, *rot, dec, zeta, xi, wo_shard)


def _group_norm(o):
    ys, rs = [], []
    for hd in range(N_HEADS):
        oh = o[:, hd * HEAD_DIM : (hd + 1) * HEAD_DIM]
        xc = oh - jnp.mean(oh, axis=-1, keepdims=True)
        rstd = lax.rsqrt(jnp.mean(xc * xc, axis=-1, keepdims=True) + EPS)
        ys.append(xc * rstd)
        rs.append(jnp.broadcast_to(rstd, oh.shape))
    return jnp.concatenate(ys, axis=1), jnp.concatenate(rs, axis=1)


def _f2(proj, o, conv_out, x2, t2, wout, gr, gf):
    n_rows = proj.shape[0]
    tm = ROW_TILE
    nt = n_rows // tm

    def body(o_ref, rg_ref, co_ref, x_ref, t_ref, wo_ref, gr_ref, gf_ref, dh2_ref, gw_ref, acc_ref, gacc):
        i = pl.program_id(0)

        @pl.when(i == 0)
        def _():
            acc_ref[...] = jnp.zeros_like(acc_ref)
            gacc[...] = jnp.zeros_like(gacc)
            dh2_ref[...] = jnp.zeros_like(dh2_ref)

        @pl.when(i > 0)
        def _():
            yh, _ = _group_norm(o_ref[...])
            rg = rg_ref[...].astype(F32)
            ro = (yh * gr_ref[...] * (rg * _sigmoid(rg))).astype(BF16)
            h2 = (
                x_ref[...]
                + jnp.dot(co_ref[...], wo_ref[0:D_CONV], preferred_element_type=F32)
                + jnp.dot(ro, wo_ref[D_CONV:], preferred_element_type=F32)
            )
            r2 = lax.rsqrt(jnp.mean(h2 * h2, axis=-1, keepdims=True) + EPS)
            yn = h2 * r2
            gfv = gf_ref[...]
            err = yn * gfv - t_ref[...]
            tile_loss = jnp.sum(jnp.sum(err * err, axis=-1, keepdims=True), axis=0, keepdims=True) * (0.5 / D_MODEL)
            acc_ref[0:1, :] += jnp.sum(err * yn, axis=0, keepdims=True) * (1.0 / D_MODEL)
            acc_ref[1:2, :] += tile_loss
            dyn = err * (gfv * (1.0 / D_MODEL))
            dh2 = r2 * (dyn - yn * jnp.mean(dyn * yn, axis=-1, keepdims=True))
            dh2_ref[...] = dh2
            dh2_bf = dh2.astype(BF16)
            gacc[0:D_CONV, :] += lax.dot_general(co_ref[...], dh2_bf, TN, preferred_element_type=F32)
            gacc[D_CONV:, :] += lax.dot_general(ro, dh2_bf, TN, preferred_element_type=F32)

        @pl.when(i == nt - 1)
        def _():
            gw_ref[...] = gacc[...].astype(BF16)

    row = lambda w, j=0: pl.BlockSpec((tm, w), lambda i: (i, j))
    tok = pl.BlockSpec((tm, D_MODEL), lambda i: (_token_tile(i), 0))
    return pl.pallas_call(
        body,
        name="f2_retention_out",
        grid=(nt,),
        in_specs=[row(D_RET), row(BLK, 7), row(D_CONV), tok, tok] + [_VMEM] * 3,
        out_specs=(
            row(D_MODEL),
            pl.BlockSpec((D_MODEL, D_MODEL), lambda i: (0, 0)),
            pl.BlockSpec((8, D_MODEL), lambda i: (0, 0)),
        ),
        out_shape=(
            jax.ShapeDtypeStruct((n_rows, D_MODEL), F32),
            jax.ShapeDtypeStruct((D_MODEL, D_MODEL), BF16),
            jax.ShapeDtypeStruct((8, D_MODEL), F32),
        ),
        scratch_shapes=[pltpu.VMEM((D_MODEL, D_MODEL), F32)],
        compiler_params=_params(),
    )(o, proj, conv_out, x2, t2, wout, gr, gf)


def _b1(dh2, proj, o, states, wout, gr, tables, zeta0, rot):
    n_rows = dh2.shape[0]
    tm = ROW_TILE
    nt = n_rows // tm
    chunk = B1_CHUNK
    nct = tm // chunk
    decay = tuple(math.exp(chunk * lg) for lg in LOG_G)

    def body(dh2_ref, pr_ref, o_ref, st_ref, wo_ref, gr_ref, dec_ref, dect_ref, zeta_ref, xi_ref, zeta0_ref,
             ca_ref, sa_ref, cb_ref, sb_ref, dp_ref, dco_ref, acc_ref, dstate):
        i = pl.program_id(0)
        tile = nt - 1 - i
        cos_all, sin_all = _tile_rotary(ca_ref, sa_ref, cb_ref, sb_ref, tile)

        @pl.when(i == 0)
        def _():
            dstate[...] = jnp.zeros_like(dstate)
            acc_ref[...] = jnp.zeros_like(acc_ref)

        @pl.when(tile > 0)
        def _():
            dmix = lax.dot_general(dh2_ref[...].astype(BF16), wo_ref[...], NT, preferred_element_type=F32)
            dco_ref[...] = dmix[:, :D_CONV].astype(BF16)
            dro = dmix[:, D_CONV:]
            rg = pr_ref[:, 3 * BLK : 4 * BLK].astype(F32)
            sg = _sigmoid(rg)
            silu, dsilu = rg * sg, sg * (1.0 + rg * (1.0 - sg))
            yh, rstd = _group_norm(o_ref[...])
            grv = gr_ref[...]
            dp_ref[:, 3 * BLK : 4 * BLK] = (dro * (yh * grv) * dsilu).astype(BF16)
            dret = dro * silu
            acc_ref[0:1, :] += jnp.sum(dret * yh, axis=0, keepdims=True)
            dyh = dret * grv
            for hd in range(N_HEADS):
                cs = slice(hd * HEAD_DIM, (hd + 1) * HEAD_DIM)
                a, b = dyh[:, cs], yh[:, cs]
                do_head = (
                    rstd[:, cs]
                    * (a - jnp.mean(a, axis=-1, keepdims=True) - b * jnp.mean(a * b, axis=-1, keepdims=True))
                ).astype(BF16)
                dupd = {}
                for c in range(nct):
                    rs = slice(c * chunk, (c + 1) * chunk)
                    qx = (pr_ref[rs, cs].astype(F32) * xi_ref[hd]).astype(BF16)
                    dupd[c] = lax.dot_general(qx, do_head[rs, :], TN, preferred_element_type=F32)
                dst = dstate[hd]
                for c in reversed(range(nct)):
                    rs = slice(c * chunk, (c + 1) * chunk)
                    cos_t, sin_t = cos_all[rs, :], sin_all[rs, :]
                    q = pr_ref[rs, hd * HEAD_DIM : (hd + 1) * HEAD_DIM]
                    k = pr_ref[rs, BLK + hd * HEAD_DIM : BLK + (hd + 1) * HEAD_DIM]
                    v = pr_ref[rs, 2 * BLK + hd * HEAD_DIM : 2 * BLK + (hd + 1) * HEAD_DIM]
                    do = do_head[rs, :]
                    st_bf = st_ref[c * (chunk // CHUNK), hd]
                    dst_bf = dst.astype(BF16)
                    zt, xt = zeta_ref[hd], xi_ref[hd]
                    sT = (lax.dot_general(k, q, NT, preferred_element_type=F32) * dect_ref[hd]).astype(BF16)
                    dsT = (lax.dot_general(v, do, NT, preferred_element_type=F32) * dect_ref[hd]).astype(BF16)
                    ds = (lax.dot_general(do, v, NT, preferred_element_type=F32) * dec_ref[hd]).astype(BF16)
                    kz = (k.astype(F32) * zt).astype(BF16)
                    dv = jnp.dot(sT, do, preferred_element_type=F32) + jnp.dot(kz, dst_bf, preferred_element_type=F32)
                    dq = jnp.dot(ds, k, preferred_element_type=F32) + xt * lax.dot_general(
                        do, st_bf, NT, preferred_element_type=F32
                    )
                    dk = jnp.dot(dsT, q, preferred_element_type=F32) + zt * lax.dot_general(
                        v, dst_bf, NT, preferred_element_type=F32
                    )
                    dst = decay[hd] * dst + dupd[c]
                    dp_ref[rs, hd * HEAD_DIM : (hd + 1) * HEAD_DIM] = _rot_bwd(dq * Q_SCALE, cos_t, sin_t).astype(BF16)
                    dp_ref[rs, BLK + hd * HEAD_DIM : BLK + (hd + 1) * HEAD_DIM] = _rot_bwd(dk, cos_t, sin_t).astype(BF16)
                    dp_ref[rs, 2 * BLK + hd * HEAD_DIM : 2 * BLK + (hd + 1) * HEAD_DIM] = dv.astype(BF16)
                dstate[hd] = dst

        @pl.when(tile == 0)
        def _():
            rs = slice(LIVE0, tm)
            cos_t, sin_t = cos_all[rs, :], sin_all[rs, :]
            zeros = jnp.zeros((tm - LIVE0, HEAD_DIM), BF16)
            for hd in range(N_HEADS):
                k = pr_ref[rs, BLK + hd * HEAD_DIM : BLK + (hd + 1) * HEAD_DIM]
                v = pr_ref[rs, 2 * BLK + hd * HEAD_DIM : 2 * BLK + (hd + 1) * HEAD_DIM]
                dst_bf = dstate[hd].astype(BF16)
                zt = zeta0_ref[hd]
                kz = (k.astype(F32) * zt).astype(BF16)
                dv = jnp.dot(kz, dst_bf, preferred_element_type=F32)
                dk = zt * lax.dot_general(v, dst_bf, NT, preferred_element_type=F32)
                dp_ref[rs, hd * HEAD_DIM : (hd + 1) * HEAD_DIM] = zeros
                dp_ref[rs, BLK + hd * HEAD_DIM : BLK + (hd + 1) * HEAD_DIM] = _rot_bwd(dk, cos_t, sin_t).astype(BF16)
                dp_ref[rs, 2 * BLK + hd * HEAD_DIM : 2 * BLK + (hd + 1) * HEAD_DIM] = dv.astype(BF16)
                dp_ref[rs, 3 * BLK + hd * HEAD_DIM : 3 * BLK + (hd + 1) * HEAD_DIM] = zeros
            dco_ref[rs, :] = jnp.zeros((tm - LIVE0, D_CONV), BF16)

    half = N_PROJ * BLK // 2
    rev = lambda w, j=0: pl.BlockSpec((tm, w), lambda i: (nt - 1 - i, j))
    return pl.pallas_call(
        body,
        name="b1_dret",
        grid=(nt,),
        in_specs=[
            rev(D_MODEL),
            rev(half, 1),
            rev(D_RET),
            pl.BlockSpec((tm // CHUNK, N_HEADS, HEAD_DIM, HEAD_DIM), lambda i: (nt - 1 - i, 0, 0, 0)),
        ]
        + [_VMEM] * 11,
        out_specs=(rev(half, 1), rev(D_CONV), pl.BlockSpec((8, D_RET), lambda i: (0, 0))),
        out_shape=(
            jax.ShapeDtypeStruct((n_rows, N_PROJ * BLK), BF16),
            jax.ShapeDtypeStruct((n_rows, D_CONV), BF16),
            jax.ShapeDtypeStruct((8, D_RET), F32),
        ),
        scratch_shapes=[pltpu.VMEM((N_HEADS, HEAD_DIM, HEAD_DIM), F32)],
        compiler_params=_params(),
    )(dh2, proj, o, states, wout, gr, *tables, zeta0, *rot)


def _b2(dproj, dco, proj, conv, w3, cw8, x2, meta_chunk, dh2, g1):
    n_rows = dproj.shape[0]
    tm = ROW_TILE
    nt = n_rows // tm
    half = N_PROJ * BLK // 2

    def body(dpr_ref, dco_ref, pr_ref, cv_ref, w_ref, cw_ref, x_ref, mc_ref, dh2_ref, g_ref,
             dpc_ref, gx_ref, dm_ref, acc_ref, accc_ref, halo):
        i = pl.program_id(0)
        tile = nt - 1 - i

        @pl.when(i == 0)
        def _():
            acc_ref[...] = jnp.zeros_like(acc_ref)
            accc_ref[...] = jnp.zeros_like(accc_ref)
            halo[...] = jnp.zeros_like(halo)

        def work(h, r0):
            rs = slice(r0, tm)
            n = tm - r0
            dco = dco_ref[rs, :].astype(F32)
            cx = pr_ref[rs, 0 * BLK : 1 * BLK].astype(F32)
            cb = pr_ref[rs, 1 * BLK : 2 * BLK].astype(F32)
            cc = pr_ref[rs, 2 * BLK : 3 * BLK].astype(F32)
            cg = pr_ref[rs, 3 * BLK : 4 * BLK].astype(F32)
            conv = cv_ref[rs, :].astype(F32)
            sg = _sigmoid(cg)
            silu, dsilu = cg * sg, sg * (1.0 + cg * (1.0 - sg))
            t = dco * cb
            dcb = (dco * conv * silu).astype(BF16)
            dcg = (t * conv * dsilu).astype(BF16)
            dconv = t * silu
            rows = lax.broadcasted_iota(jnp.int32, dconv.shape, 0)
            hl = halo[...]
            dc1 = jnp.where(rows == n - 1, hl[0:1], pltpu.roll(dconv, n - 1, 0))
            dc2 = jnp.where(rows == n - 2, hl[0:1], jnp.where(rows == n - 1, hl[1:2], pltpu.roll(dconv, n - 2, 0)))
            halo[...] = dconv[0:8]
            cw = cw_ref[...]
            du = cw[2:3] * dconv + cw[1:2] * dc1 + cw[0:1] * dc2
            u = cc * cx
            accc_ref[0:1, :] += jnp.sum(u * dc2, axis=0, keepdims=True)
            accc_ref[1:2, :] += jnp.sum(u * dc1, axis=0, keepdims=True)
            accc_ref[2:3, :] += jnp.sum(u * dconv, axis=0, keepdims=True)
            dcx = (du * cc).astype(BF16)
            dcc = (du * cx).astype(BF16)
            dhn = lax.dot_general(dpr_ref[rs, 0:BLK], w_ref[4], NT, preferred_element_type=F32)
            for j in range(1, 4):
                dhn += lax.dot_general(dpr_ref[rs, j * BLK : (j + 1) * BLK], w_ref[4 + j], NT, preferred_element_type=F32)
            for blk, d in ((1, dcb), (3, dcg), (0, dcx), (2, dcc)):
                dpc_ref[rs, blk * BLK : (blk + 1) * BLK] = d
                dhn += lax.dot_general(d, w_ref[blk], NT, preferred_element_type=F32)
            r = lax.rsqrt(jnp.mean(h * h, axis=-1, keepdims=True) + EPS)
            hh = h * r
            acc_ref[0:1, :] += jnp.sum(dhn * hh, axis=0, keepdims=True)
            dg = dhn * g_ref[...]
            return dh2_ref[rs, :] + r * (dg - hh * jnp.mean(dg * hh, axis=-1, keepdims=True))

        @pl.when(tile == 0)
        def _():
            dh = work(mc_ref[...], LIVE0)
            dm_ref[...] = dh[CHUNK - N_META : CHUNK]

        @pl.when(tile > 0)
        def _():
            gx_ref[...] = work(x_ref[...], 0)

    rev = lambda w, j=0: pl.BlockSpec((tm, w), lambda i: (nt - 1 - i, j))
    tok = pl.BlockSpec((tm, D_MODEL), lambda i: (_token_tile(nt - 1 - i), 0))
    const = lambda r, c: pl.BlockSpec((r, c), lambda i: (0, 0))
    return pl.pallas_call(
        body,
        name="b2_dconv_dh",
        grid=(nt,),
        in_specs=[rev(half, 1), rev(D_CONV), rev(half, 0), rev(D_CONV), _VMEM, _VMEM, tok, _VMEM, rev(D_MODEL), _VMEM],
        out_specs=(rev(half, 0), tok, const(N_META, D_MODEL), const(8, D_MODEL), const(8, D_CONV)),
        out_shape=(
            jax.ShapeDtypeStruct(dproj.shape, BF16),
            jax.ShapeDtypeStruct(x2.shape, F32),
            jax.ShapeDtypeStruct((N_META, D_MODEL), F32),
            jax.ShapeDtypeStruct((8, D_MODEL), F32),
            jax.ShapeDtypeStruct((8, D_CONV), F32),
        ),
        scratch_shapes=[pltpu.VMEM((8, D_CONV), F32)],
        input_output_aliases={0: 0},
        compiler_params=_params(),
    )(dproj, dco, proj, conv, w3, cw8, x2, meta_chunk, dh2, g1)


def _gw_in_scatter(hnT_tok, hnT_meta, dproj, gw_out_parts, dmeta_rows, acc_conv, acc_b2, acc_b1, acc_f2, me_arr):
    n_rows = dproj.shape[0]
    last = N_DEV - 1
    cols = D_CONV // N_DEV
    by_dest = (True, True, False)
    small_blocks = (gw_out_parts.shape[1:], (SC_CONV0 + 8, CHUNK), (GA_LOSS + 8, CHUNK))

    def body(me_ref, a_ref, am_ref, b_ref, go_ref, dm_ref, ac_ref, ab2_ref, ab1_ref, af2_ref,
             land_in, land_go, land_sc, land_ga, d2d_buf, d2d_land, ici_buf, sc_ref, ga_ref,
             d2d_send, d2d_recv, ici_send, ici_recv, send_sems, recv_sems, local_sems):
        del me_ref
        s = pl.program_id(0)
        t = last - s
        q = t >> 1
        x, y, c = lax.axis_index("x"), lax.axis_index("y"), lax.axis_index("c")
        me = 4 * x + 2 * y + c
        chip = 2 * x + y
        srcs, lands = (go_ref, sc_ref, ga_ref), (land_go, land_sc, land_ga)

        def peer_at(k):
            return (1 - x if k & 4 else x, 1 - y if k & 2 else y, 1 - c if k & 1 else c)

        def small_copy(a, k):
            px, py, pc = peer_at(k)
            return pltpu.make_async_remote_copy(
                src_ref=srcs[a].at[4 * px + 2 * py + pc] if by_dest[a] else srcs[a],
                dst_ref=lands[a].at[me],
                send_sem=send_sems.at[a * last + k - 1],
                recv_sem=recv_sems.at[a * last + k - 1],
                device_id=(px, py, pc),
                device_id_type=MESH,
            )

        def small_local(a):
            return pltpu.make_async_copy(srcs[a].at[me] if by_dest[a] else srcs[a], lands[a].at[me], local_sems.at[a + 1])

        def d2d_copy(j):
            return pltpu.make_async_remote_copy(
                src_ref=d2d_buf.at[j],
                dst_ref=d2d_land.at[j],
                send_sem=d2d_send.at[j],
                recv_sem=d2d_recv.at[j],
                device_id=(x, y, 1 - c),
                device_id_type=MESH,
            )

        def ici_copy(j, to):
            return pltpu.make_async_remote_copy(
                src_ref=ici_buf.at[j],
                dst_ref=land_in.at[chip],
                send_sem=ici_send.at[j],
                recv_sem=ici_recv.at[j],
                device_id=to,
                device_id_type=MESH,
            )

        def own_copy():
            return pltpu.make_async_copy(ici_buf.at[0], land_in.at[chip], local_sems.at[0])

        @pl.when(s == 0)
        def _():
            sc_ref[...] = jnp.zeros_like(sc_ref)
            for d in range(N_DEV):
                sc_ref[d, SC_META0 : SC_META0 + N_META, :] = dm_ref[:, d * CHUNK : (d + 1) * CHUNK]
                lane0 = (d * cols) % CHUNK
                rows8 = ac_ref[:, d * cols - lane0 : d * cols - lane0 + CHUNK]
                rows8 = pltpu.roll(rows8, CHUNK - lane0, 1) if lane0 else rows8
                sc_ref[d, SC_CONV0 : SC_CONV0 + 3, 0:cols] = rows8[0:3, 0:cols]
            ga_ref[...] = jnp.zeros_like(ga_ref)
            for row0, acc, n in ((GA_N1, ab2_ref, 8), (GA_RG, ab1_ref, 4), (GA_FG, af2_ref, 8)):
                for j in range(n):
                    ga_ref[row0 + j : row0 + j + 1, :] = acc[0:1, j * CHUNK : (j + 1) * CHUNK]
            ga_ref[GA_LOSS : GA_LOSS + 1, :] = af2_ref[1:2, 0:CHUNK]
            for a in range(3):
                small_local(a).start()
            for k in range(last, 0, -1):
                for a in range(3):
                    small_copy(a, k).start()

        blk = jnp.dot(a_ref[...], b_ref[ROW_TILE:, :], preferred_element_type=F32)
        blk += jnp.dot(am_ref[...], b_ref[LIVE0:ROW_TILE, :], preferred_element_type=F32)

        @pl.when((t & 1) == 1)
        def _():
            d2d_buf[q] = blk.astype(BF16)
            d2d_copy(q).start()

        @pl.when((t & 1) == 0)
        def _():
            d2d_copy(q).wait_recv()
            ici_buf[q] = (blk + d2d_land[q].astype(F32)).astype(BF16)

            @pl.when(t != 0)
            def _():
                ici_copy(q, (jnp.bitwise_xor(x, (t >> 2) & 1), jnp.bitwise_xor(y, (t >> 1) & 1), c)).start()

            @pl.when(t == 0)
            def _():
                own_copy().start()
                for j in range(N_CHIPS):
                    d2d_copy(j).wait_send()
                for j in range(N_CHIPS - 1, 0, -1):
                    ici_copy(j, peer_at(2 * j)).wait()
                for k in range(last, 0, -1):
                    for a in range(3):
                        small_copy(a, k).wait()
                for a in range(3):
                    small_local(a).wait()
                own_copy().wait()

    grid_spec = pltpu.PrefetchScalarGridSpec(
        num_scalar_prefetch=1,
        grid=(N_DEV,),
        in_specs=[_VMEM, _VMEM, pl.BlockSpec((n_rows, BLK), lambda s, me: (0, jnp.bitwise_xor(me[0], last - s))), _HBM]
        + [_VMEM] * 5,
        out_specs=tuple([_HBM] * 4),
        scratch_shapes=[
            pltpu.VMEM((N_CHIPS, D_MODEL, BLK), BF16),
            pltpu.VMEM((N_CHIPS, D_MODEL, BLK), BF16),
            pltpu.VMEM((N_CHIPS, D_MODEL, BLK), BF16),
            pltpu.VMEM((N_DEV, *small_blocks[1]), F32),
            pltpu.VMEM(small_blocks[2], F32),
            pltpu.SemaphoreType.DMA((N_CHIPS,)),
            pltpu.SemaphoreType.DMA((N_CHIPS,)),
            pltpu.SemaphoreType.DMA((N_CHIPS,)),
            pltpu.SemaphoreType.DMA((N_CHIPS,)),
            pltpu.SemaphoreType.DMA((3 * last,)),
            pltpu.SemaphoreType.DMA((3 * last,)),
            pltpu.SemaphoreType.DMA((4,)),
        ],
    )
    return pl.pallas_call(
        body,
        name="gw_in_scatter",
        grid_spec=grid_spec,
        out_shape=(jax.ShapeDtypeStruct((N_CHIPS, D_MODEL, BLK), BF16),)
        + tuple(jax.ShapeDtypeStruct((N_DEV, *b), dt) for b, dt in zip(small_blocks, (gw_out_parts.dtype, F32, F32), strict=True)),
        compiler_params=_params(),
    )(me_arr, hnT_tok, hnT_meta, dproj, gw_out_parts, dmeta_rows, acc_conv, acc_b2, acc_b1, acc_f2)


def _adamw(w, g, m, v):
    m = ADAM_B1 * m + (1.0 - ADAM_B1) * g
    v = ADAM_B2 * v + (1.0 - ADAM_B2) * (g * g)
    m_hat = m / (1.0 - ADAM_B1**ADAM_STEP)
    v_hat = v / (1.0 - ADAM_B2**ADAM_STEP)
    delta = -ADAM_LR * (m_hat / (jnp.sqrt(v_hat) + ADAM_EPS) + ADAM_WD * w)
    return delta, m, v


def _sum_adamw(name, parts, w, m, v, rows_per_step):
    n_r, n_c = w.shape
    n_parts = parts.shape[0]
    tr = rows_per_step

    def body(p_ref, w_ref, m_ref, v_ref, g_ref, d_ref, nm_ref, nv_ref):
        g = p_ref[0].astype(F32)
        for s in range(1, n_parts):
            g = g + p_ref[s].astype(F32)
        g_ref[...] = g
        d_ref[...], nm_ref[...], nv_ref[...] = _adamw(w_ref[...], g, m_ref[...], v_ref[...])

    blk = pl.BlockSpec((tr, n_c), lambda i: (i, 0))
    return pl.pallas_call(
        body,
        name=name,
        grid=(n_r // tr,),
        in_specs=[pl.BlockSpec((n_parts, tr, n_c), lambda i: (0, i, 0)), blk, blk, blk],
        out_specs=(blk,) * 4,
        out_shape=(jax.ShapeDtypeStruct((n_r, n_c), F32),) * 4,
        compiler_params=_params(),
    )(parts, w, m, v)


def _small_leaves(meta, conv_w, n1, rg, fg):
    return meta, conv_w, n1.reshape(8, CHUNK), rg.reshape(4, CHUNK), fg.reshape(8, CHUNK)


def _from_small_leaves(meta, conv_w, n1, rg, fg):
    return meta, conv_w, n1.reshape(D_MODEL), rg.reshape(D_RET), fg.reshape(D_MODEL)


def _adamw_small(land_sc, land_ga, w, m, v):
    n_leaf = 5

    def body(sc_ref, ga_ref, *refs):
        ins, outs = refs[: 3 * n_leaf], refs[3 * n_leaf :]
        sc, ga = sc_ref[0], ga_ref[0]
        for s in range(1, N_DEV):
            sc = sc + sc_ref[s]
            ga = ga + ga_ref[s]
        grads = (
            sc[SC_META0 : SC_META0 + N_META],
            sc[SC_CONV0 : SC_CONV0 + 3, 0 : D_CONV // N_DEV],
            ga[GA_N1 : GA_N1 + 8],
            ga[GA_RG : GA_RG + 4],
            ga[GA_FG : GA_FG + 8],
        )
        for leaf, g in enumerate(grads):
            d, nm, nv = _adamw(ins[leaf][...], g, ins[n_leaf + leaf][...], ins[2 * n_leaf + leaf][...])
            outs[leaf][...] = g
            outs[n_leaf + leaf][...] = d
            outs[2 * n_leaf + leaf][...] = nm
            outs[3 * n_leaf + leaf][...] = nv
        outs[4 * n_leaf][...] = ga[GA_LOSS : GA_LOSS + 1]

    leaf_shapes = tuple(jax.ShapeDtypeStruct(a.shape, F32) for a in w)
    out = pl.pallas_call(
        body,
        name="adamw_small",
        out_shape=leaf_shapes * 4 + (jax.ShapeDtypeStruct((1, CHUNK), F32),),
    )(land_sc, land_ga, *w, *m, *v)
    return tuple(out[k * n_leaf : (k + 1) * n_leaf] for k in range(4)), out[4 * n_leaf]


def kernel(x, meta, norm1_g, w_in, conv_w, ret_norm_g, w_out, final_g, loss_target, m_meta, m_norm1_g, m_w_in, m_conv_w, m_ret_norm_g, m_w_out, m_final_g, v_meta, v_norm1_g, v_w_in, v_conv_w, v_ret_norm_g, v_w_out, v_final_g):
    seq = x.shape[1]
    assert x.shape == (1, seq, D_MODEL) and seq % ROW_TILE == 0
    n_tiles = seq // ROW_TILE + 1
    x2, t2 = x[0], loss_target[0]

    small_w = _small_leaves(meta, conv_w, norm1_g, ret_norm_g, final_g)
    g1 = norm1_g.reshape(1, D_MODEL)
    hn_tok, hnT_tok, w3, meta_chunk, cw8 = _gather_weights_and_norm(x2, g1, w_in, meta, conv_w)

    rot = _rotary_tables(n_tiles, ROW_TILE)
    dec, _, zeta, xi = _decay_tables()
    gr = ret_norm_g.reshape(1, D_RET)
    gf = final_g.reshape(1, D_MODEL)

    hnT_meta, proj, conv_out, conv, o, states, wo3 = _f1(hn_tok, meta_chunk, g1, w3, cw8, rot, dec, zeta, xi, w_out)
    wout = wo3.reshape(D_MODEL, D_MODEL)
    dh2, gw_out, acc_f2 = _f2(proj, o, conv_out, x2, t2, wout, gr, gf)
    dproj_ret, dco, acc_b1 = _b1(dh2, proj, o, states, wout, gr, _decay_tables(B1_CHUNK), zeta, rot)
    dproj, grad_x2, dmeta_rows, acc_b2, acc_conv = _b2(dproj_ret, dco, proj, conv, w3, cw8, x2, meta_chunk, dh2, g1)
    gw_out_parts = gw_out.reshape(N_DEV, D_MODEL // N_DEV, D_MODEL)

    me_arr = (4 * lax.axis_index("x") + 2 * lax.axis_index("y") + lax.axis_index("c")).astype(jnp.int32).reshape(1)
    land_in, land_out, land_sc, land_ga = _gw_in_scatter(
        hnT_tok, hnT_meta, dproj, gw_out_parts, dmeta_rows, acc_conv, acc_b2, acc_b1, acc_f2, me_arr
    )

    g_w_in, d_w_in, nm_w_in, nv_w_in = _sum_adamw("adamw_w_in", land_in, w_in, m_w_in, v_w_in, 256)
    g_w_out, d_w_out, nm_w_out, nv_w_out = _sum_adamw("adamw_w_out", land_out, w_out, m_w_out, v_w_out, 64)
    small_out, loss_row = _adamw_small(
        land_sc,
        land_ga,
        small_w,
        _small_leaves(m_meta, m_conv_w, m_norm1_g, m_ret_norm_g, m_final_g),
        _small_leaves(v_meta, v_conv_w, v_norm1_g, v_ret_norm_g, v_final_g),
    )
    loss = loss_row[0, 0]
    grad_x = grad_x2.reshape(1, seq, D_MODEL)

    def leaves(w_in_leaf, w_out_leaf, small_leaves):
        meta_leaf, conv_leaf, n1_leaf, rg_leaf, fg_leaf = _from_small_leaves(*small_leaves)
        return (meta_leaf, n1_leaf, w_in_leaf, conv_leaf, rg_leaf, w_out_leaf, fg_leaf)

    return (
        loss,
        grad_x,
        *leaves(g_w_in, g_w_out, small_out[0]),
        *leaves(d_w_in, d_w_out, small_out[1]),
        *leaves(nm_w_in, nm_w_out, small_out[2]),
        *leaves(nv_w_in, nv_w_out, small_out[3]),
    )
```

```python
import math

import jax
import jax.numpy as jnp
import numpy as np
from jax import lax
from jax.experimental import pallas as pl
from jax.experimental.pallas import tpu as pltpu

F32 = jnp.float32
BF16 = jnp.bfloat16

N_DEV = 8
N_CHIPS = 4
D_MODEL = 1024
N_META = 16
CHUNK = 128
D_CONV = 512
D_RET = 512
N_HEADS = 4
HEAD_DIM = 128
N_PROJ = 8
BLK = 512
ROPE_BASE = 10000.0
EPS = 1e-6
Q_SCALE = HEAD_DIM ** -0.5
LOG_G = tuple(math.log(1.0 - 2.0 ** (-5.0 - h)) for h in range(N_HEADS))
CHUNK_DECAY = tuple(math.exp(CHUNK * lg) for lg in LOG_G)

ADAM_LR = 0.001
ADAM_B1 = 0.9
ADAM_B2 = 0.999
ADAM_EPS = 1e-08
ADAM_WD = 0.01
ADAM_STEP = 10

ROW_TILE = 512
PAD_ROWS = ROW_TILE - N_META
LIVE0 = ROW_TILE - CHUNK
B1_CHUNK = 256
VMEM_LIMIT = 56 * 1024 * 1024

SC_META0, SC_CONV0 = 0, 16
GA_N1, GA_RG, GA_FG, GA_LOSS = 0, 8, 16, 24

NT = (((1,), (1,)), ((), ()))
TN = (((0,), (0,)), ((), ()))
MESH = pl.DeviceIdType.MESH

_VMEM = pl.BlockSpec(memory_space=pltpu.VMEM)
_HBM = pl.BlockSpec(memory_space=pltpu.HBM)


def _params(n_axes=1):
    return pltpu.CompilerParams(dimension_semantics=("arbitrary",) * n_axes, vmem_limit_bytes=VMEM_LIMIT)


def _sigmoid(x):
    return 0.5 * jnp.tanh(0.5 * x) + 0.5


def _decay_tables(chunk=CHUNK):
    idx = np.arange(chunk, dtype=np.float64)
    diff = idx[:, None] - idx[None, :]
    dec = np.stack([np.where(diff >= 0, np.exp(diff * lg), 0.0) for lg in LOG_G])
    zeta = np.stack([np.exp((chunk - 1 - idx) * lg) for lg in LOG_G])
    xi = np.stack([np.exp((idx + 1.0) * lg) for lg in LOG_G])
    ones = np.ones((1, 1, HEAD_DIM))
    return (
        jnp.asarray(dec, F32),
        jnp.asarray(dec.transpose(0, 2, 1), F32),
        jnp.asarray(zeta[:, :, None] * ones, F32),
        jnp.asarray(xi[:, :, None] * ones, F32),
    )


def _rotary_tables(n_tiles, tm):
    half = HEAD_DIM // 2
    freqs = (1.0 / (np.float32(ROPE_BASE) ** (np.arange(half, dtype=np.float32) / np.float32(half)))).astype(np.float64)
    sign = np.concatenate([-np.ones(half), np.ones(half)])
    two = lambda a: np.concatenate([a, a], axis=1)
    base = two((np.arange(n_tiles, dtype=np.float64) * tm - PAD_ROWS)[:, None] * freqs[None, :])
    off = two(np.arange(tm, dtype=np.float64)[:, None] * freqs[None, :])
    as32 = lambda a: jnp.asarray(a, F32)
    return as32(np.cos(base)), as32(np.sin(base) * sign), as32(np.cos(off)), as32(np.sin(off) * sign)


def _tile_rotary(ca_ref, sa_ref, cb_ref, sb_ref, tile):
    ca, sa = ca_ref[pl.ds(tile, 1), :], sa_ref[pl.ds(tile, 1), :]
    cb, sb = cb_ref[...], sb_ref[...]
    return ca * cb - sa * sb, sa * cb + ca * sb


def _rot(t, cos2, sin2):
    return t * cos2 + pltpu.roll(t, HEAD_DIM // 2, 1) * sin2


def _rot_bwd(d, cos2, sin2):
    return d * cos2 + pltpu.roll(d * sin2, HEAD_DIM // 2, 1)


def _token_tile(i):
    return jnp.maximum(i - 1, 0)


GATHER_PAIRS = 7


def _two_level_gather(src_ref, out_ref, send_sems, recv_sems, local_sem, base=0):
    x, y, c = lax.axis_index("x"), lax.axis_index("y"), lax.axis_index("c")
    me, sibling = (x, y, c), (x, y, 1 - c)
    xnb, ynb, diag = (1 - x, y), (x, 1 - y), (1 - x, 1 - y)
    relayed = (jnp.bitwise_xor(x, 1 - c), jnp.bitwise_xor(y, c))
    other = (jnp.bitwise_xor(x, c), jnp.bitwise_xor(y, 1 - c))

    def copy(k, block, to, src=None):
        dst = out_ref.at[4 * block[0] + 2 * block[1] + block[2]]
        return pltpu.make_async_remote_copy(
            src_ref=dst if src is None else src,
            dst_ref=dst,
            send_sem=send_sems.at[base + k],
            recv_sem=recv_sems.at[base + k],
            device_id=to,
            device_id_type=MESH,
        )

    mine = pltpu.make_async_copy(src_ref, out_ref.at[4 * x + 2 * y + c], local_sem)
    first = [copy(1, me, (*xnb, c), src=src_ref), copy(2, me, (*ynb, c), src=src_ref), copy(0, me, sibling, src=src_ref)]
    relay = copy(3, (*relayed, c), (*other, c))
    passed = [copy(4 + j, (*chip, c), sibling) for j, chip in enumerate((xnb, ynb, diag))]

    def start():
        mine.start()
        for cp in first:
            cp.start()

    def forward():
        copy(1 + c, (*relayed, c), me).wait_recv()
        relay.start()
        copy(2 - c, (*other, c), me).wait_recv()
        passed[0].start()
        passed[1].start()

    def forward_relayed():
        copy(3, (*diag, c), me).wait_recv()
        passed[2].start()

    def finish():
        copy(0, sibling, me).wait_recv()
        for j, chip in enumerate((xnb, ynb, diag)):
            copy(4 + j, (*chip, 1 - c), me).wait_recv()
        for cp in first + [relay] + passed:
            cp.wait_send()
        mine.wait()

    return start, forward, forward_relayed, finish


W_IN_CHUNKS = 4


def _gather_weights_and_norm(x2, g1, w_shard, meta, conv_w):
    per = GATHER_PAIRS
    n_ch = W_IN_CHUNKS
    rows = w_shard.shape[0] // n_ch
    tm = ROW_TILE
    nt = x2.shape[0] // tm
    cols = D_CONV // N_DEV
    pack_shape = (SC_CONV0 + 8, CHUNK)
    starts = [0] * (n_ch + 1)
    forwards = [min(5 + 2 * p, nt - 1) for p in range(n_ch)] + [min(12, nt - 1)]
    relayeds = [min(12 + p, nt - 1) for p in range(n_ch)] + [nt - 1]
    finishes = [nt - 1] * (n_ch + 1)

    def body(x_ref, g_ref, w_ref, meta_ref, conv_ref, hn_ref, hnT_ref, w_all_ref, mc_ref, cw_ref,
             w_bf, small_ref, small_all_ref, send_sems, recv_sems, local_sems):
        i = pl.program_id(0)

        @pl.when(i == 0)
        def _():
            w_bf[...] = w_ref[...].astype(BF16)
            small_ref[...] = jnp.zeros_like(small_ref)
            small_ref[SC_META0 : SC_META0 + N_META, :] = meta_ref[...]
            small_ref[SC_CONV0 : SC_CONV0 + 3, 0:cols] = conv_ref[...]

        parts = [
            _two_level_gather(
                w_bf.at[pl.ds(ch * rows, rows)],
                w_all_ref.at[:, pl.ds(ch * rows, rows)],
                send_sems,
                recv_sems,
                local_sems.at[ch],
                ch * per,
            )
            for ch in range(n_ch)
        ]
        parts.append(_two_level_gather(small_ref, small_all_ref, send_sems, recv_sems, local_sems.at[n_ch], n_ch * per))
        for phase, steps in enumerate((starts, forwards, relayeds, finishes)):
            for part, step in zip(parts, steps, strict=True):
                pl.when(i == step)(part[phase])

        h = x_ref[...]
        r = lax.rsqrt(jnp.mean(h * h, axis=-1, keepdims=True) + EPS)
        hn = (h * r * g_ref[...]).astype(BF16)
        hn_ref[...] = hn
        hnT_ref[...] = hn.T

        @pl.when(i == nt - 1)
        def _():
            mc_ref[0 : CHUNK - N_META, :] = jnp.zeros((CHUNK - N_META, D_MODEL), F32)
            cw_ref[...] = jnp.zeros_like(cw_ref)
            for d in range(N_DEV):
                mc_ref[CHUNK - N_META : CHUNK, d * CHUNK : (d + 1) * CHUNK] = small_all_ref[d, SC_META0 : SC_META0 + N_META, :]
                lane0 = (d * cols) % CHUNK
                rows8 = small_all_ref[d, SC_CONV0 : SC_CONV0 + 8, :]
                rows8 = pltpu.roll(rows8, lane0, 1) if lane0 else rows8
                cw_ref[0:3, d * cols : (d + 1) * cols] = rows8[0:3, lane0 : lane0 + cols]

    const = lambda r, c: pl.BlockSpec((r, c), lambda i: (0, 0))
    return pl.pallas_call(
        body,
        name="gather_weights_norm",
        grid=(nt,),
        out_shape=(
            jax.ShapeDtypeStruct(x2.shape, BF16),
            jax.ShapeDtypeStruct(x2.shape[::-1], BF16),
            jax.ShapeDtypeStruct((N_DEV, *w_shard.shape), BF16),
            jax.ShapeDtypeStruct((CHUNK, D_MODEL), F32),
            jax.ShapeDtypeStruct((8, D_CONV), F32),
        ),
        in_specs=[pl.BlockSpec((tm, D_MODEL), lambda i: (i, 0)), _VMEM, _VMEM, _VMEM, _VMEM],
        out_specs=(
            pl.BlockSpec((tm, D_MODEL), lambda i: (i, 0)),
            pl.BlockSpec((D_MODEL, tm), lambda i: (0, i)),
            _HBM,
            const(CHUNK, D_MODEL),
            const(8, D_CONV),
        ),
        scratch_shapes=[
            pltpu.VMEM(w_shard.shape, BF16),
            pltpu.VMEM(pack_shape, F32),
            pltpu.VMEM((N_DEV, *pack_shape), F32),
            pltpu.SemaphoreType.DMA(((n_ch + 1) * per,)),
            pltpu.SemaphoreType.DMA(((n_ch + 1) * per,)),
            pltpu.SemaphoreType.DMA((n_ch + 1,)),
        ],
        compiler_params=_params(),
    )(x2, g1, w_shard, meta, conv_w)


def _f1(hn_tok, meta_chunk, g1, w3, cw8, rot, dec, zeta, xi, wo_shard):
    tm = ROW_TILE
    nt = hn_tok.shape[0] // tm + 1
    n_rows = nt * tm
    nct = tm // CHUNK

    def body(hn_ref, mc_ref, g_ref, w_ref, cw_ref, ca_ref, sa_ref, cb_ref, sb_ref, dec_ref, zeta_ref, xi_ref, wo_ref,
             hnT_ref, pr_ref, co_ref, cv_ref, o_ref, st_ref, wo_all_ref,
             halo, state, wo_bf, send_sems, recv_sems, local_sem):
        i = pl.program_id(0)

        @pl.when(i == 0)
        def _():
            wo_bf[...] = wo_ref[...].astype(BF16)

        wo_phases = _two_level_gather(wo_bf, wo_all_ref, send_sems, recv_sems, local_sem)
        for step, phase in zip((0, nt // 3, 2 * nt // 3, nt - 1), wo_phases, strict=True):
            pl.when(i == step)(phase)

        def work(hn, r0):
            rs = slice(r0, tm)
            n = tm - r0

            def proj(j):
                return jnp.dot(hn, w_ref[j], preferred_element_type=F32)

            cos_all, sin_all = _tile_rotary(ca_ref, sa_ref, cb_ref, sb_ref, i)
            cos_t, sin_t = cos_all[rs, :], sin_all[rs, :]
            q, k, v = proj(4), proj(5), proj(6).astype(BF16)
            pr_ref[rs, 6 * BLK : 7 * BLK] = v
            chunks = range(r0 // CHUNK, tm // CHUNK)
            heads = []
            for hd in range(N_HEADS):
                cs = slice(hd * HEAD_DIM, (hd + 1) * HEAD_DIM)
                qh = (_rot(q[:, cs], cos_t, sin_t) * Q_SCALE).astype(BF16)
                kh = _rot(k[:, cs], cos_t, sin_t).astype(BF16)
                vh = v[:, cs]
                pr_ref[rs, 4 * BLK + hd * HEAD_DIM : 4 * BLK + (hd + 1) * HEAD_DIM] = qh
                pr_ref[rs, 5 * BLK + hd * HEAD_DIM : 5 * BLK + (hd + 1) * HEAD_DIM] = kh
                upd = {}
                for c in chunks:
                    cr = slice(c * CHUNK - r0, (c + 1) * CHUNK - r0)
                    kz = (kh[cr].astype(F32) * zeta_ref[hd]).astype(BF16)
                    upd[c] = lax.dot_general(kz, vh[cr], TN, preferred_element_type=F32)
                st = state[hd]
                outs = []
                for c in chunks:
                    cr = slice(c * CHUNK - r0, (c + 1) * CHUNK - r0)
                    qc, kc, vc = qh[cr], kh[cr], vh[cr]
                    st_bf = st.astype(BF16)
                    st_ref[c, hd] = st_bf
                    s = lax.dot_general(qc, kc, NT, preferred_element_type=F32) * dec_ref[hd]
                    inner = jnp.dot(s.astype(BF16), vc, preferred_element_type=F32)
                    qx = (qc.astype(F32) * xi_ref[hd]).astype(BF16)
                    outs.append(inner + jnp.dot(qx, st_bf, preferred_element_type=F32))
                    st = CHUNK_DECAY[hd] * st + upd[c]
                state[hd] = st
                heads.append(jnp.concatenate(outs, axis=0) if len(outs) > 1 else outs[0])
            o_ref[rs, :] = jnp.concatenate(heads, axis=1)

            cx, cb, cc, cg = proj(0), proj(1), proj(2), proj(3)
            u = cc * cx
            rows = lax.broadcasted_iota(jnp.int32, u.shape, 0)
            hl = halo[...]
            u1 = jnp.where(rows == 0, hl[7:8], pltpu.roll(u, 1, 0))
            u2 = jnp.where(rows == 0, hl[6:7], jnp.where(rows == 1, hl[7:8], pltpu.roll(u, 2, 0)))
            halo[...] = u[n - 8 : n]
            cw = cw_ref[...]
            conv = cw[0:1] * u2 + cw[1:2] * u1 + cw[2:3] * u
            co = (cb * conv * (cg * _sigmoid(cg))).astype(BF16)
            co_ref[rs, :] = co
            cv_ref[rs, :] = conv.astype(BF16)
            pr_ref[rs, 0 * BLK : 1 * BLK] = cx.astype(BF16)
            pr_ref[rs, 1 * BLK : 2 * BLK] = cb.astype(BF16)
            pr_ref[rs, 2 * BLK : 3 * BLK] = cc.astype(BF16)
            pr_ref[rs, 3 * BLK : 4 * BLK] = cg.astype(BF16)
            pr_ref[rs, 7 * BLK : 8 * BLK] = proj(7).astype(BF16)

        @pl.when(i == 0)
        def _():
            halo[...] = jnp.zeros_like(halo)
            state[...] = jnp.zeros_like(state)
            h = mc_ref[...]
            r = lax.rsqrt(jnp.mean(h * h, axis=-1, keepdims=True) + EPS)
            hn = (h * r * g_ref[...]).astype(BF16)
            hnT_ref[...] = hn.T
            work(hn, LIVE0)

        @pl.when(i > 0)
        def _():
            work(hn_ref[...], 0)

    row = lambda w: pl.BlockSpec((tm, w), lambda i: (i, 0))
    return pl.pallas_call(
        body,
        name="f1_inproj_conv",
        grid=(nt,),
        in_specs=[pl.BlockSpec((tm, D_MODEL), lambda i: (_token_tile(i), 0))] + [_VMEM] * 12,
        out_specs=(
            pl.BlockSpec((D_MODEL, CHUNK), lambda i: (0, 0)),
            row(N_PROJ * BLK),
            row(D_CONV),
            row(D_CONV),
            row(D_RET),
            pl.BlockSpec((nct, N_HEADS, HEAD_DIM, HEAD_DIM), lambda i: (i, 0, 0, 0)),
            _HBM,
        ),
        out_shape=(
            jax.ShapeDtypeStruct((D_MODEL, CHUNK), BF16),
            jax.ShapeDtypeStruct((n_rows, N_PROJ * BLK), BF16),
            jax.ShapeDtypeStruct((n_rows, D_CONV), BF16),
            jax.ShapeDtypeStruct((n_rows, D_CONV), BF16),
            jax.ShapeDtypeStruct((n_rows, D_RET), F32),
            jax.ShapeDtypeStruct((n_rows // CHUNK, N_HEADS, HEAD_DIM, HEAD_DIM), BF16),
            jax.ShapeDtypeStruct((N_DEV, *wo_shard.shape), BF16),
        ),
        scratch_shapes=[
            pltpu.VMEM((8, D_CONV), F32),
            pltpu.VMEM((N_HEADS, HEAD_DIM, HEAD_DIM), F32),
            pltpu.VMEM(wo_shard.shape, BF16),
            pltpu.SemaphoreType.DMA((GATHER_PAIRS,)),
            pltpu.SemaphoreType.DMA((GATHER_PAIRS,)),
            pltpu.SemaphoreType.DMA(()),
        ],
        compiler_params=_params(),
    )(hn_tok, meta_chunk, g1, w3, cw8, *rot, dec, zeta, xi, wo_shard)


def _group_norm(o):
    ys, rs = [], []
    for hd in range(N_HEADS):
        oh = o[:, hd * HEAD_DIM : (hd + 1) * HEAD_DIM]
        xc = oh - jnp.mean(oh, axis=-1, keepdims=True)
        rstd = lax.rsqrt(jnp.mean(xc * xc, axis=-1, keepdims=True) + EPS)
        ys.append(xc * rstd)
        rs.append(jnp.broadcast_to(rstd, oh.shape))
    return jnp.concatenate(ys, axis=1), jnp.concatenate(rs, axis=1)


def _f2(proj, o, conv_out, x2, t2, wout, gr, gf):
    n_rows = proj.shape[0]
    tm = ROW_TILE
    nt = n_rows // tm

    def body(o_ref, rg_ref, co_ref, x_ref, t_ref, wo_ref, gr_ref, gf_ref, dh2_ref, gw_ref, acc_ref, gacc):
        i = pl.program_id(0)

        @pl.when(i == 0)
        def _():
            acc_ref[...] = jnp.zeros_like(acc_ref)
            gacc[...] = jnp.zeros_like(gacc)
            dh2_ref[...] = jnp.zeros_like(dh2_ref)

        @pl.when(i > 0)
        def _():
            yh, _ = _group_norm(o_ref[...])
            rg = rg_ref[...].astype(F32)
            ro = (yh * gr_ref[...] * (rg * _sigmoid(rg))).astype(BF16)
            h2 = (
                x_ref[...]
                + jnp.dot(co_ref[...], wo_ref[0:D_CONV], preferred_element_type=F32)
                + jnp.dot(ro, wo_ref[D_CONV:], preferred_element_type=F32)
            )
            r2 = lax.rsqrt(jnp.mean(h2 * h2, axis=-1, keepdims=True) + EPS)
            yn = h2 * r2
            gfv = gf_ref[...]
            err = yn * gfv - t_ref[...]
            tile_loss = jnp.sum(jnp.sum(err * err, axis=-1, keepdims=True), axis=0, keepdims=True) * (0.5 / D_MODEL)
            acc_ref[0:1, :] += jnp.sum(err * yn, axis=0, keepdims=True) * (1.0 / D_MODEL)
            acc_ref[1:2, :] += tile_loss
            dyn = err * (gfv * (1.0 / D_MODEL))
            dh2 = r2 * (dyn - yn * jnp.mean(dyn * yn, axis=-1, keepdims=True))
            dh2_ref[...] = dh2
            dh2_bf = dh2.astype(BF16)
            gacc[0:D_CONV, :] += lax.dot_general(co_ref[...], dh2_bf, TN, preferred_element_type=F32)
            gacc[D_CONV:, :] += lax.dot_general(ro, dh2_bf, TN, preferred_element_type=F32)

        @pl.when(i == nt - 1)
        def _():
            gw_ref[...] = gacc[...].astype(BF16)

    row = lambda w, j=0: pl.BlockSpec((tm, w), lambda i: (i, j))
    tok = pl.BlockSpec((tm, D_MODEL), lambda i: (_token_tile(i), 0))
    return pl.pallas_call(
        body,
        name="f2_retention_out",
        grid=(nt,),
        in_specs=[row(D_RET), row(BLK, 7), row(D_CONV), tok, tok] + [_VMEM] * 3,
        out_specs=(
            row(D_MODEL),
            pl.BlockSpec((D_MODEL, D_MODEL), lambda i: (0, 0)),
            pl.BlockSpec((8, D_MODEL), lambda i: (0, 0)),
        ),
        out_shape=(
            jax.ShapeDtypeStruct((n_rows, D_MODEL), F32),
            jax.ShapeDtypeStruct((D_MODEL, D_MODEL), BF16),
            jax.ShapeDtypeStruct((8, D_MODEL), F32),
        ),
        scratch_shapes=[pltpu.VMEM((D_MODEL, D_MODEL), F32)],
        compiler_params=_params(),
    )(o, proj, conv_out, x2, t2, wout, gr, gf)


def _b1(dh2, proj, o, states, wout, gr, tables, zeta0, rot):
    n_rows = dh2.shape[0]
    tm = ROW_TILE
    nt = n_rows // tm
    chunk = B1_CHUNK
    nct = tm // chunk
    decay = tuple(math.exp(chunk * lg) for lg in LOG_G)

    def body(dh2_ref, pr_ref, o_ref, st_ref, wo_ref, gr_ref, dec_ref, dect_ref, zeta_ref, xi_ref, zeta0_ref,
             ca_ref, sa_ref, cb_ref, sb_ref, dp_ref, dco_ref, acc_ref, dstate):
        i = pl.program_id(0)
        tile = nt - 1 - i
        cos_all, sin_all = _tile_rotary(ca_ref, sa_ref, cb_ref, sb_ref, tile)

        @pl.when(i == 0)
        def _():
            dstate[...] = jnp.zeros_like(dstate)
            acc_ref[...] = jnp.zeros_like(acc_ref)

        @pl.when(tile > 0)
        def _():
            dmix = lax.dot_general(dh2_ref[...].astype(BF16), wo_ref[...], NT, preferred_element_type=F32)
            dco_ref[...] = dmix[:, :D_CONV].astype(BF16)
            dro = dmix[:, D_CONV:]
            rg = pr_ref[:, 3 * BLK : 4 * BLK].astype(F32)
            sg = _sigmoid(rg)
            silu, dsilu = rg * sg, sg * (1.0 + rg * (1.0 - sg))
            yh, rstd = _group_norm(o_ref[...])
            grv = gr_ref[...]
            dp_ref[:, 3 * BLK : 4 * BLK] = (dro * (yh * grv) * dsilu).astype(BF16)
            dret = dro * silu
            acc_ref[0:1, :] += jnp.sum(dret * yh, axis=0, keepdims=True)
            dyh = dret * grv
            for hd in range(N_HEADS):
                cs = slice(hd * HEAD_DIM, (hd + 1) * HEAD_DIM)
                a, b = dyh[:, cs], yh[:, cs]
                do_head = (
                    rstd[:, cs]
                    * (a - jnp.mean(a, axis=-1, keepdims=True) - b * jnp.mean(a * b, axis=-1, keepdims=True))
                ).astype(BF16)
                dupd = {}
                for c in range(nct):
                    rs = slice(c * chunk, (c + 1) * chunk)
                    qx = (pr_ref[rs, cs].astype(F32) * xi_ref[hd]).astype(BF16)
                    dupd[c] = lax.dot_general(qx, do_head[rs, :], TN, preferred_element_type=F32)
                dst = dstate[hd]
                for c in reversed(range(nct)):
                    rs = slice(c * chunk, (c + 1) * chunk)
                    cos_t, sin_t = cos_all[rs, :], sin_all[rs, :]
                    q = pr_ref[rs, hd * HEAD_DIM : (hd + 1) * HEAD_DIM]
                    k = pr_ref[rs, BLK + hd * HEAD_DIM : BLK + (hd + 1) * HEAD_DIM]
                    v = pr_ref[rs, 2 * BLK + hd * HEAD_DIM : 2 * BLK + (hd + 1) * HEAD_DIM]
                    do = do_head[rs, :]
                    st_bf = st_ref[c * (chunk // CHUNK), hd]
                    dst_bf = dst.astype(BF16)
                    zt, xt = zeta_ref[hd], xi_ref[hd]
                    sT = (lax.dot_general(k, q, NT, preferred_element_type=F32) * dect_ref[hd]).astype(BF16)
                    dsT = (lax.dot_general(v, do, NT, preferred_element_type=F32) * dect_ref[hd]).astype(BF16)
                    ds = (lax.dot_general(do, v, NT, preferred_element_type=F32) * dec_ref[hd]).astype(BF16)
                    kz = (k.astype(F32) * zt).astype(BF16)
                    dv = jnp.dot(sT, do, preferred_element_type=F32) + jnp.dot(kz, dst_bf, preferred_element_type=F32)
                    dq = jnp.dot(ds, k, preferred_element_type=F32) + xt * lax.dot_general(
                        do, st_bf, NT, preferred_element_type=F32
                    )
                    dk = jnp.dot(dsT, q, preferred_element_type=F32) + zt * lax.dot_general(
                        v, dst_bf, NT, preferred_element_type=F32
                    )
                    dst = decay[hd] * dst + dupd[c]
                    dp_ref[rs, hd * HEAD_DIM : (hd + 1) * HEAD_DIM] = _rot_bwd(dq * Q_SCALE, cos_t, sin_t).astype(BF16)
                    dp_ref[rs, BLK + hd * HEAD_DIM : BLK + (hd + 1) * HEAD_DIM] = _rot_bwd(dk, cos_t, sin_t).astype(BF16)
                    dp_ref[rs, 2 * BLK + hd * HEAD_DIM : 2 * BLK + (hd + 1) * HEAD_DIM] = dv.astype(BF16)
                dstate[hd] = dst

        @pl.when(tile == 0)
        def _():
            rs = slice(LIVE0, tm)
            cos_t, sin_t = cos_all[rs, :], sin_all[rs, :]
            zeros = jnp.zeros((tm - LIVE0, HEAD_DIM), BF16)
            for hd in range(N_HEADS):
                k = pr_ref[rs, BLK + hd * HEAD_DIM : BLK + (hd + 1) * HEAD_DIM]
                v = pr_ref[rs, 2 * BLK + hd * HEAD_DIM : 2 * BLK + (hd + 1) * HEAD_DIM]
                dst_bf = dstate[hd].astype(BF16)
                zt = zeta0_ref[hd]
                kz = (k.astype(F32) * zt).astype(BF16)
                dv = jnp.dot(kz, dst_bf, preferred_element_type=F32)
                dk = zt * lax.dot_general(v, dst_bf, NT, preferred_element_type=F32)
                dp_ref[rs, hd * HEAD_DIM : (hd + 1) * HEAD_DIM] = zeros
                dp_ref[rs, BLK + hd * HEAD_DIM : BLK + (hd + 1) * HEAD_DIM] = _rot_bwd(dk, cos_t, sin_t).astype(BF16)
                dp_ref[rs, 2 * BLK + hd * HEAD_DIM : 2 * BLK + (hd + 1) * HEAD_DIM] = dv.astype(BF16)
                dp_ref[rs, 3 * BLK + hd * HEAD_DIM : 3 * BLK + (hd + 1) * HEAD_DIM] = zeros
            dco_ref[rs, :] = jnp.zeros((tm - LIVE0, D_CONV), BF16)

    half = N_PROJ * BLK // 2
    rev = lambda w, j=0: pl.BlockSpec((tm, w), lambda i: (nt - 1 - i, j))
    return pl.pallas_call(
        body,
        name="b1_dret",
        grid=(nt,),
        in_specs=[
            rev(D_MODEL),
            rev(half, 1),
            rev(D_RET),
            pl.BlockSpec((tm // CHUNK, N_HEADS, HEAD_DIM, HEAD_DIM), lambda i: (nt - 1 - i, 0, 0, 0)),
        ]
        + [_VMEM] * 11,
        out_specs=(rev(half, 1), rev(D_CONV), pl.BlockSpec((8, D_RET), lambda i: (0, 0))),
        out_shape=(
            jax.ShapeDtypeStruct((n_rows, N_PROJ * BLK), BF16),
            jax.ShapeDtypeStruct((n_rows, D_CONV), BF16),
            jax.ShapeDtypeStruct((8, D_RET), F32),
        ),
        scratch_shapes=[pltpu.VMEM((N_HEADS, HEAD_DIM, HEAD_DIM), F32)],
        compiler_params=_params(),
    )(dh2, proj, o, states, wout, gr, *tables, zeta0, *rot)


def _b2(dproj, dco, proj, conv, w3, cw8, x2, meta_chunk, dh2, g1):
    n_rows = dproj.shape[0]
    tm = ROW_TILE
    nt = n_rows // tm
    half = N_PROJ * BLK // 2

    def body(dpr_ref, dco_ref, pr_ref, cv_ref, w_ref, cw_ref, x_ref, mc_ref, dh2_ref, g_ref,
             dpc_ref, gx_ref, dm_ref, acc_ref, accc_ref, halo):
        i = pl.program_id(0)
        tile = nt - 1 - i

        @pl.when(i == 0)
        def _():
            acc_ref[...] = jnp.zeros_like(acc_ref)
            accc_ref[...] = jnp.zeros_like(accc_ref)
            halo[...] = jnp.zeros_like(halo)

        def work(h, r0):
            rs = slice(r0, tm)
            n = tm - r0
            dco = dco_ref[rs, :].astype(F32)
            cx = pr_ref[rs, 0 * BLK : 1 * BLK].astype(F32)
            cb = pr_ref[rs, 1 * BLK : 2 * BLK].astype(F32)
            cc = pr_ref[rs, 2 * BLK : 3 * BLK].astype(F32)
            cg = pr_ref[rs, 3 * BLK : 4 * BLK].astype(F32)
            conv = cv_ref[rs, :].astype(F32)
            sg = _sigmoid(cg)
            silu, dsilu = cg * sg, sg * (1.0 + cg * (1.0 - sg))
            t = dco * cb
            dcb = (dco * conv * silu).astype(BF16)
            dcg = (t * conv * dsilu).astype(BF16)
            dconv = t * silu
            rows = lax.broadcasted_iota(jnp.int32, dconv.shape, 0)
            hl = halo[...]
            dc1 = jnp.where(rows == n - 1, hl[0:1], pltpu.roll(dconv, n - 1, 0))
            dc2 = jnp.where(rows == n - 2, hl[0:1], jnp.where(rows == n - 1, hl[1:2], pltpu.roll(dconv, n - 2, 0)))
            halo[...] = dconv[0:8]
            cw = cw_ref[...]
            du = cw[2:3] * dconv + cw[1:2] * dc1 + cw[0:1] * dc2
            u = cc * cx
            accc_ref[0:1, :] += jnp.sum(u * dc2, axis=0, keepdims=True)
            accc_ref[1:2, :] += jnp.sum(u * dc1, axis=0, keepdims=True)
            accc_ref[2:3, :] += jnp.sum(u * dconv, axis=0, keepdims=True)
            dcx = (du * cc).astype(BF16)
            dcc = (du * cx).astype(BF16)
            dhn = lax.dot_general(dpr_ref[rs, 0:BLK], w_ref[4], NT, preferred_element_type=F32)
            for j in range(1, 4):
                dhn += lax.dot_general(dpr_ref[rs, j * BLK : (j + 1) * BLK], w_ref[4 + j], NT, preferred_element_type=F32)
            for blk, d in ((1, dcb), (3, dcg), (0, dcx), (2, dcc)):
                dpc_ref[rs, blk * BLK : (blk + 1) * BLK] = d
                dhn += lax.dot_general(d, w_ref[blk], NT, preferred_element_type=F32)
            r = lax.rsqrt(jnp.mean(h * h, axis=-1, keepdims=True) + EPS)
            hh = h * r
            acc_ref[0:1, :] += jnp.sum(dhn * hh, axis=0, keepdims=True)
            dg = dhn * g_ref[...]
            return dh2_ref[rs, :] + r * (dg - hh * jnp.mean(dg * hh, axis=-1, keepdims=True))

        @pl.when(tile == 0)
        def _():
            dh = work(mc_ref[...], LIVE0)
            dm_ref[...] = dh[CHUNK - N_META : CHUNK]

        @pl.when(tile > 0)
        def _():
            gx_ref[...] = work(x_ref[...], 0)

    rev = lambda w, j=0: pl.BlockSpec((tm, w), lambda i: (nt - 1 - i, j))
    tok = pl.BlockSpec((tm, D_MODEL), lambda i: (_token_tile(nt - 1 - i), 0))
    const = lambda r, c: pl.BlockSpec((r, c), lambda i: (0, 0))
    return pl.pallas_call(
        body,
        name="b2_dconv_dh",
        grid=(nt,),
        in_specs=[rev(half, 1), rev(D_CONV), rev(half, 0), rev(D_CONV), _VMEM, _VMEM, tok, _VMEM, rev(D_MODEL), _VMEM],
        out_specs=(rev(half, 0), tok, const(N_META, D_MODEL), const(8, D_MODEL), const(8, D_CONV)),
        out_shape=(
            jax.ShapeDtypeStruct(dproj.shape, BF16),
            jax.ShapeDtypeStruct(x2.shape, F32),
            jax.ShapeDtypeStruct((N_META, D_MODEL), F32),
            jax.ShapeDtypeStruct((8, D_MODEL), F32),
            jax.ShapeDtypeStruct((8, D_CONV), F32),
        ),
        scratch_shapes=[pltpu.VMEM((8, D_CONV), F32)],
        input_output_aliases={0: 0},
        compiler_params=_params(),
    )(dproj, dco, proj, conv, w3, cw8, x2, meta_chunk, dh2, g1)


def _gw_in_scatter(hnT_tok, hnT_meta, dproj, gw_out_parts, dmeta_rows, acc_conv, acc_b2, acc_b1, acc_f2, me_arr):
    n_rows = dproj.shape[0]
    last = N_DEV - 1
    cols = D_CONV // N_DEV
    by_dest = (True, True, False)
    small_blocks = (gw_out_parts.shape[1:], (SC_CONV0 + 8, CHUNK), (GA_LOSS + 8, CHUNK))

    def body(me_ref, a_ref, am_ref, b_ref, go_ref, dm_ref, ac_ref, ab2_ref, ab1_ref, af2_ref,
             land_in, land_go, land_sc, land_ga, d2d_buf, d2d_land, ici_buf, sc_ref, ga_ref,
             d2d_send, d2d_recv, ici_send, ici_recv, send_sems, recv_sems, local_sems):
        del me_ref
        s = pl.program_id(0)
        t = last - s
        q = t >> 1
        x, y, c = lax.axis_index("x"), lax.axis_index("y"), lax.axis_index("c")
        me = 4 * x + 2 * y + c
        chip = 2 * x + y
        srcs, lands = (go_ref, sc_ref, ga_ref), (land_go, land_sc, land_ga)

        def peer_at(k):
            return (1 - x if k & 4 else x, 1 - y if k & 2 else y, 1 - c if k & 1 else c)

        def small_copy(a, k):
            px, py, pc = peer_at(k)
            return pltpu.make_async_remote_copy(
                src_ref=srcs[a].at[4 * px + 2 * py + pc] if by_dest[a] else srcs[a],
                dst_ref=lands[a].at[me],
                send_sem=send_sems.at[a * last + k - 1],
                recv_sem=recv_sems.at[a * last + k - 1],
                device_id=(px, py, pc),
                device_id_type=MESH,
            )

        def small_local(a):
            return pltpu.make_async_copy(srcs[a].at[me] if by_dest[a] else srcs[a], lands[a].at[me], local_sems.at[a + 1])

        def d2d_copy(j):
            return pltpu.make_async_remote_copy(
                src_ref=d2d_buf.at[j],
                dst_ref=d2d_land.at[j],
                send_sem=d2d_send.at[j],
                recv_sem=d2d_recv.at[j],
                device_id=(x, y, 1 - c),
                device_id_type=MESH,
            )

        def ici_copy(j, to):
            return pltpu.make_async_remote_copy(
                src_ref=ici_buf.at[j],
                dst_ref=land_in.at[chip],
                send_sem=ici_send.at[j],
                recv_sem=ici_recv.at[j],
                device_id=to,
                device_id_type=MESH,
            )

        def own_copy():
            return pltpu.make_async_copy(ici_buf.at[0], land_in.at[chip], local_sems.at[0])

        @pl.when(s == 0)
        def _():
            sc_ref[...] = jnp.zeros_like(sc_ref)
            for d in range(N_DEV):
                sc_ref[d, SC_META0 : SC_META0 + N_META, :] = dm_ref[:, d * CHUNK : (d + 1) * CHUNK]
                lane0 = (d * cols) % CHUNK
                rows8 = ac_ref[:, d * cols - lane0 : d * cols - lane0 + CHUNK]
                rows8 = pltpu.roll(rows8, CHUNK - lane0, 1) if lane0 else rows8
                sc_ref[d, SC_CONV0 : SC_CONV0 + 3, 0:cols] = rows8[0:3, 0:cols]
            ga_ref[...] = jnp.zeros_like(ga_ref)
            for row0, acc, n in ((GA_N1, ab2_ref, 8), (GA_RG, ab1_ref, 4), (GA_FG, af2_ref, 8)):
                for j in range(n):
                    ga_ref[row0 + j : row0 + j + 1, :] = acc[0:1, j * CHUNK : (j + 1) * CHUNK]
            ga_ref[GA_LOSS : GA_LOSS + 1, :] = af2_ref[1:2, 0:CHUNK]
            for a in range(3):
                small_local(a).start()
            for k in range(last, 0, -1):
                for a in range(3):
                    small_copy(a, k).start()

        blk = jnp.dot(a_ref[...], b_ref[ROW_TILE:, :], preferred_element_type=F32)
        blk += jnp.dot(am_ref[...], b_ref[LIVE0:ROW_TILE, :], preferred_element_type=F32)

        @pl.when((t & 1) == 1)
        def _():
            d2d_buf[q] = blk.astype(BF16)
            d2d_copy(q).start()

        @pl.when((t & 1) == 0)
        def _():
            d2d_copy(q).wait_recv()
            ici_buf[q] = (blk + d2d_land[q].astype(F32)).astype(BF16)

            @pl.when(t != 0)
            def _():
                ici_copy(q, (jnp.bitwise_xor(x, (t >> 2) & 1), jnp.bitwise_xor(y, (t >> 1) & 1), c)).start()

            @pl.when(t == 0)
            def _():
                own_copy().start()
                for j in range(N_CHIPS):
                    d2d_copy(j).wait_send()
                for j in range(N_CHIPS - 1, 0, -1):
                    ici_copy(j, peer_at(2 * j)).wait()
                for k in range(last, 0, -1):
                    for a in range(3):
                        small_copy(a, k).wait()
                for a in range(3):
                    small_local(a).wait()
                own_copy().wait()

    grid_spec = pltpu.PrefetchScalarGridSpec(
        num_scalar_prefetch=1,
        grid=(N_DEV,),
        in_specs=[_VMEM, _VMEM, pl.BlockSpec((n_rows, BLK), lambda s, me: (0, jnp.bitwise_xor(me[0], last - s))), _HBM]
        + [_VMEM] * 5,
        out_specs=tuple([_HBM] * 4),
        scratch_shapes=[
            pltpu.VMEM((N_CHIPS, D_MODEL, BLK), BF16),
            pltpu.VMEM((N_CHIPS, D_MODEL, BLK), BF16),
            pltpu.VMEM((N_CHIPS, D_MODEL, BLK), BF16),
            pltpu.VMEM((N_DEV, *small_blocks[1]), F32),
            pltpu.VMEM(small_blocks[2], F32),
            pltpu.SemaphoreType.DMA((N_CHIPS,)),
            pltpu.SemaphoreType.DMA((N_CHIPS,)),
            pltpu.SemaphoreType.DMA((N_CHIPS,)),
            pltpu.SemaphoreType.DMA((N_CHIPS,)),
            pltpu.SemaphoreType.DMA((3 * last,)),
            pltpu.SemaphoreType.DMA((3 * last,)),
            pltpu.SemaphoreType.DMA((4,)),
        ],
    )
    return pl.pallas_call(
        body,
        name="gw_in_scatter",
        grid_spec=grid_spec,
        out_shape=(jax.ShapeDtypeStruct((N_CHIPS, D_MODEL, BLK), BF16),)
        + tuple(jax.ShapeDtypeStruct((N_DEV, *b), dt) for b, dt in zip(small_blocks, (gw_out_parts.dtype, F32, F32), strict=True)),
        compiler_params=_params(),
    )(me_arr, hnT_tok, hnT_meta, dproj, gw_out_parts, dmeta_rows, acc_conv, acc_b2, acc_b1, acc_f2)


def _adamw(w, g, m, v):
    m = ADAM_B1 * m + (1.0 - ADAM_B1) * g
    v = ADAM_B2 * v + (1.0 - ADAM_B2) * (g * g)
    m_hat = m / (1.0 - ADAM_B1**ADAM_STEP)
    v_hat = v / (1.0 - ADAM_B2**ADAM_STEP)
    delta = -ADAM_LR * (m_hat / (jnp.sqrt(v_hat) + ADAM_EPS) + ADAM_WD * w)
    return delta, m, v


def _small_leaves(meta, conv_w, n1, rg, fg):
    return meta, conv_w, n1.reshape(8, CHUNK), rg.reshape(4, CHUNK), fg.reshape(8, CHUNK)


def _from_small_leaves(meta, conv_w, n1, rg, fg):
    return meta, conv_w, n1.reshape(D_MODEL), rg.reshape(D_RET), fg.reshape(D_MODEL)


ADAM_GRID = 4


def _adamw_all(land_in, land_out, land_sc, land_ga, big, small):
    n_leaf = 5
    lands = (land_in, land_out)
    n_big = len(big)

    def body(*refs):
        big_in, sc_ref, ga_ref = refs[: 4 * n_big], refs[4 * n_big], refs[4 * n_big + 1]
        small_in = refs[4 * n_big + 2 : 4 * n_big + 2 + 3 * n_leaf]
        big_out = refs[4 * n_big + 2 + 3 * n_leaf : 8 * n_big + 2 + 3 * n_leaf]
        small_out = refs[8 * n_big + 2 + 3 * n_leaf :]
        for b in range(n_big):
            p_ref, w_ref, m_ref, v_ref = big_in[4 * b : 4 * b + 4]
            g_ref, d_ref, nm_ref, nv_ref = big_out[4 * b : 4 * b + 4]
            g = p_ref[0].astype(F32)
            for s in range(1, p_ref.shape[0]):
                g = g + p_ref[s].astype(F32)
            g_ref[...] = g
            d_ref[...], nm_ref[...], nv_ref[...] = _adamw(w_ref[...], g, m_ref[...], v_ref[...])

        @pl.when(pl.program_id(0) == 0)
        def _():
            sc, ga = sc_ref[0], ga_ref[0]
            for s in range(1, N_DEV):
                sc = sc + sc_ref[s]
                ga = ga + ga_ref[s]
            grads = (
                sc[SC_META0 : SC_META0 + N_META],
                sc[SC_CONV0 : SC_CONV0 + 3, 0 : D_CONV // N_DEV],
                ga[GA_N1 : GA_N1 + 8],
                ga[GA_RG : GA_RG + 4],
                ga[GA_FG : GA_FG + 8],
            )
            for leaf, g in enumerate(grads):
                d, nm, nv = _adamw(
                    small_in[leaf][...], g, small_in[n_leaf + leaf][...], small_in[2 * n_leaf + leaf][...]
                )
                small_out[leaf][...] = g
                small_out[n_leaf + leaf][...] = d
                small_out[2 * n_leaf + leaf][...] = nm
                small_out[3 * n_leaf + leaf][...] = nv
            small_out[4 * n_leaf][...] = ga[GA_LOSS : GA_LOSS + 1]

    in_specs, out_specs, out_shape, operands = [], [], [], []
    for land, (w, m, v) in zip(lands, big, strict=True):
        n_r, n_c = w.shape
        tr = n_r // ADAM_GRID
        blk = pl.BlockSpec((tr, n_c), lambda i: (i, 0))
        in_specs += [pl.BlockSpec((land.shape[0], tr, n_c), lambda i: (0, i, 0)), blk, blk, blk]
        out_specs += [blk] * 4
        out_shape += [jax.ShapeDtypeStruct((n_r, n_c), F32)] * 4
        operands += [land, w, m, v]
    small_shapes = [jax.ShapeDtypeStruct(a.shape, F32) for a in small[0]] * 4 + [jax.ShapeDtypeStruct((1, CHUNK), F32)]
    out = pl.pallas_call(
        body,
        name="adamw_all",
        grid=(ADAM_GRID,),
        in_specs=in_specs + [_VMEM] * (2 + 3 * n_leaf),
        out_specs=tuple(out_specs + [_VMEM] * len(small_shapes)),
        out_shape=tuple(out_shape + small_shapes),
        compiler_params=_params(),
    )(*operands, land_sc, land_ga, *small[0], *small[1], *small[2])
    bigs = tuple(out[4 * b : 4 * b + 4] for b in range(n_big))
    rest = out[4 * n_big :]
    return bigs, tuple(rest[k * n_leaf : (k + 1) * n_leaf] for k in range(4)), rest[4 * n_leaf]


def kernel(x, meta, norm1_g, w_in, conv_w, ret_norm_g, w_out, final_g, loss_target, m_meta, m_norm1_g, m_w_in, m_conv_w, m_ret_norm_g, m_w_out, m_final_g, v_meta, v_norm1_g, v_w_in, v_conv_w, v_ret_norm_g, v_w_out, v_final_g):
    seq = x.shape[1]
    assert x.shape == (1, seq, D_MODEL) and seq % ROW_TILE == 0
    n_tiles = seq // ROW_TILE + 1
    x2, t2 = x[0], loss_target[0]

    small_w = _small_leaves(meta, conv_w, norm1_g, ret_norm_g, final_g)
    g1 = norm1_g.reshape(1, D_MODEL)
    hn_tok, hnT_tok, w3, meta_chunk, cw8 = _gather_weights_and_norm(x2, g1, w_in, meta, conv_w)

    rot = _rotary_tables(n_tiles, ROW_TILE)
    dec, _, zeta, xi = _decay_tables()
    gr = ret_norm_g.reshape(1, D_RET)
    gf = final_g.reshape(1, D_MODEL)

    hnT_meta, proj, conv_out, conv, o, states, wo3 = _f1(hn_tok, meta_chunk, g1, w3, cw8, rot, dec, zeta, xi, w_out)
    wout = wo3.reshape(D_MODEL, D_MODEL)
    dh2, gw_out, acc_f2 = _f2(proj, o, conv_out, x2, t2, wout, gr, gf)
    dproj_ret, dco, acc_b1 = _b1(dh2, proj, o, states, wout, gr, _decay_tables(B1_CHUNK), zeta, rot)
    dproj, grad_x2, dmeta_rows, acc_b2, acc_conv = _b2(dproj_ret, dco, proj, conv, w3, cw8, x2, meta_chunk, dh2, g1)
    gw_out_parts = gw_out.reshape(N_DEV, D_MODEL // N_DEV, D_MODEL)

    me_arr = (4 * lax.axis_index("x") + 2 * lax.axis_index("y") + lax.axis_index("c")).astype(jnp.int32).reshape(1)
    land_in, land_out, land_sc, land_ga = _gw_in_scatter(
        hnT_tok, hnT_meta, dproj, gw_out_parts, dmeta_rows, acc_conv, acc_b2, acc_b1, acc_f2, me_arr
    )

    (w_in_out, w_out_out), small_out, loss_row = _adamw_all(
        land_in,
        land_out,
        land_sc,
        land_ga,
        ((w_in, m_w_in, v_w_in), (w_out, m_w_out, v_w_out)),
        (
            small_w,
            _small_leaves(m_meta, m_conv_w, m_norm1_g, m_ret_norm_g, m_final_g),
            _small_leaves(v_meta, v_conv_w, v_norm1_g, v_ret_norm_g, v_final_g),
        ),
    )
    g_w_in, d_w_in, nm_w_in, nv_w_in = w_in_out
    g_w_out, d_w_out, nm_w_out, nv_w_out = w_out_out
    loss = loss_row[0, 0]
    grad_x = grad_x2.reshape(1, seq, D_MODEL)

    def leaves(w_in_leaf, w_out_leaf, small_leaves):
        meta_leaf, conv_leaf, n1_leaf, rg_leaf, fg_leaf = _from_small_leaves(*small_leaves)
        return (meta_leaf, n1_leaf, w_in_leaf, conv_leaf, rg_leaf, w_out_leaf, fg_leaf)

    return (
        loss,
        grad_x,
        *leaves(g_w_in, g_w_out, small_out[0]),
        *leaves(d_w_in, d_w_out, small_out[1]),
        *leaves(nm_w_in, nm_w_out, small_out[2]),
        *leaves(nv_w_in, nv_w_out, small_out[3]),
    )
```

```python
import math

import jax
import jax.numpy as jnp
import numpy as np
from jax import lax
from jax.experimental import pallas as pl
from jax.experimental.pallas import tpu as pltpu

F32 = jnp.float32
BF16 = jnp.bfloat16

N_DEV = 8
N_CHIPS = 4
D_MODEL = 1024
N_META = 16
CHUNK = 128
D_CONV = 512
D_RET = 512
N_HEADS = 4
HEAD_DIM = 128
N_PROJ = 8
BLK = 512
ROPE_BASE = 10000.0
EPS = 1e-6
Q_SCALE = HEAD_DIM ** -0.5
LOG_G = tuple(math.log(1.0 - 2.0 ** (-5.0 - h)) for h in range(N_HEADS))
CHUNK_DECAY = tuple(math.exp(CHUNK * lg) for lg in LOG_G)

ADAM_LR = 0.001
ADAM_B1 = 0.9
ADAM_B2 = 0.999
ADAM_EPS = 1e-08
ADAM_WD = 0.01
ADAM_STEP = 10

ROW_TILE = 512
PAD_ROWS = ROW_TILE - N_META
LIVE0 = ROW_TILE - CHUNK
B1_CHUNK = 256
VMEM_LIMIT = 56 * 1024 * 1024

SC_META0, SC_CONV0 = 0, 16
GA_N1, GA_RG, GA_FG, GA_LOSS = 0, 8, 16, 24

NT = (((1,), (1,)), ((), ()))
TN = (((0,), (0,)), ((), ()))
MESH = pl.DeviceIdType.MESH

_VMEM = pl.BlockSpec(memory_space=pltpu.VMEM)
_HBM = pl.BlockSpec(memory_space=pltpu.HBM)


def _params(n_axes=1):
    return pltpu.CompilerParams(dimension_semantics=("arbitrary",) * n_axes, vmem_limit_bytes=VMEM_LIMIT)


def _sigmoid(x):
    return 0.5 * jnp.tanh(0.5 * x) + 0.5


def _decay_tables(chunk=CHUNK):
    idx = np.arange(chunk, dtype=np.float64)
    diff = idx[:, None] - idx[None, :]
    dec = np.stack([np.where(diff >= 0, np.exp(diff * lg), 0.0) for lg in LOG_G])
    zeta = np.stack([np.exp((chunk - 1 - idx) * lg) for lg in LOG_G])
    xi = np.stack([np.exp((idx + 1.0) * lg) for lg in LOG_G])
    ones = np.ones((1, 1, HEAD_DIM))
    return (
        jnp.asarray(dec, F32),
        jnp.asarray(dec.transpose(0, 2, 1), F32),
        jnp.asarray(zeta[:, :, None] * ones, F32),
        jnp.asarray(xi[:, :, None] * ones, F32),
    )


def _rotary_tables(n_tiles, tm):
    half = HEAD_DIM // 2
    freqs = (1.0 / (np.float32(ROPE_BASE) ** (np.arange(half, dtype=np.float32) / np.float32(half)))).astype(np.float64)
    sign = np.concatenate([-np.ones(half), np.ones(half)])
    two = lambda a: np.concatenate([a, a], axis=1)
    base = two((np.arange(n_tiles, dtype=np.float64) * tm - PAD_ROWS)[:, None] * freqs[None, :])
    off = two(np.arange(tm, dtype=np.float64)[:, None] * freqs[None, :])
    as32 = lambda a: jnp.asarray(a, F32)
    return as32(np.cos(base)), as32(np.sin(base) * sign), as32(np.cos(off)), as32(np.sin(off) * sign)


def _tile_rotary(ca_ref, sa_ref, cb_ref, sb_ref, tile):
    ca, sa = ca_ref[pl.ds(tile, 1), :], sa_ref[pl.ds(tile, 1), :]
    cb, sb = cb_ref[...], sb_ref[...]
    return ca * cb - sa * sb, sa * cb + ca * sb


def _rot(t, cos2, sin2):
    return t * cos2 + pltpu.roll(t, HEAD_DIM // 2, 1) * sin2


def _rot_bwd(d, cos2, sin2):
    return d * cos2 + pltpu.roll(d * sin2, HEAD_DIM // 2, 1)


def _token_tile(i):
    return jnp.maximum(i - 1, 0)


GATHER_PAIRS = 7


def _two_level_gather(src_ref, out_ref, send_sems, recv_sems, local_sem, base=0):
    x, y, c = lax.axis_index("x"), lax.axis_index("y"), lax.axis_index("c")
    me, sibling = (x, y, c), (x, y, 1 - c)
    xnb, ynb, diag = (1 - x, y), (x, 1 - y), (1 - x, 1 - y)
    relayed = (jnp.bitwise_xor(x, 1 - c), jnp.bitwise_xor(y, c))
    other = (jnp.bitwise_xor(x, c), jnp.bitwise_xor(y, 1 - c))

    def copy(k, block, to, src=None):
        dst = out_ref.at[4 * block[0] + 2 * block[1] + block[2]]
        return pltpu.make_async_remote_copy(
            src_ref=dst if src is None else src,
            dst_ref=dst,
            send_sem=send_sems.at[base + k],
            recv_sem=recv_sems.at[base + k],
            device_id=to,
            device_id_type=MESH,
        )

    mine = pltpu.make_async_copy(src_ref, out_ref.at[4 * x + 2 * y + c], local_sem)
    first = [copy(1, me, (*xnb, c), src=src_ref), copy(2, me, (*ynb, c), src=src_ref), copy(0, me, sibling, src=src_ref)]
    relay = copy(3, (*relayed, c), (*other, c))
    passed = [copy(4 + j, (*chip, c), sibling) for j, chip in enumerate((xnb, ynb, diag))]

    def start():
        mine.start()
        for cp in first:
            cp.start()

    def forward():
        copy(1 + c, (*relayed, c), me).wait_recv()
        relay.start()
        copy(2 - c, (*other, c), me).wait_recv()
        passed[0].start()
        passed[1].start()

    def forward_relayed():
        copy(3, (*diag, c), me).wait_recv()
        passed[2].start()

    def finish():
        copy(0, sibling, me).wait_recv()
        for j, chip in enumerate((xnb, ynb, diag)):
            copy(4 + j, (*chip, 1 - c), me).wait_recv()
        for cp in first + [relay] + passed:
            cp.wait_send()
        mine.wait()

    return start, forward, forward_relayed, finish


W_IN_CHUNKS = 4


def _gather_weights_and_norm(x2, g1, w_shard, meta, conv_w):
    per = GATHER_PAIRS
    n_ch = W_IN_CHUNKS
    rows = w_shard.shape[0] // n_ch
    tm = ROW_TILE
    nt = x2.shape[0] // tm
    cols = D_CONV // N_DEV
    pack_shape = (SC_CONV0 + 8, CHUNK)
    starts = [0] * (n_ch + 1)
    forwards = [min(3 + 2 * p, nt - 1) for p in range(n_ch)] + [min(10, nt - 1)]
    relayeds = [min(10 + p, nt - 1) for p in range(n_ch)] + [nt - 1]
    finishes = [nt - 1] * (n_ch + 1)

    def body(x_ref, g_ref, w_ref, meta_ref, conv_ref, hn_ref, hnT_ref, w_all_ref, mc_ref, cw_ref,
             w_bf, small_ref, small_all_ref, send_sems, recv_sems, local_sems):
        i = pl.program_id(0)

        @pl.when(i == 0)
        def _():
            w_bf[...] = w_ref[...].astype(BF16)
            small_ref[...] = jnp.zeros_like(small_ref)
            small_ref[SC_META0 : SC_META0 + N_META, :] = meta_ref[...]
            small_ref[SC_CONV0 : SC_CONV0 + 3, 0:cols] = conv_ref[...]

        parts = [
            _two_level_gather(
                w_bf.at[pl.ds(ch * rows, rows)],
                w_all_ref.at[:, pl.ds(ch * rows, rows)],
                send_sems,
                recv_sems,
                local_sems.at[ch],
                ch * per,
            )
            for ch in range(n_ch)
        ]
        parts.append(_two_level_gather(small_ref, small_all_ref, send_sems, recv_sems, local_sems.at[n_ch], n_ch * per))
        for phase, steps in enumerate((starts, forwards, relayeds, finishes)):
            for part, step in zip(parts, steps, strict=True):
                pl.when(i == step)(part[phase])

        h = x_ref[...]
        r = lax.rsqrt(jnp.mean(h * h, axis=-1, keepdims=True) + EPS)
        hn = (h * r * g_ref[...]).astype(BF16)
        hn_ref[...] = hn
        hnT_ref[...] = hn.T

        @pl.when(i == nt - 1)
        def _():
            mc_ref[0 : CHUNK - N_META, :] = jnp.zeros((CHUNK - N_META, D_MODEL), F32)
            cw_ref[...] = jnp.zeros_like(cw_ref)
            for d in range(N_DEV):
                mc_ref[CHUNK - N_META : CHUNK, d * CHUNK : (d + 1) * CHUNK] = small_all_ref[d, SC_META0 : SC_META0 + N_META, :]
                lane0 = (d * cols) % CHUNK
                rows8 = small_all_ref[d, SC_CONV0 : SC_CONV0 + 8, :]
                rows8 = pltpu.roll(rows8, lane0, 1) if lane0 else rows8
                cw_ref[0:3, d * cols : (d + 1) * cols] = rows8[0:3, lane0 : lane0 + cols]

    const = lambda r, c: pl.BlockSpec((r, c), lambda i: (0, 0))
    return pl.pallas_call(
        body,
        name="gather_weights_norm",
        grid=(nt,),
        out_shape=(
            jax.ShapeDtypeStruct(x2.shape, BF16),
            jax.ShapeDtypeStruct(x2.shape[::-1], BF16),
            jax.ShapeDtypeStruct((N_DEV, *w_shard.shape), BF16),
            jax.ShapeDtypeStruct((CHUNK, D_MODEL), F32),
            jax.ShapeDtypeStruct((8, D_CONV), F32),
        ),
        in_specs=[pl.BlockSpec((tm, D_MODEL), lambda i: (i, 0)), _VMEM, _VMEM, _VMEM, _VMEM],
        out_specs=(
            pl.BlockSpec((tm, D_MODEL), lambda i: (i, 0)),
            pl.BlockSpec((D_MODEL, tm), lambda i: (0, i)),
            _HBM,
            const(CHUNK, D_MODEL),
            const(8, D_CONV),
        ),
        scratch_shapes=[
            pltpu.VMEM(w_shard.shape, BF16),
            pltpu.VMEM(pack_shape, F32),
            pltpu.VMEM((N_DEV, *pack_shape), F32),
            pltpu.SemaphoreType.DMA(((n_ch + 1) * per,)),
            pltpu.SemaphoreType.DMA(((n_ch + 1) * per,)),
            pltpu.SemaphoreType.DMA((n_ch + 1,)),
        ],
        compiler_params=_params(),
    )(x2, g1, w_shard, meta, conv_w)


def _f1(hn_tok, meta_chunk, g1, w3, cw8, rot, dec, zeta, xi, wo_shard):
    tm = ROW_TILE
    nt = hn_tok.shape[0] // tm + 1
    n_rows = nt * tm
    nct = tm // CHUNK

    def body(hn_ref, mc_ref, g_ref, w_ref, cw_ref, ca_ref, sa_ref, cb_ref, sb_ref, dec_ref, zeta_ref, xi_ref, wo_ref,
             hnT_ref, pr_ref, co_ref, cv_ref, o_ref, st_ref, wo_all_ref,
             halo, state, wo_bf, send_sems, recv_sems, local_sem):
        i = pl.program_id(0)

        @pl.when(i == 0)
        def _():
            wo_bf[...] = wo_ref[...].astype(BF16)

        wo_phases = _two_level_gather(wo_bf, wo_all_ref, send_sems, recv_sems, local_sem)
        for step, phase in zip((0, nt // 3, 2 * nt // 3, nt - 1), wo_phases, strict=True):
            pl.when(i == step)(phase)

        def work(hn, r0):
            rs = slice(r0, tm)
            n = tm - r0

            def proj(j):
                return jnp.dot(hn, w_ref[j], preferred_element_type=F32)

            cos_all, sin_all = _tile_rotary(ca_ref, sa_ref, cb_ref, sb_ref, i)
            cos_t, sin_t = cos_all[rs, :], sin_all[rs, :]
            q, k, v = proj(4), proj(5), proj(6).astype(BF16)
            pr_ref[rs, 6 * BLK : 7 * BLK] = v
            chunks = range(r0 // CHUNK, tm // CHUNK)
            heads = []
            for hd in range(N_HEADS):
                cs = slice(hd * HEAD_DIM, (hd + 1) * HEAD_DIM)
                qh = (_rot(q[:, cs], cos_t, sin_t) * Q_SCALE).astype(BF16)
                kh = _rot(k[:, cs], cos_t, sin_t).astype(BF16)
                vh = v[:, cs]
                pr_ref[rs, 4 * BLK + hd * HEAD_DIM : 4 * BLK + (hd + 1) * HEAD_DIM] = qh
                pr_ref[rs, 5 * BLK + hd * HEAD_DIM : 5 * BLK + (hd + 1) * HEAD_DIM] = kh
                upd = {}
                for c in chunks:
                    cr = slice(c * CHUNK - r0, (c + 1) * CHUNK - r0)
                    kz = (kh[cr].astype(F32) * zeta_ref[hd]).astype(BF16)
                    upd[c] = lax.dot_general(kz, vh[cr], TN, preferred_element_type=F32)
                st = state[hd]
                outs = []
                for c in chunks:
                    cr = slice(c * CHUNK - r0, (c + 1) * CHUNK - r0)
                    qc, kc, vc = qh[cr], kh[cr], vh[cr]
                    st_bf = st.astype(BF16)
                    st_ref[c, hd] = st_bf
                    s = lax.dot_general(qc, kc, NT, preferred_element_type=F32) * dec_ref[hd]
                    inner = jnp.dot(s.astype(BF16), vc, preferred_element_type=F32)
                    qx = (qc.astype(F32) * xi_ref[hd]).astype(BF16)
                    outs.append(inner + jnp.dot(qx, st_bf, preferred_element_type=F32))
                    st = CHUNK_DECAY[hd] * st + upd[c]
                state[hd] = st
                heads.append(jnp.concatenate(outs, axis=0) if len(outs) > 1 else outs[0])
            o_ref[rs, :] = jnp.concatenate(heads, axis=1)

            cx, cb, cc, cg = proj(0), proj(1), proj(2), proj(3)
            u = cc * cx
            rows = lax.broadcasted_iota(jnp.int32, u.shape, 0)
            hl = halo[...]
            u1 = jnp.where(rows == 0, hl[7:8], pltpu.roll(u, 1, 0))
            u2 = jnp.where(rows == 0, hl[6:7], jnp.where(rows == 1, hl[7:8], pltpu.roll(u, 2, 0)))
            halo[...] = u[n - 8 : n]
            cw = cw_ref[...]
            conv = cw[0:1] * u2 + cw[1:2] * u1 + cw[2:3] * u
            co = (cb * conv * (cg * _sigmoid(cg))).astype(BF16)
            co_ref[rs, :] = co
            cv_ref[rs, :] = conv.astype(BF16)
            pr_ref[rs, 0 * BLK : 1 * BLK] = cx.astype(BF16)
            pr_ref[rs, 1 * BLK : 2 * BLK] = cb.astype(BF16)
            pr_ref[rs, 2 * BLK : 3 * BLK] = cc.astype(BF16)
            pr_ref[rs, 3 * BLK : 4 * BLK] = cg.astype(BF16)
            pr_ref[rs, 7 * BLK : 8 * BLK] = proj(7).astype(BF16)

        @pl.when(i == 0)
        def _():
            halo[...] = jnp.zeros_like(halo)
            state[...] = jnp.zeros_like(state)
            h = mc_ref[...]
            r = lax.rsqrt(jnp.mean(h * h, axis=-1, keepdims=True) + EPS)
            hn = (h * r * g_ref[...]).astype(BF16)
            hnT_ref[...] = hn.T
            work(hn, LIVE0)

        @pl.when(i > 0)
        def _():
            work(hn_ref[...], 0)

    row = lambda w: pl.BlockSpec((tm, w), lambda i: (i, 0))
    return pl.pallas_call(
        body,
        name="f1_inproj_conv",
        grid=(nt,),
        in_specs=[pl.BlockSpec((tm, D_MODEL), lambda i: (_token_tile(i), 0))] + [_VMEM] * 12,
        out_specs=(
            pl.BlockSpec((D_MODEL, CHUNK), lambda i: (0, 0)),
            row(N_PROJ * BLK),
            row(D_CONV),
            row(D_CONV),
            row(D_RET),
            pl.BlockSpec((nct, N_HEADS, HEAD_DIM, HEAD_DIM), lambda i: (i, 0, 0, 0)),
            _HBM,
        ),
        out_shape=(
            jax.ShapeDtypeStruct((D_MODEL, CHUNK), BF16),
            jax.ShapeDtypeStruct((n_rows, N_PROJ * BLK), BF16),
            jax.ShapeDtypeStruct((n_rows, D_CONV), BF16),
            jax.ShapeDtypeStruct((n_rows, D_CONV), BF16),
            jax.ShapeDtypeStruct((n_rows, D_RET), F32),
            jax.ShapeDtypeStruct((n_rows // CHUNK, N_HEADS, HEAD_DIM, HEAD_DIM), BF16),
            jax.ShapeDtypeStruct((N_DEV, *wo_shard.shape), BF16),
        ),
        scratch_shapes=[
            pltpu.VMEM((8, D_CONV), F32),
            pltpu.VMEM((N_HEADS, HEAD_DIM, HEAD_DIM), F32),
            pltpu.VMEM(wo_shard.shape, BF16),
            pltpu.SemaphoreType.DMA((GATHER_PAIRS,)),
            pltpu.SemaphoreType.DMA((GATHER_PAIRS,)),
            pltpu.SemaphoreType.DMA(()),
        ],
        compiler_params=_params(),
    )(hn_tok, meta_chunk, g1, w3, cw8, *rot, dec, zeta, xi, wo_shard)


def _group_norm(o):
    ys, rs = [], []
    for hd in range(N_HEADS):
        oh = o[:, hd * HEAD_DIM : (hd + 1) * HEAD_DIM]
        xc = oh - jnp.mean(oh, axis=-1, keepdims=True)
        rstd = lax.rsqrt(jnp.mean(xc * xc, axis=-1, keepdims=True) + EPS)
        ys.append(xc * rstd)
        rs.append(jnp.broadcast_to(rstd, oh.shape))
    return jnp.concatenate(ys, axis=1), jnp.concatenate(rs, axis=1)


def _f2(proj, o, conv_out, x2, t2, wout, gr, gf):
    n_rows = proj.shape[0]
    tm = ROW_TILE
    nt = n_rows // tm

    def body(o_ref, rg_ref, co_ref, x_ref, t_ref, wo_ref, gr_ref, gf_ref, dh2_ref, gw_ref, acc_ref, gacc):
        i = pl.program_id(0)

        @pl.when(i == 0)
        def _():
            acc_ref[...] = jnp.zeros_like(acc_ref)
            gacc[...] = jnp.zeros_like(gacc)
            dh2_ref[...] = jnp.zeros_like(dh2_ref)

        @pl.when(i > 0)
        def _():
            yh, _ = _group_norm(o_ref[...])
            rg = rg_ref[...].astype(F32)
            ro = (yh * gr_ref[...] * (rg * _sigmoid(rg))).astype(BF16)
            h2 = (
                x_ref[...]
                + jnp.dot(co_ref[...], wo_ref[0:D_CONV], preferred_element_type=F32)
                + jnp.dot(ro, wo_ref[D_CONV:], preferred_element_type=F32)
            )
            r2 = lax.rsqrt(jnp.mean(h2 * h2, axis=-1, keepdims=True) + EPS)
            yn = h2 * r2
            gfv = gf_ref[...]
            err = yn * gfv - t_ref[...]
            tile_loss = jnp.sum(jnp.sum(err * err, axis=-1, keepdims=True), axis=0, keepdims=True) * (0.5 / D_MODEL)
            acc_ref[0:1, :] += jnp.sum(err * yn, axis=0, keepdims=True) * (1.0 / D_MODEL)
            acc_ref[1:2, :] += tile_loss
            dyn = err * (gfv * (1.0 / D_MODEL))
            dh2 = r2 * (dyn - yn * jnp.mean(dyn * yn, axis=-1, keepdims=True))
            dh2_ref[...] = dh2
            dh2_bf = dh2.astype(BF16)
            gacc[0:D_CONV, :] += lax.dot_general(co_ref[...], dh2_bf, TN, preferred_element_type=F32)
            gacc[D_CONV:, :] += lax.dot_general(ro, dh2_bf, TN, preferred_element_type=F32)

        @pl.when(i == nt - 1)
        def _():
            gw_ref[...] = gacc[...].astype(BF16)

    row = lambda w, j=0: pl.BlockSpec((tm, w), lambda i: (i, j))
    tok = pl.BlockSpec((tm, D_MODEL), lambda i: (_token_tile(i), 0))
    return pl.pallas_call(
        body,
        name="f2_retention_out",
        grid=(nt,),
        in_specs=[row(D_RET), row(BLK, 7), row(D_CONV), tok, tok] + [_VMEM] * 3,
        out_specs=(
            row(D_MODEL),
            pl.BlockSpec((D_MODEL, D_MODEL), lambda i: (0, 0)),
            pl.BlockSpec((8, D_MODEL), lambda i: (0, 0)),
        ),
        out_shape=(
            jax.ShapeDtypeStruct((n_rows, D_MODEL), F32),
            jax.ShapeDtypeStruct((D_MODEL, D_MODEL), BF16),
            jax.ShapeDtypeStruct((8, D_MODEL), F32),
        ),
        scratch_shapes=[pltpu.VMEM((D_MODEL, D_MODEL), F32)],
        compiler_params=_params(),
    )(o, proj, conv_out, x2, t2, wout, gr, gf)


def _b1(dh2, proj, o, states, wout, gr, tables, zeta0, rot):
    n_rows = dh2.shape[0]
    tm = ROW_TILE
    nt = n_rows // tm
    chunk = B1_CHUNK
    nct = tm // chunk
    decay = tuple(math.exp(chunk * lg) for lg in LOG_G)

    def body(dh2_ref, pr_ref, o_ref, st_ref, wo_ref, gr_ref, dec_ref, dect_ref, zeta_ref, xi_ref, zeta0_ref,
             ca_ref, sa_ref, cb_ref, sb_ref, dp_ref, dco_ref, acc_ref, dstate):
        i = pl.program_id(0)
        tile = nt - 1 - i
        cos_all, sin_all = _tile_rotary(ca_ref, sa_ref, cb_ref, sb_ref, tile)

        @pl.when(i == 0)
        def _():
            dstate[...] = jnp.zeros_like(dstate)
            acc_ref[...] = jnp.zeros_like(acc_ref)

        @pl.when(tile > 0)
        def _():
            dmix = lax.dot_general(dh2_ref[...].astype(BF16), wo_ref[...], NT, preferred_element_type=F32)
            dco_ref[...] = dmix[:, :D_CONV].astype(BF16)
            dro = dmix[:, D_CONV:]
            rg = pr_ref[:, 3 * BLK : 4 * BLK].astype(F32)
            sg = _sigmoid(rg)
            silu, dsilu = rg * sg, sg * (1.0 + rg * (1.0 - sg))
            yh, rstd = _group_norm(o_ref[...])
            grv = gr_ref[...]
            dp_ref[:, 3 * BLK : 4 * BLK] = (dro * (yh * grv) * dsilu).astype(BF16)
            dret = dro * silu
            acc_ref[0:1, :] += jnp.sum(dret * yh, axis=0, keepdims=True)
            dyh = dret * grv
            for hd in range(N_HEADS):
                cs = slice(hd * HEAD_DIM, (hd + 1) * HEAD_DIM)
                a, b = dyh[:, cs], yh[:, cs]
                do_head = (
                    rstd[:, cs]
                    * (a - jnp.mean(a, axis=-1, keepdims=True) - b * jnp.mean(a * b, axis=-1, keepdims=True))
                ).astype(BF16)
                dupd = {}
                for c in range(nct):
                    rs = slice(c * chunk, (c + 1) * chunk)
                    qx = (pr_ref[rs, cs].astype(F32) * xi_ref[hd]).astype(BF16)
                    dupd[c] = lax.dot_general(qx, do_head[rs, :], TN, preferred_element_type=F32)
                dst = dstate[hd]
                for c in reversed(range(nct)):
                    rs = slice(c * chunk, (c + 1) * chunk)
                    cos_t, sin_t = cos_all[rs, :], sin_all[rs, :]
                    q = pr_ref[rs, hd * HEAD_DIM : (hd + 1) * HEAD_DIM]
                    k = pr_ref[rs, BLK + hd * HEAD_DIM : BLK + (hd + 1) * HEAD_DIM]
                    v = pr_ref[rs, 2 * BLK + hd * HEAD_DIM : 2 * BLK + (hd + 1) * HEAD_DIM]
                    do = do_head[rs, :]
                    st_bf = st_ref[c * (chunk // CHUNK), hd]
                    dst_bf = dst.astype(BF16)
                    zt, xt = zeta_ref[hd], xi_ref[hd]
                    sT = (lax.dot_general(k, q, NT, preferred_element_type=F32) * dect_ref[hd]).astype(BF16)
                    dsT = (lax.dot_general(v, do, NT, preferred_element_type=F32) * dect_ref[hd]).astype(BF16)
                    ds = (lax.dot_general(do, v, NT, preferred_element_type=F32) * dec_ref[hd]).astype(BF16)
                    kz = (k.astype(F32) * zt).astype(BF16)
                    dv = jnp.dot(sT, do, preferred_element_type=F32) + jnp.dot(kz, dst_bf, preferred_element_type=F32)
                    dq = jnp.dot(ds, k, preferred_element_type=F32) + xt * lax.dot_general(
                        do, st_bf, NT, preferred_element_type=F32
                    )
                    dk = jnp.dot(dsT, q, preferred_element_type=F32) + zt * lax.dot_general(
                        v, dst_bf, NT, preferred_element_type=F32
                    )
                    dst = decay[hd] * dst + dupd[c]
                    dp_ref[rs, hd * HEAD_DIM : (hd + 1) * HEAD_DIM] = _rot_bwd(dq * Q_SCALE, cos_t, sin_t).astype(BF16)
                    dp_ref[rs, BLK + hd * HEAD_DIM : BLK + (hd + 1) * HEAD_DIM] = _rot_bwd(dk, cos_t, sin_t).astype(BF16)
                    dp_ref[rs, 2 * BLK + hd * HEAD_DIM : 2 * BLK + (hd + 1) * HEAD_DIM] = dv.astype(BF16)
                dstate[hd] = dst

        @pl.when(tile == 0)
        def _():
            rs = slice(LIVE0, tm)
            cos_t, sin_t = cos_all[rs, :], sin_all[rs, :]
            zeros = jnp.zeros((tm - LIVE0, HEAD_DIM), BF16)
            for hd in range(N_HEADS):
                k = pr_ref[rs, BLK + hd * HEAD_DIM : BLK + (hd + 1) * HEAD_DIM]
                v = pr_ref[rs, 2 * BLK + hd * HEAD_DIM : 2 * BLK + (hd + 1) * HEAD_DIM]
                dst_bf = dstate[hd].astype(BF16)
                zt = zeta0_ref[hd]
                kz = (k.astype(F32) * zt).astype(BF16)
                dv = jnp.dot(kz, dst_bf, preferred_element_type=F32)
                dk = zt * lax.dot_general(v, dst_bf, NT, preferred_element_type=F32)
                dp_ref[rs, hd * HEAD_DIM : (hd + 1) * HEAD_DIM] = zeros
                dp_ref[rs, BLK + hd * HEAD_DIM : BLK + (hd + 1) * HEAD_DIM] = _rot_bwd(dk, cos_t, sin_t).astype(BF16)
                dp_ref[rs, 2 * BLK + hd * HEAD_DIM : 2 * BLK + (hd + 1) * HEAD_DIM] = dv.astype(BF16)
                dp_ref[rs, 3 * BLK + hd * HEAD_DIM : 3 * BLK + (hd + 1) * HEAD_DIM] = zeros
            dco_ref[rs, :] = jnp.zeros((tm - LIVE0, D_CONV), BF16)

    half = N_PROJ * BLK // 2
    rev = lambda w, j=0: pl.BlockSpec((tm, w), lambda i: (nt - 1 - i, j))
    return pl.pallas_call(
        body,
        name="b1_dret",
        grid=(nt,),
        in_specs=[
            rev(D_MODEL),
            rev(half, 1),
            rev(D_RET),
            pl.BlockSpec((tm // CHUNK, N_HEADS, HEAD_DIM, HEAD_DIM), lambda i: (nt - 1 - i, 0, 0, 0)),
        ]
        + [_VMEM] * 11,
        out_specs=(rev(half, 1), rev(D_CONV), pl.BlockSpec((8, D_RET), lambda i: (0, 0))),
        out_shape=(
            jax.ShapeDtypeStruct((n_rows, N_PROJ * BLK), BF16),
            jax.ShapeDtypeStruct((n_rows, D_CONV), BF16),
            jax.ShapeDtypeStruct((8, D_RET), F32),
        ),
        scratch_shapes=[pltpu.VMEM((N_HEADS, HEAD_DIM, HEAD_DIM), F32)],
        compiler_params=_params(),
    )(dh2, proj, o, states, wout, gr, *tables, zeta0, *rot)


def _b2(dproj, dco, proj, conv, w3, cw8, x2, meta_chunk, dh2, g1):
    n_rows = dproj.shape[0]
    tm = ROW_TILE
    nt = n_rows // tm
    half = N_PROJ * BLK // 2

    def body(dpr_ref, dco_ref, pr_ref, cv_ref, w_ref, cw_ref, x_ref, mc_ref, dh2_ref, g_ref,
             dpc_ref, gx_ref, dm_ref, acc_ref, accc_ref, halo):
        i = pl.program_id(0)
        tile = nt - 1 - i

        @pl.when(i == 0)
        def _():
            acc_ref[...] = jnp.zeros_like(acc_ref)
            accc_ref[...] = jnp.zeros_like(accc_ref)
            halo[...] = jnp.zeros_like(halo)

        def work(h, r0):
            rs = slice(r0, tm)
            n = tm - r0
            dco = dco_ref[rs, :].astype(F32)
            cx = pr_ref[rs, 0 * BLK : 1 * BLK].astype(F32)
            cb = pr_ref[rs, 1 * BLK : 2 * BLK].astype(F32)
            cc = pr_ref[rs, 2 * BLK : 3 * BLK].astype(F32)
            cg = pr_ref[rs, 3 * BLK : 4 * BLK].astype(F32)
            conv = cv_ref[rs, :].astype(F32)
            sg = _sigmoid(cg)
            silu, dsilu = cg * sg, sg * (1.0 + cg * (1.0 - sg))
            t = dco * cb
            dcb = (dco * conv * silu).astype(BF16)
            dcg = (t * conv * dsilu).astype(BF16)
            dconv = t * silu
            rows = lax.broadcasted_iota(jnp.int32, dconv.shape, 0)
            hl = halo[...]
            dc1 = jnp.where(rows == n - 1, hl[0:1], pltpu.roll(dconv, n - 1, 0))
            dc2 = jnp.where(rows == n - 2, hl[0:1], jnp.where(rows == n - 1, hl[1:2], pltpu.roll(dconv, n - 2, 0)))
            halo[...] = dconv[0:8]
            cw = cw_ref[...]
            du = cw[2:3] * dconv + cw[1:2] * dc1 + cw[0:1] * dc2
            u = cc * cx
            accc_ref[0:1, :] += jnp.sum(u * dc2, axis=0, keepdims=True)
            accc_ref[1:2, :] += jnp.sum(u * dc1, axis=0, keepdims=True)
            accc_ref[2:3, :] += jnp.sum(u * dconv, axis=0, keepdims=True)
            dcx = (du * cc).astype(BF16)
            dcc = (du * cx).astype(BF16)
            dhn = lax.dot_general(dpr_ref[rs, 0:BLK], w_ref[4], NT, preferred_element_type=F32)
            for j in range(1, 4):
                dhn += lax.dot_general(dpr_ref[rs, j * BLK : (j + 1) * BLK], w_ref[4 + j], NT, preferred_element_type=F32)
            for blk, d in ((1, dcb), (3, dcg), (0, dcx), (2, dcc)):
                dpc_ref[rs, blk * BLK : (blk + 1) * BLK] = d
                dhn += lax.dot_general(d, w_ref[blk], NT, preferred_element_type=F32)
            r = lax.rsqrt(jnp.mean(h * h, axis=-1, keepdims=True) + EPS)
            hh = h * r
            acc_ref[0:1, :] += jnp.sum(dhn * hh, axis=0, keepdims=True)
            dg = dhn * g_ref[...]
            return dh2_ref[rs, :] + r * (dg - hh * jnp.mean(dg * hh, axis=-1, keepdims=True))

        @pl.when(tile == 0)
        def _():
            dh = work(mc_ref[...], LIVE0)
            dm_ref[...] = dh[CHUNK - N_META : CHUNK]

        @pl.when(tile > 0)
        def _():
            gx_ref[...] = work(x_ref[...], 0)

    rev = lambda w, j=0: pl.BlockSpec((tm, w), lambda i: (nt - 1 - i, j))
    tok = pl.BlockSpec((tm, D_MODEL), lambda i: (_token_tile(nt - 1 - i), 0))
    const = lambda r, c: pl.BlockSpec((r, c), lambda i: (0, 0))
    return pl.pallas_call(
        body,
        name="b2_dconv_dh",
        grid=(nt,),
        in_specs=[rev(half, 1), rev(D_CONV), rev(half, 0), rev(D_CONV), _VMEM, _VMEM, tok, _VMEM, rev(D_MODEL), _VMEM],
        out_specs=(rev(half, 0), tok, const(N_META, D_MODEL), const(8, D_MODEL), const(8, D_CONV)),
        out_shape=(
            jax.ShapeDtypeStruct(dproj.shape, BF16),
            jax.ShapeDtypeStruct(x2.shape, F32),
            jax.ShapeDtypeStruct((N_META, D_MODEL), F32),
            jax.ShapeDtypeStruct((8, D_MODEL), F32),
            jax.ShapeDtypeStruct((8, D_CONV), F32),
        ),
        scratch_shapes=[pltpu.VMEM((8, D_CONV), F32)],
        input_output_aliases={0: 0},
        compiler_params=_params(),
    )(dproj, dco, proj, conv, w3, cw8, x2, meta_chunk, dh2, g1)


def _gw_in_scatter(hnT_tok, hnT_meta, dproj, gw_out_parts, dmeta_rows, acc_conv, acc_b2, acc_b1, acc_f2, me_arr):
    n_rows = dproj.shape[0]
    last = N_DEV - 1
    cols = D_CONV // N_DEV
    by_dest = (True, True, False)
    small_blocks = (gw_out_parts.shape[1:], (SC_CONV0 + 8, CHUNK), (GA_LOSS + 8, CHUNK))

    def body(me_ref, a_ref, am_ref, b_ref, go_ref, dm_ref, ac_ref, ab2_ref, ab1_ref, af2_ref,
             land_in, land_go, land_sc, land_ga, d2d_buf, d2d_land, ici_buf, sc_ref, ga_ref,
             d2d_send, d2d_recv, ici_send, ici_recv, send_sems, recv_sems, local_sems):
        del me_ref
        s = pl.program_id(0)
        t = last - s
        q = t >> 1
        x, y, c = lax.axis_index("x"), lax.axis_index("y"), lax.axis_index("c")
        me = 4 * x + 2 * y + c
        chip = 2 * x + y
        srcs, lands = (go_ref, sc_ref, ga_ref), (land_go, land_sc, land_ga)

        def peer_at(k):
            return (1 - x if k & 4 else x, 1 - y if k & 2 else y, 1 - c if k & 1 else c)

        def small_copy(a, k):
            px, py, pc = peer_at(k)
            return pltpu.make_async_remote_copy(
                src_ref=srcs[a].at[4 * px + 2 * py + pc] if by_dest[a] else srcs[a],
                dst_ref=lands[a].at[me],
                send_sem=send_sems.at[a * last + k - 1],
                recv_sem=recv_sems.at[a * last + k - 1],
                device_id=(px, py, pc),
                device_id_type=MESH,
            )

        def small_local(a):
            return pltpu.make_async_copy(srcs[a].at[me] if by_dest[a] else srcs[a], lands[a].at[me], local_sems.at[a + 1])

        def d2d_copy(j):
            return pltpu.make_async_remote_copy(
                src_ref=d2d_buf.at[j],
                dst_ref=d2d_land.at[j],
                send_sem=d2d_send.at[j],
                recv_sem=d2d_recv.at[j],
                device_id=(x, y, 1 - c),
                device_id_type=MESH,
            )

        def ici_copy(j, to):
            return pltpu.make_async_remote_copy(
                src_ref=ici_buf.at[j],
                dst_ref=land_in.at[chip],
                send_sem=ici_send.at[j],
                recv_sem=ici_recv.at[j],
                device_id=to,
                device_id_type=MESH,
            )

        def own_copy():
            return pltpu.make_async_copy(ici_buf.at[0], land_in.at[chip], local_sems.at[0])

        @pl.when(s == 0)
        def _():
            sc_ref[...] = jnp.zeros_like(sc_ref)
            for d in range(N_DEV):
                sc_ref[d, SC_META0 : SC_META0 + N_META, :] = dm_ref[:, d * CHUNK : (d + 1) * CHUNK]
                lane0 = (d * cols) % CHUNK
                rows8 = ac_ref[:, d * cols - lane0 : d * cols - lane0 + CHUNK]
                rows8 = pltpu.roll(rows8, CHUNK - lane0, 1) if lane0 else rows8
                sc_ref[d, SC_CONV0 : SC_CONV0 + 3, 0:cols] = rows8[0:3, 0:cols]
            ga_ref[...] = jnp.zeros_like(ga_ref)
            for row0, acc, n in ((GA_N1, ab2_ref, 8), (GA_RG, ab1_ref, 4), (GA_FG, af2_ref, 8)):
                for j in range(n):
                    ga_ref[row0 + j : row0 + j + 1, :] = acc[0:1, j * CHUNK : (j + 1) * CHUNK]
            ga_ref[GA_LOSS : GA_LOSS + 1, :] = af2_ref[1:2, 0:CHUNK]
            for a in range(3):
                small_local(a).start()
            for k in range(last, 0, -1):
                for a in range(3):
                    small_copy(a, k).start()

        blk = jnp.dot(a_ref[...], b_ref[ROW_TILE:, :], preferred_element_type=F32)
        blk += jnp.dot(am_ref[...], b_ref[LIVE0:ROW_TILE, :], preferred_element_type=F32)

        @pl.when((t & 1) == 1)
        def _():
            d2d_buf[q] = blk.astype(BF16)
            d2d_copy(q).start()

        @pl.when((t & 1) == 0)
        def _():
            d2d_copy(q).wait_recv()
            ici_buf[q] = (blk + d2d_land[q].astype(F32)).astype(BF16)

            @pl.when(t != 0)
            def _():
                ici_copy(q, (jnp.bitwise_xor(x, (t >> 2) & 1), jnp.bitwise_xor(y, (t >> 1) & 1), c)).start()

            @pl.when(t == 0)
            def _():
                own_copy().start()
                for j in range(N_CHIPS):
                    d2d_copy(j).wait_send()
                for j in range(N_CHIPS - 1, 0, -1):
                    ici_copy(j, peer_at(2 * j)).wait()
                for k in range(last, 0, -1):
                    for a in range(3):
                        small_copy(a, k).wait()
                for a in range(3):
                    small_local(a).wait()
                own_copy().wait()

    grid_spec = pltpu.PrefetchScalarGridSpec(
        num_scalar_prefetch=1,
        grid=(N_DEV,),
        in_specs=[_VMEM, _VMEM, pl.BlockSpec((n_rows, BLK), lambda s, me: (0, jnp.bitwise_xor(me[0], last - s))), _HBM]
        + [_VMEM] * 5,
        out_specs=tuple([_HBM] * 4),
        scratch_shapes=[
            pltpu.VMEM((N_CHIPS, D_MODEL, BLK), BF16),
            pltpu.VMEM((N_CHIPS, D_MODEL, BLK), BF16),
            pltpu.VMEM((N_CHIPS, D_MODEL, BLK), BF16),
            pltpu.VMEM((N_DEV, *small_blocks[1]), F32),
            pltpu.VMEM(small_blocks[2], F32),
            pltpu.SemaphoreType.DMA((N_CHIPS,)),
            pltpu.SemaphoreType.DMA((N_CHIPS,)),
            pltpu.SemaphoreType.DMA((N_CHIPS,)),
            pltpu.SemaphoreType.DMA((N_CHIPS,)),
            pltpu.SemaphoreType.DMA((3 * last,)),
            pltpu.SemaphoreType.DMA((3 * last,)),
            pltpu.SemaphoreType.DMA((4,)),
        ],
    )
    return pl.pallas_call(
        body,
        name="gw_in_scatter",
        grid_spec=grid_spec,
        out_shape=(jax.ShapeDtypeStruct((N_CHIPS, D_MODEL, BLK), BF16),)
        + tuple(jax.ShapeDtypeStruct((N_DEV, *b), dt) for b, dt in zip(small_blocks, (gw_out_parts.dtype, F32, F32), strict=True)),
        compiler_params=_params(),
    )(me_arr, hnT_tok, hnT_meta, dproj, gw_out_parts, dmeta_rows, acc_conv, acc_b2, acc_b1, acc_f2)


def _adamw(w, g, m, v):
    m = ADAM_B1 * m + (1.0 - ADAM_B1) * g
    v = ADAM_B2 * v + (1.0 - ADAM_B2) * (g * g)
    m_hat = m / (1.0 - ADAM_B1**ADAM_STEP)
    v_hat = v / (1.0 - ADAM_B2**ADAM_STEP)
    delta = -ADAM_LR * (m_hat / (jnp.sqrt(v_hat) + ADAM_EPS) + ADAM_WD * w)
    return delta, m, v


def _small_leaves(meta, conv_w, n1, rg, fg):
    return meta, conv_w, n1.reshape(8, CHUNK), rg.reshape(4, CHUNK), fg.reshape(8, CHUNK)


def _from_small_leaves(meta, conv_w, n1, rg, fg):
    return meta, conv_w, n1.reshape(D_MODEL), rg.reshape(D_RET), fg.reshape(D_MODEL)


ADAM_GRID = 4


def _adamw_all(land_in, land_out, land_sc, land_ga, big, small):
    n_leaf = 5
    lands = (land_in, land_out)
    n_big = len(big)

    def body(*refs):
        big_in, sc_ref, ga_ref = refs[: 4 * n_big], refs[4 * n_big], refs[4 * n_big + 1]
        small_in = refs[4 * n_big + 2 : 4 * n_big + 2 + 3 * n_leaf]
        big_out = refs[4 * n_big + 2 + 3 * n_leaf : 8 * n_big + 2 + 3 * n_leaf]
        small_out = refs[8 * n_big + 2 + 3 * n_leaf :]
        for b in range(n_big):
            p_ref, w_ref, m_ref, v_ref = big_in[4 * b : 4 * b + 4]
            g_ref, d_ref, nm_ref, nv_ref = big_out[4 * b : 4 * b + 4]
            g = p_ref[0].astype(F32)
            for s in range(1, p_ref.shape[0]):
                g = g + p_ref[s].astype(F32)
            g_ref[...] = g
            d_ref[...], nm_ref[...], nv_ref[...] = _adamw(w_ref[...], g, m_ref[...], v_ref[...])

        @pl.when(pl.program_id(0) == 0)
        def _():
            sc, ga = sc_ref[0], ga_ref[0]
            for s in range(1, N_DEV):
                sc = sc + sc_ref[s]
                ga = ga + ga_ref[s]
            grads = (
                sc[SC_META0 : SC_META0 + N_META],
                sc[SC_CONV0 : SC_CONV0 + 3, 0 : D_CONV // N_DEV],
                ga[GA_N1 : GA_N1 + 8],
                ga[GA_RG : GA_RG + 4],
                ga[GA_FG : GA_FG + 8],
            )
            for leaf, g in enumerate(grads):
                d, nm, nv = _adamw(
                    small_in[leaf][...], g, small_in[n_leaf + leaf][...], small_in[2 * n_leaf + leaf][...]
                )
                small_out[leaf][...] = g
                small_out[n_leaf + leaf][...] = d
                small_out[2 * n_leaf + leaf][...] = nm
                small_out[3 * n_leaf + leaf][...] = nv
            small_out[4 * n_leaf][...] = ga[GA_LOSS : GA_LOSS + 1]

    in_specs, out_specs, out_shape, operands = [], [], [], []
    for land, (w, m, v) in zip(lands, big, strict=True):
        n_r, n_c = w.shape
        tr = n_r // ADAM_GRID
        blk = pl.BlockSpec((tr, n_c), lambda i: (i, 0))
        in_specs += [pl.BlockSpec((land.shape[0], tr, n_c), lambda i: (0, i, 0)), blk, blk, blk]
        out_specs += [blk] * 4
        out_shape += [jax.ShapeDtypeStruct((n_r, n_c), F32)] * 4
        operands += [land, w, m, v]
    small_shapes = [jax.ShapeDtypeStruct(a.shape, F32) for a in small[0]] * 4 + [jax.ShapeDtypeStruct((1, CHUNK), F32)]
    out = pl.pallas_call(
        body,
        name="adamw_all",
        grid=(ADAM_GRID,),
        in_specs=in_specs + [_VMEM] * (2 + 3 * n_leaf),
        out_specs=tuple(out_specs + [_VMEM] * len(small_shapes)),
        out_shape=tuple(out_shape + small_shapes),
        compiler_params=_params(),
    )(*operands, land_sc, land_ga, *small[0], *small[1], *small[2])
    bigs = tuple(out[4 * b : 4 * b + 4] for b in range(n_big))
    rest = out[4 * n_big :]
    return bigs, tuple(rest[k * n_leaf : (k + 1) * n_leaf] for k in range(4)), rest[4 * n_leaf]


def kernel(x, meta, norm1_g, w_in, conv_w, ret_norm_g, w_out, final_g, loss_target, m_meta, m_norm1_g, m_w_in, m_conv_w, m_ret_norm_g, m_w_out, m_final_g, v_meta, v_norm1_g, v_w_in, v_conv_w, v_ret_norm_g, v_w_out, v_final_g):
    seq = x.shape[1]
    assert x.shape == (1, seq, D_MODEL) and seq % ROW_TILE == 0
    n_tiles = seq // ROW_TILE + 1
    x2, t2 = x[0], loss_target[0]

    small_w = _small_leaves(meta, conv_w, norm1_g, ret_norm_g, final_g)
    g1 = norm1_g.reshape(1, D_MODEL)
    hn_tok, hnT_tok, w3, meta_chunk, cw8 = _gather_weights_and_norm(x2, g1, w_in, meta, conv_w)

    rot = _rotary_tables(n_tiles, ROW_TILE)
    dec, _, zeta, xi = _decay_tables()
    gr = ret_norm_g.reshape(1, D_RET)
    gf = final_g.reshape(1, D_MODEL)

    hnT_meta, proj, conv_out, conv, o, states, wo3 = _f1(hn_tok, meta_chunk, g1, w3, cw8, rot, dec, zeta, xi, w_out)
    wout = wo3.reshape(D_MODEL, D_MODEL)
    dh2, gw_out, acc_f2 = _f2(proj, o, conv_out, x2, t2, wout, gr, gf)
    dproj_ret, dco, acc_b1 = _b1(dh2, proj, o, states, wout, gr, _decay_tables(B1_CHUNK), zeta, rot)
    dproj, grad_x2, dmeta_rows, acc_b2, acc_conv = _b2(dproj_ret, dco, proj, conv, w3, cw8, x2, meta_chunk, dh2, g1)
    gw_out_parts = gw_out.reshape(N_DEV, D_MODEL // N_DEV, D_MODEL)

    me_arr = (4 * lax.axis_index("x") + 2 * lax.axis_index("y") + lax.axis_index("c")).astype(jnp.int32).reshape(1)
    land_in, land_out, land_sc, land_ga = _gw_in_scatter(
        hnT_tok, hnT_meta, dproj, gw_out_parts, dmeta_rows, acc_conv, acc_b2, acc_b1, acc_f2, me_arr
    )

    (w_in_out, w_out_out), small_out, loss_row = _adamw_all(
        land_in,
        land_out,
        land_sc,
        land_ga,
        ((w_in, m_w_in, v_w_in), (w_out, m_w_out, v_w_out)),
        (
            small_w,
            _small_leaves(m_meta, m_conv_w, m_norm1_g, m_ret_norm_g, m_final_g),
            _small_leaves(v_meta, v_conv_w, v_norm1_g, v_ret_norm_g, v_final_g),
        ),
    )
    g_w_in, d_w_in, nm_w_in, nv_w_in = w_in_out
    g_w_out, d_w_out, nm_w_out, nv_w_out = w_out_out
    loss = loss_row[0, 0]
    grad_x = grad_x2.reshape(1, seq, D_MODEL)

    def leaves(w_in_leaf, w_out_leaf, small_leaves):
        meta_leaf, conv_leaf, n1_leaf, rg_leaf, fg_leaf = _from_small_leaves(*small_leaves)
        return (meta_leaf, n1_leaf, w_in_leaf, conv_leaf, rg_leaf, w_out_leaf, fg_leaf)

    return (
        loss,
        grad_x,
        *leaves(g_w_in, g_w_out, small_out[0]),
        *leaves(d_w_in, d_w_out, small_out[1]),
        *leaves(nm_w_in, nm_w_out, small_out[2]),
        *leaves(nv_w_in, nv_w_out, small_out[3]),
    )
```

```python
import math

import jax
import jax.numpy as jnp
import numpy as np
from jax import lax
from jax.experimental import pallas as pl
from jax.experimental.pallas import tpu as pltpu

F32 = jnp.float32
BF16 = jnp.bfloat16

N_DEV = 8
N_CHIPS = 4
D_MODEL = 1024
N_META = 16
CHUNK = 128
D_CONV = 512
D_RET = 512
N_HEADS = 4
HEAD_DIM = 128
N_PROJ = 8
BLK = 512
ROPE_BASE = 10000.0
EPS = 1e-6
Q_SCALE = HEAD_DIM ** -0.5
LOG_G = tuple(math.log(1.0 - 2.0 ** (-5.0 - h)) for h in range(N_HEADS))
CHUNK_DECAY = tuple(math.exp(CHUNK * lg) for lg in LOG_G)

ADAM_LR = 0.001
ADAM_B1 = 0.9
ADAM_B2 = 0.999
ADAM_EPS = 1e-08
ADAM_WD = 0.01
ADAM_STEP = 10

ROW_TILE = 512
PAD_ROWS = ROW_TILE - N_META
LIVE0 = ROW_TILE - CHUNK
B1_CHUNK = 256
VMEM_LIMIT = 56 * 1024 * 1024

SC_META0, SC_CONV0 = 0, 16
GA_N1, GA_RG, GA_FG, GA_LOSS = 0, 8, 16, 24

NT = (((1,), (1,)), ((), ()))
TN = (((0,), (0,)), ((), ()))
MESH = pl.DeviceIdType.MESH

_VMEM = pl.BlockSpec(memory_space=pltpu.VMEM)
_HBM = pl.BlockSpec(memory_space=pltpu.HBM)


def _params(n_axes=1):
    return pltpu.CompilerParams(dimension_semantics=("arbitrary",) * n_axes, vmem_limit_bytes=VMEM_LIMIT)


def _sigmoid(x):
    return 0.5 * jnp.tanh(0.5 * x) + 0.5


def _decay_tables(chunk=CHUNK):
    idx = np.arange(chunk, dtype=np.float64)
    diff = idx[:, None] - idx[None, :]
    dec = np.stack([np.where(diff >= 0, np.exp(diff * lg), 0.0) for lg in LOG_G])
    zeta = np.stack([np.exp((chunk - 1 - idx) * lg) for lg in LOG_G])
    xi = np.stack([np.exp((idx + 1.0) * lg) for lg in LOG_G])
    ones = np.ones((1, 1, HEAD_DIM))
    return (
        jnp.asarray(dec, F32),
        jnp.asarray(dec.transpose(0, 2, 1), F32),
        jnp.asarray(zeta[:, :, None] * ones, F32),
        jnp.asarray(xi[:, :, None] * ones, F32),
    )


def _rotary_tables(n_tiles, tm):
    half = HEAD_DIM // 2
    freqs = (1.0 / (np.float32(ROPE_BASE) ** (np.arange(half, dtype=np.float32) / np.float32(half)))).astype(np.float64)
    sign = np.concatenate([-np.ones(half), np.ones(half)])
    two = lambda a: np.concatenate([a, a], axis=1)
    base = two((np.arange(n_tiles, dtype=np.float64) * tm - PAD_ROWS)[:, None] * freqs[None, :])
    off = two(np.arange(tm, dtype=np.float64)[:, None] * freqs[None, :])
    as32 = lambda a: jnp.asarray(a, F32)
    return as32(np.cos(base)), as32(np.sin(base) * sign), as32(np.cos(off)), as32(np.sin(off) * sign)


def _tile_rotary(ca_ref, sa_ref, cb_ref, sb_ref, tile):
    ca, sa = ca_ref[pl.ds(tile, 1), :], sa_ref[pl.ds(tile, 1), :]
    cb, sb = cb_ref[...], sb_ref[...]
    return ca * cb - sa * sb, sa * cb + ca * sb


def _rot(t, cos2, sin2):
    return t * cos2 + pltpu.roll(t, HEAD_DIM // 2, 1) * sin2


def _rot_bwd(d, cos2, sin2):
    return d * cos2 + pltpu.roll(d * sin2, HEAD_DIM // 2, 1)


def _token_tile(i):
    return jnp.maximum(i - 1, 0)


GATHER_PAIRS = 7


def _two_level_gather(src_ref, out_ref, send_sems, recv_sems, local_sem, base=0):
    x, y, c = lax.axis_index("x"), lax.axis_index("y"), lax.axis_index("c")
    me, sibling = (x, y, c), (x, y, 1 - c)
    xnb, ynb, diag = (1 - x, y), (x, 1 - y), (1 - x, 1 - y)
    relayed = (jnp.bitwise_xor(x, 1 - c), jnp.bitwise_xor(y, c))
    other = (jnp.bitwise_xor(x, c), jnp.bitwise_xor(y, 1 - c))

    def copy(k, block, to, src=None):
        dst = out_ref.at[4 * block[0] + 2 * block[1] + block[2]]
        return pltpu.make_async_remote_copy(
            src_ref=dst if src is None else src,
            dst_ref=dst,
            send_sem=send_sems.at[base + k],
            recv_sem=recv_sems.at[base + k],
            device_id=to,
            device_id_type=MESH,
        )

    mine = pltpu.make_async_copy(src_ref, out_ref.at[4 * x + 2 * y + c], local_sem)
    first = [copy(1, me, (*xnb, c), src=src_ref), copy(2, me, (*ynb, c), src=src_ref), copy(0, me, sibling, src=src_ref)]
    relay = copy(3, (*relayed, c), (*other, c))
    passed = [copy(4 + j, (*chip, c), sibling) for j, chip in enumerate((xnb, ynb, diag))]

    def start():
        mine.start()
        for cp in first:
            cp.start()

    def forward():
        copy(1 + c, (*relayed, c), me).wait_recv()
        relay.start()
        copy(2 - c, (*other, c), me).wait_recv()
        passed[0].start()
        passed[1].start()

    def forward_relayed():
        copy(3, (*diag, c), me).wait_recv()
        passed[2].start()

    def finish():
        copy(0, sibling, me).wait_recv()
        for j, chip in enumerate((xnb, ynb, diag)):
            copy(4 + j, (*chip, 1 - c), me).wait_recv()
        for cp in first + [relay] + passed:
            cp.wait_send()
        mine.wait()

    return start, forward, forward_relayed, finish


W_IN_CHUNKS = 4


def _gather_weights_and_norm(x2, g1, w_shard, meta, conv_w):
    per = GATHER_PAIRS
    n_ch = W_IN_CHUNKS
    rows = w_shard.shape[0] // n_ch
    tm = ROW_TILE
    nt = x2.shape[0] // tm
    cols = D_CONV // N_DEV
    pack_shape = (SC_CONV0 + 8, CHUNK)
    starts = [0] * (n_ch + 1)
    forwards = [min(5 + 2 * p, nt - 1) for p in range(n_ch)] + [min(12, nt - 1)]
    relayeds = [min(12 + p, nt - 1) for p in range(n_ch)] + [nt - 1]
    finishes = [nt - 1] * (n_ch + 1)

    def body(x_ref, g_ref, w_ref, meta_ref, conv_ref, hn_ref, hnT_ref, w_all_ref, mc_ref, cw_ref,
             w_bf, small_ref, small_all_ref, send_sems, recv_sems, local_sems):
        i = pl.program_id(0)

        @pl.when(i == 0)
        def _():
            w_bf[...] = w_ref[...].astype(BF16)
            small_ref[...] = jnp.zeros_like(small_ref)
            small_ref[SC_META0 : SC_META0 + N_META, :] = meta_ref[...]
            small_ref[SC_CONV0 : SC_CONV0 + 3, 0:cols] = conv_ref[...]

        parts = [
            _two_level_gather(
                w_bf.at[pl.ds(ch * rows, rows)],
                w_all_ref.at[:, pl.ds(ch * rows, rows)],
                send_sems,
                recv_sems,
                local_sems.at[ch],
                ch * per,
            )
            for ch in range(n_ch)
        ]
        parts.append(_two_level_gather(small_ref, small_all_ref, send_sems, recv_sems, local_sems.at[n_ch], n_ch * per))
        for phase, steps in enumerate((starts, forwards, relayeds, finishes)):
            for part, step in zip(parts, steps, strict=True):
                pl.when(i == step)(part[phase])

        h = x_ref[...]
        r = lax.rsqrt(jnp.mean(h * h, axis=-1, keepdims=True) + EPS)
        hn = (h * r * g_ref[...]).astype(BF16)
        hn_ref[...] = hn
        hnT_ref[...] = hn.T

        @pl.when(i == nt - 1)
        def _():
            mc_ref[0 : CHUNK - N_META, :] = jnp.zeros((CHUNK - N_META, D_MODEL), F32)
            cw_ref[...] = jnp.zeros_like(cw_ref)
            for d in range(N_DEV):
                mc_ref[CHUNK - N_META : CHUNK, d * CHUNK : (d + 1) * CHUNK] = small_all_ref[d, SC_META0 : SC_META0 + N_META, :]
                lane0 = (d * cols) % CHUNK
                rows8 = small_all_ref[d, SC_CONV0 : SC_CONV0 + 8, :]
                rows8 = pltpu.roll(rows8, lane0, 1) if lane0 else rows8
                cw_ref[0:3, d * cols : (d + 1) * cols] = rows8[0:3, lane0 : lane0 + cols]

    const = lambda r, c: pl.BlockSpec((r, c), lambda i: (0, 0))
    return pl.pallas_call(
        body,
        name="gather_weights_norm",
        grid=(nt,),
        out_shape=(
            jax.ShapeDtypeStruct(x2.shape, BF16),
            jax.ShapeDtypeStruct(x2.shape[::-1], BF16),
            jax.ShapeDtypeStruct((N_DEV, *w_shard.shape), BF16),
            jax.ShapeDtypeStruct((CHUNK, D_MODEL), F32),
            jax.ShapeDtypeStruct((8, D_CONV), F32),
        ),
        in_specs=[pl.BlockSpec((tm, D_MODEL), lambda i: (i, 0)), _VMEM, _VMEM, _VMEM, _VMEM],
        out_specs=(
            pl.BlockSpec((tm, D_MODEL), lambda i: (i, 0)),
            pl.BlockSpec((D_MODEL, tm), lambda i: (0, i)),
            _HBM,
            const(CHUNK, D_MODEL),
            const(8, D_CONV),
        ),
        scratch_shapes=[
            pltpu.VMEM(w_shard.shape, BF16),
            pltpu.VMEM(pack_shape, F32),
            pltpu.VMEM((N_DEV, *pack_shape), F32),
            pltpu.SemaphoreType.DMA(((n_ch + 1) * per,)),
            pltpu.SemaphoreType.DMA(((n_ch + 1) * per,)),
            pltpu.SemaphoreType.DMA((n_ch + 1,)),
        ],
        compiler_params=_params(),
    )(x2, g1, w_shard, meta, conv_w)


def _f1(hn_tok, meta_chunk, g1, w3, cw8, rot, dec, zeta, xi, wo_shard):
    tm = ROW_TILE
    nt = hn_tok.shape[0] // tm + 1
    n_rows = nt * tm
    nct = tm // CHUNK

    def body(hn_ref, mc_ref, g_ref, w_ref, cw_ref, ca_ref, sa_ref, cb_ref, sb_ref, dec_ref, zeta_ref, xi_ref, wo_ref,
             hnT_ref, pr_ref, co_ref, cv_ref, o_ref, st_ref, wo_all_ref,
             halo, state, wo_bf, send_sems, recv_sems, local_sem):
        i = pl.program_id(0)

        @pl.when(i == 0)
        def _():
            wo_bf[...] = wo_ref[...].astype(BF16)

        wo_phases = _two_level_gather(wo_bf, wo_all_ref, send_sems, recv_sems, local_sem)
        for step, phase in zip((0, nt // 3, 2 * nt // 3, nt - 1), wo_phases, strict=True):
            pl.when(i == step)(phase)

        def work(hn, r0):
            rs = slice(r0, tm)
            n = tm - r0

            def proj(j):
                return jnp.dot(hn, w_ref[j], preferred_element_type=F32)

            cos_all, sin_all = _tile_rotary(ca_ref, sa_ref, cb_ref, sb_ref, i)
            cos_t, sin_t = cos_all[rs, :], sin_all[rs, :]
            q, k, v = proj(4), proj(5), proj(6).astype(BF16)
            pr_ref[rs, 6 * BLK : 7 * BLK] = v
            chunks = range(r0 // CHUNK, tm // CHUNK)
            heads = []
            for hd in range(N_HEADS):
                cs = slice(hd * HEAD_DIM, (hd + 1) * HEAD_DIM)
                qh = (_rot(q[:, cs], cos_t, sin_t) * Q_SCALE).astype(BF16)
                kh = _rot(k[:, cs], cos_t, sin_t).astype(BF16)
                vh = v[:, cs]
                pr_ref[rs, 4 * BLK + hd * HEAD_DIM : 4 * BLK + (hd + 1) * HEAD_DIM] = qh
                pr_ref[rs, 5 * BLK + hd * HEAD_DIM : 5 * BLK + (hd + 1) * HEAD_DIM] = kh
                upd = {}
                for c in chunks:
                    cr = slice(c * CHUNK - r0, (c + 1) * CHUNK - r0)
                    kz = (kh[cr].astype(F32) * zeta_ref[hd]).astype(BF16)
                    upd[c] = lax.dot_general(kz, vh[cr], TN, preferred_element_type=F32)
                st = state[hd]
                outs = []
                for c in chunks:
                    cr = slice(c * CHUNK - r0, (c + 1) * CHUNK - r0)
                    qc, kc, vc = qh[cr], kh[cr], vh[cr]
                    st_bf = st.astype(BF16)
                    st_ref[c, hd] = st_bf
                    s = lax.dot_general(qc, kc, NT, preferred_element_type=F32) * dec_ref[hd]
                    inner = jnp.dot(s.astype(BF16), vc, preferred_element_type=F32)
                    qx = (qc.astype(F32) * xi_ref[hd]).astype(BF16)
                    outs.append(inner + jnp.dot(qx, st_bf, preferred_element_type=F32))
                    st = CHUNK_DECAY[hd] * st + upd[c]
                state[hd] = st
                heads.append(jnp.concatenate(outs, axis=0) if len(outs) > 1 else outs[0])
            o_ref[rs, :] = jnp.concatenate(heads, axis=1)

            cx, cb, cc, cg = proj(0), proj(1), proj(2), proj(3)
            u = cc * cx
            rows = lax.broadcasted_iota(jnp.int32, u.shape, 0)
            hl = halo[...]
            u1 = jnp.where(rows == 0, hl[7:8], pltpu.roll(u, 1, 0))
            u2 = jnp.where(rows == 0, hl[6:7], jnp.where(rows == 1, hl[7:8], pltpu.roll(u, 2, 0)))
            halo[...] = u[n - 8 : n]
            cw = cw_ref[...]
            conv = cw[0:1] * u2 + cw[1:2] * u1 + cw[2:3] * u
            co = (cb * conv * (cg * _sigmoid(cg))).astype(BF16)
            co_ref[rs, :] = co
            cv_ref[rs, :] = conv.astype(BF16)
            pr_ref[rs, 0 * BLK : 1 * BLK] = cx.astype(BF16)
            pr_ref[rs, 1 * BLK : 2 * BLK] = cb.astype(BF16)
            pr_ref[rs, 2 * BLK : 3 * BLK] = cc.astype(BF16)
            pr_ref[rs, 3 * BLK : 4 * BLK] = cg.astype(BF16)
            pr_ref[rs, 7 * BLK : 8 * BLK] = proj(7).astype(BF16)

        @pl.when(i == 0)
        def _():
            halo[...] = jnp.zeros_like(halo)
            state[...] = jnp.zeros_like(state)
            h = mc_ref[...]
            r = lax.rsqrt(jnp.mean(h * h, axis=-1, keepdims=True) + EPS)
            hn = (h * r * g_ref[...]).astype(BF16)
            hnT_ref[...] = hn.T
            work(hn, LIVE0)

        @pl.when(i > 0)
        def _():
            work(hn_ref[...], 0)

    row = lambda w: pl.BlockSpec((tm, w), lambda i: (i, 0))
    return pl.pallas_call(
        body,
        name="f1_inproj_conv",
        grid=(nt,),
        in_specs=[pl.BlockSpec((tm, D_MODEL), lambda i: (_token_tile(i), 0))] + [_VMEM] * 12,
        out_specs=(
            pl.BlockSpec((D_MODEL, CHUNK), lambda i: (0, 0)),
            row(N_PROJ * BLK),
            row(D_CONV),
            row(D_CONV),
            row(D_RET),
            pl.BlockSpec((nct, N_HEADS, HEAD_DIM, HEAD_DIM), lambda i: (i, 0, 0, 0)),
            _HBM,
        ),
        out_shape=(
            jax.ShapeDtypeStruct((D_MODEL, CHUNK), BF16),
            jax.ShapeDtypeStruct((n_rows, N_PROJ * BLK), BF16),
            jax.ShapeDtypeStruct((n_rows, D_CONV), BF16),
            jax.ShapeDtypeStruct((n_rows, D_CONV), BF16),
            jax.ShapeDtypeStruct((n_rows, D_RET), F32),
            jax.ShapeDtypeStruct((n_rows // CHUNK, N_HEADS, HEAD_DIM, HEAD_DIM), BF16),
            jax.ShapeDtypeStruct((N_DEV, *wo_shard.shape), BF16),
        ),
        scratch_shapes=[
            pltpu.VMEM((8, D_CONV), F32),
            pltpu.VMEM((N_HEADS, HEAD_DIM, HEAD_DIM), F32),
            pltpu.VMEM(wo_shard.shape, BF16),
            pltpu.SemaphoreType.DMA((GATHER_PAIRS,)),
            pltpu.SemaphoreType.DMA((GATHER_PAIRS,)),
            pltpu.SemaphoreType.DMA(()),
        ],
        compiler_params=_params(),
    )(hn_tok, meta_chunk, g1, w3, cw8, *rot, dec, zeta, xi, wo_shard)


def _group_norm(o):
    ys, rs = [], []
    for hd in range(N_HEADS):
        oh = o[:, hd * HEAD_DIM : (hd + 1) * HEAD_DIM]
        xc = oh - jnp.mean(oh, axis=-1, keepdims=True)
        rstd = lax.rsqrt(jnp.mean(xc * xc, axis=-1, keepdims=True) + EPS)
        ys.append(xc * rstd)
        rs.append(jnp.broadcast_to(rstd, oh.shape))
    return jnp.concatenate(ys, axis=1), jnp.concatenate(rs, axis=1)


def _f2(proj, o, conv_out, x2, t2, wout, gr, gf):
    n_rows = proj.shape[0]
    tm = ROW_TILE
    nt = n_rows // tm

    def body(o_ref, rg_ref, co_ref, x_ref, t_ref, wo_ref, gr_ref, gf_ref, dh2_ref, gw_ref, acc_ref, gacc):
        i = pl.program_id(0)

        @pl.when(i == 0)
        def _():
            acc_ref[...] = jnp.zeros_like(acc_ref)
            gacc[...] = jnp.zeros_like(gacc)
            dh2_ref[...] = jnp.zeros_like(dh2_ref)

        @pl.when(i > 0)
        def _():
            yh, _ = _group_norm(o_ref[...])
            rg = rg_ref[...].astype(F32)
            ro = (yh * gr_ref[...] * (rg * _sigmoid(rg))).astype(BF16)
            h2 = (
                x_ref[...]
                + jnp.dot(co_ref[...], wo_ref[0:D_CONV], preferred_element_type=F32)
                + jnp.dot(ro, wo_ref[D_CONV:], preferred_element_type=F32)
            )
            r2 = lax.rsqrt(jnp.mean(h2 * h2, axis=-1, keepdims=True) + EPS)
            yn = h2 * r2
            gfv = gf_ref[...]
            err = yn * gfv - t_ref[...]
            tile_loss = jnp.sum(jnp.sum(err * err, axis=-1, keepdims=True), axis=0, keepdims=True) * (0.5 / D_MODEL)
            acc_ref[0:1, :] += jnp.sum(err * yn, axis=0, keepdims=True) * (1.0 / D_MODEL)
            acc_ref[1:2, :] += tile_loss
            dyn = err * (gfv * (1.0 / D_MODEL))
            dh2 = r2 * (dyn - yn * jnp.mean(dyn * yn, axis=-1, keepdims=True))
            dh2_ref[...] = dh2
            dh2_bf = dh2.astype(BF16)
            gacc[0:D_CONV, :] += lax.dot_general(co_ref[...], dh2_bf, TN, preferred_element_type=F32)
            gacc[D_CONV:, :] += lax.dot_general(ro, dh2_bf, TN, preferred_element_type=F32)

        @pl.when(i == nt - 1)
        def _():
            gw_ref[...] = gacc[...].astype(BF16)

    row = lambda w, j=0: pl.BlockSpec((tm, w), lambda i: (i, j))
    tok = pl.BlockSpec((tm, D_MODEL), lambda i: (_token_tile(i), 0))
    return pl.pallas_call(
        body,
        name="f2_out_loss",
        grid=(nt,),
        in_specs=[row(D_RET), row(BLK, 7), row(D_CONV), tok, tok] + [_VMEM] * 3,
        out_specs=(
            row(D_MODEL),
            pl.BlockSpec((D_MODEL, D_MODEL), lambda i: (0, 0)),
            pl.BlockSpec((8, D_MODEL), lambda i: (0, 0)),
        ),
        out_shape=(
            jax.ShapeDtypeStruct((n_rows, D_MODEL), F32),
            jax.ShapeDtypeStruct((D_MODEL, D_MODEL), BF16),
            jax.ShapeDtypeStruct((8, D_MODEL), F32),
        ),
        scratch_shapes=[pltpu.VMEM((D_MODEL, D_MODEL), F32)],
        compiler_params=_params(),
    )(o, proj, conv_out, x2, t2, wout, gr, gf)


def _b1(dh2, proj, o, states, wout, gr, tables, zeta0, rot):
    n_rows = dh2.shape[0]
    tm = ROW_TILE
    nt = n_rows // tm
    chunk = B1_CHUNK
    nct = tm // chunk
    decay = tuple(math.exp(chunk * lg) for lg in LOG_G)

    def body(dh2_ref, pr_ref, o_ref, st_ref, wo_ref, gr_ref, dec_ref, dect_ref, zeta_ref, xi_ref, zeta0_ref,
             ca_ref, sa_ref, cb_ref, sb_ref, dp_ref, dco_ref, acc_ref, dstate):
        i = pl.program_id(0)
        tile = nt - 1 - i
        cos_all, sin_all = _tile_rotary(ca_ref, sa_ref, cb_ref, sb_ref, tile)

        @pl.when(i == 0)
        def _():
            dstate[...] = jnp.zeros_like(dstate)
            acc_ref[...] = jnp.zeros_like(acc_ref)

        @pl.when(tile > 0)
        def _():
            dmix = lax.dot_general(dh2_ref[...].astype(BF16), wo_ref[...], NT, preferred_element_type=F32)
            dco_ref[...] = dmix[:, :D_CONV].astype(BF16)
            dro = dmix[:, D_CONV:]
            rg = pr_ref[:, 3 * BLK : 4 * BLK].astype(F32)
            sg = _sigmoid(rg)
            silu, dsilu = rg * sg, sg * (1.0 + rg * (1.0 - sg))
            yh, rstd = _group_norm(o_ref[...])
            grv = gr_ref[...]
            dp_ref[:, 3 * BLK : 4 * BLK] = (dro * (yh * grv) * dsilu).astype(BF16)
            dret = dro * silu
            acc_ref[0:1, :] += jnp.sum(dret * yh, axis=0, keepdims=True)
            dyh = dret * grv
            for hd in range(N_HEADS):
                cs = slice(hd * HEAD_DIM, (hd + 1) * HEAD_DIM)
                a, b = dyh[:, cs], yh[:, cs]
                do_head = (
                    rstd[:, cs]
                    * (a - jnp.mean(a, axis=-1, keepdims=True) - b * jnp.mean(a * b, axis=-1, keepdims=True))
                ).astype(BF16)
                dupd = {}
                for c in range(nct):
                    rs = slice(c * chunk, (c + 1) * chunk)
                    qx = (pr_ref[rs, cs].astype(F32) * xi_ref[hd]).astype(BF16)
                    dupd[c] = lax.dot_general(qx, do_head[rs, :], TN, preferred_element_type=F32)
                dst = dstate[hd]
                for c in reversed(range(nct)):
                    rs = slice(c * chunk, (c + 1) * chunk)
                    cos_t, sin_t = cos_all[rs, :], sin_all[rs, :]
                    q = pr_ref[rs, hd * HEAD_DIM : (hd + 1) * HEAD_DIM]
                    k = pr_ref[rs, BLK + hd * HEAD_DIM : BLK + (hd + 1) * HEAD_DIM]
                    v = pr_ref[rs, 2 * BLK + hd * HEAD_DIM : 2 * BLK + (hd + 1) * HEAD_DIM]
                    do = do_head[rs, :]
                    st_bf = st_ref[c * (chunk // CHUNK), hd]
                    dst_bf = dst.astype(BF16)
                    zt, xt = zeta_ref[hd], xi_ref[hd]
                    sT = (lax.dot_general(k, q, NT, preferred_element_type=F32) * dect_ref[hd]).astype(BF16)
                    dsT = (lax.dot_general(v, do, NT, preferred_element_type=F32) * dect_ref[hd]).astype(BF16)
                    ds = (lax.dot_general(do, v, NT, preferred_element_type=F32) * dec_ref[hd]).astype(BF16)
                    kz = (k.astype(F32) * zt).astype(BF16)
                    dv = jnp.dot(sT, do, preferred_element_type=F32) + jnp.dot(kz, dst_bf, preferred_element_type=F32)
                    dq = jnp.dot(ds, k, preferred_element_type=F32) + xt * lax.dot_general(
                        do, st_bf, NT, preferred_element_type=F32
                    )
                    dk = jnp.dot(dsT, q, preferred_element_type=F32) + zt * lax.dot_general(
                        v, dst_bf, NT, preferred_element_type=F32
                    )
                    dst = decay[hd] * dst + dupd[c]
                    dp_ref[rs, hd * HEAD_DIM : (hd + 1) * HEAD_DIM] = _rot_bwd(dq * Q_SCALE, cos_t, sin_t).astype(BF16)
                    dp_ref[rs, BLK + hd * HEAD_DIM : BLK + (hd + 1) * HEAD_DIM] = _rot_bwd(dk, cos_t, sin_t).astype(BF16)
                    dp_ref[rs, 2 * BLK + hd * HEAD_DIM : 2 * BLK + (hd + 1) * HEAD_DIM] = dv.astype(BF16)
                dstate[hd] = dst

        @pl.when(tile == 0)
        def _():
            rs = slice(LIVE0, tm)
            cos_t, sin_t = cos_all[rs, :], sin_all[rs, :]
            zeros = jnp.zeros((tm - LIVE0, HEAD_DIM), BF16)
            for hd in range(N_HEADS):
                k = pr_ref[rs, BLK + hd * HEAD_DIM : BLK + (hd + 1) * HEAD_DIM]
                v = pr_ref[rs, 2 * BLK + hd * HEAD_DIM : 2 * BLK + (hd + 1) * HEAD_DIM]
                dst_bf = dstate[hd].astype(BF16)
                zt = zeta0_ref[hd]
                kz = (k.astype(F32) * zt).astype(BF16)
                dv = jnp.dot(kz, dst_bf, preferred_element_type=F32)
                dk = zt * lax.dot_general(v, dst_bf, NT, preferred_element_type=F32)
                dp_ref[rs, hd * HEAD_DIM : (hd + 1) * HEAD_DIM] = zeros
                dp_ref[rs, BLK + hd * HEAD_DIM : BLK + (hd + 1) * HEAD_DIM] = _rot_bwd(dk, cos_t, sin_t).astype(BF16)
                dp_ref[rs, 2 * BLK + hd * HEAD_DIM : 2 * BLK + (hd + 1) * HEAD_DIM] = dv.astype(BF16)
                dp_ref[rs, 3 * BLK + hd * HEAD_DIM : 3 * BLK + (hd + 1) * HEAD_DIM] = zeros
            dco_ref[rs, :] = jnp.zeros((tm - LIVE0, D_CONV), BF16)

    half = N_PROJ * BLK // 2
    rev = lambda w, j=0: pl.BlockSpec((tm, w), lambda i: (nt - 1 - i, j))
    return pl.pallas_call(
        body,
        name="b1_dret",
        grid=(nt,),
        in_specs=[
            rev(D_MODEL),
            rev(half, 1),
            rev(D_RET),
            pl.BlockSpec((tm // CHUNK, N_HEADS, HEAD_DIM, HEAD_DIM), lambda i: (nt - 1 - i, 0, 0, 0)),
        ]
        + [_VMEM] * 11,
        out_specs=(rev(half, 1), rev(D_CONV), pl.BlockSpec((8, D_RET), lambda i: (0, 0))),
        out_shape=(
            jax.ShapeDtypeStruct((n_rows, N_PROJ * BLK), BF16),
            jax.ShapeDtypeStruct((n_rows, D_CONV), BF16),
            jax.ShapeDtypeStruct((8, D_RET), F32),
        ),
        scratch_shapes=[pltpu.VMEM((N_HEADS, HEAD_DIM, HEAD_DIM), F32)],
        compiler_params=_params(),
    )(dh2, proj, o, states, wout, gr, *tables, zeta0, *rot)


def _b2(dproj, dco, proj, conv, w3, cw8, x2, meta_chunk, dh2, g1):
    n_rows = dproj.shape[0]
    tm = ROW_TILE
    nt = n_rows // tm
    half = N_PROJ * BLK // 2

    def body(dpr_ref, dco_ref, pr_ref, cv_ref, w_ref, cw_ref, x_ref, mc_ref, dh2_ref, g_ref,
             dpc_ref, gx_ref, dm_ref, acc_ref, accc_ref, halo):
        i = pl.program_id(0)
        tile = nt - 1 - i

        @pl.when(i == 0)
        def _():
            acc_ref[...] = jnp.zeros_like(acc_ref)
            accc_ref[...] = jnp.zeros_like(accc_ref)
            halo[...] = jnp.zeros_like(halo)

        def work(h, r0):
            rs = slice(r0, tm)
            n = tm - r0
            dco = dco_ref[rs, :].astype(F32)
            cx = pr_ref[rs, 0 * BLK : 1 * BLK].astype(F32)
            cb = pr_ref[rs, 1 * BLK : 2 * BLK].astype(F32)
            cc = pr_ref[rs, 2 * BLK : 3 * BLK].astype(F32)
            cg = pr_ref[rs, 3 * BLK : 4 * BLK].astype(F32)
            conv = cv_ref[rs, :].astype(F32)
            sg = _sigmoid(cg)
            silu, dsilu = cg * sg, sg * (1.0 + cg * (1.0 - sg))
            t = dco * cb
            dcb = (dco * conv * silu).astype(BF16)
            dcg = (t * conv * dsilu).astype(BF16)
            dconv = t * silu
            rows = lax.broadcasted_iota(jnp.int32, dconv.shape, 0)
            hl = halo[...]
            dc1 = jnp.where(rows == n - 1, hl[0:1], pltpu.roll(dconv, n - 1, 0))
            dc2 = jnp.where(rows == n - 2, hl[0:1], jnp.where(rows == n - 1, hl[1:2], pltpu.roll(dconv, n - 2, 0)))
            halo[...] = dconv[0:8]
            cw = cw_ref[...]
            du = cw[2:3] * dconv + cw[1:2] * dc1 + cw[0:1] * dc2
            u = cc * cx
            accc_ref[0:1, :] += jnp.sum(u * dc2, axis=0, keepdims=True)
            accc_ref[1:2, :] += jnp.sum(u * dc1, axis=0, keepdims=True)
            accc_ref[2:3, :] += jnp.sum(u * dconv, axis=0, keepdims=True)
            dcx = (du * cc).astype(BF16)
            dcc = (du * cx).astype(BF16)
            dhn = lax.dot_general(dpr_ref[rs, 0:BLK], w_ref[4], NT, preferred_element_type=F32)
            for j in range(1, 4):
                dhn += lax.dot_general(dpr_ref[rs, j * BLK : (j + 1) * BLK], w_ref[4 + j], NT, preferred_element_type=F32)
            for blk, d in ((1, dcb), (3, dcg), (0, dcx), (2, dcc)):
                dpc_ref[rs, blk * BLK : (blk + 1) * BLK] = d
                dhn += lax.dot_general(d, w_ref[blk], NT, preferred_element_type=F32)
            r = lax.rsqrt(jnp.mean(h * h, axis=-1, keepdims=True) + EPS)
            hh = h * r
            acc_ref[0:1, :] += jnp.sum(dhn * hh, axis=0, keepdims=True)
            dg = dhn * g_ref[...]
            return dh2_ref[rs, :] + r * (dg - hh * jnp.mean(dg * hh, axis=-1, keepdims=True))

        @pl.when(tile == 0)
        def _():
            dh = work(mc_ref[...], LIVE0)
            dm_ref[...] = dh[CHUNK - N_META : CHUNK]

        @pl.when(tile > 0)
        def _():
            gx_ref[...] = work(x_ref[...], 0)

    rev = lambda w, j=0: pl.BlockSpec((tm, w), lambda i: (nt - 1 - i, j))
    tok = pl.BlockSpec((tm, D_MODEL), lambda i: (_token_tile(nt - 1 - i), 0))
    const = lambda r, c: pl.BlockSpec((r, c), lambda i: (0, 0))
    return pl.pallas_call(
        body,
        name="b2_dconv_dh",
        grid=(nt,),
        in_specs=[rev(half, 1), rev(D_CONV), rev(half, 0), rev(D_CONV), _VMEM, _VMEM, tok, _VMEM, rev(D_MODEL), _VMEM],
        out_specs=(rev(half, 0), tok, const(N_META, D_MODEL), const(8, D_MODEL), const(8, D_CONV)),
        out_shape=(
            jax.ShapeDtypeStruct(dproj.shape, BF16),
            jax.ShapeDtypeStruct(x2.shape, F32),
            jax.ShapeDtypeStruct((N_META, D_MODEL), F32),
            jax.ShapeDtypeStruct((8, D_MODEL), F32),
            jax.ShapeDtypeStruct((8, D_CONV), F32),
        ),
        scratch_shapes=[pltpu.VMEM((8, D_CONV), F32)],
        input_output_aliases={0: 0},
        compiler_params=_params(),
    )(dproj, dco, proj, conv, w3, cw8, x2, meta_chunk, dh2, g1)


def _gw_in_scatter(hnT_tok, hnT_meta, dproj, gw_out_parts, dmeta_rows, acc_conv, acc_b2, acc_b1, acc_f2, me_arr):
    n_rows = dproj.shape[0]
    last = N_DEV - 1
    cols = D_CONV // N_DEV
    by_dest = (True, True, False)
    small_blocks = (gw_out_parts.shape[1:], (SC_CONV0 + 8, CHUNK), (GA_LOSS + 8, CHUNK))

    def body(me_ref, a_ref, am_ref, b_ref, go_ref, dm_ref, ac_ref, ab2_ref, ab1_ref, af2_ref,
             land_in, land_go, land_sc, land_ga, d2d_buf, d2d_land, ici_buf, sc_ref, ga_ref,
             d2d_send, d2d_recv, ici_send, ici_recv, send_sems, recv_sems, local_sems):
        del me_ref
        s = pl.program_id(0)
        t = last - s
        q = t >> 1
        x, y, c = lax.axis_index("x"), lax.axis_index("y"), lax.axis_index("c")
        me = 4 * x + 2 * y + c
        chip = 2 * x + y
        srcs, lands = (go_ref, sc_ref, ga_ref), (land_go, land_sc, land_ga)

        def peer_at(k):
            return (1 - x if k & 4 else x, 1 - y if k & 2 else y, 1 - c if k & 1 else c)

        def small_copy(a, k):
            px, py, pc = peer_at(k)
            return pltpu.make_async_remote_copy(
                src_ref=srcs[a].at[4 * px + 2 * py + pc] if by_dest[a] else srcs[a],
                dst_ref=lands[a].at[me],
                send_sem=send_sems.at[a * last + k - 1],
                recv_sem=recv_sems.at[a * last + k - 1],
                device_id=(px, py, pc),
                device_id_type=MESH,
            )

        def small_local(a):
            return pltpu.make_async_copy(srcs[a].at[me] if by_dest[a] else srcs[a], lands[a].at[me], local_sems.at[a + 1])

        def d2d_copy(j):
            return pltpu.make_async_remote_copy(
                src_ref=d2d_buf.at[j],
                dst_ref=d2d_land.at[j],
                send_sem=d2d_send.at[j],
                recv_sem=d2d_recv.at[j],
                device_id=(x, y, 1 - c),
                device_id_type=MESH,
            )

        def ici_copy(j, to):
            return pltpu.make_async_remote_copy(
                src_ref=ici_buf.at[j],
                dst_ref=land_in.at[chip],
                send_sem=ici_send.at[j],
                recv_sem=ici_recv.at[j],
                device_id=to,
                device_id_type=MESH,
            )

        def own_copy():
            return pltpu.make_async_copy(ici_buf.at[0], land_in.at[chip], local_sems.at[0])

        @pl.when(s == 0)
        def _():
            sc_ref[...] = jnp.zeros_like(sc_ref)
            for d in range(N_DEV):
                sc_ref[d, SC_META0 : SC_META0 + N_META, :] = dm_ref[:, d * CHUNK : (d + 1) * CHUNK]
                lane0 = (d * cols) % CHUNK
                rows8 = ac_ref[:, d * cols - lane0 : d * cols - lane0 + CHUNK]
                rows8 = pltpu.roll(rows8, CHUNK - lane0, 1) if lane0 else rows8
                sc_ref[d, SC_CONV0 : SC_CONV0 + 3, 0:cols] = rows8[0:3, 0:cols]
            ga_ref[...] = jnp.zeros_like(ga_ref)
            for row0, acc, n in ((GA_N1, ab2_ref, 8), (GA_RG, ab1_ref, 4), (GA_FG, af2_ref, 8)):
                for j in range(n):
                    ga_ref[row0 + j : row0 + j + 1, :] = acc[0:1, j * CHUNK : (j + 1) * CHUNK]
            ga_ref[GA_LOSS : GA_LOSS + 1, :] = af2_ref[1:2, 0:CHUNK]
            for a in range(3):
                small_local(a).start()
            for k in range(last, 0, -1):
                for a in range(3):
                    small_copy(a, k).start()

        blk = jnp.dot(a_ref[...], b_ref[ROW_TILE:, :], preferred_element_type=F32)
        blk += jnp.dot(am_ref[...], b_ref[LIVE0:ROW_TILE, :], preferred_element_type=F32)

        @pl.when((t & 1) == 1)
        def _():
            d2d_buf[q] = blk.astype(BF16)
            d2d_copy(q).start()

        @pl.when((t & 1) == 0)
        def _():
            d2d_copy(q).wait_recv()
            ici_buf[q] = (blk + d2d_land[q].astype(F32)).astype(BF16)

            @pl.when(t != 0)
            def _():
                ici_copy(q, (jnp.bitwise_xor(x, (t >> 2) & 1), jnp.bitwise_xor(y, (t >> 1) & 1), c)).start()

            @pl.when(t == 0)
            def _():
                own_copy().start()
                for j in range(N_CHIPS):
                    d2d_copy(j).wait_send()
                for j in range(N_CHIPS - 1, 0, -1):
                    ici_copy(j, peer_at(2 * j)).wait()
                for k in range(last, 0, -1):
                    for a in range(3):
                        small_copy(a, k).wait()
                for a in range(3):
                    small_local(a).wait()
                own_copy().wait()

    grid_spec = pltpu.PrefetchScalarGridSpec(
        num_scalar_prefetch=1,
        grid=(N_DEV,),
        in_specs=[_VMEM, _VMEM, pl.BlockSpec((n_rows, BLK), lambda s, me: (0, jnp.bitwise_xor(me[0], last - s))), _HBM]
        + [_VMEM] * 5,
        out_specs=tuple([_HBM] * 4),
        scratch_shapes=[
            pltpu.VMEM((N_CHIPS, D_MODEL, BLK), BF16),
            pltpu.VMEM((N_CHIPS, D_MODEL, BLK), BF16),
            pltpu.VMEM((N_CHIPS, D_MODEL, BLK), BF16),
            pltpu.VMEM((N_DEV, *small_blocks[1]), F32),
            pltpu.VMEM(small_blocks[2], F32),
            pltpu.SemaphoreType.DMA((N_CHIPS,)),
            pltpu.SemaphoreType.DMA((N_CHIPS,)),
            pltpu.SemaphoreType.DMA((N_CHIPS,)),
            pltpu.SemaphoreType.DMA((N_CHIPS,)),
            pltpu.SemaphoreType.DMA((3 * last,)),
            pltpu.SemaphoreType.DMA((3 * last,)),
            pltpu.SemaphoreType.DMA((4,)),
        ],
    )
    return pl.pallas_call(
        body,
        name="gw_in_scatter",
        grid_spec=grid_spec,
        out_shape=(jax.ShapeDtypeStruct((N_CHIPS, D_MODEL, BLK), BF16),)
        + tuple(jax.ShapeDtypeStruct((N_DEV, *b), dt) for b, dt in zip(small_blocks, (gw_out_parts.dtype, F32, F32), strict=True)),
        compiler_params=_params(),
    )(me_arr, hnT_tok, hnT_meta, dproj, gw_out_parts, dmeta_rows, acc_conv, acc_b2, acc_b1, acc_f2)


def _adamw(w, g, m, v):
    m = ADAM_B1 * m + (1.0 - ADAM_B1) * g
    v = ADAM_B2 * v + (1.0 - ADAM_B2) * (g * g)
    m_hat = m / (1.0 - ADAM_B1**ADAM_STEP)
    v_hat = v / (1.0 - ADAM_B2**ADAM_STEP)
    delta = -ADAM_LR * (m_hat / (jnp.sqrt(v_hat) + ADAM_EPS) + ADAM_WD * w)
    return delta, m, v


def _small_leaves(meta, conv_w, n1, rg, fg):
    return meta, conv_w, n1.reshape(8, CHUNK), rg.reshape(4, CHUNK), fg.reshape(8, CHUNK)


def _from_small_leaves(meta, conv_w, n1, rg, fg):
    return meta, conv_w, n1.reshape(D_MODEL), rg.reshape(D_RET), fg.reshape(D_MODEL)


ADAM_GRID = 4


def _adamw_all(land_in, land_out, land_sc, land_ga, big, small):
    n_leaf = 5
    lands = (land_in, land_out)
    n_big = len(big)

    def body(*refs):
        big_in, sc_ref, ga_ref = refs[: 4 * n_big], refs[4 * n_big], refs[4 * n_big + 1]
        small_in = refs[4 * n_big + 2 : 4 * n_big + 2 + 3 * n_leaf]
        big_out = refs[4 * n_big + 2 + 3 * n_leaf : 8 * n_big + 2 + 3 * n_leaf]
        small_out = refs[8 * n_big + 2 + 3 * n_leaf :]
        for b in range(n_big):
            p_ref, w_ref, m_ref, v_ref = big_in[4 * b : 4 * b + 4]
            g_ref, d_ref, nm_ref, nv_ref = big_out[4 * b : 4 * b + 4]
            g = p_ref[0].astype(F32)
            for s in range(1, p_ref.shape[0]):
                g = g + p_ref[s].astype(F32)
            g_ref[...] = g
            d_ref[...], nm_ref[...], nv_ref[...] = _adamw(w_ref[...], g, m_ref[...], v_ref[...])

        @pl.when(pl.program_id(0) == 0)
        def _():
            sc, ga = sc_ref[0], ga_ref[0]
            for s in range(1, N_DEV):
                sc = sc + sc_ref[s]
                ga = ga + ga_ref[s]
            grads = (
                sc[SC_META0 : SC_META0 + N_META],
                sc[SC_CONV0 : SC_CONV0 + 3, 0 : D_CONV // N_DEV],
                ga[GA_N1 : GA_N1 + 8],
                ga[GA_RG : GA_RG + 4],
                ga[GA_FG : GA_FG + 8],
            )
            for leaf, g in enumerate(grads):
                d, nm, nv = _adamw(
                    small_in[leaf][...], g, small_in[n_leaf + leaf][...], small_in[2 * n_leaf + leaf][...]
                )
                small_out[leaf][...] = g
                small_out[n_leaf + leaf][...] = d
                small_out[2 * n_leaf + leaf][...] = nm
                small_out[3 * n_leaf + leaf][...] = nv
            small_out[4 * n_leaf][...] = ga[GA_LOSS : GA_LOSS + 1]

    in_specs, out_specs, out_shape, operands = [], [], [], []
    for land, (w, m, v) in zip(lands, big, strict=True):
        n_r, n_c = w.shape
        tr = n_r // ADAM_GRID
        blk = pl.BlockSpec((tr, n_c), lambda i: (i, 0))
        in_specs += [pl.BlockSpec((land.shape[0], tr, n_c), lambda i: (0, i, 0)), blk, blk, blk]
        out_specs += [blk] * 4
        out_shape += [jax.ShapeDtypeStruct((n_r, n_c), F32)] * 4
        operands += [land, w, m, v]
    small_shapes = [jax.ShapeDtypeStruct(a.shape, F32) for a in small[0]] * 4 + [jax.ShapeDtypeStruct((1, CHUNK), F32)]
    out = pl.pallas_call(
        body,
        name="adamw_all",
        grid=(ADAM_GRID,),
        in_specs=in_specs + [_VMEM] * (2 + 3 * n_leaf),
        out_specs=tuple(out_specs + [_VMEM] * len(small_shapes)),
        out_shape=tuple(out_shape + small_shapes),
        compiler_params=_params(),
    )(*operands, land_sc, land_ga, *small[0], *small[1], *small[2])
    bigs = tuple(out[4 * b : 4 * b + 4] for b in range(n_big))
    rest = out[4 * n_big :]
    return bigs, tuple(rest[k * n_leaf : (k + 1) * n_leaf] for k in range(4)), rest[4 * n_leaf]


def kernel(x, meta, norm1_g, w_in, conv_w, ret_norm_g, w_out, final_g, loss_target, m_meta, m_norm1_g, m_w_in, m_conv_w, m_ret_norm_g, m_w_out, m_final_g, v_meta, v_norm1_g, v_w_in, v_conv_w, v_ret_norm_g, v_w_out, v_final_g):
    seq = x.shape[1]
    assert x.shape == (1, seq, D_MODEL) and seq % ROW_TILE == 0
    n_tiles = seq // ROW_TILE + 1
    x2, t2 = x[0], loss_target[0]

    small_w = _small_leaves(meta, conv_w, norm1_g, ret_norm_g, final_g)
    g1 = norm1_g.reshape(1, D_MODEL)
    hn_tok, hnT_tok, w3, meta_chunk, cw8 = _gather_weights_and_norm(x2, g1, w_in, meta, conv_w)

    rot = _rotary_tables(n_tiles, ROW_TILE)
    dec, _, zeta, xi = _decay_tables()
    gr = ret_norm_g.reshape(1, D_RET)
    gf = final_g.reshape(1, D_MODEL)

    hnT_meta, proj, conv_out, conv, o, states, wo3 = _f1(hn_tok, meta_chunk, g1, w3, cw8, rot, dec, zeta, xi, w_out)
    wout = wo3.reshape(D_MODEL, D_MODEL)
    dh2, gw_out, acc_f2 = _f2(proj, o, conv_out, x2, t2, wout, gr, gf)
    dproj_ret, dco, acc_b1 = _b1(dh2, proj, o, states, wout, gr, _decay_tables(B1_CHUNK), zeta, rot)
    dproj, grad_x2, dmeta_rows, acc_b2, acc_conv = _b2(dproj_ret, dco, proj, conv, w3, cw8, x2, meta_chunk, dh2, g1)
    gw_out_parts = gw_out.reshape(N_DEV, D_MODEL // N_DEV, D_MODEL)

    me_arr = (4 * lax.axis_index("x") + 2 * lax.axis_index("y") + lax.axis_index("c")).astype(jnp.int32).reshape(1)
    land_in, land_out, land_sc, land_ga = _gw_in_scatter(
        hnT_tok, hnT_meta, dproj, gw_out_parts, dmeta_rows, acc_conv, acc_b2, acc_b1, acc_f2, me_arr
    )

    (w_in_out, w_out_out), small_out, loss_row = _adamw_all(
        land_in,
        land_out,
        land_sc,
        land_ga,
        ((w_in, m_w_in, v_w_in), (w_out, m_w_out, v_w_out)),
        (
            small_w,
            _small_leaves(m_meta, m_conv_w, m_norm1_g, m_ret_norm_g, m_final_g),
            _small_leaves(v_meta, v_conv_w, v_norm1_g, v_ret_norm_g, v_final_g),
        ),
    )
    g_w_in, d_w_in, nm_w_in, nv_w_in = w_in_out
    g_w_out, d_w_out, nm_w_out, nv_w_out = w_out_out
    loss = loss_row[0, 0]
    grad_x = grad_x2.reshape(1, seq, D_MODEL)

    def leaves(w_in_leaf, w_out_leaf, small_leaves):
        meta_leaf, conv_leaf, n1_leaf, rg_leaf, fg_leaf = _from_small_leaves(*small_leaves)
        return (meta_leaf, n1_leaf, w_in_leaf, conv_leaf, rg_leaf, w_out_leaf, fg_leaf)

    return (
        loss,
        grad_x,
        *leaves(g_w_in, g_w_out, small_out[0]),
        *leaves(d_w_in, d_w_out, small_out[1]),
        *leaves(nm_w_in, nm_w_out, small_out[2]),
        *leaves(nv_w_in, nv_w_out, small_out[3]),
    )
```

```python
import math

import jax
import jax.numpy as jnp
import numpy as np
from jax import lax
from jax.experimental import pallas as pl
from jax.experimental.pallas import tpu as pltpu

F32 = jnp.float32
BF16 = jnp.bfloat16

N_DEV = 8
N_CHIPS = 4
D_MODEL = 1024
N_META = 16
CHUNK = 128
D_CONV = 512
D_RET = 512
N_HEADS = 4
HEAD_DIM = 128
N_PROJ = 8
BLK = 512
ROPE_BASE = 10000.0
EPS = 1e-6
Q_SCALE = HEAD_DIM ** -0.5
LOG_G = tuple(math.log(1.0 - 2.0 ** (-5.0 - h)) for h in range(N_HEADS))
CHUNK_DECAY = tuple(math.exp(CHUNK * lg) for lg in LOG_G)

ADAM_LR = 0.001
ADAM_B1 = 0.9
ADAM_B2 = 0.999
ADAM_EPS = 1e-08
ADAM_WD = 0.01
ADAM_STEP = 10

ROW_TILE = 512
PAD_ROWS = ROW_TILE - N_META
LIVE0 = ROW_TILE - CHUNK
B1_CHUNK = 256
VMEM_LIMIT = 56 * 1024 * 1024

SC_META0, SC_CONV0 = 0, 16
GA_N1, GA_RG, GA_FG, GA_LOSS = 0, 8, 16, 24

NT = (((1,), (1,)), ((), ()))
TN = (((0,), (0,)), ((), ()))
MESH = pl.DeviceIdType.MESH

_VMEM = pl.BlockSpec(memory_space=pltpu.VMEM)
_HBM = pl.BlockSpec(memory_space=pltpu.HBM)


def _params(n_axes=1):
    return pltpu.CompilerParams(dimension_semantics=("arbitrary",) * n_axes, vmem_limit_bytes=VMEM_LIMIT)


def _sigmoid(x):
    return 0.5 * jnp.tanh(0.5 * x) + 0.5


def _decay_tables(chunk=CHUNK):
    idx = np.arange(chunk, dtype=np.float64)
    diff = idx[:, None] - idx[None, :]
    dec = np.stack([np.where(diff >= 0, np.exp(diff * lg), 0.0) for lg in LOG_G])
    zeta = np.stack([np.exp((chunk - 1 - idx) * lg) for lg in LOG_G])
    xi = np.stack([np.exp((idx + 1.0) * lg) for lg in LOG_G])
    ones = np.ones((1, 1, HEAD_DIM))
    return (
        jnp.asarray(dec, F32),
        jnp.asarray(dec.transpose(0, 2, 1), F32),
        jnp.asarray(zeta[:, :, None] * ones, F32),
        jnp.asarray(xi[:, :, None] * ones, F32),
    )


def _rotary_tables(n_tiles, tm):
    half = HEAD_DIM // 2
    freqs = (1.0 / (np.float32(ROPE_BASE) ** (np.arange(half, dtype=np.float32) / np.float32(half)))).astype(np.float64)
    sign = np.concatenate([-np.ones(half), np.ones(half)])
    two = lambda a: np.concatenate([a, a], axis=1)
    base = two((np.arange(n_tiles, dtype=np.float64) * tm - PAD_ROWS)[:, None] * freqs[None, :])
    off = two(np.arange(tm, dtype=np.float64)[:, None] * freqs[None, :])
    as32 = lambda a: jnp.asarray(a, F32)
    return as32(np.cos(base)), as32(np.sin(base) * sign), as32(np.cos(off)), as32(np.sin(off) * sign)


def _tile_rotary(ca_ref, sa_ref, cb_ref, sb_ref, tile):
    ca, sa = ca_ref[pl.ds(tile, 1), :], sa_ref[pl.ds(tile, 1), :]
    cb, sb = cb_ref[...], sb_ref[...]
    return ca * cb - sa * sb, sa * cb + ca * sb


def _rot(t, cos2, sin2):
    return t * cos2 + pltpu.roll(t, HEAD_DIM // 2, 1) * sin2


def _rot_bwd(d, cos2, sin2):
    return d * cos2 + pltpu.roll(d * sin2, HEAD_DIM // 2, 1)


def _token_tile(i):
    return jnp.maximum(i - 1, 0)


GATHER_PAIRS = 7


def _two_level_gather(src_ref, out_ref, send_sems, recv_sems, local_sem, base=0):
    x, y, c = lax.axis_index("x"), lax.axis_index("y"), lax.axis_index("c")
    me, sibling = (x, y, c), (x, y, 1 - c)
    xnb, ynb, diag = (1 - x, y), (x, 1 - y), (1 - x, 1 - y)
    relayed = (jnp.bitwise_xor(x, 1 - c), jnp.bitwise_xor(y, c))
    other = (jnp.bitwise_xor(x, c), jnp.bitwise_xor(y, 1 - c))

    def copy(k, block, to, src=None):
        dst = out_ref.at[4 * block[0] + 2 * block[1] + block[2]]
        return pltpu.make_async_remote_copy(
            src_ref=dst if src is None else src,
            dst_ref=dst,
            send_sem=send_sems.at[base + k],
            recv_sem=recv_sems.at[base + k],
            device_id=to,
            device_id_type=MESH,
        )

    mine = pltpu.make_async_copy(src_ref, out_ref.at[4 * x + 2 * y + c], local_sem)
    first = [copy(1, me, (*xnb, c), src=src_ref), copy(2, me, (*ynb, c), src=src_ref), copy(0, me, sibling, src=src_ref)]
    relay = copy(3, (*relayed, c), (*other, c))
    passed = [copy(4 + j, (*chip, c), sibling) for j, chip in enumerate((xnb, ynb, diag))]

    def start():
        mine.start()
        for cp in first:
            cp.start()

    def forward():
        copy(1 + c, (*relayed, c), me).wait_recv()
        relay.start()
        copy(2 - c, (*other, c), me).wait_recv()
        passed[0].start()
        passed[1].start()

    def forward_relayed():
        copy(3, (*diag, c), me).wait_recv()
        passed[2].start()

    def finish():
        copy(0, sibling, me).wait_recv()
        for j, chip in enumerate((xnb, ynb, diag)):
            copy(4 + j, (*chip, 1 - c), me).wait_recv()
        for cp in first + [relay] + passed:
            cp.wait_send()
        mine.wait()

    return start, forward, forward_relayed, finish


W_IN_CHUNKS = 4


def _gather_weights_and_norm(x2, g1, w_shard, meta, conv_w):
    per = GATHER_PAIRS
    n_ch = W_IN_CHUNKS
    rows = w_shard.shape[0] // n_ch
    tm = ROW_TILE
    nt = x2.shape[0] // tm
    cols = D_CONV // N_DEV
    pack_shape = (SC_CONV0 + 8, CHUNK)
    starts = [0] * (n_ch + 1)
    forwards = [min(5 + 2 * p, nt - 1) for p in range(n_ch)] + [min(12, nt - 1)]
    relayeds = [min(12 + p, nt - 1) for p in range(n_ch)] + [nt - 1]
    finishes = [nt - 1] * (n_ch + 1)

    def body(x_ref, g_ref, w_ref, meta_ref, conv_ref, hn_ref, hnT_ref, w_all_ref, mc_ref, cw_ref,
             w_bf, small_ref, small_all_ref, send_sems, recv_sems, local_sems):
        i = pl.program_id(0)

        @pl.when(i == 0)
        def _():
            w_bf[...] = w_ref[...].astype(BF16)
            small_ref[...] = jnp.zeros_like(small_ref)
            small_ref[SC_META0 : SC_META0 + N_META, :] = meta_ref[...]
            small_ref[SC_CONV0 : SC_CONV0 + 3, 0:cols] = conv_ref[...]

        parts = [
            _two_level_gather(
                w_bf.at[pl.ds(ch * rows, rows)],
                w_all_ref.at[:, pl.ds(ch * rows, rows)],
                send_sems,
                recv_sems,
                local_sems.at[ch],
                ch * per,
            )
            for ch in range(n_ch)
        ]
        parts.append(_two_level_gather(small_ref, small_all_ref, send_sems, recv_sems, local_sems.at[n_ch], n_ch * per))
        for phase, steps in enumerate((starts, forwards, relayeds, finishes)):
            for part, step in zip(parts, steps, strict=True):
                pl.when(i == step)(part[phase])

        h = x_ref[...]
        r = lax.rsqrt(jnp.mean(h * h, axis=-1, keepdims=True) + EPS)
        hn = (h * r * g_ref[...]).astype(BF16)
        hn_ref[...] = hn
        hnT_ref[...] = hn.T

        @pl.when(i == nt - 1)
        def _():
            mc_ref[0 : CHUNK - N_META, :] = jnp.zeros((CHUNK - N_META, D_MODEL), F32)
            cw_ref[...] = jnp.zeros_like(cw_ref)
            for d in range(N_DEV):
                mc_ref[CHUNK - N_META : CHUNK, d * CHUNK : (d + 1) * CHUNK] = small_all_ref[d, SC_META0 : SC_META0 + N_META, :]
                lane0 = (d * cols) % CHUNK
                rows8 = small_all_ref[d, SC_CONV0 : SC_CONV0 + 8, :]
                rows8 = pltpu.roll(rows8, lane0, 1) if lane0 else rows8
                cw_ref[0:3, d * cols : (d + 1) * cols] = rows8[0:3, lane0 : lane0 + cols]

    const = lambda r, c: pl.BlockSpec((r, c), lambda i: (0, 0))
    return pl.pallas_call(
        body,
        name="gather_weights_norm",
        grid=(nt,),
        out_shape=(
            jax.ShapeDtypeStruct(x2.shape, BF16),
            jax.ShapeDtypeStruct(x2.shape[::-1], BF16),
            jax.ShapeDtypeStruct((N_DEV, *w_shard.shape), BF16),
            jax.ShapeDtypeStruct((CHUNK, D_MODEL), F32),
            jax.ShapeDtypeStruct((8, D_CONV), F32),
        ),
        in_specs=[pl.BlockSpec((tm, D_MODEL), lambda i: (i, 0)), _VMEM, _VMEM, _VMEM, _VMEM],
        out_specs=(
            pl.BlockSpec((tm, D_MODEL), lambda i: (i, 0)),
            pl.BlockSpec((D_MODEL, tm), lambda i: (0, i)),
            _HBM,
            const(CHUNK, D_MODEL),
            const(8, D_CONV),
        ),
        scratch_shapes=[
            pltpu.VMEM(w_shard.shape, BF16),
            pltpu.VMEM(pack_shape, F32),
            pltpu.VMEM((N_DEV, *pack_shape), F32),
            pltpu.SemaphoreType.DMA(((n_ch + 1) * per,)),
            pltpu.SemaphoreType.DMA(((n_ch + 1) * per,)),
            pltpu.SemaphoreType.DMA((n_ch + 1,)),
        ],
        compiler_params=_params(),
    )(x2, g1, w_shard, meta, conv_w)


def _f1(hn_tok, meta_chunk, g1, w3, cw8, rot, dec, zeta, xi, wo_shard):
    tm = ROW_TILE
    nt = hn_tok.shape[0] // tm + 1
    n_rows = nt * tm
    nct = tm // CHUNK

    def body(hn_ref, mc_ref, g_ref, w_ref, cw_ref, ca_ref, sa_ref, cb_ref, sb_ref, dec_ref, zeta_ref, xi_ref, wo_ref,
             hnT_ref, pr_ref, co_ref, cv_ref, o_ref, st_ref, wo_all_ref,
             halo, state, wo_bf, send_sems, recv_sems, local_sem):
        i = pl.program_id(0)

        @pl.when(i == 0)
        def _():
            wo_bf[...] = wo_ref[...].astype(BF16)

        wo_phases = _two_level_gather(wo_bf, wo_all_ref, send_sems, recv_sems, local_sem)
        for step, phase in zip((0, nt // 3, 2 * nt // 3, nt - 1), wo_phases, strict=True):
            pl.when(i == step)(phase)

        def work(hn, r0):
            rs = slice(r0, tm)
            n = tm - r0

            def proj(j):
                return jnp.dot(hn, w_ref[j], preferred_element_type=F32)

            cos_all, sin_all = _tile_rotary(ca_ref, sa_ref, cb_ref, sb_ref, i)
            cos_t, sin_t = cos_all[rs, :], sin_all[rs, :]
            q, k, v = proj(4), proj(5), proj(6).astype(BF16)
            pr_ref[rs, 6 * BLK : 7 * BLK] = v
            chunks = range(r0 // CHUNK, tm // CHUNK)
            heads = []
            for hd in range(N_HEADS):
                cs = slice(hd * HEAD_DIM, (hd + 1) * HEAD_DIM)
                qh = (_rot(q[:, cs], cos_t, sin_t) * Q_SCALE).astype(BF16)
                kh = _rot(k[:, cs], cos_t, sin_t).astype(BF16)
                vh = v[:, cs]
                pr_ref[rs, 4 * BLK + hd * HEAD_DIM : 4 * BLK + (hd + 1) * HEAD_DIM] = qh
                pr_ref[rs, 5 * BLK + hd * HEAD_DIM : 5 * BLK + (hd + 1) * HEAD_DIM] = kh
                upd = {}
                for c in chunks:
                    cr = slice(c * CHUNK - r0, (c + 1) * CHUNK - r0)
                    kz = (kh[cr].astype(F32) * zeta_ref[hd]).astype(BF16)
                    upd[c] = lax.dot_general(kz, vh[cr], TN, preferred_element_type=F32)
                st = state[hd]
                outs = []
                for c in chunks:
                    cr = slice(c * CHUNK - r0, (c + 1) * CHUNK - r0)
                    qc, kc, vc = qh[cr], kh[cr], vh[cr]
                    st_bf = st.astype(BF16)
                    st_ref[c, hd] = st_bf
                    s = lax.dot_general(qc, kc, NT, preferred_element_type=F32) * dec_ref[hd]
                    inner = jnp.dot(s.astype(BF16), vc, preferred_element_type=F32)
                    qx = (qc.astype(F32) * xi_ref[hd]).astype(BF16)
                    outs.append(inner + jnp.dot(qx, st_bf, preferred_element_type=F32))
                    st = CHUNK_DECAY[hd] * st + upd[c]
                state[hd] = st
                heads.append(jnp.concatenate(outs, axis=0) if len(outs) > 1 else outs[0])
            o_ref[rs, :] = jnp.concatenate(heads, axis=1)

            cx, cb, cc, cg = proj(0), proj(1), proj(2), proj(3)
            u = cc * cx
            rows = lax.broadcasted_iota(jnp.int32, u.shape, 0)
            hl = halo[...]
            u1 = jnp.where(rows == 0, hl[7:8], pltpu.roll(u, 1, 0))
            u2 = jnp.where(rows == 0, hl[6:7], jnp.where(rows == 1, hl[7:8], pltpu.roll(u, 2, 0)))
            halo[...] = u[n - 8 : n]
            cw = cw_ref[...]
            conv = cw[0:1] * u2 + cw[1:2] * u1 + cw[2:3] * u
            co = (cb * conv * (cg * _sigmoid(cg))).astype(BF16)
            co_ref[rs, :] = co
            cv_ref[rs, :] = conv.astype(BF16)
            pr_ref[rs, 0 * BLK : 1 * BLK] = cx.astype(BF16)
            pr_ref[rs, 1 * BLK : 2 * BLK] = cb.astype(BF16)
            pr_ref[rs, 2 * BLK : 3 * BLK] = cc.astype(BF16)
            pr_ref[rs, 3 * BLK : 4 * BLK] = cg.astype(BF16)
            pr_ref[rs, 7 * BLK : 8 * BLK] = proj(7).astype(BF16)

        @pl.when(i == 0)
        def _():
            halo[...] = jnp.zeros_like(halo)
            state[...] = jnp.zeros_like(state)
            h = mc_ref[...]
            r = lax.rsqrt(jnp.mean(h * h, axis=-1, keepdims=True) + EPS)
            hn = (h * r * g_ref[...]).astype(BF16)
            hnT_ref[...] = hn.T
            work(hn, LIVE0)

        @pl.when(i > 0)
        def _():
            work(hn_ref[...], 0)

    row = lambda w: pl.BlockSpec((tm, w), lambda i: (i, 0))
    return pl.pallas_call(
        body,
        name="f1_inproj_conv",
        grid=(nt,),
        in_specs=[pl.BlockSpec((tm, D_MODEL), lambda i: (_token_tile(i), 0))] + [_VMEM] * 12,
        out_specs=(
            pl.BlockSpec((D_MODEL, CHUNK), lambda i: (0, 0)),
            row(N_PROJ * BLK),
            row(D_CONV),
            row(D_CONV),
            row(D_RET),
            pl.BlockSpec((nct, N_HEADS, HEAD_DIM, HEAD_DIM), lambda i: (i, 0, 0, 0)),
            _HBM,
        ),
        out_shape=(
            jax.ShapeDtypeStruct((D_MODEL, CHUNK), BF16),
            jax.ShapeDtypeStruct((n_rows, N_PROJ * BLK), BF16),
            jax.ShapeDtypeStruct((n_rows, D_CONV), BF16),
            jax.ShapeDtypeStruct((n_rows, D_CONV), BF16),
            jax.ShapeDtypeStruct((n_rows, D_RET), F32),
            jax.ShapeDtypeStruct((n_rows // CHUNK, N_HEADS, HEAD_DIM, HEAD_DIM), BF16),
            jax.ShapeDtypeStruct((N_DEV, *wo_shard.shape), BF16),
        ),
        scratch_shapes=[
            pltpu.VMEM((8, D_CONV), F32),
            pltpu.VMEM((N_HEADS, HEAD_DIM, HEAD_DIM), F32),
            pltpu.VMEM(wo_shard.shape, BF16),
            pltpu.SemaphoreType.DMA((GATHER_PAIRS,)),
            pltpu.SemaphoreType.DMA((GATHER_PAIRS,)),
            pltpu.SemaphoreType.DMA(()),
        ],
        compiler_params=_params(),
    )(hn_tok, meta_chunk, g1, w3, cw8, *rot, dec, zeta, xi, wo_shard)


def _group_norm(o):
    ys, rs = [], []
    for hd in range(N_HEADS):
        oh = o[:, hd * HEAD_DIM : (hd + 1) * HEAD_DIM]
        xc = oh - jnp.mean(oh, axis=-1, keepdims=True)
        rstd = lax.rsqrt(jnp.mean(xc * xc, axis=-1, keepdims=True) + EPS)
        ys.append(xc * rstd)
        rs.append(jnp.broadcast_to(rstd, oh.shape))
    return jnp.concatenate(ys, axis=1), jnp.concatenate(rs, axis=1)


def _f2(proj, o, conv_out, x2, t2, wout, gr, gf):
    n_rows = proj.shape[0]
    tm = ROW_TILE
    nt = n_rows // tm

    def body(o_ref, rg_ref, co_ref, x_ref, t_ref, wo_ref, gr_ref, gf_ref,
             dh2_ref, dco_ref, drg_ref, do_ref, gw_ref, acc_ref, accr_ref, gacc):
        i = pl.program_id(0)

        @pl.when(i == 0)
        def _():
            acc_ref[...] = jnp.zeros_like(acc_ref)
            accr_ref[...] = jnp.zeros_like(accr_ref)
            gacc[...] = jnp.zeros_like(gacc)
            dh2_ref[...] = jnp.zeros_like(dh2_ref)
            dco_ref[...] = jnp.zeros_like(dco_ref)
            drg_ref[...] = jnp.zeros_like(drg_ref)
            do_ref[...] = jnp.zeros_like(do_ref)

        @pl.when(i > 0)
        def _():
            yh, rstd = _group_norm(o_ref[...])
            rg = rg_ref[...].astype(F32)
            sg = _sigmoid(rg)
            silu = rg * sg
            grv = gr_ref[...]
            ro = (yh * grv * silu).astype(BF16)
            h2 = (
                x_ref[...]
                + jnp.dot(co_ref[...], wo_ref[0:D_CONV], preferred_element_type=F32)
                + jnp.dot(ro, wo_ref[D_CONV:], preferred_element_type=F32)
            )
            r2 = lax.rsqrt(jnp.mean(h2 * h2, axis=-1, keepdims=True) + EPS)
            yn = h2 * r2
            gfv = gf_ref[...]
            err = yn * gfv - t_ref[...]
            tile_loss = jnp.sum(jnp.sum(err * err, axis=-1, keepdims=True), axis=0, keepdims=True) * (0.5 / D_MODEL)
            acc_ref[0:1, :] += jnp.sum(err * yn, axis=0, keepdims=True) * (1.0 / D_MODEL)
            acc_ref[1:2, :] += tile_loss
            dyn = err * (gfv * (1.0 / D_MODEL))
            dh2 = r2 * (dyn - yn * jnp.mean(dyn * yn, axis=-1, keepdims=True))
            dh2_ref[...] = dh2
            dh2_bf = dh2.astype(BF16)
            dmix = lax.dot_general(dh2_bf, wo_ref[...], NT, preferred_element_type=F32)
            dco_ref[...] = dmix[:, :D_CONV].astype(BF16)
            dro = dmix[:, D_CONV:]
            dsilu = sg * (1.0 + rg * (1.0 - sg))
            drg_ref[...] = (dro * (yh * grv) * dsilu).astype(BF16)
            dret = dro * silu
            accr_ref[0:1, :] += jnp.sum(dret * yh, axis=0, keepdims=True)
            dyh = dret * grv
            for hd in range(N_HEADS):
                cs = slice(hd * HEAD_DIM, (hd + 1) * HEAD_DIM)
                a, b = dyh[:, cs], yh[:, cs]
                do_ref[:, cs] = (
                    rstd[:, cs]
                    * (a - jnp.mean(a, axis=-1, keepdims=True) - b * jnp.mean(a * b, axis=-1, keepdims=True))
                ).astype(BF16)
            gacc[0:D_CONV, :] += lax.dot_general(co_ref[...], dh2_bf, TN, preferred_element_type=F32)
            gacc[D_CONV:, :] += lax.dot_general(ro, dh2_bf, TN, preferred_element_type=F32)

        @pl.when(i == nt - 1)
        def _():
            gw_ref[...] = gacc[...].astype(BF16)

    row = lambda w, j=0: pl.BlockSpec((tm, w), lambda i: (i, j))
    tok = pl.BlockSpec((tm, D_MODEL), lambda i: (_token_tile(i), 0))
    return pl.pallas_call(
        body,
        name="f2_out_loss",
        grid=(nt,),
        in_specs=[row(D_RET), row(BLK, 7), row(D_CONV), tok, tok] + [_VMEM] * 3,
        out_specs=(
            row(D_MODEL),
            row(D_CONV),
            row(BLK),
            row(D_RET),
            pl.BlockSpec((D_MODEL, D_MODEL), lambda i: (0, 0)),
            pl.BlockSpec((8, D_MODEL), lambda i: (0, 0)),
            pl.BlockSpec((8, D_RET), lambda i: (0, 0)),
        ),
        out_shape=(
            jax.ShapeDtypeStruct((n_rows, D_MODEL), F32),
            jax.ShapeDtypeStruct((n_rows, D_CONV), BF16),
            jax.ShapeDtypeStruct((n_rows, BLK), BF16),
            jax.ShapeDtypeStruct((n_rows, D_RET), BF16),
            jax.ShapeDtypeStruct((D_MODEL, D_MODEL), BF16),
            jax.ShapeDtypeStruct((8, D_MODEL), F32),
            jax.ShapeDtypeStruct((8, D_RET), F32),
        ),
        scratch_shapes=[pltpu.VMEM((D_MODEL, D_MODEL), F32)],
        compiler_params=_params(),
    )(o, proj, conv_out, x2, t2, wout, gr, gf)


def _b1(do, drg, proj, states, tables, zeta0, rot):
    n_rows = do.shape[0]
    tm = ROW_TILE
    nt = n_rows // tm
    chunk = B1_CHUNK
    nct = tm // chunk
    decay = tuple(math.exp(chunk * lg) for lg in LOG_G)

    def body(do_ref, drg_ref, pr_ref, st_ref, dec_ref, dect_ref, zeta_ref, xi_ref, zeta0_ref,
             ca_ref, sa_ref, cb_ref, sb_ref, dp_ref, dstate):
        i = pl.program_id(0)
        tile = nt - 1 - i
        cos_all, sin_all = _tile_rotary(ca_ref, sa_ref, cb_ref, sb_ref, tile)

        @pl.when(i == 0)
        def _():
            dstate[...] = jnp.zeros_like(dstate)

        @pl.when(tile > 0)
        def _():
            dp_ref[:, 3 * BLK : 4 * BLK] = drg_ref[...]
            for hd in range(N_HEADS):
                cs = slice(hd * HEAD_DIM, (hd + 1) * HEAD_DIM)
                dupd = {}
                for c in range(nct):
                    rs = slice(c * chunk, (c + 1) * chunk)
                    qx = (pr_ref[rs, cs].astype(F32) * xi_ref[hd]).astype(BF16)
                    dupd[c] = lax.dot_general(qx, do_ref[rs, cs], TN, preferred_element_type=F32)
                dst = dstate[hd]
                for c in reversed(range(nct)):
                    rs = slice(c * chunk, (c + 1) * chunk)
                    cos_t, sin_t = cos_all[rs, :], sin_all[rs, :]
                    q = pr_ref[rs, hd * HEAD_DIM : (hd + 1) * HEAD_DIM]
                    k = pr_ref[rs, BLK + hd * HEAD_DIM : BLK + (hd + 1) * HEAD_DIM]
                    v = pr_ref[rs, 2 * BLK + hd * HEAD_DIM : 2 * BLK + (hd + 1) * HEAD_DIM]
                    do = do_ref[rs, cs]
                    st_bf = st_ref[c * (chunk // CHUNK), hd]
                    dst_bf = dst.astype(BF16)
                    zt, xt = zeta_ref[hd], xi_ref[hd]
                    sT = (lax.dot_general(k, q, NT, preferred_element_type=F32) * dect_ref[hd]).astype(BF16)
                    dsT = (lax.dot_general(v, do, NT, preferred_element_type=F32) * dect_ref[hd]).astype(BF16)
                    ds = (lax.dot_general(do, v, NT, preferred_element_type=F32) * dec_ref[hd]).astype(BF16)
                    kz = (k.astype(F32) * zt).astype(BF16)
                    dv = jnp.dot(sT, do, preferred_element_type=F32) + jnp.dot(kz, dst_bf, preferred_element_type=F32)
                    dq = jnp.dot(ds, k, preferred_element_type=F32) + xt * lax.dot_general(
                        do, st_bf, NT, preferred_element_type=F32
                    )
                    dk = jnp.dot(dsT, q, preferred_element_type=F32) + zt * lax.dot_general(
                        v, dst_bf, NT, preferred_element_type=F32
                    )
                    dst = decay[hd] * dst + dupd[c]
                    dp_ref[rs, hd * HEAD_DIM : (hd + 1) * HEAD_DIM] = _rot_bwd(dq * Q_SCALE, cos_t, sin_t).astype(BF16)
                    dp_ref[rs, BLK + hd * HEAD_DIM : BLK + (hd + 1) * HEAD_DIM] = _rot_bwd(dk, cos_t, sin_t).astype(BF16)
                    dp_ref[rs, 2 * BLK + hd * HEAD_DIM : 2 * BLK + (hd + 1) * HEAD_DIM] = dv.astype(BF16)
                dstate[hd] = dst

        @pl.when(tile == 0)
        def _():
            rs = slice(LIVE0, tm)
            cos_t, sin_t = cos_all[rs, :], sin_all[rs, :]
            zeros = jnp.zeros((tm - LIVE0, HEAD_DIM), BF16)
            for hd in range(N_HEADS):
                k = pr_ref[rs, BLK + hd * HEAD_DIM : BLK + (hd + 1) * HEAD_DIM]
                v = pr_ref[rs, 2 * BLK + hd * HEAD_DIM : 2 * BLK + (hd + 1) * HEAD_DIM]
                dst_bf = dstate[hd].astype(BF16)
                zt = zeta0_ref[hd]
                kz = (k.astype(F32) * zt).astype(BF16)
                dv = jnp.dot(kz, dst_bf, preferred_element_type=F32)
                dk = zt * lax.dot_general(v, dst_bf, NT, preferred_element_type=F32)
                dp_ref[rs, hd * HEAD_DIM : (hd + 1) * HEAD_DIM] = zeros
                dp_ref[rs, BLK + hd * HEAD_DIM : BLK + (hd + 1) * HEAD_DIM] = _rot_bwd(dk, cos_t, sin_t).astype(BF16)
                dp_ref[rs, 2 * BLK + hd * HEAD_DIM : 2 * BLK + (hd + 1) * HEAD_DIM] = dv.astype(BF16)
                dp_ref[rs, 3 * BLK + hd * HEAD_DIM : 3 * BLK + (hd + 1) * HEAD_DIM] = zeros

    half = N_PROJ * BLK // 2
    rev = lambda w, j=0: pl.BlockSpec((tm, w), lambda i: (nt - 1 - i, j))
    return pl.pallas_call(
        body,
        name="b1_dret",
        grid=(nt,),
        in_specs=[
            rev(D_RET),
            rev(BLK),
            rev(half, 1),
            pl.BlockSpec((tm // CHUNK, N_HEADS, HEAD_DIM, HEAD_DIM), lambda i: (nt - 1 - i, 0, 0, 0)),
        ]
        + [_VMEM] * 9,
        out_specs=rev(half, 1),
        out_shape=jax.ShapeDtypeStruct((n_rows, N_PROJ * BLK), BF16),
        scratch_shapes=[pltpu.VMEM((N_HEADS, HEAD_DIM, HEAD_DIM), F32)],
        compiler_params=_params(),
    )(do, drg, proj, states, *tables, zeta0, *rot)


def _b2(dproj, dco, proj, conv, w3, cw8, x2, meta_chunk, dh2, g1):
    n_rows = dproj.shape[0]
    tm = ROW_TILE
    nt = n_rows // tm
    half = N_PROJ * BLK // 2

    def body(dpr_ref, dco_ref, pr_ref, cv_ref, w_ref, cw_ref, x_ref, mc_ref, dh2_ref, g_ref,
             dpc_ref, gx_ref, dm_ref, acc_ref, accc_ref, halo):
        i = pl.program_id(0)
        tile = nt - 1 - i

        @pl.when(i == 0)
        def _():
            acc_ref[...] = jnp.zeros_like(acc_ref)
            accc_ref[...] = jnp.zeros_like(accc_ref)
            halo[...] = jnp.zeros_like(halo)

        def work(h, r0):
            rs = slice(r0, tm)
            n = tm - r0
            dco = dco_ref[rs, :].astype(F32)
            cx = pr_ref[rs, 0 * BLK : 1 * BLK].astype(F32)
            cb = pr_ref[rs, 1 * BLK : 2 * BLK].astype(F32)
            cc = pr_ref[rs, 2 * BLK : 3 * BLK].astype(F32)
            cg = pr_ref[rs, 3 * BLK : 4 * BLK].astype(F32)
            conv = cv_ref[rs, :].astype(F32)
            sg = _sigmoid(cg)
            silu, dsilu = cg * sg, sg * (1.0 + cg * (1.0 - sg))
            t = dco * cb
            dcb = (dco * conv * silu).astype(BF16)
            dcg = (t * conv * dsilu).astype(BF16)
            dconv = t * silu
            rows = lax.broadcasted_iota(jnp.int32, dconv.shape, 0)
            hl = halo[...]
            dc1 = jnp.where(rows == n - 1, hl[0:1], pltpu.roll(dconv, n - 1, 0))
            dc2 = jnp.where(rows == n - 2, hl[0:1], jnp.where(rows == n - 1, hl[1:2], pltpu.roll(dconv, n - 2, 0)))
            halo[...] = dconv[0:8]
            cw = cw_ref[...]
            du = cw[2:3] * dconv + cw[1:2] * dc1 + cw[0:1] * dc2
            u = cc * cx
            accc_ref[0:1, :] += jnp.sum(u * dc2, axis=0, keepdims=True)
            accc_ref[1:2, :] += jnp.sum(u * dc1, axis=0, keepdims=True)
            accc_ref[2:3, :] += jnp.sum(u * dconv, axis=0, keepdims=True)
            dcx = (du * cc).astype(BF16)
            dcc = (du * cx).astype(BF16)
            dhn = lax.dot_general(dpr_ref[rs, 0:BLK], w_ref[4], NT, preferred_element_type=F32)
            for j in range(1, 4):
                dhn += lax.dot_general(dpr_ref[rs, j * BLK : (j + 1) * BLK], w_ref[4 + j], NT, preferred_element_type=F32)
            for blk, d in ((1, dcb), (3, dcg), (0, dcx), (2, dcc)):
                dpc_ref[rs, blk * BLK : (blk + 1) * BLK] = d
                dhn += lax.dot_general(d, w_ref[blk], NT, preferred_element_type=F32)
            r = lax.rsqrt(jnp.mean(h * h, axis=-1, keepdims=True) + EPS)
            hh = h * r
            acc_ref[0:1, :] += jnp.sum(dhn * hh, axis=0, keepdims=True)
            dg = dhn * g_ref[...]
            return dh2_ref[rs, :] + r * (dg - hh * jnp.mean(dg * hh, axis=-1, keepdims=True))

        @pl.when(tile == 0)
        def _():
            dh = work(mc_ref[...], LIVE0)
            dm_ref[...] = dh[CHUNK - N_META : CHUNK]

        @pl.when(tile > 0)
        def _():
            gx_ref[...] = work(x_ref[...], 0)

    rev = lambda w, j=0: pl.BlockSpec((tm, w), lambda i: (nt - 1 - i, j))
    tok = pl.BlockSpec((tm, D_MODEL), lambda i: (_token_tile(nt - 1 - i), 0))
    const = lambda r, c: pl.BlockSpec((r, c), lambda i: (0, 0))
    return pl.pallas_call(
        body,
        name="b2_dconv_dh",
        grid=(nt,),
        in_specs=[rev(half, 1), rev(D_CONV), rev(half, 0), rev(D_CONV), _VMEM, _VMEM, tok, _VMEM, rev(D_MODEL), _VMEM],
        out_specs=(rev(half, 0), tok, const(N_META, D_MODEL), const(8, D_MODEL), const(8, D_CONV)),
        out_shape=(
            jax.ShapeDtypeStruct(dproj.shape, BF16),
            jax.ShapeDtypeStruct(x2.shape, F32),
            jax.ShapeDtypeStruct((N_META, D_MODEL), F32),
            jax.ShapeDtypeStruct((8, D_MODEL), F32),
            jax.ShapeDtypeStruct((8, D_CONV), F32),
        ),
        scratch_shapes=[pltpu.VMEM((8, D_CONV), F32)],
        input_output_aliases={0: 0},
        compiler_params=_params(),
    )(dproj, dco, proj, conv, w3, cw8, x2, meta_chunk, dh2, g1)


def _gw_in_scatter(hnT_tok, hnT_meta, dproj, gw_out_parts, dmeta_rows, acc_conv, acc_b2, acc_b1, acc_f2, me_arr):
    n_rows = dproj.shape[0]
    last = N_DEV - 1
    cols = D_CONV // N_DEV
    by_dest = (True, True, False)
    small_blocks = (gw_out_parts.shape[1:], (SC_CONV0 + 8, CHUNK), (GA_LOSS + 8, CHUNK))

    def body(me_ref, a_ref, am_ref, b_ref, go_ref, dm_ref, ac_ref, ab2_ref, ab1_ref, af2_ref,
             land_in, land_go, land_sc, land_ga, d2d_buf, d2d_land, ici_buf, sc_ref, ga_ref,
             d2d_send, d2d_recv, ici_send, ici_recv, send_sems, recv_sems, local_sems):
        del me_ref
        s = pl.program_id(0)
        t = last - s
        q = t >> 1
        x, y, c = lax.axis_index("x"), lax.axis_index("y"), lax.axis_index("c")
        me = 4 * x + 2 * y + c
        chip = 2 * x + y
        srcs, lands = (go_ref, sc_ref, ga_ref), (land_go, land_sc, land_ga)

        def peer_at(k):
            return (1 - x if k & 4 else x, 1 - y if k & 2 else y, 1 - c if k & 1 else c)

        def small_copy(a, k):
            px, py, pc = peer_at(k)
            return pltpu.make_async_remote_copy(
                src_ref=srcs[a].at[4 * px + 2 * py + pc] if by_dest[a] else srcs[a],
                dst_ref=lands[a].at[me],
                send_sem=send_sems.at[a * last + k - 1],
                recv_sem=recv_sems.at[a * last + k - 1],
                device_id=(px, py, pc),
                device_id_type=MESH,
            )

        def small_local(a):
            return pltpu.make_async_copy(srcs[a].at[me] if by_dest[a] else srcs[a], lands[a].at[me], local_sems.at[a + 1])

        def d2d_copy(j):
            return pltpu.make_async_remote_copy(
                src_ref=d2d_buf.at[j],
                dst_ref=d2d_land.at[j],
                send_sem=d2d_send.at[j],
                recv_sem=d2d_recv.at[j],
                device_id=(x, y, 1 - c),
                device_id_type=MESH,
            )

        def ici_copy(j, to):
            return pltpu.make_async_remote_copy(
                src_ref=ici_buf.at[j],
                dst_ref=land_in.at[chip],
                send_sem=ici_send.at[j],
                recv_sem=ici_recv.at[j],
                device_id=to,
                device_id_type=MESH,
            )

        def own_copy():
            return pltpu.make_async_copy(ici_buf.at[0], land_in.at[chip], local_sems.at[0])

        @pl.when(s == 0)
        def _():
            sc_ref[...] = jnp.zeros_like(sc_ref)
            for d in range(N_DEV):
                sc_ref[d, SC_META0 : SC_META0 + N_META, :] = dm_ref[:, d * CHUNK : (d + 1) * CHUNK]
                lane0 = (d * cols) % CHUNK
                rows8 = ac_ref[:, d * cols - lane0 : d * cols - lane0 + CHUNK]
                rows8 = pltpu.roll(rows8, CHUNK - lane0, 1) if lane0 else rows8
                sc_ref[d, SC_CONV0 : SC_CONV0 + 3, 0:cols] = rows8[0:3, 0:cols]
            ga_ref[...] = jnp.zeros_like(ga_ref)
            for row0, acc, n in ((GA_N1, ab2_ref, 8), (GA_RG, ab1_ref, 4), (GA_FG, af2_ref, 8)):
                for j in range(n):
                    ga_ref[row0 + j : row0 + j + 1, :] = acc[0:1, j * CHUNK : (j + 1) * CHUNK]
            ga_ref[GA_LOSS : GA_LOSS + 1, :] = af2_ref[1:2, 0:CHUNK]
            for a in range(3):
                small_local(a).start()
            for k in range(last, 0, -1):
                for a in range(3):
                    small_copy(a, k).start()

        blk = jnp.dot(a_ref[...], b_ref[ROW_TILE:, :], preferred_element_type=F32)
        blk += jnp.dot(am_ref[...], b_ref[LIVE0:ROW_TILE, :], preferred_element_type=F32)

        @pl.when((t & 1) == 1)
        def _():
            d2d_buf[q] = blk.astype(BF16)
            d2d_copy(q).start()

        @pl.when((t & 1) == 0)
        def _():
            d2d_copy(q).wait_recv()
            ici_buf[q] = (blk + d2d_land[q].astype(F32)).astype(BF16)

            @pl.when(t != 0)
            def _():
                ici_copy(q, (jnp.bitwise_xor(x, (t >> 2) & 1), jnp.bitwise_xor(y, (t >> 1) & 1), c)).start()

            @pl.when(t == 0)
            def _():
                own_copy().start()
                for j in range(N_CHIPS):
                    d2d_copy(j).wait_send()
                for j in range(N_CHIPS - 1, 0, -1):
                    ici_copy(j, peer_at(2 * j)).wait()
                for k in range(last, 0, -1):
                    for a in range(3):
                        small_copy(a, k).wait()
                for a in range(3):
                    small_local(a).wait()
                own_copy().wait()

    grid_spec = pltpu.PrefetchScalarGridSpec(
        num_scalar_prefetch=1,
        grid=(N_DEV,),
        in_specs=[_VMEM, _VMEM, pl.BlockSpec((n_rows, BLK), lambda s, me: (0, jnp.bitwise_xor(me[0], last - s))), _HBM]
        + [_VMEM] * 5,
        out_specs=tuple([_HBM] * 4),
        scratch_shapes=[
            pltpu.VMEM((N_CHIPS, D_MODEL, BLK), BF16),
            pltpu.VMEM((N_CHIPS, D_MODEL, BLK), BF16),
            pltpu.VMEM((N_CHIPS, D_MODEL, BLK), BF16),
            pltpu.VMEM((N_DEV, *small_blocks[1]), F32),
            pltpu.VMEM(small_blocks[2], F32),
            pltpu.SemaphoreType.DMA((N_CHIPS,)),
            pltpu.SemaphoreType.DMA((N_CHIPS,)),
            pltpu.SemaphoreType.DMA((N_CHIPS,)),
            pltpu.SemaphoreType.DMA((N_CHIPS,)),
            pltpu.SemaphoreType.DMA((3 * last,)),
            pltpu.SemaphoreType.DMA((3 * last,)),
            pltpu.SemaphoreType.DMA((4,)),
        ],
    )
    return pl.pallas_call(
        body,
        name="gw_in_scatter",
        grid_spec=grid_spec,
        out_shape=(jax.ShapeDtypeStruct((N_CHIPS, D_MODEL, BLK), BF16),)
        + tuple(jax.ShapeDtypeStruct((N_DEV, *b), dt) for b, dt in zip(small_blocks, (gw_out_parts.dtype, F32, F32), strict=True)),
        compiler_params=_params(),
    )(me_arr, hnT_tok, hnT_meta, dproj, gw_out_parts, dmeta_rows, acc_conv, acc_b2, acc_b1, acc_f2)


def _adamw(w, g, m, v):
    m = ADAM_B1 * m + (1.0 - ADAM_B1) * g
    v = ADAM_B2 * v + (1.0 - ADAM_B2) * (g * g)
    m_hat = m / (1.0 - ADAM_B1**ADAM_STEP)
    v_hat = v / (1.0 - ADAM_B2**ADAM_STEP)
    delta = -ADAM_LR * (m_hat / (jnp.sqrt(v_hat) + ADAM_EPS) + ADAM_WD * w)
    return delta, m, v


def _small_leaves(meta, conv_w, n1, rg, fg):
    return meta, conv_w, n1.reshape(8, CHUNK), rg.reshape(4, CHUNK), fg.reshape(8, CHUNK)


def _from_small_leaves(meta, conv_w, n1, rg, fg):
    return meta, conv_w, n1.reshape(D_MODEL), rg.reshape(D_RET), fg.reshape(D_MODEL)


ADAM_GRID = 4


def _adamw_all(land_in, land_out, land_sc, land_ga, big, small):
    n_leaf = 5
    lands = (land_in, land_out)
    n_big = len(big)

    def body(*refs):
        big_in, sc_ref, ga_ref = refs[: 4 * n_big], refs[4 * n_big], refs[4 * n_big + 1]
        small_in = refs[4 * n_big + 2 : 4 * n_big + 2 + 3 * n_leaf]
        big_out = refs[4 * n_big + 2 + 3 * n_leaf : 8 * n_big + 2 + 3 * n_leaf]
        small_out = refs[8 * n_big + 2 + 3 * n_leaf :]
        for b in range(n_big):
            p_ref, w_ref, m_ref, v_ref = big_in[4 * b : 4 * b + 4]
            g_ref, d_ref, nm_ref, nv_ref = big_out[4 * b : 4 * b + 4]
            g = p_ref[0].astype(F32)
            for s in range(1, p_ref.shape[0]):
                g = g + p_ref[s].astype(F32)
            g_ref[...] = g
            d_ref[...], nm_ref[...], nv_ref[...] = _adamw(w_ref[...], g, m_ref[...], v_ref[...])

        @pl.when(pl.program_id(0) == 0)
        def _():
            sc, ga = sc_ref[0], ga_ref[0]
            for s in range(1, N_DEV):
                sc = sc + sc_ref[s]
                ga = ga + ga_ref[s]
            grads = (
                sc[SC_META0 : SC_META0 + N_META],
                sc[SC_CONV0 : SC_CONV0 + 3, 0 : D_CONV // N_DEV],
                ga[GA_N1 : GA_N1 + 8],
                ga[GA_RG : GA_RG + 4],
                ga[GA_FG : GA_FG + 8],
            )
            for leaf, g in enumerate(grads):
                d, nm, nv = _adamw(
                    small_in[leaf][...], g, small_in[n_leaf + leaf][...], small_in[2 * n_leaf + leaf][...]
                )
                small_out[leaf][...] = g
                small_out[n_leaf + leaf][...] = d
                small_out[2 * n_leaf + leaf][...] = nm
                small_out[3 * n_leaf + leaf][...] = nv
            small_out[4 * n_leaf][...] = ga[GA_LOSS : GA_LOSS + 1]

    in_specs, out_specs, out_shape, operands = [], [], [], []
    for land, (w, m, v) in zip(lands, big, strict=True):
        n_r, n_c = w.shape
        tr = n_r // ADAM_GRID
        blk = pl.BlockSpec((tr, n_c), lambda i: (i, 0))
        in_specs += [pl.BlockSpec((land.shape[0], tr, n_c), lambda i: (0, i, 0)), blk, blk, blk]
        out_specs += [blk] * 4
        out_shape += [jax.ShapeDtypeStruct((n_r, n_c), F32)] * 4
        operands += [land, w, m, v]
    small_shapes = [jax.ShapeDtypeStruct(a.shape, F32) for a in small[0]] * 4 + [jax.ShapeDtypeStruct((1, CHUNK), F32)]
    out = pl.pallas_call(
        body,
        name="adamw_all",
        grid=(ADAM_GRID,),
        in_specs=in_specs + [_VMEM] * (2 + 3 * n_leaf),
        out_specs=tuple(out_specs + [_VMEM] * len(small_shapes)),
        out_shape=tuple(out_shape + small_shapes),
        compiler_params=_params(),
    )(*operands, land_sc, land_ga, *small[0], *small[1], *small[2])
    bigs = tuple(out[4 * b : 4 * b + 4] for b in range(n_big))
    rest = out[4 * n_big :]
    return bigs, tuple(rest[k * n_leaf : (k + 1) * n_leaf] for k in range(4)), rest[4 * n_leaf]


def kernel(x, meta, norm1_g, w_in, conv_w, ret_norm_g, w_out, final_g, loss_target, m_meta, m_norm1_g, m_w_in, m_conv_w, m_ret_norm_g, m_w_out, m_final_g, v_meta, v_norm1_g, v_w_in, v_conv_w, v_ret_norm_g, v_w_out, v_final_g):
    seq = x.shape[1]
    assert x.shape == (1, seq, D_MODEL) and seq % ROW_TILE == 0
    n_tiles = seq // ROW_TILE + 1
    x2, t2 = x[0], loss_target[0]

    small_w = _small_leaves(meta, conv_w, norm1_g, ret_norm_g, final_g)
    g1 = norm1_g.reshape(1, D_MODEL)
    hn_tok, hnT_tok, w3, meta_chunk, cw8 = _gather_weights_and_norm(x2, g1, w_in, meta, conv_w)

    rot = _rotary_tables(n_tiles, ROW_TILE)
    dec, _, zeta, xi = _decay_tables()
    gr = ret_norm_g.reshape(1, D_RET)
    gf = final_g.reshape(1, D_MODEL)

    hnT_meta, proj, conv_out, conv, o, states, wo3 = _f1(hn_tok, meta_chunk, g1, w3, cw8, rot, dec, zeta, xi, w_out)
    wout = wo3.reshape(D_MODEL, D_MODEL)
    dh2, dco, drg, do, gw_out, acc_f2, acc_b1 = _f2(proj, o, conv_out, x2, t2, wout, gr, gf)
    dproj_ret = _b1(do, drg, proj, states, _decay_tables(B1_CHUNK), zeta, rot)
    dproj, grad_x2, dmeta_rows, acc_b2, acc_conv = _b2(dproj_ret, dco, proj, conv, w3, cw8, x2, meta_chunk, dh2, g1)
    gw_out_parts = gw_out.reshape(N_DEV, D_MODEL // N_DEV, D_MODEL)

    me_arr = (4 * lax.axis_index("x") + 2 * lax.axis_index("y") + lax.axis_index("c")).astype(jnp.int32).reshape(1)
    land_in, land_out, land_sc, land_ga = _gw_in_scatter(
        hnT_tok, hnT_meta, dproj, gw_out_parts, dmeta_rows, acc_conv, acc_b2, acc_b1, acc_f2, me_arr
    )

    (w_in_out, w_out_out), small_out, loss_row = _adamw_all(
        land_in,
        land_out,
        land_sc,
        land_ga,
        ((w_in, m_w_in, v_w_in), (w_out, m_w_out, v_w_out)),
        (
            small_w,
            _small_leaves(m_meta, m_conv_w, m_norm1_g, m_ret_norm_g, m_final_g),
            _small_leaves(v_meta, v_conv_w, v_norm1_g, v_ret_norm_g, v_final_g),
        ),
    )
    g_w_in, d_w_in, nm_w_in, nv_w_in = w_in_out
    g_w_out, d_w_out, nm_w_out, nv_w_out = w_out_out
    loss = loss_row[0, 0]
    grad_x = grad_x2.reshape(1, seq, D_MODEL)

    def leaves(w_in_leaf, w_out_leaf, small_leaves):
        meta_leaf, conv_leaf, n1_leaf, rg_leaf, fg_leaf = _from_small_leaves(*small_leaves)
        return (meta_leaf, n1_leaf, w_in_leaf, conv_leaf, rg_leaf, w_out_leaf, fg_leaf)

    return (
        loss,
        grad_x,
        *leaves(g_w_in, g_w_out, small_out[0]),
        *leaves(d_w_in, d_w_out, small_out[1]),
        *leaves(nm_w_in, nm_w_out, small_out[2]),
        *leaves(nv_w_in, nv_w_out, small_out[3]),
    )
```

```python
import math

import jax
import jax.numpy as jnp
import numpy as np
from jax import lax
from jax.experimental import pallas as pl
from jax.experimental.pallas import tpu as pltpu

F32 = jnp.float32
BF16 = jnp.bfloat16

N_DEV = 8
N_CHIPS = 4
D_MODEL = 1024
N_META = 16
CHUNK = 128
D_CONV = 512
D_RET = 512
N_HEADS = 4
HEAD_DIM = 128
N_PROJ = 8
BLK = 512
ROPE_BASE = 10000.0
EPS = 1e-6
Q_SCALE = HEAD_DIM ** -0.5
LOG_G = tuple(math.log(1.0 - 2.0 ** (-5.0 - h)) for h in range(N_HEADS))
CHUNK_DECAY = tuple(math.exp(CHUNK * lg) for lg in LOG_G)

ADAM_LR = 0.001
ADAM_B1 = 0.9
ADAM_B2 = 0.999
ADAM_EPS = 1e-08
ADAM_WD = 0.01
ADAM_STEP = 10

ROW_TILE = 512
PAD_ROWS = ROW_TILE - N_META
LIVE0 = ROW_TILE - CHUNK
B1_CHUNK = 256
VMEM_LIMIT = 56 * 1024 * 1024

SC_META0, SC_CONV0 = 0, 16
GA_N1, GA_RG, GA_FG, GA_LOSS = 0, 8, 16, 24

NT = (((1,), (1,)), ((), ()))
TN = (((0,), (0,)), ((), ()))
MESH = pl.DeviceIdType.MESH

_VMEM = pl.BlockSpec(memory_space=pltpu.VMEM)
_HBM = pl.BlockSpec(memory_space=pltpu.HBM)


def _params(n_axes=1):
    return pltpu.CompilerParams(dimension_semantics=("arbitrary",) * n_axes, vmem_limit_bytes=VMEM_LIMIT)


def _sigmoid(x):
    return 0.5 * jnp.tanh(0.5 * x) + 0.5


def _decay_tables(chunk=CHUNK):
    idx = np.arange(chunk, dtype=np.float64)
    diff = idx[:, None] - idx[None, :]
    dec = np.stack([np.where(diff >= 0, np.exp(diff * lg), 0.0) for lg in LOG_G])
    zeta = np.stack([np.exp((chunk - 1 - idx) * lg) for lg in LOG_G])
    xi = np.stack([np.exp((idx + 1.0) * lg) for lg in LOG_G])
    ones = np.ones((1, 1, HEAD_DIM))
    return (
        jnp.asarray(dec, F32),
        jnp.asarray(dec.transpose(0, 2, 1), F32),
        jnp.asarray(zeta[:, :, None] * ones, F32),
        jnp.asarray(xi[:, :, None] * ones, F32),
    )


def _rotary_tables(n_tiles, tm):
    half = HEAD_DIM // 2
    freqs = (1.0 / (np.float32(ROPE_BASE) ** (np.arange(half, dtype=np.float32) / np.float32(half)))).astype(np.float64)
    sign = np.concatenate([-np.ones(half), np.ones(half)])
    two = lambda a: np.concatenate([a, a], axis=1)
    base = two((np.arange(n_tiles, dtype=np.float64) * tm - PAD_ROWS)[:, None] * freqs[None, :])
    off = two(np.arange(tm, dtype=np.float64)[:, None] * freqs[None, :])
    as32 = lambda a: jnp.asarray(a, F32)
    return as32(np.cos(base)), as32(np.sin(base) * sign), as32(np.cos(off)), as32(np.sin(off) * sign)


def _tile_rotary(ca_ref, sa_ref, cb_ref, sb_ref, tile):
    ca, sa = ca_ref[pl.ds(tile, 1), :], sa_ref[pl.ds(tile, 1), :]
    cb, sb = cb_ref[...], sb_ref[...]
    return ca * cb - sa * sb, sa * cb + ca * sb


def _rot(t, cos2, sin2):
    return t * cos2 + pltpu.roll(t, HEAD_DIM // 2, 1) * sin2


def _rot_bwd(d, cos2, sin2):
    return d * cos2 + pltpu.roll(d * sin2, HEAD_DIM // 2, 1)


def _token_tile(i):
    return jnp.maximum(i - 1, 0)


GATHER_PAIRS = 7


def _two_level_gather(src_ref, out_ref, send_sems, recv_sems, local_sem, base=0):
    x, y, c = lax.axis_index("x"), lax.axis_index("y"), lax.axis_index("c")
    me, sibling = (x, y, c), (x, y, 1 - c)
    xnb, ynb, diag = (1 - x, y), (x, 1 - y), (1 - x, 1 - y)
    relayed = (jnp.bitwise_xor(x, 1 - c), jnp.bitwise_xor(y, c))
    other = (jnp.bitwise_xor(x, c), jnp.bitwise_xor(y, 1 - c))

    def copy(k, block, to, src=None):
        dst = out_ref.at[4 * block[0] + 2 * block[1] + block[2]]
        return pltpu.make_async_remote_copy(
            src_ref=dst if src is None else src,
            dst_ref=dst,
            send_sem=send_sems.at[base + k],
            recv_sem=recv_sems.at[base + k],
            device_id=to,
            device_id_type=MESH,
        )

    mine = pltpu.make_async_copy(src_ref, out_ref.at[4 * x + 2 * y + c], local_sem)
    first = [copy(1, me, (*xnb, c), src=src_ref), copy(2, me, (*ynb, c), src=src_ref), copy(0, me, sibling, src=src_ref)]
    relay = copy(3, (*relayed, c), (*other, c))
    passed = [copy(4 + j, (*chip, c), sibling) for j, chip in enumerate((xnb, ynb, diag))]

    def start():
        mine.start()
        for cp in first:
            cp.start()

    def forward():
        copy(1 + c, (*relayed, c), me).wait_recv()
        relay.start()
        copy(2 - c, (*other, c), me).wait_recv()
        passed[0].start()
        passed[1].start()

    def forward_relayed():
        copy(3, (*diag, c), me).wait_recv()
        passed[2].start()

    def finish():
        copy(0, sibling, me).wait_recv()
        for j, chip in enumerate((xnb, ynb, diag)):
            copy(4 + j, (*chip, 1 - c), me).wait_recv()
        for cp in first + [relay] + passed:
            cp.wait_send()
        mine.wait()

    return start, forward, forward_relayed, finish


W_IN_CHUNKS = 4


def _gather_weights_and_norm(x2, g1, w_shard, meta, conv_w):
    per = GATHER_PAIRS
    n_ch = W_IN_CHUNKS
    rows = w_shard.shape[0] // n_ch
    tm = ROW_TILE
    nt = x2.shape[0] // tm
    cols = D_CONV // N_DEV
    pack_shape = (SC_CONV0 + 8, CHUNK)
    starts = [0] * (n_ch + 1)
    forwards = [min(5 + 2 * p, nt - 1) for p in range(n_ch)] + [min(12, nt - 1)]
    relayeds = [min(12 + p, nt - 1) for p in range(n_ch)] + [nt - 1]
    finishes = [nt - 1] * (n_ch + 1)

    def body(x_ref, g_ref, w_ref, meta_ref, conv_ref, hn_ref, hnT_ref, w_all_ref, mc_ref, cw_ref,
             w_bf, small_ref, small_all_ref, send_sems, recv_sems, local_sems):
        i = pl.program_id(0)

        @pl.when(i == 0)
        def _():
            w_bf[...] = w_ref[...].astype(BF16)
            small_ref[...] = jnp.zeros_like(small_ref)
            small_ref[SC_META0 : SC_META0 + N_META, :] = meta_ref[...]
            small_ref[SC_CONV0 : SC_CONV0 + 3, 0:cols] = conv_ref[...]

        parts = [
            _two_level_gather(
                w_bf.at[pl.ds(ch * rows, rows)],
                w_all_ref.at[:, pl.ds(ch * rows, rows)],
                send_sems,
                recv_sems,
                local_sems.at[ch],
                ch * per,
            )
            for ch in range(n_ch)
        ]
        parts.append(_two_level_gather(small_ref, small_all_ref, send_sems, recv_sems, local_sems.at[n_ch], n_ch * per))
        for phase, steps in enumerate((starts, forwards, relayeds, finishes)):
            for part, step in zip(parts, steps, strict=True):
                pl.when(i == step)(part[phase])

        h = x_ref[...]
        r = lax.rsqrt(jnp.mean(h * h, axis=-1, keepdims=True) + EPS)
        hn = (h * r * g_ref[...]).astype(BF16)
        hn_ref[...] = hn
        hnT_ref[...] = hn.T

        @pl.when(i == nt - 1)
        def _():
            mc_ref[0 : CHUNK - N_META, :] = jnp.zeros((CHUNK - N_META, D_MODEL), F32)
            cw_ref[...] = jnp.zeros_like(cw_ref)
            for d in range(N_DEV):
                mc_ref[CHUNK - N_META : CHUNK, d * CHUNK : (d + 1) * CHUNK] = small_all_ref[d, SC_META0 : SC_META0 + N_META, :]
                lane0 = (d * cols) % CHUNK
                rows8 = small_all_ref[d, SC_CONV0 : SC_CONV0 + 8, :]
                rows8 = pltpu.roll(rows8, lane0, 1) if lane0 else rows8
                cw_ref[0:3, d * cols : (d + 1) * cols] = rows8[0:3, lane0 : lane0 + cols]

    const = lambda r, c: pl.BlockSpec((r, c), lambda i: (0, 0))
    return pl.pallas_call(
        body,
        name="gather_weights_norm",
        grid=(nt,),
        out_shape=(
            jax.ShapeDtypeStruct(x2.shape, BF16),
            jax.ShapeDtypeStruct(x2.shape[::-1], BF16),
            jax.ShapeDtypeStruct((N_DEV, *w_shard.shape), BF16),
            jax.ShapeDtypeStruct((CHUNK, D_MODEL), F32),
            jax.ShapeDtypeStruct((8, D_CONV), F32),
        ),
        in_specs=[pl.BlockSpec((tm, D_MODEL), lambda i: (i, 0)), _VMEM, _VMEM, _VMEM, _VMEM],
        out_specs=(
            pl.BlockSpec((tm, D_MODEL), lambda i: (i, 0)),
            pl.BlockSpec((D_MODEL, tm), lambda i: (0, i)),
            _HBM,
            const(CHUNK, D_MODEL),
            const(8, D_CONV),
        ),
        scratch_shapes=[
            pltpu.VMEM(w_shard.shape, BF16),
            pltpu.VMEM(pack_shape, F32),
            pltpu.VMEM((N_DEV, *pack_shape), F32),
            pltpu.SemaphoreType.DMA(((n_ch + 1) * per,)),
            pltpu.SemaphoreType.DMA(((n_ch + 1) * per,)),
            pltpu.SemaphoreType.DMA((n_ch + 1,)),
        ],
        compiler_params=_params(),
    )(x2, g1, w_shard, meta, conv_w)


def _f1(hn_tok, meta_chunk, g1, w3, cw8, rot, dec, zeta, xi, wo_shard):
    tm = ROW_TILE
    nt = hn_tok.shape[0] // tm + 1
    n_rows = nt * tm
    nct = tm // CHUNK

    def body(hn_ref, mc_ref, g_ref, w_ref, cw_ref, ca_ref, sa_ref, cb_ref, sb_ref, dec_ref, zeta_ref, xi_ref, wo_ref,
             hnT_ref, pr_ref, co_ref, cv_ref, o_ref, st_ref, wo_all_ref,
             halo, state, wo_bf, send_sems, recv_sems, local_sem):
        i = pl.program_id(0)

        @pl.when(i == 0)
        def _():
            wo_bf[...] = wo_ref[...].astype(BF16)

        wo_phases = _two_level_gather(wo_bf, wo_all_ref, send_sems, recv_sems, local_sem)
        for step, phase in zip((0, nt // 3, 2 * nt // 3, nt - 1), wo_phases, strict=True):
            pl.when(i == step)(phase)

        def work(hn, r0):
            rs = slice(r0, tm)
            n = tm - r0

            def proj(j):
                return jnp.dot(hn, w_ref[j], preferred_element_type=F32)

            cos_all, sin_all = _tile_rotary(ca_ref, sa_ref, cb_ref, sb_ref, i)
            cos_t, sin_t = cos_all[rs, :], sin_all[rs, :]
            q, k, v = proj(4), proj(5), proj(6).astype(BF16)
            pr_ref[rs, 6 * BLK : 7 * BLK] = v
            chunks = range(r0 // CHUNK, tm // CHUNK)
            heads = []
            for hd in range(N_HEADS):
                cs = slice(hd * HEAD_DIM, (hd + 1) * HEAD_DIM)
                qh = (_rot(q[:, cs], cos_t, sin_t) * Q_SCALE).astype(BF16)
                kh = _rot(k[:, cs], cos_t, sin_t).astype(BF16)
                vh = v[:, cs]
                pr_ref[rs, 4 * BLK + hd * HEAD_DIM : 4 * BLK + (hd + 1) * HEAD_DIM] = qh
                pr_ref[rs, 5 * BLK + hd * HEAD_DIM : 5 * BLK + (hd + 1) * HEAD_DIM] = kh
                upd = {}
                for c in chunks:
                    cr = slice(c * CHUNK - r0, (c + 1) * CHUNK - r0)
                    kz = (kh[cr].astype(F32) * zeta_ref[hd]).astype(BF16)
                    upd[c] = lax.dot_general(kz, vh[cr], TN, preferred_element_type=F32)
                st = state[hd]
                outs = []
                for c in chunks:
                    cr = slice(c * CHUNK - r0, (c + 1) * CHUNK - r0)
                    qc, kc, vc = qh[cr], kh[cr], vh[cr]
                    st_bf = st.astype(BF16)
                    st_ref[c, hd] = st_bf
                    s = lax.dot_general(qc, kc, NT, preferred_element_type=F32) * dec_ref[hd]
                    inner = jnp.dot(s.astype(BF16), vc, preferred_element_type=F32)
                    qx = (qc.astype(F32) * xi_ref[hd]).astype(BF16)
                    outs.append(inner + jnp.dot(qx, st_bf, preferred_element_type=F32))
                    st = CHUNK_DECAY[hd] * st + upd[c]
                state[hd] = st
                heads.append(jnp.concatenate(outs, axis=0) if len(outs) > 1 else outs[0])
            o_ref[rs, :] = jnp.concatenate(heads, axis=1)

            cx, cb, cc, cg = proj(0), proj(1), proj(2), proj(3)
            u = cc * cx
            rows = lax.broadcasted_iota(jnp.int32, u.shape, 0)
            hl = halo[...]
            u1 = jnp.where(rows == 0, hl[7:8], pltpu.roll(u, 1, 0))
            u2 = jnp.where(rows == 0, hl[6:7], jnp.where(rows == 1, hl[7:8], pltpu.roll(u, 2, 0)))
            halo[...] = u[n - 8 : n]
            cw = cw_ref[...]
            conv = cw[0:1] * u2 + cw[1:2] * u1 + cw[2:3] * u
            co = (cb * conv * (cg * _sigmoid(cg))).astype(BF16)
            co_ref[rs, :] = co
            cv_ref[rs, :] = conv.astype(BF16)
            pr_ref[rs, 0 * BLK : 1 * BLK] = cx.astype(BF16)
            pr_ref[rs, 1 * BLK : 2 * BLK] = cb.astype(BF16)
            pr_ref[rs, 2 * BLK : 3 * BLK] = cc.astype(BF16)
            pr_ref[rs, 3 * BLK : 4 * BLK] = cg.astype(BF16)
            pr_ref[rs, 7 * BLK : 8 * BLK] = proj(7).astype(BF16)

        @pl.when(i == 0)
        def _():
            halo[...] = jnp.zeros_like(halo)
            state[...] = jnp.zeros_like(state)
            h = mc_ref[...]
            r = lax.rsqrt(jnp.mean(h * h, axis=-1, keepdims=True) + EPS)
            hn = (h * r * g_ref[...]).astype(BF16)
            hnT_ref[...] = hn.T
            work(hn, LIVE0)

        @pl.when(i > 0)
        def _():
            work(hn_ref[...], 0)

    row = lambda w: pl.BlockSpec((tm, w), lambda i: (i, 0))
    return pl.pallas_call(
        body,
        name="f1_inproj_conv",
        grid=(nt,),
        in_specs=[pl.BlockSpec((tm, D_MODEL), lambda i: (_token_tile(i), 0))] + [_VMEM] * 12,
        out_specs=(
            pl.BlockSpec((D_MODEL, CHUNK), lambda i: (0, 0)),
            row(N_PROJ * BLK),
            row(D_CONV),
            row(D_CONV),
            row(D_RET),
            pl.BlockSpec((nct, N_HEADS, HEAD_DIM, HEAD_DIM), lambda i: (i, 0, 0, 0)),
            _HBM,
        ),
        out_shape=(
            jax.ShapeDtypeStruct((D_MODEL, CHUNK), BF16),
            jax.ShapeDtypeStruct((n_rows, N_PROJ * BLK), BF16),
            jax.ShapeDtypeStruct((n_rows, D_CONV), BF16),
            jax.ShapeDtypeStruct((n_rows, D_CONV), BF16),
            jax.ShapeDtypeStruct((n_rows, D_RET), F32),
            jax.ShapeDtypeStruct((n_rows // CHUNK, N_HEADS, HEAD_DIM, HEAD_DIM), BF16),
            jax.ShapeDtypeStruct((N_DEV, *wo_shard.shape), BF16),
        ),
        scratch_shapes=[
            pltpu.VMEM((8, D_CONV), F32),
            pltpu.VMEM((N_HEADS, HEAD_DIM, HEAD_DIM), F32),
            pltpu.VMEM(wo_shard.shape, BF16),
            pltpu.SemaphoreType.DMA((GATHER_PAIRS,)),
            pltpu.SemaphoreType.DMA((GATHER_PAIRS,)),
            pltpu.SemaphoreType.DMA(()),
        ],
        compiler_params=_params(),
    )(hn_tok, meta_chunk, g1, w3, cw8, *rot, dec, zeta, xi, wo_shard)


def _group_norm(o):
    ys, rs = [], []
    for hd in range(N_HEADS):
        oh = o[:, hd * HEAD_DIM : (hd + 1) * HEAD_DIM]
        xc = oh - jnp.mean(oh, axis=-1, keepdims=True)
        rstd = lax.rsqrt(jnp.mean(xc * xc, axis=-1, keepdims=True) + EPS)
        ys.append(xc * rstd)
        rs.append(jnp.broadcast_to(rstd, oh.shape))
    return jnp.concatenate(ys, axis=1), jnp.concatenate(rs, axis=1)


def _f2(proj, o, conv_out, x2, t2, wout, gr, gf):
    n_rows = proj.shape[0]
    tm = ROW_TILE
    nt = n_rows // tm

    def body(o_ref, rg_ref, co_ref, x_ref, t_ref, wo_ref, gr_ref, gf_ref,
             dh2_ref, dco_ref, drg_ref, do_ref, gw_ref, acc_ref, accr_ref, gacc):
        i = pl.program_id(0)

        @pl.when(i == 0)
        def _():
            acc_ref[...] = jnp.zeros_like(acc_ref)
            accr_ref[...] = jnp.zeros_like(accr_ref)
            gacc[...] = jnp.zeros_like(gacc)
            dh2_ref[...] = jnp.zeros_like(dh2_ref)
            dco_ref[...] = jnp.zeros_like(dco_ref)
            drg_ref[...] = jnp.zeros_like(drg_ref)
            do_ref[...] = jnp.zeros_like(do_ref)

        @pl.when(i > 0)
        def _():
            yh, rstd = _group_norm(o_ref[...])
            rg = rg_ref[...].astype(F32)
            sg = _sigmoid(rg)
            silu = rg * sg
            grv = gr_ref[...]
            ro = (yh * grv * silu).astype(BF16)
            h2 = (
                x_ref[...]
                + jnp.dot(co_ref[...], wo_ref[0:D_CONV], preferred_element_type=F32)
                + jnp.dot(ro, wo_ref[D_CONV:], preferred_element_type=F32)
            )
            r2 = lax.rsqrt(jnp.mean(h2 * h2, axis=-1, keepdims=True) + EPS)
            yn = h2 * r2
            gfv = gf_ref[...]
            err = yn * gfv - t_ref[...]
            tile_loss = jnp.sum(jnp.sum(err * err, axis=-1, keepdims=True), axis=0, keepdims=True) * (0.5 / D_MODEL)
            acc_ref[0:1, :] += jnp.sum(err * yn, axis=0, keepdims=True) * (1.0 / D_MODEL)
            acc_ref[1:2, :] += tile_loss
            dyn = err * (gfv * (1.0 / D_MODEL))
            dh2 = r2 * (dyn - yn * jnp.mean(dyn * yn, axis=-1, keepdims=True))
            dh2_ref[...] = dh2
            dh2_bf = dh2.astype(BF16)
            dmix = lax.dot_general(dh2_bf, wo_ref[...], NT, preferred_element_type=F32)
            dco_ref[...] = dmix[:, :D_CONV].astype(BF16)
            dro = dmix[:, D_CONV:]
            dsilu = sg * (1.0 + rg * (1.0 - sg))
            drg_ref[...] = (dro * (yh * grv) * dsilu).astype(BF16)
            dret = dro * silu
            accr_ref[0:1, :] += jnp.sum(dret * yh, axis=0, keepdims=True)
            dyh = dret * grv
            for hd in range(N_HEADS):
                cs = slice(hd * HEAD_DIM, (hd + 1) * HEAD_DIM)
                a, b = dyh[:, cs], yh[:, cs]
                do_ref[:, cs] = (
                    rstd[:, cs]
                    * (a - jnp.mean(a, axis=-1, keepdims=True) - b * jnp.mean(a * b, axis=-1, keepdims=True))
                ).astype(BF16)
            gacc[0:D_CONV, :] += lax.dot_general(co_ref[...], dh2_bf, TN, preferred_element_type=F32)
            gacc[D_CONV:, :] += lax.dot_general(ro, dh2_bf, TN, preferred_element_type=F32)

        @pl.when(i == nt - 1)
        def _():
            gw_ref[...] = gacc[...].astype(BF16)

    row = lambda w, j=0: pl.BlockSpec((tm, w), lambda i: (i, j))
    tok = pl.BlockSpec((tm, D_MODEL), lambda i: (_token_tile(i), 0))
    return pl.pallas_call(
        body,
        name="f2_out_loss",
        grid=(nt,),
        in_specs=[row(D_RET), row(BLK, 7), row(D_CONV), tok, tok] + [_VMEM] * 3,
        out_specs=(
            row(D_MODEL),
            row(D_CONV),
            row(BLK),
            row(D_RET),
            pl.BlockSpec((D_MODEL, D_MODEL), lambda i: (0, 0)),
            pl.BlockSpec((8, D_MODEL), lambda i: (0, 0)),
            pl.BlockSpec((8, D_RET), lambda i: (0, 0)),
        ),
        out_shape=(
            jax.ShapeDtypeStruct((n_rows, D_MODEL), F32),
            jax.ShapeDtypeStruct((n_rows, D_CONV), BF16),
            jax.ShapeDtypeStruct((n_rows, BLK), BF16),
            jax.ShapeDtypeStruct((n_rows, D_RET), BF16),
            jax.ShapeDtypeStruct((D_MODEL, D_MODEL), BF16),
            jax.ShapeDtypeStruct((8, D_MODEL), F32),
            jax.ShapeDtypeStruct((8, D_RET), F32),
        ),
        scratch_shapes=[pltpu.VMEM((D_MODEL, D_MODEL), F32)],
        compiler_params=_params(),
    )(o, proj, conv_out, x2, t2, wout, gr, gf)


def _b2(drg, dco, do, proj, conv, states, w3, cw8, x2, meta_chunk, dh2, g1, tables, zeta0, rot):
    n_rows = drg.shape[0]
    tm = ROW_TILE
    nt = n_rows // tm
    last = N_PROJ - 1
    chunk = B1_CHUNK
    nct = tm // chunk
    decay = tuple(math.exp(chunk * lg) for lg in LOG_G)
    Q0, K0, V0 = 4 * BLK, 5 * BLK, 6 * BLK

    def body(drg_ref, dco_ref, do_ref, pr_ref, cv_ref, st_ref, w_ref, cw_ref, x_ref, mc_ref, dh2_ref, g_ref,
             dec_ref, dect_ref, zeta_ref, xi_ref, zeta0_ref, ca_ref, sa_ref, cb_ref, sb_ref,
             dpc_ref, gx_ref, dm_ref, acc_ref, accc_ref, halo, dstate):
        i = pl.program_id(0)
        tile = nt - 1 - i
        cos_all, sin_all = _tile_rotary(ca_ref, sa_ref, cb_ref, sb_ref, tile)

        @pl.when(i == 0)
        def _():
            acc_ref[...] = jnp.zeros_like(acc_ref)
            accc_ref[...] = jnp.zeros_like(accc_ref)
            halo[...] = jnp.zeros_like(halo)
            dstate[...] = jnp.zeros_like(dstate)

        def retention_backward():
            for hd in range(N_HEADS):
                cs = slice(hd * HEAD_DIM, (hd + 1) * HEAD_DIM)
                qs, ks, vs = (slice(c0 + hd * HEAD_DIM, c0 + (hd + 1) * HEAD_DIM) for c0 in (Q0, K0, V0))
                dupd = {}
                for c in range(nct):
                    rs = slice(c * chunk, (c + 1) * chunk)
                    qx = (pr_ref[rs, qs].astype(F32) * xi_ref[hd]).astype(BF16)
                    dupd[c] = lax.dot_general(qx, do_ref[rs, cs], TN, preferred_element_type=F32)
                dst = dstate[hd]
                for c in reversed(range(nct)):
                    rs = slice(c * chunk, (c + 1) * chunk)
                    cos_t, sin_t = cos_all[rs, :], sin_all[rs, :]
                    q, k, v, do = pr_ref[rs, qs], pr_ref[rs, ks], pr_ref[rs, vs], do_ref[rs, cs]
                    st_bf = st_ref[c * (chunk // CHUNK), hd]
                    dst_bf = dst.astype(BF16)
                    zt, xt = zeta_ref[hd], xi_ref[hd]
                    sT = (lax.dot_general(k, q, NT, preferred_element_type=F32) * dect_ref[hd]).astype(BF16)
                    dsT = (lax.dot_general(v, do, NT, preferred_element_type=F32) * dect_ref[hd]).astype(BF16)
                    ds = (lax.dot_general(do, v, NT, preferred_element_type=F32) * dec_ref[hd]).astype(BF16)
                    kz = (k.astype(F32) * zt).astype(BF16)
                    dv = jnp.dot(sT, do, preferred_element_type=F32) + jnp.dot(kz, dst_bf, preferred_element_type=F32)
                    dq = jnp.dot(ds, k, preferred_element_type=F32) + xt * lax.dot_general(
                        do, st_bf, NT, preferred_element_type=F32
                    )
                    dk = jnp.dot(dsT, q, preferred_element_type=F32) + zt * lax.dot_general(
                        v, dst_bf, NT, preferred_element_type=F32
                    )
                    dst = decay[hd] * dst + dupd[c]
                    dpc_ref[rs, qs] = _rot_bwd(dq * Q_SCALE, cos_t, sin_t).astype(BF16)
                    dpc_ref[rs, ks] = _rot_bwd(dk, cos_t, sin_t).astype(BF16)
                    dpc_ref[rs, vs] = dv.astype(BF16)
                dstate[hd] = dst

        def state_backward(rs):
            cos_t, sin_t = cos_all[rs, :], sin_all[rs, :]
            for hd in range(N_HEADS):
                qs, ks, vs = (slice(c0 + hd * HEAD_DIM, c0 + (hd + 1) * HEAD_DIM) for c0 in (Q0, K0, V0))
                k, v = pr_ref[rs, ks], pr_ref[rs, vs]
                dst_bf = dstate[hd].astype(BF16)
                zt = zeta0_ref[hd]
                kz = (k.astype(F32) * zt).astype(BF16)
                dv = jnp.dot(kz, dst_bf, preferred_element_type=F32)
                dk = zt * lax.dot_general(v, dst_bf, NT, preferred_element_type=F32)
                dpc_ref[rs, qs] = jnp.zeros((tm - LIVE0, HEAD_DIM), BF16)
                dpc_ref[rs, ks] = _rot_bwd(dk, cos_t, sin_t).astype(BF16)
                dpc_ref[rs, vs] = dv.astype(BF16)

        def work(h, r0, others):
            rs = slice(r0, tm)
            n = tm - r0
            dco = dco_ref[rs, :].astype(F32)
            cx = pr_ref[rs, 0 * BLK : 1 * BLK].astype(F32)
            cb = pr_ref[rs, 1 * BLK : 2 * BLK].astype(F32)
            cc = pr_ref[rs, 2 * BLK : 3 * BLK].astype(F32)
            cg = pr_ref[rs, 3 * BLK : 4 * BLK].astype(F32)
            conv = cv_ref[rs, :].astype(F32)
            sg = _sigmoid(cg)
            silu, dsilu = cg * sg, sg * (1.0 + cg * (1.0 - sg))
            t = dco * cb
            dcb = (dco * conv * silu).astype(BF16)
            dcg = (t * conv * dsilu).astype(BF16)
            dconv = t * silu
            rows = lax.broadcasted_iota(jnp.int32, dconv.shape, 0)
            hl = halo[...]
            dc1 = jnp.where(rows == n - 1, hl[0:1], pltpu.roll(dconv, n - 1, 0))
            dc2 = jnp.where(rows == n - 2, hl[0:1], jnp.where(rows == n - 1, hl[1:2], pltpu.roll(dconv, n - 2, 0)))
            halo[...] = dconv[0:8]
            cw = cw_ref[...]
            du = cw[2:3] * dconv + cw[1:2] * dc1 + cw[0:1] * dc2
            u = cc * cx
            accc_ref[0:1, :] += jnp.sum(u * dc2, axis=0, keepdims=True)
            accc_ref[1:2, :] += jnp.sum(u * dc1, axis=0, keepdims=True)
            accc_ref[2:3, :] += jnp.sum(u * dconv, axis=0, keepdims=True)
            dcx = (du * cc).astype(BF16)
            dcc = (du * cx).astype(BF16)
            dhn = None
            for blk, ref, c0 in others:
                d = lax.dot_general(ref[rs, c0 : c0 + BLK], w_ref[blk], NT, preferred_element_type=F32)
                dhn = d if dhn is None else dhn + d
            for blk, d in ((1, dcb), (3, dcg), (0, dcx), (2, dcc)):
                dpc_ref[rs, blk * BLK : (blk + 1) * BLK] = d
                dhn += lax.dot_general(d, w_ref[blk], NT, preferred_element_type=F32)
            r = lax.rsqrt(jnp.mean(h * h, axis=-1, keepdims=True) + EPS)
            hh = h * r
            acc_ref[0:1, :] += jnp.sum(dhn * hh, axis=0, keepdims=True)
            dg = dhn * g_ref[...]
            return r * (dg - hh * jnp.mean(dg * hh, axis=-1, keepdims=True))

        @pl.when(tile == 0)
        def _():
            state_backward(slice(LIVE0, tm))
            dpc_ref[LIVE0:tm, last * BLK :] = jnp.zeros((tm - LIVE0, BLK), BF16)
            dh = work(mc_ref[...], LIVE0, ((5, dpc_ref, K0), (6, dpc_ref, V0)))
            dm_ref[...] = dh[CHUNK - N_META : CHUNK]

        @pl.when(tile > 0)
        def _():
            retention_backward()
            dpc_ref[:, last * BLK :] = drg_ref[...]
            others = ((last, drg_ref, 0), (4, dpc_ref, Q0), (5, dpc_ref, K0), (6, dpc_ref, V0))
            gx_ref[...] = dh2_ref[...] + work(x_ref[...], 0, others)

    seven = last * BLK
    full = N_PROJ * BLK

    rev = lambda w, j=0: pl.BlockSpec((tm, w), lambda i: (nt - 1 - i, j))
    tok = pl.BlockSpec((tm, D_MODEL), lambda i: (_token_tile(nt - 1 - i), 0))
    const = lambda r, c: pl.BlockSpec((r, c), lambda i: (0, 0))
    return pl.pallas_call(
        body,
        name="b2_dconv_dh",
        grid=(nt,),
        in_specs=[
            rev(BLK),
            rev(D_CONV),
            rev(D_RET),
            rev(seven),
            rev(D_CONV),
            pl.BlockSpec((tm // CHUNK, N_HEADS, HEAD_DIM, HEAD_DIM), lambda i: (nt - 1 - i, 0, 0, 0)),
            _VMEM,
            _VMEM,
            tok,
            _VMEM,
            rev(D_MODEL),
        ]
        + [_VMEM] * 10,
        out_specs=(rev(full), tok, const(N_META, D_MODEL), const(8, D_MODEL), const(8, D_CONV)),
        out_shape=(
            jax.ShapeDtypeStruct((n_rows, full), BF16),
            jax.ShapeDtypeStruct(x2.shape, F32),
            jax.ShapeDtypeStruct((N_META, D_MODEL), F32),
            jax.ShapeDtypeStruct((8, D_MODEL), F32),
            jax.ShapeDtypeStruct((8, D_CONV), F32),
        ),
        scratch_shapes=[pltpu.VMEM((8, D_CONV), F32), pltpu.VMEM((N_HEADS, HEAD_DIM, HEAD_DIM), F32)],
        compiler_params=_params(),
    )(drg, dco, do, proj, conv, states, w3, cw8, x2, meta_chunk, dh2, g1, *tables, zeta0, *rot)


def _gw_in_scatter(hnT_tok, hnT_meta, dproj, gw_out_parts, dmeta_rows, acc_conv, acc_b2, acc_ret, acc_f2, me_arr):
    n_rows = dproj.shape[0]
    last = N_DEV - 1
    cols = D_CONV // N_DEV
    by_dest = (True, True, False)
    small_blocks = (gw_out_parts.shape[1:], (SC_CONV0 + 8, CHUNK), (GA_LOSS + 8, CHUNK))

    def body(me_ref, a_ref, am_ref, b_ref, go_ref, dm_ref, ac_ref, ab2_ref, aret_ref, af2_ref,
             land_in, land_go, land_sc, land_ga, d2d_buf, d2d_land, ici_buf, sc_ref, ga_ref,
             d2d_send, d2d_recv, ici_send, ici_recv, send_sems, recv_sems, local_sems):
        del me_ref
        s = pl.program_id(0)
        t = last - s
        q = t >> 1
        x, y, c = lax.axis_index("x"), lax.axis_index("y"), lax.axis_index("c")
        me = 4 * x + 2 * y + c
        chip = 2 * x + y
        srcs, lands = (go_ref, sc_ref, ga_ref), (land_go, land_sc, land_ga)

        def peer_at(k):
            return (1 - x if k & 4 else x, 1 - y if k & 2 else y, 1 - c if k & 1 else c)

        def small_copy(a, k):
            px, py, pc = peer_at(k)
            return pltpu.make_async_remote_copy(
                src_ref=srcs[a].at[4 * px + 2 * py + pc] if by_dest[a] else srcs[a],
                dst_ref=lands[a].at[me],
                send_sem=send_sems.at[a * last + k - 1],
                recv_sem=recv_sems.at[a * last + k - 1],
                device_id=(px, py, pc),
                device_id_type=MESH,
            )

        def small_local(a):
            return pltpu.make_async_copy(srcs[a].at[me] if by_dest[a] else srcs[a], lands[a].at[me], local_sems.at[a + 1])

        def d2d_copy(j):
            return pltpu.make_async_remote_copy(
                src_ref=d2d_buf.at[j],
                dst_ref=d2d_land.at[j],
                send_sem=d2d_send.at[j],
                recv_sem=d2d_recv.at[j],
                device_id=(x, y, 1 - c),
                device_id_type=MESH,
            )

        def ici_copy(j, to):
            return pltpu.make_async_remote_copy(
                src_ref=ici_buf.at[j],
                dst_ref=land_in.at[chip],
                send_sem=ici_send.at[j],
                recv_sem=ici_recv.at[j],
                device_id=to,
                device_id_type=MESH,
            )

        def own_copy():
            return pltpu.make_async_copy(ici_buf.at[0], land_in.at[chip], local_sems.at[0])

        @pl.when(s == 0)
        def _():
            sc_ref[...] = jnp.zeros_like(sc_ref)
            for d in range(N_DEV):
                sc_ref[d, SC_META0 : SC_META0 + N_META, :] = dm_ref[:, d * CHUNK : (d + 1) * CHUNK]
                lane0 = (d * cols) % CHUNK
                rows8 = ac_ref[:, d * cols - lane0 : d * cols - lane0 + CHUNK]
                rows8 = pltpu.roll(rows8, CHUNK - lane0, 1) if lane0 else rows8
                sc_ref[d, SC_CONV0 : SC_CONV0 + 3, 0:cols] = rows8[0:3, 0:cols]
            ga_ref[...] = jnp.zeros_like(ga_ref)
            for row0, acc, n in ((GA_N1, ab2_ref, 8), (GA_RG, aret_ref, 4), (GA_FG, af2_ref, 8)):
                for j in range(n):
                    ga_ref[row0 + j : row0 + j + 1, :] = acc[0:1, j * CHUNK : (j + 1) * CHUNK]
            ga_ref[GA_LOSS : GA_LOSS + 1, :] = af2_ref[1:2, 0:CHUNK]
            for a in range(3):
                small_local(a).start()
            for k in range(last, 0, -1):
                for a in range(3):
                    small_copy(a, k).start()

        blk = jnp.dot(a_ref[...], b_ref[ROW_TILE:, :], preferred_element_type=F32)
        blk += jnp.dot(am_ref[...], b_ref[LIVE0:ROW_TILE, :], preferred_element_type=F32)

        @pl.when((t & 1) == 1)
        def _():
            d2d_buf[q] = blk.astype(BF16)
            d2d_copy(q).start()

        @pl.when((t & 1) == 0)
        def _():
            d2d_copy(q).wait_recv()
            ici_buf[q] = (blk + d2d_land[q].astype(F32)).astype(BF16)

            @pl.when(t != 0)
            def _():
                ici_copy(q, (jnp.bitwise_xor(x, (t >> 2) & 1), jnp.bitwise_xor(y, (t >> 1) & 1), c)).start()

            @pl.when(t == 0)
            def _():
                own_copy().start()
                for j in range(N_CHIPS):
                    d2d_copy(j).wait_send()
                for j in range(N_CHIPS - 1, 0, -1):
                    ici_copy(j, peer_at(2 * j)).wait()
                for k in range(last, 0, -1):
                    for a in range(3):
                        small_copy(a, k).wait()
                for a in range(3):
                    small_local(a).wait()
                own_copy().wait()

    grid_spec = pltpu.PrefetchScalarGridSpec(
        num_scalar_prefetch=1,
        grid=(N_DEV,),
        in_specs=[_VMEM, _VMEM, pl.BlockSpec((n_rows, BLK), lambda s, me: (0, jnp.bitwise_xor(me[0], last - s))), _HBM]
        + [_VMEM] * 5,
        out_specs=tuple([_HBM] * 4),
        scratch_shapes=[
            pltpu.VMEM((N_CHIPS, D_MODEL, BLK), BF16),
            pltpu.VMEM((N_CHIPS, D_MODEL, BLK), BF16),
            pltpu.VMEM((N_CHIPS, D_MODEL, BLK), BF16),
            pltpu.VMEM((N_DEV, *small_blocks[1]), F32),
            pltpu.VMEM(small_blocks[2], F32),
            pltpu.SemaphoreType.DMA((N_CHIPS,)),
            pltpu.SemaphoreType.DMA((N_CHIPS,)),
            pltpu.SemaphoreType.DMA((N_CHIPS,)),
            pltpu.SemaphoreType.DMA((N_CHIPS,)),
            pltpu.SemaphoreType.DMA((3 * last,)),
            pltpu.SemaphoreType.DMA((3 * last,)),
            pltpu.SemaphoreType.DMA((4,)),
        ],
    )
    return pl.pallas_call(
        body,
        name="gw_in_scatter",
        grid_spec=grid_spec,
        out_shape=(jax.ShapeDtypeStruct((N_CHIPS, D_MODEL, BLK), BF16),)
        + tuple(jax.ShapeDtypeStruct((N_DEV, *b), dt) for b, dt in zip(small_blocks, (gw_out_parts.dtype, F32, F32), strict=True)),
        compiler_params=_params(),
    )(me_arr, hnT_tok, hnT_meta, dproj, gw_out_parts, dmeta_rows, acc_conv, acc_b2, acc_ret, acc_f2)


def _adamw(w, g, m, v):
    m = ADAM_B1 * m + (1.0 - ADAM_B1) * g
    v = ADAM_B2 * v + (1.0 - ADAM_B2) * (g * g)
    m_hat = m / (1.0 - ADAM_B1**ADAM_STEP)
    v_hat = v / (1.0 - ADAM_B2**ADAM_STEP)
    delta = -ADAM_LR * (m_hat / (jnp.sqrt(v_hat) + ADAM_EPS) + ADAM_WD * w)
    return delta, m, v


def _small_leaves(meta, conv_w, n1, rg, fg):
    return meta, conv_w, n1.reshape(8, CHUNK), rg.reshape(4, CHUNK), fg.reshape(8, CHUNK)


def _from_small_leaves(meta, conv_w, n1, rg, fg):
    return meta, conv_w, n1.reshape(D_MODEL), rg.reshape(D_RET), fg.reshape(D_MODEL)


ADAM_GRID = 4


def _adamw_all(land_in, land_out, land_sc, land_ga, big, small):
    n_leaf = 5
    lands = (land_in, land_out)
    n_big = len(big)

    def body(*refs):
        big_in, sc_ref, ga_ref = refs[: 4 * n_big], refs[4 * n_big], refs[4 * n_big + 1]
        small_in = refs[4 * n_big + 2 : 4 * n_big + 2 + 3 * n_leaf]
        big_out = refs[4 * n_big + 2 + 3 * n_leaf : 8 * n_big + 2 + 3 * n_leaf]
        small_out = refs[8 * n_big + 2 + 3 * n_leaf :]
        for b in range(n_big):
            p_ref, w_ref, m_ref, v_ref = big_in[4 * b : 4 * b + 4]
            g_ref, d_ref, nm_ref, nv_ref = big_out[4 * b : 4 * b + 4]
            g = p_ref[0].astype(F32)
            for s in range(1, p_ref.shape[0]):
                g = g + p_ref[s].astype(F32)
            g_ref[...] = g
            d_ref[...], nm_ref[...], nv_ref[...] = _adamw(w_ref[...], g, m_ref[...], v_ref[...])

        @pl.when(pl.program_id(0) == 0)
        def _():
            sc, ga = sc_ref[0], ga_ref[0]
            for s in range(1, N_DEV):
                sc = sc + sc_ref[s]
                ga = ga + ga_ref[s]
            grads = (
                sc[SC_META0 : SC_META0 + N_META],
                sc[SC_CONV0 : SC_CONV0 + 3, 0 : D_CONV // N_DEV],
                ga[GA_N1 : GA_N1 + 8],
                ga[GA_RG : GA_RG + 4],
                ga[GA_FG : GA_FG + 8],
            )
            for leaf, g in enumerate(grads):
                d, nm, nv = _adamw(
                    small_in[leaf][...], g, small_in[n_leaf + leaf][...], small_in[2 * n_leaf + leaf][...]
                )
                small_out[leaf][...] = g
                small_out[n_leaf + leaf][...] = d
                small_out[2 * n_leaf + leaf][...] = nm
                small_out[3 * n_leaf + leaf][...] = nv
            small_out[4 * n_leaf][...] = ga[GA_LOSS : GA_LOSS + 1]

    in_specs, out_specs, out_shape, operands = [], [], [], []
    for land, (w, m, v) in zip(lands, big, strict=True):
        n_r, n_c = w.shape
        tr = n_r // ADAM_GRID
        blk = pl.BlockSpec((tr, n_c), lambda i: (i, 0))
        in_specs += [pl.BlockSpec((land.shape[0], tr, n_c), lambda i: (0, i, 0)), blk, blk, blk]
        out_specs += [blk] * 4
        out_shape += [jax.ShapeDtypeStruct((n_r, n_c), F32)] * 4
        operands += [land, w, m, v]
    small_shapes = [jax.ShapeDtypeStruct(a.shape, F32) for a in small[0]] * 4 + [jax.ShapeDtypeStruct((1, CHUNK), F32)]
    out = pl.pallas_call(
        body,
        name="adamw_all",
        grid=(ADAM_GRID,),
        in_specs=in_specs + [_VMEM] * (2 + 3 * n_leaf),
        out_specs=tuple(out_specs + [_VMEM] * len(small_shapes)),
        out_shape=tuple(out_shape + small_shapes),
        compiler_params=_params(),
    )(*operands, land_sc, land_ga, *small[0], *small[1], *small[2])
    bigs = tuple(out[4 * b : 4 * b + 4] for b in range(n_big))
    rest = out[4 * n_big :]
    return bigs, tuple(rest[k * n_leaf : (k + 1) * n_leaf] for k in range(4)), rest[4 * n_leaf]


def kernel(x, meta, norm1_g, w_in, conv_w, ret_norm_g, w_out, final_g, loss_target, m_meta, m_norm1_g, m_w_in, m_conv_w, m_ret_norm_g, m_w_out, m_final_g, v_meta, v_norm1_g, v_w_in, v_conv_w, v_ret_norm_g, v_w_out, v_final_g):
    seq = x.shape[1]
    assert x.shape == (1, seq, D_MODEL) and seq % ROW_TILE == 0
    n_tiles = seq // ROW_TILE + 1
    x2, t2 = x[0], loss_target[0]

    small_w = _small_leaves(meta, conv_w, norm1_g, ret_norm_g, final_g)
    g1 = norm1_g.reshape(1, D_MODEL)
    hn_tok, hnT_tok, w3, meta_chunk, cw8 = _gather_weights_and_norm(x2, g1, w_in, meta, conv_w)

    rot = _rotary_tables(n_tiles, ROW_TILE)
    dec, _, zeta, xi = _decay_tables()
    gr = ret_norm_g.reshape(1, D_RET)
    gf = final_g.reshape(1, D_MODEL)

    hnT_meta, proj, conv_out, conv, o, states, wo3 = _f1(hn_tok, meta_chunk, g1, w3, cw8, rot, dec, zeta, xi, w_out)
    wout = wo3.reshape(D_MODEL, D_MODEL)
    dh2, dco, drg, do, gw_out, acc_f2, acc_ret = _f2(proj, o, conv_out, x2, t2, wout, gr, gf)
    dproj, grad_x2, dmeta_rows, acc_b2, acc_conv = _b2(
        drg, dco, do, proj, conv, states, w3, cw8, x2, meta_chunk, dh2, g1, _decay_tables(B1_CHUNK), zeta, rot
    )
    gw_out_parts = gw_out.reshape(N_DEV, D_MODEL // N_DEV, D_MODEL)

    me_arr = (4 * lax.axis_index("x") + 2 * lax.axis_index("y") + lax.axis_index("c")).astype(jnp.int32).reshape(1)
    land_in, land_out, land_sc, land_ga = _gw_in_scatter(
        hnT_tok, hnT_meta, dproj, gw_out_parts, dmeta_rows, acc_conv, acc_b2, acc_ret, acc_f2, me_arr
    )

    (w_in_out, w_out_out), small_out, loss_row = _adamw_all(
        land_in,
        land_out,
        land_sc,
        land_ga,
        ((w_in, m_w_in, v_w_in), (w_out, m_w_out, v_w_out)),
        (
            small_w,
            _small_leaves(m_meta, m_conv_w, m_norm1_g, m_ret_norm_g, m_final_g),
            _small_leaves(v_meta, v_conv_w, v_norm1_g, v_ret_norm_g, v_final_g),
        ),
    )
    g_w_in, d_w_in, nm_w_in, nv_w_in = w_in_out
    g_w_out, d_w_out, nm_w_out, nv_w_out = w_out_out
    loss = loss_row[0, 0]
    grad_x = grad_x2.reshape(1, seq, D_MODEL)

    def leaves(w_in_leaf, w_out_leaf, small_leaves):
        meta_leaf, conv_leaf, n1_leaf, rg_leaf, fg_leaf = _from_small_leaves(*small_leaves)
        return (meta_leaf, n1_leaf, w_in_leaf, conv_leaf, rg_leaf, w_out_leaf, fg_leaf)

    return (
        loss,
        grad_x,
        *leaves(g_w_in, g_w_out, small_out[0]),
        *leaves(d_w_in, d_w_out, small_out[1]),
        *leaves(nm_w_in, nm_w_out, small_out[2]),
        *leaves(nv_w_in, nv_w_out, small_out[3]),
    )
```

```python
import math

import jax
import jax.numpy as jnp
import numpy as np
from jax import lax
from jax.experimental import pallas as pl
from jax.experimental.pallas import tpu as pltpu

F32 = jnp.float32
BF16 = jnp.bfloat16

N_DEV = 8
N_CHIPS = 4
D_MODEL = 1024
N_META = 16
CHUNK = 128
D_CONV = 512
D_RET = 512
N_HEADS = 4
HEAD_DIM = 128
N_PROJ = 8
BLK = 512
ROPE_BASE = 10000.0
EPS = 1e-6
Q_SCALE = HEAD_DIM ** -0.5
LOG_G = tuple(math.log(1.0 - 2.0 ** (-5.0 - h)) for h in range(N_HEADS))
CHUNK_DECAY = tuple(math.exp(CHUNK * lg) for lg in LOG_G)

ADAM_LR = 0.001
ADAM_B1 = 0.9
ADAM_B2 = 0.999
ADAM_EPS = 1e-08
ADAM_WD = 0.01
ADAM_STEP = 10

ROW_TILE = 512
PAD_ROWS = ROW_TILE - N_META
LIVE0 = ROW_TILE - CHUNK
B1_CHUNK = 256
B2_ROWS = ROW_TILE // 2
VMEM_LIMIT = 56 * 1024 * 1024

SC_META0, SC_CONV0 = 0, 16
GA_N1, GA_RG, GA_FG, GA_LOSS = 0, 8, 16, 24

NT = (((1,), (1,)), ((), ()))
TN = (((0,), (0,)), ((), ()))
MESH = pl.DeviceIdType.MESH

_VMEM = pl.BlockSpec(memory_space=pltpu.VMEM)
_HBM = pl.BlockSpec(memory_space=pltpu.HBM)


def _params(n_axes=1):
    return pltpu.CompilerParams(dimension_semantics=("arbitrary",) * n_axes, vmem_limit_bytes=VMEM_LIMIT)


def _sigmoid(x):
    return 0.5 * jnp.tanh(0.5 * x) + 0.5


def _decay_tables(chunk=CHUNK):
    idx = np.arange(chunk, dtype=np.float64)
    diff = idx[:, None] - idx[None, :]
    dec = np.stack([np.where(diff >= 0, np.exp(diff * lg), 0.0) for lg in LOG_G])
    zeta = np.stack([np.exp((chunk - 1 - idx) * lg) for lg in LOG_G])
    xi = np.stack([np.exp((idx + 1.0) * lg) for lg in LOG_G])
    ones = np.ones((1, 1, HEAD_DIM))
    return (
        jnp.asarray(dec, F32),
        jnp.asarray(dec.transpose(0, 2, 1), F32),
        jnp.asarray(zeta[:, :, None] * ones, F32),
        jnp.asarray(xi[:, :, None] * ones, F32),
    )


def _rotary_tables(n_tiles, tm):
    half = HEAD_DIM // 2
    freqs = (1.0 / (np.float32(ROPE_BASE) ** (np.arange(half, dtype=np.float32) / np.float32(half)))).astype(np.float64)
    sign = np.concatenate([-np.ones(half), np.ones(half)])
    two = lambda a: np.concatenate([a, a], axis=1)
    base = two((np.arange(n_tiles, dtype=np.float64) * tm - PAD_ROWS)[:, None] * freqs[None, :])
    off = two(np.arange(tm, dtype=np.float64)[:, None] * freqs[None, :])
    as32 = lambda a: jnp.asarray(a, F32)
    return as32(np.cos(base)), as32(np.sin(base) * sign), as32(np.cos(off)), as32(np.sin(off) * sign)


def _tile_rotary(ca_ref, sa_ref, cb_ref, sb_ref, tile):
    ca, sa = ca_ref[pl.ds(tile, 1), :], sa_ref[pl.ds(tile, 1), :]
    cb, sb = cb_ref[...], sb_ref[...]
    return ca * cb - sa * sb, sa * cb + ca * sb


def _rot(t, cos2, sin2):
    return t * cos2 + pltpu.roll(t, HEAD_DIM // 2, 1) * sin2


def _rot_bwd(d, cos2, sin2):
    return d * cos2 + pltpu.roll(d * sin2, HEAD_DIM // 2, 1)


def _token_tile(i):
    return jnp.maximum(i - 1, 0)


GATHER_PAIRS = 7


def _two_level_gather(src_ref, out_ref, send_sems, recv_sems, local_sem, base=0):
    x, y, c = lax.axis_index("x"), lax.axis_index("y"), lax.axis_index("c")
    me, sibling = (x, y, c), (x, y, 1 - c)
    xnb, ynb, diag = (1 - x, y), (x, 1 - y), (1 - x, 1 - y)
    relayed = (jnp.bitwise_xor(x, 1 - c), jnp.bitwise_xor(y, c))
    other = (jnp.bitwise_xor(x, c), jnp.bitwise_xor(y, 1 - c))

    def copy(k, block, to, src=None):
        dst = out_ref.at[4 * block[0] + 2 * block[1] + block[2]]
        return pltpu.make_async_remote_copy(
            src_ref=dst if src is None else src,
            dst_ref=dst,
            send_sem=send_sems.at[base + k],
            recv_sem=recv_sems.at[base + k],
            device_id=to,
            device_id_type=MESH,
        )

    mine = pltpu.make_async_copy(src_ref, out_ref.at[4 * x + 2 * y + c], local_sem)
    first = [copy(1, me, (*xnb, c), src=src_ref), copy(2, me, (*ynb, c), src=src_ref), copy(0, me, sibling, src=src_ref)]
    relay = copy(3, (*relayed, c), (*other, c))
    passed = [copy(4 + j, (*chip, c), sibling) for j, chip in enumerate((xnb, ynb, diag))]

    def start():
        mine.start()
        for cp in first:
            cp.start()

    def forward():
        copy(1 + c, (*relayed, c), me).wait_recv()
        relay.start()
        copy(2 - c, (*other, c), me).wait_recv()
        passed[0].start()
        passed[1].start()

    def forward_relayed():
        copy(3, (*diag, c), me).wait_recv()
        passed[2].start()

    def finish():
        copy(0, sibling, me).wait_recv()
        for j, chip in enumerate((xnb, ynb, diag)):
            copy(4 + j, (*chip, 1 - c), me).wait_recv()
        for cp in first + [relay] + passed:
            cp.wait_send()
        mine.wait()

    return start, forward, forward_relayed, finish


W_IN_CHUNKS = 4


def _gather_weights_and_norm(x2, g1, w_shard, meta, conv_w):
    per = GATHER_PAIRS
    n_ch = W_IN_CHUNKS
    rows = w_shard.shape[0] // n_ch
    tm = ROW_TILE
    nt = x2.shape[0] // tm
    cols = D_CONV // N_DEV
    pack_shape = (SC_CONV0 + 8, CHUNK)
    starts = [0] * (n_ch + 1)
    forwards = [min(5 + 2 * p, nt - 1) for p in range(n_ch)] + [min(12, nt - 1)]
    relayeds = [min(12 + p, nt - 1) for p in range(n_ch)] + [nt - 1]
    finishes = [nt - 1] * (n_ch + 1)

    def body(x_ref, g_ref, w_ref, meta_ref, conv_ref, hn_ref, hnT_ref, w_all_ref, mc_ref, cw_ref,
             w_bf, small_ref, small_all_ref, send_sems, recv_sems, local_sems):
        i = pl.program_id(0)

        @pl.when(i == 0)
        def _():
            w_bf[...] = w_ref[...].astype(BF16)
            small_ref[...] = jnp.zeros_like(small_ref)
            small_ref[SC_META0 : SC_META0 + N_META, :] = meta_ref[...]
            small_ref[SC_CONV0 : SC_CONV0 + 3, 0:cols] = conv_ref[...]

        parts = [
            _two_level_gather(
                w_bf.at[pl.ds(ch * rows, rows)],
                w_all_ref.at[:, pl.ds(ch * rows, rows)],
                send_sems,
                recv_sems,
                local_sems.at[ch],
                ch * per,
            )
            for ch in range(n_ch)
        ]
        parts.append(_two_level_gather(small_ref, small_all_ref, send_sems, recv_sems, local_sems.at[n_ch], n_ch * per))
        for phase, steps in enumerate((starts, forwards, relayeds, finishes)):
            for part, step in zip(parts, steps, strict=True):
                pl.when(i == step)(part[phase])

        h = x_ref[...]
        r = lax.rsqrt(jnp.mean(h * h, axis=-1, keepdims=True) + EPS)
        hn = (h * r * g_ref[...]).astype(BF16)
        hn_ref[...] = hn
        hnT_ref[...] = hn.T

        @pl.when(i == nt - 1)
        def _():
            mc_ref[0 : CHUNK - N_META, :] = jnp.zeros((CHUNK - N_META, D_MODEL), F32)
            cw_ref[...] = jnp.zeros_like(cw_ref)
            for d in range(N_DEV):
                mc_ref[CHUNK - N_META : CHUNK, d * CHUNK : (d + 1) * CHUNK] = small_all_ref[d, SC_META0 : SC_META0 + N_META, :]
                lane0 = (d * cols) % CHUNK
                rows8 = small_all_ref[d, SC_CONV0 : SC_CONV0 + 8, :]
                rows8 = pltpu.roll(rows8, lane0, 1) if lane0 else rows8
                cw_ref[0:3, d * cols : (d + 1) * cols] = rows8[0:3, lane0 : lane0 + cols]

    const = lambda r, c: pl.BlockSpec((r, c), lambda i: (0, 0))
    return pl.pallas_call(
        body,
        name="gather_weights_norm",
        grid=(nt,),
        out_shape=(
            jax.ShapeDtypeStruct(x2.shape, BF16),
            jax.ShapeDtypeStruct(x2.shape[::-1], BF16),
            jax.ShapeDtypeStruct((N_DEV, *w_shard.shape), BF16),
            jax.ShapeDtypeStruct((CHUNK, D_MODEL), F32),
            jax.ShapeDtypeStruct((8, D_CONV), F32),
        ),
        in_specs=[pl.BlockSpec((tm, D_MODEL), lambda i: (i, 0)), _VMEM, _VMEM, _VMEM, _VMEM],
        out_specs=(
            pl.BlockSpec((tm, D_MODEL), lambda i: (i, 0)),
            pl.BlockSpec((D_MODEL, tm), lambda i: (0, i)),
            _HBM,
            const(CHUNK, D_MODEL),
            const(8, D_CONV),
        ),
        scratch_shapes=[
            pltpu.VMEM(w_shard.shape, BF16),
            pltpu.VMEM(pack_shape, F32),
            pltpu.VMEM((N_DEV, *pack_shape), F32),
            pltpu.SemaphoreType.DMA(((n_ch + 1) * per,)),
            pltpu.SemaphoreType.DMA(((n_ch + 1) * per,)),
            pltpu.SemaphoreType.DMA((n_ch + 1,)),
        ],
        compiler_params=_params(),
    )(x2, g1, w_shard, meta, conv_w)


def _f1(hn_tok, meta_chunk, g1, w3, cw8, rot, dec, zeta, xi, wo_shard):
    tm = ROW_TILE
    nt = hn_tok.shape[0] // tm + 1
    n_rows = nt * tm
    nct = tm // CHUNK

    def body(hn_ref, mc_ref, g_ref, w_ref, cw_ref, ca_ref, sa_ref, cb_ref, sb_ref, dec_ref, zeta_ref, xi_ref, wo_ref,
             hnT_ref, pr_ref, co_ref, cv_ref, o_ref, st_ref, wo_all_ref,
             halo, state, wo_bf, send_sems, recv_sems, local_sem):
        i = pl.program_id(0)

        @pl.when(i == 0)
        def _():
            wo_bf[...] = wo_ref[...].astype(BF16)

        wo_phases = _two_level_gather(wo_bf, wo_all_ref, send_sems, recv_sems, local_sem)
        for step, phase in zip((0, nt // 3, 2 * nt // 3, nt - 1), wo_phases, strict=True):
            pl.when(i == step)(phase)

        def work(hn, r0):
            rs = slice(r0, tm)
            n = tm - r0

            def proj(j):
                return jnp.dot(hn, w_ref[j], preferred_element_type=F32)

            cos_all, sin_all = _tile_rotary(ca_ref, sa_ref, cb_ref, sb_ref, i)
            cos_t, sin_t = cos_all[rs, :], sin_all[rs, :]
            q, k, v = proj(4), proj(5), proj(6).astype(BF16)
            pr_ref[rs, 6 * BLK : 7 * BLK] = v
            chunks = range(r0 // CHUNK, tm // CHUNK)
            heads = []
            for hd in range(N_HEADS):
                cs = slice(hd * HEAD_DIM, (hd + 1) * HEAD_DIM)
                qh = (_rot(q[:, cs], cos_t, sin_t) * Q_SCALE).astype(BF16)
                kh = _rot(k[:, cs], cos_t, sin_t).astype(BF16)
                vh = v[:, cs]
                pr_ref[rs, 4 * BLK + hd * HEAD_DIM : 4 * BLK + (hd + 1) * HEAD_DIM] = qh
                pr_ref[rs, 5 * BLK + hd * HEAD_DIM : 5 * BLK + (hd + 1) * HEAD_DIM] = kh
                upd = {}
                for c in chunks:
                    cr = slice(c * CHUNK - r0, (c + 1) * CHUNK - r0)
                    kz = (kh[cr].astype(F32) * zeta_ref[hd]).astype(BF16)
                    upd[c] = lax.dot_general(kz, vh[cr], TN, preferred_element_type=F32)
                st = state[hd]
                outs = []
                for c in chunks:
                    cr = slice(c * CHUNK - r0, (c + 1) * CHUNK - r0)
                    qc, kc, vc = qh[cr], kh[cr], vh[cr]
                    st_bf = st.astype(BF16)
                    st_ref[c, hd] = st_bf
                    s = lax.dot_general(qc, kc, NT, preferred_element_type=F32) * dec_ref[hd]
                    inner = jnp.dot(s.astype(BF16), vc, preferred_element_type=F32)
                    qx = (qc.astype(F32) * xi_ref[hd]).astype(BF16)
                    outs.append(inner + jnp.dot(qx, st_bf, preferred_element_type=F32))
                    st = CHUNK_DECAY[hd] * st + upd[c]
                state[hd] = st
                heads.append(jnp.concatenate(outs, axis=0) if len(outs) > 1 else outs[0])
            o_ref[rs, :] = jnp.concatenate(heads, axis=1)

            cx, cb, cc, cg = proj(0), proj(1), proj(2), proj(3)
            u = cc * cx
            rows = lax.broadcasted_iota(jnp.int32, u.shape, 0)
            hl = halo[...]
            u1 = jnp.where(rows == 0, hl[7:8], pltpu.roll(u, 1, 0))
            u2 = jnp.where(rows == 0, hl[6:7], jnp.where(rows == 1, hl[7:8], pltpu.roll(u, 2, 0)))
            halo[...] = u[n - 8 : n]
            cw = cw_ref[...]
            conv = cw[0:1] * u2 + cw[1:2] * u1 + cw[2:3] * u
            co = (cb * conv * (cg * _sigmoid(cg))).astype(BF16)
            co_ref[rs, :] = co
            cv_ref[rs, :] = conv.astype(BF16)
            pr_ref[rs, 0 * BLK : 1 * BLK] = cx.astype(BF16)
            pr_ref[rs, 1 * BLK : 2 * BLK] = cb.astype(BF16)
            pr_ref[rs, 2 * BLK : 3 * BLK] = cc.astype(BF16)
            pr_ref[rs, 3 * BLK : 4 * BLK] = cg.astype(BF16)
            pr_ref[rs, 7 * BLK : 8 * BLK] = proj(7).astype(BF16)

        @pl.when(i == 0)
        def _():
            halo[...] = jnp.zeros_like(halo)
            state[...] = jnp.zeros_like(state)
            h = mc_ref[...]
            r = lax.rsqrt(jnp.mean(h * h, axis=-1, keepdims=True) + EPS)
            hn = (h * r * g_ref[...]).astype(BF16)
            hnT_ref[...] = hn.T
            work(hn, LIVE0)

        @pl.when(i > 0)
        def _():
            work(hn_ref[...], 0)

    row = lambda w: pl.BlockSpec((tm, w), lambda i: (i, 0))
    return pl.pallas_call(
        body,
        name="f1_inproj_conv",
        grid=(nt,),
        in_specs=[pl.BlockSpec((tm, D_MODEL), lambda i: (_token_tile(i), 0))] + [_VMEM] * 12,
        out_specs=(
            pl.BlockSpec((D_MODEL, CHUNK), lambda i: (0, 0)),
            row(N_PROJ * BLK),
            row(D_CONV),
            row(D_CONV),
            row(D_RET),
            pl.BlockSpec((nct, N_HEADS, HEAD_DIM, HEAD_DIM), lambda i: (i, 0, 0, 0)),
            _HBM,
        ),
        out_shape=(
            jax.ShapeDtypeStruct((D_MODEL, CHUNK), BF16),
            jax.ShapeDtypeStruct((n_rows, N_PROJ * BLK), BF16),
            jax.ShapeDtypeStruct((n_rows, D_CONV), BF16),
            jax.ShapeDtypeStruct((n_rows, D_CONV), BF16),
            jax.ShapeDtypeStruct((n_rows, D_RET), F32),
            jax.ShapeDtypeStruct((n_rows // CHUNK, N_HEADS, HEAD_DIM, HEAD_DIM), BF16),
            jax.ShapeDtypeStruct((N_DEV, *wo_shard.shape), BF16),
        ),
        scratch_shapes=[
            pltpu.VMEM((8, D_CONV), F32),
            pltpu.VMEM((N_HEADS, HEAD_DIM, HEAD_DIM), F32),
            pltpu.VMEM(wo_shard.shape, BF16),
            pltpu.SemaphoreType.DMA((GATHER_PAIRS,)),
            pltpu.SemaphoreType.DMA((GATHER_PAIRS,)),
            pltpu.SemaphoreType.DMA(()),
        ],
        compiler_params=_params(),
    )(hn_tok, meta_chunk, g1, w3, cw8, *rot, dec, zeta, xi, wo_shard)


def _group_norm(o):
    ys, rs = [], []
    for hd in range(N_HEADS):
        oh = o[:, hd * HEAD_DIM : (hd + 1) * HEAD_DIM]
        xc = oh - jnp.mean(oh, axis=-1, keepdims=True)
        rstd = lax.rsqrt(jnp.mean(xc * xc, axis=-1, keepdims=True) + EPS)
        ys.append(xc * rstd)
        rs.append(jnp.broadcast_to(rstd, oh.shape))
    return jnp.concatenate(ys, axis=1), jnp.concatenate(rs, axis=1)


def _f2(proj, o, conv_out, x2, t2, wout, gr, gf):
    n_rows = proj.shape[0]
    tm = ROW_TILE
    nt = n_rows // tm

    def body(o_ref, rg_ref, co_ref, x_ref, t_ref, wo_ref, gr_ref, gf_ref,
             dh2_ref, dco_ref, drg_ref, do_ref, gw_ref, acc_ref, accr_ref, gacc):
        i = pl.program_id(0)

        @pl.when(i == 0)
        def _():
            acc_ref[...] = jnp.zeros_like(acc_ref)
            accr_ref[...] = jnp.zeros_like(accr_ref)
            gacc[...] = jnp.zeros_like(gacc)
            dh2_ref[...] = jnp.zeros_like(dh2_ref)
            dco_ref[...] = jnp.zeros_like(dco_ref)
            drg_ref[...] = jnp.zeros_like(drg_ref)
            do_ref[...] = jnp.zeros_like(do_ref)

        @pl.when(i > 0)
        def _():
            yh, rstd = _group_norm(o_ref[...])
            rg = rg_ref[...].astype(F32)
            sg = _sigmoid(rg)
            silu = rg * sg
            grv = gr_ref[...]
            ro = (yh * grv * silu).astype(BF16)
            h2 = (
                x_ref[...]
                + jnp.dot(co_ref[...], wo_ref[0:D_CONV], preferred_element_type=F32)
                + jnp.dot(ro, wo_ref[D_CONV:], preferred_element_type=F32)
            )
            r2 = lax.rsqrt(jnp.mean(h2 * h2, axis=-1, keepdims=True) + EPS)
            yn = h2 * r2
            gfv = gf_ref[...]
            err = yn * gfv - t_ref[...]
            tile_loss = jnp.sum(jnp.sum(err * err, axis=-1, keepdims=True), axis=0, keepdims=True) * (0.5 / D_MODEL)
            acc_ref[0:1, :] += jnp.sum(err * yn, axis=0, keepdims=True) * (1.0 / D_MODEL)
            acc_ref[1:2, :] += tile_loss
            dyn = err * (gfv * (1.0 / D_MODEL))
            dh2 = r2 * (dyn - yn * jnp.mean(dyn * yn, axis=-1, keepdims=True))
            dh2_ref[...] = dh2
            dh2_bf = dh2.astype(BF16)
            dmix = lax.dot_general(dh2_bf, wo_ref[...], NT, preferred_element_type=F32)
            dco_ref[...] = dmix[:, :D_CONV].astype(BF16)
            dro = dmix[:, D_CONV:]
            dsilu = sg * (1.0 + rg * (1.0 - sg))
            drg_ref[...] = (dro * (yh * grv) * dsilu).astype(BF16)
            dret = dro * silu
            accr_ref[0:1, :] += jnp.sum(dret * yh, axis=0, keepdims=True)
            dyh = dret * grv
            for hd in range(N_HEADS):
                cs = slice(hd * HEAD_DIM, (hd + 1) * HEAD_DIM)
                a, b = dyh[:, cs], yh[:, cs]
                do_ref[:, cs] = (
                    rstd[:, cs]
                    * (a - jnp.mean(a, axis=-1, keepdims=True) - b * jnp.mean(a * b, axis=-1, keepdims=True))
                ).astype(BF16)
            gacc[0:D_CONV, :] += lax.dot_general(co_ref[...], dh2_bf, TN, preferred_element_type=F32)
            gacc[D_CONV:, :] += lax.dot_general(ro, dh2_bf, TN, preferred_element_type=F32)

        @pl.when(i == nt - 1)
        def _():
            gw_ref[...] = gacc[...].astype(BF16)

    row = lambda w, j=0: pl.BlockSpec((tm, w), lambda i: (i, j))
    tok = pl.BlockSpec((tm, D_MODEL), lambda i: (_token_tile(i), 0))
    return pl.pallas_call(
        body,
        name="f2_out_loss",
        grid=(nt,),
        in_specs=[row(D_RET), row(BLK, 7), row(D_CONV), tok, tok] + [_VMEM] * 3,
        out_specs=(
            row(D_MODEL),
            row(D_CONV),
            row(BLK),
            row(D_RET),
            pl.BlockSpec((D_MODEL, D_MODEL), lambda i: (0, 0)),
            pl.BlockSpec((8, D_MODEL), lambda i: (0, 0)),
            pl.BlockSpec((8, D_RET), lambda i: (0, 0)),
        ),
        out_shape=(
            jax.ShapeDtypeStruct((n_rows, D_MODEL), F32),
            jax.ShapeDtypeStruct((n_rows, D_CONV), BF16),
            jax.ShapeDtypeStruct((n_rows, BLK), BF16),
            jax.ShapeDtypeStruct((n_rows, D_RET), BF16),
            jax.ShapeDtypeStruct((D_MODEL, D_MODEL), BF16),
            jax.ShapeDtypeStruct((8, D_MODEL), F32),
            jax.ShapeDtypeStruct((8, D_RET), F32),
        ),
        scratch_shapes=[pltpu.VMEM((D_MODEL, D_MODEL), F32)],
        compiler_params=_params(),
    )(o, proj, conv_out, x2, t2, wout, gr, gf)


def _b2(drg, dco, do, proj, conv, states, w3, cw8, x2, meta_chunk, dh2, g1, tables, zeta0, rot):
    n_rows = drg.shape[0]
    tm = ROW_TILE
    nt = n_rows // tm
    last = N_PROJ - 1
    chunk = B1_CHUNK
    nct = tm // chunk
    decay = tuple(math.exp(chunk * lg) for lg in LOG_G)
    Q0, K0, V0 = 4 * BLK, 5 * BLK, 6 * BLK

    def body(drg_ref, dco_ref, do_ref, pr_ref, cv_ref, st_ref, w_ref, cw_ref, x_ref, mc_ref, dh2_ref, g_ref,
             dec_ref, dect_ref, zeta_ref, xi_ref, zeta0_ref, ca_ref, sa_ref, cb_ref, sb_ref,
             dpc_ref, gx_ref, dm_ref, acc_ref, accc_ref, halo, dstate):
        i = pl.program_id(0)
        tile = nt - 1 - i
        cos_all, sin_all = _tile_rotary(ca_ref, sa_ref, cb_ref, sb_ref, tile)

        @pl.when(i == 0)
        def _():
            acc_ref[...] = jnp.zeros_like(acc_ref)
            accc_ref[...] = jnp.zeros_like(accc_ref)
            halo[...] = jnp.zeros_like(halo)
            dstate[...] = jnp.zeros_like(dstate)

        def retention_backward():
            for hd in range(N_HEADS):
                cs = slice(hd * HEAD_DIM, (hd + 1) * HEAD_DIM)
                qs, ks, vs = (slice(c0 + hd * HEAD_DIM, c0 + (hd + 1) * HEAD_DIM) for c0 in (Q0, K0, V0))
                dupd = {}
                for c in range(nct):
                    rs = slice(c * chunk, (c + 1) * chunk)
                    qx = (pr_ref[rs, qs].astype(F32) * xi_ref[hd]).astype(BF16)
                    dupd[c] = lax.dot_general(qx, do_ref[rs, cs], TN, preferred_element_type=F32)
                dst = dstate[hd]
                for c in reversed(range(nct)):
                    rs = slice(c * chunk, (c + 1) * chunk)
                    cos_t, sin_t = cos_all[rs, :], sin_all[rs, :]
                    q, k, v, do = pr_ref[rs, qs], pr_ref[rs, ks], pr_ref[rs, vs], do_ref[rs, cs]
                    st_bf = st_ref[c * (chunk // CHUNK), hd]
                    dst_bf = dst.astype(BF16)
                    zt, xt = zeta_ref[hd], xi_ref[hd]
                    sT = (lax.dot_general(k, q, NT, preferred_element_type=F32) * dect_ref[hd]).astype(BF16)
                    dsT = (lax.dot_general(v, do, NT, preferred_element_type=F32) * dect_ref[hd]).astype(BF16)
                    ds = (lax.dot_general(do, v, NT, preferred_element_type=F32) * dec_ref[hd]).astype(BF16)
                    kz = (k.astype(F32) * zt).astype(BF16)
                    dv = jnp.dot(sT, do, preferred_element_type=F32) + jnp.dot(kz, dst_bf, preferred_element_type=F32)
                    dq = jnp.dot(ds, k, preferred_element_type=F32) + xt * lax.dot_general(
                        do, st_bf, NT, preferred_element_type=F32
                    )
                    dk = jnp.dot(dsT, q, preferred_element_type=F32) + zt * lax.dot_general(
                        v, dst_bf, NT, preferred_element_type=F32
                    )
                    dst = decay[hd] * dst + dupd[c]
                    dpc_ref[rs, qs] = _rot_bwd(dq * Q_SCALE, cos_t, sin_t).astype(BF16)
                    dpc_ref[rs, ks] = _rot_bwd(dk, cos_t, sin_t).astype(BF16)
                    dpc_ref[rs, vs] = dv.astype(BF16)
                dstate[hd] = dst

        def state_backward(rs):
            cos_t, sin_t = cos_all[rs, :], sin_all[rs, :]
            for hd in range(N_HEADS):
                qs, ks, vs = (slice(c0 + hd * HEAD_DIM, c0 + (hd + 1) * HEAD_DIM) for c0 in (Q0, K0, V0))
                k, v = pr_ref[rs, ks], pr_ref[rs, vs]
                dst_bf = dstate[hd].astype(BF16)
                zt = zeta0_ref[hd]
                kz = (k.astype(F32) * zt).astype(BF16)
                dv = jnp.dot(kz, dst_bf, preferred_element_type=F32)
                dk = zt * lax.dot_general(v, dst_bf, NT, preferred_element_type=F32)
                dpc_ref[rs, qs] = jnp.zeros((tm - LIVE0, HEAD_DIM), BF16)
                dpc_ref[rs, ks] = _rot_bwd(dk, cos_t, sin_t).astype(BF16)
                dpc_ref[rs, vs] = dv.astype(BF16)

        def work(h, r0, others, r1=tm):
            rs = slice(r0, r1)
            n = r1 - r0
            dco = dco_ref[rs, :].astype(F32)
            cx = pr_ref[rs, 0 * BLK : 1 * BLK].astype(F32)
            cb = pr_ref[rs, 1 * BLK : 2 * BLK].astype(F32)
            cc = pr_ref[rs, 2 * BLK : 3 * BLK].astype(F32)
            cg = pr_ref[rs, 3 * BLK : 4 * BLK].astype(F32)
            conv = cv_ref[rs, :].astype(F32)
            sg = _sigmoid(cg)
            silu, dsilu = cg * sg, sg * (1.0 + cg * (1.0 - sg))
            t = dco * cb
            dcb = (dco * conv * silu).astype(BF16)
            dcg = (t * conv * dsilu).astype(BF16)
            dconv = t * silu
            rows = lax.broadcasted_iota(jnp.int32, dconv.shape, 0)
            hl = halo[...]
            dc1 = jnp.where(rows == n - 1, hl[0:1], pltpu.roll(dconv, n - 1, 0))
            dc2 = jnp.where(rows == n - 2, hl[0:1], jnp.where(rows == n - 1, hl[1:2], pltpu.roll(dconv, n - 2, 0)))
            halo[...] = dconv[0:8]
            cw = cw_ref[...]
            du = cw[2:3] * dconv + cw[1:2] * dc1 + cw[0:1] * dc2
            u = cc * cx
            accc_ref[0:1, :] += jnp.sum(u * dc2, axis=0, keepdims=True)
            accc_ref[1:2, :] += jnp.sum(u * dc1, axis=0, keepdims=True)
            accc_ref[2:3, :] += jnp.sum(u * dconv, axis=0, keepdims=True)
            dcx = (du * cc).astype(BF16)
            dcc = (du * cx).astype(BF16)
            dhn = None
            for blk, ref, c0 in others:
                d = lax.dot_general(ref[rs, c0 : c0 + BLK], w_ref[blk], NT, preferred_element_type=F32)
                dhn = d if dhn is None else dhn + d
            for blk, d in ((1, dcb), (3, dcg), (0, dcx), (2, dcc)):
                dpc_ref[rs, blk * BLK : (blk + 1) * BLK] = d
                dhn += lax.dot_general(d, w_ref[blk], NT, preferred_element_type=F32)
            r = lax.rsqrt(jnp.mean(h * h, axis=-1, keepdims=True) + EPS)
            hh = h * r
            acc_ref[0:1, :] += jnp.sum(dhn * hh, axis=0, keepdims=True)
            dg = dhn * g_ref[...]
            return r * (dg - hh * jnp.mean(dg * hh, axis=-1, keepdims=True))

        @pl.when(tile == 0)
        def _():
            state_backward(slice(LIVE0, tm))
            dpc_ref[LIVE0:tm, last * BLK :] = jnp.zeros((tm - LIVE0, BLK), BF16)
            dh = work(mc_ref[...], LIVE0, ((5, dpc_ref, K0), (6, dpc_ref, V0)))
            dm_ref[...] = dh[CHUNK - N_META : CHUNK]

        @pl.when(tile > 0)
        def _():
            retention_backward()
            dpc_ref[:, last * BLK :] = drg_ref[...]
            others = ((last, drg_ref, 0), (4, dpc_ref, Q0), (5, dpc_ref, K0), (6, dpc_ref, V0))
            for r0 in range(tm - B2_ROWS, -1, -B2_ROWS):
                r1 = r0 + B2_ROWS
                gx_ref[r0:r1, :] = dh2_ref[r0:r1, :] + work(x_ref[r0:r1, :], r0, others, r1)

    seven = last * BLK
    full = N_PROJ * BLK

    rev = lambda w, j=0: pl.BlockSpec((tm, w), lambda i: (nt - 1 - i, j))
    tok = pl.BlockSpec((tm, D_MODEL), lambda i: (_token_tile(nt - 1 - i), 0))
    const = lambda r, c: pl.BlockSpec((r, c), lambda i: (0, 0))
    return pl.pallas_call(
        body,
        name="b2_dconv_dh",
        grid=(nt,),
        in_specs=[
            rev(BLK),
            rev(D_CONV),
            rev(D_RET),
            rev(seven),
            rev(D_CONV),
            pl.BlockSpec((tm // CHUNK, N_HEADS, HEAD_DIM, HEAD_DIM), lambda i: (nt - 1 - i, 0, 0, 0)),
            _VMEM,
            _VMEM,
            tok,
            _VMEM,
            rev(D_MODEL),
        ]
        + [_VMEM] * 10,
        out_specs=(rev(full), tok, const(N_META, D_MODEL), const(8, D_MODEL), const(8, D_CONV)),
        out_shape=(
            jax.ShapeDtypeStruct((n_rows, full), BF16),
            jax.ShapeDtypeStruct(x2.shape, F32),
            jax.ShapeDtypeStruct((N_META, D_MODEL), F32),
            jax.ShapeDtypeStruct((8, D_MODEL), F32),
            jax.ShapeDtypeStruct((8, D_CONV), F32),
        ),
        scratch_shapes=[pltpu.VMEM((8, D_CONV), F32), pltpu.VMEM((N_HEADS, HEAD_DIM, HEAD_DIM), F32)],
        compiler_params=_params(),
    )(drg, dco, do, proj, conv, states, w3, cw8, x2, meta_chunk, dh2, g1, *tables, zeta0, *rot)


def _gw_in_scatter(hnT_tok, hnT_meta, dproj, gw_out_parts, dmeta_rows, acc_conv, acc_b2, acc_ret, acc_f2, me_arr):
    n_rows = dproj.shape[0]
    last = N_DEV - 1
    cols = D_CONV // N_DEV
    by_dest = (True, True, False)
    small_blocks = (gw_out_parts.shape[1:], (SC_CONV0 + 8, CHUNK), (GA_LOSS + 8, CHUNK))

    def body(me_ref, a_ref, am_ref, b_ref, go_ref, dm_ref, ac_ref, ab2_ref, aret_ref, af2_ref,
             land_in, land_go, land_sc, land_ga, d2d_buf, d2d_land, ici_buf, sc_ref, ga_ref,
             d2d_send, d2d_recv, ici_send, ici_recv, send_sems, recv_sems, local_sems):
        del me_ref
        s = pl.program_id(0)
        t = last - s
        q = t >> 1
        x, y, c = lax.axis_index("x"), lax.axis_index("y"), lax.axis_index("c")
        me = 4 * x + 2 * y + c
        chip = 2 * x + y
        srcs, lands = (go_ref, sc_ref, ga_ref), (land_go, land_sc, land_ga)

        def peer_at(k):
            return (1 - x if k & 4 else x, 1 - y if k & 2 else y, 1 - c if k & 1 else c)

        def small_copy(a, k):
            px, py, pc = peer_at(k)
            return pltpu.make_async_remote_copy(
                src_ref=srcs[a].at[4 * px + 2 * py + pc] if by_dest[a] else srcs[a],
                dst_ref=lands[a].at[me],
                send_sem=send_sems.at[a * last + k - 1],
                recv_sem=recv_sems.at[a * last + k - 1],
                device_id=(px, py, pc),
                device_id_type=MESH,
            )

        def small_local(a):
            return pltpu.make_async_copy(srcs[a].at[me] if by_dest[a] else srcs[a], lands[a].at[me], local_sems.at[a + 1])

        def d2d_copy(j):
            return pltpu.make_async_remote_copy(
                src_ref=d2d_buf.at[j],
                dst_ref=d2d_land.at[j],
                send_sem=d2d_send.at[j],
                recv_sem=d2d_recv.at[j],
                device_id=(x, y, 1 - c),
                device_id_type=MESH,
            )

        def ici_copy(j, to):
            return pltpu.make_async_remote_copy(
                src_ref=ici_buf.at[j],
                dst_ref=land_in.at[chip],
                send_sem=ici_send.at[j],
                recv_sem=ici_recv.at[j],
                device_id=to,
                device_id_type=MESH,
            )

        def own_copy():
            return pltpu.make_async_copy(ici_buf.at[0], land_in.at[chip], local_sems.at[0])

        @pl.when(s == 0)
        def _():
            sc_ref[...] = jnp.zeros_like(sc_ref)
            for d in range(N_DEV):
                sc_ref[d, SC_META0 : SC_META0 + N_META, :] = dm_ref[:, d * CHUNK : (d + 1) * CHUNK]
                lane0 = (d * cols) % CHUNK
                rows8 = ac_ref[:, d * cols - lane0 : d * cols - lane0 + CHUNK]
                rows8 = pltpu.roll(rows8, CHUNK - lane0, 1) if lane0 else rows8
                sc_ref[d, SC_CONV0 : SC_CONV0 + 3, 0:cols] = rows8[0:3, 0:cols]
            ga_ref[...] = jnp.zeros_like(ga_ref)
            for row0, acc, n in ((GA_N1, ab2_ref, 8), (GA_RG, aret_ref, 4), (GA_FG, af2_ref, 8)):
                for j in range(n):
                    ga_ref[row0 + j : row0 + j + 1, :] = acc[0:1, j * CHUNK : (j + 1) * CHUNK]
            ga_ref[GA_LOSS : GA_LOSS + 1, :] = af2_ref[1:2, 0:CHUNK]
            for a in range(3):
                small_local(a).start()
            for k in range(last, 0, -1):
                for a in range(3):
                    small_copy(a, k).start()

        blk = jnp.dot(a_ref[...], b_ref[ROW_TILE:, :], preferred_element_type=F32)
        blk += jnp.dot(am_ref[...], b_ref[LIVE0:ROW_TILE, :], preferred_element_type=F32)

        @pl.when((t & 1) == 1)
        def _():
            d2d_buf[q] = blk.astype(BF16)
            d2d_copy(q).start()

        @pl.when((t & 1) == 0)
        def _():
            d2d_copy(q).wait_recv()
            ici_buf[q] = (blk + d2d_land[q].astype(F32)).astype(BF16)

            @pl.when(t != 0)
            def _():
                ici_copy(q, (jnp.bitwise_xor(x, (t >> 2) & 1), jnp.bitwise_xor(y, (t >> 1) & 1), c)).start()

            @pl.when(t == 0)
            def _():
                own_copy().start()
                for j in range(N_CHIPS):
                    d2d_copy(j).wait_send()
                for j in range(N_CHIPS - 1, 0, -1):
                    ici_copy(j, peer_at(2 * j)).wait()
                for k in range(last, 0, -1):
                    for a in range(3):
                        small_copy(a, k).wait()
                for a in range(3):
                    small_local(a).wait()
                own_copy().wait()

    grid_spec = pltpu.PrefetchScalarGridSpec(
        num_scalar_prefetch=1,
        grid=(N_DEV,),
        in_specs=[_VMEM, _VMEM, pl.BlockSpec((n_rows, BLK), lambda s, me: (0, jnp.bitwise_xor(me[0], last - s))), _HBM]
        + [_VMEM] * 5,
        out_specs=tuple([_HBM] * 4),
        scratch_shapes=[
            pltpu.VMEM((N_CHIPS, D_MODEL, BLK), BF16),
            pltpu.VMEM((N_CHIPS, D_MODEL, BLK), BF16),
            pltpu.VMEM((N_CHIPS, D_MODEL, BLK), BF16),
            pltpu.VMEM((N_DEV, *small_blocks[1]), F32),
            pltpu.VMEM(small_blocks[2], F32),
            pltpu.SemaphoreType.DMA((N_CHIPS,)),
            pltpu.SemaphoreType.DMA((N_CHIPS,)),
            pltpu.SemaphoreType.DMA((N_CHIPS,)),
            pltpu.SemaphoreType.DMA((N_CHIPS,)),
            pltpu.SemaphoreType.DMA((3 * last,)),
            pltpu.SemaphoreType.DMA((3 * last,)),
            pltpu.SemaphoreType.DMA((4,)),
        ],
    )
    return pl.pallas_call(
        body,
        name="gw_in_scatter",
        grid_spec=grid_spec,
        out_shape=(jax.ShapeDtypeStruct((N_CHIPS, D_MODEL, BLK), BF16),)
        + tuple(jax.ShapeDtypeStruct((N_DEV, *b), dt) for b, dt in zip(small_blocks, (gw_out_parts.dtype, F32, F32), strict=True)),
        compiler_params=_params(),
    )(me_arr, hnT_tok, hnT_meta, dproj, gw_out_parts, dmeta_rows, acc_conv, acc_b2, acc_ret, acc_f2)


def _adamw(w, g, m, v):
    m = ADAM_B1 * m + (1.0 - ADAM_B1) * g
    v = ADAM_B2 * v + (1.0 - ADAM_B2) * (g * g)
    m_hat = m / (1.0 - ADAM_B1**ADAM_STEP)
    v_hat = v / (1.0 - ADAM_B2**ADAM_STEP)
    delta = -ADAM_LR * (m_hat / (jnp.sqrt(v_hat) + ADAM_EPS) + ADAM_WD * w)
    return delta, m, v


def _small_leaves(meta, conv_w, n1, rg, fg):
    return meta, conv_w, n1.reshape(8, CHUNK), rg.reshape(4, CHUNK), fg.reshape(8, CHUNK)


def _from_small_leaves(meta, conv_w, n1, rg, fg):
    return meta, conv_w, n1.reshape(D_MODEL), rg.reshape(D_RET), fg.reshape(D_MODEL)


ADAM_GRID = 4


def _adamw_all(land_in, land_out, land_sc, land_ga, big, small):
    n_leaf = 5
    lands = (land_in, land_out)
    n_big = len(big)

    def body(*refs):
        big_in, sc_ref, ga_ref = refs[: 4 * n_big], refs[4 * n_big], refs[4 * n_big + 1]
        small_in = refs[4 * n_big + 2 : 4 * n_big + 2 + 3 * n_leaf]
        big_out = refs[4 * n_big + 2 + 3 * n_leaf : 8 * n_big + 2 + 3 * n_leaf]
        small_out = refs[8 * n_big + 2 + 3 * n_leaf :]
        for b in range(n_big):
            p_ref, w_ref, m_ref, v_ref = big_in[4 * b : 4 * b + 4]
            g_ref, d_ref, nm_ref, nv_ref = big_out[4 * b : 4 * b + 4]
            g = p_ref[0].astype(F32)
            for s in range(1, p_ref.shape[0]):
                g = g + p_ref[s].astype(F32)
            g_ref[...] = g
            d_ref[...], nm_ref[...], nv_ref[...] = _adamw(w_ref[...], g, m_ref[...], v_ref[...])

        @pl.when(pl.program_id(0) == 0)
        def _():
            sc, ga = sc_ref[0], ga_ref[0]
            for s in range(1, N_DEV):
                sc = sc + sc_ref[s]
                ga = ga + ga_ref[s]
            grads = (
                sc[SC_META0 : SC_META0 + N_META],
                sc[SC_CONV0 : SC_CONV0 + 3, 0 : D_CONV // N_DEV],
                ga[GA_N1 : GA_N1 + 8],
                ga[GA_RG : GA_RG + 4],
                ga[GA_FG : GA_FG + 8],
            )
            for leaf, g in enumerate(grads):
                d, nm, nv = _adamw(
                    small_in[leaf][...], g, small_in[n_leaf + leaf][...], small_in[2 * n_leaf + leaf][...]
                )
                small_out[leaf][...] = g
                small_out[n_leaf + leaf][...] = d
                small_out[2 * n_leaf + leaf][...] = nm
                small_out[3 * n_leaf + leaf][...] = nv
            small_out[4 * n_leaf][...] = ga[GA_LOSS : GA_LOSS + 1]

    in_specs, out_specs, out_shape, operands = [], [], [], []
    for land, (w, m, v) in zip(lands, big, strict=True):
        n_r, n_c = w.shape
        tr = n_r // ADAM_GRID
        blk = pl.BlockSpec((tr, n_c), lambda i: (i, 0))
        in_specs += [pl.BlockSpec((land.shape[0], tr, n_c), lambda i: (0, i, 0)), blk, blk, blk]
        out_specs += [blk] * 4
        out_shape += [jax.ShapeDtypeStruct((n_r, n_c), F32)] * 4
        operands += [land, w, m, v]
    small_shapes = [jax.ShapeDtypeStruct(a.shape, F32) for a in small[0]] * 4 + [jax.ShapeDtypeStruct((1, CHUNK), F32)]
    out = pl.pallas_call(
        body,
        name="adamw_all",
        grid=(ADAM_GRID,),
        in_specs=in_specs + [_VMEM] * (2 + 3 * n_leaf),
        out_specs=tuple(out_specs + [_VMEM] * len(small_shapes)),
        out_shape=tuple(out_shape + small_shapes),
        compiler_params=_params(),
    )(*operands, land_sc, land_ga, *small[0], *small[1], *small[2])
    bigs = tuple(out[4 * b : 4 * b + 4] for b in range(n_big))
    rest = out[4 * n_big :]
    return bigs, tuple(rest[k * n_leaf : (k + 1) * n_leaf] for k in range(4)), rest[4 * n_leaf]


def kernel(x, meta, norm1_g, w_in, conv_w, ret_norm_g, w_out, final_g, loss_target, m_meta, m_norm1_g, m_w_in, m_conv_w, m_ret_norm_g, m_w_out, m_final_g, v_meta, v_norm1_g, v_w_in, v_conv_w, v_ret_norm_g, v_w_out, v_final_g):
    seq = x.shape[1]
    assert x.shape == (1, seq, D_MODEL) and seq % ROW_TILE == 0
    n_tiles = seq // ROW_TILE + 1
    x2, t2 = x[0], loss_target[0]

    small_w = _small_leaves(meta, conv_w, norm1_g, ret_norm_g, final_g)
    g1 = norm1_g.reshape(1, D_MODEL)
    hn_tok, hnT_tok, w3, meta_chunk, cw8 = _gather_weights_and_norm(x2, g1, w_in, meta, conv_w)

    rot = _rotary_tables(n_tiles, ROW_TILE)
    dec, _, zeta, xi = _decay_tables()
    gr = ret_norm_g.reshape(1, D_RET)
    gf = final_g.reshape(1, D_MODEL)

    hnT_meta, proj, conv_out, conv, o, states, wo3 = _f1(hn_tok, meta_chunk, g1, w3, cw8, rot, dec, zeta, xi, w_out)
    wout = wo3.reshape(D_MODEL, D_MODEL)
    dh2, dco, drg, do, gw_out, acc_f2, acc_ret = _f2(proj, o, conv_out, x2, t2, wout, gr, gf)
    dproj, grad_x2, dmeta_rows, acc_b2, acc_conv = _b2(
        drg, dco, do, proj, conv, states, w3, cw8, x2, meta_chunk, dh2, g1, _decay_tables(B1_CHUNK), zeta, rot
    )
    gw_out_parts = gw_out.reshape(N_DEV, D_MODEL // N_DEV, D_MODEL)

    me_arr = (4 * lax.axis_index("x") + 2 * lax.axis_index("y") + lax.axis_index("c")).astype(jnp.int32).reshape(1)
    land_in, land_out, land_sc, land_ga = _gw_in_scatter(
        hnT_tok, hnT_meta, dproj, gw_out_parts, dmeta_rows, acc_conv, acc_b2, acc_ret, acc_f2, me_arr
    )

    (w_in_out, w_out_out), small_out, loss_row = _adamw_all(
        land_in,
        land_out,
        land_sc,
        land_ga,
        ((w_in, m_w_in, v_w_in), (w_out, m_w_out, v_w_out)),
        (
            small_w,
            _small_leaves(m_meta, m_conv_w, m_norm1_g, m_ret_norm_g, m_final_g),
            _small_leaves(v_meta, v_conv_w, v_norm1_g, v_ret_norm_g, v_final_g),
        ),
    )
    g_w_in, d_w_in, nm_w_in, nv_w_in = w_in_out
    g_w_out, d_w_out, nm_w_out, nv_w_out = w_out_out
    loss = loss_row[0, 0]
    grad_x = grad_x2.reshape(1, seq, D_MODEL)

    def leaves(w_in_leaf, w_out_leaf, small_leaves):
        meta_leaf, conv_leaf, n1_leaf, rg_leaf, fg_leaf = _from_small_leaves(*small_leaves)
        return (meta_leaf, n1_leaf, w_in_leaf, conv_leaf, rg_leaf, w_out_leaf, fg_leaf)

    return (
        loss,
        grad_x,
        *leaves(g_w_in, g_w_out, small_out[0]),
        *leaves(d_w_in, d_w_out, small_out[1]),
        *leaves(nm_w_in, nm_w_out, small_out[2]),
        *leaves(nv_w_in, nv_w_out, small_out[3]),
    )
```

```python
import math

import jax
import jax.numpy as jnp
import numpy as np
from jax import lax
from jax.experimental import pallas as pl
from jax.experimental.pallas import tpu as pltpu

F32 = jnp.float32
BF16 = jnp.bfloat16

N_DEV = 8
N_CHIPS = 4
D_MODEL = 1024
N_META = 16
CHUNK = 128
D_CONV = 512
D_RET = 512
N_HEADS = 4
HEAD_DIM = 128
N_PROJ = 8
BLK = 512
ROPE_BASE = 10000.0
EPS = 1e-6
Q_SCALE = HEAD_DIM ** -0.5
LOG_G = tuple(math.log(1.0 - 2.0 ** (-5.0 - h)) for h in range(N_HEADS))
CHUNK_DECAY = tuple(math.exp(CHUNK * lg) for lg in LOG_G)

ADAM_LR = 0.001
ADAM_B1 = 0.9
ADAM_B2 = 0.999
ADAM_EPS = 1e-08
ADAM_WD = 0.01
ADAM_STEP = 10

ROW_TILE = 512
PAD_ROWS = ROW_TILE - N_META
LIVE0 = ROW_TILE - CHUNK
B1_CHUNK = 256
B2_ROWS = ROW_TILE // 2
VMEM_LIMIT = 56 * 1024 * 1024

SC_META0, SC_CONV0 = 0, 16
GA_N1, GA_RG, GA_FG, GA_LOSS = 0, 8, 16, 24

NT = (((1,), (1,)), ((), ()))
TN = (((0,), (0,)), ((), ()))
MESH = pl.DeviceIdType.MESH

_VMEM = pl.BlockSpec(memory_space=pltpu.VMEM)
_HBM = pl.BlockSpec(memory_space=pltpu.HBM)


def _params(n_axes=1):
    return pltpu.CompilerParams(dimension_semantics=("arbitrary",) * n_axes, vmem_limit_bytes=VMEM_LIMIT)


def _sigmoid(x):
    return 0.5 * jnp.tanh(0.5 * x) + 0.5


def _decay_tables(chunk=CHUNK):
    idx = np.arange(chunk, dtype=np.float64)
    diff = idx[:, None] - idx[None, :]
    dec = np.stack([np.where(diff >= 0, np.exp(diff * lg), 0.0) for lg in LOG_G])
    zeta = np.stack([np.exp((chunk - 1 - idx) * lg) for lg in LOG_G])
    xi = np.stack([np.exp((idx + 1.0) * lg) for lg in LOG_G])
    ones = np.ones((1, 1, HEAD_DIM))
    return (
        jnp.asarray(dec, F32),
        jnp.asarray(dec.transpose(0, 2, 1), F32),
        jnp.asarray(zeta[:, :, None] * ones, F32),
        jnp.asarray(xi[:, :, None] * ones, F32),
    )


def _rotary_tables(n_tiles, tm):
    half = HEAD_DIM // 2
    freqs = (1.0 / (np.float32(ROPE_BASE) ** (np.arange(half, dtype=np.float32) / np.float32(half)))).astype(np.float64)
    sign = np.concatenate([-np.ones(half), np.ones(half)])
    two = lambda a: np.concatenate([a, a], axis=1)
    base = two((np.arange(n_tiles, dtype=np.float64) * tm - PAD_ROWS)[:, None] * freqs[None, :])
    off = two(np.arange(tm, dtype=np.float64)[:, None] * freqs[None, :])
    as32 = lambda a: jnp.asarray(a, F32)
    return as32(np.cos(base)), as32(np.sin(base) * sign), as32(np.cos(off)), as32(np.sin(off) * sign)


def _tile_rotary(ca_ref, sa_ref, cb_ref, sb_ref, tile):
    ca, sa = ca_ref[pl.ds(tile, 1), :], sa_ref[pl.ds(tile, 1), :]
    cb, sb = cb_ref[...], sb_ref[...]
    return ca * cb - sa * sb, sa * cb + ca * sb


def _rot(t, cos2, sin2):
    return t * cos2 + pltpu.roll(t, HEAD_DIM // 2, 1) * sin2


def _rot_bwd(d, cos2, sin2):
    return d * cos2 + pltpu.roll(d * sin2, HEAD_DIM // 2, 1)


def _token_tile(i):
    return jnp.maximum(i - 1, 0)


GATHER_PAIRS = 7


def _two_level_gather(src_ref, out_ref, send_sems, recv_sems, local_sem, base=0):
    x, y, c = lax.axis_index("x"), lax.axis_index("y"), lax.axis_index("c")
    me, sibling = (x, y, c), (x, y, 1 - c)
    xnb, ynb, diag = (1 - x, y), (x, 1 - y), (1 - x, 1 - y)
    relayed = (jnp.bitwise_xor(x, 1 - c), jnp.bitwise_xor(y, c))
    other = (jnp.bitwise_xor(x, c), jnp.bitwise_xor(y, 1 - c))

    def copy(k, block, to, src=None):
        dst = out_ref.at[4 * block[0] + 2 * block[1] + block[2]]
        return pltpu.make_async_remote_copy(
            src_ref=dst if src is None else src,
            dst_ref=dst,
            send_sem=send_sems.at[base + k],
            recv_sem=recv_sems.at[base + k],
            device_id=to,
            device_id_type=MESH,
        )

    mine = pltpu.make_async_copy(src_ref, out_ref.at[4 * x + 2 * y + c], local_sem)
    first = [copy(1, me, (*xnb, c), src=src_ref), copy(2, me, (*ynb, c), src=src_ref), copy(0, me, sibling, src=src_ref)]
    relay = copy(3, (*relayed, c), (*other, c))
    passed = [copy(4 + j, (*chip, c), sibling) for j, chip in enumerate((xnb, ynb, diag))]

    def start():
        mine.start()
        for cp in first:
            cp.start()

    def forward():
        copy(1 + c, (*relayed, c), me).wait_recv()
        relay.start()
        copy(2 - c, (*other, c), me).wait_recv()
        passed[0].start()
        passed[1].start()

    def forward_relayed():
        copy(3, (*diag, c), me).wait_recv()
        passed[2].start()

    def finish():
        copy(0, sibling, me).wait_recv()
        for j, chip in enumerate((xnb, ynb, diag)):
            copy(4 + j, (*chip, 1 - c), me).wait_recv()
        for cp in first + [relay] + passed:
            cp.wait_send()
        mine.wait()

    return start, forward, forward_relayed, finish


W_IN_CHUNKS = 4


def _gather_weights_and_norm(x2, g1, w_shard, meta, conv_w):
    per = GATHER_PAIRS
    n_ch = W_IN_CHUNKS
    rows = w_shard.shape[0] // n_ch
    tm = ROW_TILE
    nt = x2.shape[0] // tm
    cols = D_CONV // N_DEV
    pack_shape = (SC_CONV0 + 8, CHUNK)
    starts = [0] * (n_ch + 1)
    forwards = [min(5 + 2 * p, nt - 1) for p in range(n_ch)] + [min(12, nt - 1)]
    relayeds = [min(12 + p, nt - 1) for p in range(n_ch)] + [nt - 1]
    finishes = [nt - 1] * (n_ch + 1)

    def body(x_ref, g_ref, w_ref, meta_ref, conv_ref, hn_ref, hnT_ref, w_all_ref, mc_ref, cw_ref,
             w_bf, small_ref, small_all_ref, send_sems, recv_sems, local_sems):
        i = pl.program_id(0)

        @pl.when(i == 0)
        def _():
            w_bf[...] = w_ref[...].astype(BF16)
            small_ref[...] = jnp.zeros_like(small_ref)
            small_ref[SC_META0 : SC_META0 + N_META, :] = meta_ref[...]
            small_ref[SC_CONV0 : SC_CONV0 + 3, 0:cols] = conv_ref[...]

        parts = [
            _two_level_gather(
                w_bf.at[pl.ds(ch * rows, rows)],
                w_all_ref.at[:, pl.ds(ch * rows, rows)],
                send_sems,
                recv_sems,
                local_sems.at[ch],
                ch * per,
            )
            for ch in range(n_ch)
        ]
        parts.append(_two_level_gather(small_ref, small_all_ref, send_sems, recv_sems, local_sems.at[n_ch], n_ch * per))
        for phase, steps in enumerate((starts, forwards, relayeds, finishes)):
            for part, step in zip(parts, steps, strict=True):
                pl.when(i == step)(part[phase])

        h = x_ref[...]
        r = lax.rsqrt(jnp.mean(h * h, axis=-1, keepdims=True) + EPS)
        hn = (h * r * g_ref[...]).astype(BF16)
        hn_ref[...] = hn
        hnT_ref[...] = hn.T

        @pl.when(i == nt - 1)
        def _():
            mc_ref[0 : CHUNK - N_META, :] = jnp.zeros((CHUNK - N_META, D_MODEL), F32)
            cw_ref[...] = jnp.zeros_like(cw_ref)
            for d in range(N_DEV):
                mc_ref[CHUNK - N_META : CHUNK, d * CHUNK : (d + 1) * CHUNK] = small_all_ref[d, SC_META0 : SC_META0 + N_META, :]
                lane0 = (d * cols) % CHUNK
                rows8 = small_all_ref[d, SC_CONV0 : SC_CONV0 + 8, :]
                rows8 = pltpu.roll(rows8, lane0, 1) if lane0 else rows8
                cw_ref[0:3, d * cols : (d + 1) * cols] = rows8[0:3, lane0 : lane0 + cols]

    const = lambda r, c: pl.BlockSpec((r, c), lambda i: (0, 0))
    return pl.pallas_call(
        body,
        name="gather_weights_norm",
        grid=(nt,),
        out_shape=(
            jax.ShapeDtypeStruct(x2.shape, BF16),
            jax.ShapeDtypeStruct(x2.shape[::-1], BF16),
            jax.ShapeDtypeStruct((N_DEV, *w_shard.shape), BF16),
            jax.ShapeDtypeStruct((CHUNK, D_MODEL), F32),
            jax.ShapeDtypeStruct((8, D_CONV), F32),
        ),
        in_specs=[pl.BlockSpec((tm, D_MODEL), lambda i: (i, 0)), _VMEM, _VMEM, _VMEM, _VMEM],
        out_specs=(
            pl.BlockSpec((tm, D_MODEL), lambda i: (i, 0)),
            pl.BlockSpec((D_MODEL, tm), lambda i: (0, i)),
            _HBM,
            const(CHUNK, D_MODEL),
            const(8, D_CONV),
        ),
        scratch_shapes=[
            pltpu.VMEM(w_shard.shape, BF16),
            pltpu.VMEM(pack_shape, F32),
            pltpu.VMEM((N_DEV, *pack_shape), F32),
            pltpu.SemaphoreType.DMA(((n_ch + 1) * per,)),
            pltpu.SemaphoreType.DMA(((n_ch + 1) * per,)),
            pltpu.SemaphoreType.DMA((n_ch + 1,)),
        ],
        compiler_params=_params(),
    )(x2, g1, w_shard, meta, conv_w)


def _f1(hn_tok, meta_chunk, g1, w3, cw8, rot, dec, zeta, xi, wo_shard):
    tm = ROW_TILE
    nt = hn_tok.shape[0] // tm + 1
    n_rows = nt * tm
    nct = tm // CHUNK

    def body(hn_ref, mc_ref, g_ref, w_ref, cw_ref, ca_ref, sa_ref, cb_ref, sb_ref, dec_ref, zeta_ref, xi_ref, wo_ref,
             hnT_ref, pr_ref, co_ref, cv_ref, o_ref, st_ref, wo_all_ref,
             halo, state, wo_bf, send_sems, recv_sems, local_sem):
        i = pl.program_id(0)

        @pl.when(i == 0)
        def _():
            wo_bf[...] = wo_ref[...].astype(BF16)

        wo_phases = _two_level_gather(wo_bf, wo_all_ref, send_sems, recv_sems, local_sem)
        for step, phase in zip((0, nt // 3, 2 * nt // 3, nt - 1), wo_phases, strict=True):
            pl.when(i == step)(phase)

        def work(hn, r0):
            rs = slice(r0, tm)
            n = tm - r0

            def proj(j):
                return jnp.dot(hn, w_ref[j], preferred_element_type=F32)

            cos_all, sin_all = _tile_rotary(ca_ref, sa_ref, cb_ref, sb_ref, i)
            cos_t, sin_t = cos_all[rs, :], sin_all[rs, :]
            q, k, v = proj(4), proj(5), proj(6).astype(BF16)
            pr_ref[rs, 6 * BLK : 7 * BLK] = v
            chunks = range(r0 // CHUNK, tm // CHUNK)
            heads = []
            for hd in range(N_HEADS):
                cs = slice(hd * HEAD_DIM, (hd + 1) * HEAD_DIM)
                qh = (_rot(q[:, cs], cos_t, sin_t) * Q_SCALE).astype(BF16)
                kh = _rot(k[:, cs], cos_t, sin_t).astype(BF16)
                vh = v[:, cs]
                pr_ref[rs, 4 * BLK + hd * HEAD_DIM : 4 * BLK + (hd + 1) * HEAD_DIM] = qh
                pr_ref[rs, 5 * BLK + hd * HEAD_DIM : 5 * BLK + (hd + 1) * HEAD_DIM] = kh
                upd = {}
                for c in chunks:
                    cr = slice(c * CHUNK - r0, (c + 1) * CHUNK - r0)
                    kz = (kh[cr].astype(F32) * zeta_ref[hd]).astype(BF16)
                    upd[c] = lax.dot_general(kz, vh[cr], TN, preferred_element_type=F32)
                st = state[hd]
                outs = []
                for c in chunks:
                    cr = slice(c * CHUNK - r0, (c + 1) * CHUNK - r0)
                    qc, kc, vc = qh[cr], kh[cr], vh[cr]
                    st_bf = st.astype(BF16)
                    st_ref[c, hd] = st_bf
                    s = lax.dot_general(qc, kc, NT, preferred_element_type=F32) * dec_ref[hd]
                    inner = jnp.dot(s.astype(BF16), vc, preferred_element_type=F32)
                    qx = (qc.astype(F32) * xi_ref[hd]).astype(BF16)
                    outs.append(inner + jnp.dot(qx, st_bf, preferred_element_type=F32))
                    st = CHUNK_DECAY[hd] * st + upd[c]
                state[hd] = st
                heads.append(jnp.concatenate(outs, axis=0) if len(outs) > 1 else outs[0])
            o_ref[rs, :] = jnp.concatenate(heads, axis=1)

            cx, cb, cc, cg = proj(0), proj(1), proj(2), proj(3)
            u = cc * cx
            rows = lax.broadcasted_iota(jnp.int32, u.shape, 0)
            hl = halo[...]
            u1 = jnp.where(rows == 0, hl[7:8], pltpu.roll(u, 1, 0))
            u2 = jnp.where(rows == 0, hl[6:7], jnp.where(rows == 1, hl[7:8], pltpu.roll(u, 2, 0)))
            halo[...] = u[n - 8 : n]
            cw = cw_ref[...]
            conv = cw[0:1] * u2 + cw[1:2] * u1 + cw[2:3] * u
            co = (cb * conv * (cg * _sigmoid(cg))).astype(BF16)
            co_ref[rs, :] = co
            cv_ref[rs, :] = conv.astype(BF16)
            pr_ref[rs, 0 * BLK : 1 * BLK] = cx.astype(BF16)
            pr_ref[rs, 1 * BLK : 2 * BLK] = cb.astype(BF16)
            pr_ref[rs, 2 * BLK : 3 * BLK] = cc.astype(BF16)
            pr_ref[rs, 3 * BLK : 4 * BLK] = cg.astype(BF16)
            pr_ref[rs, 7 * BLK : 8 * BLK] = proj(7).astype(BF16)

        @pl.when(i == 0)
        def _():
            halo[...] = jnp.zeros_like(halo)
            state[...] = jnp.zeros_like(state)
            h = mc_ref[...]
            r = lax.rsqrt(jnp.mean(h * h, axis=-1, keepdims=True) + EPS)
            hn = (h * r * g_ref[...]).astype(BF16)
            hnT_ref[...] = hn.T
            work(hn, LIVE0)

        @pl.when(i > 0)
        def _():
            work(hn_ref[...], 0)

    row = lambda w: pl.BlockSpec((tm, w), lambda i: (i, 0))
    return pl.pallas_call(
        body,
        name="f1_inproj_conv",
        grid=(nt,),
        in_specs=[pl.BlockSpec((tm, D_MODEL), lambda i: (_token_tile(i), 0))] + [_VMEM] * 12,
        out_specs=(
            pl.BlockSpec((D_MODEL, CHUNK), lambda i: (0, 0)),
            row(N_PROJ * BLK),
            row(D_CONV),
            row(D_CONV),
            row(D_RET),
            pl.BlockSpec((nct, N_HEADS, HEAD_DIM, HEAD_DIM), lambda i: (i, 0, 0, 0)),
            _HBM,
        ),
        out_shape=(
            jax.ShapeDtypeStruct((D_MODEL, CHUNK), BF16),
            jax.ShapeDtypeStruct((n_rows, N_PROJ * BLK), BF16),
            jax.ShapeDtypeStruct((n_rows, D_CONV), BF16),
            jax.ShapeDtypeStruct((n_rows, D_CONV), BF16),
            jax.ShapeDtypeStruct((n_rows, D_RET), F32),
            jax.ShapeDtypeStruct((n_rows // CHUNK, N_HEADS, HEAD_DIM, HEAD_DIM), BF16),
            jax.ShapeDtypeStruct((N_DEV, *wo_shard.shape), BF16),
        ),
        scratch_shapes=[
            pltpu.VMEM((8, D_CONV), F32),
            pltpu.VMEM((N_HEADS, HEAD_DIM, HEAD_DIM), F32),
            pltpu.VMEM(wo_shard.shape, BF16),
            pltpu.SemaphoreType.DMA((GATHER_PAIRS,)),
            pltpu.SemaphoreType.DMA((GATHER_PAIRS,)),
            pltpu.SemaphoreType.DMA(()),
        ],
        compiler_params=_params(),
    )(hn_tok, meta_chunk, g1, w3, cw8, *rot, dec, zeta, xi, wo_shard)


def _group_norm(o):
    ys, rs = [], []
    for hd in range(N_HEADS):
        oh = o[:, hd * HEAD_DIM : (hd + 1) * HEAD_DIM]
        xc = oh - jnp.mean(oh, axis=-1, keepdims=True)
        rstd = lax.rsqrt(jnp.mean(xc * xc, axis=-1, keepdims=True) + EPS)
        ys.append(xc * rstd)
        rs.append(jnp.broadcast_to(rstd, oh.shape))
    return jnp.concatenate(ys, axis=1), jnp.concatenate(rs, axis=1)


def _f2(proj, o, conv_out, x2, t2, wout, gr, gf):
    n_rows = proj.shape[0]
    tm = ROW_TILE
    nt = n_rows // tm

    def body(o_ref, rg_ref, co_ref, x_ref, t_ref, wo_ref, gr_ref, gf_ref,
             dh2_ref, dco_ref, drg_ref, do_ref, gw_ref, acc_ref, accr_ref, gacc):
        i = pl.program_id(0)

        @pl.when(i == 0)
        def _():
            acc_ref[...] = jnp.zeros_like(acc_ref)
            accr_ref[...] = jnp.zeros_like(accr_ref)
            gacc[...] = jnp.zeros_like(gacc)
            dh2_ref[...] = jnp.zeros_like(dh2_ref)
            dco_ref[...] = jnp.zeros_like(dco_ref)
            drg_ref[...] = jnp.zeros_like(drg_ref)
            do_ref[...] = jnp.zeros_like(do_ref)

        @pl.when(i > 0)
        def _():
            yh, rstd = _group_norm(o_ref[...])
            rg = rg_ref[...].astype(F32)
            sg = _sigmoid(rg)
            silu = rg * sg
            grv = gr_ref[...]
            ro = (yh * grv * silu).astype(BF16)
            h2 = (
                x_ref[...]
                + jnp.dot(co_ref[...], wo_ref[0:D_CONV], preferred_element_type=F32)
                + jnp.dot(ro, wo_ref[D_CONV:], preferred_element_type=F32)
            )
            r2 = lax.rsqrt(jnp.mean(h2 * h2, axis=-1, keepdims=True) + EPS)
            yn = h2 * r2
            gfv = gf_ref[...]
            err = yn * gfv - t_ref[...]
            tile_loss = jnp.sum(jnp.sum(err * err, axis=-1, keepdims=True), axis=0, keepdims=True) * (0.5 / D_MODEL)
            acc_ref[0:1, :] += jnp.sum(err * yn, axis=0, keepdims=True) * (1.0 / D_MODEL)
            acc_ref[1:2, :] += tile_loss
            dyn = err * (gfv * (1.0 / D_MODEL))
            dh2 = r2 * (dyn - yn * jnp.mean(dyn * yn, axis=-1, keepdims=True))
            dh2_ref[...] = dh2
            dh2_bf = dh2.astype(BF16)
            dmix = lax.dot_general(dh2_bf, wo_ref[...], NT, preferred_element_type=F32)
            dco_ref[...] = dmix[:, :D_CONV].astype(BF16)
            dro = dmix[:, D_CONV:]
            dsilu = sg * (1.0 + rg * (1.0 - sg))
            drg_ref[...] = (dro * (yh * grv) * dsilu).astype(BF16)
            dret = dro * silu
            accr_ref[0:1, :] += jnp.sum(dret * yh, axis=0, keepdims=True)
            dyh = dret * grv
            for hd in range(N_HEADS):
                cs = slice(hd * HEAD_DIM, (hd + 1) * HEAD_DIM)
                a, b = dyh[:, cs], yh[:, cs]
                do_ref[:, cs] = (
                    rstd[:, cs]
                    * (a - jnp.mean(a, axis=-1, keepdims=True) - b * jnp.mean(a * b, axis=-1, keepdims=True))
                ).astype(BF16)
            gacc[0:D_CONV, :] += lax.dot_general(co_ref[...], dh2_bf, TN, preferred_element_type=F32)
            gacc[D_CONV:, :] += lax.dot_general(ro, dh2_bf, TN, preferred_element_type=F32)

        @pl.when(i == nt - 1)
        def _():
            gw_ref[...] = gacc[...].astype(BF16)

    row = lambda w, j=0: pl.BlockSpec((tm, w), lambda i: (i, j))
    tok = pl.BlockSpec((tm, D_MODEL), lambda i: (_token_tile(i), 0))
    return pl.pallas_call(
        body,
        name="f2_out_loss",
        grid=(nt,),
        in_specs=[row(D_RET), row(BLK, 7), row(D_CONV), tok, tok] + [_VMEM] * 3,
        out_specs=(
            row(D_MODEL),
            row(D_CONV),
            row(BLK),
            row(D_RET),
            pl.BlockSpec((D_MODEL, D_MODEL), lambda i: (0, 0)),
            pl.BlockSpec((8, D_MODEL), lambda i: (0, 0)),
            pl.BlockSpec((8, D_RET), lambda i: (0, 0)),
        ),
        out_shape=(
            jax.ShapeDtypeStruct((n_rows, D_MODEL), F32),
            jax.ShapeDtypeStruct((n_rows, D_CONV), BF16),
            jax.ShapeDtypeStruct((n_rows, BLK), BF16),
            jax.ShapeDtypeStruct((n_rows, D_RET), BF16),
            jax.ShapeDtypeStruct((D_MODEL, D_MODEL), BF16),
            jax.ShapeDtypeStruct((8, D_MODEL), F32),
            jax.ShapeDtypeStruct((8, D_RET), F32),
        ),
        scratch_shapes=[pltpu.VMEM((D_MODEL, D_MODEL), F32)],
        compiler_params=_params(),
    )(o, proj, conv_out, x2, t2, wout, gr, gf)


def _b2(drg, dco, do, proj, conv, states, w3, cw8, x2, meta_chunk, dh2, g1, tables, zeta0, rot):
    n_rows = drg.shape[0]
    tm = ROW_TILE
    nt = n_rows // tm
    last = N_PROJ - 1
    chunk = B1_CHUNK
    nct = tm // chunk
    decay = tuple(math.exp(chunk * lg) for lg in LOG_G)
    Q0, K0, V0 = 4 * BLK, 5 * BLK, 6 * BLK

    def body(drg_ref, dco_ref, do_ref, pr_ref, cv_ref, st_ref, w_ref, cw_ref, x_ref, mc_ref, dh2_ref, g_ref,
             dec_ref, dect_ref, zeta_ref, xi_ref, zeta0_ref, ca_ref, sa_ref, cb_ref, sb_ref,
             dpc_ref, gx_ref, dm_ref, acc_ref, accc_ref, halo, dstate):
        i = pl.program_id(0)
        tile = nt - 1 - i
        cos_all, sin_all = _tile_rotary(ca_ref, sa_ref, cb_ref, sb_ref, tile)

        @pl.when(i == 0)
        def _():
            acc_ref[...] = jnp.zeros_like(acc_ref)
            accc_ref[...] = jnp.zeros_like(accc_ref)
            halo[...] = jnp.zeros_like(halo)
            dstate[...] = jnp.zeros_like(dstate)

        def retention_backward():
            for hd in range(N_HEADS):
                cs = slice(hd * HEAD_DIM, (hd + 1) * HEAD_DIM)
                qs, ks, vs = (slice(c0 + hd * HEAD_DIM, c0 + (hd + 1) * HEAD_DIM) for c0 in (Q0, K0, V0))
                dupd = {}
                for c in range(nct):
                    rs = slice(c * chunk, (c + 1) * chunk)
                    qx = (pr_ref[rs, qs].astype(F32) * xi_ref[hd]).astype(BF16)
                    dupd[c] = lax.dot_general(qx, do_ref[rs, cs], TN, preferred_element_type=F32)
                dst = dstate[hd]
                for c in reversed(range(nct)):
                    rs = slice(c * chunk, (c + 1) * chunk)
                    cos_t, sin_t = cos_all[rs, :], sin_all[rs, :]
                    q, k, v, do = pr_ref[rs, qs], pr_ref[rs, ks], pr_ref[rs, vs], do_ref[rs, cs]
                    st_bf = st_ref[c * (chunk // CHUNK), hd]
                    dst_bf = dst.astype(BF16)
                    zt, xt = zeta_ref[hd], xi_ref[hd]
                    sT = (lax.dot_general(k, q, NT, preferred_element_type=F32) * dect_ref[hd]).astype(BF16)
                    dsT = (lax.dot_general(v, do, NT, preferred_element_type=F32) * dect_ref[hd]).astype(BF16)
                    ds = (lax.dot_general(do, v, NT, preferred_element_type=F32) * dec_ref[hd]).astype(BF16)
                    kz = (k.astype(F32) * zt).astype(BF16)
                    dv = jnp.dot(sT, do, preferred_element_type=F32) + jnp.dot(kz, dst_bf, preferred_element_type=F32)
                    dq = jnp.dot(ds, k, preferred_element_type=F32) + xt * lax.dot_general(
                        do, st_bf, NT, preferred_element_type=F32
                    )
                    dk = jnp.dot(dsT, q, preferred_element_type=F32) + zt * lax.dot_general(
                        v, dst_bf, NT, preferred_element_type=F32
                    )
                    dst = decay[hd] * dst + dupd[c]
                    dpc_ref[rs, qs] = _rot_bwd(dq * Q_SCALE, cos_t, sin_t).astype(BF16)
                    dpc_ref[rs, ks] = _rot_bwd(dk, cos_t, sin_t).astype(BF16)
                    dpc_ref[rs, vs] = dv.astype(BF16)
                dstate[hd] = dst

        def state_backward(rs):
            cos_t, sin_t = cos_all[rs, :], sin_all[rs, :]
            for hd in range(N_HEADS):
                qs, ks, vs = (slice(c0 + hd * HEAD_DIM, c0 + (hd + 1) * HEAD_DIM) for c0 in (Q0, K0, V0))
                k, v = pr_ref[rs, ks], pr_ref[rs, vs]
                dst_bf = dstate[hd].astype(BF16)
                zt = zeta0_ref[hd]
                kz = (k.astype(F32) * zt).astype(BF16)
                dv = jnp.dot(kz, dst_bf, preferred_element_type=F32)
                dk = zt * lax.dot_general(v, dst_bf, NT, preferred_element_type=F32)
                dpc_ref[rs, qs] = jnp.zeros((tm - LIVE0, HEAD_DIM), BF16)
                dpc_ref[rs, ks] = _rot_bwd(dk, cos_t, sin_t).astype(BF16)
                dpc_ref[rs, vs] = dv.astype(BF16)

        def work(h, r0, others, r1=tm):
            rs = slice(r0, r1)
            n = r1 - r0
            dco = dco_ref[rs, :].astype(F32)
            cx = pr_ref[rs, 0 * BLK : 1 * BLK].astype(F32)
            cb = pr_ref[rs, 1 * BLK : 2 * BLK].astype(F32)
            cc = pr_ref[rs, 2 * BLK : 3 * BLK].astype(F32)
            cg = pr_ref[rs, 3 * BLK : 4 * BLK].astype(F32)
            conv = cv_ref[rs, :].astype(F32)
            sg = _sigmoid(cg)
            silu, dsilu = cg * sg, sg * (1.0 + cg * (1.0 - sg))
            t = dco * cb
            dcb = (dco * conv * silu).astype(BF16)
            dcg = (t * conv * dsilu).astype(BF16)
            dconv = t * silu
            rows = lax.broadcasted_iota(jnp.int32, dconv.shape, 0)
            hl = halo[...]
            dc1 = jnp.where(rows == n - 1, hl[0:1], pltpu.roll(dconv, n - 1, 0))
            dc2 = jnp.where(rows == n - 2, hl[0:1], jnp.where(rows == n - 1, hl[1:2], pltpu.roll(dconv, n - 2, 0)))
            halo[...] = dconv[0:8]
            cw = cw_ref[...]
            du = cw[2:3] * dconv + cw[1:2] * dc1 + cw[0:1] * dc2
            u = cc * cx
            accc_ref[0:1, :] += jnp.sum(u * dc2, axis=0, keepdims=True)
            accc_ref[1:2, :] += jnp.sum(u * dc1, axis=0, keepdims=True)
            accc_ref[2:3, :] += jnp.sum(u * dconv, axis=0, keepdims=True)
            dcx = (du * cc).astype(BF16)
            dcc = (du * cx).astype(BF16)
            dhn = None
            for blk, ref, c0 in others:
                d = lax.dot_general(ref[rs, c0 : c0 + BLK], w_ref[blk], NT, preferred_element_type=F32)
                dhn = d if dhn is None else dhn + d
            for blk, d in ((1, dcb), (3, dcg), (0, dcx), (2, dcc)):
                dpc_ref[rs, blk * BLK : (blk + 1) * BLK] = d
                dhn += lax.dot_general(d, w_ref[blk], NT, preferred_element_type=F32)
            r = lax.rsqrt(jnp.mean(h * h, axis=-1, keepdims=True) + EPS)
            hh = h * r
            acc_ref[0:1, :] += jnp.sum(dhn * hh, axis=0, keepdims=True)
            dg = dhn * g_ref[...]
            return r * (dg - hh * jnp.mean(dg * hh, axis=-1, keepdims=True))

        @pl.when(tile == 0)
        def _():
            state_backward(slice(LIVE0, tm))
            dpc_ref[LIVE0:tm, last * BLK :] = jnp.zeros((tm - LIVE0, BLK), BF16)
            dh = work(mc_ref[...], LIVE0, ((5, dpc_ref, K0), (6, dpc_ref, V0)))
            dm_ref[...] = dh[CHUNK - N_META : CHUNK]

        @pl.when(tile > 0)
        def _():
            retention_backward()
            dpc_ref[:, last * BLK :] = drg_ref[...]
            others = ((last, drg_ref, 0), (4, dpc_ref, Q0), (5, dpc_ref, K0), (6, dpc_ref, V0))
            for r0 in range(tm - B2_ROWS, -1, -B2_ROWS):
                r1 = r0 + B2_ROWS
                gx_ref[r0:r1, :] = dh2_ref[r0:r1, :] + work(x_ref[r0:r1, :], r0, others, r1)

    seven = last * BLK
    full = N_PROJ * BLK

    rev = lambda w, j=0: pl.BlockSpec((tm, w), lambda i: (nt - 1 - i, j))
    tok = pl.BlockSpec((tm, D_MODEL), lambda i: (_token_tile(nt - 1 - i), 0))
    const = lambda r, c: pl.BlockSpec((r, c), lambda i: (0, 0))
    return pl.pallas_call(
        body,
        name="b2_dconv_dh",
        grid=(nt,),
        in_specs=[
            rev(BLK),
            rev(D_CONV),
            rev(D_RET),
            rev(seven),
            rev(D_CONV),
            pl.BlockSpec((tm // CHUNK, N_HEADS, HEAD_DIM, HEAD_DIM), lambda i: (nt - 1 - i, 0, 0, 0)),
            _VMEM,
            _VMEM,
            tok,
            _VMEM,
            rev(D_MODEL),
        ]
        + [_VMEM] * 10,
        out_specs=(rev(full), tok, const(N_META, D_MODEL), const(8, D_MODEL), const(8, D_CONV)),
        out_shape=(
            jax.ShapeDtypeStruct((n_rows, full), BF16),
            jax.ShapeDtypeStruct(x2.shape, F32),
            jax.ShapeDtypeStruct((N_META, D_MODEL), F32),
            jax.ShapeDtypeStruct((8, D_MODEL), F32),
            jax.ShapeDtypeStruct((8, D_CONV), F32),
        ),
        scratch_shapes=[pltpu.VMEM((8, D_CONV), F32), pltpu.VMEM((N_HEADS, HEAD_DIM, HEAD_DIM), F32)],
        compiler_params=_params(),
    )(drg, dco, do, proj, conv, states, w3, cw8, x2, meta_chunk, dh2, g1, *tables, zeta0, *rot)


B_CHUNKS = 4


def _gw_in_scatter(hnT_tok, hnT_meta, dproj, gw_out_parts, dmeta_rows, acc_conv, acc_b2, acc_ret, acc_f2, me_arr):
    n_rows = dproj.shape[0]
    last = N_DEV - 1
    cols = D_CONV // N_DEV
    b_rows = hnT_tok.shape[1] // B_CHUNKS
    by_dest = (True, True, False)
    small_blocks = (gw_out_parts.shape[1:], (SC_CONV0 + 8, CHUNK), (GA_LOSS + 8, CHUNK))

    def body(me_ref, a_ref, am_ref, b_hbm, go_ref, dm_ref, ac_ref, ab2_ref, aret_ref, af2_ref,
             land_in, land_go, land_sc, land_ga, d2d_buf, d2d_land, ici_buf, sc_ref, ga_ref, b_buf,
             d2d_send, d2d_recv, ici_send, ici_recv, send_sems, recv_sems, local_sems, b_sems):
        del me_ref
        s = pl.program_id(0)
        t = last - s
        q = t >> 1
        x, y, c = lax.axis_index("x"), lax.axis_index("y"), lax.axis_index("c")
        me = 4 * x + 2 * y + c
        chip = 2 * x + y
        srcs, lands = (go_ref, sc_ref, ga_ref), (land_go, land_sc, land_ga)

        def b_copy(step, ch):
            r0 = ROW_TILE + ch * b_rows if ch else LIVE0
            n = b_rows if ch else b_rows + ROW_TILE - LIVE0
            col = pl.multiple_of(jnp.bitwise_xor(me, last - step) * BLK, BLK)
            return pltpu.make_async_copy(
                b_hbm.at[pl.ds(r0, n), pl.ds(col, BLK)], b_buf.at[step % 2, pl.ds(r0, n)], b_sems.at[step % 2, ch]
            )

        @pl.when(s == 0)
        def _():
            for ch in range(B_CHUNKS):
                b_copy(0, ch).start()

        def peer_at(k):
            return (1 - x if k & 4 else x, 1 - y if k & 2 else y, 1 - c if k & 1 else c)

        def small_copy(a, k):
            px, py, pc = peer_at(k)
            return pltpu.make_async_remote_copy(
                src_ref=srcs[a].at[4 * px + 2 * py + pc] if by_dest[a] else srcs[a],
                dst_ref=lands[a].at[me],
                send_sem=send_sems.at[a * last + k - 1],
                recv_sem=recv_sems.at[a * last + k - 1],
                device_id=(px, py, pc),
                device_id_type=MESH,
            )

        def small_local(a):
            return pltpu.make_async_copy(srcs[a].at[me] if by_dest[a] else srcs[a], lands[a].at[me], local_sems.at[a + 1])

        def d2d_copy(j):
            return pltpu.make_async_remote_copy(
                src_ref=d2d_buf.at[j],
                dst_ref=d2d_land.at[j],
                send_sem=d2d_send.at[j],
                recv_sem=d2d_recv.at[j],
                device_id=(x, y, 1 - c),
                device_id_type=MESH,
            )

        def ici_copy(j, to):
            return pltpu.make_async_remote_copy(
                src_ref=ici_buf.at[j],
                dst_ref=land_in.at[chip],
                send_sem=ici_send.at[j],
                recv_sem=ici_recv.at[j],
                device_id=to,
                device_id_type=MESH,
            )

        def own_copy():
            return pltpu.make_async_copy(ici_buf.at[0], land_in.at[chip], local_sems.at[0])

        @pl.when(s == 0)
        def _():
            sc_ref[...] = jnp.zeros_like(sc_ref)
            for d in range(N_DEV):
                sc_ref[d, SC_META0 : SC_META0 + N_META, :] = dm_ref[:, d * CHUNK : (d + 1) * CHUNK]
                lane0 = (d * cols) % CHUNK
                rows8 = ac_ref[:, d * cols - lane0 : d * cols - lane0 + CHUNK]
                rows8 = pltpu.roll(rows8, CHUNK - lane0, 1) if lane0 else rows8
                sc_ref[d, SC_CONV0 : SC_CONV0 + 3, 0:cols] = rows8[0:3, 0:cols]
            ga_ref[...] = jnp.zeros_like(ga_ref)
            for row0, acc, n in ((GA_N1, ab2_ref, 8), (GA_RG, aret_ref, 4), (GA_FG, af2_ref, 8)):
                for j in range(n):
                    ga_ref[row0 + j : row0 + j + 1, :] = acc[0:1, j * CHUNK : (j + 1) * CHUNK]
            ga_ref[GA_LOSS : GA_LOSS + 1, :] = af2_ref[1:2, 0:CHUNK]
            for a in range(3):
                small_local(a).start()
            for k in range(last, 0, -1):
                for a in range(3):
                    small_copy(a, k).start()

        @pl.when(s < last)
        def _():
            for ch in range(B_CHUNKS):
                b_copy(s + 1, ch).start()

        b_ref = b_buf.at[s % 2]
        b_copy(s, 0).wait()
        blk = jnp.dot(am_ref[...], b_ref[LIVE0:ROW_TILE, :], preferred_element_type=F32)
        for ch in range(B_CHUNKS):
            if ch:
                b_copy(s, ch).wait()
            blk += jnp.dot(
                a_ref[:, ch * b_rows : (ch + 1) * b_rows],
                b_ref[ROW_TILE + ch * b_rows : ROW_TILE + (ch + 1) * b_rows, :],
                preferred_element_type=F32,
            )

        @pl.when((t & 1) == 1)
        def _():
            d2d_buf[q] = blk.astype(BF16)
            d2d_copy(q).start()

        @pl.when((t & 1) == 0)
        def _():
            d2d_copy(q).wait_recv()
            ici_buf[q] = (blk + d2d_land[q].astype(F32)).astype(BF16)

            @pl.when(t != 0)
            def _():
                ici_copy(q, (jnp.bitwise_xor(x, (t >> 2) & 1), jnp.bitwise_xor(y, (t >> 1) & 1), c)).start()

            @pl.when(t == 0)
            def _():
                own_copy().start()
                for j in range(N_CHIPS):
                    d2d_copy(j).wait_send()
                for j in range(N_CHIPS - 1, 0, -1):
                    ici_copy(j, peer_at(2 * j)).wait()
                for k in range(last, 0, -1):
                    for a in range(3):
                        small_copy(a, k).wait()
                for a in range(3):
                    small_local(a).wait()
                own_copy().wait()

    grid_spec = pltpu.PrefetchScalarGridSpec(
        num_scalar_prefetch=1,
        grid=(N_DEV,),
        in_specs=[_VMEM, _VMEM, _HBM, _HBM] + [_VMEM] * 5,
        out_specs=tuple([_HBM] * 4),
        scratch_shapes=[
            pltpu.VMEM((N_CHIPS, D_MODEL, BLK), BF16),
            pltpu.VMEM((N_CHIPS, D_MODEL, BLK), BF16),
            pltpu.VMEM((N_CHIPS, D_MODEL, BLK), BF16),
            pltpu.VMEM((N_DEV, *small_blocks[1]), F32),
            pltpu.VMEM(small_blocks[2], F32),
            pltpu.VMEM((2, n_rows, BLK), dproj.dtype),
            pltpu.SemaphoreType.DMA((N_CHIPS,)),
            pltpu.SemaphoreType.DMA((N_CHIPS,)),
            pltpu.SemaphoreType.DMA((N_CHIPS,)),
            pltpu.SemaphoreType.DMA((N_CHIPS,)),
            pltpu.SemaphoreType.DMA((3 * last,)),
            pltpu.SemaphoreType.DMA((3 * last,)),
            pltpu.SemaphoreType.DMA((4,)),
            pltpu.SemaphoreType.DMA((2, B_CHUNKS)),
        ],
    )
    return pl.pallas_call(
        body,
        name="gw_in_scatter",
        grid_spec=grid_spec,
        out_shape=(jax.ShapeDtypeStruct((N_CHIPS, D_MODEL, BLK), BF16),)
        + tuple(jax.ShapeDtypeStruct((N_DEV, *b), dt) for b, dt in zip(small_blocks, (gw_out_parts.dtype, F32, F32), strict=True)),
        compiler_params=_params(),
    )(me_arr, hnT_tok, hnT_meta, dproj, gw_out_parts, dmeta_rows, acc_conv, acc_b2, acc_ret, acc_f2)


def _adamw(w, g, m, v):
    m = ADAM_B1 * m + (1.0 - ADAM_B1) * g
    v = ADAM_B2 * v + (1.0 - ADAM_B2) * (g * g)
    m_hat = m / (1.0 - ADAM_B1**ADAM_STEP)
    v_hat = v / (1.0 - ADAM_B2**ADAM_STEP)
    delta = -ADAM_LR * (m_hat / (jnp.sqrt(v_hat) + ADAM_EPS) + ADAM_WD * w)
    return delta, m, v


def _small_leaves(meta, conv_w, n1, rg, fg):
    return meta, conv_w, n1.reshape(8, CHUNK), rg.reshape(4, CHUNK), fg.reshape(8, CHUNK)


def _from_small_leaves(meta, conv_w, n1, rg, fg):
    return meta, conv_w, n1.reshape(D_MODEL), rg.reshape(D_RET), fg.reshape(D_MODEL)


ADAM_GRID = 4


def _adamw_all(land_in, land_out, land_sc, land_ga, big, small):
    n_leaf = 5
    lands = (land_in, land_out)
    n_big = len(big)

    def body(*refs):
        big_in, sc_ref, ga_ref = refs[: 4 * n_big], refs[4 * n_big], refs[4 * n_big + 1]
        small_in = refs[4 * n_big + 2 : 4 * n_big + 2 + 3 * n_leaf]
        big_out = refs[4 * n_big + 2 + 3 * n_leaf : 8 * n_big + 2 + 3 * n_leaf]
        small_out = refs[8 * n_big + 2 + 3 * n_leaf :]
        for b in range(n_big):
            p_ref, w_ref, m_ref, v_ref = big_in[4 * b : 4 * b + 4]
            g_ref, d_ref, nm_ref, nv_ref = big_out[4 * b : 4 * b + 4]
            g = p_ref[0].astype(F32)
            for s in range(1, p_ref.shape[0]):
                g = g + p_ref[s].astype(F32)
            g_ref[...] = g
            d_ref[...], nm_ref[...], nv_ref[...] = _adamw(w_ref[...], g, m_ref[...], v_ref[...])

        @pl.when(pl.program_id(0) == 0)
        def _():
            sc, ga = sc_ref[0], ga_ref[0]
            for s in range(1, N_DEV):
                sc = sc + sc_ref[s]
                ga = ga + ga_ref[s]
            grads = (
                sc[SC_META0 : SC_META0 + N_META],
                sc[SC_CONV0 : SC_CONV0 + 3, 0 : D_CONV // N_DEV],
                ga[GA_N1 : GA_N1 + 8],
                ga[GA_RG : GA_RG + 4],
                ga[GA_FG : GA_FG + 8],
            )
            for leaf, g in enumerate(grads):
                d, nm, nv = _adamw(
                    small_in[leaf][...], g, small_in[n_leaf + leaf][...], small_in[2 * n_leaf + leaf][...]
                )
                small_out[leaf][...] = g
                small_out[n_leaf + leaf][...] = d
                small_out[2 * n_leaf + leaf][...] = nm
                small_out[3 * n_leaf + leaf][...] = nv
            small_out[4 * n_leaf][...] = ga[GA_LOSS : GA_LOSS + 1]

    in_specs, out_specs, out_shape, operands = [], [], [], []
    for land, (w, m, v) in zip(lands, big, strict=True):
        n_r, n_c = w.shape
        tr = n_r // ADAM_GRID
        blk = pl.BlockSpec((tr, n_c), lambda i: (i, 0))
        in_specs += [pl.BlockSpec((land.shape[0], tr, n_c), lambda i: (0, i, 0)), blk, blk, blk]
        out_specs += [blk] * 4
        out_shape += [jax.ShapeDtypeStruct((n_r, n_c), F32)] * 4
        operands += [land, w, m, v]
    small_shapes = [jax.ShapeDtypeStruct(a.shape, F32) for a in small[0]] * 4 + [jax.ShapeDtypeStruct((1, CHUNK), F32)]
    out = pl.pallas_call(
        body,
        name="adamw_all",
        grid=(ADAM_GRID,),
        in_specs=in_specs + [_VMEM] * (2 + 3 * n_leaf),
        out_specs=tuple(out_specs + [_VMEM] * len(small_shapes)),
        out_shape=tuple(out_shape + small_shapes),
        compiler_params=_params(),
    )(*operands, land_sc, land_ga, *small[0], *small[1], *small[2])
    bigs = tuple(out[4 * b : 4 * b + 4] for b in range(n_big))
    rest = out[4 * n_big :]
    return bigs, tuple(rest[k * n_leaf : (k + 1) * n_leaf] for k in range(4)), rest[4 * n_leaf]


def kernel(x, meta, norm1_g, w_in, conv_w, ret_norm_g, w_out, final_g, loss_target, m_meta, m_norm1_g, m_w_in, m_conv_w, m_ret_norm_g, m_w_out, m_final_g, v_meta, v_norm1_g, v_w_in, v_conv_w, v_ret_norm_g, v_w_out, v_final_g):
    seq = x.shape[1]
    assert x.shape == (1, seq, D_MODEL) and seq % ROW_TILE == 0
    n_tiles = seq // ROW_TILE + 1
    x2, t2 = x[0], loss_target[0]

    small_w = _small_leaves(meta, conv_w, norm1_g, ret_norm_g, final_g)
    g1 = norm1_g.reshape(1, D_MODEL)
    hn_tok, hnT_tok, w3, meta_chunk, cw8 = _gather_weights_and_norm(x2, g1, w_in, meta, conv_w)

    rot = _rotary_tables(n_tiles, ROW_TILE)
    dec, _, zeta, xi = _decay_tables()
    gr = ret_norm_g.reshape(1, D_RET)
    gf = final_g.reshape(1, D_MODEL)

    hnT_meta, proj, conv_out, conv, o, states, wo3 = _f1(hn_tok, meta_chunk, g1, w3, cw8, rot, dec, zeta, xi, w_out)
    wout = wo3.reshape(D_MODEL, D_MODEL)
    dh2, dco, drg, do, gw_out, acc_f2, acc_ret = _f2(proj, o, conv_out, x2, t2, wout, gr, gf)
    dproj, grad_x2, dmeta_rows, acc_b2, acc_conv = _b2(
        drg, dco, do, proj, conv, states, w3, cw8, x2, meta_chunk, dh2, g1, _decay_tables(B1_CHUNK), zeta, rot
    )
    gw_out_parts = gw_out.reshape(N_DEV, D_MODEL // N_DEV, D_MODEL)

    me_arr = (4 * lax.axis_index("x") + 2 * lax.axis_index("y") + lax.axis_index("c")).astype(jnp.int32).reshape(1)
    land_in, land_out, land_sc, land_ga = _gw_in_scatter(
        hnT_tok, hnT_meta, dproj, gw_out_parts, dmeta_rows, acc_conv, acc_b2, acc_ret, acc_f2, me_arr
    )

    (w_in_out, w_out_out), small_out, loss_row = _adamw_all(
        land_in,
        land_out,
        land_sc,
        land_ga,
        ((w_in, m_w_in, v_w_in), (w_out, m_w_out, v_w_out)),
        (
            small_w,
            _small_leaves(m_meta, m_conv_w, m_norm1_g, m_ret_norm_g, m_final_g),
            _small_leaves(v_meta, v_conv_w, v_norm1_g, v_ret_norm_g, v_final_g),
        ),
    )
    g_w_in, d_w_in, nm_w_in, nv_w_in = w_in_out
    g_w_out, d_w_out, nm_w_out, nv_w_out = w_out_out
    loss = loss_row[0, 0]
    grad_x = grad_x2.reshape(1, seq, D_MODEL)

    def leaves(w_in_leaf, w_out_leaf, small_leaves):
        meta_leaf, conv_leaf, n1_leaf, rg_leaf, fg_leaf = _from_small_leaves(*small_leaves)
        return (meta_leaf, n1_leaf, w_in_leaf, conv_leaf, rg_leaf, w_out_leaf, fg_leaf)

    return (
        loss,
        grad_x,
        *leaves(g_w_in, g_w_out, small_out[0]),
        *leaves(d_w_in, d_w_out, small_out[1]),
        *leaves(nm_w_in, nm_w_out, small_out[2]),
        *leaves(nv_w_in, nv_w_out, small_out[3]),
    )
```

```python
import math

import jax
import jax.numpy as jnp
import numpy as np
from jax import lax
from jax.experimental import pallas as pl
from jax.experimental.pallas import tpu as pltpu

F32 = jnp.float32
BF16 = jnp.bfloat16

N_DEV = 8
N_CHIPS = 4
D_MODEL = 1024
N_META = 16
CHUNK = 128
D_CONV = 512
D_RET = 512
N_HEADS = 4
HEAD_DIM = 128
N_PROJ = 8
BLK = 512
ROPE_BASE = 10000.0
EPS = 1e-6
Q_SCALE = HEAD_DIM ** -0.5
LOG_G = tuple(math.log(1.0 - 2.0 ** (-5.0 - h)) for h in range(N_HEADS))
CHUNK_DECAY = tuple(math.exp(CHUNK * lg) for lg in LOG_G)

ADAM_LR = 0.001
ADAM_B1 = 0.9
ADAM_B2 = 0.999
ADAM_EPS = 1e-08
ADAM_WD = 0.01
ADAM_STEP = 10

ROW_TILE = 512
PAD_ROWS = ROW_TILE - N_META
LIVE0 = ROW_TILE - CHUNK
B1_CHUNK = 256
B2_ROWS = ROW_TILE // 2
VMEM_LIMIT = 56 * 1024 * 1024

SC_META0, SC_CONV0 = 0, 16
GA_N1, GA_RG, GA_FG, GA_LOSS = 0, 8, 16, 24

NT = (((1,), (1,)), ((), ()))
TN = (((0,), (0,)), ((), ()))
MESH = pl.DeviceIdType.MESH

_VMEM = pl.BlockSpec(memory_space=pltpu.VMEM)
_HBM = pl.BlockSpec(memory_space=pltpu.HBM)


def _params(n_axes=1):
    return pltpu.CompilerParams(dimension_semantics=("arbitrary",) * n_axes, vmem_limit_bytes=VMEM_LIMIT)


def _sigmoid(x):
    return 0.5 * jnp.tanh(0.5 * x) + 0.5


def _decay_tables(chunk=CHUNK):
    idx = np.arange(chunk, dtype=np.float64)
    diff = idx[:, None] - idx[None, :]
    dec = np.stack([np.where(diff >= 0, np.exp(diff * lg), 0.0) for lg in LOG_G])
    zeta = np.stack([np.exp((chunk - 1 - idx) * lg) for lg in LOG_G])
    xi = np.stack([np.exp((idx + 1.0) * lg) for lg in LOG_G])
    ones = np.ones((1, 1, HEAD_DIM))
    return (
        jnp.asarray(dec, F32),
        jnp.asarray(dec.transpose(0, 2, 1), F32),
        jnp.asarray(zeta[:, :, None] * ones, F32),
        jnp.asarray(xi[:, :, None] * ones, F32),
    )


def _rotary_tables(n_tiles, tm):
    half = HEAD_DIM // 2
    freqs = (1.0 / (np.float32(ROPE_BASE) ** (np.arange(half, dtype=np.float32) / np.float32(half)))).astype(np.float64)
    sign = np.concatenate([-np.ones(half), np.ones(half)])
    two = lambda a: np.concatenate([a, a], axis=1)
    base = two((np.arange(n_tiles, dtype=np.float64) * tm - PAD_ROWS)[:, None] * freqs[None, :])
    off = two(np.arange(tm, dtype=np.float64)[:, None] * freqs[None, :])
    as32 = lambda a: jnp.asarray(a, F32)
    return as32(np.cos(base)), as32(np.sin(base) * sign), as32(np.cos(off)), as32(np.sin(off) * sign)


def _tile_rotary(ca_ref, sa_ref, cb_ref, sb_ref, tile):
    ca, sa = ca_ref[pl.ds(tile, 1), :], sa_ref[pl.ds(tile, 1), :]
    cb, sb = cb_ref[...], sb_ref[...]
    return ca * cb - sa * sb, sa * cb + ca * sb


def _rot(t, cos2, sin2):
    return t * cos2 + pltpu.roll(t, HEAD_DIM // 2, 1) * sin2


def _rot_bwd(d, cos2, sin2):
    return d * cos2 + pltpu.roll(d * sin2, HEAD_DIM // 2, 1)


def _token_tile(i):
    return jnp.maximum(i - 1, 0)


GATHER_PAIRS = 7


def _two_level_gather(src_ref, out_ref, send_sems, recv_sems, local_sem, base=0):
    x, y, c = lax.axis_index("x"), lax.axis_index("y"), lax.axis_index("c")
    me, sibling = (x, y, c), (x, y, 1 - c)
    xnb, ynb, diag = (1 - x, y), (x, 1 - y), (1 - x, 1 - y)
    relayed = (jnp.bitwise_xor(x, 1 - c), jnp.bitwise_xor(y, c))
    other = (jnp.bitwise_xor(x, c), jnp.bitwise_xor(y, 1 - c))

    def copy(k, block, to, src=None):
        dst = out_ref.at[4 * block[0] + 2 * block[1] + block[2]]
        return pltpu.make_async_remote_copy(
            src_ref=dst if src is None else src,
            dst_ref=dst,
            send_sem=send_sems.at[base + k],
            recv_sem=recv_sems.at[base + k],
            device_id=to,
            device_id_type=MESH,
        )

    mine = pltpu.make_async_copy(src_ref, out_ref.at[4 * x + 2 * y + c], local_sem)
    first = [copy(1, me, (*xnb, c), src=src_ref), copy(2, me, (*ynb, c), src=src_ref), copy(0, me, sibling, src=src_ref)]
    relay = copy(3, (*relayed, c), (*other, c))
    passed = [copy(4 + j, (*chip, c), sibling) for j, chip in enumerate((xnb, ynb, diag))]

    def start():
        mine.start()
        for cp in first:
            cp.start()

    def forward():
        copy(1 + c, (*relayed, c), me).wait_recv()
        relay.start()
        copy(2 - c, (*other, c), me).wait_recv()
        passed[0].start()
        passed[1].start()

    def forward_relayed():
        copy(3, (*diag, c), me).wait_recv()
        passed[2].start()

    def finish():
        copy(0, sibling, me).wait_recv()
        for j, chip in enumerate((xnb, ynb, diag)):
            copy(4 + j, (*chip, 1 - c), me).wait_recv()
        for cp in first + [relay] + passed:
            cp.wait_send()
        mine.wait()

    return start, forward, forward_relayed, finish


W_IN_CHUNKS = 4


def _gather_weights_and_norm(x2, g1, w_shard, meta, conv_w):
    per = GATHER_PAIRS
    n_ch = W_IN_CHUNKS
    rows = w_shard.shape[0] // n_ch
    tm = ROW_TILE
    nt = x2.shape[0] // tm
    cols = D_CONV // N_DEV
    pack_shape = (SC_CONV0 + 8, CHUNK)
    starts = [0] * (n_ch + 1)
    forwards = [min(5 + 2 * p, nt - 1) for p in range(n_ch)] + [min(12, nt - 1)]
    relayeds = [min(12 + p, nt - 1) for p in range(n_ch)] + [nt - 1]
    finishes = [nt - 1] * (n_ch + 1)

    def body(x_ref, g_ref, w_ref, meta_ref, conv_ref, hn_ref, hnT_ref, w_all_ref, mc_ref, cw_ref,
             w_bf, small_ref, small_all_ref, send_sems, recv_sems, local_sems):
        i = pl.program_id(0)

        @pl.when(i == 0)
        def _():
            w_bf[...] = w_ref[...].astype(BF16)
            small_ref[...] = jnp.zeros_like(small_ref)
            small_ref[SC_META0 : SC_META0 + N_META, :] = meta_ref[...]
            small_ref[SC_CONV0 : SC_CONV0 + 3, 0:cols] = conv_ref[...]

        parts = [
            _two_level_gather(
                w_bf.at[pl.ds(ch * rows, rows)],
                w_all_ref.at[:, pl.ds(ch * rows, rows)],
                send_sems,
                recv_sems,
                local_sems.at[ch],
                ch * per,
            )
            for ch in range(n_ch)
        ]
        parts.append(_two_level_gather(small_ref, small_all_ref, send_sems, recv_sems, local_sems.at[n_ch], n_ch * per))
        def run(phase, steps):
            for part, step in zip(parts, steps, strict=True):
                pl.when(i == step)(part[phase])

        run(0, starts)
        run(1, forwards)
        h = x_ref[...]
        r = lax.rsqrt(jnp.mean(h * h, axis=-1, keepdims=True) + EPS)
        hn = (h * r * g_ref[...]).astype(BF16)
        hn_ref[...] = hn
        hnT_ref[...] = hn.T
        run(2, relayeds)
        run(3, finishes)

        @pl.when(i == nt - 1)
        def _():
            mc_ref[0 : CHUNK - N_META, :] = jnp.zeros((CHUNK - N_META, D_MODEL), F32)
            cw_ref[...] = jnp.zeros_like(cw_ref)
            for d in range(N_DEV):
                mc_ref[CHUNK - N_META : CHUNK, d * CHUNK : (d + 1) * CHUNK] = small_all_ref[d, SC_META0 : SC_META0 + N_META, :]
                lane0 = (d * cols) % CHUNK
                rows8 = small_all_ref[d, SC_CONV0 : SC_CONV0 + 8, :]
                rows8 = pltpu.roll(rows8, lane0, 1) if lane0 else rows8
                cw_ref[0:3, d * cols : (d + 1) * cols] = rows8[0:3, lane0 : lane0 + cols]

    const = lambda r, c: pl.BlockSpec((r, c), lambda i: (0, 0))
    return pl.pallas_call(
        body,
        name="gather_weights_norm",
        grid=(nt,),
        out_shape=(
            jax.ShapeDtypeStruct(x2.shape, BF16),
            jax.ShapeDtypeStruct(x2.shape[::-1], BF16),
            jax.ShapeDtypeStruct((N_DEV, *w_shard.shape), BF16),
            jax.ShapeDtypeStruct((CHUNK, D_MODEL), F32),
            jax.ShapeDtypeStruct((8, D_CONV), F32),
        ),
        in_specs=[pl.BlockSpec((tm, D_MODEL), lambda i: (i, 0)), _VMEM, _VMEM, _VMEM, _VMEM],
        out_specs=(
            pl.BlockSpec((tm, D_MODEL), lambda i: (i, 0)),
            pl.BlockSpec((D_MODEL, tm), lambda i: (0, i)),
            _HBM,
            const(CHUNK, D_MODEL),
            const(8, D_CONV),
        ),
        scratch_shapes=[
            pltpu.VMEM(w_shard.shape, BF16),
            pltpu.VMEM(pack_shape, F32),
            pltpu.VMEM((N_DEV, *pack_shape), F32),
            pltpu.SemaphoreType.DMA(((n_ch + 1) * per,)),
            pltpu.SemaphoreType.DMA(((n_ch + 1) * per,)),
            pltpu.SemaphoreType.DMA((n_ch + 1,)),
        ],
        compiler_params=_params(),
    )(x2, g1, w_shard, meta, conv_w)


def _f1(hn_tok, meta_chunk, g1, w3, cw8, rot, dec, zeta, xi, wo_shard):
    tm = ROW_TILE
    nt = hn_tok.shape[0] // tm + 1
    n_rows = nt * tm
    nct = tm // CHUNK

    def body(hn_ref, mc_ref, g_ref, w_ref, cw_ref, ca_ref, sa_ref, cb_ref, sb_ref, dec_ref, zeta_ref, xi_ref, wo_ref,
             hnT_ref, pr_ref, co_ref, cv_ref, o_ref, st_ref, wo_all_ref,
             halo, state, wo_bf, send_sems, recv_sems, local_sem):
        i = pl.program_id(0)

        @pl.when(i == 0)
        def _():
            wo_bf[...] = wo_ref[...].astype(BF16)

        wo_phases = _two_level_gather(wo_bf, wo_all_ref, send_sems, recv_sems, local_sem)
        for step, phase in zip((0, nt // 3, 2 * nt // 3, nt - 1), wo_phases, strict=True):
            pl.when(i == step)(phase)

        def work(hn, r0):
            rs = slice(r0, tm)
            n = tm - r0

            def proj(j):
                return jnp.dot(hn, w_ref[j], preferred_element_type=F32)

            cos_all, sin_all = _tile_rotary(ca_ref, sa_ref, cb_ref, sb_ref, i)
            cos_t, sin_t = cos_all[rs, :], sin_all[rs, :]
            q, k, v = proj(4), proj(5), proj(6).astype(BF16)
            pr_ref[rs, 6 * BLK : 7 * BLK] = v
            chunks = range(r0 // CHUNK, tm // CHUNK)
            heads = []
            for hd in range(N_HEADS):
                cs = slice(hd * HEAD_DIM, (hd + 1) * HEAD_DIM)
                qh = (_rot(q[:, cs], cos_t, sin_t) * Q_SCALE).astype(BF16)
                kh = _rot(k[:, cs], cos_t, sin_t).astype(BF16)
                vh = v[:, cs]
                pr_ref[rs, 4 * BLK + hd * HEAD_DIM : 4 * BLK + (hd + 1) * HEAD_DIM] = qh
                pr_ref[rs, 5 * BLK + hd * HEAD_DIM : 5 * BLK + (hd + 1) * HEAD_DIM] = kh
                upd = {}
                for c in chunks:
                    cr = slice(c * CHUNK - r0, (c + 1) * CHUNK - r0)
                    kz = (kh[cr].astype(F32) * zeta_ref[hd]).astype(BF16)
                    upd[c] = lax.dot_general(kz, vh[cr], TN, preferred_element_type=F32)
                st = state[hd]
                outs = []
                for c in chunks:
                    cr = slice(c * CHUNK - r0, (c + 1) * CHUNK - r0)
                    qc, kc, vc = qh[cr], kh[cr], vh[cr]
                    st_bf = st.astype(BF16)
                    st_ref[c, hd] = st_bf
                    s = lax.dot_general(qc, kc, NT, preferred_element_type=F32) * dec_ref[hd]
                    inner = jnp.dot(s.astype(BF16), vc, preferred_element_type=F32)
                    qx = (qc.astype(F32) * xi_ref[hd]).astype(BF16)
                    outs.append(inner + jnp.dot(qx, st_bf, preferred_element_type=F32))
                    st = CHUNK_DECAY[hd] * st + upd[c]
                state[hd] = st
                heads.append(jnp.concatenate(outs, axis=0) if len(outs) > 1 else outs[0])
            o_ref[rs, :] = jnp.concatenate(heads, axis=1)

            cx, cb, cc, cg = proj(0), proj(1), proj(2), proj(3)
            u = cc * cx
            rows = lax.broadcasted_iota(jnp.int32, u.shape, 0)
            hl = halo[...]
            u1 = jnp.where(rows == 0, hl[7:8], pltpu.roll(u, 1, 0))
            u2 = jnp.where(rows == 0, hl[6:7], jnp.where(rows == 1, hl[7:8], pltpu.roll(u, 2, 0)))
            halo[...] = u[n - 8 : n]
            cw = cw_ref[...]
            conv = cw[0:1] * u2 + cw[1:2] * u1 + cw[2:3] * u
            co = (cb * conv * (cg * _sigmoid(cg))).astype(BF16)
            co_ref[rs, :] = co
            cv_ref[rs, :] = conv.astype(BF16)
            pr_ref[rs, 0 * BLK : 1 * BLK] = cx.astype(BF16)
            pr_ref[rs, 1 * BLK : 2 * BLK] = cb.astype(BF16)
            pr_ref[rs, 2 * BLK : 3 * BLK] = cc.astype(BF16)
            pr_ref[rs, 3 * BLK : 4 * BLK] = cg.astype(BF16)
            pr_ref[rs, 7 * BLK : 8 * BLK] = proj(7).astype(BF16)

        @pl.when(i == 0)
        def _():
            halo[...] = jnp.zeros_like(halo)
            state[...] = jnp.zeros_like(state)
            h = mc_ref[...]
            r = lax.rsqrt(jnp.mean(h * h, axis=-1, keepdims=True) + EPS)
            hn = (h * r * g_ref[...]).astype(BF16)
            hnT_ref[...] = hn.T
            work(hn, LIVE0)

        @pl.when(i > 0)
        def _():
            work(hn_ref[...], 0)

    row = lambda w: pl.BlockSpec((tm, w), lambda i: (i, 0))
    return pl.pallas_call(
        body,
        name="f1_inproj_conv",
        grid=(nt,),
        in_specs=[pl.BlockSpec((tm, D_MODEL), lambda i: (_token_tile(i), 0))] + [_VMEM] * 12,
        out_specs=(
            pl.BlockSpec((D_MODEL, CHUNK), lambda i: (0, 0)),
            row(N_PROJ * BLK),
            row(D_CONV),
            row(D_CONV),
            row(D_RET),
            pl.BlockSpec((nct, N_HEADS, HEAD_DIM, HEAD_DIM), lambda i: (i, 0, 0, 0)),
            _HBM,
        ),
        out_shape=(
            jax.ShapeDtypeStruct((D_MODEL, CHUNK), BF16),
            jax.ShapeDtypeStruct((n_rows, N_PROJ * BLK), BF16),
            jax.ShapeDtypeStruct((n_rows, D_CONV), BF16),
            jax.ShapeDtypeStruct((n_rows, D_CONV), BF16),
            jax.ShapeDtypeStruct((n_rows, D_RET), F32),
            jax.ShapeDtypeStruct((n_rows // CHUNK, N_HEADS, HEAD_DIM, HEAD_DIM), BF16),
            jax.ShapeDtypeStruct((N_DEV, *wo_shard.shape), BF16),
        ),
        scratch_shapes=[
            pltpu.VMEM((8, D_CONV), F32),
            pltpu.VMEM((N_HEADS, HEAD_DIM, HEAD_DIM), F32),
            pltpu.VMEM(wo_shard.shape, BF16),
            pltpu.SemaphoreType.DMA((GATHER_PAIRS,)),
            pltpu.SemaphoreType.DMA((GATHER_PAIRS,)),
            pltpu.SemaphoreType.DMA(()),
        ],
        compiler_params=_params(),
    )(hn_tok, meta_chunk, g1, w3, cw8, *rot, dec, zeta, xi, wo_shard)


def _group_norm(o):
    ys, rs = [], []
    for hd in range(N_HEADS):
        oh = o[:, hd * HEAD_DIM : (hd + 1) * HEAD_DIM]
        xc = oh - jnp.mean(oh, axis=-1, keepdims=True)
        rstd = lax.rsqrt(jnp.mean(xc * xc, axis=-1, keepdims=True) + EPS)
        ys.append(xc * rstd)
        rs.append(jnp.broadcast_to(rstd, oh.shape))
    return jnp.concatenate(ys, axis=1), jnp.concatenate(rs, axis=1)


def _f2(proj, o, conv_out, x2, t2, wout, gr, gf):
    n_rows = proj.shape[0]
    tm = ROW_TILE
    nt = n_rows // tm

    def body(o_ref, rg_ref, co_ref, x_ref, t_ref, wo_ref, gr_ref, gf_ref,
             dh2_ref, dco_ref, drg_ref, do_ref, gw_ref, acc_ref, accr_ref, gacc):
        i = pl.program_id(0)

        @pl.when(i == 0)
        def _():
            acc_ref[...] = jnp.zeros_like(acc_ref)
            accr_ref[...] = jnp.zeros_like(accr_ref)
            gacc[...] = jnp.zeros_like(gacc)
            dh2_ref[...] = jnp.zeros_like(dh2_ref)
            dco_ref[...] = jnp.zeros_like(dco_ref)
            drg_ref[...] = jnp.zeros_like(drg_ref)
            do_ref[...] = jnp.zeros_like(do_ref)

        @pl.when(i > 0)
        def _():
            yh, rstd = _group_norm(o_ref[...])
            rg = rg_ref[...].astype(F32)
            sg = _sigmoid(rg)
            silu = rg * sg
            grv = gr_ref[...]
            ro = (yh * grv * silu).astype(BF16)
            h2 = (
                x_ref[...]
                + jnp.dot(co_ref[...], wo_ref[0:D_CONV], preferred_element_type=F32)
                + jnp.dot(ro, wo_ref[D_CONV:], preferred_element_type=F32)
            )
            r2 = lax.rsqrt(jnp.mean(h2 * h2, axis=-1, keepdims=True) + EPS)
            yn = h2 * r2
            gfv = gf_ref[...]
            err = yn * gfv - t_ref[...]
            tile_loss = jnp.sum(jnp.sum(err * err, axis=-1, keepdims=True), axis=0, keepdims=True) * (0.5 / D_MODEL)
            acc_ref[0:1, :] += jnp.sum(err * yn, axis=0, keepdims=True) * (1.0 / D_MODEL)
            acc_ref[1:2, :] += tile_loss
            dyn = err * (gfv * (1.0 / D_MODEL))
            dh2 = r2 * (dyn - yn * jnp.mean(dyn * yn, axis=-1, keepdims=True))
            dh2_ref[...] = dh2
            dh2_bf = dh2.astype(BF16)
            dmix = lax.dot_general(dh2_bf, wo_ref[...], NT, preferred_element_type=F32)
            dco_ref[...] = dmix[:, :D_CONV].astype(BF16)
            dro = dmix[:, D_CONV:]
            dsilu = sg * (1.0 + rg * (1.0 - sg))
            drg_ref[...] = (dro * (yh * grv) * dsilu).astype(BF16)
            dret = dro * silu
            accr_ref[0:1, :] += jnp.sum(dret * yh, axis=0, keepdims=True)
            dyh = dret * grv
            for hd in range(N_HEADS):
                cs = slice(hd * HEAD_DIM, (hd + 1) * HEAD_DIM)
                a, b = dyh[:, cs], yh[:, cs]
                do_ref[:, cs] = (
                    rstd[:, cs]
                    * (a - jnp.mean(a, axis=-1, keepdims=True) - b * jnp.mean(a * b, axis=-1, keepdims=True))
                ).astype(BF16)
            gacc[0:D_CONV, :] += lax.dot_general(co_ref[...], dh2_bf, TN, preferred_element_type=F32)
            gacc[D_CONV:, :] += lax.dot_general(ro, dh2_bf, TN, preferred_element_type=F32)

        @pl.when(i == nt - 1)
        def _():
            gw_ref[...] = gacc[...].astype(BF16)

    row = lambda w, j=0: pl.BlockSpec((tm, w), lambda i: (i, j))
    tok = pl.BlockSpec((tm, D_MODEL), lambda i: (_token_tile(i), 0))
    return pl.pallas_call(
        body,
        name="f2_out_loss",
        grid=(nt,),
        in_specs=[row(D_RET), row(BLK, 7), row(D_CONV), tok, tok] + [_VMEM] * 3,
        out_specs=(
            row(D_MODEL),
            row(D_CONV),
            row(BLK),
            row(D_RET),
            pl.BlockSpec((D_MODEL, D_MODEL), lambda i: (0, 0)),
            pl.BlockSpec((8, D_MODEL), lambda i: (0, 0)),
            pl.BlockSpec((8, D_RET), lambda i: (0, 0)),
        ),
        out_shape=(
            jax.ShapeDtypeStruct((n_rows, D_MODEL), F32),
            jax.ShapeDtypeStruct((n_rows, D_CONV), BF16),
            jax.ShapeDtypeStruct((n_rows, BLK), BF16),
            jax.ShapeDtypeStruct((n_rows, D_RET), BF16),
            jax.ShapeDtypeStruct((D_MODEL, D_MODEL), BF16),
            jax.ShapeDtypeStruct((8, D_MODEL), F32),
            jax.ShapeDtypeStruct((8, D_RET), F32),
        ),
        scratch_shapes=[pltpu.VMEM((D_MODEL, D_MODEL), F32)],
        compiler_params=_params(),
    )(o, proj, conv_out, x2, t2, wout, gr, gf)


def _b2(drg, dco, do, proj, conv, states, w3, cw8, x2, meta_chunk, dh2, g1, tables, zeta0, rot):
    n_rows = drg.shape[0]
    tm = ROW_TILE
    nt = n_rows // tm
    last = N_PROJ - 1
    chunk = B1_CHUNK
    nct = tm // chunk
    decay = tuple(math.exp(chunk * lg) for lg in LOG_G)
    Q0, K0, V0 = 4 * BLK, 5 * BLK, 6 * BLK

    def body(drg_ref, dco_ref, do_ref, pr_ref, cv_ref, st_ref, w_ref, cw_ref, x_ref, mc_ref, dh2_ref, g_ref,
             dec_ref, dect_ref, zeta_ref, xi_ref, zeta0_ref, ca_ref, sa_ref, cb_ref, sb_ref,
             dpc_ref, gx_ref, dm_ref, acc_ref, accc_ref, halo, dstate):
        i = pl.program_id(0)
        tile = nt - 1 - i
        cos_all, sin_all = _tile_rotary(ca_ref, sa_ref, cb_ref, sb_ref, tile)

        @pl.when(i == 0)
        def _():
            acc_ref[...] = jnp.zeros_like(acc_ref)
            accc_ref[...] = jnp.zeros_like(accc_ref)
            halo[...] = jnp.zeros_like(halo)
            dstate[...] = jnp.zeros_like(dstate)

        def retention_backward():
            for hd in range(N_HEADS):
                cs = slice(hd * HEAD_DIM, (hd + 1) * HEAD_DIM)
                qs, ks, vs = (slice(c0 + hd * HEAD_DIM, c0 + (hd + 1) * HEAD_DIM) for c0 in (Q0, K0, V0))
                dupd = {}
                for c in range(nct):
                    rs = slice(c * chunk, (c + 1) * chunk)
                    qx = (pr_ref[rs, qs].astype(F32) * xi_ref[hd]).astype(BF16)
                    dupd[c] = lax.dot_general(qx, do_ref[rs, cs], TN, preferred_element_type=F32)
                dst = dstate[hd]
                for c in reversed(range(nct)):
                    rs = slice(c * chunk, (c + 1) * chunk)
                    cos_t, sin_t = cos_all[rs, :], sin_all[rs, :]
                    q, k, v, do = pr_ref[rs, qs], pr_ref[rs, ks], pr_ref[rs, vs], do_ref[rs, cs]
                    st_bf = st_ref[c * (chunk // CHUNK), hd]
                    dst_bf = dst.astype(BF16)
                    zt, xt = zeta_ref[hd], xi_ref[hd]
                    sT = (lax.dot_general(k, q, NT, preferred_element_type=F32) * dect_ref[hd]).astype(BF16)
                    dsT = (lax.dot_general(v, do, NT, preferred_element_type=F32) * dect_ref[hd]).astype(BF16)
                    ds = (lax.dot_general(do, v, NT, preferred_element_type=F32) * dec_ref[hd]).astype(BF16)
                    kz = (k.astype(F32) * zt).astype(BF16)
                    dv = jnp.dot(sT, do, preferred_element_type=F32) + jnp.dot(kz, dst_bf, preferred_element_type=F32)
                    dq = jnp.dot(ds, k, preferred_element_type=F32) + xt * lax.dot_general(
                        do, st_bf, NT, preferred_element_type=F32
                    )
                    dk = jnp.dot(dsT, q, preferred_element_type=F32) + zt * lax.dot_general(
                        v, dst_bf, NT, preferred_element_type=F32
                    )
                    dst = decay[hd] * dst + dupd[c]
                    dpc_ref[rs, qs] = _rot_bwd(dq * Q_SCALE, cos_t, sin_t).astype(BF16)
                    dpc_ref[rs, ks] = _rot_bwd(dk, cos_t, sin_t).astype(BF16)
                    dpc_ref[rs, vs] = dv.astype(BF16)
                dstate[hd] = dst

        def state_backward(rs):
            cos_t, sin_t = cos_all[rs, :], sin_all[rs, :]
            for hd in range(N_HEADS):
                qs, ks, vs = (slice(c0 + hd * HEAD_DIM, c0 + (hd + 1) * HEAD_DIM) for c0 in (Q0, K0, V0))
                k, v = pr_ref[rs, ks], pr_ref[rs, vs]
                dst_bf = dstate[hd].astype(BF16)
                zt = zeta0_ref[hd]
                kz = (k.astype(F32) * zt).astype(BF16)
                dv = jnp.dot(kz, dst_bf, preferred_element_type=F32)
                dk = zt * lax.dot_general(v, dst_bf, NT, preferred_element_type=F32)
                dpc_ref[rs, qs] = jnp.zeros((tm - LIVE0, HEAD_DIM), BF16)
                dpc_ref[rs, ks] = _rot_bwd(dk, cos_t, sin_t).astype(BF16)
                dpc_ref[rs, vs] = dv.astype(BF16)

        def work(h, r0, others, r1=tm):
            rs = slice(r0, r1)
            n = r1 - r0
            dco = dco_ref[rs, :].astype(F32)
            cx = pr_ref[rs, 0 * BLK : 1 * BLK].astype(F32)
            cb = pr_ref[rs, 1 * BLK : 2 * BLK].astype(F32)
            cc = pr_ref[rs, 2 * BLK : 3 * BLK].astype(F32)
            cg = pr_ref[rs, 3 * BLK : 4 * BLK].astype(F32)
            conv = cv_ref[rs, :].astype(F32)
            sg = _sigmoid(cg)
            silu, dsilu = cg * sg, sg * (1.0 + cg * (1.0 - sg))
            t = dco * cb
            dcb = (dco * conv * silu).astype(BF16)
            dcg = (t * conv * dsilu).astype(BF16)
            dconv = t * silu
            rows = lax.broadcasted_iota(jnp.int32, dconv.shape, 0)
            hl = halo[...]
            dc1 = jnp.where(rows == n - 1, hl[0:1], pltpu.roll(dconv, n - 1, 0))
            dc2 = jnp.where(rows == n - 2, hl[0:1], jnp.where(rows == n - 1, hl[1:2], pltpu.roll(dconv, n - 2, 0)))
            halo[...] = dconv[0:8]
            cw = cw_ref[...]
            du = cw[2:3] * dconv + cw[1:2] * dc1 + cw[0:1] * dc2
            u = cc * cx
            accc_ref[0:1, :] += jnp.sum(u * dc2, axis=0, keepdims=True)
            accc_ref[1:2, :] += jnp.sum(u * dc1, axis=0, keepdims=True)
            accc_ref[2:3, :] += jnp.sum(u * dconv, axis=0, keepdims=True)
            dcx = (du * cc).astype(BF16)
            dcc = (du * cx).astype(BF16)
            dhn = None
            for blk, ref, c0 in others:
                d = lax.dot_general(ref[rs, c0 : c0 + BLK], w_ref[blk], NT, preferred_element_type=F32)
                dhn = d if dhn is None else dhn + d
            for blk, d in ((1, dcb), (3, dcg), (0, dcx), (2, dcc)):
                dpc_ref[rs, blk * BLK : (blk + 1) * BLK] = d
                dhn += lax.dot_general(d, w_ref[blk], NT, preferred_element_type=F32)
            r = lax.rsqrt(jnp.mean(h * h, axis=-1, keepdims=True) + EPS)
            hh = h * r
            acc_ref[0:1, :] += jnp.sum(dhn * hh, axis=0, keepdims=True)
            dg = dhn * g_ref[...]
            return r * (dg - hh * jnp.mean(dg * hh, axis=-1, keepdims=True))

        @pl.when(tile == 0)
        def _():
            state_backward(slice(LIVE0, tm))
            dpc_ref[LIVE0:tm, last * BLK :] = jnp.zeros((tm - LIVE0, BLK), BF16)
            dh = work(mc_ref[...], LIVE0, ((5, dpc_ref, K0), (6, dpc_ref, V0)))
            dm_ref[...] = dh[CHUNK - N_META : CHUNK]

        @pl.when(tile > 0)
        def _():
            retention_backward()
            dpc_ref[:, last * BLK :] = drg_ref[...]
            others = ((last, drg_ref, 0), (4, dpc_ref, Q0), (5, dpc_ref, K0), (6, dpc_ref, V0))
            for r0 in range(tm - B2_ROWS, -1, -B2_ROWS):
                r1 = r0 + B2_ROWS
                gx_ref[r0:r1, :] = dh2_ref[r0:r1, :] + work(x_ref[r0:r1, :], r0, others, r1)

    seven = last * BLK
    full = N_PROJ * BLK

    rev = lambda w, j=0: pl.BlockSpec((tm, w), lambda i: (nt - 1 - i, j))
    tok = pl.BlockSpec((tm, D_MODEL), lambda i: (_token_tile(nt - 1 - i), 0))
    const = lambda r, c: pl.BlockSpec((r, c), lambda i: (0, 0))
    return pl.pallas_call(
        body,
        name="b2_dconv_dh",
        grid=(nt,),
        in_specs=[
            rev(BLK),
            rev(D_CONV),
            rev(D_RET),
            rev(seven),
            rev(D_CONV),
            pl.BlockSpec((tm // CHUNK, N_HEADS, HEAD_DIM, HEAD_DIM), lambda i: (nt - 1 - i, 0, 0, 0)),
            _VMEM,
            _VMEM,
            tok,
            _VMEM,
            rev(D_MODEL),
        ]
        + [_VMEM] * 10,
        out_specs=(rev(full), tok, const(N_META, D_MODEL), const(8, D_MODEL), const(8, D_CONV)),
        out_shape=(
            jax.ShapeDtypeStruct((n_rows, full), BF16),
            jax.ShapeDtypeStruct(x2.shape, F32),
            jax.ShapeDtypeStruct((N_META, D_MODEL), F32),
            jax.ShapeDtypeStruct((8, D_MODEL), F32),
            jax.ShapeDtypeStruct((8, D_CONV), F32),
        ),
        scratch_shapes=[pltpu.VMEM((8, D_CONV), F32), pltpu.VMEM((N_HEADS, HEAD_DIM, HEAD_DIM), F32)],
        compiler_params=_params(),
    )(drg, dco, do, proj, conv, states, w3, cw8, x2, meta_chunk, dh2, g1, *tables, zeta0, *rot)


def _gw_in_scatter(hnT_tok, hnT_meta, dproj, gw_out_parts, dmeta_rows, acc_conv, acc_b2, acc_ret, acc_f2, me_arr):
    n_rows = dproj.shape[0]
    last = N_DEV - 1
    cols = D_CONV // N_DEV
    by_dest = (True, True, False)
    small_blocks = (gw_out_parts.shape[1:], (SC_CONV0 + 8, CHUNK), (GA_LOSS + 8, CHUNK))

    def body(me_ref, a_ref, am_ref, b_ref, go_ref, dm_ref, ac_ref, ab2_ref, aret_ref, af2_ref,
             land_in, land_go, land_sc, land_ga, d2d_buf, d2d_land, ici_buf, sc_ref, ga_ref,
             d2d_send, d2d_recv, ici_send, ici_recv, send_sems, recv_sems, local_sems):
        del me_ref
        s = pl.program_id(0)
        t = last - s
        q = t >> 1
        x, y, c = lax.axis_index("x"), lax.axis_index("y"), lax.axis_index("c")
        me = 4 * x + 2 * y + c
        chip = 2 * x + y
        srcs, lands = (go_ref, sc_ref, ga_ref), (land_go, land_sc, land_ga)

        def peer_at(k):
            return (1 - x if k & 4 else x, 1 - y if k & 2 else y, 1 - c if k & 1 else c)

        def small_copy(a, k):
            px, py, pc = peer_at(k)
            return pltpu.make_async_remote_copy(
                src_ref=srcs[a].at[4 * px + 2 * py + pc] if by_dest[a] else srcs[a],
                dst_ref=lands[a].at[me],
                send_sem=send_sems.at[a * last + k - 1],
                recv_sem=recv_sems.at[a * last + k - 1],
                device_id=(px, py, pc),
                device_id_type=MESH,
            )

        def small_local(a):
            return pltpu.make_async_copy(srcs[a].at[me] if by_dest[a] else srcs[a], lands[a].at[me], local_sems.at[a + 1])

        def d2d_copy(j):
            return pltpu.make_async_remote_copy(
                src_ref=d2d_buf.at[j],
                dst_ref=d2d_land.at[j],
                send_sem=d2d_send.at[j],
                recv_sem=d2d_recv.at[j],
                device_id=(x, y, 1 - c),
                device_id_type=MESH,
            )

        def ici_copy(j, to):
            return pltpu.make_async_remote_copy(
                src_ref=ici_buf.at[j],
                dst_ref=land_in.at[chip],
                send_sem=ici_send.at[j],
                recv_sem=ici_recv.at[j],
                device_id=to,
                device_id_type=MESH,
            )

        def own_copy():
            return pltpu.make_async_copy(ici_buf.at[0], land_in.at[chip], local_sems.at[0])

        @pl.when(s == 0)
        def _():
            sc_ref[...] = jnp.zeros_like(sc_ref)
            for d in range(N_DEV):
                sc_ref[d, SC_META0 : SC_META0 + N_META, :] = dm_ref[:, d * CHUNK : (d + 1) * CHUNK]
                lane0 = (d * cols) % CHUNK
                rows8 = ac_ref[:, d * cols - lane0 : d * cols - lane0 + CHUNK]
                rows8 = pltpu.roll(rows8, CHUNK - lane0, 1) if lane0 else rows8
                sc_ref[d, SC_CONV0 : SC_CONV0 + 3, 0:cols] = rows8[0:3, 0:cols]
            ga_ref[...] = jnp.zeros_like(ga_ref)
            for row0, acc, n in ((GA_N1, ab2_ref, 8), (GA_RG, aret_ref, 4), (GA_FG, af2_ref, 8)):
                for j in range(n):
                    ga_ref[row0 + j : row0 + j + 1, :] = acc[0:1, j * CHUNK : (j + 1) * CHUNK]
            ga_ref[GA_LOSS : GA_LOSS + 1, :] = af2_ref[1:2, 0:CHUNK]
            for a in range(3):
                small_local(a).start()
            for k in range(last, 0, -1):
                for a in range(3):
                    small_copy(a, k).start()

        blk = jnp.dot(a_ref[...], b_ref[ROW_TILE:, :], preferred_element_type=F32)
        blk += jnp.dot(am_ref[...], b_ref[LIVE0:ROW_TILE, :], preferred_element_type=F32)

        @pl.when((t & 1) == 1)
        def _():
            d2d_buf[q] = blk.astype(BF16)
            d2d_copy(q).start()

        @pl.when((t & 1) == 0)
        def _():
            d2d_copy(q).wait_recv()
            ici_buf[q] = (blk + d2d_land[q].astype(F32)).astype(BF16)

            @pl.when(t != 0)
            def _():
                ici_copy(q, (jnp.bitwise_xor(x, (t >> 2) & 1), jnp.bitwise_xor(y, (t >> 1) & 1), c)).start()

            @pl.when(t == 0)
            def _():
                own_copy().start()
                for j in range(N_CHIPS):
                    d2d_copy(j).wait_send()
                for j in range(N_CHIPS - 1, 0, -1):
                    ici_copy(j, peer_at(2 * j)).wait()
                for k in range(last, 0, -1):
                    for a in range(3):
                        small_copy(a, k).wait()
                for a in range(3):
                    small_local(a).wait()
                own_copy().wait()

    grid_spec = pltpu.PrefetchScalarGridSpec(
        num_scalar_prefetch=1,
        grid=(N_DEV,),
        in_specs=[_VMEM, _VMEM, pl.BlockSpec((n_rows, BLK), lambda s, me: (0, jnp.bitwise_xor(me[0], last - s))), _HBM]
        + [_VMEM] * 5,
        out_specs=tuple([_HBM] * 4),
        scratch_shapes=[
            pltpu.VMEM((N_CHIPS, D_MODEL, BLK), BF16),
            pltpu.VMEM((N_CHIPS, D_MODEL, BLK), BF16),
            pltpu.VMEM((N_CHIPS, D_MODEL, BLK), BF16),
            pltpu.VMEM((N_DEV, *small_blocks[1]), F32),
            pltpu.VMEM(small_blocks[2], F32),
            pltpu.SemaphoreType.DMA((N_CHIPS,)),
            pltpu.SemaphoreType.DMA((N_CHIPS,)),
            pltpu.SemaphoreType.DMA((N_CHIPS,)),
            pltpu.SemaphoreType.DMA((N_CHIPS,)),
            pltpu.SemaphoreType.DMA((3 * last,)),
            pltpu.SemaphoreType.DMA((3 * last,)),
            pltpu.SemaphoreType.DMA((4,)),
        ],
    )
    return pl.pallas_call(
        body,
        name="gw_in_scatter",
        grid_spec=grid_spec,
        out_shape=(jax.ShapeDtypeStruct((N_CHIPS, D_MODEL, BLK), BF16),)
        + tuple(jax.ShapeDtypeStruct((N_DEV, *b), dt) for b, dt in zip(small_blocks, (gw_out_parts.dtype, F32, F32), strict=True)),
        compiler_params=_params(),
    )(me_arr, hnT_tok, hnT_meta, dproj, gw_out_parts, dmeta_rows, acc_conv, acc_b2, acc_ret, acc_f2)


def _adamw(w, g, m, v):
    m = ADAM_B1 * m + (1.0 - ADAM_B1) * g
    v = ADAM_B2 * v + (1.0 - ADAM_B2) * (g * g)
    m_hat = m / (1.0 - ADAM_B1**ADAM_STEP)
    v_hat = v / (1.0 - ADAM_B2**ADAM_STEP)
    delta = -ADAM_LR * (m_hat / (jnp.sqrt(v_hat) + ADAM_EPS) + ADAM_WD * w)
    return delta, m, v


def _small_leaves(meta, conv_w, n1, rg, fg):
    return meta, conv_w, n1.reshape(8, CHUNK), rg.reshape(4, CHUNK), fg.reshape(8, CHUNK)


def _from_small_leaves(meta, conv_w, n1, rg, fg):
    return meta, conv_w, n1.reshape(D_MODEL), rg.reshape(D_RET), fg.reshape(D_MODEL)


ADAM_GRID = 4


def _adamw_all(land_in, land_out, land_sc, land_ga, big, small):
    n_leaf = 5
    lands = (land_in, land_out)
    n_big = len(big)

    def body(*refs):
        big_in, sc_ref, ga_ref = refs[: 4 * n_big], refs[4 * n_big], refs[4 * n_big + 1]
        small_in = refs[4 * n_big + 2 : 4 * n_big + 2 + 3 * n_leaf]
        big_out = refs[4 * n_big + 2 + 3 * n_leaf : 8 * n_big + 2 + 3 * n_leaf]
        small_out = refs[8 * n_big + 2 + 3 * n_leaf :]
        for b in range(n_big):
            p_ref, w_ref, m_ref, v_ref = big_in[4 * b : 4 * b + 4]
            g_ref, d_ref, nm_ref, nv_ref = big_out[4 * b : 4 * b + 4]
            g = p_ref[0].astype(F32)
            for s in range(1, p_ref.shape[0]):
                g = g + p_ref[s].astype(F32)
            g_ref[...] = g
            d_ref[...], nm_ref[...], nv_ref[...] = _adamw(w_ref[...], g, m_ref[...], v_ref[...])

        @pl.when(pl.program_id(0) == 0)
        def _():
            sc, ga = sc_ref[0], ga_ref[0]
            for s in range(1, N_DEV):
                sc = sc + sc_ref[s]
                ga = ga + ga_ref[s]
            grads = (
                sc[SC_META0 : SC_META0 + N_META],
                sc[SC_CONV0 : SC_CONV0 + 3, 0 : D_CONV // N_DEV],
                ga[GA_N1 : GA_N1 + 8],
                ga[GA_RG : GA_RG + 4],
                ga[GA_FG : GA_FG + 8],
            )
            for leaf, g in enumerate(grads):
                d, nm, nv = _adamw(
                    small_in[leaf][...], g, small_in[n_leaf + leaf][...], small_in[2 * n_leaf + leaf][...]
                )
                small_out[leaf][...] = g
                small_out[n_leaf + leaf][...] = d
                small_out[2 * n_leaf + leaf][...] = nm
                small_out[3 * n_leaf + leaf][...] = nv
            small_out[4 * n_leaf][...] = ga[GA_LOSS : GA_LOSS + 1]

    in_specs, out_specs, out_shape, operands = [], [], [], []
    for land, (w, m, v) in zip(lands, big, strict=True):
        n_r, n_c = w.shape
        tr = n_r // ADAM_GRID
        blk = pl.BlockSpec((tr, n_c), lambda i: (i, 0))
        in_specs += [pl.BlockSpec((land.shape[0], tr, n_c), lambda i: (0, i, 0)), blk, blk, blk]
        out_specs += [blk] * 4
        out_shape += [jax.ShapeDtypeStruct((n_r, n_c), F32)] * 4
        operands += [land, w, m, v]
    small_shapes = [jax.ShapeDtypeStruct(a.shape, F32) for a in small[0]] * 4 + [jax.ShapeDtypeStruct((1, CHUNK), F32)]
    out = pl.pallas_call(
        body,
        name="adamw_all",
        grid=(ADAM_GRID,),
        in_specs=in_specs + [_VMEM] * (2 + 3 * n_leaf),
        out_specs=tuple(out_specs + [_VMEM] * len(small_shapes)),
        out_shape=tuple(out_shape + small_shapes),
        compiler_params=_params(),
    )(*operands, land_sc, land_ga, *small[0], *small[1], *small[2])
    bigs = tuple(out[4 * b : 4 * b + 4] for b in range(n_big))
    rest = out[4 * n_big :]
    return bigs, tuple(rest[k * n_leaf : (k + 1) * n_leaf] for k in range(4)), rest[4 * n_leaf]


def kernel(x, meta, norm1_g, w_in, conv_w, ret_norm_g, w_out, final_g, loss_target, m_meta, m_norm1_g, m_w_in, m_conv_w, m_ret_norm_g, m_w_out, m_final_g, v_meta, v_norm1_g, v_w_in, v_conv_w, v_ret_norm_g, v_w_out, v_final_g):
    seq = x.shape[1]
    assert x.shape == (1, seq, D_MODEL) and seq % ROW_TILE == 0
    n_tiles = seq // ROW_TILE + 1
    x2, t2 = x[0], loss_target[0]

    small_w = _small_leaves(meta, conv_w, norm1_g, ret_norm_g, final_g)
    g1 = norm1_g.reshape(1, D_MODEL)
    hn_tok, hnT_tok, w3, meta_chunk, cw8 = _gather_weights_and_norm(x2, g1, w_in, meta, conv_w)

    rot = _rotary_tables(n_tiles, ROW_TILE)
    dec, _, zeta, xi = _decay_tables()
    gr = ret_norm_g.reshape(1, D_RET)
    gf = final_g.reshape(1, D_MODEL)

    hnT_meta, proj, conv_out, conv, o, states, wo3 = _f1(hn_tok, meta_chunk, g1, w3, cw8, rot, dec, zeta, xi, w_out)
    wout = wo3.reshape(D_MODEL, D_MODEL)
    dh2, dco, drg, do, gw_out, acc_f2, acc_ret = _f2(proj, o, conv_out, x2, t2, wout, gr, gf)
    dproj, grad_x2, dmeta_rows, acc_b2, acc_conv = _b2(
        drg, dco, do, proj, conv, states, w3, cw8, x2, meta_chunk, dh2, g1, _decay_tables(B1_CHUNK), zeta, rot
    )
    gw_out_parts = gw_out.reshape(N_DEV, D_MODEL // N_DEV, D_MODEL)

    me_arr = (4 * lax.axis_index("x") + 2 * lax.axis_index("y") + lax.axis_index("c")).astype(jnp.int32).reshape(1)
    land_in, land_out, land_sc, land_ga = _gw_in_scatter(
        hnT_tok, hnT_meta, dproj, gw_out_parts, dmeta_rows, acc_conv, acc_b2, acc_ret, acc_f2, me_arr
    )

    (w_in_out, w_out_out), small_out, loss_row = _adamw_all(
        land_in,
        land_out,
        land_sc,
        land_ga,
        ((w_in, m_w_in, v_w_in), (w_out, m_w_out, v_w_out)),
        (
            small_w,
            _small_leaves(m_meta, m_conv_w, m_norm1_g, m_ret_norm_g, m_final_g),
            _small_leaves(v_meta, v_conv_w, v_norm1_g, v_ret_norm_g, v_final_g),
        ),
    )
    g_w_in, d_w_in, nm_w_in, nv_w_in = w_in_out
    g_w_out, d_w_out, nm_w_out, nv_w_out = w_out_out
    loss = loss_row[0, 0]
    grad_x = grad_x2.reshape(1, seq, D_MODEL)

    def leaves(w_in_leaf, w_out_leaf, small_leaves):
        meta_leaf, conv_leaf, n1_leaf, rg_leaf, fg_leaf = _from_small_leaves(*small_leaves)
        return (meta_leaf, n1_leaf, w_in_leaf, conv_leaf, rg_leaf, w_out_leaf, fg_leaf)

    return (
        loss,
        grad_x,
        *leaves(g_w_in, g_w_out, small_out[0]),
        *leaves(d_w_in, d_w_out, small_out[1]),
        *leaves(nm_w_in, nm_w_out, small_out[2]),
        *leaves(nv_w_in, nv_w_out, small_out[3]),
    )
```

```python
import math

import jax
import jax.numpy as jnp
import numpy as np
from jax import lax
from jax.experimental import pallas as pl
from jax.experimental.pallas import tpu as pltpu

F32 = jnp.float32
BF16 = jnp.bfloat16

N_DEV = 8
N_CHIPS = 4
D_MODEL = 1024
N_META = 16
CHUNK = 128
D_CONV = 512
D_RET = 512
N_HEADS = 4
HEAD_DIM = 128
N_PROJ = 8
BLK = 512
ROPE_BASE = 10000.0
EPS = 1e-6
Q_SCALE = HEAD_DIM ** -0.5
LOG_G = tuple(math.log(1.0 - 2.0 ** (-5.0 - h)) for h in range(N_HEADS))
CHUNK_DECAY = tuple(math.exp(CHUNK * lg) for lg in LOG_G)

ADAM_LR = 0.001
ADAM_B1 = 0.9
ADAM_B2 = 0.999
ADAM_EPS = 1e-08
ADAM_WD = 0.01
ADAM_STEP = 10

ROW_TILE = 512
PAD_ROWS = ROW_TILE - N_META
LIVE0 = ROW_TILE - CHUNK
B1_CHUNK = 256
B2_ROWS = ROW_TILE // 2
VMEM_LIMIT = 56 * 1024 * 1024

SC_META0, SC_CONV0 = 0, 16
GA_N1, GA_RG, GA_FG, GA_LOSS = 0, 8, 16, 24

NT = (((1,), (1,)), ((), ()))
TN = (((0,), (0,)), ((), ()))
MESH = pl.DeviceIdType.MESH

_VMEM = pl.BlockSpec(memory_space=pltpu.VMEM)
_HBM = pl.BlockSpec(memory_space=pltpu.HBM)


def _params(n_axes=1):
    return pltpu.CompilerParams(dimension_semantics=("arbitrary",) * n_axes, vmem_limit_bytes=VMEM_LIMIT)


def _sigmoid(x):
    return 0.5 * jnp.tanh(0.5 * x) + 0.5


def _decay_tables(chunk=CHUNK):
    idx = np.arange(chunk, dtype=np.float64)
    diff = idx[:, None] - idx[None, :]
    dec = np.stack([np.where(diff >= 0, np.exp(diff * lg), 0.0) for lg in LOG_G])
    zeta = np.stack([np.exp((chunk - 1 - idx) * lg) for lg in LOG_G])
    xi = np.stack([np.exp((idx + 1.0) * lg) for lg in LOG_G])
    ones = np.ones((1, 1, HEAD_DIM))
    return (
        jnp.asarray(dec, F32),
        jnp.asarray(dec.transpose(0, 2, 1), F32),
        jnp.asarray(zeta[:, :, None] * ones, F32),
        jnp.asarray(xi[:, :, None] * ones, F32),
    )


def _rotary_tables(n_tiles, tm):
    half = HEAD_DIM // 2
    freqs = (1.0 / (np.float32(ROPE_BASE) ** (np.arange(half, dtype=np.float32) / np.float32(half)))).astype(np.float64)
    sign = np.concatenate([-np.ones(half), np.ones(half)])
    two = lambda a: np.concatenate([a, a], axis=1)
    base = two((np.arange(n_tiles, dtype=np.float64) * tm - PAD_ROWS)[:, None] * freqs[None, :])
    off = two(np.arange(tm, dtype=np.float64)[:, None] * freqs[None, :])
    as32 = lambda a: jnp.asarray(a, F32)
    return as32(np.cos(base)), as32(np.sin(base) * sign), as32(np.cos(off)), as32(np.sin(off) * sign)


def _tile_rotary(ca_ref, sa_ref, cb_ref, sb_ref, tile):
    ca, sa = ca_ref[pl.ds(tile, 1), :], sa_ref[pl.ds(tile, 1), :]
    cb, sb = cb_ref[...], sb_ref[...]
    return ca * cb - sa * sb, sa * cb + ca * sb


def _rot(t, cos2, sin2):
    return t * cos2 + pltpu.roll(t, HEAD_DIM // 2, 1) * sin2


def _rot_bwd(d, cos2, sin2):
    return d * cos2 + pltpu.roll(d * sin2, HEAD_DIM // 2, 1)


def _token_tile(i):
    return jnp.maximum(i - 1, 0)


GATHER_PAIRS = 7


def _two_level_gather(src_ref, out_ref, send_sems, recv_sems, local_sem, base=0):
    x, y, c = lax.axis_index("x"), lax.axis_index("y"), lax.axis_index("c")
    me, sibling = (x, y, c), (x, y, 1 - c)
    xnb, ynb, diag = (1 - x, y), (x, 1 - y), (1 - x, 1 - y)
    relayed = (jnp.bitwise_xor(x, 1 - c), jnp.bitwise_xor(y, c))
    other = (jnp.bitwise_xor(x, c), jnp.bitwise_xor(y, 1 - c))

    def copy(k, block, to, src=None):
        dst = out_ref.at[4 * block[0] + 2 * block[1] + block[2]]
        return pltpu.make_async_remote_copy(
            src_ref=dst if src is None else src,
            dst_ref=dst,
            send_sem=send_sems.at[base + k],
            recv_sem=recv_sems.at[base + k],
            device_id=to,
            device_id_type=MESH,
        )

    mine = pltpu.make_async_copy(src_ref, out_ref.at[4 * x + 2 * y + c], local_sem)
    first = [copy(1, me, (*xnb, c), src=src_ref), copy(2, me, (*ynb, c), src=src_ref), copy(0, me, sibling, src=src_ref)]
    relay = copy(3, (*relayed, c), (*other, c))
    passed = [copy(4 + j, (*chip, c), sibling) for j, chip in enumerate((xnb, ynb, diag))]

    def start():
        mine.start()
        for cp in first:
            cp.start()

    def forward():
        copy(1 + c, (*relayed, c), me).wait_recv()
        relay.start()
        copy(2 - c, (*other, c), me).wait_recv()
        passed[0].start()
        passed[1].start()

    def forward_relayed():
        copy(3, (*diag, c), me).wait_recv()
        passed[2].start()

    def finish():
        copy(0, sibling, me).wait_recv()
        for j, chip in enumerate((xnb, ynb, diag)):
            copy(4 + j, (*chip, 1 - c), me).wait_recv()
        for cp in first + [relay] + passed:
            cp.wait_send()
        mine.wait()

    return start, forward, forward_relayed, finish


W_IN_CHUNKS = 4


def _gather_weights_and_norm(x2, g1, w_shard, meta, conv_w):
    per = GATHER_PAIRS
    n_ch = W_IN_CHUNKS
    rows = w_shard.shape[0] // n_ch
    tm = ROW_TILE
    nt = x2.shape[0] // tm
    cols = D_CONV // N_DEV
    pack_shape = (SC_CONV0 + 8, CHUNK)
    starts = [0] * (n_ch + 1)
    forwards = [min(5 + 2 * p, nt - 1) for p in range(n_ch)] + [min(12, nt - 1)]
    relayeds = [min(12 + p, nt - 1) for p in range(n_ch)] + [nt - 1]
    finishes = [nt - 1] * (n_ch + 1)

    def body(x_ref, g_ref, w_ref, meta_ref, conv_ref, hn_ref, hnT_ref, w_all_ref, mc_ref, cw_ref,
             w_bf, small_ref, small_all_ref, send_sems, recv_sems, local_sems):
        i = pl.program_id(0)

        @pl.when(i == 0)
        def _():
            w_bf[...] = w_ref[...].astype(BF16)
            small_ref[...] = jnp.zeros_like(small_ref)
            small_ref[SC_META0 : SC_META0 + N_META, :] = meta_ref[...]
            small_ref[SC_CONV0 : SC_CONV0 + 3, 0:cols] = conv_ref[...]

        parts = [
            _two_level_gather(
                w_bf.at[pl.ds(ch * rows, rows)],
                w_all_ref.at[:, pl.ds(ch * rows, rows)],
                send_sems,
                recv_sems,
                local_sems.at[ch],
                ch * per,
            )
            for ch in range(n_ch)
        ]
        parts.append(_two_level_gather(small_ref, small_all_ref, send_sems, recv_sems, local_sems.at[n_ch], n_ch * per))
        def run(phase, steps):
            for part, step in zip(parts, steps, strict=True):
                pl.when(i == step)(part[phase])

        run(0, starts)
        run(1, forwards)
        h = x_ref[...]
        r = lax.rsqrt(jnp.mean(h * h, axis=-1, keepdims=True) + EPS)
        hn = (h * r * g_ref[...]).astype(BF16)
        hn_ref[...] = hn
        hnT_ref[...] = hn.T
        run(2, relayeds)
        run(3, finishes)

        @pl.when(i == nt - 1)
        def _():
            mc_ref[0 : CHUNK - N_META, :] = jnp.zeros((CHUNK - N_META, D_MODEL), F32)
            cw_ref[...] = jnp.zeros_like(cw_ref)
            for d in range(N_DEV):
                mc_ref[CHUNK - N_META : CHUNK, d * CHUNK : (d + 1) * CHUNK] = small_all_ref[d, SC_META0 : SC_META0 + N_META, :]
                lane0 = (d * cols) % CHUNK
                rows8 = small_all_ref[d, SC_CONV0 : SC_CONV0 + 8, :]
                rows8 = pltpu.roll(rows8, lane0, 1) if lane0 else rows8
                cw_ref[0:3, d * cols : (d + 1) * cols] = rows8[0:3, lane0 : lane0 + cols]

    const = lambda r, c: pl.BlockSpec((r, c), lambda i: (0, 0))
    return pl.pallas_call(
        body,
        name="gather_weights_norm",
        grid=(nt,),
        out_shape=(
            jax.ShapeDtypeStruct(x2.shape, BF16),
            jax.ShapeDtypeStruct(x2.shape[::-1], BF16),
            jax.ShapeDtypeStruct((N_DEV, *w_shard.shape), BF16),
            jax.ShapeDtypeStruct((CHUNK, D_MODEL), F32),
            jax.ShapeDtypeStruct((8, D_CONV), F32),
        ),
        in_specs=[pl.BlockSpec((tm, D_MODEL), lambda i: (i, 0)), _VMEM, _VMEM, _VMEM, _VMEM],
        out_specs=(
            pl.BlockSpec((tm, D_MODEL), lambda i: (i, 0)),
            pl.BlockSpec((D_MODEL, tm), lambda i: (0, i)),
            _HBM,
            const(CHUNK, D_MODEL),
            const(8, D_CONV),
        ),
        scratch_shapes=[
            pltpu.VMEM(w_shard.shape, BF16),
            pltpu.VMEM(pack_shape, F32),
            pltpu.VMEM((N_DEV, *pack_shape), F32),
            pltpu.SemaphoreType.DMA(((n_ch + 1) * per,)),
            pltpu.SemaphoreType.DMA(((n_ch + 1) * per,)),
            pltpu.SemaphoreType.DMA((n_ch + 1,)),
        ],
        compiler_params=_params(),
    )(x2, g1, w_shard, meta, conv_w)


def _f1(hn_tok, meta_chunk, g1, w3, cw8, rot, dec, zeta, xi, wo_shard):
    tm = ROW_TILE
    nt = hn_tok.shape[0] // tm + 1
    n_rows = nt * tm
    nct = tm // CHUNK

    def body(hn_ref, mc_ref, g_ref, w_ref, cw_ref, ca_ref, sa_ref, cb_ref, sb_ref, dec_ref, zeta_ref, xi_ref, wo_ref,
             hnT_ref, pr_ref, co_ref, cv_ref, o_ref, st_ref, wo_all_ref,
             halo, state, wo_bf, send_sems, recv_sems, local_sem):
        i = pl.program_id(0)

        @pl.when(i == 0)
        def _():
            wo_bf[...] = wo_ref[...].astype(BF16)

        wo_phases = _two_level_gather(wo_bf, wo_all_ref, send_sems, recv_sems, local_sem)
        for step, phase in zip((0, nt // 3, 2 * nt // 3, nt - 1), wo_phases, strict=True):
            pl.when(i == step)(phase)

        def work(hn, r0):
            rs = slice(r0, tm)
            n = tm - r0

            def proj(j):
                return jnp.dot(hn, w_ref[j], preferred_element_type=F32)

            cos_all, sin_all = _tile_rotary(ca_ref, sa_ref, cb_ref, sb_ref, i)
            cos_t, sin_t = cos_all[rs, :], sin_all[rs, :]
            q, k, v = proj(4), proj(5), proj(6).astype(BF16)
            pr_ref[rs, 6 * BLK : 7 * BLK] = v
            chunks = range(r0 // CHUNK, tm // CHUNK)
            heads = []
            for hd in range(N_HEADS):
                cs = slice(hd * HEAD_DIM, (hd + 1) * HEAD_DIM)
                qh = (_rot(q[:, cs], cos_t, sin_t) * Q_SCALE).astype(BF16)
                kh = _rot(k[:, cs], cos_t, sin_t).astype(BF16)
                vh = v[:, cs]
                pr_ref[rs, 4 * BLK + hd * HEAD_DIM : 4 * BLK + (hd + 1) * HEAD_DIM] = qh
                pr_ref[rs, 5 * BLK + hd * HEAD_DIM : 5 * BLK + (hd + 1) * HEAD_DIM] = kh
                upd = {}
                for c in chunks:
                    cr = slice(c * CHUNK - r0, (c + 1) * CHUNK - r0)
                    kz = (kh[cr].astype(F32) * zeta_ref[hd]).astype(BF16)
                    upd[c] = lax.dot_general(kz, vh[cr], TN, preferred_element_type=F32)
                st = state[hd]
                outs = []
                for c in chunks:
                    cr = slice(c * CHUNK - r0, (c + 1) * CHUNK - r0)
                    qc, kc, vc = qh[cr], kh[cr], vh[cr]
                    st_bf = st.astype(BF16)
                    st_ref[c, hd] = st_bf
                    s = lax.dot_general(qc, kc, NT, preferred_element_type=F32) * dec_ref[hd]
                    inner = jnp.dot(s.astype(BF16), vc, preferred_element_type=F32)
                    qx = (qc.astype(F32) * xi_ref[hd]).astype(BF16)
                    outs.append(inner + jnp.dot(qx, st_bf, preferred_element_type=F32))
                    st = CHUNK_DECAY[hd] * st + upd[c]
                state[hd] = st
                heads.append(jnp.concatenate(outs, axis=0) if len(outs) > 1 else outs[0])
            o_ref[rs, :] = jnp.concatenate(heads, axis=1)

            cx, cb, cc, cg = proj(0), proj(1), proj(2), proj(3)
            u = cc * cx
            rows = lax.broadcasted_iota(jnp.int32, u.shape, 0)
            hl = halo[...]
            u1 = jnp.where(rows == 0, hl[7:8], pltpu.roll(u, 1, 0))
            u2 = jnp.where(rows == 0, hl[6:7], jnp.where(rows == 1, hl[7:8], pltpu.roll(u, 2, 0)))
            halo[...] = u[n - 8 : n]
            cw = cw_ref[...]
            conv = cw[0:1] * u2 + cw[1:2] * u1 + cw[2:3] * u
            co = (cb * conv * (cg * _sigmoid(cg))).astype(BF16)
            co_ref[rs, :] = co
            cv_ref[rs, :] = conv.astype(BF16)
            pr_ref[rs, 0 * BLK : 1 * BLK] = cx.astype(BF16)
            pr_ref[rs, 1 * BLK : 2 * BLK] = cb.astype(BF16)
            pr_ref[rs, 2 * BLK : 3 * BLK] = cc.astype(BF16)
            pr_ref[rs, 3 * BLK : 4 * BLK] = cg.astype(BF16)
            pr_ref[rs, 7 * BLK : 8 * BLK] = proj(7).astype(BF16)

        @pl.when(i == 0)
        def _():
            halo[...] = jnp.zeros_like(halo)
            state[...] = jnp.zeros_like(state)
            h = mc_ref[...]
            r = lax.rsqrt(jnp.mean(h * h, axis=-1, keepdims=True) + EPS)
            hn = (h * r * g_ref[...]).astype(BF16)
            hnT_ref[...] = hn.T
            work(hn, LIVE0)

        @pl.when(i > 0)
        def _():
            work(hn_ref[...], 0)

    row = lambda w: pl.BlockSpec((tm, w), lambda i: (i, 0))
    return pl.pallas_call(
        body,
        name="f1_inproj_conv",
        grid=(nt,),
        in_specs=[pl.BlockSpec((tm, D_MODEL), lambda i: (_token_tile(i), 0))] + [_VMEM] * 12,
        out_specs=(
            pl.BlockSpec((D_MODEL, CHUNK), lambda i: (0, 0)),
            row(N_PROJ * BLK),
            row(D_CONV),
            row(D_CONV),
            row(D_RET),
            pl.BlockSpec((nct, N_HEADS, HEAD_DIM, HEAD_DIM), lambda i: (i, 0, 0, 0)),
            _HBM,
        ),
        out_shape=(
            jax.ShapeDtypeStruct((D_MODEL, CHUNK), BF16),
            jax.ShapeDtypeStruct((n_rows, N_PROJ * BLK), BF16),
            jax.ShapeDtypeStruct((n_rows, D_CONV), BF16),
            jax.ShapeDtypeStruct((n_rows, D_CONV), BF16),
            jax.ShapeDtypeStruct((n_rows, D_RET), F32),
            jax.ShapeDtypeStruct((n_rows // CHUNK, N_HEADS, HEAD_DIM, HEAD_DIM), BF16),
            jax.ShapeDtypeStruct((N_DEV, *wo_shard.shape), BF16),
        ),
        scratch_shapes=[
            pltpu.VMEM((8, D_CONV), F32),
            pltpu.VMEM((N_HEADS, HEAD_DIM, HEAD_DIM), F32),
            pltpu.VMEM(wo_shard.shape, BF16),
            pltpu.SemaphoreType.DMA((GATHER_PAIRS,)),
            pltpu.SemaphoreType.DMA((GATHER_PAIRS,)),
            pltpu.SemaphoreType.DMA(()),
        ],
        compiler_params=_params(),
    )(hn_tok, meta_chunk, g1, w3, cw8, *rot, dec, zeta, xi, wo_shard)


def _group_norm(o):
    ys, rs = [], []
    for hd in range(N_HEADS):
        oh = o[:, hd * HEAD_DIM : (hd + 1) * HEAD_DIM]
        xc = oh - jnp.mean(oh, axis=-1, keepdims=True)
        rstd = lax.rsqrt(jnp.mean(xc * xc, axis=-1, keepdims=True) + EPS)
        ys.append(xc * rstd)
        rs.append(jnp.broadcast_to(rstd, oh.shape))
    return jnp.concatenate(ys, axis=1), jnp.concatenate(rs, axis=1)


def _f2(proj, o, conv_out, x2, t2, wout, gr, gf):
    n_rows = proj.shape[0]
    tm = ROW_TILE
    nt = n_rows // tm

    def body(o_ref, rg_ref, co_ref, x_ref, t_ref, wo_ref, gr_ref, gf_ref,
             dh2_ref, dco_ref, drg_ref, do_ref, gw_ref, acc_ref, accr_ref, gacc):
        i = pl.program_id(0)

        @pl.when(i == 0)
        def _():
            acc_ref[...] = jnp.zeros_like(acc_ref)
            accr_ref[...] = jnp.zeros_like(accr_ref)
            gacc[...] = jnp.zeros_like(gacc)
            dh2_ref[...] = jnp.zeros_like(dh2_ref)
            dco_ref[...] = jnp.zeros_like(dco_ref)
            drg_ref[...] = jnp.zeros_like(drg_ref)
            do_ref[...] = jnp.zeros_like(do_ref)

        @pl.when(i > 0)
        def _():
            yh, rstd = _group_norm(o_ref[...])
            rg = rg_ref[...].astype(F32)
            sg = _sigmoid(rg)
            silu = rg * sg
            grv = gr_ref[...]
            ro = (yh * grv * silu).astype(BF16)
            h2 = (
                x_ref[...]
                + jnp.dot(co_ref[...], wo_ref[0:D_CONV], preferred_element_type=F32)
                + jnp.dot(ro, wo_ref[D_CONV:], preferred_element_type=F32)
            )
            r2 = lax.rsqrt(jnp.mean(h2 * h2, axis=-1, keepdims=True) + EPS)
            yn = h2 * r2
            gfv = gf_ref[...]
            err = yn * gfv - t_ref[...]
            tile_loss = jnp.sum(jnp.sum(err * err, axis=-1, keepdims=True), axis=0, keepdims=True) * (0.5 / D_MODEL)
            acc_ref[0:1, :] += jnp.sum(err * yn, axis=0, keepdims=True) * (1.0 / D_MODEL)
            acc_ref[1:2, :] += tile_loss
            dyn = err * (gfv * (1.0 / D_MODEL))
            dh2 = r2 * (dyn - yn * jnp.mean(dyn * yn, axis=-1, keepdims=True))
            dh2_ref[...] = dh2
            dh2_bf = dh2.astype(BF16)
            dmix = lax.dot_general(dh2_bf, wo_ref[...], NT, preferred_element_type=F32)
            dco_ref[...] = dmix[:, :D_CONV].astype(BF16)
            dro = dmix[:, D_CONV:]
            dsilu = sg * (1.0 + rg * (1.0 - sg))
            drg_ref[...] = (dro * (yh * grv) * dsilu).astype(BF16)
            dret = dro * silu
            accr_ref[0:1, :] += jnp.sum(dret * yh, axis=0, keepdims=True)
            dyh = dret * grv
            for hd in range(N_HEADS):
                cs = slice(hd * HEAD_DIM, (hd + 1) * HEAD_DIM)
                a, b = dyh[:, cs], yh[:, cs]
                do_ref[:, cs] = (
                    rstd[:, cs]
                    * (a - jnp.mean(a, axis=-1, keepdims=True) - b * jnp.mean(a * b, axis=-1, keepdims=True))
                ).astype(BF16)
            gacc[0:D_CONV, :] += lax.dot_general(co_ref[...], dh2_bf, TN, preferred_element_type=F32)
            gacc[D_CONV:, :] += lax.dot_general(ro, dh2_bf, TN, preferred_element_type=F32)

        @pl.when(i == nt - 1)
        def _():
            gw_ref[...] = gacc[...].astype(BF16)

    row = lambda w, j=0: pl.BlockSpec((tm, w), lambda i: (i, j))
    tok = pl.BlockSpec((tm, D_MODEL), lambda i: (_token_tile(i), 0))
    return pl.pallas_call(
        body,
        name="f2_out_loss",
        grid=(nt,),
        in_specs=[row(D_RET), row(BLK, 7), row(D_CONV), tok, tok] + [_VMEM] * 3,
        out_specs=(
            row(D_MODEL),
            row(D_CONV),
            row(BLK),
            row(D_RET),
            pl.BlockSpec((D_MODEL, D_MODEL), lambda i: (0, 0)),
            pl.BlockSpec((8, D_MODEL), lambda i: (0, 0)),
            pl.BlockSpec((8, D_RET), lambda i: (0, 0)),
        ),
        out_shape=(
            jax.ShapeDtypeStruct((n_rows, D_MODEL), F32),
            jax.ShapeDtypeStruct((n_rows, D_CONV), BF16),
            jax.ShapeDtypeStruct((n_rows, BLK), BF16),
            jax.ShapeDtypeStruct((n_rows, D_RET), BF16),
            jax.ShapeDtypeStruct((D_MODEL, D_MODEL), BF16),
            jax.ShapeDtypeStruct((8, D_MODEL), F32),
            jax.ShapeDtypeStruct((8, D_RET), F32),
        ),
        scratch_shapes=[pltpu.VMEM((D_MODEL, D_MODEL), F32)],
        compiler_params=_params(),
    )(o, proj, conv_out, x2, t2, wout, gr, gf)


def _b2(drg, dco, do, proj, conv, states, w3, cw8, x2, meta_chunk, dh2, g1, tables, zeta0, rot):
    n_rows = drg.shape[0]
    tm = ROW_TILE
    nt = n_rows // tm
    last = N_PROJ - 1
    chunk = B1_CHUNK
    nct = tm // chunk
    decay = tuple(math.exp(chunk * lg) for lg in LOG_G)
    Q0, K0, V0 = 4 * BLK, 5 * BLK, 6 * BLK

    def body(drg_ref, dco_ref, do_ref, pr_ref, cv_ref, st_ref, w_ref, cw_ref, x_ref, mc_ref, dh2_ref, g_ref,
             dec_ref, dect_ref, zeta_ref, xi_ref, zeta0_ref, ca_ref, sa_ref, cb_ref, sb_ref,
             dpc_ref, gx_ref, dm_ref, acc_ref, accc_ref, halo, dstate):
        i = pl.program_id(0)
        tile = nt - 1 - i
        cos_all, sin_all = _tile_rotary(ca_ref, sa_ref, cb_ref, sb_ref, tile)

        @pl.when(i == 0)
        def _():
            acc_ref[...] = jnp.zeros_like(acc_ref)
            accc_ref[...] = jnp.zeros_like(accc_ref)
            halo[...] = jnp.zeros_like(halo)
            dstate[...] = jnp.zeros_like(dstate)

        def retention_backward():
            for hd in range(N_HEADS):
                cs = slice(hd * HEAD_DIM, (hd + 1) * HEAD_DIM)
                qs, ks, vs = (slice(c0 + hd * HEAD_DIM, c0 + (hd + 1) * HEAD_DIM) for c0 in (Q0, K0, V0))
                dupd = {}
                for c in range(nct):
                    rs = slice(c * chunk, (c + 1) * chunk)
                    qx = (pr_ref[rs, qs].astype(F32) * xi_ref[hd]).astype(BF16)
                    dupd[c] = lax.dot_general(qx, do_ref[rs, cs], TN, preferred_element_type=F32)
                dst = dstate[hd]
                for c in reversed(range(nct)):
                    rs = slice(c * chunk, (c + 1) * chunk)
                    cos_t, sin_t = cos_all[rs, :], sin_all[rs, :]
                    q, k, v, do = pr_ref[rs, qs], pr_ref[rs, ks], pr_ref[rs, vs], do_ref[rs, cs]
                    st_bf = st_ref[c * (chunk // CHUNK), hd]
                    dst_bf = dst.astype(BF16)
                    zt, xt = zeta_ref[hd], xi_ref[hd]
                    sT = (lax.dot_general(k, q, NT, preferred_element_type=F32) * dect_ref[hd]).astype(BF16)
                    dsT = (lax.dot_general(v, do, NT, preferred_element_type=F32) * dect_ref[hd]).astype(BF16)
                    ds = (lax.dot_general(do, v, NT, preferred_element_type=F32) * dec_ref[hd]).astype(BF16)
                    kz = (k.astype(F32) * zt).astype(BF16)
                    dv = jnp.dot(sT, do, preferred_element_type=F32) + jnp.dot(kz, dst_bf, preferred_element_type=F32)
                    dq = jnp.dot(ds, k, preferred_element_type=F32) + xt * lax.dot_general(
                        do, st_bf, NT, preferred_element_type=F32
                    )
                    dk = jnp.dot(dsT, q, preferred_element_type=F32) + zt * lax.dot_general(
                        v, dst_bf, NT, preferred_element_type=F32
                    )
                    dst = decay[hd] * dst + dupd[c]
                    dpc_ref[rs, qs] = _rot_bwd(dq * Q_SCALE, cos_t, sin_t).astype(BF16)
                    dpc_ref[rs, ks] = _rot_bwd(dk, cos_t, sin_t).astype(BF16)
                    dpc_ref[rs, vs] = dv.astype(BF16)
                dstate[hd] = dst

        def state_backward(rs):
            cos_t, sin_t = cos_all[rs, :], sin_all[rs, :]
            for hd in range(N_HEADS):
                qs, ks, vs = (slice(c0 + hd * HEAD_DIM, c0 + (hd + 1) * HEAD_DIM) for c0 in (Q0, K0, V0))
                k, v = pr_ref[rs, ks], pr_ref[rs, vs]
                dst_bf = dstate[hd].astype(BF16)
                zt = zeta0_ref[hd]
                kz = (k.astype(F32) * zt).astype(BF16)
                dv = jnp.dot(kz, dst_bf, preferred_element_type=F32)
                dk = zt * lax.dot_general(v, dst_bf, NT, preferred_element_type=F32)
                dpc_ref[rs, qs] = jnp.zeros((tm - LIVE0, HEAD_DIM), BF16)
                dpc_ref[rs, ks] = _rot_bwd(dk, cos_t, sin_t).astype(BF16)
                dpc_ref[rs, vs] = dv.astype(BF16)

        def work(h, r0, others, r1=tm):
            rs = slice(r0, r1)
            n = r1 - r0
            dco = dco_ref[rs, :].astype(F32)
            cx = pr_ref[rs, 0 * BLK : 1 * BLK].astype(F32)
            cb = pr_ref[rs, 1 * BLK : 2 * BLK].astype(F32)
            cc = pr_ref[rs, 2 * BLK : 3 * BLK].astype(F32)
            cg = pr_ref[rs, 3 * BLK : 4 * BLK].astype(F32)
            conv = cv_ref[rs, :].astype(F32)
            sg = _sigmoid(cg)
            silu, dsilu = cg * sg, sg * (1.0 + cg * (1.0 - sg))
            t = dco * cb
            dcb = (dco * conv * silu).astype(BF16)
            dcg = (t * conv * dsilu).astype(BF16)
            dconv = t * silu
            rows = lax.broadcasted_iota(jnp.int32, dconv.shape, 0)
            hl = halo[...]
            dc1 = jnp.where(rows == n - 1, hl[0:1], pltpu.roll(dconv, n - 1, 0))
            dc2 = jnp.where(rows == n - 2, hl[0:1], jnp.where(rows == n - 1, hl[1:2], pltpu.roll(dconv, n - 2, 0)))
            halo[...] = dconv[0:8]
            cw = cw_ref[...]
            du = cw[2:3] * dconv + cw[1:2] * dc1 + cw[0:1] * dc2
            u = cc * cx
            accc_ref[0:1, :] += jnp.sum(u * dc2, axis=0, keepdims=True)
            accc_ref[1:2, :] += jnp.sum(u * dc1, axis=0, keepdims=True)
            accc_ref[2:3, :] += jnp.sum(u * dconv, axis=0, keepdims=True)
            dcx = (du * cc).astype(BF16)
            dcc = (du * cx).astype(BF16)
            dhn = None
            for blk, ref, c0 in others:
                d = lax.dot_general(ref[rs, c0 : c0 + BLK], w_ref[blk], NT, preferred_element_type=F32)
                dhn = d if dhn is None else dhn + d
            for blk, d in ((1, dcb), (3, dcg), (0, dcx), (2, dcc)):
                dpc_ref[rs, blk * BLK : (blk + 1) * BLK] = d
                dhn += lax.dot_general(d, w_ref[blk], NT, preferred_element_type=F32)
            r = lax.rsqrt(jnp.mean(h * h, axis=-1, keepdims=True) + EPS)
            hh = h * r
            acc_ref[0:1, :] += jnp.sum(dhn * hh, axis=0, keepdims=True)
            dg = dhn * g_ref[...]
            return r * (dg - hh * jnp.mean(dg * hh, axis=-1, keepdims=True))

        @pl.when(tile == 0)
        def _():
            state_backward(slice(LIVE0, tm))
            dpc_ref[LIVE0:tm, last * BLK :] = jnp.zeros((tm - LIVE0, BLK), BF16)
            dh = work(mc_ref[...], LIVE0, ((5, dpc_ref, K0), (6, dpc_ref, V0)))
            dm_ref[...] = dh[CHUNK - N_META : CHUNK]

        @pl.when(tile > 0)
        def _():
            retention_backward()
            dpc_ref[:, last * BLK :] = drg_ref[...]
            others = ((last, drg_ref, 0), (4, dpc_ref, Q0), (5, dpc_ref, K0), (6, dpc_ref, V0))
            for r0 in range(tm - B2_ROWS, -1, -B2_ROWS):
                r1 = r0 + B2_ROWS
                gx_ref[r0:r1, :] = dh2_ref[r0:r1, :] + work(x_ref[r0:r1, :], r0, others, r1)

    seven = last * BLK
    full = N_PROJ * BLK

    rev = lambda w, j=0: pl.BlockSpec((tm, w), lambda i: (nt - 1 - i, j))
    tok = pl.BlockSpec((tm, D_MODEL), lambda i: (_token_tile(nt - 1 - i), 0))
    const = lambda r, c: pl.BlockSpec((r, c), lambda i: (0, 0))
    return pl.pallas_call(
        body,
        name="b2_dconv_dh",
        grid=(nt,),
        in_specs=[
            rev(BLK),
            rev(D_CONV),
            rev(D_RET),
            rev(seven),
            rev(D_CONV),
            pl.BlockSpec((tm // CHUNK, N_HEADS, HEAD_DIM, HEAD_DIM), lambda i: (nt - 1 - i, 0, 0, 0)),
            _VMEM,
            _VMEM,
            tok,
            _VMEM,
            rev(D_MODEL),
        ]
        + [_VMEM] * 10,
        out_specs=(rev(full), tok, const(N_META, D_MODEL), const(8, D_MODEL), const(8, D_CONV)),
        out_shape=(
            jax.ShapeDtypeStruct((n_rows, full), BF16),
            jax.ShapeDtypeStruct(x2.shape, F32),
            jax.ShapeDtypeStruct((N_META, D_MODEL), F32),
            jax.ShapeDtypeStruct((8, D_MODEL), F32),
            jax.ShapeDtypeStruct((8, D_CONV), F32),
        ),
        scratch_shapes=[pltpu.VMEM((8, D_CONV), F32), pltpu.VMEM((N_HEADS, HEAD_DIM, HEAD_DIM), F32)],
        compiler_params=_params(),
    )(drg, dco, do, proj, conv, states, w3, cw8, x2, meta_chunk, dh2, g1, *tables, zeta0, *rot)


def _gw_in_scatter(hnT_tok, hnT_meta, dproj, gw_out_parts, dmeta_rows, acc_conv, acc_b2, acc_ret, acc_f2, me_arr):
    n_rows = dproj.shape[0]
    last = N_DEV - 1
    cols = D_CONV // N_DEV
    by_dest = (True, True, False)
    small_blocks = (gw_out_parts.shape[1:], (SC_CONV0 + 8, CHUNK), (GA_LOSS + 8, CHUNK))

    def body(me_ref, a_ref, am_ref, b_ref, go_ref, dm_ref, ac_ref, ab2_ref, aret_ref, af2_ref,
             land_in, land_go, land_sc, land_ga, d2d_buf, d2d_land, ici_buf, sc_ref, ga_ref,
             d2d_send, d2d_recv, ici_send, ici_recv, send_sems, recv_sems, local_sems):
        del me_ref
        s = pl.program_id(0)
        t = last - s
        q = t >> 1
        x, y, c = lax.axis_index("x"), lax.axis_index("y"), lax.axis_index("c")
        me = 4 * x + 2 * y + c
        chip = 2 * x + y
        srcs, lands = (go_ref, sc_ref, ga_ref), (land_go, land_sc, land_ga)

        def peer_at(k):
            return (1 - x if k & 4 else x, 1 - y if k & 2 else y, 1 - c if k & 1 else c)

        def small_copy(a, k):
            px, py, pc = peer_at(k)
            return pltpu.make_async_remote_copy(
                src_ref=srcs[a].at[4 * px + 2 * py + pc] if by_dest[a] else srcs[a],
                dst_ref=lands[a].at[me],
                send_sem=send_sems.at[a * last + k - 1],
                recv_sem=recv_sems.at[a * last + k - 1],
                device_id=(px, py, pc),
                device_id_type=MESH,
            )

        def small_local(a):
            return pltpu.make_async_copy(srcs[a].at[me] if by_dest[a] else srcs[a], lands[a].at[me], local_sems.at[a + 1])

        def d2d_copy(j):
            return pltpu.make_async_remote_copy(
                src_ref=d2d_buf.at[j],
                dst_ref=d2d_land.at[j],
                send_sem=d2d_send.at[j],
                recv_sem=d2d_recv.at[j],
                device_id=(x, y, 1 - c),
                device_id_type=MESH,
            )

        def ici_copy(j, to):
            return pltpu.make_async_remote_copy(
                src_ref=ici_buf.at[j],
                dst_ref=land_in.at[chip],
                send_sem=ici_send.at[j],
                recv_sem=ici_recv.at[j],
                device_id=to,
                device_id_type=MESH,
            )

        def own_copy():
            return pltpu.make_async_copy(ici_buf.at[0], land_in.at[chip], local_sems.at[0])

        def send_smalls():
            sc_ref[...] = jnp.zeros_like(sc_ref)
            for d in range(N_DEV):
                sc_ref[d, SC_META0 : SC_META0 + N_META, :] = dm_ref[:, d * CHUNK : (d + 1) * CHUNK]
                lane0 = (d * cols) % CHUNK
                rows8 = ac_ref[:, d * cols - lane0 : d * cols - lane0 + CHUNK]
                rows8 = pltpu.roll(rows8, CHUNK - lane0, 1) if lane0 else rows8
                sc_ref[d, SC_CONV0 : SC_CONV0 + 3, 0:cols] = rows8[0:3, 0:cols]
            ga_ref[...] = jnp.zeros_like(ga_ref)
            for row0, acc, n in ((GA_N1, ab2_ref, 8), (GA_RG, aret_ref, 4), (GA_FG, af2_ref, 8)):
                for j in range(n):
                    ga_ref[row0 + j : row0 + j + 1, :] = acc[0:1, j * CHUNK : (j + 1) * CHUNK]
            ga_ref[GA_LOSS : GA_LOSS + 1, :] = af2_ref[1:2, 0:CHUNK]
            for a in range(3):
                small_local(a).start()
            for k in range(last, 0, -1):
                for a in range(3):
                    small_copy(a, k).start()

        blk = jnp.dot(a_ref[...], b_ref[ROW_TILE:, :], preferred_element_type=F32)
        blk += jnp.dot(am_ref[...], b_ref[LIVE0:ROW_TILE, :], preferred_element_type=F32)

        @pl.when((t & 1) == 1)
        def _():
            d2d_buf[q] = blk.astype(BF16)
            d2d_copy(q).start()

        pl.when(s == 0)(send_smalls)

        @pl.when((t & 1) == 0)
        def _():
            d2d_copy(q).wait_recv()
            ici_buf[q] = (blk + d2d_land[q].astype(F32)).astype(BF16)

            @pl.when(t != 0)
            def _():
                ici_copy(q, (jnp.bitwise_xor(x, (t >> 2) & 1), jnp.bitwise_xor(y, (t >> 1) & 1), c)).start()

            @pl.when(t == 0)
            def _():
                own_copy().start()
                for j in range(N_CHIPS):
                    d2d_copy(j).wait_send()
                for j in range(N_CHIPS - 1, 0, -1):
                    ici_copy(j, peer_at(2 * j)).wait()
                for k in range(last, 0, -1):
                    for a in range(3):
                        small_copy(a, k).wait()
                for a in range(3):
                    small_local(a).wait()
                own_copy().wait()

    grid_spec = pltpu.PrefetchScalarGridSpec(
        num_scalar_prefetch=1,
        grid=(N_DEV,),
        in_specs=[_VMEM, _VMEM, pl.BlockSpec((n_rows, BLK), lambda s, me: (0, jnp.bitwise_xor(me[0], last - s))), _HBM]
        + [_VMEM] * 5,
        out_specs=tuple([_HBM] * 4),
        scratch_shapes=[
            pltpu.VMEM((N_CHIPS, D_MODEL, BLK), BF16),
            pltpu.VMEM((N_CHIPS, D_MODEL, BLK), BF16),
            pltpu.VMEM((N_CHIPS, D_MODEL, BLK), BF16),
            pltpu.VMEM((N_DEV, *small_blocks[1]), F32),
            pltpu.VMEM(small_blocks[2], F32),
            pltpu.SemaphoreType.DMA((N_CHIPS,)),
            pltpu.SemaphoreType.DMA((N_CHIPS,)),
            pltpu.SemaphoreType.DMA((N_CHIPS,)),
            pltpu.SemaphoreType.DMA((N_CHIPS,)),
            pltpu.SemaphoreType.DMA((3 * last,)),
            pltpu.SemaphoreType.DMA((3 * last,)),
            pltpu.SemaphoreType.DMA((4,)),
        ],
    )
    return pl.pallas_call(
        body,
        name="gw_in_scatter",
        grid_spec=grid_spec,
        out_shape=(jax.ShapeDtypeStruct((N_CHIPS, D_MODEL, BLK), BF16),)
        + tuple(jax.ShapeDtypeStruct((N_DEV, *b), dt) for b, dt in zip(small_blocks, (gw_out_parts.dtype, F32, F32), strict=True)),
        compiler_params=_params(),
    )(me_arr, hnT_tok, hnT_meta, dproj, gw_out_parts, dmeta_rows, acc_conv, acc_b2, acc_ret, acc_f2)


def _adamw(w, g, m, v):
    m = ADAM_B1 * m + (1.0 - ADAM_B1) * g
    v = ADAM_B2 * v + (1.0 - ADAM_B2) * (g * g)
    m_hat = m / (1.0 - ADAM_B1**ADAM_STEP)
    v_hat = v / (1.0 - ADAM_B2**ADAM_STEP)
    delta = -ADAM_LR * (m_hat / (jnp.sqrt(v_hat) + ADAM_EPS) + ADAM_WD * w)
    return delta, m, v


def _small_leaves(meta, conv_w, n1, rg, fg):
    return meta, conv_w, n1.reshape(8, CHUNK), rg.reshape(4, CHUNK), fg.reshape(8, CHUNK)


def _from_small_leaves(meta, conv_w, n1, rg, fg):
    return meta, conv_w, n1.reshape(D_MODEL), rg.reshape(D_RET), fg.reshape(D_MODEL)


ADAM_GRID = 4


def _adamw_all(land_in, land_out, land_sc, land_ga, big, small):
    n_leaf = 5
    lands = (land_in, land_out)
    n_big = len(big)

    def body(*refs):
        big_in, sc_ref, ga_ref = refs[: 4 * n_big], refs[4 * n_big], refs[4 * n_big + 1]
        small_in = refs[4 * n_big + 2 : 4 * n_big + 2 + 3 * n_leaf]
        big_out = refs[4 * n_big + 2 + 3 * n_leaf : 8 * n_big + 2 + 3 * n_leaf]
        small_out = refs[8 * n_big + 2 + 3 * n_leaf :]
        for b in range(n_big):
            p_ref, w_ref, m_ref, v_ref = big_in[4 * b : 4 * b + 4]
            g_ref, d_ref, nm_ref, nv_ref = big_out[4 * b : 4 * b + 4]
            g = p_ref[0].astype(F32)
            for s in range(1, p_ref.shape[0]):
                g = g + p_ref[s].astype(F32)
            g_ref[...] = g
            d_ref[...], nm_ref[...], nv_ref[...] = _adamw(w_ref[...], g, m_ref[...], v_ref[...])

        @pl.when(pl.program_id(0) == 0)
        def _():
            sc, ga = sc_ref[0], ga_ref[0]
            for s in range(1, N_DEV):
                sc = sc + sc_ref[s]
                ga = ga + ga_ref[s]
            grads = (
                sc[SC_META0 : SC_META0 + N_META],
                sc[SC_CONV0 : SC_CONV0 + 3, 0 : D_CONV // N_DEV],
                ga[GA_N1 : GA_N1 + 8],
                ga[GA_RG : GA_RG + 4],
                ga[GA_FG : GA_FG + 8],
            )
            for leaf, g in enumerate(grads):
                d, nm, nv = _adamw(
                    small_in[leaf][...], g, small_in[n_leaf + leaf][...], small_in[2 * n_leaf + leaf][...]
                )
                small_out[leaf][...] = g
                small_out[n_leaf + leaf][...] = d
                small_out[2 * n_leaf + leaf][...] = nm
                small_out[3 * n_leaf + leaf][...] = nv
            small_out[4 * n_leaf][...] = ga[GA_LOSS : GA_LOSS + 1]

    in_specs, out_specs, out_shape, operands = [], [], [], []
    for land, (w, m, v) in zip(lands, big, strict=True):
        n_r, n_c = w.shape
        tr = n_r // ADAM_GRID
        blk = pl.BlockSpec((tr, n_c), lambda i: (i, 0))
        in_specs += [pl.BlockSpec((land.shape[0], tr, n_c), lambda i: (0, i, 0)), blk, blk, blk]
        out_specs += [blk] * 4
        out_shape += [jax.ShapeDtypeStruct((n_r, n_c), F32)] * 4
        operands += [land, w, m, v]
    small_shapes = [jax.ShapeDtypeStruct(a.shape, F32) for a in small[0]] * 4 + [jax.ShapeDtypeStruct((1, CHUNK), F32)]
    out = pl.pallas_call(
        body,
        name="adamw_all",
        grid=(ADAM_GRID,),
        in_specs=in_specs + [_VMEM] * (2 + 3 * n_leaf),
        out_specs=tuple(out_specs + [_VMEM] * len(small_shapes)),
        out_shape=tuple(out_shape + small_shapes),
        compiler_params=_params(),
    )(*operands, land_sc, land_ga, *small[0], *small[1], *small[2])
    bigs = tuple(out[4 * b : 4 * b + 4] for b in range(n_big))
    rest = out[4 * n_big :]
    return bigs, tuple(rest[k * n_leaf : (k + 1) * n_leaf] for k in range(4)), rest[4 * n_leaf]


def kernel(x, meta, norm1_g, w_in, conv_w, ret_norm_g, w_out, final_g, loss_target, m_meta, m_norm1_g, m_w_in, m_conv_w, m_ret_norm_g, m_w_out, m_final_g, v_meta, v_norm1_g, v_w_in, v_conv_w, v_ret_norm_g, v_w_out, v_final_g):
    seq = x.shape[1]
    assert x.shape == (1, seq, D_MODEL) and seq % ROW_TILE == 0
    n_tiles = seq // ROW_TILE + 1
    x2, t2 = x[0], loss_target[0]

    small_w = _small_leaves(meta, conv_w, norm1_g, ret_norm_g, final_g)
    g1 = norm1_g.reshape(1, D_MODEL)
    hn_tok, hnT_tok, w3, meta_chunk, cw8 = _gather_weights_and_norm(x2, g1, w_in, meta, conv_w)

    rot = _rotary_tables(n_tiles, ROW_TILE)
    dec, _, zeta, xi = _decay_tables()
    gr = ret_norm_g.reshape(1, D_RET)
    gf = final_g.reshape(1, D_MODEL)

    hnT_meta, proj, conv_out, conv, o, states, wo3 = _f1(hn_tok, meta_chunk, g1, w3, cw8, rot, dec, zeta, xi, w_out)
    wout = wo3.reshape(D_MODEL, D_MODEL)
    dh2, dco, drg, do, gw_out, acc_f2, acc_ret = _f2(proj, o, conv_out, x2, t2, wout, gr, gf)
    dproj, grad_x2, dmeta_rows, acc_b2, acc_conv = _b2(
        drg, dco, do, proj, conv, states, w3, cw8, x2, meta_chunk, dh2, g1, _decay_tables(B1_CHUNK), zeta, rot
    )
    gw_out_parts = gw_out.reshape(N_DEV, D_MODEL // N_DEV, D_MODEL)

    me_arr = (4 * lax.axis_index("x") + 2 * lax.axis_index("y") + lax.axis_index("c")).astype(jnp.int32).reshape(1)
    land_in, land_out, land_sc, land_ga = _gw_in_scatter(
        hnT_tok, hnT_meta, dproj, gw_out_parts, dmeta_rows, acc_conv, acc_b2, acc_ret, acc_f2, me_arr
    )

    (w_in_out, w_out_out), small_out, loss_row = _adamw_all(
        land_in,
        land_out,
        land_sc,
        land_ga,
        ((w_in, m_w_in, v_w_in), (w_out, m_w_out, v_w_out)),
        (
            small_w,
            _small_leaves(m_meta, m_conv_w, m_norm1_g, m_ret_norm_g, m_final_g),
            _small_leaves(v_meta, v_conv_w, v_norm1_g, v_ret_norm_g, v_final_g),
        ),
    )
    g_w_in, d_w_in, nm_w_in, nv_w_in = w_in_out
    g_w_out, d_w_out, nm_w_out, nv_w_out = w_out_out
    loss = loss_row[0, 0]
    grad_x = grad_x2.reshape(1, seq, D_MODEL)

    def leaves(w_in_leaf, w_out_leaf, small_leaves):
        meta_leaf, conv_leaf, n1_leaf, rg_leaf, fg_leaf = _from_small_leaves(*small_leaves)
        return (meta_leaf, n1_leaf, w_in_leaf, conv_leaf, rg_leaf, w_out_leaf, fg_leaf)

    return (
        loss,
        grad_x,
        *leaves(g_w_in, g_w_out, small_out[0]),
        *leaves(d_w_in, d_w_out, small_out[1]),
        *leaves(nm_w_in, nm_w_out, small_out[2]),
        *leaves(nv_w_in, nv_w_out, small_out[3]),
    )
```

```python
import math

import jax
import jax.numpy as jnp
import numpy as np
from jax import lax
from jax.experimental import pallas as pl
from jax.experimental.pallas import tpu as pltpu

F32 = jnp.float32
BF16 = jnp.bfloat16

N_DEV = 8
N_CHIPS = 4
D_MODEL = 1024
N_META = 16
CHUNK = 128
D_CONV = 512
D_RET = 512
N_HEADS = 4
HEAD_DIM = 128
N_PROJ = 8
BLK = 512
ROPE_BASE = 10000.0
EPS = 1e-6
Q_SCALE = HEAD_DIM ** -0.5
LOG_G = tuple(math.log(1.0 - 2.0 ** (-5.0 - h)) for h in range(N_HEADS))
CHUNK_DECAY = tuple(math.exp(CHUNK * lg) for lg in LOG_G)

ADAM_LR = 0.001
ADAM_B1 = 0.9
ADAM_B2 = 0.999
ADAM_EPS = 1e-08
ADAM_WD = 0.01
ADAM_STEP = 10

ROW_TILE = 512
PAD_ROWS = ROW_TILE - N_META
LIVE0 = ROW_TILE - CHUNK
B1_CHUNK = 256
B2_ROWS = ROW_TILE // 2
VMEM_LIMIT = 56 * 1024 * 1024

SC_META0, SC_CONV0 = 0, 16
GA_N1, GA_RG, GA_FG, GA_LOSS = 0, 8, 16, 24

NT = (((1,), (1,)), ((), ()))
TN = (((0,), (0,)), ((), ()))
MESH = pl.DeviceIdType.MESH

_VMEM = pl.BlockSpec(memory_space=pltpu.VMEM)
_HBM = pl.BlockSpec(memory_space=pltpu.HBM)


def _params(n_axes=1):
    return pltpu.CompilerParams(dimension_semantics=("arbitrary",) * n_axes, vmem_limit_bytes=VMEM_LIMIT)


def _sigmoid(x):
    return 0.5 * jnp.tanh(0.5 * x) + 0.5


def _decay_tables(chunk=CHUNK):
    idx = np.arange(chunk, dtype=np.float64)
    diff = idx[:, None] - idx[None, :]
    dec = np.stack([np.where(diff >= 0, np.exp(diff * lg), 0.0) for lg in LOG_G])
    zeta = np.stack([np.exp((chunk - 1 - idx) * lg) for lg in LOG_G])
    xi = np.stack([np.exp((idx + 1.0) * lg) for lg in LOG_G])
    ones = np.ones((1, 1, HEAD_DIM))
    return (
        jnp.asarray(dec, F32),
        jnp.asarray(dec.transpose(0, 2, 1), F32),
        jnp.asarray(zeta[:, :, None] * ones, F32),
        jnp.asarray(xi[:, :, None] * ones, F32),
    )


def _rotary_tables(n_tiles, tm):
    half = HEAD_DIM // 2
    freqs = (1.0 / (np.float32(ROPE_BASE) ** (np.arange(half, dtype=np.float32) / np.float32(half)))).astype(np.float64)
    sign = np.concatenate([-np.ones(half), np.ones(half)])
    two = lambda a: np.concatenate([a, a], axis=1)
    base = two((np.arange(n_tiles, dtype=np.float64) * tm - PAD_ROWS)[:, None] * freqs[None, :])
    off = two(np.arange(tm, dtype=np.float64)[:, None] * freqs[None, :])
    as32 = lambda a: jnp.asarray(a, F32)
    return as32(np.cos(base)), as32(np.sin(base) * sign), as32(np.cos(off)), as32(np.sin(off) * sign)


def _tile_rotary(ca_ref, sa_ref, cb_ref, sb_ref, tile):
    ca, sa = ca_ref[pl.ds(tile, 1), :], sa_ref[pl.ds(tile, 1), :]
    cb, sb = cb_ref[...], sb_ref[...]
    return ca * cb - sa * sb, sa * cb + ca * sb


def _rot(t, cos2, sin2):
    return t * cos2 + pltpu.roll(t, HEAD_DIM // 2, 1) * sin2


def _rot_bwd(d, cos2, sin2):
    return d * cos2 + pltpu.roll(d * sin2, HEAD_DIM // 2, 1)


def _token_tile(i):
    return jnp.maximum(i - 1, 0)


GATHER_PAIRS = 7


def _two_level_gather(src_ref, out_ref, send_sems, recv_sems, local_sem, base=0):
    x, y, c = lax.axis_index("x"), lax.axis_index("y"), lax.axis_index("c")
    me, sibling = (x, y, c), (x, y, 1 - c)
    xnb, ynb, diag = (1 - x, y), (x, 1 - y), (1 - x, 1 - y)
    relayed = (jnp.bitwise_xor(x, 1 - c), jnp.bitwise_xor(y, c))
    other = (jnp.bitwise_xor(x, c), jnp.bitwise_xor(y, 1 - c))

    def copy(k, block, to, src=None):
        dst = out_ref.at[4 * block[0] + 2 * block[1] + block[2]]
        return pltpu.make_async_remote_copy(
            src_ref=dst if src is None else src,
            dst_ref=dst,
            send_sem=send_sems.at[base + k],
            recv_sem=recv_sems.at[base + k],
            device_id=to,
            device_id_type=MESH,
        )

    mine = pltpu.make_async_copy(src_ref, out_ref.at[4 * x + 2 * y + c], local_sem)
    first = [copy(1, me, (*xnb, c), src=src_ref), copy(2, me, (*ynb, c), src=src_ref), copy(0, me, sibling, src=src_ref)]
    relay = copy(3, (*relayed, c), (*other, c))
    passed = [copy(4 + j, (*chip, c), sibling) for j, chip in enumerate((xnb, ynb, diag))]

    def start():
        mine.start()
        for cp in first:
            cp.start()

    def forward():
        copy(1 + c, (*relayed, c), me).wait_recv()
        relay.start()
        copy(2 - c, (*other, c), me).wait_recv()
        passed[0].start()
        passed[1].start()

    def forward_relayed():
        copy(3, (*diag, c), me).wait_recv()
        passed[2].start()

    def finish():
        copy(0, sibling, me).wait_recv()
        for j, chip in enumerate((xnb, ynb, diag)):
            copy(4 + j, (*chip, 1 - c), me).wait_recv()
        for cp in first + [relay] + passed:
            cp.wait_send()
        mine.wait()

    return start, forward, forward_relayed, finish


W_IN_CHUNKS = 4


def _gather_weights_and_norm(x2, g1, w_shard, meta, conv_w):
    per = GATHER_PAIRS
    n_ch = W_IN_CHUNKS
    rows = w_shard.shape[0] // n_ch
    tm = ROW_TILE
    nt = x2.shape[0] // tm
    cols = D_CONV // N_DEV
    pack_shape = (SC_CONV0 + 8, CHUNK)
    starts = [0] * (n_ch + 1)
    forwards = [min(5 + 2 * p, nt - 1) for p in range(n_ch)] + [min(12, nt - 1)]
    relayeds = [min(12 + p, nt - 1) for p in range(n_ch)] + [nt - 1]
    finishes = [nt - 1] * (n_ch + 1)

    def body(x_ref, g_ref, w_ref, meta_ref, conv_ref, hn_ref, hnT_ref, w_all_ref, mc_ref, cw_ref,
             w_bf, small_ref, small_all_ref, send_sems, recv_sems, local_sems):
        i = pl.program_id(0)

        @pl.when(i == 0)
        def _():
            w_bf[...] = w_ref[...].astype(BF16)
            small_ref[...] = jnp.zeros_like(small_ref)
            small_ref[SC_META0 : SC_META0 + N_META, :] = meta_ref[...]
            small_ref[SC_CONV0 : SC_CONV0 + 3, 0:cols] = conv_ref[...]

        parts = [
            _two_level_gather(
                w_bf.at[pl.ds(ch * rows, rows)],
                w_all_ref.at[:, pl.ds(ch * rows, rows)],
                send_sems,
                recv_sems,
                local_sems.at[ch],
                ch * per,
            )
            for ch in range(n_ch)
        ]
        parts.append(_two_level_gather(small_ref, small_all_ref, send_sems, recv_sems, local_sems.at[n_ch], n_ch * per))
        def run(phase, steps):
            for part, step in zip(parts, steps, strict=True):
                pl.when(i == step)(part[phase])

        run(0, starts)
        h = x_ref[...]
        r = lax.rsqrt(jnp.mean(h * h, axis=-1, keepdims=True) + EPS)
        hn = (h * r * g_ref[...]).astype(BF16)
        hn_ref[...] = hn
        hnT_ref[...] = hn.T
        run(1, forwards)
        run(2, relayeds)
        run(3, finishes)

        @pl.when(i == nt - 1)
        def _():
            mc_ref[0 : CHUNK - N_META, :] = jnp.zeros((CHUNK - N_META, D_MODEL), F32)
            cw_ref[...] = jnp.zeros_like(cw_ref)
            for d in range(N_DEV):
                mc_ref[CHUNK - N_META : CHUNK, d * CHUNK : (d + 1) * CHUNK] = small_all_ref[d, SC_META0 : SC_META0 + N_META, :]
                lane0 = (d * cols) % CHUNK
                rows8 = small_all_ref[d, SC_CONV0 : SC_CONV0 + 8, :]
                rows8 = pltpu.roll(rows8, lane0, 1) if lane0 else rows8
                cw_ref[0:3, d * cols : (d + 1) * cols] = rows8[0:3, lane0 : lane0 + cols]

    const = lambda r, c: pl.BlockSpec((r, c), lambda i: (0, 0))
    return pl.pallas_call(
        body,
        name="gather_weights_norm",
        grid=(nt,),
        out_shape=(
            jax.ShapeDtypeStruct(x2.shape, BF16),
            jax.ShapeDtypeStruct(x2.shape[::-1], BF16),
            jax.ShapeDtypeStruct((N_DEV, *w_shard.shape), BF16),
            jax.ShapeDtypeStruct((CHUNK, D_MODEL), F32),
            jax.ShapeDtypeStruct((8, D_CONV), F32),
        ),
        in_specs=[pl.BlockSpec((tm, D_MODEL), lambda i: (i, 0)), _VMEM, _VMEM, _VMEM, _VMEM],
        out_specs=(
            pl.BlockSpec((tm, D_MODEL), lambda i: (i, 0)),
            pl.BlockSpec((D_MODEL, tm), lambda i: (0, i)),
            _HBM,
            const(CHUNK, D_MODEL),
            const(8, D_CONV),
        ),
        scratch_shapes=[
            pltpu.VMEM(w_shard.shape, BF16),
            pltpu.VMEM(pack_shape, F32),
            pltpu.VMEM((N_DEV, *pack_shape), F32),
            pltpu.SemaphoreType.DMA(((n_ch + 1) * per,)),
            pltpu.SemaphoreType.DMA(((n_ch + 1) * per,)),
            pltpu.SemaphoreType.DMA((n_ch + 1,)),
        ],
        compiler_params=_params(),
    )(x2, g1, w_shard, meta, conv_w)


def _f1(hn_tok, meta_chunk, g1, w3, cw8, rot, dec, zeta, xi, wo_shard):
    tm = ROW_TILE
    nt = hn_tok.shape[0] // tm + 1
    n_rows = nt * tm
    nct = tm // CHUNK

    def body(hn_ref, mc_ref, g_ref, w_ref, cw_ref, ca_ref, sa_ref, cb_ref, sb_ref, dec_ref, zeta_ref, xi_ref, wo_ref,
             hnT_ref, pr_ref, co_ref, cv_ref, o_ref, st_ref, wo_all_ref,
             halo, state, wo_bf, send_sems, recv_sems, local_sem):
        i = pl.program_id(0)

        @pl.when(i == 0)
        def _():
            wo_bf[...] = wo_ref[...].astype(BF16)

        wo_phases = _two_level_gather(wo_bf, wo_all_ref, send_sems, recv_sems, local_sem)
        for step, phase in zip((0, nt // 3, 2 * nt // 3, nt - 1), wo_phases, strict=True):
            pl.when(i == step)(phase)

        def work(hn, r0):
            rs = slice(r0, tm)
            n = tm - r0

            def proj(j):
                return jnp.dot(hn, w_ref[j], preferred_element_type=F32)

            cos_all, sin_all = _tile_rotary(ca_ref, sa_ref, cb_ref, sb_ref, i)
            cos_t, sin_t = cos_all[rs, :], sin_all[rs, :]
            q, k, v = proj(4), proj(5), proj(6).astype(BF16)
            pr_ref[rs, 6 * BLK : 7 * BLK] = v
            chunks = range(r0 // CHUNK, tm // CHUNK)
            heads = []
            for hd in range(N_HEADS):
                cs = slice(hd * HEAD_DIM, (hd + 1) * HEAD_DIM)
                qh = (_rot(q[:, cs], cos_t, sin_t) * Q_SCALE).astype(BF16)
                kh = _rot(k[:, cs], cos_t, sin_t).astype(BF16)
                vh = v[:, cs]
                pr_ref[rs, 4 * BLK + hd * HEAD_DIM : 4 * BLK + (hd + 1) * HEAD_DIM] = qh
                pr_ref[rs, 5 * BLK + hd * HEAD_DIM : 5 * BLK + (hd + 1) * HEAD_DIM] = kh
                upd = {}
                for c in chunks:
                    cr = slice(c * CHUNK - r0, (c + 1) * CHUNK - r0)
                    kz = (kh[cr].astype(F32) * zeta_ref[hd]).astype(BF16)
                    upd[c] = lax.dot_general(kz, vh[cr], TN, preferred_element_type=F32)
                st = state[hd]
                outs = []
                for c in chunks:
                    cr = slice(c * CHUNK - r0, (c + 1) * CHUNK - r0)
                    qc, kc, vc = qh[cr], kh[cr], vh[cr]
                    st_bf = st.astype(BF16)
                    st_ref[c, hd] = st_bf
                    s = lax.dot_general(qc, kc, NT, preferred_element_type=F32) * dec_ref[hd]
                    inner = jnp.dot(s.astype(BF16), vc, preferred_element_type=F32)
                    qx = (qc.astype(F32) * xi_ref[hd]).astype(BF16)
                    outs.append(inner + jnp.dot(qx, st_bf, preferred_element_type=F32))
                    st = CHUNK_DECAY[hd] * st + upd[c]
                state[hd] = st
                heads.append(jnp.concatenate(outs, axis=0) if len(outs) > 1 else outs[0])
            o_ref[rs, :] = jnp.concatenate(heads, axis=1)

            cx, cb, cc, cg = proj(0), proj(1), proj(2), proj(3)
            u = cc * cx
            rows = lax.broadcasted_iota(jnp.int32, u.shape, 0)
            hl = halo[...]
            u1 = jnp.where(rows == 0, hl[7:8], pltpu.roll(u, 1, 0))
            u2 = jnp.where(rows == 0, hl[6:7], jnp.where(rows == 1, hl[7:8], pltpu.roll(u, 2, 0)))
            halo[...] = u[n - 8 : n]
            cw = cw_ref[...]
            conv = cw[0:1] * u2 + cw[1:2] * u1 + cw[2:3] * u
            co = (cb * conv * (cg * _sigmoid(cg))).astype(BF16)
            co_ref[rs, :] = co
            cv_ref[rs, :] = conv.astype(BF16)
            pr_ref[rs, 0 * BLK : 1 * BLK] = cx.astype(BF16)
            pr_ref[rs, 1 * BLK : 2 * BLK] = cb.astype(BF16)
            pr_ref[rs, 2 * BLK : 3 * BLK] = cc.astype(BF16)
            pr_ref[rs, 3 * BLK : 4 * BLK] = cg.astype(BF16)
            pr_ref[rs, 7 * BLK : 8 * BLK] = proj(7).astype(BF16)

        @pl.when(i == 0)
        def _():
            halo[...] = jnp.zeros_like(halo)
            state[...] = jnp.zeros_like(state)
            h = mc_ref[...]
            r = lax.rsqrt(jnp.mean(h * h, axis=-1, keepdims=True) + EPS)
            hn = (h * r * g_ref[...]).astype(BF16)
            hnT_ref[...] = hn.T
            work(hn, LIVE0)

        @pl.when(i > 0)
        def _():
            work(hn_ref[...], 0)

    row = lambda w: pl.BlockSpec((tm, w), lambda i: (i, 0))
    return pl.pallas_call(
        body,
        name="f1_inproj_conv",
        grid=(nt,),
        in_specs=[pl.BlockSpec((tm, D_MODEL), lambda i: (_token_tile(i), 0))] + [_VMEM] * 12,
        out_specs=(
            pl.BlockSpec((D_MODEL, CHUNK), lambda i: (0, 0)),
            row(N_PROJ * BLK),
            row(D_CONV),
            row(D_CONV),
            row(D_RET),
            pl.BlockSpec((nct, N_HEADS, HEAD_DIM, HEAD_DIM), lambda i: (i, 0, 0, 0)),
            _HBM,
        ),
        out_shape=(
            jax.ShapeDtypeStruct((D_MODEL, CHUNK), BF16),
            jax.ShapeDtypeStruct((n_rows, N_PROJ * BLK), BF16),
            jax.ShapeDtypeStruct((n_rows, D_CONV), BF16),
            jax.ShapeDtypeStruct((n_rows, D_CONV), BF16),
            jax.ShapeDtypeStruct((n_rows, D_RET), F32),
            jax.ShapeDtypeStruct((n_rows // CHUNK, N_HEADS, HEAD_DIM, HEAD_DIM), BF16),
            jax.ShapeDtypeStruct((N_DEV, *wo_shard.shape), BF16),
        ),
        scratch_shapes=[
            pltpu.VMEM((8, D_CONV), F32),
            pltpu.VMEM((N_HEADS, HEAD_DIM, HEAD_DIM), F32),
            pltpu.VMEM(wo_shard.shape, BF16),
            pltpu.SemaphoreType.DMA((GATHER_PAIRS,)),
            pltpu.SemaphoreType.DMA((GATHER_PAIRS,)),
            pltpu.SemaphoreType.DMA(()),
        ],
        compiler_params=_params(),
    )(hn_tok, meta_chunk, g1, w3, cw8, *rot, dec, zeta, xi, wo_shard)


def _group_norm(o):
    ys, rs = [], []
    for hd in range(N_HEADS):
        oh = o[:, hd * HEAD_DIM : (hd + 1) * HEAD_DIM]
        xc = oh - jnp.mean(oh, axis=-1, keepdims=True)
        rstd = lax.rsqrt(jnp.mean(xc * xc, axis=-1, keepdims=True) + EPS)
        ys.append(xc * rstd)
        rs.append(jnp.broadcast_to(rstd, oh.shape))
    return jnp.concatenate(ys, axis=1), jnp.concatenate(rs, axis=1)


def _f2(proj, o, conv_out, x2, t2, wout, gr, gf):
    n_rows = proj.shape[0]
    tm = ROW_TILE
    nt = n_rows // tm

    def body(o_ref, rg_ref, co_ref, x_ref, t_ref, wo_ref, gr_ref, gf_ref,
             dh2_ref, dco_ref, drg_ref, do_ref, gw_ref, acc_ref, accr_ref, gacc):
        i = pl.program_id(0)

        @pl.when(i == 0)
        def _():
            acc_ref[...] = jnp.zeros_like(acc_ref)
            accr_ref[...] = jnp.zeros_like(accr_ref)
            gacc[...] = jnp.zeros_like(gacc)
            dh2_ref[...] = jnp.zeros_like(dh2_ref)
            dco_ref[...] = jnp.zeros_like(dco_ref)
            drg_ref[...] = jnp.zeros_like(drg_ref)
            do_ref[...] = jnp.zeros_like(do_ref)

        @pl.when(i > 0)
        def _():
            yh, rstd = _group_norm(o_ref[...])
            rg = rg_ref[...].astype(F32)
            sg = _sigmoid(rg)
            silu = rg * sg
            grv = gr_ref[...]
            ro = (yh * grv * silu).astype(BF16)
            h2 = (
                x_ref[...]
                + jnp.dot(co_ref[...], wo_ref[0:D_CONV], preferred_element_type=F32)
                + jnp.dot(ro, wo_ref[D_CONV:], preferred_element_type=F32)
            )
            r2 = lax.rsqrt(jnp.mean(h2 * h2, axis=-1, keepdims=True) + EPS)
            yn = h2 * r2
            gfv = gf_ref[...]
            err = yn * gfv - t_ref[...]
            tile_loss = jnp.sum(jnp.sum(err * err, axis=-1, keepdims=True), axis=0, keepdims=True) * (0.5 / D_MODEL)
            acc_ref[0:1, :] += jnp.sum(err * yn, axis=0, keepdims=True) * (1.0 / D_MODEL)
            acc_ref[1:2, :] += tile_loss
            dyn = err * (gfv * (1.0 / D_MODEL))
            dh2 = r2 * (dyn - yn * jnp.mean(dyn * yn, axis=-1, keepdims=True))
            dh2_ref[...] = dh2
            dh2_bf = dh2.astype(BF16)
            dmix = lax.dot_general(dh2_bf, wo_ref[...], NT, preferred_element_type=F32)
            dco_ref[...] = dmix[:, :D_CONV].astype(BF16)
            dro = dmix[:, D_CONV:]
            dsilu = sg * (1.0 + rg * (1.0 - sg))
            drg_ref[...] = (dro * (yh * grv) * dsilu).astype(BF16)
            dret = dro * silu
            accr_ref[0:1, :] += jnp.sum(dret * yh, axis=0, keepdims=True)
            dyh = dret * grv
            for hd in range(N_HEADS):
                cs = slice(hd * HEAD_DIM, (hd + 1) * HEAD_DIM)
                a, b = dyh[:, cs], yh[:, cs]
                do_ref[:, cs] = (
                    rstd[:, cs]
                    * (a - jnp.mean(a, axis=-1, keepdims=True) - b * jnp.mean(a * b, axis=-1, keepdims=True))
                ).astype(BF16)
            gacc[0:D_CONV, :] += lax.dot_general(co_ref[...], dh2_bf, TN, preferred_element_type=F32)
            gacc[D_CONV:, :] += lax.dot_general(ro, dh2_bf, TN, preferred_element_type=F32)

        @pl.when(i == nt - 1)
        def _():
            gw_ref[...] = gacc[...].astype(BF16)

    row = lambda w, j=0: pl.BlockSpec((tm, w), lambda i: (i, j))
    tok = pl.BlockSpec((tm, D_MODEL), lambda i: (_token_tile(i), 0))
    return pl.pallas_call(
        body,
        name="f2_out_loss",
        grid=(nt,),
        in_specs=[row(D_RET), row(BLK, 7), row(D_CONV), tok, tok] + [_VMEM] * 3,
        out_specs=(
            row(D_MODEL),
            row(D_CONV),
            row(BLK),
            row(D_RET),
            pl.BlockSpec((D_MODEL, D_MODEL), lambda i: (0, 0)),
            pl.BlockSpec((8, D_MODEL), lambda i: (0, 0)),
            pl.BlockSpec((8, D_RET), lambda i: (0, 0)),
        ),
        out_shape=(
            jax.ShapeDtypeStruct((n_rows, D_MODEL), F32),
            jax.ShapeDtypeStruct((n_rows, D_CONV), BF16),
            jax.ShapeDtypeStruct((n_rows, BLK), BF16),
            jax.ShapeDtypeStruct((n_rows, D_RET), BF16),
            jax.ShapeDtypeStruct((D_MODEL, D_MODEL), BF16),
            jax.ShapeDtypeStruct((8, D_MODEL), F32),
            jax.ShapeDtypeStruct((8, D_RET), F32),
        ),
        scratch_shapes=[pltpu.VMEM((D_MODEL, D_MODEL), F32)],
        compiler_params=_params(),
    )(o, proj, conv_out, x2, t2, wout, gr, gf)


def _b2(drg, dco, do, proj, conv, states, w3, cw8, x2, meta_chunk, dh2, g1, tables, zeta0, rot):
    n_rows = drg.shape[0]
    tm = ROW_TILE
    nt = n_rows // tm
    last = N_PROJ - 1
    chunk = B1_CHUNK
    nct = tm // chunk
    decay = tuple(math.exp(chunk * lg) for lg in LOG_G)
    Q0, K0, V0 = 4 * BLK, 5 * BLK, 6 * BLK

    def body(drg_ref, dco_ref, do_ref, pr_ref, cv_ref, st_ref, w_ref, cw_ref, x_ref, mc_ref, dh2_ref, g_ref,
             dec_ref, dect_ref, zeta_ref, xi_ref, zeta0_ref, ca_ref, sa_ref, cb_ref, sb_ref,
             dpc_ref, gx_ref, dm_ref, acc_ref, accc_ref, halo, dstate):
        i = pl.program_id(0)
        tile = nt - 1 - i
        cos_all, sin_all = _tile_rotary(ca_ref, sa_ref, cb_ref, sb_ref, tile)

        @pl.when(i == 0)
        def _():
            acc_ref[...] = jnp.zeros_like(acc_ref)
            accc_ref[...] = jnp.zeros_like(accc_ref)
            halo[...] = jnp.zeros_like(halo)
            dstate[...] = jnp.zeros_like(dstate)

        def retention_backward():
            for hd in range(N_HEADS):
                cs = slice(hd * HEAD_DIM, (hd + 1) * HEAD_DIM)
                qs, ks, vs = (slice(c0 + hd * HEAD_DIM, c0 + (hd + 1) * HEAD_DIM) for c0 in (Q0, K0, V0))
                dupd = {}
                for c in range(nct):
                    rs = slice(c * chunk, (c + 1) * chunk)
                    qx = (pr_ref[rs, qs].astype(F32) * xi_ref[hd]).astype(BF16)
                    dupd[c] = lax.dot_general(qx, do_ref[rs, cs], TN, preferred_element_type=F32)
                dst = dstate[hd]
                for c in reversed(range(nct)):
                    rs = slice(c * chunk, (c + 1) * chunk)
                    cos_t, sin_t = cos_all[rs, :], sin_all[rs, :]
                    q, k, v, do = pr_ref[rs, qs], pr_ref[rs, ks], pr_ref[rs, vs], do_ref[rs, cs]
                    st_bf = st_ref[c * (chunk // CHUNK), hd]
                    dst_bf = dst.astype(BF16)
                    zt, xt = zeta_ref[hd], xi_ref[hd]
                    sT = (lax.dot_general(k, q, NT, preferred_element_type=F32) * dect_ref[hd]).astype(BF16)
                    dsT = (lax.dot_general(v, do, NT, preferred_element_type=F32) * dect_ref[hd]).astype(BF16)
                    ds = (lax.dot_general(do, v, NT, preferred_element_type=F32) * dec_ref[hd]).astype(BF16)
                    kz = (k.astype(F32) * zt).astype(BF16)
                    dv = jnp.dot(sT, do, preferred_element_type=F32) + jnp.dot(kz, dst_bf, preferred_element_type=F32)
                    dq = jnp.dot(ds, k, preferred_element_type=F32) + xt * lax.dot_general(
                        do, st_bf, NT, preferred_element_type=F32
                    )
                    dk = jnp.dot(dsT, q, preferred_element_type=F32) + zt * lax.dot_general(
                        v, dst_bf, NT, preferred_element_type=F32
                    )
                    dst = decay[hd] * dst + dupd[c]
                    dpc_ref[rs, qs] = _rot_bwd(dq * Q_SCALE, cos_t, sin_t).astype(BF16)
                    dpc_ref[rs, ks] = _rot_bwd(dk, cos_t, sin_t).astype(BF16)
                    dpc_ref[rs, vs] = dv.astype(BF16)
                dstate[hd] = dst

        def state_backward(rs):
            cos_t, sin_t = cos_all[rs, :], sin_all[rs, :]
            for hd in range(N_HEADS):
                qs, ks, vs = (slice(c0 + hd * HEAD_DIM, c0 + (hd + 1) * HEAD_DIM) for c0 in (Q0, K0, V0))
                k, v = pr_ref[rs, ks], pr_ref[rs, vs]
                dst_bf = dstate[hd].astype(BF16)
                zt = zeta0_ref[hd]
                kz = (k.astype(F32) * zt).astype(BF16)
                dv = jnp.dot(kz, dst_bf, preferred_element_type=F32)
                dk = zt * lax.dot_general(v, dst_bf, NT, preferred_element_type=F32)
                dpc_ref[rs, qs] = jnp.zeros((tm - LIVE0, HEAD_DIM), BF16)
                dpc_ref[rs, ks] = _rot_bwd(dk, cos_t, sin_t).astype(BF16)
                dpc_ref[rs, vs] = dv.astype(BF16)

        def work(h, r0, others, r1=tm):
            rs = slice(r0, r1)
            n = r1 - r0
            dco = dco_ref[rs, :].astype(F32)
            cx = pr_ref[rs, 0 * BLK : 1 * BLK].astype(F32)
            cb = pr_ref[rs, 1 * BLK : 2 * BLK].astype(F32)
            cc = pr_ref[rs, 2 * BLK : 3 * BLK].astype(F32)
            cg = pr_ref[rs, 3 * BLK : 4 * BLK].astype(F32)
            conv = cv_ref[rs, :].astype(F32)
            sg = _sigmoid(cg)
            silu, dsilu = cg * sg, sg * (1.0 + cg * (1.0 - sg))
            t = dco * cb
            dcb = (dco * conv * silu).astype(BF16)
            dcg = (t * conv * dsilu).astype(BF16)
            dconv = t * silu
            rows = lax.broadcasted_iota(jnp.int32, dconv.shape, 0)
            hl = halo[...]
            dc1 = jnp.where(rows == n - 1, hl[0:1], pltpu.roll(dconv, n - 1, 0))
            dc2 = jnp.where(rows == n - 2, hl[0:1], jnp.where(rows == n - 1, hl[1:2], pltpu.roll(dconv, n - 2, 0)))
            halo[...] = dconv[0:8]
            cw = cw_ref[...]
            du = cw[2:3] * dconv + cw[1:2] * dc1 + cw[0:1] * dc2
            u = cc * cx
            accc_ref[0:1, :] += jnp.sum(u * dc2, axis=0, keepdims=True)
            accc_ref[1:2, :] += jnp.sum(u * dc1, axis=0, keepdims=True)
            accc_ref[2:3, :] += jnp.sum(u * dconv, axis=0, keepdims=True)
            dcx = (du * cc).astype(BF16)
            dcc = (du * cx).astype(BF16)
            dhn = None
            for blk, ref, c0 in others:
                d = lax.dot_general(ref[rs, c0 : c0 + BLK], w_ref[blk], NT, preferred_element_type=F32)
                dhn = d if dhn is None else dhn + d
            for blk, d in ((1, dcb), (3, dcg), (0, dcx), (2, dcc)):
                dpc_ref[rs, blk * BLK : (blk + 1) * BLK] = d
                dhn += lax.dot_general(d, w_ref[blk], NT, preferred_element_type=F32)
            r = lax.rsqrt(jnp.mean(h * h, axis=-1, keepdims=True) + EPS)
            hh = h * r
            acc_ref[0:1, :] += jnp.sum(dhn * hh, axis=0, keepdims=True)
            dg = dhn * g_ref[...]
            return r * (dg - hh * jnp.mean(dg * hh, axis=-1, keepdims=True))

        @pl.when(tile == 0)
        def _():
            state_backward(slice(LIVE0, tm))
            dpc_ref[LIVE0:tm, last * BLK :] = jnp.zeros((tm - LIVE0, BLK), BF16)
            dh = work(mc_ref[...], LIVE0, ((5, dpc_ref, K0), (6, dpc_ref, V0)))
            dm_ref[...] = dh[CHUNK - N_META : CHUNK]

        @pl.when(tile > 0)
        def _():
            retention_backward()
            dpc_ref[:, last * BLK :] = drg_ref[...]
            others = ((last, drg_ref, 0), (4, dpc_ref, Q0), (5, dpc_ref, K0), (6, dpc_ref, V0))
            for r0 in range(tm - B2_ROWS, -1, -B2_ROWS):
                r1 = r0 + B2_ROWS
                gx_ref[r0:r1, :] = dh2_ref[r0:r1, :] + work(x_ref[r0:r1, :], r0, others, r1)

    seven = last * BLK
    full = N_PROJ * BLK

    rev = lambda w, j=0: pl.BlockSpec((tm, w), lambda i: (nt - 1 - i, j))
    tok = pl.BlockSpec((tm, D_MODEL), lambda i: (_token_tile(nt - 1 - i), 0))
    const = lambda r, c: pl.BlockSpec((r, c), lambda i: (0, 0))
    return pl.pallas_call(
        body,
        name="b2_dconv_dh",
        grid=(nt,),
        in_specs=[
            rev(BLK),
            rev(D_CONV),
            rev(D_RET),
            rev(seven),
            rev(D_CONV),
            pl.BlockSpec((tm // CHUNK, N_HEADS, HEAD_DIM, HEAD_DIM), lambda i: (nt - 1 - i, 0, 0, 0)),
            _VMEM,
            _VMEM,
            tok,
            _VMEM,
            rev(D_MODEL),
        ]
        + [_VMEM] * 10,
        out_specs=(rev(full), tok, const(N_META, D_MODEL), const(8, D_MODEL), const(8, D_CONV)),
        out_shape=(
            jax.ShapeDtypeStruct((n_rows, full), BF16),
            jax.ShapeDtypeStruct(x2.shape, F32),
            jax.ShapeDtypeStruct((N_META, D_MODEL), F32),
            jax.ShapeDtypeStruct((8, D_MODEL), F32),
            jax.ShapeDtypeStruct((8, D_CONV), F32),
        ),
        scratch_shapes=[pltpu.VMEM((8, D_CONV), F32), pltpu.VMEM((N_HEADS, HEAD_DIM, HEAD_DIM), F32)],
        compiler_params=_params(),
    )(drg, dco, do, proj, conv, states, w3, cw8, x2, meta_chunk, dh2, g1, *tables, zeta0, *rot)


def _gw_in_scatter(hnT_tok, hnT_meta, dproj, gw_out_parts, dmeta_rows, acc_conv, acc_b2, acc_ret, acc_f2, me_arr):
    n_rows = dproj.shape[0]
    last = N_DEV - 1
    cols = D_CONV // N_DEV
    by_dest = (True, True, False)
    small_blocks = (gw_out_parts.shape[1:], (SC_CONV0 + 8, CHUNK), (GA_LOSS + 8, CHUNK))

    def body(me_ref, a_ref, am_ref, b_ref, go_ref, dm_ref, ac_ref, ab2_ref, aret_ref, af2_ref,
             land_in, land_go, land_sc, land_ga, d2d_buf, d2d_land, ici_buf, sc_ref, ga_ref,
             d2d_send, d2d_recv, ici_send, ici_recv, send_sems, recv_sems, local_sems):
        del me_ref
        s = pl.program_id(0)
        t = last - s
        q = t >> 1
        x, y, c = lax.axis_index("x"), lax.axis_index("y"), lax.axis_index("c")
        me = 4 * x + 2 * y + c
        chip = 2 * x + y
        srcs, lands = (go_ref, sc_ref, ga_ref), (land_go, land_sc, land_ga)

        def peer_at(k):
            return (1 - x if k & 4 else x, 1 - y if k & 2 else y, 1 - c if k & 1 else c)

        def small_copy(a, k):
            px, py, pc = peer_at(k)
            return pltpu.make_async_remote_copy(
                src_ref=srcs[a].at[4 * px + 2 * py + pc] if by_dest[a] else srcs[a],
                dst_ref=lands[a].at[me],
                send_sem=send_sems.at[a * last + k - 1],
                recv_sem=recv_sems.at[a * last + k - 1],
                device_id=(px, py, pc),
                device_id_type=MESH,
            )

        def small_local(a):
            return pltpu.make_async_copy(srcs[a].at[me] if by_dest[a] else srcs[a], lands[a].at[me], local_sems.at[a + 1])

        def d2d_copy(j):
            return pltpu.make_async_remote_copy(
                src_ref=d2d_buf.at[j],
                dst_ref=d2d_land.at[j],
                send_sem=d2d_send.at[j],
                recv_sem=d2d_recv.at[j],
                device_id=(x, y, 1 - c),
                device_id_type=MESH,
            )

        def ici_copy(j, to):
            return pltpu.make_async_remote_copy(
                src_ref=ici_buf.at[j],
                dst_ref=land_in.at[chip],
                send_sem=ici_send.at[j],
                recv_sem=ici_recv.at[j],
                device_id=to,
                device_id_type=MESH,
            )

        def own_copy():
            return pltpu.make_async_copy(ici_buf.at[0], land_in.at[chip], local_sems.at[0])

        @pl.when(s == 0)
        def _():
            sc_ref[...] = jnp.zeros_like(sc_ref)
            for d in range(N_DEV):
                sc_ref[d, SC_META0 : SC_META0 + N_META, :] = dm_ref[:, d * CHUNK : (d + 1) * CHUNK]
                lane0 = (d * cols) % CHUNK
                rows8 = ac_ref[:, d * cols - lane0 : d * cols - lane0 + CHUNK]
                rows8 = pltpu.roll(rows8, CHUNK - lane0, 1) if lane0 else rows8
                sc_ref[d, SC_CONV0 : SC_CONV0 + 3, 0:cols] = rows8[0:3, 0:cols]
            ga_ref[...] = jnp.zeros_like(ga_ref)
            for row0, acc, n in ((GA_N1, ab2_ref, 8), (GA_RG, aret_ref, 4), (GA_FG, af2_ref, 8)):
                for j in range(n):
                    ga_ref[row0 + j : row0 + j + 1, :] = acc[0:1, j * CHUNK : (j + 1) * CHUNK]
            ga_ref[GA_LOSS : GA_LOSS + 1, :] = af2_ref[1:2, 0:CHUNK]
            for a in range(3):
                small_local(a).start()
            for k in range(last, 0, -1):
                for a in range(3):
                    small_copy(a, k).start()

        blk = jnp.dot(a_ref[...], b_ref[ROW_TILE:, :], preferred_element_type=F32)
        blk += jnp.dot(am_ref[...], b_ref[LIVE0:ROW_TILE, :], preferred_element_type=F32)

        @pl.when((t & 1) == 1)
        def _():
            d2d_buf[q] = blk.astype(BF16)
            d2d_copy(q).start()

        @pl.when((t & 1) == 0)
        def _():
            d2d_copy(q).wait_recv()
            ici_buf[q] = (blk + d2d_land[q].astype(F32)).astype(BF16)

            @pl.when(t != 0)
            def _():
                ici_copy(q, (jnp.bitwise_xor(x, (t >> 2) & 1), jnp.bitwise_xor(y, (t >> 1) & 1), c)).start()

            @pl.when(t == 0)
            def _():
                own_copy().start()
                for j in range(N_CHIPS):
                    d2d_copy(j).wait_send()
                for j in range(N_CHIPS - 1, 0, -1):
                    ici_copy(j, peer_at(2 * j)).wait()
                for k in range(last, 0, -1):
                    for a in range(3):
                        small_copy(a, k).wait()
                for a in range(3):
                    small_local(a).wait()
                own_copy().wait()

    grid_spec = pltpu.PrefetchScalarGridSpec(
        num_scalar_prefetch=1,
        grid=(N_DEV,),
        in_specs=[_VMEM, _VMEM, pl.BlockSpec((n_rows, BLK), lambda s, me: (0, jnp.bitwise_xor(me[0], last - s))), _HBM]
        + [_VMEM] * 5,
        out_specs=tuple([_HBM] * 4),
        scratch_shapes=[
            pltpu.VMEM((N_CHIPS, D_MODEL, BLK), BF16),
            pltpu.VMEM((N_CHIPS, D_MODEL, BLK), BF16),
            pltpu.VMEM((N_CHIPS, D_MODEL, BLK), BF16),
            pltpu.VMEM((N_DEV, *small_blocks[1]), F32),
            pltpu.VMEM(small_blocks[2], F32),
            pltpu.SemaphoreType.DMA((N_CHIPS,)),
            pltpu.SemaphoreType.DMA((N_CHIPS,)),
            pltpu.SemaphoreType.DMA((N_CHIPS,)),
            pltpu.SemaphoreType.DMA((N_CHIPS,)),
            pltpu.SemaphoreType.DMA((3 * last,)),
            pltpu.SemaphoreType.DMA((3 * last,)),
            pltpu.SemaphoreType.DMA((4,)),
        ],
    )
    return pl.pallas_call(
        body,
        name="gw_in_scatter",
        grid_spec=grid_spec,
        out_shape=(jax.ShapeDtypeStruct((N_CHIPS, D_MODEL, BLK), BF16),)
        + tuple(jax.ShapeDtypeStruct((N_DEV, *b), dt) for b, dt in zip(small_blocks, (gw_out_parts.dtype, F32, F32), strict=True)),
        compiler_params=_params(),
    )(me_arr, hnT_tok, hnT_meta, dproj, gw_out_parts, dmeta_rows, acc_conv, acc_b2, acc_ret, acc_f2)


def _adamw(w, g, m, v):
    m = ADAM_B1 * m + (1.0 - ADAM_B1) * g
    v = ADAM_B2 * v + (1.0 - ADAM_B2) * (g * g)
    m_hat = m / (1.0 - ADAM_B1**ADAM_STEP)
    v_hat = v / (1.0 - ADAM_B2**ADAM_STEP)
    delta = -ADAM_LR * (m_hat / (jnp.sqrt(v_hat) + ADAM_EPS) + ADAM_WD * w)
    return delta, m, v


def _small_leaves(meta, conv_w, n1, rg, fg):
    return meta, conv_w, n1.reshape(8, CHUNK), rg.reshape(4, CHUNK), fg.reshape(8, CHUNK)


def _from_small_leaves(meta, conv_w, n1, rg, fg):
    return meta, conv_w, n1.reshape(D_MODEL), rg.reshape(D_RET), fg.reshape(D_MODEL)


ADAM_GRID = 4


def _adamw_all(land_in, land_out, land_sc, land_ga, big, small):
    n_leaf = 5
    lands = (land_in, land_out)
    n_big = len(big)

    def body(*refs):
        big_in, sc_ref, ga_ref = refs[: 4 * n_big], refs[4 * n_big], refs[4 * n_big + 1]
        small_in = refs[4 * n_big + 2 : 4 * n_big + 2 + 3 * n_leaf]
        big_out = refs[4 * n_big + 2 + 3 * n_leaf : 8 * n_big + 2 + 3 * n_leaf]
        small_out = refs[8 * n_big + 2 + 3 * n_leaf :]
        for b in range(n_big):
            p_ref, w_ref, m_ref, v_ref = big_in[4 * b : 4 * b + 4]
            g_ref, d_ref, nm_ref, nv_ref = big_out[4 * b : 4 * b + 4]
            g = p_ref[0].astype(F32)
            for s in range(1, p_ref.shape[0]):
                g = g + p_ref[s].astype(F32)
            g_ref[...] = g
            d_ref[...], nm_ref[...], nv_ref[...] = _adamw(w_ref[...], g, m_ref[...], v_ref[...])

        @pl.when(pl.program_id(0) == 0)
        def _():
            sc, ga = sc_ref[0], ga_ref[0]
            for s in range(1, N_DEV):
                sc = sc + sc_ref[s]
                ga = ga + ga_ref[s]
            grads = (
                sc[SC_META0 : SC_META0 + N_META],
                sc[SC_CONV0 : SC_CONV0 + 3, 0 : D_CONV // N_DEV],
                ga[GA_N1 : GA_N1 + 8],
                ga[GA_RG : GA_RG + 4],
                ga[GA_FG : GA_FG + 8],
            )
            for leaf, g in enumerate(grads):
                d, nm, nv = _adamw(
                    small_in[leaf][...], g, small_in[n_leaf + leaf][...], small_in[2 * n_leaf + leaf][...]
                )
                small_out[leaf][...] = g
                small_out[n_leaf + leaf][...] = d
                small_out[2 * n_leaf + leaf][...] = nm
                small_out[3 * n_leaf + leaf][...] = nv
            small_out[4 * n_leaf][...] = ga[GA_LOSS : GA_LOSS + 1]

    in_specs, out_specs, out_shape, operands = [], [], [], []
    for land, (w, m, v) in zip(lands, big, strict=True):
        n_r, n_c = w.shape
        tr = n_r // ADAM_GRID
        blk = pl.BlockSpec((tr, n_c), lambda i: (i, 0))
        in_specs += [pl.BlockSpec((land.shape[0], tr, n_c), lambda i: (0, i, 0)), blk, blk, blk]
        out_specs += [blk] * 4
        out_shape += [jax.ShapeDtypeStruct((n_r, n_c), F32)] * 4
        operands += [land, w, m, v]
    small_shapes = [jax.ShapeDtypeStruct(a.shape, F32) for a in small[0]] * 4 + [jax.ShapeDtypeStruct((1, CHUNK), F32)]
    out = pl.pallas_call(
        body,
        name="adamw_all",
        grid=(ADAM_GRID,),
        in_specs=in_specs + [_VMEM] * (2 + 3 * n_leaf),
        out_specs=tuple(out_specs + [_VMEM] * len(small_shapes)),
        out_shape=tuple(out_shape + small_shapes),
        compiler_params=_params(),
    )(*operands, land_sc, land_ga, *small[0], *small[1], *small[2])
    bigs = tuple(out[4 * b : 4 * b + 4] for b in range(n_big))
    rest = out[4 * n_big :]
    return bigs, tuple(rest[k * n_leaf : (k + 1) * n_leaf] for k in range(4)), rest[4 * n_leaf]


def kernel(x, meta, norm1_g, w_in, conv_w, ret_norm_g, w_out, final_g, loss_target, m_meta, m_norm1_g, m_w_in, m_conv_w, m_ret_norm_g, m_w_out, m_final_g, v_meta, v_norm1_g, v_w_in, v_conv_w, v_ret_norm_g, v_w_out, v_final_g):
    seq = x.shape[1]
    assert x.shape == (1, seq, D_MODEL) and seq % ROW_TILE == 0
    n_tiles = seq // ROW_TILE + 1
    x2, t2 = x[0], loss_target[0]

    small_w = _small_leaves(meta, conv_w, norm1_g, ret_norm_g, final_g)
    g1 = norm1_g.reshape(1, D_MODEL)
    hn_tok, hnT_tok, w3, meta_chunk, cw8 = _gather_weights_and_norm(x2, g1, w_in, meta, conv_w)

    rot = _rotary_tables(n_tiles, ROW_TILE)
    dec, _, zeta, xi = _decay_tables()
    gr = ret_norm_g.reshape(1, D_RET)
    gf = final_g.reshape(1, D_MODEL)

    hnT_meta, proj, conv_out, conv, o, states, wo3 = _f1(hn_tok, meta_chunk, g1, w3, cw8, rot, dec, zeta, xi, w_out)
    wout = wo3.reshape(D_MODEL, D_MODEL)
    dh2, dco, drg, do, gw_out, acc_f2, acc_ret = _f2(proj, o, conv_out, x2, t2, wout, gr, gf)
    dproj, grad_x2, dmeta_rows, acc_b2, acc_conv = _b2(
        drg, dco, do, proj, conv, states, w3, cw8, x2, meta_chunk, dh2, g1, _decay_tables(B1_CHUNK), zeta, rot
    )
    gw_out_parts = gw_out.reshape(N_DEV, D_MODEL // N_DEV, D_MODEL)

    me_arr = (4 * lax.axis_index("x") + 2 * lax.axis_index("y") + lax.axis_index("c")).astype(jnp.int32).reshape(1)
    land_in, land_out, land_sc, land_ga = _gw_in_scatter(
        hnT_tok, hnT_meta, dproj, gw_out_parts, dmeta_rows, acc_conv, acc_b2, acc_ret, acc_f2, me_arr
    )

    (w_in_out, w_out_out), small_out, loss_row = _adamw_all(
        land_in,
        land_out,
        land_sc,
        land_ga,
        ((w_in, m_w_in, v_w_in), (w_out, m_w_out, v_w_out)),
        (
            small_w,
            _small_leaves(m_meta, m_conv_w, m_norm1_g, m_ret_norm_g, m_final_g),
            _small_leaves(v_meta, v_conv_w, v_norm1_g, v_ret_norm_g, v_final_g),
        ),
    )
    g_w_in, d_w_in, nm_w_in, nv_w_in = w_in_out
    g_w_out, d_w_out, nm_w_out, nv_w_out = w_out_out
    loss = loss_row[0, 0]
    grad_x = grad_x2.reshape(1, seq, D_MODEL)

    def leaves(w_in_leaf, w_out_leaf, small_leaves):
        meta_leaf, conv_leaf, n1_leaf, rg_leaf, fg_leaf = _from_small_leaves(*small_leaves)
        return (meta_leaf, n1_leaf, w_in_leaf, conv_leaf, rg_leaf, w_out_leaf, fg_leaf)

    return (
        loss,
        grad_x,
        *leaves(g_w_in, g_w_out, small_out[0]),
        *leaves(d_w_in, d_w_out, small_out[1]),
        *leaves(nm_w_in, nm_w_out, small_out[2]),
        *leaves(nv_w_in, nv_w_out, small_out[3]),
    )
```
